```python
import math
import jax, jax.numpy as jnp
from jax import lax
import numpy as np

D_MODEL = 1024
BATCH = 8
SEQ = 8192
DEPTH = 1

HG_WIDTH = 512
HG_HEAD_DIM = 128
HG_HEADS = HG_WIDTH // HG_HEAD_DIM
HG_CHUNK = 64
S5_WIDTH = 512
S5_GROUP = 16
S5_GROUPS = S5_WIDTH // S5_GROUP
S5_STATE = 64
DT_MIN = 1e-3
DT_MAX = 1e-1
D_FF = 2816
CONV_WIDTH = 3
PLE_DIM = 256
N_BRANCH = 2
N_IN = 4 * HG_WIDTH + S5_WIDTH + N_BRANCH * D_MODEL
SPLITS = (HG_WIDTH, 2 * HG_WIDTH, 3 * HG_WIDTH, 4 * HG_WIDTH,
          4 * HG_WIDTH + S5_WIDTH, 4 * HG_WIDTH + S5_WIDTH + D_MODEL)
EPS = 1e-6

kernel_name = "hybrid_hgrn2_s5_gated_block"


def _rmsnorm(x, g):
    xf = x.astype(jnp.float32)
    y = xf * lax.rsqrt(jnp.mean(xf * xf, axis=-1, keepdims=True) + EPS) * g.astype(jnp.float32)
    return y.astype(x.dtype)


def _hgrn2(q_raw, f_raw, i_raw, og_raw, lb, norm_g):
    bsz, seqlen, _ = q_raw.shape
    nc = seqlen // HG_CHUNK

    def heads(t):
        t = t.astype(jnp.float32).reshape(bsz, nc, HG_CHUNK, HG_HEADS, HG_HEAD_DIM)
        return t.transpose(0, 3, 1, 2, 4)

    lb = lb.astype(jnp.float32).reshape(HG_HEADS, 1, 1, HG_HEAD_DIM)
    f = lb + (1.0 - lb) * jax.nn.sigmoid(heads(f_raw))
    k = 1.0 - f
    q = jax.nn.silu(heads(q_raw))
    v = heads(i_raw)
    b = jnp.cumsum(jnp.log(f), axis=-2)
    b_ref = b[..., HG_CHUNK // 2:HG_CHUNK // 2 + 1, :]
    b_last = b[..., -1:, :]
    scores = jnp.einsum('bhncd,bhnsd->bhncs', q * jnp.exp(b - b_ref), k * jnp.exp(b_ref - b))
    causal = jnp.tril(jnp.ones((HG_CHUNK, HG_CHUNK), dtype=bool))
    scores = jnp.where(causal, scores, 0.0)
    o_intra = jnp.einsum('bhncs,bhnse->bhnce', scores, v)
    u_chunk = jnp.einsum('bhncd,bhnce->bhnde', k * jnp.exp(b_last - b), v)
    decay = jnp.exp(b_last[..., 0, :])

    def step(S, inp):
        dec, u = inp
        return dec[..., None] * S + u, S

    S0 = jnp.zeros((bsz, HG_HEADS, HG_HEAD_DIM, HG_HEAD_DIM), jnp.float32)
    _, S_prev = lax.scan(step, S0, (jnp.moveaxis(decay, 2, 0), jnp.moveaxis(u_chunk, 2, 0)))
    S_prev = jnp.moveaxis(S_prev, 0, 2)
    o_inter = jnp.einsum('bhncd,bhnde->bhnce', q * jnp.exp(b), S_prev)
    o = o_intra + o_inter
    o = o * lax.rsqrt(jnp.mean(o * o, axis=-1, keepdims=True) + EPS) * norm_g.astype(jnp.float32)
    o = o.transpose(0, 2, 3, 1, 4).reshape(bsz, seqlen, HG_WIDTH)
    return (o * jax.nn.silu(og_raw.astype(jnp.float32))).astype(q_raw.dtype)


def _complex_affine_combine(e1, e2):
    a1r, a1i, b1r, b1i = e1
    a2r, a2i, b2r, b2i = e2
    return (a2r * a1r - a2i * a1i,
            a2r * a1i + a2i * a1r,
            a2r * b1r - a2i * b1i + b2r,
            a2r * b1i + a2i * b1r + b2i)


def _s5(u_raw, lam_re, lam_im, log_dt, b_re, b_im, c_re, c_im, d_skip, glu_w, glu_b):
    bsz, seqlen, _ = u_raw.shape
    u = u_raw.astype(jnp.float32).reshape(bsz, seqlen, S5_GROUPS, S5_GROUP)
    lr = lam_re.astype(jnp.float32)
    li = lam_im.astype(jnp.float32)
    dt = jnp.exp(log_dt.astype(jnp.float32))[:, None]
    mag = jnp.exp(lr * dt)
    a_re = mag * jnp.cos(li * dt)
    a_im = mag * jnp.sin(li * dt)
    den = lr * lr + li * li
    coef_re = ((a_re - 1.0) * lr + a_im * li) / den
    coef_im = (a_im * lr - (a_re - 1.0) * li) / den
    br = b_re.astype(jnp.float32)
    bi = b_im.astype(jnp.float32)
    bbar_re = coef_re[..., None] * br - coef_im[..., None] * bi
    bbar_im = coef_re[..., None] * bi + coef_im[..., None] * br
    bu_re = jnp.einsum('gnp,blgp->blgn', bbar_re, u)
    bu_im = jnp.einsum('gnp,blgp->blgn', bbar_im, u)
    shape = bu_re.shape
    elems = (jnp.broadcast_to(a_re, shape), jnp.broadcast_to(a_im, shape), bu_re, bu_im)
    _, _, x_re, x_im = lax.associative_scan(_complex_affine_combine, elems, axis=1)
    y = (jnp.einsum('gpn,blgn->blgp', c_re.astype(jnp.float32), x_re)
         - jnp.einsum('gpn,blgn->blgp', c_im.astype(jnp.float32), x_im))
    y = y + d_skip.astype(jnp.float32).reshape(S5_GROUPS, S5_GROUP) * u
    y = y.reshape(bsz, seqlen, S5_WIDTH).astype(u_raw.dtype)
    g = jax.nn.gelu(y)
    return g * jax.nn.sigmoid(g @ glu_w + glu_b)


def _conv_gated_ffn(h, w_up, conv_w, conv_b, w_down):
    a = h @ w_up
    seqlen = a.shape[1]
    ap = jnp.pad(a, ((0, 0), (CONV_WIDTH - 1, 0), (0, 0)))
    c = conv_b + conv_w[0] * ap[:, 0:seqlen]
    for k in range(1, CONV_WIDTH):
        c = c + conv_w[k] * ap[:, k:k + seqlen]
    gate, val = jnp.split(c, 2, axis=-1)
    return (jax.nn.gelu(gate) * val) @ w_down


def _fwd_setup_inputs(seed: int = 0) -> dict:
    key = jax.random.key(seed)
    ks = jax.random.split(key, 32)
    f32 = jnp.float32

    def nrm(k, shape, scale):
        return jax.random.normal(k, shape, f32) * scale

    def gain(k, shape):
        return 1.0 + 0.01 * jax.random.normal(k, shape, f32)

    n_idx = jnp.arange(S5_STATE, dtype=f32)
    lam_re = -0.5 + 0.01 * jax.random.normal(ks[4], (DEPTH, S5_GROUPS, S5_STATE), f32)
    lam_im = math.pi * n_idx + 0.01 * jax.random.normal(ks[5], (DEPTH, S5_GROUPS, S5_STATE), f32)
    log_dt = jax.random.uniform(ks[6], (DEPTH, S5_GROUPS), f32,
                                minval=math.log(DT_MIN), maxval=math.log(DT_MAX))
    return {
        "x": nrm(ks[0], (BATCH, SEQ, D_MODEL), 1.0),
        "p": nrm(ks[1], (DEPTH, BATCH, SEQ, PLE_DIM), 1.0),
        "norm_mix_g": gain(ks[2], (DEPTH, D_MODEL)),
        "w_in": nrm(ks[3], (DEPTH, D_MODEL, N_IN), D_MODEL ** -0.5),
        "hg_lb_logits": nrm(ks[7], (DEPTH + 1, HG_WIDTH), 0.1),
        "hg_norm_g": gain(ks[8], (DEPTH, HG_HEAD_DIM)),
        "s5_lambda_re": lam_re,
        "s5_lambda_im": lam_im,
        "s5_log_dt": log_dt,
        "s5_b_re": nrm(ks[9], (DEPTH, S5_GROUPS, S5_STATE, S5_GROUP), (2 * S5_GROUP) ** -0.5),
        "s5_b_im": nrm(ks[10], (DEPTH, S5_GROUPS, S5_STATE, S5_GROUP), (2 * S5_GROUP) ** -0.5),
        "s5_c_re": nrm(ks[11], (DEPTH, S5_GROUPS, S5_GROUP, S5_STATE), S5_STATE ** -0.5),
        "s5_c_im": nrm(ks[12], (DEPTH, S5_GROUPS, S5_GROUP, S5_STATE), S5_STATE ** -0.5),
        "s5_d": nrm(ks[13], (DEPTH, S5_WIDTH), 1.0),
        "s5_glu_w": nrm(ks[14], (DEPTH, S5_WIDTH, S5_WIDTH), S5_WIDTH ** -0.5),
        "s5_glu_b": nrm(ks[15], (DEPTH, S5_WIDTH), 0.01),
        "w_branch_hg": nrm(ks[16], (DEPTH, HG_WIDTH, D_MODEL), HG_WIDTH ** -0.5),
        "w_branch_s5": nrm(ks[17], (DEPTH, S5_WIDTH, D_MODEL), S5_WIDTH ** -0.5),
        "w_out": nrm(ks[18], (DEPTH, D_MODEL, D_MODEL), D_MODEL ** -0.5),
        "norm_ffn_g": gain(ks[19], (DEPTH, D_MODEL)),
        "w_up": nrm(ks[20], (DEPTH, D_MODEL, 2 * D_FF), D_MODEL ** -0.5),
        "conv_w": nrm(ks[21], (DEPTH, CONV_WIDTH, 2 * D_FF), CONV_WIDTH ** -0.5),
        "conv_b": nrm(ks[22], (DEPTH, 2 * D_FF), 0.01),
        "w_down": nrm(ks[23], (DEPTH, D_FF, D_MODEL), D_FF ** -0.5),
        "norm_ple_g": gain(ks[24], (DEPTH, D_MODEL)),
        "w_ple_gate": nrm(ks[25], (DEPTH, D_MODEL, D_MODEL), D_MODEL ** -0.5),
        "w_ple_proj": nrm(ks[26], (DEPTH, PLE_DIM, D_MODEL), PLE_DIM ** -0.5),
        "norm_final_g": gain(ks[27], (D_MODEL,)),
    }


def _fwd_reference(x, p, norm_mix_g, w_in, hg_lb_logits, hg_norm_g, s5_lambda_re, s5_lambda_im,
              s5_log_dt, s5_b_re, s5_b_im, s5_c_re, s5_c_im, s5_d, s5_glu_w, s5_glu_b,
              w_branch_hg, w_branch_s5, w_out, norm_ffn_g, w_up, conv_w, conv_b, w_down,
              norm_ple_g, w_ple_gate, w_ple_proj, norm_final_g):
    lbs = jnp.cumsum(jax.nn.softmax(hg_lb_logits.astype(jnp.float32), axis=0), axis=0)
    for i in range(DEPTH):
        h = _rmsnorm(x, norm_mix_g[i])
        proj = h @ w_in[i]
        q_raw, f_raw, i_raw, og_raw, u_raw, gate_hg, gate_s5 = jnp.split(proj, SPLITS, axis=-1)
        y_hg = _hgrn2(q_raw, f_raw, i_raw, og_raw, lbs[i], hg_norm_g[i]) @ w_branch_hg[i]
        y_s5 = _s5(u_raw, s5_lambda_re[i], s5_lambda_im[i], s5_log_dt[i], s5_b_re[i], s5_b_im[i],
                   s5_c_re[i], s5_c_im[i], s5_d[i], s5_glu_w[i], s5_glu_b[i]) @ w_branch_s5[i]
        merged = jax.nn.sigmoid(gate_hg) * y_hg + jax.nn.sigmoid(gate_s5) * y_s5
        x = x + merged @ w_out[i]
        x = x + _conv_gated_ffn(_rmsnorm(x, norm_ffn_g[i]), w_up[i], conv_w[i], conv_b[i], w_down[i])
        ple_gate = jax.nn.sigmoid(_rmsnorm(x, norm_ple_g[i]) @ w_ple_gate[i])
        x = x + ple_gate * (p[i] @ w_ple_proj[i])
    return _rmsnorm(x, norm_final_g)


import jax as _jax
import jax.numpy as _jnp

TWIN_FORMAT = 'train_step'
FWD_PARAMS = ['x', 'p', 'norm_mix_g', 'w_in', 'hg_lb_logits', 'hg_norm_g', 's5_lambda_re', 's5_lambda_im', 's5_log_dt', 's5_b_re', 's5_b_im', 's5_c_re', 's5_c_im', 's5_d', 's5_glu_w', 's5_glu_b', 'w_branch_hg', 'w_branch_s5', 'w_out', 'norm_ffn_g', 'w_up', 'conv_w', 'conv_b', 'w_down', 'norm_ple_g', 'w_ple_gate', 'w_ple_proj', 'norm_final_g']
TWIN_WEIGHTS = ['norm_mix_g', 'w_in', 'hg_lb_logits', 'hg_norm_g', 's5_lambda_re', 's5_lambda_im', 's5_log_dt', 's5_b_re', 's5_b_im', 's5_c_re', 's5_c_im', 's5_d', 's5_glu_w', 's5_glu_b', 'w_branch_hg', 'w_branch_s5', 'w_out', 'norm_ffn_g', 'w_up', 'conv_w', 'conv_b', 'w_down', 'norm_ple_g', 'w_ple_gate', 'w_ple_proj', 'norm_final_g']
TWIN_DIFF_INPUT = 'x'
TWIN_INPUTS = ['x', 'p', 'norm_mix_g', 'w_in', 'hg_lb_logits', 'hg_norm_g', 's5_lambda_re', 's5_lambda_im', 's5_log_dt', 's5_b_re', 's5_b_im', 's5_c_re', 's5_c_im', 's5_d', 's5_glu_w', 's5_glu_b', 'w_branch_hg', 'w_branch_s5', 'w_out', 'norm_ffn_g', 'w_up', 'conv_w', 'conv_b', 'w_down', 'norm_ple_g', 'w_ple_gate', 'w_ple_proj', 'norm_final_g', 'loss_target', 'm_norm_mix_g', 'm_w_in', 'm_hg_lb_logits', 'm_hg_norm_g', 'm_s5_lambda_re', 'm_s5_lambda_im', 'm_s5_log_dt', 'm_s5_b_re', 'm_s5_b_im', 'm_s5_c_re', 'm_s5_c_im', 'm_s5_d', 'm_s5_glu_w', 'm_s5_glu_b', 'm_w_branch_hg', 'm_w_branch_s5', 'm_w_out', 'm_norm_ffn_g', 'm_w_up', 'm_conv_w', 'm_conv_b', 'm_w_down', 'm_norm_ple_g', 'm_w_ple_gate', 'm_w_ple_proj', 'm_norm_final_g', 'v_norm_mix_g', 'v_w_in', 'v_hg_lb_logits', 'v_hg_norm_g', 'v_s5_lambda_re', 'v_s5_lambda_im', 'v_s5_log_dt', 'v_s5_b_re', 'v_s5_b_im', 'v_s5_c_re', 'v_s5_c_im', 'v_s5_d', 'v_s5_glu_w', 'v_s5_glu_b', 'v_w_branch_hg', 'v_w_branch_s5', 'v_w_out', 'v_norm_ffn_g', 'v_w_up', 'v_conv_w', 'v_conv_b', 'v_w_down', 'v_norm_ple_g', 'v_w_ple_gate', 'v_w_ple_proj', 'v_norm_final_g']
TWIN_OUTPUTS = ['loss', 'grad_x', 'grad_norm_mix_g', 'grad_w_in', 'grad_hg_lb_logits', 'grad_hg_norm_g', 'grad_s5_lambda_re', 'grad_s5_lambda_im', 'grad_s5_log_dt', 'grad_s5_b_re', 'grad_s5_b_im', 'grad_s5_c_re', 'grad_s5_c_im', 'grad_s5_d', 'grad_s5_glu_w', 'grad_s5_glu_b', 'grad_w_branch_hg', 'grad_w_branch_s5', 'grad_w_out', 'grad_norm_ffn_g', 'grad_w_up', 'grad_conv_w', 'grad_conv_b', 'grad_w_down', 'grad_norm_ple_g', 'grad_w_ple_gate', 'grad_w_ple_proj', 'grad_norm_final_g', 'delta_norm_mix_g', 'delta_w_in', 'delta_hg_lb_logits', 'delta_hg_norm_g', 'delta_s5_lambda_re', 'delta_s5_lambda_im', 'delta_s5_log_dt', 'delta_s5_b_re', 'delta_s5_b_im', 'delta_s5_c_re', 'delta_s5_c_im', 'delta_s5_d', 'delta_s5_glu_w', 'delta_s5_glu_b', 'delta_w_branch_hg', 'delta_w_branch_s5', 'delta_w_out', 'delta_norm_ffn_g', 'delta_w_up', 'delta_conv_w', 'delta_conv_b', 'delta_w_down', 'delta_norm_ple_g', 'delta_w_ple_gate', 'delta_w_ple_proj', 'delta_norm_final_g', 'new_m_norm_mix_g', 'new_m_w_in', 'new_m_hg_lb_logits', 'new_m_hg_norm_g', 'new_m_s5_lambda_re', 'new_m_s5_lambda_im', 'new_m_s5_log_dt', 'new_m_s5_b_re', 'new_m_s5_b_im', 'new_m_s5_c_re', 'new_m_s5_c_im', 'new_m_s5_d', 'new_m_s5_glu_w', 'new_m_s5_glu_b', 'new_m_w_branch_hg', 'new_m_w_branch_s5', 'new_m_w_out', 'new_m_norm_ffn_g', 'new_m_w_up', 'new_m_conv_w', 'new_m_conv_b', 'new_m_w_down', 'new_m_norm_ple_g', 'new_m_w_ple_gate', 'new_m_w_ple_proj', 'new_m_norm_final_g', 'new_v_norm_mix_g', 'new_v_w_in', 'new_v_hg_lb_logits', 'new_v_hg_norm_g', 'new_v_s5_lambda_re', 'new_v_s5_lambda_im', 'new_v_s5_log_dt', 'new_v_s5_b_re', 'new_v_s5_b_im', 'new_v_s5_c_re', 'new_v_s5_c_im', 'new_v_s5_d', 'new_v_s5_glu_w', 'new_v_s5_glu_b', 'new_v_w_branch_hg', 'new_v_w_branch_s5', 'new_v_w_out', 'new_v_norm_ffn_g', 'new_v_w_up', 'new_v_conv_w', 'new_v_conv_b', 'new_v_w_down', 'new_v_norm_ple_g', 'new_v_w_ple_gate', 'new_v_w_ple_proj', 'new_v_norm_final_g']
TWIN_LEAF_KINDS = {'loss': 'loss', 'grad_x': 'grad_x', 'grad_norm_mix_g': 'grad_w', 'grad_w_in': 'grad_w', 'grad_hg_lb_logits': 'grad_w', 'grad_hg_norm_g': 'grad_w', 'grad_s5_lambda_re': 'grad_w', 'grad_s5_lambda_im': 'grad_w', 'grad_s5_log_dt': 'grad_w', 'grad_s5_b_re': 'grad_w', 'grad_s5_b_im': 'grad_w', 'grad_s5_c_re': 'grad_w', 'grad_s5_c_im': 'grad_w', 'grad_s5_d': 'grad_w', 'grad_s5_glu_w': 'grad_w', 'grad_s5_glu_b': 'grad_w', 'grad_w_branch_hg': 'grad_w', 'grad_w_branch_s5': 'grad_w', 'grad_w_out': 'grad_w', 'grad_norm_ffn_g': 'grad_w', 'grad_w_up': 'grad_w', 'grad_conv_w': 'grad_w', 'grad_conv_b': 'grad_w', 'grad_w_down': 'grad_w', 'grad_norm_ple_g': 'grad_w', 'grad_w_ple_gate': 'grad_w', 'grad_w_ple_proj': 'grad_w', 'grad_norm_final_g': 'grad_w', 'delta_norm_mix_g': 'delta_w', 'delta_w_in': 'delta_w', 'delta_hg_lb_logits': 'delta_w', 'delta_hg_norm_g': 'delta_w', 'delta_s5_lambda_re': 'delta_w', 'delta_s5_lambda_im': 'delta_w', 'delta_s5_log_dt': 'delta_w', 'delta_s5_b_re': 'delta_w', 'delta_s5_b_im': 'delta_w', 'delta_s5_c_re': 'delta_w', 'delta_s5_c_im': 'delta_w', 'delta_s5_d': 'delta_w', 'delta_s5_glu_w': 'delta_w', 'delta_s5_glu_b': 'delta_w', 'delta_w_branch_hg': 'delta_w', 'delta_w_branch_s5': 'delta_w', 'delta_w_out': 'delta_w', 'delta_norm_ffn_g': 'delta_w', 'delta_w_up': 'delta_w', 'delta_conv_w': 'delta_w', 'delta_conv_b': 'delta_w', 'delta_w_down': 'delta_w', 'delta_norm_ple_g': 'delta_w', 'delta_w_ple_gate': 'delta_w', 'delta_w_ple_proj': 'delta_w', 'delta_norm_final_g': 'delta_w', 'new_m_norm_mix_g': 'new_m', 'new_m_w_in': 'new_m', 'new_m_hg_lb_logits': 'new_m', 'new_m_hg_norm_g': 'new_m', 'new_m_s5_lambda_re': 'new_m', 'new_m_s5_lambda_im': 'new_m', 'new_m_s5_log_dt': 'new_m', 'new_m_s5_b_re': 'new_m', 'new_m_s5_b_im': 'new_m', 'new_m_s5_c_re': 'new_m', 'new_m_s5_c_im': 'new_m', 'new_m_s5_d': 'new_m', 'new_m_s5_glu_w': 'new_m', 'new_m_s5_glu_b': 'new_m', 'new_m_w_branch_hg': 'new_m', 'new_m_w_branch_s5': 'new_m', 'new_m_w_out': 'new_m', 'new_m_norm_ffn_g': 'new_m', 'new_m_w_up': 'new_m', 'new_m_conv_w': 'new_m', 'new_m_conv_b': 'new_m', 'new_m_w_down': 'new_m', 'new_m_norm_ple_g': 'new_m', 'new_m_w_ple_gate': 'new_m', 'new_m_w_ple_proj': 'new_m', 'new_m_norm_final_g': 'new_m', 'new_v_norm_mix_g': 'new_v', 'new_v_w_in': 'new_v', 'new_v_hg_lb_logits': 'new_v', 'new_v_hg_norm_g': 'new_v', 'new_v_s5_lambda_re': 'new_v', 'new_v_s5_lambda_im': 'new_v', 'new_v_s5_log_dt': 'new_v', 'new_v_s5_b_re': 'new_v', 'new_v_s5_b_im': 'new_v', 'new_v_s5_c_re': 'new_v', 'new_v_s5_c_im': 'new_v', 'new_v_s5_d': 'new_v', 'new_v_s5_glu_w': 'new_v', 'new_v_s5_glu_b': 'new_v', 'new_v_w_branch_hg': 'new_v', 'new_v_w_branch_s5': 'new_v', 'new_v_w_out': 'new_v', 'new_v_norm_ffn_g': 'new_v', 'new_v_w_up': 'new_v', 'new_v_conv_w': 'new_v', 'new_v_conv_b': 'new_v', 'new_v_w_down': 'new_v', 'new_v_norm_ple_g': 'new_v', 'new_v_w_ple_gate': 'new_v', 'new_v_w_ple_proj': 'new_v', 'new_v_norm_final_g': 'new_v'}


def _forward(args):
    return _fwd_reference(*[args[k] for k in FWD_PARAMS])


def _output_shape():
    def fwd():
        inp = _fwd_setup_inputs(0)
        return _fwd_reference(*[inp[k] for k in FWD_PARAMS])
    out = _jax.eval_shape(fwd)
    return out.shape, out.dtype

N_MICROBATCH = 1
ADAM_LR = 0.001
ADAM_B1 = 0.9
ADAM_B2 = 0.999
ADAM_EPS = 1e-08
ADAM_WD = 0.01
ADAM_STEP = 10
PER_EXAMPLE_BATCH_AXIS = {'x': 0, 'p': 1, 'loss_target': 0}
SHARED_INPUTS = []
_WEIGHT_DTYPES = {'norm_mix_g': _jnp.float32, 'w_in': _jnp.float32, 'hg_lb_logits': _jnp.float32, 'hg_norm_g': _jnp.float32, 's5_lambda_re': _jnp.float32, 's5_lambda_im': _jnp.float32, 's5_log_dt': _jnp.float32, 's5_b_re': _jnp.float32, 's5_b_im': _jnp.float32, 's5_c_re': _jnp.float32, 's5_c_im': _jnp.float32, 's5_d': _jnp.float32, 's5_glu_w': _jnp.float32, 's5_glu_b': _jnp.float32, 'w_branch_hg': _jnp.float32, 'w_branch_s5': _jnp.float32, 'w_out': _jnp.float32, 'norm_ffn_g': _jnp.float32, 'w_up': _jnp.float32, 'conv_w': _jnp.float32, 'conv_b': _jnp.float32, 'w_down': _jnp.float32, 'norm_ple_g': _jnp.float32, 'w_ple_gate': _jnp.float32, 'w_ple_proj': _jnp.float32, 'norm_final_g': _jnp.float32}
MOMENT_SCALE = {'norm_mix_g': 1.329083e-01, 'w_in': 6.255624e-02, 'hg_lb_logits': 1.012062e-02, 'hg_norm_g': 2.743745e-01, 's5_lambda_re': 5.654470e-03, 's5_lambda_im': 5.226106e-03, 's5_log_dt': 7.438229e+00, 's5_b_re': 3.482943e-03, 's5_b_im': 3.346609e-03, 's5_c_re': 4.858140e-03, 's5_c_im': 4.693789e-03, 's5_d': 7.922851e-02, 's5_glu_w': 2.048563e-02, 's5_glu_b': 2.990448e-02, 'w_branch_hg': 8.058644e-02, 'w_branch_s5': 4.771505e-02, 'w_out': 9.357247e-02, 'norm_ffn_g': 1.859590e-01, 'w_up': 7.606797e-02, 'conv_w': 7.525707e-02, 'conv_b': 7.784338e-02, 'w_down': 1.247116e-01, 'norm_ple_g': 3.772100e-02, 'w_ple_gate': 3.866721e-02, 'w_ple_proj': 9.886973e-02, 'norm_final_g': 6.395765e+01}


def _to_microbatches(a, axis):
    t = _jnp.moveaxis(a, axis, 0)
    t = t.reshape((N_MICROBATCH, t.shape[0] // N_MICROBATCH) + t.shape[1:])
    return _jnp.moveaxis(t, 1, axis + 1)


def setup_inputs(seed: int = 0) -> dict:
    inp = _fwd_setup_inputs(seed)
    key = _jax.random.fold_in(_jax.random.key(seed), 7919)
    shape, _ = _output_shape()
    out = dict(inp)
    out["loss_target"] = _jax.random.normal(_jax.random.fold_in(key, 0), shape, _jnp.float32)
    for i, name in enumerate(TWIN_WEIGHTS):
        w = inp[name].astype(_jnp.float32)
        if MOMENT_SCALE is None:
            s = _jnp.sqrt(_jnp.mean(_jnp.square(w)) + 1e-30)
        else:
            s = MOMENT_SCALE[name]
        km, kv = _jax.random.split(_jax.random.fold_in(key, i + 1))
        out[name] = w
        out["m_" + name] = s * _jax.random.normal(km, w.shape, _jnp.float32)
        out["v_" + name] = (s * s) * _jax.random.uniform(kv, w.shape, _jnp.float32, 0.5, 1.5)
    if N_MICROBATCH > 1:
        for name, axis in PER_EXAMPLE_BATCH_AXIS.items():
            out[name] = _to_microbatches(out[name], axis)
    return {'x': out['x'], 'p': out['p'], 'norm_mix_g': out['norm_mix_g'], 'w_in': out['w_in'], 'hg_lb_logits': out['hg_lb_logits'], 'hg_norm_g': out['hg_norm_g'], 's5_lambda_re': out['s5_lambda_re'], 's5_lambda_im': out['s5_lambda_im'], 's5_log_dt': out['s5_log_dt'], 's5_b_re': out['s5_b_re'], 's5_b_im': out['s5_b_im'], 's5_c_re': out['s5_c_re'], 's5_c_im': out['s5_c_im'], 's5_d': out['s5_d'], 's5_glu_w': out['s5_glu_w'], 's5_glu_b': out['s5_glu_b'], 'w_branch_hg': out['w_branch_hg'], 'w_branch_s5': out['w_branch_s5'], 'w_out': out['w_out'], 'norm_ffn_g': out['norm_ffn_g'], 'w_up': out['w_up'], 'conv_w': out['conv_w'], 'conv_b': out['conv_b'], 'w_down': out['w_down'], 'norm_ple_g': out['norm_ple_g'], 'w_ple_gate': out['w_ple_gate'], 'w_ple_proj': out['w_ple_proj'], 'norm_final_g': out['norm_final_g'], 'loss_target': out['loss_target'], 'm_norm_mix_g': out['m_norm_mix_g'], 'm_w_in': out['m_w_in'], 'm_hg_lb_logits': out['m_hg_lb_logits'], 'm_hg_norm_g': out['m_hg_norm_g'], 'm_s5_lambda_re': out['m_s5_lambda_re'], 'm_s5_lambda_im': out['m_s5_lambda_im'], 'm_s5_log_dt': out['m_s5_log_dt'], 'm_s5_b_re': out['m_s5_b_re'], 'm_s5_b_im': out['m_s5_b_im'], 'm_s5_c_re': out['m_s5_c_re'], 'm_s5_c_im': out['m_s5_c_im'], 'm_s5_d': out['m_s5_d'], 'm_s5_glu_w': out['m_s5_glu_w'], 'm_s5_glu_b': out['m_s5_glu_b'], 'm_w_branch_hg': out['m_w_branch_hg'], 'm_w_branch_s5': out['m_w_branch_s5'], 'm_w_out': out['m_w_out'], 'm_norm_ffn_g': out['m_norm_ffn_g'], 'm_w_up': out['m_w_up'], 'm_conv_w': out['m_conv_w'], 'm_conv_b': out['m_conv_b'], 'm_w_down': out['m_w_down'], 'm_norm_ple_g': out['m_norm_ple_g'], 'm_w_ple_gate': out['m_w_ple_gate'], 'm_w_ple_proj': out['m_w_ple_proj'], 'm_norm_final_g': out['m_norm_final_g'], 'v_norm_mix_g': out['v_norm_mix_g'], 'v_w_in': out['v_w_in'], 'v_hg_lb_logits': out['v_hg_lb_logits'], 'v_hg_norm_g': out['v_hg_norm_g'], 'v_s5_lambda_re': out['v_s5_lambda_re'], 'v_s5_lambda_im': out['v_s5_lambda_im'], 'v_s5_log_dt': out['v_s5_log_dt'], 'v_s5_b_re': out['v_s5_b_re'], 'v_s5_b_im': out['v_s5_b_im'], 'v_s5_c_re': out['v_s5_c_re'], 'v_s5_c_im': out['v_s5_c_im'], 'v_s5_d': out['v_s5_d'], 'v_s5_glu_w': out['v_s5_glu_w'], 'v_s5_glu_b': out['v_s5_glu_b'], 'v_w_branch_hg': out['v_w_branch_hg'], 'v_w_branch_s5': out['v_w_branch_s5'], 'v_w_out': out['v_w_out'], 'v_norm_ffn_g': out['v_norm_ffn_g'], 'v_w_up': out['v_w_up'], 'v_conv_w': out['v_conv_w'], 'v_conv_b': out['v_conv_b'], 'v_w_down': out['v_w_down'], 'v_norm_ple_g': out['v_norm_ple_g'], 'v_w_ple_gate': out['v_w_ple_gate'], 'v_w_ple_proj': out['v_w_ple_proj'], 'v_norm_final_g': out['v_norm_final_g']}


def _loss(weights, diff, rest, loss_target):
    with _jax.named_scope("forward"):
        args = {**rest, TWIN_DIFF_INPUT: diff, **{k: w.astype(_WEIGHT_DTYPES[k]) for k, w in weights.items()}}
        y = _forward(args)
    with _jax.named_scope("loss_head"):
        err = _jnp.square(y.astype(_jnp.float32) - loss_target)
        return 0.5 * _jnp.sum(_jnp.mean(err, axis=-1)) if err.ndim else 0.5 * err


def _adamw(w, g, m, v):
    m = ADAM_B1 * m + (1.0 - ADAM_B1) * g
    v = ADAM_B2 * v + (1.0 - ADAM_B2) * _jnp.square(g)
    m_hat = m / (1.0 - ADAM_B1 ** ADAM_STEP)
    v_hat = v / (1.0 - ADAM_B2 ** ADAM_STEP)
    delta = -ADAM_LR * (m_hat / (_jnp.sqrt(v_hat) + ADAM_EPS) + ADAM_WD * w)
    return delta, m, v


def reference(x, p, norm_mix_g, w_in, hg_lb_logits, hg_norm_g, s5_lambda_re, s5_lambda_im, s5_log_dt, s5_b_re, s5_b_im, s5_c_re, s5_c_im, s5_d, s5_glu_w, s5_glu_b, w_branch_hg, w_branch_s5, w_out, norm_ffn_g, w_up, conv_w, conv_b, w_down, norm_ple_g, w_ple_gate, w_ple_proj, norm_final_g, loss_target, m_norm_mix_g, m_w_in, m_hg_lb_logits, m_hg_norm_g, m_s5_lambda_re, m_s5_lambda_im, m_s5_log_dt, m_s5_b_re, m_s5_b_im, m_s5_c_re, m_s5_c_im, m_s5_d, m_s5_glu_w, m_s5_glu_b, m_w_branch_hg, m_w_branch_s5, m_w_out, m_norm_ffn_g, m_w_up, m_conv_w, m_conv_b, m_w_down, m_norm_ple_g, m_w_ple_gate, m_w_ple_proj, m_norm_final_g, v_norm_mix_g, v_w_in, v_hg_lb_logits, v_hg_norm_g, v_s5_lambda_re, v_s5_lambda_im, v_s5_log_dt, v_s5_b_re, v_s5_b_im, v_s5_c_re, v_s5_c_im, v_s5_d, v_s5_glu_w, v_s5_glu_b, v_w_branch_hg, v_w_branch_s5, v_w_out, v_norm_ffn_g, v_w_up, v_conv_w, v_conv_b, v_w_down, v_norm_ple_g, v_w_ple_gate, v_w_ple_proj, v_norm_final_g):
    given = dict(x=x, p=p, norm_mix_g=norm_mix_g, w_in=w_in, hg_lb_logits=hg_lb_logits, hg_norm_g=hg_norm_g, s5_lambda_re=s5_lambda_re, s5_lambda_im=s5_lambda_im, s5_log_dt=s5_log_dt, s5_b_re=s5_b_re, s5_b_im=s5_b_im, s5_c_re=s5_c_re, s5_c_im=s5_c_im, s5_d=s5_d, s5_glu_w=s5_glu_w, s5_glu_b=s5_glu_b, w_branch_hg=w_branch_hg, w_branch_s5=w_branch_s5, w_out=w_out, norm_ffn_g=norm_ffn_g, w_up=w_up, conv_w=conv_w, conv_b=conv_b, w_down=w_down, norm_ple_g=norm_ple_g, w_ple_gate=w_ple_gate, w_ple_proj=w_ple_proj, norm_final_g=norm_final_g, loss_target=loss_target, m_norm_mix_g=m_norm_mix_g, m_w_in=m_w_in, m_hg_lb_logits=m_hg_lb_logits, m_hg_norm_g=m_hg_norm_g, m_s5_lambda_re=m_s5_lambda_re, m_s5_lambda_im=m_s5_lambda_im, m_s5_log_dt=m_s5_log_dt, m_s5_b_re=m_s5_b_re, m_s5_b_im=m_s5_b_im, m_s5_c_re=m_s5_c_re, m_s5_c_im=m_s5_c_im, m_s5_d=m_s5_d, m_s5_glu_w=m_s5_glu_w, m_s5_glu_b=m_s5_glu_b, m_w_branch_hg=m_w_branch_hg, m_w_branch_s5=m_w_branch_s5, m_w_out=m_w_out, m_norm_ffn_g=m_norm_ffn_g, m_w_up=m_w_up, m_conv_w=m_conv_w, m_conv_b=m_conv_b, m_w_down=m_w_down, m_norm_ple_g=m_norm_ple_g, m_w_ple_gate=m_w_ple_gate, m_w_ple_proj=m_w_ple_proj, m_norm_final_g=m_norm_final_g, v_norm_mix_g=v_norm_mix_g, v_w_in=v_w_in, v_hg_lb_logits=v_hg_lb_logits, v_hg_norm_g=v_hg_norm_g, v_s5_lambda_re=v_s5_lambda_re, v_s5_lambda_im=v_s5_lambda_im, v_s5_log_dt=v_s5_log_dt, v_s5_b_re=v_s5_b_re, v_s5_b_im=v_s5_b_im, v_s5_c_re=v_s5_c_re, v_s5_c_im=v_s5_c_im, v_s5_d=v_s5_d, v_s5_glu_w=v_s5_glu_w, v_s5_glu_b=v_s5_glu_b, v_w_branch_hg=v_w_branch_hg, v_w_branch_s5=v_w_branch_s5, v_w_out=v_w_out, v_norm_ffn_g=v_norm_ffn_g, v_w_up=v_w_up, v_conv_w=v_conv_w, v_conv_b=v_conv_b, v_w_down=v_w_down, v_norm_ple_g=v_norm_ple_g, v_w_ple_gate=v_w_ple_gate, v_w_ple_proj=v_w_ple_proj, v_norm_final_g=v_norm_final_g)
    weights = {n: given[n] for n in TWIN_WEIGHTS}
    shared = {n: given[n] for n in SHARED_INPUTS}
    per_example = {n: given[n] for n in ['x', 'p']}
    grad_fn = _jax.value_and_grad(_loss, argnums=(0, 1))

    def one_microbatch(ex, loss_target):
        ex = dict(ex)
        diff = ex.pop(TWIN_DIFF_INPUT)
        return grad_fn(weights, diff, {**shared, **ex}, loss_target)

    if N_MICROBATCH == 1:
        loss, (grad_w, grad_x) = one_microbatch(per_example, given["loss_target"])
    else:
        def body(carry, xs):
            loss_sum, grad_sum = carry
            l_k, (gw_k, gx_k) = one_microbatch(xs[0], xs[1])
            with _jax.named_scope("update"):
                return (loss_sum + l_k, _jax.tree.map(_jnp.add, grad_sum, gw_k)), gx_k

        init = (_jnp.zeros((), _jnp.float32), _jax.tree.map(_jnp.zeros_like, weights))
        (loss, grad_w), grad_x = _jax.lax.scan(body, init, (per_example, given["loss_target"]))
    with _jax.named_scope("update"):
        delta_w, new_m, new_v = {}, {}, {}
        for n in TWIN_WEIGHTS:
            delta_w[n], new_m[n], new_v[n] = _adamw(weights[n], grad_w[n], given["m_" + n], given["v_" + n])
    return (loss, grad_x, *[grad_w[n] for n in TWIN_WEIGHTS], *[delta_w[n] for n in TWIN_WEIGHTS],
            *[new_m[n] for n in TWIN_WEIGHTS], *[new_v[n] for n in TWIN_WEIGHTS])
```

```python
import functools
import math

import jax
import jax.numpy as jnp
from jax import lax
from jax.experimental import pallas as pl
from jax.experimental.pallas import tpu as pltpu

f32 = jnp.float32
bf16 = jnp.bfloat16
HIGHEST = lax.Precision.HIGHEST
MESH = pl.DeviceIdType.MESH

N_DEV = 8
D = 1024
HW = 512
HD = 128
NH = 4
CH = 64
SW = 512
SG = 32
SP = 16
SN = 64
SL = SG * SN
NST = 4
STW = SL // NST
DFF = 2816
PLE = 256
EPS = 1e-6
LANES = 1024
VMEM_LIMIT = 56 * 1024 * 1024

ADAM_LR, ADAM_B1, ADAM_B2, ADAM_EPS, ADAM_WD, ADAM_STEP = 0.001, 0.9, 0.999, 1e-08, 0.01, 10


def _pc(body, **kw):
    return pl.pallas_call(body, **kw)


def _params(n_axes=1, **kw):
    return pltpu.CompilerParams(dimension_semantics=("arbitrary",) * n_axes, vmem_limit_bytes=VMEM_LIMIT, **kw)


def _whole(shape):
    nd = len(shape)
    return pl.BlockSpec(shape, lambda *_: (0,) * nd, pipeline_mode=pl.Buffered(1))


def _acc(shape):
    nd = len(shape)
    return pl.BlockSpec(shape, lambda *_: (0,) * nd)


def _dot(a, b):
    return jnp.dot(a.astype(bf16), b.astype(bf16), preferred_element_type=f32)


def _dot_nt(a, b):
    return lax.dot_general(a.astype(bf16), b.astype(bf16), (((1,), (1,)), ((), ())), preferred_element_type=f32)


def _dot_tn(a, b):
    return lax.dot_general(a.astype(bf16), b.astype(bf16), (((0,), (0,)), ((), ())), preferred_element_type=f32)


def _sig(x):
    return jax.nn.sigmoid(x)


def _dsilu(z, s):
    return s * (1.0 + z * (1.0 - s))


_GC = math.sqrt(2.0 / math.pi)


def _gelu_and_grad(y):
    t = jnp.tanh(_GC * (y + 0.044715 * y * y * y))
    g = 0.5 * y * (1.0 + t)
    dg = 0.5 * (1.0 + t) + 0.5 * y * (1.0 - t * t) * _GC * (1.0 + 3.0 * 0.044715 * y * y)
    return g, dg


def _rms(x):
    r = lax.rsqrt(jnp.mean(x * x, axis=-1, keepdims=True) + EPS)
    return x * r, r


def _rms_bwd(dy, xh, r, g):
    dxh = dy * g
    dx = r * (dxh - xh * jnp.mean(dxh * xh, axis=-1, keepdims=True))
    return dx, dy * xh


def _colsum(x):
    return jnp.sum(x, axis=0, keepdims=True)


def _in_proj(x, g, w):
    T = x.shape[0]
    tm = 256

    def body(x_ref, g_ref, w_ref, h_ref, hg_ref, u_ref, gt_ref):
        xh, _ = _rms(x_ref[...])
        h = (xh * g_ref[...]).astype(bf16)
        h_ref[...] = h
        hg_ref[...] = jnp.dot(h, w_ref[:, 0:4 * HW], preferred_element_type=f32)
        u_ref[...] = jnp.dot(h, w_ref[:, 4 * HW:4 * HW + SW], preferred_element_type=f32)
        gt_ref[...] = jnp.dot(h, w_ref[:, 4 * HW + SW:], preferred_element_type=f32)

    row = lambda n: pl.BlockSpec((tm, n), lambda i: (i, 0))
    return _pc(
        body, name="in_proj", grid=(T // tm,),
        in_specs=[row(D), _whole((1, D)), _whole(w.shape)],
        out_specs=[row(D), row(4 * HW), row(SW), row(2 * D)],
        out_shape=[jax.ShapeDtypeStruct((T, D), bf16), jax.ShapeDtypeStruct((T, 4 * HW), f32),
                   jax.ShapeDtypeStruct((T, SW), f32), jax.ShapeDtypeStruct((T, 2 * D), f32)],
        compiler_params=_params(),
    )(x, g, w)


def _hgrn_common(lg_ref, q_ref, f_ref, h):
    sl = slice(h * HD, (h + 1) * HD)
    lg = lg_ref[...]
    mx = jnp.max(lg, axis=0, keepdims=True)
    e = jnp.exp(lg - mx)
    lb = (e[0:1, :] / (e[0:1, :] + e[1:2, :]))[:, sl]
    qr = q_ref[:, sl]
    fr = f_ref[:, sl]
    sig = _sig(fr)
    f = lb + (1.0 - lb) * sig
    k = 1.0 - f
    r_i = lax.broadcasted_iota(jnp.int32, (CH, CH), 0)
    c_i = lax.broadcasted_iota(jnp.int32, (CH, CH), 1)
    tril = (r_i >= c_i)
    b = jnp.dot(tril.astype(f32), jnp.log(f), precision=HIGHEST, preferred_element_type=f32)
    bref = b[CH // 2:CH // 2 + 1, :]
    blast = b[CH - 1:CH, :]
    sq = _sig(qr)
    q = qr * sq
    e1 = jnp.exp(b - bref)
    e2 = jnp.exp(bref - b)
    e3 = jnp.exp(blast - b)
    e4 = jnp.exp(b)
    return dict(sl=sl, lb=lb, qr=qr, sq=sq, sig=sig, f=f, k=k, tril=tril, q=q, e1=e1, e2=e2, e3=e3, e4=e4,
                qs=q * e1, ks=k * e2, kl=k * e3, qb=q * e4, dec=jnp.exp(blast))


def _hgrn_fwd(proj_hg, logits, ng):
    T = proj_hg.shape[0]
    nch = T // CH

    def body(q_ref, f_ref, i_ref, og_ref, lg_ref, ng_ref, out_ref, sprev_ref, st_ref):
        @pl.when(pl.program_id(0) == 0)
        def _():
            st_ref[...] = jnp.zeros_like(st_ref)

        for h in range(NH):
            c = _hgrn_common(lg_ref, q_ref, f_ref, h)
            sl = c["sl"]
            v = i_ref[:, sl]
            og = og_ref[:, sl]
            p = jnp.where(c["tril"], _dot_nt(c["qs"], c["ks"]), 0.0)
            st = st_ref[h]
            sprev_ref[0, h] = st
            o = _dot(p, v) + _dot_nt(c["qb"], st)
            st_ref[h] = c["dec"] * st + _dot_tn(v, c["kl"])
            oh, _ = _rms(o)
            out_ref[:, sl] = (oh * ng_ref[...] * (og * _sig(og))).astype(bf16)

    col = lambda j: pl.BlockSpec((CH, HW), lambda n, j=j: (n, j))
    return _pc(
        body, name="hgrn_fwd", grid=(nch,),
        in_specs=[col(0), col(1), col(2), col(3), _whole((2, HW)), _whole((1, HD))],
        out_specs=[pl.BlockSpec((CH, HW), lambda n: (n, 0)),
                   pl.BlockSpec((1, NH, HD, HD), lambda n: (n, 0, 0, 0))],
        out_shape=[jax.ShapeDtypeStruct((T, HW), bf16), jax.ShapeDtypeStruct((nch, NH, HD, HD), f32)],
        scratch_shapes=[pltpu.VMEM((NH, HD, HD), f32)],
        compiler_params=_params(),
    )(proj_hg, proj_hg, proj_hg, proj_hg, logits, ng)


def _hgrn_bwd(proj_hg, logits, ng, sprev, d_out):
    T = proj_hg.shape[0]
    nch = T // CH

    def body(q_ref, f_ref, i_ref, og_ref, lg_ref, ng_ref, sp_ref, do_ref, dp_ref, dlb_ref, dng_ref, gt_ref):
        @pl.when(pl.program_id(0) == 0)
        def _():
            gt_ref[...] = jnp.zeros_like(gt_ref)
            dlb_ref[...] = jnp.zeros_like(dlb_ref)
            dng_ref[...] = jnp.zeros_like(dng_ref)

        row = lax.broadcasted_iota(jnp.int32, (CH, HD), 0)
        ng_v = ng_ref[...]
        for h in range(NH):
            c = _hgrn_common(lg_ref, q_ref, f_ref, h)
            sl = c["sl"]
            tril = c["tril"]
            v = i_ref[:, sl]
            og = og_ref[:, sl]
            st = sp_ref[0, h]
            gt = gt_ref[h]
            p = jnp.where(tril, _dot_nt(c["qs"], c["ks"]), 0.0)
            o = _dot(p, v) + _dot_nt(c["qb"], st)
            oh, r = _rms(o)
            sog = _sig(og)
            d_o_gated = do_ref[:, sl]
            d_on = d_o_gated * (og * sog)
            dog = d_o_gated * (oh * ng_v) * _dsilu(og, sog)
            d_o, dng_rows = _rms_bwd(d_on, oh, r, ng_v)
            dng_ref[...] += _colsum(dng_rows)
            dqb = _dot(d_o, st)
            dst = _dot_tn(d_o, c["qb"])
            dp = jnp.where(tril, _dot_nt(d_o, v), 0.0)
            dv = _dot_tn(p, d_o) + _dot_nt(c["kl"], gt)
            dqs = _dot(dp, c["ks"])
            dks = _dot_tn(dp, c["qs"])
            dkl = _dot(v, gt)
            ddec = _colsum(gt * st)
            gt_ref[h] = dst + c["dec"] * gt
            dq = dqs * c["e1"] + dqb * c["e4"]
            dk = dks * c["e2"] + dkl * c["e3"]
            t_qs = dqs * c["qs"]
            t_ks = dks * c["ks"]
            t_kl = dkl * c["kl"]
            db = t_qs - t_ks - t_kl + dqb * c["qb"]
            dbref = _colsum(t_ks - t_qs)
            dblast = _colsum(t_kl) + ddec * c["dec"]
            db = db + jnp.where(row == CH // 2, dbref, 0.0) + jnp.where(row == CH - 1, dblast, 0.0)
            r_i = lax.broadcasted_iota(jnp.int32, (CH, CH), 0)
            c_i = lax.broadcasted_iota(jnp.int32, (CH, CH), 1)
            dlf = jnp.dot((c_i >= r_i).astype(f32), db, precision=HIGHEST, preferred_element_type=f32)
            df = dlf / c["f"] - dk
            sig = c["sig"]
            dfr = df * (1.0 - c["lb"]) * sig * (1.0 - sig)
            dlb_ref[:, sl] += _colsum(df * (1.0 - sig))
            dp_ref[:, h * HD:(h + 1) * HD] = dq * _dsilu(c["qr"], c["sq"])
            dp_ref[:, HW + h * HD:HW + (h + 1) * HD] = dfr
            dp_ref[:, 2 * HW + h * HD:2 * HW + (h + 1) * HD] = dv
            dp_ref[:, 3 * HW + h * HD:3 * HW + (h + 1) * HD] = dog

    rev = lambda n: nch - 1 - n
    col = lambda j: pl.BlockSpec((CH, HW), lambda n, j=j: (rev(n), j))
    return _pc(
        body, name="hgrn_bwd", grid=(nch,),
        in_specs=[col(0), col(1), col(2), col(3), _whole((2, HW)), _whole((1, HD)),
                  pl.BlockSpec((1, NH, HD, HD), lambda n: (rev(n), 0, 0, 0)),
                  pl.BlockSpec((CH, HW), lambda n: (rev(n), 0))],
        out_specs=[pl.BlockSpec((CH, 4 * HW), lambda n: (rev(n), 0)), _acc((1, HW)), _acc((1, HD))],
        out_shape=[jax.ShapeDtypeStruct((T, 4 * HW), f32), jax.ShapeDtypeStruct((1, HW), f32),
                   jax.ShapeDtypeStruct((1, HD), f32)],
        scratch_shapes=[pltpu.VMEM((NH, HD, HD), f32)],
        compiler_params=_params(),
    )(proj_hg, proj_hg, proj_hg, proj_hg, logits, ng, sprev, d_out)


S5_TM = 256
S5_LW = 512


def _scan_rows(xs_ref, pw_ref, carry_ref, tm, reverse):
    nblk = tm // 8
    row = lax.broadcasted_iota(jnp.int32, (8, S5_LW), 0)
    sgn = -1.0 if reverse else 1.0
    for ls in range(SL // S5_LW):
        lre = pl.ds(ls * S5_LW, S5_LW)
        lim = pl.ds(SL + ls * S5_LW, S5_LW)
        pr = pw_ref[:, lre]
        pi = pw_ref[:, lim] * sgn
        steps = []
        for d in (1, 2, 4):
            j = (8 - d) if reverse else (d - 1)
            steps.append((d, pr[j:j + 1, :], pi[j:j + 1, :]))
        crow = 0 if reverse else 7
        c0r = carry_ref[crow:crow + 1, lre]
        c0i = carry_ref[crow:crow + 1, lim]

        def blk(n, carry, lre=lre, lim=lim, pr=pr, pi=pi, steps=steps):
            cr, ci = carry
            r0 = pl.multiple_of(((nblk - 1 - n) if reverse else n) * 8, 8)
            xr = xs_ref[pl.ds(r0, 8), lre]
            xi = xs_ref[pl.ds(r0, 8), lim]
            for d, ar, ai in steps:
                if reverse:
                    sr = pltpu.roll(xr, 8 - d, 0)
                    si = pltpu.roll(xi, 8 - d, 0)
                    m = row < 8 - d
                else:
                    sr = pltpu.roll(xr, d, 0)
                    si = pltpu.roll(xi, d, 0)
                    m = row >= d
                xr = xr + jnp.where(m, ar * sr - ai * si, 0.0)
                xi = xi + jnp.where(m, ar * si + ai * sr, 0.0)
            xr = xr + pr * cr - pi * ci
            xi = xi + pr * ci + pi * cr
            xs_ref[pl.ds(r0, 8), lre] = xr
            xs_ref[pl.ds(r0, 8), lim] = xi
            return (xr[crow:crow + 1, :], xi[crow:crow + 1, :])

        lax.fori_loop(0, nblk, blk, (c0r, c0i))
    last = (0 if reverse else tm - 8)
    carry_ref[...] = xs_ref[last:last + 8, :]


def _s5_fwd(u, bdb, bdc, pw, dskip, glu_w, glu_b):
    T = u.shape[0]
    tm = S5_TM

    def body(u_ref, bdb_ref, bdc_ref, pw_ref, ds_ref, gw_ref, gb_ref, x_ref, y_ref, g_ref, o_ref, xs_ref, carry_ref):
        @pl.when(pl.program_id(0) == 0)
        def _():
            carry_ref[...] = jnp.zeros_like(carry_ref)

        uv = u_ref[...]
        ub = uv.astype(bf16)
        for part in range(2):
            for s in range(NST):
                xs_ref[:, part * SL + s * STW:part * SL + (s + 1) * STW] = jnp.dot(
                    ub[:, s * 128:(s + 1) * 128], bdb_ref[part * NST + s], preferred_element_type=f32)
        _scan_rows(xs_ref, pw_ref, carry_ref, tm, reverse=False)
        x_ref[...] = xs_ref[...]
        ys = []
        for s in range(NST):
            acc = None
            for part in range(2):
                xv = xs_ref[:, part * SL + s * STW:part * SL + (s + 1) * STW].astype(bf16)
                t = jnp.dot(xv, bdc_ref[part * NST + s], preferred_element_type=f32)
                acc = t if acc is None else acc + t
            ys.append(acc)
        y = jnp.concatenate(ys, axis=1) + ds_ref[...] * uv
        y_ref[...] = y
        g, _ = _gelu_and_grad(y)
        gb = g.astype(bf16)
        g_ref[...] = gb
        z = jnp.dot(gb, gw_ref[...], preferred_element_type=f32) + gb_ref[...]
        o_ref[...] = (g * _sig(z)).astype(bf16)

    row = lambda n: pl.BlockSpec((tm, n), lambda i: (i, 0))
    return _pc(
        body, name="s5_fwd", grid=(T // tm,),
        in_specs=[row(SW), _whole(bdb.shape), _whole(bdc.shape), _whole(pw.shape), _whole((1, SW)),
                  _whole((SW, SW)), _whole((1, SW))],
        out_specs=[row(2 * SL), row(SW), row(SW), row(SW)],
        out_shape=[jax.ShapeDtypeStruct((T, 2 * SL), f32), jax.ShapeDtypeStruct((T, SW), f32),
                   jax.ShapeDtypeStruct((T, SW), bf16), jax.ShapeDtypeStruct((T, SW), bf16)],
        scratch_shapes=[pltpu.VMEM((tm, 2 * SL), f32), pltpu.VMEM((8, 2 * SL), f32)],
        compiler_params=_params(),
    )(u, bdb, bdc, pw, dskip, glu_w, glu_b)


def _s5_bwd(d_out, y, u, x, bdb, bdc, pwr, dskip, glu_w, glu_b):
    T = u.shape[0]
    tm = S5_TM
    nt = T // tm

    def body(do_ref, y_ref, u_ref, x_ref, xh_ref, bdb_ref, bdc_ref, pw_ref, ds_ref, gw_ref, gb_ref,
             du_ref, lam_ref, dy_ref, dz_ref, dar_ref, dai_ref, dd_ref, dgb_ref, gs_ref, carry_ref):
        i = pl.program_id(0)

        @pl.when(i == 0)
        def _():
            carry_ref[...] = jnp.zeros_like(carry_ref)
            dar_ref[...] = jnp.zeros_like(dar_ref)
            dai_ref[...] = jnp.zeros_like(dai_ref)
            dd_ref[...] = jnp.zeros_like(dd_ref)
            dgb_ref[...] = jnp.zeros_like(dgb_ref)

        yv = y_ref[...]
        uv = u_ref[...]
        g, gp = _gelu_and_grad(yv)
        z = jnp.dot(g.astype(bf16), gw_ref[...], preferred_element_type=f32) + gb_ref[...]
        sg = _sig(z)
        do = do_ref[...].astype(f32)
        dz = do * g * sg * (1.0 - sg)
        dz_ref[...] = dz.astype(bf16)
        dgb_ref[...] += _colsum(dz)
        dy = (do * sg + _dot_nt(dz, gw_ref[...])) * gp
        dyb = dy.astype(bf16)
        dy_ref[...] = dyb
        dd_ref[...] += _colsum(dy * uv)
        for part in range(2):
            for s in range(NST):
                gs_ref[:, part * SL + s * STW:part * SL + (s + 1) * STW] = lax.dot_general(
                    dyb[:, s * 128:(s + 1) * 128], bdc_ref[part * NST + s], (((1,), (1,)), ((), ())),
                    preferred_element_type=f32)
        _scan_rows(gs_ref, pw_ref, carry_ref, tm, reverse=True)
        lam_ref[...] = gs_ref[...].astype(bf16)
        dus = []
        for s in range(NST):
            acc = None
            for part in range(2):
                lv = gs_ref[:, part * SL + s * STW:part * SL + (s + 1) * STW].astype(bf16)
                t = lax.dot_general(lv, bdb_ref[part * NST + s], (((1,), (1,)), ((), ())), preferred_element_type=f32)
                acc = t if acc is None else acc + t
            dus.append(acc)
        du_ref[...] = jnp.concatenate(dus, axis=1) + dy * ds_ref[...]
        first = (i == nt - 1)
        rowi = lax.broadcasted_iota(jnp.int32, (tm, S5_LW), 0)
        for ls in range(SL // S5_LW):
            lre = slice(ls * S5_LW, (ls + 1) * S5_LW)
            lim = slice(SL + ls * S5_LW, SL + (ls + 1) * S5_LW)
            hr = jnp.where(first, 0.0, xh_ref[7:8, lre])
            hi = jnp.where(first, 0.0, xh_ref[7:8, lim])
            xpr = jnp.where(rowi == 0, hr, pltpu.roll(x_ref[:, lre], 1, 0))
            xpi = jnp.where(rowi == 0, hi, pltpu.roll(x_ref[:, lim], 1, 0))
            lr = gs_ref[:, lre]
            li = gs_ref[:, lim]
            dar_ref[:, lre] += _colsum(lr * xpr + li * xpi)
            dai_ref[:, lre] += _colsum(li * xpr - lr * xpi)

    rev = lambda i: nt - 1 - i
    row = lambda n: pl.BlockSpec((tm, n), lambda i: (rev(i), 0))
    halo = pl.BlockSpec((8, 2 * SL), lambda i: (jnp.maximum(rev(i) * (tm // 8) - 1, 0), 0))
    return _pc(
        body, name="s5_bwd", grid=(nt,),
        in_specs=[row(SW), row(SW), row(SW), row(2 * SL), halo, _whole(bdb.shape), _whole(bdc.shape),
                  _whole(pwr.shape), _whole((1, SW)), _whole((SW, SW)), _whole((1, SW))],
        out_specs=[row(SW), row(2 * SL), row(SW), row(SW), _acc((1, SL)), _acc((1, SL)), _acc((1, SW)), _acc((1, SW))],
        out_shape=[jax.ShapeDtypeStruct((T, SW), f32), jax.ShapeDtypeStruct((T, 2 * SL), bf16),
                   jax.ShapeDtypeStruct((T, SW), bf16), jax.ShapeDtypeStruct((T, SW), bf16),
                   jax.ShapeDtypeStruct((1, SL), f32), jax.ShapeDtypeStruct((1, SL), f32),
                   jax.ShapeDtypeStruct((1, SW), f32), jax.ShapeDtypeStruct((1, SW), f32)],
        scratch_shapes=[pltpu.VMEM((tm, 2 * SL), f32), pltpu.VMEM((8, 2 * SL), f32)],
        compiler_params=_params(),
    )(d_out, y, u, x, x, bdb, bdc, pwr, dskip, glu_w, glu_b)


def _mix_up(x, hg_o, s5_o, gates, w_bhg, w_bs5, w_out, g_ffn, w_up):
    T = x.shape[0]
    tm = 256

    def body(x_ref, hg_ref, s5_ref, gt_ref, wh_ref, ws_ref, wo_ref, g_ref, wu_ref, x1_ref, mg_ref, h2_ref, a_ref):
        yh = jnp.dot(hg_ref[...], wh_ref[...], preferred_element_type=f32)
        ys = jnp.dot(s5_ref[...], ws_ref[...], preferred_element_type=f32)
        merged = (_sig(gt_ref[:, 0:D]) * yh + _sig(gt_ref[:, D:2 * D]) * ys).astype(bf16)
        mg_ref[...] = merged
        x1 = x_ref[...] + jnp.dot(merged, wo_ref[...], preferred_element_type=f32)
        x1_ref[...] = x1
        xh, _ = _rms(x1)
        h2 = (xh * g_ref[...]).astype(bf16)
        h2_ref[...] = h2
        a_ref[...] = jnp.dot(h2, wu_ref[...], preferred_element_type=f32)

    row = lambda n: pl.BlockSpec((tm, n), lambda i: (i, 0))
    return _pc(
        body, name="mix_up", grid=(T // tm,),
        in_specs=[row(D), row(HW), row(SW), row(2 * D), _whole(w_bhg.shape), _whole(w_bs5.shape), _whole(w_out.shape),
                  _whole((1, D)), _whole(w_up.shape)],
        out_specs=[row(D), row(D), row(D), row(2 * DFF)],
        out_shape=[jax.ShapeDtypeStruct((T, D), f32), jax.ShapeDtypeStruct((T, D), bf16),
                   jax.ShapeDtypeStruct((T, D), bf16), jax.ShapeDtypeStruct((T, 2 * DFF), f32)],
        compiler_params=_params(),
    )(x, hg_o, s5_o, gates, w_bhg, w_bs5, w_out, g_ffn, w_up)


FFN_TM = 128


def _conv_rows(a, halo, first, conv_w, conv_b):
    tm = a.shape[0]
    row = lax.broadcasted_iota(jnp.int32, (tm, 1), 0)
    hm1 = jnp.where(first, 0.0, halo[7:8, :])
    hm2 = jnp.where(first, 0.0, halo[6:7, :])
    a1 = jnp.where(row == 0, hm1, pltpu.roll(a, 1, 0))
    a2 = jnp.where(row == 0, hm2, jnp.where(row == 1, hm1, pltpu.roll(a, 2, 0)))
    c = conv_b + conv_w[0:1, :] * a2 + conv_w[1:2, :] * a1 + conv_w[2:3, :] * a
    return c, a1, a2


def _ffn_tail(a, conv_w, conv_b, w_down, x1, p, g_ple, w_pg, w_pp, g_fin, tgt):
    T = a.shape[0]
    tm = FFN_TM

    def body(a_ref, ah_ref, cw_ref, cb_ref, wd_ref, x1_ref, p_ref, gp_ref, wpg_ref, wpp_ref, gf_ref, t_ref,
             dx2_ref, gd_ref, h3_ref, dz_ref, dpp_ref, loss_ref, dgf_ref, dgp_ref):
        i = pl.program_id(0)

        @pl.when(i == 0)
        def _():
            loss_ref[...] = jnp.zeros_like(loss_ref)
            dgf_ref[...] = jnp.zeros_like(dgf_ref)
            dgp_ref[...] = jnp.zeros_like(dgp_ref)

        c, _, _ = _conv_rows(a_ref[...], ah_ref[...], i == 0, cw_ref[...], cb_ref[...])
        gl, _ = _gelu_and_grad(c[:, 0:DFF])
        gated = (gl * c[:, DFF:]).astype(bf16)
        gd_ref[...] = gated
        x2 = x1_ref[...] + jnp.dot(gated, wd_ref[...], preferred_element_type=f32)
        xh2, r2 = _rms(x2)
        h3 = (xh2 * gp_ref[...]).astype(bf16)
        h3_ref[...] = h3
        pg = _sig(jnp.dot(h3, wpg_ref[...], preferred_element_type=f32))
        pp = _dot(p_ref[...], wpp_ref[...])
        x3 = x2 + pg * pp
        xh3, r3 = _rms(x3)
        diff = xh3 * gf_ref[...] - t_ref[...]
        loss_ref[...] += 0.5 * jnp.sum(jnp.mean(diff * diff, axis=-1, keepdims=True), axis=0, keepdims=True)
        dy = diff * (1.0 / D)
        dx3, dgf_rows = _rms_bwd(dy, xh3, r3, gf_ref[...])
        dgf_ref[...] += _colsum(dgf_rows)
        dpp = dx3 * pg
        dpp_ref[...] = dpp.astype(bf16)
        dz = dx3 * pp * pg * (1.0 - pg)
        dz_ref[...] = dz.astype(bf16)
        dh3 = _dot_nt(dz, wpg_ref[...])
        dx2n, dgp_rows = _rms_bwd(dh3, xh2, r2, gp_ref[...])
        dgp_ref[...] += _colsum(dgp_rows)
        dx2_ref[...] = dx3 + dx2n

    row = lambda n: pl.BlockSpec((tm, n), lambda i: (i, 0))
    halo = pl.BlockSpec((8, 2 * DFF), lambda i: (jnp.maximum(i * (tm // 8) - 1, 0), 0))
    return _pc(
        body, name="ffn_tail", grid=(T // tm,),
        in_specs=[row(2 * DFF), halo, _whole((3, 2 * DFF)), _whole((1, 2 * DFF)), _whole(w_down.shape), row(D), row(PLE),
                  _whole((1, D)), _whole(w_pg.shape), _whole(w_pp.shape), _whole((1, D)), row(D)],
        out_specs=[row(D), row(DFF), row(D), row(D), row(D), _acc((1, 128)), _acc((1, D)), _acc((1, D))],
        out_shape=[jax.ShapeDtypeStruct((T, D), f32), jax.ShapeDtypeStruct((T, DFF), bf16), jax.ShapeDtypeStruct((T, D), bf16),
                   jax.ShapeDtypeStruct((T, D), bf16), jax.ShapeDtypeStruct((T, D), bf16),
                   jax.ShapeDtypeStruct((1, 128), f32), jax.ShapeDtypeStruct((1, D), f32), jax.ShapeDtypeStruct((1, D), f32)],
        compiler_params=_params(),
    )(a, a, conv_w, conv_b, w_down, x1, p, g_ple, w_pg, w_pp, g_fin, tgt)


def _ffn_bwd(dx2, a, conv_w, conv_b, w_down, w_up, x1, g_ffn):
    T = a.shape[0]
    tm = FFN_TM
    nt = T // tm

    def body(dx2_ref, a_ref, ah_ref, cw_ref, cb_ref, wd_ref, wu_ref, x1_ref, g_ref,
             da_ref, dx1_ref, dcw_ref, dcb_ref, dg_ref, carry_ref):
        i = pl.program_id(0)

        @pl.when(i == 0)
        def _():
            carry_ref[...] = jnp.zeros_like(carry_ref)
            dcw_ref[...] = jnp.zeros_like(dcw_ref)
            dcb_ref[...] = jnp.zeros_like(dcb_ref)
            dg_ref[...] = jnp.zeros_like(dg_ref)

        av = a_ref[...]
        cw = cw_ref[...]
        c, a1, a2 = _conv_rows(av, ah_ref[...], i == nt - 1, cw, cb_ref[...])
        dx2 = dx2_ref[...]
        dgd = _dot_nt(dx2, wd_ref[...])
        gl, gp = _gelu_and_grad(c[:, 0:DFF])
        dc = jnp.concatenate([dgd * c[:, DFF:] * gp, dgd * gl], axis=1)
        dcb_ref[...] += _colsum(dc)
        dcw_ref[0:1, :] += _colsum(dc * a2)
        dcw_ref[1:2, :] += _colsum(dc * a1)
        dcw_ref[2:3, :] += _colsum(dc * av)
        row = lax.broadcasted_iota(jnp.int32, (tm, 1), 0)
        n1 = carry_ref[0:1, :]
        n2 = carry_ref[1:2, :]
        up1 = jnp.where(row == tm - 1, n1, pltpu.roll(dc, tm - 1, 0))
        up2 = jnp.where(row == tm - 1, n2, jnp.where(row == tm - 2, n1, pltpu.roll(dc, tm - 2, 0)))
        da = (cw[2:3, :] * dc + cw[1:2, :] * up1 + cw[0:1, :] * up2).astype(bf16)
        carry_ref[...] = dc[0:8, :]
        da_ref[...] = da
        dh2 = lax.dot_general(da, wu_ref[...], (((1,), (1,)), ((), ())), preferred_element_type=f32)
        xh, r = _rms(x1_ref[...])
        dx1n, dg_rows = _rms_bwd(dh2, xh, r, g_ref[...])
        dg_ref[...] += _colsum(dg_rows)
        dx1_ref[...] = dx2 + dx1n

    rev = lambda i: nt - 1 - i
    row = lambda n: pl.BlockSpec((tm, n), lambda i: (rev(i), 0))
    halo = pl.BlockSpec((8, 2 * DFF), lambda i: (jnp.maximum(rev(i) * (tm // 8) - 1, 0), 0))
    return _pc(
        body, name="ffn_bwd", grid=(nt,),
        in_specs=[row(D), row(2 * DFF), halo, _whole((3, 2 * DFF)), _whole((1, 2 * DFF)), _whole(w_down.shape),
                  _whole(w_up.shape), row(D), _whole((1, D))],
        out_specs=[row(2 * DFF), row(D), _acc((3, 2 * DFF)), _acc((1, 2 * DFF)), _acc((1, D))],
        out_shape=[jax.ShapeDtypeStruct((T, 2 * DFF), bf16), jax.ShapeDtypeStruct((T, D), f32),
                   jax.ShapeDtypeStruct((3, 2 * DFF), f32), jax.ShapeDtypeStruct((1, 2 * DFF), f32),
                   jax.ShapeDtypeStruct((1, D), f32)],
        scratch_shapes=[pltpu.VMEM((8, 2 * DFF), f32)],
        compiler_params=_params(),
    )(dx2, a, a, conv_w, conv_b, w_down, w_up, x1, g_ffn)


def _mix_bwd(dx1, hg_o, s5_o, gates, w_bhg, w_bs5, w_out):
    T = dx1.shape[0]
    tm = 256

    def body(dx1_ref, hg_ref, s5_ref, gt_ref, wh_ref, ws_ref, wo_ref, dgt_ref, dhg_ref, ds5_ref, dyh_ref, dys_ref):
        dm = _dot_nt(dx1_ref[...], wo_ref[...])
        yh = jnp.dot(hg_ref[...], wh_ref[...], preferred_element_type=f32)
        ys = jnp.dot(s5_ref[...], ws_ref[...], preferred_element_type=f32)
        sh = _sig(gt_ref[:, 0:D])
        ss = _sig(gt_ref[:, D:2 * D])
        dgt_ref[:, 0:D] = dm * yh * sh * (1.0 - sh)
        dgt_ref[:, D:2 * D] = dm * ys * ss * (1.0 - ss)
        dyh = (dm * sh).astype(bf16)
        dys = (dm * ss).astype(bf16)
        dyh_ref[...] = dyh
        dys_ref[...] = dys
        dhg_ref[...] = lax.dot_general(dyh, wh_ref[...], (((1,), (1,)), ((), ())), preferred_element_type=f32)
        ds5_ref[...] = lax.dot_general(dys, ws_ref[...], (((1,), (1,)), ((), ())), preferred_element_type=f32)

    row = lambda n: pl.BlockSpec((tm, n), lambda i: (i, 0))
    return _pc(
        body, name="mix_bwd", grid=(T // tm,),
        in_specs=[row(D), row(HW), row(SW), row(2 * D), _whole(w_bhg.shape), _whole(w_bs5.shape), _whole(w_out.shape)],
        out_specs=[row(2 * D), row(HW), row(SW), row(D), row(D)],
        out_shape=[jax.ShapeDtypeStruct((T, 2 * D), f32), jax.ShapeDtypeStruct((T, HW), f32), jax.ShapeDtypeStruct((T, SW), f32),
                   jax.ShapeDtypeStruct((T, D), bf16), jax.ShapeDtypeStruct((T, D), bf16)],
        compiler_params=_params(),
    )(dx1, hg_o, s5_o, gates, w_bhg, w_bs5, w_out)


def _in_bwd(d_hg, d_u, d_gt, x, dx1, w, g):
    T = x.shape[0]
    tm = 256

    def body(dhg_ref, du_ref, dgt_ref, x_ref, dx1_ref, w_ref, g_ref, dx_ref, dg_ref):
        @pl.when(pl.program_id(0) == 0)
        def _():
            dg_ref[...] = jnp.zeros_like(dg_ref)

        dh = (_dot_nt(dhg_ref[...], w_ref[:, 0:4 * HW]) + _dot_nt(du_ref[...], w_ref[:, 4 * HW:4 * HW + SW])
              + _dot_nt(dgt_ref[...], w_ref[:, 4 * HW + SW:]))
        xh, r = _rms(x_ref[...])
        dxn, dg_rows = _rms_bwd(dh, xh, r, g_ref[...])
        dg_ref[...] += _colsum(dg_rows)
        dx_ref[...] = dx1_ref[...] + dxn

    row = lambda n: pl.BlockSpec((tm, n), lambda i: (i, 0))
    return _pc(
        body, name="in_bwd", grid=(T // tm,),
        in_specs=[row(4 * HW), row(SW), row(2 * D), row(D), row(D), _whole(w.shape), _whole((1, D))],
        out_specs=[row(D), _acc((1, D))],
        out_shape=[jax.ShapeDtypeStruct((T, D), f32), jax.ShapeDtypeStruct((1, D), f32)],
        compiler_params=_params(),
    )(d_hg, d_u, d_gt, x, dx1, w, g)


def _wgrad(name, a, b, nj=None, a_blk=None, a_idx=None, b_blk=None, b_idx=None):
    T = a.shape[0]
    tm = 512
    dense = nj is None
    if dense:
        K, N = a.shape[1], b.shape[1]
        a_blk, a_idx = K, (lambda j: 0)
        b_blk = N
        while K * b_blk * 4 > 6 * 1024 * 1024 and b_blk % 256 == 0:
            b_blk //= 2
        nj, b_idx = N // b_blk, (lambda j: j)

    def body(a_ref, b_ref, o_ref):
        @pl.when(pl.program_id(1) == 0)
        def _():
            o_ref[...] = jnp.zeros_like(o_ref)

        o_ref[0] += _dot_tn(a_ref[...], b_ref[...])

    out = _pc(
        body, name=name, grid=(nj, T // tm),
        in_specs=[pl.BlockSpec((tm, a_blk), lambda j, i: (i, a_idx(j))), pl.BlockSpec((tm, b_blk), lambda j, i: (i, b_idx(j)))],
        out_specs=pl.BlockSpec((1, a_blk, b_blk), lambda j, i: (j, 0, 0)),
        out_shape=jax.ShapeDtypeStruct((nj, a_blk, b_blk), f32),
        compiler_params=_params(2),
    )(a, b)
    if dense:
        return out[0] if nj == 1 else jnp.transpose(out, (1, 0, 2)).reshape(a.shape[1], b.shape[1])
    return out


ANY = pl.BlockSpec(memory_space=pl.ANY)


def _all_gather(name, shard):
    R, C = shard.shape

    def body(x_ref, out_ref, send_sems, recv_sems, local_sem):
        x, y, c = lax.axis_index("x"), lax.axis_index("y"), lax.axis_index("c")
        me, sibling = (x, y, c), (x, y, 1 - c)
        chips = [(1 - x, y), (x, 1 - y), (1 - x, 1 - y)]

        def slot(px, py, pc):
            return out_ref.at[4 * px + 2 * py + pc]

        def copy(k, block, to, src=None):
            return pltpu.make_async_remote_copy(
                src_ref=slot(*block) if src is None else src, dst_ref=slot(*block),
                send_sem=send_sems.at[k], recv_sem=recv_sems.at[k], device_id=to, device_id_type=MESH)

        mine = pltpu.make_async_copy(x_ref, slot(*me), local_sem)
        mine.start()
        first = [copy(0, me, sibling, src=x_ref)]
        first += [copy(1 + j, me, (*chip, c), src=x_ref) for j, chip in enumerate(chips)]
        for cp in first:
            cp.start()
        passed = [copy(4 + j, (*chip, c), sibling) for j, chip in enumerate(chips)]
        for j, chip in enumerate(chips):
            copy(1 + j, (*chip, c), me).wait_recv()
            passed[j].start()
        copy(0, sibling, me).wait_recv()
        for j, chip in enumerate(chips):
            copy(4 + j, (*chip, 1 - c), me).wait_recv()
        for cp in first + passed:
            cp.wait_send()
        mine.wait()

    return _pc(
        body, name=name, in_specs=[ANY], out_specs=ANY,
        out_shape=jax.ShapeDtypeStruct((N_DEV, R, C), shard.dtype),
        scratch_shapes=[pltpu.SemaphoreType.DMA((7,)), pltpu.SemaphoreType.DMA((7,)), pltpu.SemaphoreType.DMA],
    )(shard)


def _swap_sibling(g2):
    _, nchip, R, C = g2.shape

    def body(g_ref, recv_ref, send_sem, recv_sem):
        x, y, c = lax.axis_index("x"), lax.axis_index("y"), lax.axis_index("c")
        cp = pltpu.make_async_remote_copy(src_ref=g_ref.at[1 - c], dst_ref=recv_ref, send_sem=send_sem, recv_sem=recv_sem,
                                          device_id=(x, y, 1 - c), device_id_type=MESH)
        cp.start()
        cp.wait()

    return _pc(
        body, name="rs_sibling", in_specs=[ANY], out_specs=ANY,
        out_shape=jax.ShapeDtypeStruct((nchip, R, C), g2.dtype),
        scratch_shapes=[pltpu.SemaphoreType.DMA, pltpu.SemaphoreType.DMA],
    )(g2)


def _swap_chips(p4):
    _, R, C = p4.shape

    def body(p_ref, recv_ref, send_sems, recv_sems):
        x, y, c = lax.axis_index("x"), lax.axis_index("y"), lax.axis_index("c")
        chips = [(1 - x, y), (x, 1 - y), (1 - x, 1 - y)]
        cps = [pltpu.make_async_remote_copy(src_ref=p_ref.at[2 * px + py], dst_ref=recv_ref.at[k], send_sem=send_sems.at[k],
                                            recv_sem=recv_sems.at[k], device_id=(px, py, c), device_id_type=MESH)
               for k, (px, py) in enumerate(chips)]
        for cp in cps:
            cp.start()
        for cp in cps:
            cp.wait()

    return _pc(
        body, name="rs_chips", in_specs=[ANY], out_specs=ANY,
        out_shape=jax.ShapeDtypeStruct((3, R, C), p4.dtype),
        scratch_shapes=[pltpu.SemaphoreType.DMA((3,)), pltpu.SemaphoreType.DMA((3,))],
    )(p4)


def _add_halves(mine, got):
    n, R, C = mine.shape
    tr = 8
    for cand in (512, 256, 128, 64, 32, 16):
        if R % cand == 0:
            tr = cand
            break

    def body(a_ref, b_ref, o_ref):
        o_ref[...] = a_ref[...] + b_ref[...]

    blk = pl.BlockSpec((1, tr, C), lambda k, i: (k, i, 0))
    return _pc(body, name="rs_add", grid=(n, R // tr), in_specs=[blk, blk], out_specs=blk,
               out_shape=jax.ShapeDtypeStruct(mine.shape, mine.dtype), compiler_params=_params(2))(mine, got)


def _adam_math(g, w, m, v):
    m2 = ADAM_B1 * m + (1.0 - ADAM_B1) * g
    v2 = ADAM_B2 * v + (1.0 - ADAM_B2) * (g * g)
    m_hat = m2 / (1.0 - ADAM_B1 ** ADAM_STEP)
    v_hat = v2 / (1.0 - ADAM_B2 ** ADAM_STEP)
    delta = -ADAM_LR * (m_hat / (jnp.sqrt(v_hat) + ADAM_EPS) + ADAM_WD * w)
    return delta, m2, v2


def _adam_sum(name, parts, w, m, v):
    R, C = w.shape
    tr = 8
    for cand in (256, 128, 64, 32, 16):
        if R % cand == 0:
            tr = cand
            break
    counts = [p.shape[0] for p in parts]

    def body(*refs):
        p_refs = refs[:len(parts)]
        w_ref, m_ref, v_ref, g_ref, d_ref, m2_ref, v2_ref = refs[len(parts):]
        g = None
        for p_ref, n in zip(p_refs, counts):
            for k in range(n):
                g = p_ref[k] if g is None else g + p_ref[k]
        g_ref[...] = g
        delta, m2, v2 = _adam_math(g, w_ref[...], m_ref[...], v_ref[...])
        d_ref[...] = delta
        m2_ref[...] = m2
        v2_ref[...] = v2

    blk = pl.BlockSpec((tr, C), lambda i: (i, 0))
    out = jax.ShapeDtypeStruct((R, C), f32)
    return _pc(
        body, name=name, grid=(R // tr,),
        in_specs=[pl.BlockSpec((n, tr, C), lambda i: (0, i, 0)) for n in counts] + [blk, blk, blk],
        out_specs=[blk, blk, blk, blk], out_shape=[out, out, out, out], compiler_params=_params(),
    )(*parts, w, m, v)


def _pack(arrs, dtype, row_mult):
    rows = []
    for a in arrs:
        flat = a.reshape(-1).astype(dtype)
        pad = (-flat.shape[0]) % LANES
        if pad:
            flat = jnp.concatenate([flat, jnp.zeros((pad,), dtype)])
        rows.append(flat.reshape(-1, LANES))
    out = jnp.concatenate(rows, axis=0)
    pad = (-out.shape[0]) % row_mult
    if pad:
        out = jnp.concatenate([out, jnp.zeros((pad, LANES), dtype)], axis=0)
    return out


def _unpack(buf, shapes):
    lead = buf.shape[:-2]
    outs, r = [], 0
    for shp in shapes:
        n = math.prod(shp)
        nr = -(-n // LANES)
        piece = buf[..., r:r + nr, :].reshape(*lead, nr * LANES)[..., :n]
        outs.append(piece.reshape(*lead, *shp))
        r += nr
    return outs


def _to_slabs(full, axis):
    shp = full.shape
    n = shp[axis] // N_DEV
    return jnp.moveaxis(full.reshape(*shp[:axis], N_DEV, n, *shp[axis + 1:]), axis, 0)


def _from_slabs(slabs, axis):
    t = jnp.moveaxis(slabs, 0, axis)
    shp = t.shape
    return t.reshape(*shp[:axis], shp[axis] * shp[axis + 1], *shp[axis + 2:])


def _s5_discretise(lam_re, lam_im, log_dt, b_re, b_im):
    dt = jnp.exp(log_dt)[:, None]
    mag = jnp.exp(lam_re * dt)
    a_re = mag * jnp.cos(lam_im * dt)
    a_im = mag * jnp.sin(lam_im * dt)
    den = lam_re * lam_re + lam_im * lam_im
    coef_re = ((a_re - 1.0) * lam_re + a_im * lam_im) / den
    coef_im = (a_im * lam_re - (a_re - 1.0) * lam_im) / den
    bbar_re = coef_re[..., None] * b_re - coef_im[..., None] * b_im
    bbar_im = coef_re[..., None] * b_im + coef_im[..., None] * b_re
    return a_re, a_im, bbar_re, bbar_im


def _s5_operands(bbar_re, bbar_im, c_re, c_im):
    eye = jnp.eye(SG // NST, dtype=f32)

    def b_op(bb):
        return jnp.einsum("sgnq,gh->sgqhn", bb.reshape(NST, SG // NST, SN, SP), eye).reshape(NST, 128, STW)

    def c_op(cc):
        return jnp.einsum("sgpn,gh->shngp", cc.reshape(NST, SG // NST, SP, SN), eye).reshape(NST, STW, 128)

    bdb = jnp.concatenate([b_op(bbar_re), b_op(bbar_im)], axis=0)
    bdc = jnp.concatenate([c_op(c_re), c_op(-c_im)], axis=0)
    return bdb, bdc


def _s5_powers(a_re, a_im):
    ar, ai = a_re.reshape(1, SL), a_im.reshape(1, SL)
    pr, pi = [ar], [ai]
    for _ in range(7):
        pr, pi = pr + [pr[-1] * ar - pi[-1] * ai], pi + [pr[-1] * ai + pi[-1] * ar]
    return jnp.concatenate([jnp.concatenate(pr, axis=0), jnp.concatenate(pi, axis=0)], axis=1)


_BIG = ["w_in", "s5_glu_w", "w_branch_hg", "w_branch_s5", "w_out", "w_up", "w_down", "w_ple_gate", "w_ple_proj", "conv_w"]
_BIG_AXIS = {"w_in": 1, "s5_glu_w": 0, "w_branch_hg": 1, "w_branch_s5": 1, "w_out": 0, "w_up": 1, "w_down": 0,
             "w_ple_gate": 0, "w_ple_proj": 1, "conv_w": 1}
_SMALL = ["norm_mix_g", "hg_lb_logits", "hg_norm_g", "s5_lambda_re", "s5_lambda_im", "s5_log_dt", "s5_b_re", "s5_b_im",
          "s5_c_re", "s5_c_im", "s5_d", "s5_glu_b", "norm_ffn_g", "conv_b", "norm_ple_g", "norm_final_g"]
_ORDER = ["norm_mix_g", "w_in", "hg_lb_logits", "hg_norm_g", "s5_lambda_re", "s5_lambda_im", "s5_log_dt", "s5_b_re",
          "s5_b_im", "s5_c_re", "s5_c_im", "s5_d", "s5_glu_w", "s5_glu_b", "w_branch_hg", "w_branch_s5", "w_out",
          "norm_ffn_g", "w_up", "conv_w", "conv_b", "w_down", "norm_ple_g", "w_ple_gate", "w_ple_proj", "norm_final_g"]


def kernel(x, p, norm_mix_g, w_in, hg_lb_logits, hg_norm_g, s5_lambda_re, s5_lambda_im, s5_log_dt, s5_b_re, s5_b_im, s5_c_re, s5_c_im, s5_d, s5_glu_w, s5_glu_b, w_branch_hg, w_branch_s5, w_out, norm_ffn_g, w_up, conv_w, conv_b, w_down, norm_ple_g, w_ple_gate, w_ple_proj, norm_final_g, loss_target, m_norm_mix_g, m_w_in, m_hg_lb_logits, m_hg_norm_g, m_s5_lambda_re, m_s5_lambda_im, m_s5_log_dt, m_s5_b_re, m_s5_b_im, m_s5_c_re, m_s5_c_im, m_s5_d, m_s5_glu_w, m_s5_glu_b, m_w_branch_hg, m_w_branch_s5, m_w_out, m_norm_ffn_g, m_w_up, m_conv_w, m_conv_b, m_w_down, m_norm_ple_g, m_w_ple_gate, m_w_ple_proj, m_norm_final_g, v_norm_mix_g, v_w_in, v_hg_lb_logits, v_hg_norm_g, v_s5_lambda_re, v_s5_lambda_im, v_s5_log_dt, v_s5_b_re, v_s5_b_im, v_s5_c_re, v_s5_c_im, v_s5_d, v_s5_glu_w, v_s5_glu_b, v_w_branch_hg, v_w_branch_s5, v_w_out, v_norm_ffn_g, v_w_up, v_conv_w, v_conv_b, v_w_down, v_norm_ple_g, v_w_ple_gate, v_w_ple_proj, v_norm_final_g):
    W = dict(norm_mix_g=norm_mix_g, w_in=w_in, hg_lb_logits=hg_lb_logits, hg_norm_g=hg_norm_g, s5_lambda_re=s5_lambda_re, s5_lambda_im=s5_lambda_im, s5_log_dt=s5_log_dt, s5_b_re=s5_b_re, s5_b_im=s5_b_im, s5_c_re=s5_c_re, s5_c_im=s5_c_im, s5_d=s5_d, s5_glu_w=s5_glu_w, s5_glu_b=s5_glu_b, w_branch_hg=w_branch_hg, w_branch_s5=w_branch_s5, w_out=w_out, norm_ffn_g=norm_ffn_g, w_up=w_up, conv_w=conv_w, conv_b=conv_b, w_down=w_down, norm_ple_g=norm_ple_g, w_ple_gate=w_ple_gate, w_ple_proj=w_ple_proj, norm_final_g=norm_final_g)
    M = dict(norm_mix_g=m_norm_mix_g, w_in=m_w_in, hg_lb_logits=m_hg_lb_logits, hg_norm_g=m_hg_norm_g, s5_lambda_re=m_s5_lambda_re, s5_lambda_im=m_s5_lambda_im, s5_log_dt=m_s5_log_dt, s5_b_re=m_s5_b_re, s5_b_im=m_s5_b_im, s5_c_re=m_s5_c_re, s5_c_im=m_s5_c_im, s5_d=m_s5_d, s5_glu_w=m_s5_glu_w, s5_glu_b=m_s5_glu_b, w_branch_hg=m_w_branch_hg, w_branch_s5=m_w_branch_s5, w_out=m_w_out, norm_ffn_g=m_norm_ffn_g, w_up=m_w_up, conv_w=m_conv_w, conv_b=m_conv_b, w_down=m_w_down, norm_ple_g=m_norm_ple_g, w_ple_gate=m_w_ple_gate, w_ple_proj=m_w_ple_proj, norm_final_g=m_norm_final_g)
    V = dict(norm_mix_g=v_norm_mix_g, w_in=v_w_in, hg_lb_logits=v_hg_lb_logits, hg_norm_g=v_hg_norm_g, s5_lambda_re=v_s5_lambda_re, s5_lambda_im=v_s5_lambda_im, s5_log_dt=v_s5_log_dt, s5_b_re=v_s5_b_re, s5_b_im=v_s5_b_im, s5_c_re=v_s5_c_re, s5_c_im=v_s5_c_im, s5_d=v_s5_d, s5_glu_w=v_s5_glu_w, s5_glu_b=v_s5_glu_b, w_branch_hg=v_w_branch_hg, w_branch_s5=v_w_branch_s5, w_out=v_w_out, norm_ffn_g=v_norm_ffn_g, w_up=v_w_up, conv_w=v_conv_w, conv_b=v_conv_b, w_down=v_w_down, norm_ple_g=v_norm_ple_g, w_ple_gate=v_w_ple_gate, w_ple_proj=v_w_ple_proj, norm_final_g=v_norm_final_g)

    mm_names = _BIG[:-1]
    shard2 = {n: W[n][0] for n in _BIG}
    conv_bits = lax.bitcast_convert_type(shard2["conv_w"], bf16)
    wpack = _pack([shard2[n] for n in mm_names] + [conv_bits], bf16, 16)
    gathered = _all_gather("ag_weights", wpack)
    pieces = _unpack(gathered, [shard2[n].shape for n in mm_names] + [conv_bits.shape])
    full = {n: _from_slabs(pc, _BIG_AXIS[n]) for n, pc in zip(mm_names, pieces[:-1])}
    conv_w_full = _from_slabs(lax.bitcast_convert_type(pieces[-1], f32), 1)

    xt = x[0]
    pt = p[0, 0]
    tgt = loss_target[0]
    T = xt.shape[0]
    lam_re, lam_im, log_dt = s5_lambda_re[0], s5_lambda_im[0], s5_log_dt[0]
    b_re, b_im, c_re, c_im = s5_b_re[0], s5_b_im[0], s5_c_re[0], s5_c_im[0]

    def s5_prep(lam_re, lam_im, log_dt, b_re, b_im, c_re, c_im):
        a_re, a_im, bbar_re, bbar_im = _s5_discretise(lam_re, lam_im, log_dt, b_re, b_im)
        bdb, bdc = _s5_operands(bbar_re, bbar_im, c_re, c_im)
        return a_re, a_im, bdb, bdc

    (a_re, a_im, bdb, bdc), s5_prep_vjp = jax.vjp(s5_prep, lam_re, lam_im, log_dt, b_re, b_im, c_re, c_im)
    pw = _s5_powers(a_re, a_im)
    pw_rev = pw[::-1]
    bdb_b, bdc_b = bdb.astype(bf16), bdc.astype(bf16)

    h1, proj_hg, u_raw, gates = _in_proj(xt, norm_mix_g, full["w_in"])
    hg_o, sprev = _hgrn_fwd(proj_hg, hg_lb_logits, hg_norm_g)
    x_st, y_s5, g_s5, s5_o = _s5_fwd(u_raw, bdb_b, bdc_b, pw, s5_d, full["s5_glu_w"], s5_glu_b)
    x1, merged, h2, a_up = _mix_up(xt, hg_o, s5_o, gates, full["w_branch_hg"], full["w_branch_s5"], full["w_out"],
                                   norm_ffn_g, full["w_up"])
    (dx2, gated, h3, dz_ple, dpp, loss_part, d_norm_final, d_norm_ple) = _ffn_tail(
        a_up, conv_w_full, conv_b, full["w_down"], x1, pt, norm_ple_g, full["w_ple_gate"], full["w_ple_proj"],
        norm_final_g.reshape(1, D), tgt)

    da_up, dx1, d_conv_w, d_conv_b, d_norm_ffn = _ffn_bwd(dx2, a_up, conv_w_full, conv_b, full["w_down"], full["w_up"],
                                                           x1, norm_ffn_g)
    d_gates, d_hg_o, d_s5_o, dyh, dys = _mix_bwd(dx1, hg_o, s5_o, gates, full["w_branch_hg"], full["w_branch_s5"],
                                                   full["w_out"])
    d_proj_hg, d_lb, d_hg_norm = _hgrn_bwd(proj_hg, hg_lb_logits, hg_norm_g, sprev, d_hg_o)
    d_u, lam_st, dy_s5, dz_glu, d_a_re, d_a_im, d_s5_d, d_glu_b = _s5_bwd(
        d_s5_o, y_s5, u_raw, x_st, bdb_b, bdc_b, pw_rev, s5_d, full["s5_glu_w"], s5_glu_b)
    grad_x, d_norm_mix = _in_bwd(d_proj_hg, d_u, d_gates, xt, dx1, full["w_in"], norm_mix_g)

    gw = {}
    gw["w_in"] = jnp.concatenate([_wgrad("wg_in_hg", h1, d_proj_hg), _wgrad("wg_in_u", h1, d_u),
                                  _wgrad("wg_in_gates", h1, d_gates)], axis=1)
    gw["s5_glu_w"] = _wgrad("wg_glu", g_s5, dz_glu)
    gw["w_branch_hg"] = _wgrad("wg_bhg", hg_o, dyh)
    gw["w_branch_s5"] = _wgrad("wg_bs5", s5_o, dys)
    gw["w_out"] = _wgrad("wg_out", merged, dx1)
    gw["w_up"] = _wgrad("wg_up", h2, da_up)
    gw["w_down"] = _wgrad("wg_down", gated, dx2)
    gw["w_ple_gate"] = _wgrad("wg_pg", h3, dz_ple)
    gw["w_ple_proj"] = _wgrad("wg_pp", pt, dpp)
    gw["conv_w"] = d_conv_w
    d_bdb = _wgrad("wg_s5b", u_raw, lam_st, nj=2 * NST, a_blk=128, a_idx=lambda j: j % NST, b_blk=STW, b_idx=lambda j: j)
    d_bdc = _wgrad("wg_s5c", x_st, dy_s5, nj=2 * NST, a_blk=STW, a_idx=lambda j: j, b_blk=128, b_idx=lambda j: j % NST)
    (d_lam_re, d_lam_im, d_log_dt, d_b_re, d_b_im, d_c_re, d_c_im) = s5_prep_vjp(
        (d_a_re.reshape(SG, SN), d_a_im.reshape(SG, SN), d_bdb, d_bdc))
    sm = jax.nn.softmax(hg_lb_logits, axis=0)
    d_l0 = d_lb[0] * sm[0] * sm[1]
    d_logits = jnp.stack([d_l0, -d_l0], axis=0)

    gs = {"norm_mix_g": d_norm_mix, "hg_lb_logits": d_logits, "hg_norm_g": d_hg_norm, "s5_lambda_re": d_lam_re,
          "s5_lambda_im": d_lam_im, "s5_log_dt": d_log_dt, "s5_b_re": d_b_re, "s5_b_im": d_b_im, "s5_c_re": d_c_re,
          "s5_c_im": d_c_im, "s5_d": d_s5_d, "s5_glu_b": d_glu_b, "norm_ffn_g": d_norm_ffn, "conv_b": d_conv_b,
          "norm_ple_g": d_norm_ple, "norm_final_g": d_norm_final}

    gpack = jnp.stack([_pack([_to_slabs(gw[n], _BIG_AXIS[n])[j] for n in _BIG], f32, 128) for j in range(N_DEV)])
    rows = gpack.shape[1]
    g2 = jnp.transpose(gpack.reshape(4, 2, rows, LANES), (1, 0, 2, 3))
    my_c = lax.axis_index("c")
    my_chip = 2 * lax.axis_index("x") + lax.axis_index("y")
    got = _swap_sibling(g2)
    p4 = _add_halves(lax.dynamic_index_in_dim(g2, my_c, 0, keepdims=False), got)
    got3 = _swap_chips(p4)
    mine = lax.dynamic_index_in_dim(p4, my_chip, 0, keepdims=True)
    big_shapes = [shard2[n].shape for n in _BIG]
    wp = _pack([shard2[n] for n in _BIG], f32, 128)
    mp = _pack([M[n][0] for n in _BIG], f32, 128)
    vp = _pack([V[n][0] for n in _BIG], f32, 128)
    big_out = [_unpack(o, big_shapes) for o in _adam_sum("adam_sharded", [mine, got3], wp, mp, vp)]

    small_shapes = [W[n].shape for n in _SMALL]
    spack = _pack([gs[n].reshape(W[n].shape) for n in _SMALL], f32, 128)
    sparts = _all_gather("ag_small_grads", spack)
    swp = _pack([W[n] for n in _SMALL], f32, 128)
    smp = _pack([M[n] for n in _SMALL], f32, 128)
    svp = _pack([V[n] for n in _SMALL], f32, 128)
    small_out = [_unpack(o, small_shapes) for o in _adam_sum("adam_replicated", [sparts], swp, smp, svp)]

    res = {}
    for k in range(4):
        d = {n: big_out[k][i][None] for i, n in enumerate(_BIG)}
        d.update({n: small_out[k][i] for i, n in enumerate(_SMALL)})
        res[k] = d
    loss = lax.psum(loss_part[0, 0], ("x", "y", "c"))
    return (loss, grad_x[None], *[res[0][n] for n in _ORDER], *[res[1][n] for n in _ORDER],
            *[res[2][n] for n in _ORDER], *[res[3][n] for n in _ORDER])
```

```python
import functools
import math

import jax
import jax.numpy as jnp
from jax import lax
from jax.experimental import pallas as pl
from jax.experimental.pallas import tpu as pltpu

f32 = jnp.float32
bf16 = jnp.bfloat16
HIGHEST = lax.Precision.HIGHEST
MESH = pl.DeviceIdType.MESH

N_DEV = 8
D = 1024
HW = 512
HD = 128
NH = 4
CH = 64
SW = 512
SG = 32
SP = 16
SN = 64
SL = SG * SN
NST = 4
STW = SL // NST
DFF = 2816
PLE = 256
EPS = 1e-6
LANES = 1024
VMEM_LIMIT = 56 * 1024 * 1024

ADAM_LR, ADAM_B1, ADAM_B2, ADAM_EPS, ADAM_WD, ADAM_STEP = 0.001, 0.9, 0.999, 1e-08, 0.01, 10


def _pc(body, **kw):
    return pl.pallas_call(body, **kw)


def _params(n_axes=1, **kw):
    return pltpu.CompilerParams(dimension_semantics=("arbitrary",) * n_axes, vmem_limit_bytes=VMEM_LIMIT, **kw)


def _whole(shape):
    nd = len(shape)
    return pl.BlockSpec(shape, lambda *_: (0,) * nd, pipeline_mode=pl.Buffered(1))


def _acc(shape):
    nd = len(shape)
    return pl.BlockSpec(shape, lambda *_: (0,) * nd)


def _dot(a, b):
    return jnp.dot(a.astype(bf16), b.astype(bf16), preferred_element_type=f32)


def _dot_nt(a, b):
    return lax.dot_general(a.astype(bf16), b.astype(bf16), (((1,), (1,)), ((), ())), preferred_element_type=f32)


def _dot_tn(a, b):
    return lax.dot_general(a.astype(bf16), b.astype(bf16), (((0,), (0,)), ((), ())), preferred_element_type=f32)


def _sig(x):
    return jax.nn.sigmoid(x)


def _dsilu(z, s):
    return s * (1.0 + z * (1.0 - s))


_GC = math.sqrt(2.0 / math.pi)


def _gelu_and_grad(y):
    t = jnp.tanh(_GC * (y + 0.044715 * y * y * y))
    g = 0.5 * y * (1.0 + t)
    dg = 0.5 * (1.0 + t) + 0.5 * y * (1.0 - t * t) * _GC * (1.0 + 3.0 * 0.044715 * y * y)
    return g, dg


def _rms(x):
    r = lax.rsqrt(jnp.mean(x * x, axis=-1, keepdims=True) + EPS)
    return x * r, r


def _rms_bwd(dy, xh, r, g):
    dxh = dy * g
    dx = r * (dxh - xh * jnp.mean(dxh * xh, axis=-1, keepdims=True))
    return dx, dy * xh


def _colsum(x):
    return jnp.sum(x, axis=0, keepdims=True)


def _in_proj(x, g, w):
    T = x.shape[0]
    tm = 256

    def body(x_ref, g_ref, w_ref, h_ref, hg_ref, u_ref, gt_ref):
        xh, _ = _rms(x_ref[...])
        h = (xh * g_ref[...]).astype(bf16)
        h_ref[...] = h
        hg_ref[...] = jnp.dot(h, w_ref[:, 0:4 * HW], preferred_element_type=f32)
        u_ref[...] = jnp.dot(h, w_ref[:, 4 * HW:4 * HW + SW], preferred_element_type=f32)
        gt_ref[...] = jnp.dot(h, w_ref[:, 4 * HW + SW:], preferred_element_type=f32)

    row = lambda n: pl.BlockSpec((tm, n), lambda i: (i, 0))
    return _pc(
        body, name="in_proj", grid=(T // tm,),
        in_specs=[row(D), _whole((1, D)), _whole(w.shape)],
        out_specs=[row(D), row(4 * HW), row(SW), row(2 * D)],
        out_shape=[jax.ShapeDtypeStruct((T, D), bf16), jax.ShapeDtypeStruct((T, 4 * HW), f32),
                   jax.ShapeDtypeStruct((T, SW), f32), jax.ShapeDtypeStruct((T, 2 * D), f32)],
        compiler_params=_params(),
    )(x, g, w)


def _hgrn_common(lg_ref, q_ref, f_ref, h):
    sl = slice(h * HD, (h + 1) * HD)
    lg = lg_ref[...]
    mx = jnp.max(lg, axis=0, keepdims=True)
    e = jnp.exp(lg - mx)
    lb = (e[0:1, :] / (e[0:1, :] + e[1:2, :]))[:, sl]
    qr = q_ref[:, sl]
    fr = f_ref[:, sl]
    sig = _sig(fr)
    f = lb + (1.0 - lb) * sig
    k = 1.0 - f
    r_i = lax.broadcasted_iota(jnp.int32, (CH, CH), 0)
    c_i = lax.broadcasted_iota(jnp.int32, (CH, CH), 1)
    tril = (r_i >= c_i)
    b = jnp.dot(tril.astype(f32), jnp.log(f), precision=HIGHEST, preferred_element_type=f32)
    bref = b[CH // 2:CH // 2 + 1, :]
    blast = b[CH - 1:CH, :]
    sq = _sig(qr)
    q = qr * sq
    e1 = jnp.exp(b - bref)
    e2 = jnp.exp(bref - b)
    e3 = jnp.exp(blast - b)
    e4 = jnp.exp(b)
    return dict(sl=sl, lb=lb, qr=qr, sq=sq, sig=sig, f=f, k=k, tril=tril, q=q, e1=e1, e2=e2, e3=e3, e4=e4,
                qs=q * e1, ks=k * e2, kl=k * e3, qb=q * e4, dec=jnp.exp(blast))


def _hgrn_fwd(proj_hg, logits, ng):
    T = proj_hg.shape[0]
    nch = T // CH

    def body(q_ref, f_ref, i_ref, og_ref, lg_ref, ng_ref, out_ref, sprev_ref, st_ref):
        @pl.when(pl.program_id(0) == 0)
        def _():
            st_ref[...] = jnp.zeros_like(st_ref)

        for h in range(NH):
            c = _hgrn_common(lg_ref, q_ref, f_ref, h)
            sl = c["sl"]
            v = i_ref[:, sl]
            og = og_ref[:, sl]
            p = jnp.where(c["tril"], _dot_nt(c["qs"], c["ks"]), 0.0)
            st = st_ref[h]
            sprev_ref[0, h] = st
            o = _dot(p, v) + _dot_nt(c["qb"], st)
            st_ref[h] = c["dec"] * st + _dot_tn(v, c["kl"])
            oh, _ = _rms(o)
            out_ref[:, sl] = (oh * ng_ref[...] * (og * _sig(og))).astype(bf16)

    col = lambda j: pl.BlockSpec((CH, HW), lambda n, j=j: (n, j))
    return _pc(
        body, name="hgrn_fwd", grid=(nch,),
        in_specs=[col(0), col(1), col(2), col(3), _whole((2, HW)), _whole((1, HD))],
        out_specs=[pl.BlockSpec((CH, HW), lambda n: (n, 0)),
                   pl.BlockSpec((1, NH, HD, HD), lambda n: (n, 0, 0, 0))],
        out_shape=[jax.ShapeDtypeStruct((T, HW), bf16), jax.ShapeDtypeStruct((nch, NH, HD, HD), f32)],
        scratch_shapes=[pltpu.VMEM((NH, HD, HD), f32)],
        compiler_params=_params(),
    )(proj_hg, proj_hg, proj_hg, proj_hg, logits, ng)


def _hgrn_bwd(proj_hg, logits, ng, sprev, d_out):
    T = proj_hg.shape[0]
    nch = T // CH

    def body(q_ref, f_ref, i_ref, og_ref, lg_ref, ng_ref, sp_ref, do_ref, dp_ref, dlb_ref, dng_ref, gt_ref):
        @pl.when(pl.program_id(0) == 0)
        def _():
            gt_ref[...] = jnp.zeros_like(gt_ref)
            dlb_ref[...] = jnp.zeros_like(dlb_ref)
            dng_ref[...] = jnp.zeros_like(dng_ref)

        row = lax.broadcasted_iota(jnp.int32, (CH, HD), 0)
        ng_v = ng_ref[...]
        for h in range(NH):
            c = _hgrn_common(lg_ref, q_ref, f_ref, h)
            sl = c["sl"]
            tril = c["tril"]
            v = i_ref[:, sl]
            og = og_ref[:, sl]
            st = sp_ref[0, h]
            gt = gt_ref[h]
            p = jnp.where(tril, _dot_nt(c["qs"], c["ks"]), 0.0)
            o = _dot(p, v) + _dot_nt(c["qb"], st)
            oh, r = _rms(o)
            sog = _sig(og)
            d_o_gated = do_ref[:, sl]
            d_on = d_o_gated * (og * sog)
            dog = d_o_gated * (oh * ng_v) * _dsilu(og, sog)
            d_o, dng_rows = _rms_bwd(d_on, oh, r, ng_v)
            dng_ref[...] += _colsum(dng_rows)
            dqb = _dot(d_o, st)
            dst = _dot_tn(d_o, c["qb"])
            dp = jnp.where(tril, _dot_nt(d_o, v), 0.0)
            dv = _dot_tn(p, d_o) + _dot_nt(c["kl"], gt)
            dqs = _dot(dp, c["ks"])
            dks = _dot_tn(dp, c["qs"])
            dkl = _dot(v, gt)
            ddec = _colsum(gt * st)
            gt_ref[h] = dst + c["dec"] * gt
            dq = dqs * c["e1"] + dqb * c["e4"]
            dk = dks * c["e2"] + dkl * c["e3"]
            t_qs = dqs * c["qs"]
            t_ks = dks * c["ks"]
            t_kl = dkl * c["kl"]
            db = t_qs - t_ks - t_kl + dqb * c["qb"]
            dbref = _colsum(t_ks - t_qs)
            dblast = _colsum(t_kl) + ddec * c["dec"]
            db = db + jnp.where(row == CH // 2, dbref, 0.0) + jnp.where(row == CH - 1, dblast, 0.0)
            r_i = lax.broadcasted_iota(jnp.int32, (CH, CH), 0)
            c_i = lax.broadcasted_iota(jnp.int32, (CH, CH), 1)
            dlf = jnp.dot((c_i >= r_i).astype(f32), db, precision=HIGHEST, preferred_element_type=f32)
            df = dlf / c["f"] - dk
            sig = c["sig"]
            dfr = df * (1.0 - c["lb"]) * sig * (1.0 - sig)
            dlb_ref[:, sl] += _colsum(df * (1.0 - sig))
            dp_ref[:, h * HD:(h + 1) * HD] = dq * _dsilu(c["qr"], c["sq"])
            dp_ref[:, HW + h * HD:HW + (h + 1) * HD] = dfr
            dp_ref[:, 2 * HW + h * HD:2 * HW + (h + 1) * HD] = dv
            dp_ref[:, 3 * HW + h * HD:3 * HW + (h + 1) * HD] = dog

    rev = lambda n: nch - 1 - n
    col = lambda j: pl.BlockSpec((CH, HW), lambda n, j=j: (rev(n), j))
    return _pc(
        body, name="hgrn_bwd", grid=(nch,),
        in_specs=[col(0), col(1), col(2), col(3), _whole((2, HW)), _whole((1, HD)),
                  pl.BlockSpec((1, NH, HD, HD), lambda n: (rev(n), 0, 0, 0)),
                  pl.BlockSpec((CH, HW), lambda n: (rev(n), 0))],
        out_specs=[pl.BlockSpec((CH, 4 * HW), lambda n: (rev(n), 0)), _acc((1, HW)), _acc((1, HD))],
        out_shape=[jax.ShapeDtypeStruct((T, 4 * HW), f32), jax.ShapeDtypeStruct((1, HW), f32),
                   jax.ShapeDtypeStruct((1, HD), f32)],
        scratch_shapes=[pltpu.VMEM((NH, HD, HD), f32)],
        compiler_params=_params(),
    )(proj_hg, proj_hg, proj_hg, proj_hg, logits, ng, sprev, d_out)


S5_TM = 256
S5_LW = 512


def _scan_rows(xs_ref, pw_ref, carry_ref, tm, reverse):
    nblk = tm // 8
    row = lax.broadcasted_iota(jnp.int32, (8, S5_LW), 0)
    sgn = -1.0 if reverse else 1.0
    for ls in range(SL // S5_LW):
        lre = pl.ds(ls * S5_LW, S5_LW)
        lim = pl.ds(SL + ls * S5_LW, S5_LW)
        pr = pw_ref[:, lre]
        pi = pw_ref[:, lim] * sgn
        steps = []
        for d in (1, 2, 4):
            j = (8 - d) if reverse else (d - 1)
            steps.append((d, pr[j:j + 1, :], pi[j:j + 1, :]))
        crow = 0 if reverse else 7
        c0r = carry_ref[crow:crow + 1, lre]
        c0i = carry_ref[crow:crow + 1, lim]

        def blk(n, carry, lre=lre, lim=lim, pr=pr, pi=pi, steps=steps):
            cr, ci = carry
            r0 = pl.multiple_of(((nblk - 1 - n) if reverse else n) * 8, 8)
            xr = xs_ref[pl.ds(r0, 8), lre]
            xi = xs_ref[pl.ds(r0, 8), lim]
            for d, ar, ai in steps:
                if reverse:
                    sr = pltpu.roll(xr, 8 - d, 0)
                    si = pltpu.roll(xi, 8 - d, 0)
                    m = row < 8 - d
                else:
                    sr = pltpu.roll(xr, d, 0)
                    si = pltpu.roll(xi, d, 0)
                    m = row >= d
                xr = xr + jnp.where(m, ar * sr - ai * si, 0.0)
                xi = xi + jnp.where(m, ar * si + ai * sr, 0.0)
            xr = xr + pr * cr - pi * ci
            xi = xi + pr * ci + pi * cr
            xs_ref[pl.ds(r0, 8), lre] = xr
            xs_ref[pl.ds(r0, 8), lim] = xi
            return (xr[crow:crow + 1, :], xi[crow:crow + 1, :])

        lax.fori_loop(0, nblk, blk, (c0r, c0i))
    last = (0 if reverse else tm - 8)
    carry_ref[...] = xs_ref[last:last + 8, :]


def _s5_fwd(u, bdb, bdc, pw, dskip, glu_w, glu_b):
    T = u.shape[0]
    tm = S5_TM

    def body(u_ref, bdb_ref, bdc_ref, pw_ref, ds_ref, gw_ref, gb_ref, x_ref, y_ref, g_ref, o_ref, xs_ref, carry_ref):
        @pl.when(pl.program_id(0) == 0)
        def _():
            carry_ref[...] = jnp.zeros_like(carry_ref)

        uv = u_ref[...]
        ub = uv.astype(bf16)
        for part in range(2):
            for s in range(NST):
                xs_ref[:, part * SL + s * STW:part * SL + (s + 1) * STW] = jnp.dot(
                    ub[:, s * 128:(s + 1) * 128], bdb_ref[part * NST + s], preferred_element_type=f32)
        _scan_rows(xs_ref, pw_ref, carry_ref, tm, reverse=False)
        x_ref[...] = xs_ref[...]
        ys = []
        for s in range(NST):
            acc = None
            for part in range(2):
                xv = xs_ref[:, part * SL + s * STW:part * SL + (s + 1) * STW].astype(bf16)
                t = jnp.dot(xv, bdc_ref[part * NST + s], preferred_element_type=f32)
                acc = t if acc is None else acc + t
            ys.append(acc)
        y = jnp.concatenate(ys, axis=1) + ds_ref[...] * uv
        y_ref[...] = y
        g, _ = _gelu_and_grad(y)
        gb = g.astype(bf16)
        g_ref[...] = gb
        z = jnp.dot(gb, gw_ref[...], preferred_element_type=f32) + gb_ref[...]
        o_ref[...] = (g * _sig(z)).astype(bf16)

    row = lambda n: pl.BlockSpec((tm, n), lambda i: (i, 0))
    return _pc(
        body, name="s5_fwd", grid=(T // tm,),
        in_specs=[row(SW), _whole(bdb.shape), _whole(bdc.shape), _whole(pw.shape), _whole((1, SW)),
                  _whole((SW, SW)), _whole((1, SW))],
        out_specs=[row(2 * SL), row(SW), row(SW), row(SW)],
        out_shape=[jax.ShapeDtypeStruct((T, 2 * SL), f32), jax.ShapeDtypeStruct((T, SW), f32),
                   jax.ShapeDtypeStruct((T, SW), bf16), jax.ShapeDtypeStruct((T, SW), bf16)],
        scratch_shapes=[pltpu.VMEM((tm, 2 * SL), f32), pltpu.VMEM((8, 2 * SL), f32)],
        compiler_params=_params(),
    )(u, bdb, bdc, pw, dskip, glu_w, glu_b)


def _s5_bwd(d_out, y, u, x, bdb, bdc, pwr, dskip, glu_w, glu_b):
    T = u.shape[0]
    tm = S5_TM
    nt = T // tm

    def body(do_ref, y_ref, u_ref, x_ref, xh_ref, bdb_ref, bdc_ref, pw_ref, ds_ref, gw_ref, gb_ref,
             du_ref, lam_ref, dy_ref, dz_ref, dar_ref, dai_ref, dd_ref, dgb_ref, gs_ref, carry_ref):
        i = pl.program_id(0)

        @pl.when(i == 0)
        def _():
            carry_ref[...] = jnp.zeros_like(carry_ref)
            dar_ref[...] = jnp.zeros_like(dar_ref)
            dai_ref[...] = jnp.zeros_like(dai_ref)
            dd_ref[...] = jnp.zeros_like(dd_ref)
            dgb_ref[...] = jnp.zeros_like(dgb_ref)

        yv = y_ref[...]
        uv = u_ref[...]
        g, gp = _gelu_and_grad(yv)
        z = jnp.dot(g.astype(bf16), gw_ref[...], preferred_element_type=f32) + gb_ref[...]
        sg = _sig(z)
        do = do_ref[...].astype(f32)
        dz = do * g * sg * (1.0 - sg)
        dz_ref[...] = dz.astype(bf16)
        dgb_ref[...] += _colsum(dz)
        dy = (do * sg + _dot_nt(dz, gw_ref[...])) * gp
        dyb = dy.astype(bf16)
        dy_ref[...] = dyb
        dd_ref[...] += _colsum(dy * uv)
        for part in range(2):
            for s in range(NST):
                gs_ref[:, part * SL + s * STW:part * SL + (s + 1) * STW] = lax.dot_general(
                    dyb[:, s * 128:(s + 1) * 128], bdc_ref[part * NST + s], (((1,), (1,)), ((), ())),
                    preferred_element_type=f32)
        _scan_rows(gs_ref, pw_ref, carry_ref, tm, reverse=True)
        lam_ref[...] = gs_ref[...].astype(bf16)
        dus = []
        for s in range(NST):
            acc = None
            for part in range(2):
                lv = gs_ref[:, part * SL + s * STW:part * SL + (s + 1) * STW].astype(bf16)
                t = lax.dot_general(lv, bdb_ref[part * NST + s], (((1,), (1,)), ((), ())), preferred_element_type=f32)
                acc = t if acc is None else acc + t
            dus.append(acc)
        du_ref[...] = jnp.concatenate(dus, axis=1) + dy * ds_ref[...]
        first = (i == nt - 1)
        rowi = lax.broadcasted_iota(jnp.int32, (tm, S5_LW), 0)
        for ls in range(SL // S5_LW):
            lre = slice(ls * S5_LW, (ls + 1) * S5_LW)
            lim = slice(SL + ls * S5_LW, SL + (ls + 1) * S5_LW)
            hr = jnp.where(first, 0.0, xh_ref[7:8, lre])
            hi = jnp.where(first, 0.0, xh_ref[7:8, lim])
            xpr = jnp.where(rowi == 0, hr, pltpu.roll(x_ref[:, lre], 1, 0))
            xpi = jnp.where(rowi == 0, hi, pltpu.roll(x_ref[:, lim], 1, 0))
            lr = gs_ref[:, lre]
            li = gs_ref[:, lim]
            dar_ref[:, lre] += _colsum(lr * xpr + li * xpi)
            dai_ref[:, lre] += _colsum(li * xpr - lr * xpi)

    rev = lambda i: nt - 1 - i
    row = lambda n: pl.BlockSpec((tm, n), lambda i: (rev(i), 0))
    halo = pl.BlockSpec((8, 2 * SL), lambda i: (jnp.maximum(rev(i) * (tm // 8) - 1, 0), 0))
    return _pc(
        body, name="s5_bwd", grid=(nt,),
        in_specs=[row(SW), row(SW), row(SW), row(2 * SL), halo, _whole(bdb.shape), _whole(bdc.shape),
                  _whole(pwr.shape), _whole((1, SW)), _whole((SW, SW)), _whole((1, SW))],
        out_specs=[row(SW), row(2 * SL), row(SW), row(SW), _acc((1, SL)), _acc((1, SL)), _acc((1, SW)), _acc((1, SW))],
        out_shape=[jax.ShapeDtypeStruct((T, SW), f32), jax.ShapeDtypeStruct((T, 2 * SL), bf16),
                   jax.ShapeDtypeStruct((T, SW), bf16), jax.ShapeDtypeStruct((T, SW), bf16),
                   jax.ShapeDtypeStruct((1, SL), f32), jax.ShapeDtypeStruct((1, SL), f32),
                   jax.ShapeDtypeStruct((1, SW), f32), jax.ShapeDtypeStruct((1, SW), f32)],
        scratch_shapes=[pltpu.VMEM((tm, 2 * SL), f32), pltpu.VMEM((8, 2 * SL), f32)],
        compiler_params=_params(),
    )(d_out, y, u, x, x, bdb, bdc, pwr, dskip, glu_w, glu_b)


def _mix_up(x, hg_o, s5_o, gates, w_bhg, w_bs5, w_out, g_ffn, w_up):
    T = x.shape[0]
    tm = 256

    def body(x_ref, hg_ref, s5_ref, gt_ref, wh_ref, ws_ref, wo_ref, g_ref, wu_ref, x1_ref, mg_ref, h2_ref, a_ref):
        yh = jnp.dot(hg_ref[...], wh_ref[...], preferred_element_type=f32)
        ys = jnp.dot(s5_ref[...], ws_ref[...], preferred_element_type=f32)
        merged = (_sig(gt_ref[:, 0:D]) * yh + _sig(gt_ref[:, D:2 * D]) * ys).astype(bf16)
        mg_ref[...] = merged
        x1 = x_ref[...] + jnp.dot(merged, wo_ref[...], preferred_element_type=f32)
        x1_ref[...] = x1
        xh, _ = _rms(x1)
        h2 = (xh * g_ref[...]).astype(bf16)
        h2_ref[...] = h2
        a_ref[...] = jnp.dot(h2, wu_ref[...], preferred_element_type=f32)

    row = lambda n: pl.BlockSpec((tm, n), lambda i: (i, 0))
    return _pc(
        body, name="mix_up", grid=(T // tm,),
        in_specs=[row(D), row(HW), row(SW), row(2 * D), _whole(w_bhg.shape), _whole(w_bs5.shape), _whole(w_out.shape),
                  _whole((1, D)), _whole(w_up.shape)],
        out_specs=[row(D), row(D), row(D), row(2 * DFF)],
        out_shape=[jax.ShapeDtypeStruct((T, D), f32), jax.ShapeDtypeStruct((T, D), bf16),
                   jax.ShapeDtypeStruct((T, D), bf16), jax.ShapeDtypeStruct((T, 2 * DFF), f32)],
        compiler_params=_params(),
    )(x, hg_o, s5_o, gates, w_bhg, w_bs5, w_out, g_ffn, w_up)


FFN_TM = 128


def _conv_rows(a, halo, first, conv_w, conv_b):
    tm = a.shape[0]
    row = lax.broadcasted_iota(jnp.int32, (tm, 1), 0)
    hm1 = jnp.where(first, 0.0, halo[7:8, :])
    hm2 = jnp.where(first, 0.0, halo[6:7, :])
    a1 = jnp.where(row == 0, hm1, pltpu.roll(a, 1, 0))
    a2 = jnp.where(row == 0, hm2, jnp.where(row == 1, hm1, pltpu.roll(a, 2, 0)))
    c = conv_b + conv_w[0:1, :] * a2 + conv_w[1:2, :] * a1 + conv_w[2:3, :] * a
    return c, a1, a2


def _ffn_tail(a, conv_w, conv_b, w_down, x1, p, g_ple, w_pg, w_pp, g_fin, tgt):
    T = a.shape[0]
    tm = FFN_TM

    def body(a_ref, ah_ref, cw_ref, cb_ref, wd_ref, x1_ref, p_ref, gp_ref, wpg_ref, wpp_ref, gf_ref, t_ref,
             dx2_ref, gd_ref, h3_ref, dz_ref, dpp_ref, loss_ref, dgf_ref, dgp_ref):
        i = pl.program_id(0)

        @pl.when(i == 0)
        def _():
            loss_ref[...] = jnp.zeros_like(loss_ref)
            dgf_ref[...] = jnp.zeros_like(dgf_ref)
            dgp_ref[...] = jnp.zeros_like(dgp_ref)

        c, _, _ = _conv_rows(a_ref[...], ah_ref[...], i == 0, cw_ref[...], cb_ref[...])
        gl, _ = _gelu_and_grad(c[:, 0:DFF])
        gated = (gl * c[:, DFF:]).astype(bf16)
        gd_ref[...] = gated
        x2 = x1_ref[...] + jnp.dot(gated, wd_ref[...], preferred_element_type=f32)
        xh2, r2 = _rms(x2)
        h3 = (xh2 * gp_ref[...]).astype(bf16)
        h3_ref[...] = h3
        pg = _sig(jnp.dot(h3, wpg_ref[...], preferred_element_type=f32))
        pp = _dot(p_ref[...], wpp_ref[...])
        x3 = x2 + pg * pp
        xh3, r3 = _rms(x3)
        diff = xh3 * gf_ref[...] - t_ref[...]
        loss_ref[...] += 0.5 * jnp.sum(jnp.mean(diff * diff, axis=-1, keepdims=True), axis=0, keepdims=True)
        dy = diff * (1.0 / D)
        dx3, dgf_rows = _rms_bwd(dy, xh3, r3, gf_ref[...])
        dgf_ref[...] += _colsum(dgf_rows)
        dpp = dx3 * pg
        dpp_ref[...] = dpp.astype(bf16)
        dz = dx3 * pp * pg * (1.0 - pg)
        dz_ref[...] = dz.astype(bf16)
        dh3 = _dot_nt(dz, wpg_ref[...])
        dx2n, dgp_rows = _rms_bwd(dh3, xh2, r2, gp_ref[...])
        dgp_ref[...] += _colsum(dgp_rows)
        dx2_ref[...] = dx3 + dx2n

    row = lambda n: pl.BlockSpec((tm, n), lambda i: (i, 0))
    halo = pl.BlockSpec((8, 2 * DFF), lambda i: (jnp.maximum(i * (tm // 8) - 1, 0), 0))
    return _pc(
        body, name="ffn_tail", grid=(T // tm,),
        in_specs=[row(2 * DFF), halo, _whole((3, 2 * DFF)), _whole((1, 2 * DFF)), _whole(w_down.shape), row(D), row(PLE),
                  _whole((1, D)), _whole(w_pg.shape), _whole(w_pp.shape), _whole((1, D)), row(D)],
        out_specs=[row(D), row(DFF), row(D), row(D), row(D), _acc((1, 128)), _acc((1, D)), _acc((1, D))],
        out_shape=[jax.ShapeDtypeStruct((T, D), f32), jax.ShapeDtypeStruct((T, DFF), bf16), jax.ShapeDtypeStruct((T, D), bf16),
                   jax.ShapeDtypeStruct((T, D), bf16), jax.ShapeDtypeStruct((T, D), bf16),
                   jax.ShapeDtypeStruct((1, 128), f32), jax.ShapeDtypeStruct((1, D), f32), jax.ShapeDtypeStruct((1, D), f32)],
        compiler_params=_params(),
    )(a, a, conv_w, conv_b, w_down, x1, p, g_ple, w_pg, w_pp, g_fin, tgt)


def _ffn_bwd(dx2, a, conv_w, conv_b, w_down, w_up, x1, g_ffn):
    T = a.shape[0]
    tm = FFN_TM
    nt = T // tm

    def body(dx2_ref, a_ref, ah_ref, cw_ref, cb_ref, wd_ref, wu_ref, x1_ref, g_ref,
             da_ref, dx1_ref, dcw_ref, dcb_ref, dg_ref, carry_ref):
        i = pl.program_id(0)

        @pl.when(i == 0)
        def _():
            carry_ref[...] = jnp.zeros_like(carry_ref)
            dcw_ref[...] = jnp.zeros_like(dcw_ref)
            dcb_ref[...] = jnp.zeros_like(dcb_ref)
            dg_ref[...] = jnp.zeros_like(dg_ref)

        av = a_ref[...]
        cw = cw_ref[...]
        c, a1, a2 = _conv_rows(av, ah_ref[...], i == nt - 1, cw, cb_ref[...])
        dx2 = dx2_ref[...]
        dgd = _dot_nt(dx2, wd_ref[...])
        gl, gp = _gelu_and_grad(c[:, 0:DFF])
        dc = jnp.concatenate([dgd * c[:, DFF:] * gp, dgd * gl], axis=1)
        dcb_ref[...] += _colsum(dc)
        dcw_ref[0:1, :] += _colsum(dc * a2)
        dcw_ref[1:2, :] += _colsum(dc * a1)
        dcw_ref[2:3, :] += _colsum(dc * av)
        row = lax.broadcasted_iota(jnp.int32, (tm, 1), 0)
        n1 = carry_ref[0:1, :]
        n2 = carry_ref[1:2, :]
        up1 = jnp.where(row == tm - 1, n1, pltpu.roll(dc, tm - 1, 0))
        up2 = jnp.where(row == tm - 1, n2, jnp.where(row == tm - 2, n1, pltpu.roll(dc, tm - 2, 0)))
        da = (cw[2:3, :] * dc + cw[1:2, :] * up1 + cw[0:1, :] * up2).astype(bf16)
        carry_ref[...] = dc[0:8, :]
        da_ref[...] = da
        dh2 = lax.dot_general(da, wu_ref[...], (((1,), (1,)), ((), ())), preferred_element_type=f32)
        xh, r = _rms(x1_ref[...])
        dx1n, dg_rows = _rms_bwd(dh2, xh, r, g_ref[...])
        dg_ref[...] += _colsum(dg_rows)
        dx1_ref[...] = dx2 + dx1n

    rev = lambda i: nt - 1 - i
    row = lambda n: pl.BlockSpec((tm, n), lambda i: (rev(i), 0))
    halo = pl.BlockSpec((8, 2 * DFF), lambda i: (jnp.maximum(rev(i) * (tm // 8) - 1, 0), 0))
    return _pc(
        body, name="ffn_bwd", grid=(nt,),
        in_specs=[row(D), row(2 * DFF), halo, _whole((3, 2 * DFF)), _whole((1, 2 * DFF)), _whole(w_down.shape),
                  _whole(w_up.shape), row(D), _whole((1, D))],
        out_specs=[row(2 * DFF), row(D), _acc((3, 2 * DFF)), _acc((1, 2 * DFF)), _acc((1, D))],
        out_shape=[jax.ShapeDtypeStruct((T, 2 * DFF), bf16), jax.ShapeDtypeStruct((T, D), f32),
                   jax.ShapeDtypeStruct((3, 2 * DFF), f32), jax.ShapeDtypeStruct((1, 2 * DFF), f32),
                   jax.ShapeDtypeStruct((1, D), f32)],
        scratch_shapes=[pltpu.VMEM((8, 2 * DFF), f32)],
        compiler_params=_params(),
    )(dx2, a, a, conv_w, conv_b, w_down, w_up, x1, g_ffn)


def _mix_bwd(dx1, hg_o, s5_o, gates, w_bhg, w_bs5, w_out):
    T = dx1.shape[0]
    tm = 256

    def body(dx1_ref, hg_ref, s5_ref, gt_ref, wh_ref, ws_ref, wo_ref, dgt_ref, dhg_ref, ds5_ref, dyh_ref, dys_ref):
        dm = _dot_nt(dx1_ref[...], wo_ref[...])
        yh = jnp.dot(hg_ref[...], wh_ref[...], preferred_element_type=f32)
        ys = jnp.dot(s5_ref[...], ws_ref[...], preferred_element_type=f32)
        sh = _sig(gt_ref[:, 0:D])
        ss = _sig(gt_ref[:, D:2 * D])
        dgt_ref[:, 0:D] = dm * yh * sh * (1.0 - sh)
        dgt_ref[:, D:2 * D] = dm * ys * ss * (1.0 - ss)
        dyh = (dm * sh).astype(bf16)
        dys = (dm * ss).astype(bf16)
        dyh_ref[...] = dyh
        dys_ref[...] = dys
        dhg_ref[...] = lax.dot_general(dyh, wh_ref[...], (((1,), (1,)), ((), ())), preferred_element_type=f32)
        ds5_ref[...] = lax.dot_general(dys, ws_ref[...], (((1,), (1,)), ((), ())), preferred_element_type=f32)

    row = lambda n: pl.BlockSpec((tm, n), lambda i: (i, 0))
    return _pc(
        body, name="mix_bwd", grid=(T // tm,),
        in_specs=[row(D), row(HW), row(SW), row(2 * D), _whole(w_bhg.shape), _whole(w_bs5.shape), _whole(w_out.shape)],
        out_specs=[row(2 * D), row(HW), row(SW), row(D), row(D)],
        out_shape=[jax.ShapeDtypeStruct((T, 2 * D), f32), jax.ShapeDtypeStruct((T, HW), f32), jax.ShapeDtypeStruct((T, SW), f32),
                   jax.ShapeDtypeStruct((T, D), bf16), jax.ShapeDtypeStruct((T, D), bf16)],
        compiler_params=_params(),
    )(dx1, hg_o, s5_o, gates, w_bhg, w_bs5, w_out)


def _in_bwd(d_hg, d_u, d_gt, x, dx1, w, g):
    T = x.shape[0]
    tm = 256

    def body(dhg_ref, du_ref, dgt_ref, x_ref, dx1_ref, w_ref, g_ref, dx_ref, dg_ref):
        @pl.when(pl.program_id(0) == 0)
        def _():
            dg_ref[...] = jnp.zeros_like(dg_ref)

        dh = (_dot_nt(dhg_ref[...], w_ref[:, 0:4 * HW]) + _dot_nt(du_ref[...], w_ref[:, 4 * HW:4 * HW + SW])
              + _dot_nt(dgt_ref[...], w_ref[:, 4 * HW + SW:]))
        xh, r = _rms(x_ref[...])
        dxn, dg_rows = _rms_bwd(dh, xh, r, g_ref[...])
        dg_ref[...] += _colsum(dg_rows)
        dx_ref[...] = dx1_ref[...] + dxn

    row = lambda n: pl.BlockSpec((tm, n), lambda i: (i, 0))
    return _pc(
        body, name="in_bwd", grid=(T // tm,),
        in_specs=[row(4 * HW), row(SW), row(2 * D), row(D), row(D), _whole(w.shape), _whole((1, D))],
        out_specs=[row(D), _acc((1, D))],
        out_shape=[jax.ShapeDtypeStruct((T, D), f32), jax.ShapeDtypeStruct((1, D), f32)],
        compiler_params=_params(),
    )(d_hg, d_u, d_gt, x, dx1, w, g)


def _wgrad(name, a, b, nj=None, a_blk=None, a_idx=None, b_blk=None, b_idx=None):
    T = a.shape[0]
    tm = 512
    dense = nj is None
    if dense:
        K, N = a.shape[1], b.shape[1]
        a_blk, a_idx = K, (lambda j: 0)
        b_blk = N
        while K * b_blk * 4 > 6 * 1024 * 1024 and b_blk % 256 == 0:
            b_blk //= 2
        nj, b_idx = N // b_blk, (lambda j: j)

    def body(a_ref, b_ref, o_ref):
        @pl.when(pl.program_id(1) == 0)
        def _():
            o_ref[...] = jnp.zeros_like(o_ref)

        o_ref[0] += _dot_tn(a_ref[...], b_ref[...])

    out = _pc(
        body, name=name, grid=(nj, T // tm),
        in_specs=[pl.BlockSpec((tm, a_blk), lambda j, i: (i, a_idx(j))), pl.BlockSpec((tm, b_blk), lambda j, i: (i, b_idx(j)))],
        out_specs=pl.BlockSpec((1, a_blk, b_blk), lambda j, i: (j, 0, 0)),
        out_shape=jax.ShapeDtypeStruct((nj, a_blk, b_blk), f32),
        compiler_params=_params(2),
    )(a, b)
    if dense:
        return out[0] if nj == 1 else jnp.transpose(out, (1, 0, 2)).reshape(a.shape[1], b.shape[1])
    return out


ANY = pl.BlockSpec(memory_space=pl.ANY)


def _all_gather(name, shard):
    R, C = shard.shape

    def body(x_ref, out_ref, send_sems, recv_sems, local_sem):
        x, y, c = lax.axis_index("x"), lax.axis_index("y"), lax.axis_index("c")
        me, sibling = (x, y, c), (x, y, 1 - c)
        chips = [(1 - x, y), (x, 1 - y), (1 - x, 1 - y)]

        def slot(px, py, pc):
            return out_ref.at[4 * px + 2 * py + pc]

        def copy(k, block, to, src=None):
            return pltpu.make_async_remote_copy(
                src_ref=slot(*block) if src is None else src, dst_ref=slot(*block),
                send_sem=send_sems.at[k], recv_sem=recv_sems.at[k], device_id=to, device_id_type=MESH)

        mine = pltpu.make_async_copy(x_ref, slot(*me), local_sem)
        mine.start()
        first = [copy(0, me, sibling, src=x_ref)]
        first += [copy(1 + j, me, (*chip, c), src=x_ref) for j, chip in enumerate(chips)]
        for cp in first:
            cp.start()
        passed = [copy(4 + j, (*chip, c), sibling) for j, chip in enumerate(chips)]
        for j, chip in enumerate(chips):
            copy(1 + j, (*chip, c), me).wait_recv()
            passed[j].start()
        copy(0, sibling, me).wait_recv()
        for j, chip in enumerate(chips):
            copy(4 + j, (*chip, 1 - c), me).wait_recv()
        for cp in first + passed:
            cp.wait_send()
        mine.wait()

    return _pc(
        body, name=name, in_specs=[ANY], out_specs=ANY,
        out_shape=jax.ShapeDtypeStruct((N_DEV, R, C), shard.dtype),
        scratch_shapes=[pltpu.SemaphoreType.DMA((7,)), pltpu.SemaphoreType.DMA((7,)), pltpu.SemaphoreType.DMA],
    )(shard)


def _swap_sibling(gs):
    n = len(gs)

    def body(*refs):
        g_refs, r_refs, (send_sems, recv_sems) = refs[:n], refs[n:2 * n], refs[2 * n:]
        x, y, c = lax.axis_index("x"), lax.axis_index("y"), lax.axis_index("c")
        cps = [pltpu.make_async_remote_copy(src_ref=g.at[k, 1 - c], dst_ref=r.at[k], send_sem=send_sems.at[4 * i + k],
                                            recv_sem=recv_sems.at[4 * i + k], device_id=(x, y, 1 - c), device_id_type=MESH)
               for i, (g, r) in enumerate(zip(g_refs, r_refs)) for k in range(4)]
        for cp in cps:
            cp.start()
        for cp in cps:
            cp.wait()

    return _pc(
        body, name="rs_sibling", in_specs=[ANY] * n, out_specs=[ANY] * n,
        out_shape=[jax.ShapeDtypeStruct((4, *g.shape[2:]), g.dtype) for g in gs],
        scratch_shapes=[pltpu.SemaphoreType.DMA((4 * n,)), pltpu.SemaphoreType.DMA((4 * n,))],
    )(*gs)


def _swap_chips(ps):
    n = len(ps)

    def body(*refs):
        p_refs, r_refs, (send_sems, recv_sems) = refs[:n], refs[n:2 * n], refs[2 * n:]
        x, y, c = lax.axis_index("x"), lax.axis_index("y"), lax.axis_index("c")
        chips = [(1 - x, y), (x, 1 - y), (1 - x, 1 - y)]
        cps = [pltpu.make_async_remote_copy(src_ref=p.at[2 * px + py], dst_ref=r.at[k], send_sem=send_sems.at[3 * i + k],
                                            recv_sem=recv_sems.at[3 * i + k], device_id=(px, py, c), device_id_type=MESH)
               for i, (p, r) in enumerate(zip(p_refs, r_refs)) for k, (px, py) in enumerate(chips)]
        for cp in cps:
            cp.start()
        for cp in cps:
            cp.wait()

    return _pc(
        body, name="rs_chips", in_specs=[ANY] * n, out_specs=[ANY] * n,
        out_shape=[jax.ShapeDtypeStruct((3, *p.shape[1:]), p.dtype) for p in ps],
        scratch_shapes=[pltpu.SemaphoreType.DMA((3 * n,)), pltpu.SemaphoreType.DMA((3 * n,))],
    )(*ps)


def _add_halves(name, g4, got, ids):
    _, _, K, c = g4.shape

    def body(ids_ref, a_ref, b_ref, p16_ref, own_ref):
        s = a_ref[0, 0] + b_ref[0]
        p16_ref[0] = s.astype(bf16)

        @pl.when(pl.program_id(0) == ids_ref[1])
        def _():
            own_ref[...] = s

    return _pc(
        body, name=name,
        grid_spec=pltpu.PrefetchScalarGridSpec(
            num_scalar_prefetch=1, grid=(4,),
            in_specs=[pl.BlockSpec((1, 1, K, c), lambda k, ids: (k, ids[0], 0, 0)),
                      pl.BlockSpec((1, K, c), lambda k, ids: (k, 0, 0))],
            out_specs=[pl.BlockSpec((1, K, c), lambda k, ids: (k, 0, 0)), pl.BlockSpec((K, c), lambda k, ids: (0, 0))]),
        out_shape=[jax.ShapeDtypeStruct((4, K, c), bf16), jax.ShapeDtypeStruct((K, c), f32)],
        compiler_params=_params(),
    )(ids, g4, got)


def _row_tile(K):
    for cand in (256, 176, 128, 64):
        if K % cand == 0:
            return cand
    return K


def _adam_shard(name, own, got3, w, m, v):
    K, c = own.shape
    tr = _row_tile(K)

    def body(own_ref, got_ref, w_ref, m_ref, v_ref, g_ref, d_ref, m2_ref, v2_ref):
        g = own_ref[...] + got_ref[0].astype(f32) + got_ref[1].astype(f32) + got_ref[2].astype(f32)
        g_ref[0] = g
        delta, m2, v2 = _adam_math(g, w_ref[0], m_ref[0], v_ref[0])
        d_ref[0] = delta
        m2_ref[0] = m2
        v2_ref[0] = v2

    blk = pl.BlockSpec((1, tr, c), lambda i: (0, i, 0))
    out = jax.ShapeDtypeStruct((1, K, c), f32)
    return _pc(
        body, name=name, grid=(K // tr,),
        in_specs=[pl.BlockSpec((tr, c), lambda i: (i, 0)), pl.BlockSpec((3, tr, c), lambda i: (0, i, 0)), blk, blk, blk],
        out_specs=[blk, blk, blk, blk], out_shape=[out, out, out, out], compiler_params=_params(),
    )(own, got3, w, m, v)


def _adam_math(g, w, m, v):
    m2 = ADAM_B1 * m + (1.0 - ADAM_B1) * g
    v2 = ADAM_B2 * v + (1.0 - ADAM_B2) * (g * g)
    m_hat = m2 / (1.0 - ADAM_B1 ** ADAM_STEP)
    v_hat = v2 / (1.0 - ADAM_B2 ** ADAM_STEP)
    delta = -ADAM_LR * (m_hat / (jnp.sqrt(v_hat) + ADAM_EPS) + ADAM_WD * w)
    return delta, m2, v2


def _adam_sum(name, parts, w, m, v):
    R, C = w.shape
    tr = 8
    for cand in (256, 128, 64, 32, 16):
        if R % cand == 0:
            tr = cand
            break
    counts = [p.shape[0] for p in parts]

    def body(*refs):
        p_refs = refs[:len(parts)]
        w_ref, m_ref, v_ref, g_ref, d_ref, m2_ref, v2_ref = refs[len(parts):]
        g = None
        for p_ref, n in zip(p_refs, counts):
            for k in range(n):
                g = p_ref[k] if g is None else g + p_ref[k]
        g_ref[...] = g
        delta, m2, v2 = _adam_math(g, w_ref[...], m_ref[...], v_ref[...])
        d_ref[...] = delta
        m2_ref[...] = m2
        v2_ref[...] = v2

    blk = pl.BlockSpec((tr, C), lambda i: (i, 0))
    out = jax.ShapeDtypeStruct((R, C), f32)
    return _pc(
        body, name=name, grid=(R // tr,),
        in_specs=[pl.BlockSpec((n, tr, C), lambda i: (0, i, 0)) for n in counts] + [blk, blk, blk],
        out_specs=[blk, blk, blk, blk], out_shape=[out, out, out, out], compiler_params=_params(),
    )(*parts, w, m, v)


def _pack(arrs, dtype, row_mult):
    rows = []
    for a in arrs:
        flat = a.reshape(-1).astype(dtype)
        pad = (-flat.shape[0]) % LANES
        if pad:
            flat = jnp.concatenate([flat, jnp.zeros((pad,), dtype)])
        rows.append(flat.reshape(-1, LANES))
    out = jnp.concatenate(rows, axis=0)
    pad = (-out.shape[0]) % row_mult
    if pad:
        out = jnp.concatenate([out, jnp.zeros((pad, LANES), dtype)], axis=0)
    return out


def _unpack(buf, shapes):
    lead = buf.shape[:-2]
    outs, r = [], 0
    for shp in shapes:
        n = math.prod(shp)
        nr = -(-n // LANES)
        piece = buf[..., r:r + nr, :].reshape(*lead, nr * LANES)[..., :n]
        outs.append(piece.reshape(*lead, *shp))
        r += nr
    return outs


def _to_slabs(full, axis):
    shp = full.shape
    n = shp[axis] // N_DEV
    return jnp.moveaxis(full.reshape(*shp[:axis], N_DEV, n, *shp[axis + 1:]), axis, 0)


def _from_slabs(slabs, axis):
    t = jnp.moveaxis(slabs, 0, axis)
    shp = t.shape
    return t.reshape(*shp[:axis], shp[axis] * shp[axis + 1], *shp[axis + 2:])


def _s5_discretise(lam_re, lam_im, log_dt, b_re, b_im):
    dt = jnp.exp(log_dt)[:, None]
    mag = jnp.exp(lam_re * dt)
    a_re = mag * jnp.cos(lam_im * dt)
    a_im = mag * jnp.sin(lam_im * dt)
    den = lam_re * lam_re + lam_im * lam_im
    coef_re = ((a_re - 1.0) * lam_re + a_im * lam_im) / den
    coef_im = (a_im * lam_re - (a_re - 1.0) * lam_im) / den
    bbar_re = coef_re[..., None] * b_re - coef_im[..., None] * b_im
    bbar_im = coef_re[..., None] * b_im + coef_im[..., None] * b_re
    return a_re, a_im, bbar_re, bbar_im


def _s5_operands(bbar_re, bbar_im, c_re, c_im):
    eye = jnp.eye(SG // NST, dtype=f32)

    def b_op(bb):
        return jnp.einsum("sgnq,gh->sgqhn", bb.reshape(NST, SG // NST, SN, SP), eye).reshape(NST, 128, STW)

    def c_op(cc):
        return jnp.einsum("sgpn,gh->shngp", cc.reshape(NST, SG // NST, SP, SN), eye).reshape(NST, STW, 128)

    bdb = jnp.concatenate([b_op(bbar_re), b_op(bbar_im)], axis=0)
    bdc = jnp.concatenate([c_op(c_re), c_op(-c_im)], axis=0)
    return bdb, bdc


def _s5_powers(a_re, a_im):
    ar, ai = a_re.reshape(1, SL), a_im.reshape(1, SL)
    pr, pi = [ar], [ai]
    for _ in range(7):
        pr, pi = pr + [pr[-1] * ar - pi[-1] * ai], pi + [pr[-1] * ai + pi[-1] * ar]
    return jnp.concatenate([jnp.concatenate(pr, axis=0), jnp.concatenate(pi, axis=0)], axis=1)


_BIG = ["w_in", "s5_glu_w", "w_branch_hg", "w_branch_s5", "w_out", "w_up", "w_down", "w_ple_gate", "w_ple_proj", "conv_w"]
_BIG_AXIS = {"w_in": 1, "s5_glu_w": 0, "w_branch_hg": 1, "w_branch_s5": 1, "w_out": 0, "w_up": 1, "w_down": 0,
             "w_ple_gate": 0, "w_ple_proj": 1, "conv_w": 1}
_SMALL = ["norm_mix_g", "hg_lb_logits", "hg_norm_g", "s5_lambda_re", "s5_lambda_im", "s5_log_dt", "s5_b_re", "s5_b_im",
          "s5_c_re", "s5_c_im", "s5_d", "s5_glu_b", "norm_ffn_g", "conv_b", "norm_ple_g", "norm_final_g"]
_ORDER = ["norm_mix_g", "w_in", "hg_lb_logits", "hg_norm_g", "s5_lambda_re", "s5_lambda_im", "s5_log_dt", "s5_b_re",
          "s5_b_im", "s5_c_re", "s5_c_im", "s5_d", "s5_glu_w", "s5_glu_b", "w_branch_hg", "w_branch_s5", "w_out",
          "norm_ffn_g", "w_up", "conv_w", "conv_b", "w_down", "norm_ple_g", "w_ple_gate", "w_ple_proj", "norm_final_g"]


def kernel(x, p, norm_mix_g, w_in, hg_lb_logits, hg_norm_g, s5_lambda_re, s5_lambda_im, s5_log_dt, s5_b_re, s5_b_im, s5_c_re, s5_c_im, s5_d, s5_glu_w, s5_glu_b, w_branch_hg, w_branch_s5, w_out, norm_ffn_g, w_up, conv_w, conv_b, w_down, norm_ple_g, w_ple_gate, w_ple_proj, norm_final_g, loss_target, m_norm_mix_g, m_w_in, m_hg_lb_logits, m_hg_norm_g, m_s5_lambda_re, m_s5_lambda_im, m_s5_log_dt, m_s5_b_re, m_s5_b_im, m_s5_c_re, m_s5_c_im, m_s5_d, m_s5_glu_w, m_s5_glu_b, m_w_branch_hg, m_w_branch_s5, m_w_out, m_norm_ffn_g, m_w_up, m_conv_w, m_conv_b, m_w_down, m_norm_ple_g, m_w_ple_gate, m_w_ple_proj, m_norm_final_g, v_norm_mix_g, v_w_in, v_hg_lb_logits, v_hg_norm_g, v_s5_lambda_re, v_s5_lambda_im, v_s5_log_dt, v_s5_b_re, v_s5_b_im, v_s5_c_re, v_s5_c_im, v_s5_d, v_s5_glu_w, v_s5_glu_b, v_w_branch_hg, v_w_branch_s5, v_w_out, v_norm_ffn_g, v_w_up, v_conv_w, v_conv_b, v_w_down, v_norm_ple_g, v_w_ple_gate, v_w_ple_proj, v_norm_final_g):
    W = dict(norm_mix_g=norm_mix_g, w_in=w_in, hg_lb_logits=hg_lb_logits, hg_norm_g=hg_norm_g, s5_lambda_re=s5_lambda_re, s5_lambda_im=s5_lambda_im, s5_log_dt=s5_log_dt, s5_b_re=s5_b_re, s5_b_im=s5_b_im, s5_c_re=s5_c_re, s5_c_im=s5_c_im, s5_d=s5_d, s5_glu_w=s5_glu_w, s5_glu_b=s5_glu_b, w_branch_hg=w_branch_hg, w_branch_s5=w_branch_s5, w_out=w_out, norm_ffn_g=norm_ffn_g, w_up=w_up, conv_w=conv_w, conv_b=conv_b, w_down=w_down, norm_ple_g=norm_ple_g, w_ple_gate=w_ple_gate, w_ple_proj=w_ple_proj, norm_final_g=norm_final_g)
    M = dict(norm_mix_g=m_norm_mix_g, w_in=m_w_in, hg_lb_logits=m_hg_lb_logits, hg_norm_g=m_hg_norm_g, s5_lambda_re=m_s5_lambda_re, s5_lambda_im=m_s5_lambda_im, s5_log_dt=m_s5_log_dt, s5_b_re=m_s5_b_re, s5_b_im=m_s5_b_im, s5_c_re=m_s5_c_re, s5_c_im=m_s5_c_im, s5_d=m_s5_d, s5_glu_w=m_s5_glu_w, s5_glu_b=m_s5_glu_b, w_branch_hg=m_w_branch_hg, w_branch_s5=m_w_branch_s5, w_out=m_w_out, norm_ffn_g=m_norm_ffn_g, w_up=m_w_up, conv_w=m_conv_w, conv_b=m_conv_b, w_down=m_w_down, norm_ple_g=m_norm_ple_g, w_ple_gate=m_w_ple_gate, w_ple_proj=m_w_ple_proj, norm_final_g=m_norm_final_g)
    V = dict(norm_mix_g=v_norm_mix_g, w_in=v_w_in, hg_lb_logits=v_hg_lb_logits, hg_norm_g=v_hg_norm_g, s5_lambda_re=v_s5_lambda_re, s5_lambda_im=v_s5_lambda_im, s5_log_dt=v_s5_log_dt, s5_b_re=v_s5_b_re, s5_b_im=v_s5_b_im, s5_c_re=v_s5_c_re, s5_c_im=v_s5_c_im, s5_d=v_s5_d, s5_glu_w=v_s5_glu_w, s5_glu_b=v_s5_glu_b, w_branch_hg=v_w_branch_hg, w_branch_s5=v_w_branch_s5, w_out=v_w_out, norm_ffn_g=v_norm_ffn_g, w_up=v_w_up, conv_w=v_conv_w, conv_b=v_conv_b, w_down=v_w_down, norm_ple_g=v_norm_ple_g, w_ple_gate=v_w_ple_gate, w_ple_proj=v_w_ple_proj, norm_final_g=v_norm_final_g)

    mm_names = _BIG[:-1]
    shard2 = {n: W[n][0] for n in _BIG}
    conv_bits = lax.bitcast_convert_type(shard2["conv_w"], bf16)
    wpack = _pack([shard2[n] for n in mm_names] + [conv_bits], bf16, 16)
    gathered = _all_gather("ag_weights", wpack)
    pieces = _unpack(gathered, [shard2[n].shape for n in mm_names] + [conv_bits.shape])
    full = {n: _from_slabs(pc, _BIG_AXIS[n]) for n, pc in zip(mm_names, pieces[:-1])}
    conv_w_full = _from_slabs(lax.bitcast_convert_type(pieces[-1], f32), 1)

    xt = x[0]
    pt = p[0, 0]
    tgt = loss_target[0]
    T = xt.shape[0]
    lam_re, lam_im, log_dt = s5_lambda_re[0], s5_lambda_im[0], s5_log_dt[0]
    b_re, b_im, c_re, c_im = s5_b_re[0], s5_b_im[0], s5_c_re[0], s5_c_im[0]

    def s5_prep(lam_re, lam_im, log_dt, b_re, b_im, c_re, c_im):
        a_re, a_im, bbar_re, bbar_im = _s5_discretise(lam_re, lam_im, log_dt, b_re, b_im)
        bdb, bdc = _s5_operands(bbar_re, bbar_im, c_re, c_im)
        return a_re, a_im, bdb, bdc

    (a_re, a_im, bdb, bdc), s5_prep_vjp = jax.vjp(s5_prep, lam_re, lam_im, log_dt, b_re, b_im, c_re, c_im)
    pw = _s5_powers(a_re, a_im)
    pw_rev = pw[::-1]
    bdb_b, bdc_b = bdb.astype(bf16), bdc.astype(bf16)

    h1, proj_hg, u_raw, gates = _in_proj(xt, norm_mix_g, full["w_in"])
    hg_o, sprev = _hgrn_fwd(proj_hg, hg_lb_logits, hg_norm_g)
    x_st, y_s5, g_s5, s5_o = _s5_fwd(u_raw, bdb_b, bdc_b, pw, s5_d, full["s5_glu_w"], s5_glu_b)
    x1, merged, h2, a_up = _mix_up(xt, hg_o, s5_o, gates, full["w_branch_hg"], full["w_branch_s5"], full["w_out"],
                                   norm_ffn_g, full["w_up"])
    (dx2, gated, h3, dz_ple, dpp, loss_part, d_norm_final, d_norm_ple) = _ffn_tail(
        a_up, conv_w_full, conv_b, full["w_down"], x1, pt, norm_ple_g, full["w_ple_gate"], full["w_ple_proj"],
        norm_final_g.reshape(1, D), tgt)

    da_up, dx1, d_conv_w, d_conv_b, d_norm_ffn = _ffn_bwd(dx2, a_up, conv_w_full, conv_b, full["w_down"], full["w_up"],
                                                           x1, norm_ffn_g)
    d_gates, d_hg_o, d_s5_o, dyh, dys = _mix_bwd(dx1, hg_o, s5_o, gates, full["w_branch_hg"], full["w_branch_s5"],
                                                   full["w_out"])
    d_proj_hg, d_lb, d_hg_norm = _hgrn_bwd(proj_hg, hg_lb_logits, hg_norm_g, sprev, d_hg_o)
    d_u, lam_st, dy_s5, dz_glu, d_a_re, d_a_im, d_s5_d, d_glu_b = _s5_bwd(
        d_s5_o, y_s5, u_raw, x_st, bdb_b, bdc_b, pw_rev, s5_d, full["s5_glu_w"], s5_glu_b)
    grad_x, d_norm_mix = _in_bwd(d_proj_hg, d_u, d_gates, xt, dx1, full["w_in"], norm_mix_g)

    gw = {}
    gw["w_in"] = jnp.concatenate([_wgrad("wg_in_hg", h1, d_proj_hg), _wgrad("wg_in_u", h1, d_u),
                                  _wgrad("wg_in_gates", h1, d_gates)], axis=1)
    gw["s5_glu_w"] = _wgrad("wg_glu", g_s5, dz_glu)
    gw["w_branch_hg"] = _wgrad("wg_bhg", hg_o, dyh)
    gw["w_branch_s5"] = _wgrad("wg_bs5", s5_o, dys)
    gw["w_out"] = _wgrad("wg_out", merged, dx1)
    gw["w_up"] = _wgrad("wg_up", h2, da_up)
    gw["w_down"] = _wgrad("wg_down", gated, dx2)
    gw["w_ple_gate"] = _wgrad("wg_pg", h3, dz_ple)
    gw["w_ple_proj"] = _wgrad("wg_pp", pt, dpp)
    gw["conv_w"] = d_conv_w
    d_bdb = _wgrad("wg_s5b", u_raw, lam_st, nj=2 * NST, a_blk=128, a_idx=lambda j: j % NST, b_blk=STW, b_idx=lambda j: j)
    d_bdc = _wgrad("wg_s5c", x_st, dy_s5, nj=2 * NST, a_blk=STW, a_idx=lambda j: j, b_blk=128, b_idx=lambda j: j % NST)
    (d_lam_re, d_lam_im, d_log_dt, d_b_re, d_b_im, d_c_re, d_c_im) = s5_prep_vjp(
        (d_a_re.reshape(SG, SN), d_a_im.reshape(SG, SN), d_bdb, d_bdc))
    sm = jax.nn.softmax(hg_lb_logits, axis=0)
    d_l0 = d_lb[0] * sm[0] * sm[1]
    d_logits = jnp.stack([d_l0, -d_l0], axis=0)

    gs = {"norm_mix_g": d_norm_mix, "hg_lb_logits": d_logits, "hg_norm_g": d_hg_norm, "s5_lambda_re": d_lam_re,
          "s5_lambda_im": d_lam_im, "s5_log_dt": d_log_dt, "s5_b_re": d_b_re, "s5_b_im": d_b_im, "s5_c_re": d_c_re,
          "s5_c_im": d_c_im, "s5_d": d_s5_d, "s5_glu_b": d_glu_b, "norm_ffn_g": d_norm_ffn, "conv_b": d_conv_b,
          "norm_ple_g": d_norm_ple, "norm_final_g": d_norm_final}

    ids = jnp.stack([lax.axis_index("c"), 2 * lax.axis_index("x") + lax.axis_index("y")]).astype(jnp.int32)
    g4 = [_to_slabs(gw[n], _BIG_AXIS[n]).reshape(4, 2, *shard2[n].shape) for n in _BIG]
    got = _swap_sibling(g4)
    sums = [_add_halves("rs_add_" + n, g, r, ids) for n, g, r in zip(_BIG, g4, got)]
    got3 = _swap_chips([p16 for p16, _ in sums])
    big_out = [_adam_shard("adam_" + n, own, r3, W[n], M[n], V[n]) for n, (_, own), r3 in zip(_BIG, sums, got3)]

    small_shapes = [W[n].shape for n in _SMALL]
    spack = _pack([gs[n].reshape(W[n].shape) for n in _SMALL], f32, 128)
    sparts = _all_gather("ag_small_grads", spack)
    swp = _pack([W[n] for n in _SMALL], f32, 128)
    smp = _pack([M[n] for n in _SMALL], f32, 128)
    svp = _pack([V[n] for n in _SMALL], f32, 128)
    small_out = [_unpack(o, small_shapes) for o in _adam_sum("adam_replicated", [sparts], swp, smp, svp)]

    res = {}
    for k in range(4):
        d = {n: big_out[i][k] for i, n in enumerate(_BIG)}
        d.update({n: small_out[k][i] for i, n in enumerate(_SMALL)})
        res[k] = d
    loss = lax.psum(loss_part[0, 0], ("x", "y", "c"))
    return (loss, grad_x[None], *[res[0][n] for n in _ORDER], *[res[1][n] for n in _ORDER],
            *[res[2][n] for n in _ORDER], *[res[3][n] for n in _ORDER])
```

```python
import functools
import math

import jax
import jax.numpy as jnp
from jax import lax
from jax.experimental import pallas as pl
from jax.experimental.pallas import tpu as pltpu

f32 = jnp.float32
bf16 = jnp.bfloat16
MESH = pl.DeviceIdType.MESH

N_DEV = 8
D = 1024
HW = 512
HD = 128
NH = 4
CH = 64
SW = 512
SG = 32
SP = 16
SN = 64
SL = SG * SN
NST = 4
STW = SL // NST
DFF = 2816
PLE = 256
EPS = 1e-6
LANES = 1024
VMEM_LIMIT = 56 * 1024 * 1024

ADAM_LR, ADAM_B1, ADAM_B2, ADAM_EPS, ADAM_WD, ADAM_STEP = 0.001, 0.9, 0.999, 1e-08, 0.01, 10


def _pc(body, **kw):
    return pl.pallas_call(body, **kw)


def _params(n_axes=1, **kw):
    return pltpu.CompilerParams(dimension_semantics=("arbitrary",) * n_axes, vmem_limit_bytes=VMEM_LIMIT, **kw)


def _whole(shape):
    nd = len(shape)
    return pl.BlockSpec(shape, lambda *_: (0,) * nd, pipeline_mode=pl.Buffered(1))


def _acc(shape):
    nd = len(shape)
    return pl.BlockSpec(shape, lambda *_: (0,) * nd)


def _dot(a, b):
    return jnp.dot(a.astype(bf16), b.astype(bf16), preferred_element_type=f32)


def _dot_nt(a, b):
    return lax.dot_general(a.astype(bf16), b.astype(bf16), (((1,), (1,)), ((), ())), preferred_element_type=f32)


def _dot_tn(a, b):
    return lax.dot_general(a.astype(bf16), b.astype(bf16), (((0,), (0,)), ((), ())), preferred_element_type=f32)


def _sig(x):
    return jax.nn.sigmoid(x)


def _dsilu(z, s):
    return s * (1.0 + z * (1.0 - s))


_GC = math.sqrt(2.0 / math.pi)


def _gelu_and_grad(y):
    t = jnp.tanh(_GC * (y + 0.044715 * y * y * y))
    g = 0.5 * y * (1.0 + t)
    dg = 0.5 * (1.0 + t) + 0.5 * y * (1.0 - t * t) * _GC * (1.0 + 3.0 * 0.044715 * y * y)
    return g, dg


def _rms(x):
    r = lax.rsqrt(jnp.mean(x * x, axis=-1, keepdims=True) + EPS)
    return x * r, r


def _rms_bwd(dy, xh, r, g):
    dxh = dy * g
    dx = r * (dxh - xh * jnp.mean(dxh * xh, axis=-1, keepdims=True))
    return dx, dy * xh


def _colsum(x):
    return jnp.sum(x, axis=0, keepdims=True)


def _in_proj(x, g, w):
    T = x.shape[0]
    tm = 256

    def body(x_ref, g_ref, w_ref, h_ref, hg_ref, u_ref, gt_ref):
        xh, _ = _rms(x_ref[...])
        h = (xh * g_ref[...]).astype(bf16)
        h_ref[...] = h
        hg_ref[...] = jnp.dot(h, w_ref[:, 0:4 * HW], preferred_element_type=f32)
        u_ref[...] = jnp.dot(h, w_ref[:, 4 * HW:4 * HW + SW], preferred_element_type=f32)
        gt_ref[...] = jnp.dot(h, w_ref[:, 4 * HW + SW:], preferred_element_type=f32)

    row = lambda n: pl.BlockSpec((tm, n), lambda i: (i, 0))
    return _pc(
        body, name="in_proj", grid=(T // tm,),
        in_specs=[row(D), _whole((1, D)), _whole(w.shape)],
        out_specs=[row(D), row(4 * HW), row(SW), row(2 * D)],
        out_shape=[jax.ShapeDtypeStruct((T, D), bf16), jax.ShapeDtypeStruct((T, 4 * HW), f32),
                   jax.ShapeDtypeStruct((T, SW), f32), jax.ShapeDtypeStruct((T, 2 * D), f32)],
        compiler_params=_params(),
    )(x, g, w)


HG_NC = 2


def _tri_matmul(tri, x):
    hi = x.astype(bf16)
    r1 = x - hi.astype(f32)
    mid = r1.astype(bf16)
    lo = (r1 - mid.astype(f32)).astype(bf16)
    n = x.shape[1]
    out = jnp.dot(tri.astype(bf16), jnp.concatenate([hi, mid, lo], axis=1), preferred_element_type=f32)
    return out[:, 0:n] + out[:, n:2 * n] + out[:, 2 * n:3 * n]


def _hgrn_gates(lg, qr, fr):
    mx = jnp.max(lg, axis=0, keepdims=True)
    e = jnp.exp(lg - mx)
    lb = e[0:1, :] / (e[0:1, :] + e[1:2, :])
    sig = _sig(fr)
    f = lb + (1.0 - lb) * sig
    k = 1.0 - f
    r_i = lax.broadcasted_iota(jnp.int32, (CH, CH), 0)
    c_i = lax.broadcasted_iota(jnp.int32, (CH, CH), 1)
    tril = (r_i >= c_i)
    b = _tri_matmul(tril, jnp.log(f))
    bref = b[CH // 2:CH // 2 + 1, :]
    blast = b[CH - 1:CH, :]
    sq = _sig(qr)
    q = qr * sq
    e1 = jnp.exp(b - bref)
    e2 = jnp.exp(bref - b)
    e3 = jnp.exp(blast - b)
    e4 = jnp.exp(b)
    return dict(lb=lb, qr=qr, sq=sq, sig=sig, f=f, k=k, tril=tril, triu=(c_i >= r_i), e1=e1, e2=e2, e3=e3, e4=e4,
                qs=q * e1, ks=k * e2, kl=k * e3, qb=q * e4, dec=jnp.exp(blast))


def _hgrn_fwd(proj_hg, logits, ng4):
    T = proj_hg.shape[0]
    nch = T // CH
    tm = HG_NC * CH

    def body(q_ref, f_ref, i_ref, og_ref, lg_ref, ng_ref, out_ref, sprev_ref, st_ref):
        @pl.when(pl.program_id(0) == 0)
        def _():
            st_ref[...] = jnp.zeros_like(st_ref)

        for ci in range(HG_NC):
            rows = slice(ci * CH, (ci + 1) * CH)
            c = _hgrn_gates(lg_ref[...], q_ref[rows, :], f_ref[rows, :])
            v = i_ref[rows, :]
            og = og_ref[rows, :]
            ohs = []
            for h in range(NH):
                sl = slice(h * HD, (h + 1) * HD)
                p = jnp.where(c["tril"], _dot_nt(c["qs"][:, sl], c["ks"][:, sl]), 0.0)
                st = st_ref[h]
                sprev_ref[ci, h] = st
                o = _dot(p, v[:, sl]) + _dot_nt(c["qb"][:, sl], st)
                st_ref[h] = c["dec"][:, sl] * st + _dot_tn(v[:, sl], c["kl"][:, sl])
                ohs.append(_rms(o)[0])
            out_ref[rows, :] = (jnp.concatenate(ohs, axis=1) * ng_ref[...] * (og * _sig(og))).astype(bf16)

    col = lambda j: pl.BlockSpec((tm, HW), lambda n, j=j: (n, j))
    return _pc(
        body, name="hgrn_fwd", grid=(nch // HG_NC,),
        in_specs=[col(0), col(1), col(2), col(3), _whole((2, HW)), _whole((1, HW))],
        out_specs=[pl.BlockSpec((tm, HW), lambda n: (n, 0)),
                   pl.BlockSpec((HG_NC, NH, HD, HD), lambda n: (n, 0, 0, 0))],
        out_shape=[jax.ShapeDtypeStruct((T, HW), bf16), jax.ShapeDtypeStruct((nch, NH, HD, HD), f32)],
        scratch_shapes=[pltpu.VMEM((NH, HD, HD), f32)],
        compiler_params=_params(),
    )(proj_hg, proj_hg, proj_hg, proj_hg, logits, ng4)


def _hgrn_bwd(proj_hg, logits, ng4, sprev, d_out):
    T = proj_hg.shape[0]
    nch = T // CH
    tm = HG_NC * CH
    nst = nch // HG_NC

    def body(q_ref, f_ref, i_ref, og_ref, lg_ref, ng_ref, sp_ref, do_ref, dp_ref, dlb_ref, dng_ref, gt_ref):
        @pl.when(pl.program_id(0) == 0)
        def _():
            gt_ref[...] = jnp.zeros_like(gt_ref)
            dlb_ref[...] = jnp.zeros_like(dlb_ref)
            dng_ref[...] = jnp.zeros_like(dng_ref)

        row = lax.broadcasted_iota(jnp.int32, (CH, HW), 0)
        for ci in reversed(range(HG_NC)):
            rows = slice(ci * CH, (ci + 1) * CH)
            c = _hgrn_gates(lg_ref[...], q_ref[rows, :], f_ref[rows, :])
            tril = c["tril"]
            v = i_ref[rows, :]
            og = og_ref[rows, :]
            sog = _sig(og)
            d_gated = do_ref[rows, :]
            d_on_all = d_gated * (og * sog)
            parts = {n: [] for n in ("dqs", "dks", "dkl", "dqb", "dv", "ohg", "ddec")}
            for h in range(NH):
                sl = slice(h * HD, (h + 1) * HD)
                ng_h = ng_ref[:, sl]
                qs, ks, kl, qb, vh = c["qs"][:, sl], c["ks"][:, sl], c["kl"][:, sl], c["qb"][:, sl], v[:, sl]
                st = sp_ref[ci, h]
                gt = gt_ref[h]
                p = jnp.where(tril, _dot_nt(qs, ks), 0.0)
                o = _dot(p, vh) + _dot_nt(qb, st)
                oh, r = _rms(o)
                d_o, dng_rows = _rms_bwd(d_on_all[:, sl], oh, r, ng_h)
                dng_ref[...] += _colsum(dng_rows)
                dp = jnp.where(tril, _dot_nt(d_o, vh), 0.0)
                parts["dqb"].append(_dot(d_o, st))
                parts["dv"].append(_dot_tn(p, d_o) + _dot_nt(kl, gt))
                parts["dqs"].append(_dot(dp, ks))
                parts["dks"].append(_dot_tn(dp, qs))
                parts["dkl"].append(_dot(vh, gt))
                parts["ddec"].append(_colsum(gt * st))
                parts["ohg"].append(oh * ng_h)
                gt_ref[h] = _dot_tn(d_o, qb) + c["dec"][:, sl] * gt
            cat = {n: jnp.concatenate(vs, axis=1) for n, vs in parts.items()}
            dqs, dks, dkl, dqb = cat["dqs"], cat["dks"], cat["dkl"], cat["dqb"]
            dq = dqs * c["e1"] + dqb * c["e4"]
            dk = dks * c["e2"] + dkl * c["e3"]
            t_qs = dqs * c["qs"]
            t_ks = dks * c["ks"]
            t_kl = dkl * c["kl"]
            db = t_qs - t_ks - t_kl + dqb * c["qb"]
            dbref = _colsum(t_ks - t_qs)
            dblast = _colsum(t_kl) + cat["ddec"] * c["dec"]
            db = db + jnp.where(row == CH // 2, dbref, 0.0) + jnp.where(row == CH - 1, dblast, 0.0)
            df = _tri_matmul(c["triu"], db) / c["f"] - dk
            sig = c["sig"]
            dlb_ref[...] += _colsum(df * (1.0 - sig))
            dp_ref[rows, 0:HW] = dq * _dsilu(c["qr"], c["sq"])
            dp_ref[rows, HW:2 * HW] = df * (1.0 - c["lb"]) * sig * (1.0 - sig)
            dp_ref[rows, 2 * HW:3 * HW] = cat["dv"]
            dp_ref[rows, 3 * HW:4 * HW] = d_gated * cat["ohg"] * _dsilu(og, sog)

    rev = lambda n: nst - 1 - n
    col = lambda j: pl.BlockSpec((tm, HW), lambda n, j=j: (rev(n), j))
    return _pc(
        body, name="hgrn_bwd", grid=(nst,),
        in_specs=[col(0), col(1), col(2), col(3), _whole((2, HW)), _whole((1, HW)),
                  pl.BlockSpec((HG_NC, NH, HD, HD), lambda n: (rev(n), 0, 0, 0)),
                  pl.BlockSpec((tm, HW), lambda n: (rev(n), 0))],
        out_specs=[pl.BlockSpec((tm, 4 * HW), lambda n: (rev(n), 0)), _acc((1, HW)), _acc((1, HD))],
        out_shape=[jax.ShapeDtypeStruct((T, 4 * HW), f32), jax.ShapeDtypeStruct((1, HW), f32),
                   jax.ShapeDtypeStruct((1, HD), f32)],
        scratch_shapes=[pltpu.VMEM((NH, HD, HD), f32)],
        compiler_params=_params(),
    )(proj_hg, proj_hg, proj_hg, proj_hg, logits, ng4, sprev, d_out)


S5_TM = 256
S5_LW = 512


def _scan_rows(xs_ref, pw_ref, carry_ref, tm, reverse):
    nblk = tm // 8
    row = lax.broadcasted_iota(jnp.int32, (8, S5_LW), 0)
    sgn = -1.0 if reverse else 1.0
    for ls in range(SL // S5_LW):
        lre = pl.ds(ls * S5_LW, S5_LW)
        lim = pl.ds(SL + ls * S5_LW, S5_LW)
        pr = pw_ref[:, lre]
        pi = pw_ref[:, lim] * sgn
        steps = []
        for d in (1, 2, 4):
            j = (8 - d) if reverse else (d - 1)
            steps.append((d, pr[j:j + 1, :], pi[j:j + 1, :]))
        crow = 0 if reverse else 7
        c0r = carry_ref[crow:crow + 1, lre]
        c0i = carry_ref[crow:crow + 1, lim]

        def blk(n, carry, lre=lre, lim=lim, pr=pr, pi=pi, steps=steps):
            cr, ci = carry
            r0 = pl.multiple_of(((nblk - 1 - n) if reverse else n) * 8, 8)
            xr = xs_ref[pl.ds(r0, 8), lre]
            xi = xs_ref[pl.ds(r0, 8), lim]
            for d, ar, ai in steps:
                if reverse:
                    sr = pltpu.roll(xr, 8 - d, 0)
                    si = pltpu.roll(xi, 8 - d, 0)
                    m = row < 8 - d
                else:
                    sr = pltpu.roll(xr, d, 0)
                    si = pltpu.roll(xi, d, 0)
                    m = row >= d
                xr = xr + jnp.where(m, ar * sr - ai * si, 0.0)
                xi = xi + jnp.where(m, ar * si + ai * sr, 0.0)
            xr = xr + pr * cr - pi * ci
            xi = xi + pr * ci + pi * cr
            xs_ref[pl.ds(r0, 8), lre] = xr
            xs_ref[pl.ds(r0, 8), lim] = xi
            return (xr[crow:crow + 1, :], xi[crow:crow + 1, :])

        lax.fori_loop(0, nblk, blk, (c0r, c0i))
    last = (0 if reverse else tm - 8)
    carry_ref[...] = xs_ref[last:last + 8, :]


def _s5_fwd(u, bdb, bdc, pw, dskip, glu_w, glu_b):
    T = u.shape[0]
    tm = S5_TM

    def body(u_ref, bdb_ref, bdc_ref, pw_ref, ds_ref, gw_ref, gb_ref, x_ref, y_ref, g_ref, o_ref, xs_ref, carry_ref):
        @pl.when(pl.program_id(0) == 0)
        def _():
            carry_ref[...] = jnp.zeros_like(carry_ref)

        uv = u_ref[...]
        ub = uv.astype(bf16)
        for part in range(2):
            for s in range(NST):
                xs_ref[:, part * SL + s * STW:part * SL + (s + 1) * STW] = jnp.dot(
                    ub[:, s * 128:(s + 1) * 128], bdb_ref[part * NST + s], preferred_element_type=f32)
        _scan_rows(xs_ref, pw_ref, carry_ref, tm, reverse=False)
        x_ref[...] = xs_ref[...]
        ys = []
        for s in range(NST):
            acc = None
            for part in range(2):
                xv = xs_ref[:, part * SL + s * STW:part * SL + (s + 1) * STW].astype(bf16)
                t = jnp.dot(xv, bdc_ref[part * NST + s], preferred_element_type=f32)
                acc = t if acc is None else acc + t
            ys.append(acc)
        y = jnp.concatenate(ys, axis=1) + ds_ref[...] * uv
        y_ref[...] = y
        g, _ = _gelu_and_grad(y)
        gb = g.astype(bf16)
        g_ref[...] = gb
        z = jnp.dot(gb, gw_ref[...], preferred_element_type=f32) + gb_ref[...]
        o_ref[...] = (g * _sig(z)).astype(bf16)

    row = lambda n: pl.BlockSpec((tm, n), lambda i: (i, 0))
    return _pc(
        body, name="s5_fwd", grid=(T // tm,),
        in_specs=[row(SW), _whole(bdb.shape), _whole(bdc.shape), _whole(pw.shape), _whole((1, SW)),
                  _whole((SW, SW)), _whole((1, SW))],
        out_specs=[row(2 * SL), row(SW), row(SW), row(SW)],
        out_shape=[jax.ShapeDtypeStruct((T, 2 * SL), f32), jax.ShapeDtypeStruct((T, SW), f32),
                   jax.ShapeDtypeStruct((T, SW), bf16), jax.ShapeDtypeStruct((T, SW), bf16)],
        scratch_shapes=[pltpu.VMEM((tm, 2 * SL), f32), pltpu.VMEM((8, 2 * SL), f32)],
        compiler_params=_params(),
    )(u, bdb, bdc, pw, dskip, glu_w, glu_b)


def _s5_bwd(d_out, y, u, x, bdb, bdc, pwr, dskip, glu_w, glu_b):
    T = u.shape[0]
    tm = S5_TM
    nt = T // tm

    def body(do_ref, y_ref, u_ref, x_ref, xh_ref, bdb_ref, bdc_ref, pw_ref, ds_ref, gw_ref, gb_ref,
             du_ref, dz_ref, dar_ref, dai_ref, dd_ref, dgb_ref, dbdb_ref, dbdc_ref, gs_ref, carry_ref):
        i = pl.program_id(0)

        @pl.when(i == 0)
        def _():
            carry_ref[...] = jnp.zeros_like(carry_ref)
            dar_ref[...] = jnp.zeros_like(dar_ref)
            dai_ref[...] = jnp.zeros_like(dai_ref)
            dd_ref[...] = jnp.zeros_like(dd_ref)
            dgb_ref[...] = jnp.zeros_like(dgb_ref)
            dbdb_ref[...] = jnp.zeros_like(dbdb_ref)
            dbdc_ref[...] = jnp.zeros_like(dbdc_ref)

        yv = y_ref[...]
        uv = u_ref[...]
        g, gp = _gelu_and_grad(yv)
        z = jnp.dot(g.astype(bf16), gw_ref[...], preferred_element_type=f32) + gb_ref[...]
        sg = _sig(z)
        do = do_ref[...].astype(f32)
        dz = do * g * sg * (1.0 - sg)
        dz_ref[...] = dz.astype(bf16)
        dgb_ref[...] += _colsum(dz)
        dy = (do * sg + _dot_nt(dz, gw_ref[...])) * gp
        dyb = dy.astype(bf16)
        dd_ref[...] += _colsum(dy * uv)
        for part in range(2):
            for s in range(NST):
                gs_ref[:, part * SL + s * STW:part * SL + (s + 1) * STW] = lax.dot_general(
                    dyb[:, s * 128:(s + 1) * 128], bdc_ref[part * NST + s], (((1,), (1,)), ((), ())),
                    preferred_element_type=f32)
        _scan_rows(gs_ref, pw_ref, carry_ref, tm, reverse=True)
        ub = uv.astype(bf16)
        dus = []
        for s in range(NST):
            acc = None
            for part in range(2):
                strip = slice(part * SL + s * STW, part * SL + (s + 1) * STW)
                lv = gs_ref[:, strip].astype(bf16)
                t = lax.dot_general(lv, bdb_ref[part * NST + s], (((1,), (1,)), ((), ())), preferred_element_type=f32)
                acc = t if acc is None else acc + t
                dbdb_ref[part * NST + s] += _dot_tn(ub[:, s * 128:(s + 1) * 128], lv)
                dbdc_ref[part * NST + s] += _dot_tn(x_ref[:, strip], dyb[:, s * 128:(s + 1) * 128])
            dus.append(acc)
        du_ref[...] = jnp.concatenate(dus, axis=1) + dy * ds_ref[...]
        first = (i == nt - 1)
        rowi = lax.broadcasted_iota(jnp.int32, (tm, S5_LW), 0)
        for ls in range(SL // S5_LW):
            lre = slice(ls * S5_LW, (ls + 1) * S5_LW)
            lim = slice(SL + ls * S5_LW, SL + (ls + 1) * S5_LW)
            hr = jnp.where(first, 0.0, xh_ref[7:8, lre])
            hi = jnp.where(first, 0.0, xh_ref[7:8, lim])
            xpr = jnp.where(rowi == 0, hr, pltpu.roll(x_ref[:, lre], 1, 0))
            xpi = jnp.where(rowi == 0, hi, pltpu.roll(x_ref[:, lim], 1, 0))
            lr = gs_ref[:, lre]
            li = gs_ref[:, lim]
            dar_ref[:, lre] += _colsum(lr * xpr + li * xpi)
            dai_ref[:, lre] += _colsum(li * xpr - lr * xpi)

    rev = lambda i: nt - 1 - i
    row = lambda n: pl.BlockSpec((tm, n), lambda i: (rev(i), 0))
    halo = pl.BlockSpec((8, 2 * SL), lambda i: (jnp.maximum(rev(i) * (tm // 8) - 1, 0), 0))
    return _pc(
        body, name="s5_bwd", grid=(nt,),
        in_specs=[row(SW), row(SW), row(SW), row(2 * SL), halo, _whole(bdb.shape), _whole(bdc.shape),
                  _whole(pwr.shape), _whole((1, SW)), _whole((SW, SW)), _whole((1, SW))],
        out_specs=[row(SW), row(SW), _acc((1, SL)), _acc((1, SL)), _acc((1, SW)), _acc((1, SW)),
                   _acc(bdb.shape), _acc(bdc.shape)],
        out_shape=[jax.ShapeDtypeStruct((T, SW), f32), jax.ShapeDtypeStruct((T, SW), bf16),
                   jax.ShapeDtypeStruct((1, SL), f32), jax.ShapeDtypeStruct((1, SL), f32),
                   jax.ShapeDtypeStruct((1, SW), f32), jax.ShapeDtypeStruct((1, SW), f32),
                   jax.ShapeDtypeStruct(bdb.shape, f32), jax.ShapeDtypeStruct(bdc.shape, f32)],
        scratch_shapes=[pltpu.VMEM((tm, 2 * SL), f32), pltpu.VMEM((8, 2 * SL), f32)],
        compiler_params=_params(),
    )(d_out, y, u, x, x, bdb, bdc, pwr, dskip, glu_w, glu_b)


def _mix_up(x, hg_o, s5_o, gates, w_bhg, w_bs5, w_out, g_ffn, w_up):
    T = x.shape[0]
    tm = 256

    def body(x_ref, hg_ref, s5_ref, gt_ref, wh_ref, ws_ref, wo_ref, g_ref, wu_ref, x1_ref, mg_ref, h2_ref, a_ref):
        yh = jnp.dot(hg_ref[...], wh_ref[...], preferred_element_type=f32)
        ys = jnp.dot(s5_ref[...], ws_ref[...], preferred_element_type=f32)
        merged = (_sig(gt_ref[:, 0:D]) * yh + _sig(gt_ref[:, D:2 * D]) * ys).astype(bf16)
        mg_ref[...] = merged
        x1 = x_ref[...] + jnp.dot(merged, wo_ref[...], preferred_element_type=f32)
        x1_ref[...] = x1
        xh, _ = _rms(x1)
        h2 = (xh * g_ref[...]).astype(bf16)
        h2_ref[...] = h2
        a_ref[...] = jnp.dot(h2, wu_ref[...], preferred_element_type=f32)

    row = lambda n: pl.BlockSpec((tm, n), lambda i: (i, 0))
    return _pc(
        body, name="mix_up", grid=(T // tm,),
        in_specs=[row(D), row(HW), row(SW), row(2 * D), _whole(w_bhg.shape), _whole(w_bs5.shape), _whole(w_out.shape),
                  _whole((1, D)), _whole(w_up.shape)],
        out_specs=[row(D), row(D), row(D), row(2 * DFF)],
        out_shape=[jax.ShapeDtypeStruct((T, D), f32), jax.ShapeDtypeStruct((T, D), bf16),
                   jax.ShapeDtypeStruct((T, D), bf16), jax.ShapeDtypeStruct((T, 2 * DFF), f32)],
        compiler_params=_params(),
    )(x, hg_o, s5_o, gates, w_bhg, w_bs5, w_out, g_ffn, w_up)


FFN_TM = 128


def _conv_rows(a, halo, first, conv_w, conv_b):
    tm = a.shape[0]
    row = lax.broadcasted_iota(jnp.int32, (tm, 1), 0)
    hm1 = jnp.where(first, 0.0, halo[7:8, :])
    hm2 = jnp.where(first, 0.0, halo[6:7, :])
    a1 = jnp.where(row == 0, hm1, pltpu.roll(a, 1, 0))
    a2 = jnp.where(row == 0, hm2, jnp.where(row == 1, hm1, pltpu.roll(a, 2, 0)))
    c = conv_b + conv_w[0:1, :] * a2 + conv_w[1:2, :] * a1 + conv_w[2:3, :] * a
    return c, a1, a2


def _ffn_tail(a, conv_w, conv_b, w_down, x1, p, g_ple, w_pg, w_pp, g_fin, tgt):
    T = a.shape[0]
    tm = FFN_TM

    def body(a_ref, ah_ref, cw_ref, cb_ref, wd_ref, x1_ref, p_ref, gp_ref, wpg_ref, wpp_ref, gf_ref, t_ref,
             dx2_ref, gd_ref, ga_ref, gb_ref, h3_ref, dz_ref, dpp_ref, loss_ref, dgf_ref, dgp_ref):
        i = pl.program_id(0)

        @pl.when(i == 0)
        def _():
            loss_ref[...] = jnp.zeros_like(loss_ref)
            dgf_ref[...] = jnp.zeros_like(dgf_ref)
            dgp_ref[...] = jnp.zeros_like(dgp_ref)

        c, _, _ = _conv_rows(a_ref[...], ah_ref[...], i == 0, cw_ref[...], cb_ref[...])
        gl, gp = _gelu_and_grad(c[:, 0:DFF])
        ga_ref[...] = c[:, DFF:] * gp
        gb_ref[...] = gl
        gated = (gl * c[:, DFF:]).astype(bf16)
        gd_ref[...] = gated
        x2 = x1_ref[...] + jnp.dot(gated, wd_ref[...], preferred_element_type=f32)
        xh2, r2 = _rms(x2)
        h3 = (xh2 * gp_ref[...]).astype(bf16)
        h3_ref[...] = h3
        pg = _sig(jnp.dot(h3, wpg_ref[...], preferred_element_type=f32))
        pp = _dot(p_ref[...], wpp_ref[...])
        x3 = x2 + pg * pp
        xh3, r3 = _rms(x3)
        diff = xh3 * gf_ref[...] - t_ref[...]
        loss_ref[...] += 0.5 * jnp.sum(jnp.mean(diff * diff, axis=-1, keepdims=True), axis=0, keepdims=True)
        dy = diff * (1.0 / D)
        dx3, dgf_rows = _rms_bwd(dy, xh3, r3, gf_ref[...])
        dgf_ref[...] += _colsum(dgf_rows)
        dpp = dx3 * pg
        dpp_ref[...] = dpp.astype(bf16)
        dz = dx3 * pp * pg * (1.0 - pg)
        dz_ref[...] = dz.astype(bf16)
        dh3 = _dot_nt(dz, wpg_ref[...])
        dx2n, dgp_rows = _rms_bwd(dh3, xh2, r2, gp_ref[...])
        dgp_ref[...] += _colsum(dgp_rows)
        dx2_ref[...] = dx3 + dx2n

    row = lambda n: pl.BlockSpec((tm, n), lambda i: (i, 0))
    halo = pl.BlockSpec((8, 2 * DFF), lambda i: (jnp.maximum(i * (tm // 8) - 1, 0), 0))
    return _pc(
        body, name="ffn_tail", grid=(T // tm,),
        in_specs=[row(2 * DFF), halo, _whole((3, 2 * DFF)), _whole((1, 2 * DFF)), _whole(w_down.shape), row(D), row(PLE),
                  _whole((1, D)), _whole(w_pg.shape), _whole(w_pp.shape), _whole((1, D)), row(D)],
        out_specs=[row(D), row(DFF), row(DFF), row(DFF), row(D), row(D), row(D), _acc((1, 128)), _acc((1, D)), _acc((1, D))],
        out_shape=[jax.ShapeDtypeStruct((T, D), f32), jax.ShapeDtypeStruct((T, DFF), bf16),
                   jax.ShapeDtypeStruct((T, DFF), f32), jax.ShapeDtypeStruct((T, DFF), f32), jax.ShapeDtypeStruct((T, D), bf16),
                   jax.ShapeDtypeStruct((T, D), bf16), jax.ShapeDtypeStruct((T, D), bf16),
                   jax.ShapeDtypeStruct((1, 128), f32), jax.ShapeDtypeStruct((1, D), f32), jax.ShapeDtypeStruct((1, D), f32)],
        compiler_params=_params(),
    )(a, a, conv_w, conv_b, w_down, x1, p, g_ple, w_pg, w_pp, g_fin, tgt)


def _ffn_bwd(dx2, a, g_a, g_b, conv_w, w_down, w_up, x1, g_ffn):
    T = a.shape[0]
    tm = FFN_TM
    nt = T // tm

    def body(dx2_ref, a_ref, ga_ref, gb_ref, cw_ref, wd_ref, wu_ref, x1_ref, g_ref,
             da_ref, dx1_ref, dcw_ref, dcb_ref, dg_ref, carry_ref):
        i = pl.program_id(0)

        @pl.when(i == 0)
        def _():
            carry_ref[...] = jnp.zeros_like(carry_ref)
            dcw_ref[...] = jnp.zeros_like(dcw_ref)
            dcb_ref[...] = jnp.zeros_like(dcb_ref)
            dg_ref[...] = jnp.zeros_like(dg_ref)

        av = a_ref[...]
        cw = cw_ref[...]
        dx2 = dx2_ref[...]
        dgd = _dot_nt(dx2, wd_ref[...])
        dc = jnp.concatenate([dgd * ga_ref[...], dgd * gb_ref[...]], axis=1)
        row = lax.broadcasted_iota(jnp.int32, (tm, 1), 0)
        n1 = carry_ref[0:1, :]
        n2 = carry_ref[1:2, :]
        up1 = jnp.where(row == tm - 1, n1, pltpu.roll(dc, tm - 1, 0))
        up2 = jnp.where(row == tm - 1, n2, jnp.where(row == tm - 2, n1, pltpu.roll(dc, tm - 2, 0)))
        dcb_ref[...] += _colsum(dc)
        dcw_ref[0:1, :] += _colsum(up2 * av)
        dcw_ref[1:2, :] += _colsum(up1 * av)
        dcw_ref[2:3, :] += _colsum(dc * av)
        da = (cw[2:3, :] * dc + cw[1:2, :] * up1 + cw[0:1, :] * up2).astype(bf16)
        carry_ref[...] = dc[0:8, :]
        da_ref[...] = da
        dh2 = lax.dot_general(da, wu_ref[...], (((1,), (1,)), ((), ())), preferred_element_type=f32)
        xh, r = _rms(x1_ref[...])
        dx1n, dg_rows = _rms_bwd(dh2, xh, r, g_ref[...])
        dg_ref[...] += _colsum(dg_rows)
        dx1_ref[...] = dx2 + dx1n

    rev = lambda i: nt - 1 - i
    row = lambda n: pl.BlockSpec((tm, n), lambda i: (rev(i), 0))
    return _pc(
        body, name="ffn_bwd", grid=(nt,),
        in_specs=[row(D), row(2 * DFF), row(DFF), row(DFF), _whole((3, 2 * DFF)), _whole(w_down.shape),
                  _whole(w_up.shape), row(D), _whole((1, D))],
        out_specs=[row(2 * DFF), row(D), _acc((3, 2 * DFF)), _acc((1, 2 * DFF)), _acc((1, D))],
        out_shape=[jax.ShapeDtypeStruct((T, 2 * DFF), bf16), jax.ShapeDtypeStruct((T, D), f32),
                   jax.ShapeDtypeStruct((3, 2 * DFF), f32), jax.ShapeDtypeStruct((1, 2 * DFF), f32),
                   jax.ShapeDtypeStruct((1, D), f32)],
        scratch_shapes=[pltpu.VMEM((8, 2 * DFF), f32)],
        compiler_params=_params(),
    )(dx2, a, g_a, g_b, conv_w, w_down, w_up, x1, g_ffn)


def _mix_bwd(dx1, hg_o, s5_o, gates, w_bhg, w_bs5, w_out):
    T = dx1.shape[0]
    tm = 256

    def body(dx1_ref, hg_ref, s5_ref, gt_ref, wh_ref, ws_ref, wo_ref, dgt_ref, dhg_ref, ds5_ref, dyh_ref, dys_ref):
        dm = _dot_nt(dx1_ref[...], wo_ref[...])
        yh = jnp.dot(hg_ref[...], wh_ref[...], preferred_element_type=f32)
        ys = jnp.dot(s5_ref[...], ws_ref[...], preferred_element_type=f32)
        sh = _sig(gt_ref[:, 0:D])
        ss = _sig(gt_ref[:, D:2 * D])
        dgt_ref[:, 0:D] = dm * yh * sh * (1.0 - sh)
        dgt_ref[:, D:2 * D] = dm * ys * ss * (1.0 - ss)
        dyh = (dm * sh).astype(bf16)
        dys = (dm * ss).astype(bf16)
        dyh_ref[...] = dyh
        dys_ref[...] = dys
        dhg_ref[...] = lax.dot_general(dyh, wh_ref[...], (((1,), (1,)), ((), ())), preferred_element_type=f32)
        ds5_ref[...] = lax.dot_general(dys, ws_ref[...], (((1,), (1,)), ((), ())), preferred_element_type=f32)

    row = lambda n: pl.BlockSpec((tm, n), lambda i: (i, 0))
    return _pc(
        body, name="mix_bwd", grid=(T // tm,),
        in_specs=[row(D), row(HW), row(SW), row(2 * D), _whole(w_bhg.shape), _whole(w_bs5.shape), _whole(w_out.shape)],
        out_specs=[row(2 * D), row(HW), row(SW), row(D), row(D)],
        out_shape=[jax.ShapeDtypeStruct((T, 2 * D), f32), jax.ShapeDtypeStruct((T, HW), f32), jax.ShapeDtypeStruct((T, SW), f32),
                   jax.ShapeDtypeStruct((T, D), bf16), jax.ShapeDtypeStruct((T, D), bf16)],
        compiler_params=_params(),
    )(dx1, hg_o, s5_o, gates, w_bhg, w_bs5, w_out)


def _in_bwd(d_hg, d_u, d_gt, x, dx1, w, g):
    T = x.shape[0]
    tm = 256

    def body(dhg_ref, du_ref, dgt_ref, x_ref, dx1_ref, w_ref, g_ref, dx_ref, dg_ref):
        @pl.when(pl.program_id(0) == 0)
        def _():
            dg_ref[...] = jnp.zeros_like(dg_ref)

        dh = (_dot_nt(dhg_ref[...], w_ref[:, 0:4 * HW]) + _dot_nt(du_ref[...], w_ref[:, 4 * HW:4 * HW + SW])
              + _dot_nt(dgt_ref[...], w_ref[:, 4 * HW + SW:]))
        xh, r = _rms(x_ref[...])
        dxn, dg_rows = _rms_bwd(dh, xh, r, g_ref[...])
        dg_ref[...] += _colsum(dg_rows)
        dx_ref[...] = dx1_ref[...] + dxn

    row = lambda n: pl.BlockSpec((tm, n), lambda i: (i, 0))
    return _pc(
        body, name="in_bwd", grid=(T // tm,),
        in_specs=[row(4 * HW), row(SW), row(2 * D), row(D), row(D), _whole(w.shape), _whole((1, D))],
        out_specs=[row(D), _acc((1, D))],
        out_shape=[jax.ShapeDtypeStruct((T, D), f32), jax.ShapeDtypeStruct((1, D), f32)],
        compiler_params=_params(),
    )(d_hg, d_u, d_gt, x, dx1, w, g)


def _wgrad(name, a, b, nj=None, a_blk=None, a_idx=None, b_blk=None, b_idx=None):
    T = a.shape[0]
    tm = 512
    dense = nj is None
    if dense:
        K, N = a.shape[1], b.shape[1]
        a_blk, a_idx = K, (lambda j: 0)
        b_blk = N
        while K * b_blk * 4 > 6 * 1024 * 1024 and b_blk % 256 == 0:
            b_blk //= 2
        nj, b_idx = N // b_blk, (lambda j: j)

    def body(a_ref, b_ref, o_ref):
        @pl.when(pl.program_id(1) == 0)
        def _():
            o_ref[...] = jnp.zeros_like(o_ref)

        o_ref[0] += _dot_tn(a_ref[...], b_ref[...])

    out = _pc(
        body, name=name, grid=(nj, T // tm),
        in_specs=[pl.BlockSpec((tm, a_blk), lambda j, i: (i, a_idx(j))), pl.BlockSpec((tm, b_blk), lambda j, i: (i, b_idx(j)))],
        out_specs=pl.BlockSpec((1, a_blk, b_blk), lambda j, i: (j, 0, 0)),
        out_shape=jax.ShapeDtypeStruct((nj, a_blk, b_blk), f32),
        compiler_params=_params(2),
    )(a, b)
    if dense:
        return out[0] if nj == 1 else jnp.transpose(out, (1, 0, 2)).reshape(a.shape[1], b.shape[1])
    return out


ANY = pl.BlockSpec(memory_space=pl.ANY)


def _all_gather(name, shard):
    R, C = shard.shape

    def body(x_ref, out_ref, send_sems, recv_sems, local_sem):
        x, y, c = lax.axis_index("x"), lax.axis_index("y"), lax.axis_index("c")
        me, sibling = (x, y, c), (x, y, 1 - c)
        chips = [(1 - x, y), (x, 1 - y), (1 - x, 1 - y)]

        def slot(px, py, pc):
            return out_ref.at[4 * px + 2 * py + pc]

        def copy(k, block, to, src=None):
            return pltpu.make_async_remote_copy(
                src_ref=slot(*block) if src is None else src, dst_ref=slot(*block),
                send_sem=send_sems.at[k], recv_sem=recv_sems.at[k], device_id=to, device_id_type=MESH)

        mine = pltpu.make_async_copy(x_ref, slot(*me), local_sem)
        mine.start()
        first = [copy(0, me, sibling, src=x_ref)]
        first += [copy(1 + j, me, (*chip, c), src=x_ref) for j, chip in enumerate(chips)]
        for cp in first:
            cp.start()
        passed = [copy(4 + j, (*chip, c), sibling) for j, chip in enumerate(chips)]
        for j, chip in enumerate(chips):
            copy(1 + j, (*chip, c), me).wait_recv()
            passed[j].start()
        copy(0, sibling, me).wait_recv()
        for j, chip in enumerate(chips):
            copy(4 + j, (*chip, 1 - c), me).wait_recv()
        for cp in first + passed:
            cp.wait_send()
        mine.wait()

    return _pc(
        body, name=name, in_specs=[ANY], out_specs=ANY,
        out_shape=jax.ShapeDtypeStruct((N_DEV, R, C), shard.dtype),
        scratch_shapes=[pltpu.SemaphoreType.DMA((7,)), pltpu.SemaphoreType.DMA((7,)), pltpu.SemaphoreType.DMA],
    )(shard)


def _swap_sibling(gs):
    n = len(gs)

    def body(*refs):
        g_refs, r_refs, (send_sems, recv_sems) = refs[:n], refs[n:2 * n], refs[2 * n:]
        x, y, c = lax.axis_index("x"), lax.axis_index("y"), lax.axis_index("c")
        cps = [pltpu.make_async_remote_copy(src_ref=g.at[k, 1 - c], dst_ref=r.at[k], send_sem=send_sems.at[4 * i + k],
                                            recv_sem=recv_sems.at[4 * i + k], device_id=(x, y, 1 - c), device_id_type=MESH)
               for i, (g, r) in enumerate(zip(g_refs, r_refs)) for k in range(4)]
        for cp in cps:
            cp.start()
        for cp in cps:
            cp.wait()

    return _pc(
        body, name="rs_sibling", in_specs=[ANY] * n, out_specs=[ANY] * n,
        out_shape=[jax.ShapeDtypeStruct((4, *g.shape[2:]), g.dtype) for g in gs],
        scratch_shapes=[pltpu.SemaphoreType.DMA((4 * n,)), pltpu.SemaphoreType.DMA((4 * n,))],
    )(*gs)


def _swap_chips(ps):
    n = len(ps)

    def body(*refs):
        p_refs, r_refs, (send_sems, recv_sems) = refs[:n], refs[n:2 * n], refs[2 * n:]
        x, y, c = lax.axis_index("x"), lax.axis_index("y"), lax.axis_index("c")
        chips = [(1 - x, y), (x, 1 - y), (1 - x, 1 - y)]
        cps = [pltpu.make_async_remote_copy(src_ref=p.at[2 * px + py], dst_ref=r.at[k], send_sem=send_sems.at[3 * i + k],
                                            recv_sem=recv_sems.at[3 * i + k], device_id=(px, py, c), device_id_type=MESH)
               for i, (p, r) in enumerate(zip(p_refs, r_refs)) for k, (px, py) in enumerate(chips)]
        for cp in cps:
            cp.start()
        for cp in cps:
            cp.wait()

    return _pc(
        body, name="rs_chips", in_specs=[ANY] * n, out_specs=[ANY] * n,
        out_shape=[jax.ShapeDtypeStruct((3, *p.shape[1:]), p.dtype) for p in ps],
        scratch_shapes=[pltpu.SemaphoreType.DMA((3 * n,)), pltpu.SemaphoreType.DMA((3 * n,))],
    )(*ps)


def _add_halves(name, g4, got, ids):
    _, _, K, c = g4.shape

    def body(ids_ref, a_ref, b_ref, p16_ref, own_ref):
        s = a_ref[0, 0] + b_ref[0]
        p16_ref[0] = s.astype(bf16)

        @pl.when(pl.program_id(0) == ids_ref[1])
        def _():
            own_ref[...] = s

    return _pc(
        body, name=name,
        grid_spec=pltpu.PrefetchScalarGridSpec(
            num_scalar_prefetch=1, grid=(4,),
            in_specs=[pl.BlockSpec((1, 1, K, c), lambda k, ids: (k, ids[0], 0, 0)),
                      pl.BlockSpec((1, K, c), lambda k, ids: (k, 0, 0))],
            out_specs=[pl.BlockSpec((1, K, c), lambda k, ids: (k, 0, 0)), pl.BlockSpec((K, c), lambda k, ids: (0, 0))]),
        out_shape=[jax.ShapeDtypeStruct((4, K, c), bf16), jax.ShapeDtypeStruct((K, c), f32)],
        compiler_params=_params(),
    )(ids, g4, got)


def _row_tile(K):
    for cand in (256, 176, 128, 64):
        if K % cand == 0:
            return cand
    return K


def _adam_shard(name, own, got3, w, m, v):
    K, c = own.shape
    tr = _row_tile(K)

    def body(own_ref, got_ref, w_ref, m_ref, v_ref, g_ref, d_ref, m2_ref, v2_ref):
        g = own_ref[...] + got_ref[0].astype(f32) + got_ref[1].astype(f32) + got_ref[2].astype(f32)
        g_ref[0] = g
        delta, m2, v2 = _adam_math(g, w_ref[0], m_ref[0], v_ref[0])
        d_ref[0] = delta
        m2_ref[0] = m2
        v2_ref[0] = v2

    blk = pl.BlockSpec((1, tr, c), lambda i: (0, i, 0))
    out = jax.ShapeDtypeStruct((1, K, c), f32)
    return _pc(
        body, name=name, grid=(K // tr,),
        in_specs=[pl.BlockSpec((tr, c), lambda i: (i, 0)), pl.BlockSpec((3, tr, c), lambda i: (0, i, 0)), blk, blk, blk],
        out_specs=[blk, blk, blk, blk], out_shape=[out, out, out, out], compiler_params=_params(),
    )(own, got3, w, m, v)


def _allreduce_adam(grads, ws, ms, vs):
    n = len(grads)
    shapes = [g.shape for g in grads]

    def body(*refs):
        g_refs, w_refs, m_refs, v_refs = refs[0:n], refs[n:2 * n], refs[2 * n:3 * n], refs[3 * n:4 * n]
        outs = refs[4 * n:8 * n]
        acc = refs[8 * n:9 * n]
        recv = refs[9 * n:12 * n]
        send_sems, recv_sems = refs[12 * n:]
        x, y, c = lax.axis_index("x"), lax.axis_index("y"), lax.axis_index("c")
        peers = [(x, y, 1 - c), (1 - x, y, c), (x, 1 - y, c)]
        for i in range(n):
            acc[i][...] = g_refs[i][...]
        for s, peer in enumerate(peers):
            cps = [pltpu.make_async_remote_copy(src_ref=acc[i], dst_ref=recv[s * n + i], send_sem=send_sems.at[s * n + i],
                                                recv_sem=recv_sems.at[s * n + i], device_id=peer, device_id_type=MESH)
                   for i in range(n)]
            for cp in cps:
                cp.start()
            for cp in cps:
                cp.wait()
            for i in range(n):
                acc[i][...] = acc[i][...] + recv[s * n + i][...]
        for i in range(n):
            g = acc[i][...]
            delta, m2, v2 = _adam_math(g, w_refs[i][...], m_refs[i][...], v_refs[i][...])
            outs[i][...] = g
            outs[n + i][...] = delta
            outs[2 * n + i][...] = m2
            outs[3 * n + i][...] = v2

    vm = pl.BlockSpec(memory_space=pltpu.VMEM)
    return _pc(
        body, name="allreduce_adam_small", in_specs=[vm] * (4 * n), out_specs=[vm] * (4 * n),
        out_shape=[jax.ShapeDtypeStruct(s, f32) for s in shapes] * 4,
        scratch_shapes=[pltpu.VMEM(s, f32) for s in shapes] * 4
        + [pltpu.SemaphoreType.DMA((3 * n,)), pltpu.SemaphoreType.DMA((3 * n,))],
        compiler_params=pltpu.CompilerParams(vmem_limit_bytes=VMEM_LIMIT),
    )(*grads, *ws, *ms, *vs)


def _adam_math(g, w, m, v):
    m2 = ADAM_B1 * m + (1.0 - ADAM_B1) * g
    v2 = ADAM_B2 * v + (1.0 - ADAM_B2) * (g * g)
    m_hat = m2 / (1.0 - ADAM_B1 ** ADAM_STEP)
    v_hat = v2 / (1.0 - ADAM_B2 ** ADAM_STEP)
    delta = -ADAM_LR * (m_hat / (jnp.sqrt(v_hat) + ADAM_EPS) + ADAM_WD * w)
    return delta, m2, v2


def _adam_sum(name, parts, w, m, v):
    R, C = w.shape
    tr = 8
    for cand in (256, 128, 64, 32, 16):
        if R % cand == 0:
            tr = cand
            break
    counts = [p.shape[0] for p in parts]

    def body(*refs):
        p_refs = refs[:len(parts)]
        w_ref, m_ref, v_ref, g_ref, d_ref, m2_ref, v2_ref = refs[len(parts):]
        g = None
        for p_ref, n in zip(p_refs, counts):
            for k in range(n):
                g = p_ref[k] if g is None else g + p_ref[k]
        g_ref[...] = g
        delta, m2, v2 = _adam_math(g, w_ref[...], m_ref[...], v_ref[...])
        d_ref[...] = delta
        m2_ref[...] = m2
        v2_ref[...] = v2

    blk = pl.BlockSpec((tr, C), lambda i: (i, 0))
    out = jax.ShapeDtypeStruct((R, C), f32)
    return _pc(
        body, name=name, grid=(R // tr,),
        in_specs=[pl.BlockSpec((n, tr, C), lambda i: (0, i, 0)) for n in counts] + [blk, blk, blk],
        out_specs=[blk, blk, blk, blk], out_shape=[out, out, out, out], compiler_params=_params(),
    )(*parts, w, m, v)


def _pack(arrs, dtype, row_mult):
    rows = []
    for a in arrs:
        flat = a.reshape(-1).astype(dtype)
        pad = (-flat.shape[0]) % LANES
        if pad:
            flat = jnp.concatenate([flat, jnp.zeros((pad,), dtype)])
        rows.append(flat.reshape(-1, LANES))
    out = jnp.concatenate(rows, axis=0)
    pad = (-out.shape[0]) % row_mult
    if pad:
        out = jnp.concatenate([out, jnp.zeros((pad, LANES), dtype)], axis=0)
    return out


def _unpack(buf, shapes):
    lead = buf.shape[:-2]
    outs, r = [], 0
    for shp in shapes:
        n = math.prod(shp)
        nr = -(-n // LANES)
        piece = buf[..., r:r + nr, :].reshape(*lead, nr * LANES)[..., :n]
        outs.append(piece.reshape(*lead, *shp))
        r += nr
    return outs


def _to_slabs(full, axis):
    shp = full.shape
    n = shp[axis] // N_DEV
    return jnp.moveaxis(full.reshape(*shp[:axis], N_DEV, n, *shp[axis + 1:]), axis, 0)


def _from_slabs(slabs, axis):
    t = jnp.moveaxis(slabs, 0, axis)
    shp = t.shape
    return t.reshape(*shp[:axis], shp[axis] * shp[axis + 1], *shp[axis + 2:])


def _s5_discretise(lam_re, lam_im, log_dt, b_re, b_im):
    dt = jnp.exp(log_dt)[:, None]
    mag = jnp.exp(lam_re * dt)
    a_re = mag * jnp.cos(lam_im * dt)
    a_im = mag * jnp.sin(lam_im * dt)
    den = lam_re * lam_re + lam_im * lam_im
    coef_re = ((a_re - 1.0) * lam_re + a_im * lam_im) / den
    coef_im = (a_im * lam_re - (a_re - 1.0) * lam_im) / den
    bbar_re = coef_re[..., None] * b_re - coef_im[..., None] * b_im
    bbar_im = coef_re[..., None] * b_im + coef_im[..., None] * b_re
    return a_re, a_im, bbar_re, bbar_im


def _s5_operands(bbar_re, bbar_im, c_re, c_im):
    eye = jnp.eye(SG // NST, dtype=f32)

    def b_op(bb):
        return jnp.einsum("sgnq,gh->sgqhn", bb.reshape(NST, SG // NST, SN, SP), eye).reshape(NST, 128, STW)

    def c_op(cc):
        return jnp.einsum("sgpn,gh->shngp", cc.reshape(NST, SG // NST, SP, SN), eye).reshape(NST, STW, 128)

    bdb = jnp.concatenate([b_op(bbar_re), b_op(bbar_im)], axis=0)
    bdc = jnp.concatenate([c_op(c_re), c_op(-c_im)], axis=0)
    return bdb, bdc


def _s5_powers(a_re, a_im, descending=False):
    ar, ai = a_re.reshape(1, SL), a_im.reshape(1, SL)
    pr, pi = [ar], [ai]
    for _ in range(7):
        pr, pi = pr + [pr[-1] * ar - pi[-1] * ai], pi + [pr[-1] * ai + pi[-1] * ar]
    if descending:
        pr, pi = pr[::-1], pi[::-1]
    return jnp.concatenate([jnp.concatenate(pr, axis=0), jnp.concatenate(pi, axis=0)], axis=1)


_BIG = ["w_in", "s5_glu_w", "w_branch_hg", "w_branch_s5", "w_out", "w_up", "w_down", "w_ple_gate", "w_ple_proj", "conv_w"]
_BIG_AXIS = {"w_in": 1, "s5_glu_w": 0, "w_branch_hg": 1, "w_branch_s5": 1, "w_out": 0, "w_up": 1, "w_down": 0,
             "w_ple_gate": 0, "w_ple_proj": 1, "conv_w": 1}
_SMALL = ["norm_mix_g", "hg_lb_logits", "hg_norm_g", "s5_lambda_re", "s5_lambda_im", "s5_log_dt", "s5_b_re", "s5_b_im",
          "s5_c_re", "s5_c_im", "s5_d", "s5_glu_b", "norm_ffn_g", "conv_b", "norm_ple_g", "norm_final_g"]
_ORDER = ["norm_mix_g", "w_in", "hg_lb_logits", "hg_norm_g", "s5_lambda_re", "s5_lambda_im", "s5_log_dt", "s5_b_re",
          "s5_b_im", "s5_c_re", "s5_c_im", "s5_d", "s5_glu_w", "s5_glu_b", "w_branch_hg", "w_branch_s5", "w_out",
          "norm_ffn_g", "w_up", "conv_w", "conv_b", "w_down", "norm_ple_g", "w_ple_gate", "w_ple_proj", "norm_final_g"]


def kernel(x, p, norm_mix_g, w_in, hg_lb_logits, hg_norm_g, s5_lambda_re, s5_lambda_im, s5_log_dt, s5_b_re, s5_b_im, s5_c_re, s5_c_im, s5_d, s5_glu_w, s5_glu_b, w_branch_hg, w_branch_s5, w_out, norm_ffn_g, w_up, conv_w, conv_b, w_down, norm_ple_g, w_ple_gate, w_ple_proj, norm_final_g, loss_target, m_norm_mix_g, m_w_in, m_hg_lb_logits, m_hg_norm_g, m_s5_lambda_re, m_s5_lambda_im, m_s5_log_dt, m_s5_b_re, m_s5_b_im, m_s5_c_re, m_s5_c_im, m_s5_d, m_s5_glu_w, m_s5_glu_b, m_w_branch_hg, m_w_branch_s5, m_w_out, m_norm_ffn_g, m_w_up, m_conv_w, m_conv_b, m_w_down, m_norm_ple_g, m_w_ple_gate, m_w_ple_proj, m_norm_final_g, v_norm_mix_g, v_w_in, v_hg_lb_logits, v_hg_norm_g, v_s5_lambda_re, v_s5_lambda_im, v_s5_log_dt, v_s5_b_re, v_s5_b_im, v_s5_c_re, v_s5_c_im, v_s5_d, v_s5_glu_w, v_s5_glu_b, v_w_branch_hg, v_w_branch_s5, v_w_out, v_norm_ffn_g, v_w_up, v_conv_w, v_conv_b, v_w_down, v_norm_ple_g, v_w_ple_gate, v_w_ple_proj, v_norm_final_g):
    W = dict(norm_mix_g=norm_mix_g, w_in=w_in, hg_lb_logits=hg_lb_logits, hg_norm_g=hg_norm_g, s5_lambda_re=s5_lambda_re, s5_lambda_im=s5_lambda_im, s5_log_dt=s5_log_dt, s5_b_re=s5_b_re, s5_b_im=s5_b_im, s5_c_re=s5_c_re, s5_c_im=s5_c_im, s5_d=s5_d, s5_glu_w=s5_glu_w, s5_glu_b=s5_glu_b, w_branch_hg=w_branch_hg, w_branch_s5=w_branch_s5, w_out=w_out, norm_ffn_g=norm_ffn_g, w_up=w_up, conv_w=conv_w, conv_b=conv_b, w_down=w_down, norm_ple_g=norm_ple_g, w_ple_gate=w_ple_gate, w_ple_proj=w_ple_proj, norm_final_g=norm_final_g)
    M = dict(norm_mix_g=m_norm_mix_g, w_in=m_w_in, hg_lb_logits=m_hg_lb_logits, hg_norm_g=m_hg_norm_g, s5_lambda_re=m_s5_lambda_re, s5_lambda_im=m_s5_lambda_im, s5_log_dt=m_s5_log_dt, s5_b_re=m_s5_b_re, s5_b_im=m_s5_b_im, s5_c_re=m_s5_c_re, s5_c_im=m_s5_c_im, s5_d=m_s5_d, s5_glu_w=m_s5_glu_w, s5_glu_b=m_s5_glu_b, w_branch_hg=m_w_branch_hg, w_branch_s5=m_w_branch_s5, w_out=m_w_out, norm_ffn_g=m_norm_ffn_g, w_up=m_w_up, conv_w=m_conv_w, conv_b=m_conv_b, w_down=m_w_down, norm_ple_g=m_norm_ple_g, w_ple_gate=m_w_ple_gate, w_ple_proj=m_w_ple_proj, norm_final_g=m_norm_final_g)
    V = dict(norm_mix_g=v_norm_mix_g, w_in=v_w_in, hg_lb_logits=v_hg_lb_logits, hg_norm_g=v_hg_norm_g, s5_lambda_re=v_s5_lambda_re, s5_lambda_im=v_s5_lambda_im, s5_log_dt=v_s5_log_dt, s5_b_re=v_s5_b_re, s5_b_im=v_s5_b_im, s5_c_re=v_s5_c_re, s5_c_im=v_s5_c_im, s5_d=v_s5_d, s5_glu_w=v_s5_glu_w, s5_glu_b=v_s5_glu_b, w_branch_hg=v_w_branch_hg, w_branch_s5=v_w_branch_s5, w_out=v_w_out, norm_ffn_g=v_norm_ffn_g, w_up=v_w_up, conv_w=v_conv_w, conv_b=v_conv_b, w_down=v_w_down, norm_ple_g=v_norm_ple_g, w_ple_gate=v_w_ple_gate, w_ple_proj=v_w_ple_proj, norm_final_g=v_norm_final_g)

    mm_names = _BIG[:-1]
    shard2 = {n: W[n][0] for n in _BIG}
    conv_bits = lax.bitcast_convert_type(shard2["conv_w"], bf16)
    wpack = _pack([shard2[n] for n in mm_names] + [conv_bits], bf16, 16)
    gathered = _all_gather("ag_weights", wpack)
    pieces = _unpack(gathered, [shard2[n].shape for n in mm_names] + [conv_bits.shape])
    full = {n: _from_slabs(pc, _BIG_AXIS[n]) for n, pc in zip(mm_names, pieces[:-1])}
    conv_w_full = _from_slabs(lax.bitcast_convert_type(pieces[-1], f32), 1)

    xt = x[0]
    pt = p[0, 0]
    tgt = loss_target[0]
    T = xt.shape[0]
    lam_re, lam_im, log_dt = s5_lambda_re[0], s5_lambda_im[0], s5_log_dt[0]
    b_re, b_im, c_re, c_im = s5_b_re[0], s5_b_im[0], s5_c_re[0], s5_c_im[0]

    def s5_prep(lam_re, lam_im, log_dt, b_re, b_im, c_re, c_im):
        a_re, a_im, bbar_re, bbar_im = _s5_discretise(lam_re, lam_im, log_dt, b_re, b_im)
        bdb, bdc = _s5_operands(bbar_re, bbar_im, c_re, c_im)
        return a_re, a_im, bdb, bdc

    (a_re, a_im, bdb, bdc), s5_prep_vjp = jax.vjp(s5_prep, lam_re, lam_im, log_dt, b_re, b_im, c_re, c_im)
    pw = _s5_powers(a_re, a_im)
    pw_rev = _s5_powers(a_re, a_im, descending=True)
    bdb_b, bdc_b = bdb.astype(bf16), bdc.astype(bf16)

    h1, proj_hg, u_raw, gates = _in_proj(xt, norm_mix_g, full["w_in"])
    ng4 = jnp.tile(hg_norm_g, (1, NH))
    hg_o, sprev = _hgrn_fwd(proj_hg, hg_lb_logits, ng4)
    x_st, y_s5, g_s5, s5_o = _s5_fwd(u_raw, bdb_b, bdc_b, pw, s5_d, full["s5_glu_w"], s5_glu_b)
    x1, merged, h2, a_up = _mix_up(xt, hg_o, s5_o, gates, full["w_branch_hg"], full["w_branch_s5"], full["w_out"],
                                   norm_ffn_g, full["w_up"])
    (dx2, gated, g_a, g_b, h3, dz_ple, dpp, loss_part, d_norm_final, d_norm_ple) = _ffn_tail(
        a_up, conv_w_full, conv_b, full["w_down"], x1, pt, norm_ple_g, full["w_ple_gate"], full["w_ple_proj"],
        norm_final_g.reshape(1, D), tgt)

    da_up, dx1, d_conv_w, d_conv_b, d_norm_ffn = _ffn_bwd(dx2, a_up, g_a, g_b, conv_w_full, full["w_down"],
                                                           full["w_up"], x1, norm_ffn_g)
    d_gates, d_hg_o, d_s5_o, dyh, dys = _mix_bwd(dx1, hg_o, s5_o, gates, full["w_branch_hg"], full["w_branch_s5"],
                                                   full["w_out"])
    d_proj_hg, d_lb, d_hg_norm = _hgrn_bwd(proj_hg, hg_lb_logits, ng4, sprev, d_hg_o)
    d_u, dz_glu, d_a_re, d_a_im, d_s5_d, d_glu_b, d_bdb, d_bdc = _s5_bwd(
        d_s5_o, y_s5, u_raw, x_st, bdb_b, bdc_b, pw_rev, s5_d, full["s5_glu_w"], s5_glu_b)
    grad_x, d_norm_mix = _in_bwd(d_proj_hg, d_u, d_gates, xt, dx1, full["w_in"], norm_mix_g)

    gw = {}
    gw["w_in"] = jnp.concatenate([_wgrad("wg_in_hg", h1, d_proj_hg), _wgrad("wg_in_u", h1, d_u),
                                  _wgrad("wg_in_gates", h1, d_gates)], axis=1)
    gw["s5_glu_w"] = _wgrad("wg_glu", g_s5, dz_glu)
    gw["w_branch_hg"] = _wgrad("wg_bhg", hg_o, dyh)
    gw["w_branch_s5"] = _wgrad("wg_bs5", s5_o, dys)
    gw["w_out"] = _wgrad("wg_out", merged, dx1)
    gw["w_up"] = _wgrad("wg_up", h2, da_up)
    gw["w_down"] = _wgrad("wg_down", gated, dx2)
    gw["w_ple_gate"] = _wgrad("wg_pg", h3, dz_ple)
    gw["w_ple_proj"] = _wgrad("wg_pp", pt, dpp)
    gw["conv_w"] = d_conv_w
    (d_lam_re, d_lam_im, d_log_dt, d_b_re, d_b_im, d_c_re, d_c_im) = s5_prep_vjp(
        (d_a_re.reshape(SG, SN), d_a_im.reshape(SG, SN), d_bdb, d_bdc))
    sm = jax.nn.softmax(hg_lb_logits, axis=0)
    d_l0 = d_lb[0] * sm[0] * sm[1]
    d_logits = jnp.stack([d_l0, -d_l0], axis=0)

    gs = {"norm_mix_g": d_norm_mix, "hg_lb_logits": d_logits, "hg_norm_g": d_hg_norm, "s5_lambda_re": d_lam_re,
          "s5_lambda_im": d_lam_im, "s5_log_dt": d_log_dt, "s5_b_re": d_b_re, "s5_b_im": d_b_im, "s5_c_re": d_c_re,
          "s5_c_im": d_c_im, "s5_d": d_s5_d, "s5_glu_b": d_glu_b, "norm_ffn_g": d_norm_ffn, "conv_b": d_conv_b,
          "norm_ple_g": d_norm_ple, "norm_final_g": d_norm_final}

    ids = jnp.stack([lax.axis_index("c"), 2 * lax.axis_index("x") + lax.axis_index("y")]).astype(jnp.int32)
    g4 = [_to_slabs(gw[n], _BIG_AXIS[n]).reshape(4, 2, *shard2[n].shape) for n in _BIG]
    got = _swap_sibling(g4)
    sums = [_add_halves("rs_add_" + n, g, r, ids) for n, g, r in zip(_BIG, g4, got)]
    got3 = _swap_chips([p16 for p16, _ in sums])
    big_out = [_adam_shard("adam_" + n, own, r3, W[n], M[n], V[n]) for n, (_, own), r3 in zip(_BIG, sums, got3)]

    two_d = lambda a: a.reshape(1, -1) if a.ndim == 1 else a
    small_out = _allreduce_adam([two_d(gs[n].reshape(W[n].shape)) for n in _SMALL], [two_d(W[n]) for n in _SMALL],
                                [two_d(M[n]) for n in _SMALL], [two_d(V[n]) for n in _SMALL])

    res = {}
    for k in range(4):
        d = {n: big_out[i][k] for i, n in enumerate(_BIG)}
        d.update({n: small_out[k * len(_SMALL) + i].reshape(W[n].shape) for i, n in enumerate(_SMALL)})
        res[k] = d
    loss = lax.psum(loss_part[0, 0], ("x", "y", "c"))
    return (loss, grad_x[None], *[res[0][n] for n in _ORDER], *[res[1][n] for n in _ORDER],
            *[res[2][n] for n in _ORDER], *[res[3][n] for n in _ORDER])
```

```python
import functools
import math

import jax
import jax.numpy as jnp
from jax import lax
from jax.experimental import pallas as pl
from jax.experimental.pallas import tpu as pltpu

f32 = jnp.float32
bf16 = jnp.bfloat16
MESH = pl.DeviceIdType.MESH

N_DEV = 8
D = 1024
HW = 512
HD = 128
NH = 4
CH = 64
SW = 512
SG = 32
SP = 16
SN = 64
SL = SG * SN
NST = 4
STW = SL // NST
DFF = 2816
PLE = 256
EPS = 1e-6
LANES = 1024
VMEM_LIMIT = 56 * 1024 * 1024

ADAM_LR, ADAM_B1, ADAM_B2, ADAM_EPS, ADAM_WD, ADAM_STEP = 0.001, 0.9, 0.999, 1e-08, 0.01, 10


def _pc(body, **kw):
    return pl.pallas_call(body, **kw)


def _params(n_axes=1, **kw):
    return pltpu.CompilerParams(dimension_semantics=("arbitrary",) * n_axes, vmem_limit_bytes=VMEM_LIMIT, **kw)


def _whole(shape):
    nd = len(shape)
    return pl.BlockSpec(shape, lambda *_: (0,) * nd, pipeline_mode=pl.Buffered(1))


def _acc(shape):
    nd = len(shape)
    return pl.BlockSpec(shape, lambda *_: (0,) * nd)


def _dot(a, b):
    return jnp.dot(a.astype(bf16), b.astype(bf16), preferred_element_type=f32)


def _dot_nt(a, b):
    return lax.dot_general(a.astype(bf16), b.astype(bf16), (((1,), (1,)), ((), ())), preferred_element_type=f32)


def _dot_tn(a, b):
    return lax.dot_general(a.astype(bf16), b.astype(bf16), (((0,), (0,)), ((), ())), preferred_element_type=f32)


def _sig(x):
    return jax.nn.sigmoid(x)


def _dsilu(z, s):
    return s * (1.0 + z * (1.0 - s))


_GC = math.sqrt(2.0 / math.pi)


def _gelu_and_grad(y):
    t = jnp.tanh(_GC * (y + 0.044715 * y * y * y))
    g = 0.5 * y * (1.0 + t)
    dg = 0.5 * (1.0 + t) + 0.5 * y * (1.0 - t * t) * _GC * (1.0 + 3.0 * 0.044715 * y * y)
    return g, dg


def _rms(x):
    r = lax.rsqrt(jnp.mean(x * x, axis=-1, keepdims=True) + EPS)
    return x * r, r


def _rms_bwd(dy, xh, r, g):
    dxh = dy * g
    dx = r * (dxh - xh * jnp.mean(dxh * xh, axis=-1, keepdims=True))
    return dx, dy * xh


def _colsum(x):
    return jnp.sum(x, axis=0, keepdims=True)


def _in_proj(x, g, w):
    T = x.shape[0]
    tm = 256

    def body(x_ref, g_ref, w_ref, h_ref, hg_ref, u_ref, gt_ref):
        xh, _ = _rms(x_ref[...])
        h = (xh * g_ref[...]).astype(bf16)
        h_ref[...] = h
        hg_ref[...] = jnp.dot(h, w_ref[:, 0:4 * HW], preferred_element_type=f32)
        u_ref[...] = jnp.dot(h, w_ref[:, 4 * HW:4 * HW + SW], preferred_element_type=f32)
        gt_ref[...] = jnp.dot(h, w_ref[:, 4 * HW + SW:], preferred_element_type=f32)

    row = lambda n: pl.BlockSpec((tm, n), lambda i: (i, 0))
    return _pc(
        body, name="in_proj", grid=(T // tm,),
        in_specs=[row(D), _whole((1, D)), _whole(w.shape)],
        out_specs=[row(D), row(4 * HW), row(SW), row(2 * D)],
        out_shape=[jax.ShapeDtypeStruct((T, D), bf16), jax.ShapeDtypeStruct((T, 4 * HW), f32),
                   jax.ShapeDtypeStruct((T, SW), f32), jax.ShapeDtypeStruct((T, 2 * D), f32)],
        compiler_params=_params(),
    )(x, g, w)


HG_NC = 2


def _tri_matmul(tri, x):
    hi = x.astype(bf16)
    r1 = x - hi.astype(f32)
    mid = r1.astype(bf16)
    lo = (r1 - mid.astype(f32)).astype(bf16)
    n = x.shape[1]
    out = jnp.dot(tri.astype(bf16), jnp.concatenate([hi, mid, lo], axis=1), preferred_element_type=f32)
    return out[:, 0:n] + out[:, n:2 * n] + out[:, 2 * n:3 * n]


def _hgrn_gates(lg, qr, fr):
    mx = jnp.max(lg, axis=0, keepdims=True)
    e = jnp.exp(lg - mx)
    lb = e[0:1, :] / (e[0:1, :] + e[1:2, :])
    sig = _sig(fr)
    f = lb + (1.0 - lb) * sig
    k = 1.0 - f
    r_i = lax.broadcasted_iota(jnp.int32, (CH, CH), 0)
    c_i = lax.broadcasted_iota(jnp.int32, (CH, CH), 1)
    tril = (r_i >= c_i)
    b = _tri_matmul(tril, jnp.log(f))
    bref = b[CH // 2:CH // 2 + 1, :]
    blast = b[CH - 1:CH, :]
    sq = _sig(qr)
    q = qr * sq
    e1 = jnp.exp(b - bref)
    e2 = jnp.exp(bref - b)
    e3 = jnp.exp(blast - b)
    e4 = jnp.exp(b)
    return dict(lb=lb, qr=qr, sq=sq, sig=sig, f=f, k=k, tril=tril, triu=(c_i >= r_i), e1=e1, e2=e2, e3=e3, e4=e4,
                qs=q * e1, ks=k * e2, kl=k * e3, qb=q * e4, dec=jnp.exp(blast))


def _hgrn_fwd(proj_hg, logits, ng4):
    T = proj_hg.shape[0]
    nch = T // CH
    tm = HG_NC * CH

    def body(q_ref, f_ref, i_ref, og_ref, lg_ref, ng_ref, out_ref, sprev_ref, st_ref):
        @pl.when(pl.program_id(0) == 0)
        def _():
            st_ref[...] = jnp.zeros_like(st_ref)

        for ci in range(HG_NC):
            rows = slice(ci * CH, (ci + 1) * CH)
            c = _hgrn_gates(lg_ref[...], q_ref[rows, :], f_ref[rows, :])
            v = i_ref[rows, :]
            og = og_ref[rows, :]
            ohs = []
            for h in range(NH):
                sl = slice(h * HD, (h + 1) * HD)
                p = jnp.where(c["tril"], _dot_nt(c["qs"][:, sl], c["ks"][:, sl]), 0.0)
                st = st_ref[h]
                sprev_ref[ci, h] = st
                o = _dot(p, v[:, sl]) + _dot_nt(c["qb"][:, sl], st)
                st_ref[h] = c["dec"][:, sl] * st + _dot_tn(v[:, sl], c["kl"][:, sl])
                ohs.append(_rms(o)[0])
            out_ref[rows, :] = (jnp.concatenate(ohs, axis=1) * ng_ref[...] * (og * _sig(og))).astype(bf16)

    col = lambda j: pl.BlockSpec((tm, HW), lambda n, j=j: (n, j))
    return _pc(
        body, name="hgrn_fwd", grid=(nch // HG_NC,),
        in_specs=[col(0), col(1), col(2), col(3), _whole((2, HW)), _whole((1, HW))],
        out_specs=[pl.BlockSpec((tm, HW), lambda n: (n, 0)),
                   pl.BlockSpec((HG_NC, NH, HD, HD), lambda n: (n, 0, 0, 0))],
        out_shape=[jax.ShapeDtypeStruct((T, HW), bf16), jax.ShapeDtypeStruct((nch, NH, HD, HD), f32)],
        scratch_shapes=[pltpu.VMEM((NH, HD, HD), f32)],
        compiler_params=_params(),
    )(proj_hg, proj_hg, proj_hg, proj_hg, logits, ng4)


def _hgrn_bwd(proj_hg, logits, ng4, sprev, d_out):
    T = proj_hg.shape[0]
    nch = T // CH
    tm = HG_NC * CH
    nst = nch // HG_NC

    def body(q_ref, f_ref, i_ref, og_ref, lg_ref, ng_ref, sp_ref, do_ref, dp_ref, dlb_ref, dng_ref, gt_ref):
        @pl.when(pl.program_id(0) == 0)
        def _():
            gt_ref[...] = jnp.zeros_like(gt_ref)
            dlb_ref[...] = jnp.zeros_like(dlb_ref)
            dng_ref[...] = jnp.zeros_like(dng_ref)

        row = lax.broadcasted_iota(jnp.int32, (CH, HW), 0)
        for ci in reversed(range(HG_NC)):
            rows = slice(ci * CH, (ci + 1) * CH)
            c = _hgrn_gates(lg_ref[...], q_ref[rows, :], f_ref[rows, :])
            tril = c["tril"]
            v = i_ref[rows, :]
            og = og_ref[rows, :]
            sog = _sig(og)
            d_gated = do_ref[rows, :]
            d_on_all = d_gated * (og * sog)
            parts = {n: [] for n in ("dqs", "dks", "dkl", "dqb", "dv", "ohg", "ddec")}
            for h in range(NH):
                sl = slice(h * HD, (h + 1) * HD)
                ng_h = ng_ref[:, sl]
                qs, ks, kl, qb, vh = c["qs"][:, sl], c["ks"][:, sl], c["kl"][:, sl], c["qb"][:, sl], v[:, sl]
                st = sp_ref[ci, h]
                gt = gt_ref[h]
                p = jnp.where(tril, _dot_nt(qs, ks), 0.0)
                o = _dot(p, vh) + _dot_nt(qb, st)
                oh, r = _rms(o)
                d_o, dng_rows = _rms_bwd(d_on_all[:, sl], oh, r, ng_h)
                dng_ref[...] += _colsum(dng_rows)
                dp = jnp.where(tril, _dot_nt(d_o, vh), 0.0)
                parts["dqb"].append(_dot(d_o, st))
                parts["dv"].append(_dot_tn(p, d_o) + _dot_nt(kl, gt))
                parts["dqs"].append(_dot(dp, ks))
                parts["dks"].append(_dot_tn(dp, qs))
                parts["dkl"].append(_dot(vh, gt))
                parts["ddec"].append(_colsum(gt * st))
                parts["ohg"].append(oh * ng_h)
                gt_ref[h] = _dot_tn(d_o, qb) + c["dec"][:, sl] * gt
            cat = {n: jnp.concatenate(vs, axis=1) for n, vs in parts.items()}
            dqs, dks, dkl, dqb = cat["dqs"], cat["dks"], cat["dkl"], cat["dqb"]
            dq = dqs * c["e1"] + dqb * c["e4"]
            dk = dks * c["e2"] + dkl * c["e3"]
            t_qs = dqs * c["qs"]
            t_ks = dks * c["ks"]
            t_kl = dkl * c["kl"]
            db = t_qs - t_ks - t_kl + dqb * c["qb"]
            dbref = _colsum(t_ks - t_qs)
            dblast = _colsum(t_kl) + cat["ddec"] * c["dec"]
            db = db + jnp.where(row == CH // 2, dbref, 0.0) + jnp.where(row == CH - 1, dblast, 0.0)
            df = _tri_matmul(c["triu"], db) / c["f"] - dk
            sig = c["sig"]
            dlb_ref[...] += _colsum(df * (1.0 - sig))
            dp_ref[rows, 0:HW] = dq * _dsilu(c["qr"], c["sq"])
            dp_ref[rows, HW:2 * HW] = df * (1.0 - c["lb"]) * sig * (1.0 - sig)
            dp_ref[rows, 2 * HW:3 * HW] = cat["dv"]
            dp_ref[rows, 3 * HW:4 * HW] = d_gated * cat["ohg"] * _dsilu(og, sog)

    rev = lambda n: nst - 1 - n
    col = lambda j: pl.BlockSpec((tm, HW), lambda n, j=j: (rev(n), j))
    return _pc(
        body, name="hgrn_bwd", grid=(nst,),
        in_specs=[col(0), col(1), col(2), col(3), _whole((2, HW)), _whole((1, HW)),
                  pl.BlockSpec((HG_NC, NH, HD, HD), lambda n: (rev(n), 0, 0, 0)),
                  pl.BlockSpec((tm, HW), lambda n: (rev(n), 0))],
        out_specs=[pl.BlockSpec((tm, 4 * HW), lambda n: (rev(n), 0)), _acc((1, HW)), _acc((1, HD))],
        out_shape=[jax.ShapeDtypeStruct((T, 4 * HW), f32), jax.ShapeDtypeStruct((1, HW), f32),
                   jax.ShapeDtypeStruct((1, HD), f32)],
        scratch_shapes=[pltpu.VMEM((NH, HD, HD), f32)],
        compiler_params=_params(),
    )(proj_hg, proj_hg, proj_hg, proj_hg, logits, ng4, sprev, d_out)


S5_TM = 256
S5_SEG = 8
S5_STEPS = S5_TM // S5_SEG
NLT = SL // 128


def _s5_tables(a_ref, pw_ref, pseg_ref, descending):
    re, im = slice(0, SL), slice(SL, 2 * SL)

    def cmul(ar, ai, br, bi):
        return ar * br - ai * bi, ar * bi + ai * br

    pw_ref[0:1, :] = a_ref[...]
    m = 1
    while m < S5_STEPS:
        pr, pi = cmul(pw_ref[0:m, re], pw_ref[0:m, im], pw_ref[m - 1:m, re], pw_ref[m - 1:m, im])
        pw_ref[m:2 * m, re] = pr
        pw_ref[m:2 * m, im] = pi
        m *= 2
    base = S5_STEPS - 1
    if descending:
        pseg_ref[7:8, :] = pw_ref[base:base + 1, :]
        m = 1
        while m < 8:
            pr, pi = cmul(pseg_ref[8 - m:8, re], pseg_ref[8 - m:8, im], pseg_ref[8 - m:9 - m, re], pseg_ref[8 - m:9 - m, im])
            pseg_ref[8 - 2 * m:8 - m, re] = pr
            pseg_ref[8 - 2 * m:8 - m, im] = pi
            m *= 2
    else:
        pseg_ref[0:1, :] = pw_ref[base:base + 1, :]
        m = 1
        while m < 8:
            pr, pi = cmul(pseg_ref[0:m, re], pseg_ref[0:m, im], pseg_ref[m - 1:m, re], pseg_ref[m - 1:m, im])
            pseg_ref[m:2 * m, re] = pr
            pseg_ref[m:2 * m, im] = pi
            m *= 2


def _seg_rows(j):
    return pl.ds(j, S5_SEG, stride=S5_STEPS)


def _scan_fwd(x3_ref, pw_ref, pseg_ref, carry_ref):
    row8 = lax.broadcasted_iota(jnp.int32, (S5_SEG, 128), 0)
    for lt in range(NLT):
        kr, ki = lt, NLT + lt
        lr, li = slice(lt * 128, (lt + 1) * 128), slice(SL + lt * 128, SL + (lt + 1) * 128)
        ar, ai = pw_ref[0:1, lr], pw_ref[0:1, li]
        sr = jnp.zeros((S5_SEG, 128), f32)
        si = jnp.zeros((S5_SEG, 128), f32)
        for j in range(S5_STEPS):
            sr, si = ar * sr - ai * si + x3_ref[kr, _seg_rows(j), :], ar * si + ai * sr + x3_ref[ki, _seg_rows(j), :]
            x3_ref[kr, _seg_rows(j), :] = sr
            x3_ref[ki, _seg_rows(j), :] = si
        for d in (1, 2, 4):
            pr, pi = pseg_ref[d - 1:d, lr], pseg_ref[d - 1:d, li]
            tr, ti = pltpu.roll(sr, d, 0), pltpu.roll(si, d, 0)
            m = row8 >= d
            sr, si = sr + jnp.where(m, pr * tr - pi * ti, 0.0), si + jnp.where(m, pr * ti + pi * tr, 0.0)
        c0r, c0i = carry_ref[7:8, lr], carry_ref[7:8, li]
        qr, qi = pseg_ref[:, lr], pseg_ref[:, li]
        sr, si = sr + qr * c0r - qi * c0i, si + qr * c0i + qi * c0r
        carry_ref[:, lr] = sr
        carry_ref[:, li] = si
        cr = jnp.where(row8 == 0, c0r, pltpu.roll(sr, 1, 0))
        ci = jnp.where(row8 == 0, c0i, pltpu.roll(si, 1, 0))
        for j in range(S5_STEPS):
            pr, pi = pw_ref[j:j + 1, lr], pw_ref[j:j + 1, li]
            x3_ref[kr, _seg_rows(j), :] = x3_ref[kr, _seg_rows(j), :] + pr * cr - pi * ci
            x3_ref[ki, _seg_rows(j), :] = x3_ref[ki, _seg_rows(j), :] + pr * ci + pi * cr


def _scan_bwd(g3_ref, x3_ref, xh_ref, first, pw_ref, pseg_ref, carry_ref, dar_ref, dai_ref):
    row8 = lax.broadcasted_iota(jnp.int32, (S5_SEG, 128), 0)
    for lt in range(NLT):
        kr, ki = lt, NLT + lt
        lr, li = slice(lt * 128, (lt + 1) * 128), slice(SL + lt * 128, SL + (lt + 1) * 128)
        ar, ai = pw_ref[0:1, lr], pw_ref[0:1, li]
        sr = jnp.zeros((S5_SEG, 128), f32)
        si = jnp.zeros((S5_SEG, 128), f32)
        for j in reversed(range(S5_STEPS)):
            sr, si = ar * sr + ai * si + g3_ref[kr, _seg_rows(j), :], ar * si - ai * sr + g3_ref[ki, _seg_rows(j), :]
            g3_ref[kr, _seg_rows(j), :] = sr
            g3_ref[ki, _seg_rows(j), :] = si
        for d in (1, 2, 4):
            pr, pi = pseg_ref[8 - d:9 - d, lr], pseg_ref[8 - d:9 - d, li]
            tr, ti = pltpu.roll(sr, 8 - d, 0), pltpu.roll(si, 8 - d, 0)
            m = row8 < 8 - d
            sr, si = sr + jnp.where(m, pr * tr + pi * ti, 0.0), si + jnp.where(m, pr * ti - pi * tr, 0.0)
        c0r, c0i = carry_ref[0:1, lr], carry_ref[0:1, li]
        qr, qi = pseg_ref[:, lr], pseg_ref[:, li]
        sr, si = sr + qr * c0r + qi * c0i, si + qr * c0i - qi * c0r
        carry_ref[:, lr] = sr
        carry_ref[:, li] = si
        cr = jnp.where(row8 == 7, c0r, pltpu.roll(sr, 7, 0))
        ci = jnp.where(row8 == 7, c0i, pltpu.roll(si, 7, 0))
        hr = jnp.where(first, 0.0, xh_ref[kr, 7:8, :])
        hi = jnp.where(first, 0.0, xh_ref[ki, 7:8, :])
        acc_r = jnp.zeros((S5_SEG, 128), f32)
        acc_i = jnp.zeros((S5_SEG, 128), f32)
        for j in range(S5_STEPS):
            pr, pi = pw_ref[S5_STEPS - 1 - j:S5_STEPS - j, lr], pw_ref[S5_STEPS - 1 - j:S5_STEPS - j, li]
            lam_r = g3_ref[kr, _seg_rows(j), :] + pr * cr + pi * ci
            lam_i = g3_ref[ki, _seg_rows(j), :] + pr * ci - pi * cr
            g3_ref[kr, _seg_rows(j), :] = lam_r
            g3_ref[ki, _seg_rows(j), :] = lam_i
            if j == 0:
                xpr = jnp.where(row8 == 0, hr, pltpu.roll(x3_ref[kr, _seg_rows(S5_STEPS - 1), :], 1, 0))
                xpi = jnp.where(row8 == 0, hi, pltpu.roll(x3_ref[ki, _seg_rows(S5_STEPS - 1), :], 1, 0))
            else:
                xpr = x3_ref[kr, _seg_rows(j - 1), :]
                xpi = x3_ref[ki, _seg_rows(j - 1), :]
            acc_r = acc_r + lam_r * xpr + lam_i * xpi
            acc_i = acc_i + lam_i * xpr - lam_r * xpi
        dar_ref[:, lr] += _colsum(acc_r)
        dai_ref[:, lr] += _colsum(acc_i)


def _strip(x3_ref, part, s):
    k0 = part * NLT + s * (STW // 128)
    return jnp.concatenate([x3_ref[k0 + q] for q in range(STW // 128)], axis=1)


def _s5_fwd(u, a_row, bdb, bdc, dskip, glu_w, glu_b):
    T = u.shape[0]
    tm = S5_TM

    def body(u_ref, a_ref, bdb_ref, bdc_ref, ds_ref, gw_ref, gb_ref, x_ref, y_ref, g_ref, o_ref, pw_ref, pseg_ref, carry_ref):
        @pl.when(pl.program_id(0) == 0)
        def _():
            carry_ref[...] = jnp.zeros_like(carry_ref)
            _s5_tables(a_ref, pw_ref, pseg_ref, descending=False)

        uv = u_ref[...]
        ub = uv.astype(bf16)
        for part in range(2):
            for s in range(NST):
                bu = jnp.dot(ub[:, s * 128:(s + 1) * 128], bdb_ref[part * NST + s], preferred_element_type=f32)
                for q in range(STW // 128):
                    x_ref[part * NLT + s * (STW // 128) + q] = bu[:, q * 128:(q + 1) * 128]
        _scan_fwd(x_ref, pw_ref, pseg_ref, carry_ref)
        ys = []
        for s in range(NST):
            acc = None
            for part in range(2):
                t = jnp.dot(_strip(x_ref, part, s).astype(bf16), bdc_ref[part * NST + s], preferred_element_type=f32)
                acc = t if acc is None else acc + t
            ys.append(acc)
        y = jnp.concatenate(ys, axis=1) + ds_ref[...] * uv
        y_ref[...] = y
        g, _ = _gelu_and_grad(y)
        gb = g.astype(bf16)
        g_ref[...] = gb
        z = jnp.dot(gb, gw_ref[...], preferred_element_type=f32) + gb_ref[...]
        o_ref[...] = (g * _sig(z)).astype(bf16)

    row = lambda n: pl.BlockSpec((tm, n), lambda i: (i, 0))
    return _pc(
        body, name="s5_fwd", grid=(T // tm,),
        in_specs=[row(SW), _whole((1, 2 * SL)), _whole(bdb.shape), _whole(bdc.shape), _whole((1, SW)),
                  _whole((SW, SW)), _whole((1, SW))],
        out_specs=[pl.BlockSpec((2 * NLT, tm, 128), lambda i: (0, i, 0)), row(SW), row(SW), row(SW)],
        out_shape=[jax.ShapeDtypeStruct((2 * NLT, T, 128), f32), jax.ShapeDtypeStruct((T, SW), f32),
                   jax.ShapeDtypeStruct((T, SW), bf16), jax.ShapeDtypeStruct((T, SW), bf16)],
        scratch_shapes=[pltpu.VMEM((S5_STEPS, 2 * SL), f32), pltpu.VMEM((8, 2 * SL), f32), pltpu.VMEM((8, 2 * SL), f32)],
        compiler_params=_params(),
    )(u, a_row, bdb, bdc, dskip, glu_w, glu_b)


def _s5_bwd(d_out, y, u, x, a_row, bdb, bdc, dskip, glu_w, glu_b):
    T = u.shape[0]
    tm = S5_TM
    nt = T // tm

    def body(do_ref, y_ref, u_ref, x_ref, xh_ref, a_ref, bdb_ref, bdc_ref, ds_ref, gw_ref, gb_ref,
             du_ref, dz_ref, dar_ref, dai_ref, dd_ref, dgb_ref, dbdb_ref, dbdc_ref, gs_ref, pw_ref, pseg_ref, carry_ref):
        i = pl.program_id(0)

        @pl.when(i == 0)
        def _():
            carry_ref[...] = jnp.zeros_like(carry_ref)
            _s5_tables(a_ref, pw_ref, pseg_ref, descending=True)
            dar_ref[...] = jnp.zeros_like(dar_ref)
            dai_ref[...] = jnp.zeros_like(dai_ref)
            dd_ref[...] = jnp.zeros_like(dd_ref)
            dgb_ref[...] = jnp.zeros_like(dgb_ref)
            dbdb_ref[...] = jnp.zeros_like(dbdb_ref)
            dbdc_ref[...] = jnp.zeros_like(dbdc_ref)

        yv = y_ref[...]
        uv = u_ref[...]
        g, gp = _gelu_and_grad(yv)
        z = jnp.dot(g.astype(bf16), gw_ref[...], preferred_element_type=f32) + gb_ref[...]
        sg = _sig(z)
        do = do_ref[...].astype(f32)
        dz = do * g * sg * (1.0 - sg)
        dz_ref[...] = dz.astype(bf16)
        dgb_ref[...] += _colsum(dz)
        dy = (do * sg + _dot_nt(dz, gw_ref[...])) * gp
        dyb = dy.astype(bf16)
        dd_ref[...] += _colsum(dy * uv)
        for part in range(2):
            for s in range(NST):
                gx = lax.dot_general(dyb[:, s * 128:(s + 1) * 128], bdc_ref[part * NST + s], (((1,), (1,)), ((), ())),
                                     preferred_element_type=f32)
                for q in range(STW // 128):
                    gs_ref[part * NLT + s * (STW // 128) + q] = gx[:, q * 128:(q + 1) * 128]
        _scan_bwd(gs_ref, x_ref, xh_ref, i == nt - 1, pw_ref, pseg_ref, carry_ref, dar_ref, dai_ref)
        ub = uv.astype(bf16)
        dus = []
        for s in range(NST):
            acc = None
            for part in range(2):
                lv = _strip(gs_ref, part, s).astype(bf16)
                t = lax.dot_general(lv, bdb_ref[part * NST + s], (((1,), (1,)), ((), ())), preferred_element_type=f32)
                acc = t if acc is None else acc + t
                dbdb_ref[part * NST + s] += _dot_tn(ub[:, s * 128:(s + 1) * 128], lv)
                dbdc_ref[part * NST + s] += _dot_tn(_strip(x_ref, part, s), dyb[:, s * 128:(s + 1) * 128])
            dus.append(acc)
        du_ref[...] = jnp.concatenate(dus, axis=1) + dy * ds_ref[...]

    rev = lambda i: nt - 1 - i
    row = lambda n: pl.BlockSpec((tm, n), lambda i: (rev(i), 0))
    xblk = pl.BlockSpec((2 * NLT, tm, 128), lambda i: (0, rev(i), 0))
    halo = pl.BlockSpec((2 * NLT, 8, 128), lambda i: (0, jnp.maximum(rev(i) * (tm // 8) - 1, 0), 0))
    return _pc(
        body, name="s5_bwd", grid=(nt,),
        in_specs=[row(SW), row(SW), row(SW), xblk, halo, _whole((1, 2 * SL)), _whole(bdb.shape), _whole(bdc.shape),
                  _whole((1, SW)), _whole((SW, SW)), _whole((1, SW))],
        out_specs=[row(SW), row(SW), _acc((1, SL)), _acc((1, SL)), _acc((1, SW)), _acc((1, SW)),
                   _acc(bdb.shape), _acc(bdc.shape)],
        out_shape=[jax.ShapeDtypeStruct((T, SW), f32), jax.ShapeDtypeStruct((T, SW), bf16),
                   jax.ShapeDtypeStruct((1, SL), f32), jax.ShapeDtypeStruct((1, SL), f32),
                   jax.ShapeDtypeStruct((1, SW), f32), jax.ShapeDtypeStruct((1, SW), f32),
                   jax.ShapeDtypeStruct(bdb.shape, f32), jax.ShapeDtypeStruct(bdc.shape, f32)],
        scratch_shapes=[pltpu.VMEM((2 * NLT, tm, 128), f32), pltpu.VMEM((S5_STEPS, 2 * SL), f32),
                        pltpu.VMEM((8, 2 * SL), f32), pltpu.VMEM((8, 2 * SL), f32)],
        compiler_params=_params(),
    )(d_out, y, u, x, x, a_row, bdb, bdc, dskip, glu_w, glu_b)


def _mix_up(x, hg_o, s5_o, gates, w_bhg, w_bs5, w_out, g_ffn, w_up):
    T = x.shape[0]
    tm = 256

    def body(x_ref, hg_ref, s5_ref, gt_ref, wh_ref, ws_ref, wo_ref, g_ref, wu_ref, x1_ref, mg_ref, h2_ref, a_ref):
        yh = jnp.dot(hg_ref[...], wh_ref[...], preferred_element_type=f32)
        ys = jnp.dot(s5_ref[...], ws_ref[...], preferred_element_type=f32)
        merged = (_sig(gt_ref[:, 0:D]) * yh + _sig(gt_ref[:, D:2 * D]) * ys).astype(bf16)
        mg_ref[...] = merged
        x1 = x_ref[...] + jnp.dot(merged, wo_ref[...], preferred_element_type=f32)
        x1_ref[...] = x1
        xh, _ = _rms(x1)
        h2 = (xh * g_ref[...]).astype(bf16)
        h2_ref[...] = h2
        a_ref[...] = jnp.dot(h2, wu_ref[...], preferred_element_type=f32)

    row = lambda n: pl.BlockSpec((tm, n), lambda i: (i, 0))
    return _pc(
        body, name="mix_up", grid=(T // tm,),
        in_specs=[row(D), row(HW), row(SW), row(2 * D), _whole(w_bhg.shape), _whole(w_bs5.shape), _whole(w_out.shape),
                  _whole((1, D)), _whole(w_up.shape)],
        out_specs=[row(D), row(D), row(D), row(2 * DFF)],
        out_shape=[jax.ShapeDtypeStruct((T, D), f32), jax.ShapeDtypeStruct((T, D), bf16),
                   jax.ShapeDtypeStruct((T, D), bf16), jax.ShapeDtypeStruct((T, 2 * DFF), f32)],
        compiler_params=_params(),
    )(x, hg_o, s5_o, gates, w_bhg, w_bs5, w_out, g_ffn, w_up)


FFN_TM = 128


def _conv_rows(a, halo, first, conv_w, conv_b):
    tm = a.shape[0]
    row = lax.broadcasted_iota(jnp.int32, (tm, 1), 0)
    hm1 = jnp.where(first, 0.0, halo[7:8, :])
    hm2 = jnp.where(first, 0.0, halo[6:7, :])
    a1 = jnp.where(row == 0, hm1, pltpu.roll(a, 1, 0))
    a2 = jnp.where(row == 0, hm2, jnp.where(row == 1, hm1, pltpu.roll(a, 2, 0)))
    c = conv_b + conv_w[0:1, :] * a2 + conv_w[1:2, :] * a1 + conv_w[2:3, :] * a
    return c, a1, a2


def _ffn_tail(a, conv_w, conv_b, w_down, x1, p, g_ple, w_pg, w_pp, g_fin, tgt):
    T = a.shape[0]
    tm = FFN_TM

    def body(a_ref, ah_ref, cw_ref, cb_ref, wd_ref, x1_ref, p_ref, gp_ref, wpg_ref, wpp_ref, gf_ref, t_ref,
             dx2_ref, gd_ref, ga_ref, gb_ref, h3_ref, dz_ref, dpp_ref, loss_ref, dgf_ref, dgp_ref):
        i = pl.program_id(0)

        @pl.when(i == 0)
        def _():
            loss_ref[...] = jnp.zeros_like(loss_ref)
            dgf_ref[...] = jnp.zeros_like(dgf_ref)
            dgp_ref[...] = jnp.zeros_like(dgp_ref)

        c, _, _ = _conv_rows(a_ref[...], ah_ref[...], i == 0, cw_ref[...], cb_ref[...])
        gl, gp = _gelu_and_grad(c[:, 0:DFF])
        ga_ref[...] = c[:, DFF:] * gp
        gb_ref[...] = gl
        gated = (gl * c[:, DFF:]).astype(bf16)
        gd_ref[...] = gated
        x2 = x1_ref[...] + jnp.dot(gated, wd_ref[...], preferred_element_type=f32)
        xh2, r2 = _rms(x2)
        h3 = (xh2 * gp_ref[...]).astype(bf16)
        h3_ref[...] = h3
        pg = _sig(jnp.dot(h3, wpg_ref[...], preferred_element_type=f32))
        pp = _dot(p_ref[...], wpp_ref[...])
        x3 = x2 + pg * pp
        xh3, r3 = _rms(x3)
        diff = xh3 * gf_ref[...] - t_ref[...]
        loss_ref[...] += 0.5 * jnp.sum(jnp.mean(diff * diff, axis=-1, keepdims=True), axis=0, keepdims=True)
        dy = diff * (1.0 / D)
        dx3, dgf_rows = _rms_bwd(dy, xh3, r3, gf_ref[...])
        dgf_ref[...] += _colsum(dgf_rows)
        dpp = dx3 * pg
        dpp_ref[...] = dpp.astype(bf16)
        dz = dx3 * pp * pg * (1.0 - pg)
        dz_ref[...] = dz.astype(bf16)
        dh3 = _dot_nt(dz, wpg_ref[...])
        dx2n, dgp_rows = _rms_bwd(dh3, xh2, r2, gp_ref[...])
        dgp_ref[...] += _colsum(dgp_rows)
        dx2_ref[...] = dx3 + dx2n

    row = lambda n: pl.BlockSpec((tm, n), lambda i: (i, 0))
    halo = pl.BlockSpec((8, 2 * DFF), lambda i: (jnp.maximum(i * (tm // 8) - 1, 0), 0))
    return _pc(
        body, name="ffn_tail", grid=(T // tm,),
        in_specs=[row(2 * DFF), halo, _whole((3, 2 * DFF)), _whole((1, 2 * DFF)), _whole(w_down.shape), row(D), row(PLE),
                  _whole((1, D)), _whole(w_pg.shape), _whole(w_pp.shape), _whole((1, D)), row(D)],
        out_specs=[row(D), row(DFF), row(DFF), row(DFF), row(D), row(D), row(D), _acc((1, 128)), _acc((1, D)), _acc((1, D))],
        out_shape=[jax.ShapeDtypeStruct((T, D), f32), jax.ShapeDtypeStruct((T, DFF), bf16),
                   jax.ShapeDtypeStruct((T, DFF), f32), jax.ShapeDtypeStruct((T, DFF), f32), jax.ShapeDtypeStruct((T, D), bf16),
                   jax.ShapeDtypeStruct((T, D), bf16), jax.ShapeDtypeStruct((T, D), bf16),
                   jax.ShapeDtypeStruct((1, 128), f32), jax.ShapeDtypeStruct((1, D), f32), jax.ShapeDtypeStruct((1, D), f32)],
        compiler_params=_params(),
    )(a, a, conv_w, conv_b, w_down, x1, p, g_ple, w_pg, w_pp, g_fin, tgt)


def _ffn_bwd(dx2, a, g_a, g_b, conv_w, w_down, w_up, x1, g_ffn):
    T = a.shape[0]
    tm = FFN_TM
    nt = T // tm

    def body(dx2_ref, a_ref, ga_ref, gb_ref, cw_ref, wd_ref, wu_ref, x1_ref, g_ref,
             da_ref, dx1_ref, dcw_ref, dcb_ref, dg_ref, carry_ref):
        i = pl.program_id(0)

        @pl.when(i == 0)
        def _():
            carry_ref[...] = jnp.zeros_like(carry_ref)
            dcw_ref[...] = jnp.zeros_like(dcw_ref)
            dcb_ref[...] = jnp.zeros_like(dcb_ref)
            dg_ref[...] = jnp.zeros_like(dg_ref)

        av = a_ref[...]
        cw = cw_ref[...]
        dx2 = dx2_ref[...]
        dgd = _dot_nt(dx2, wd_ref[...])
        dc = jnp.concatenate([dgd * ga_ref[...], dgd * gb_ref[...]], axis=1)
        row = lax.broadcasted_iota(jnp.int32, (tm, 1), 0)
        n1 = carry_ref[0:1, :]
        n2 = carry_ref[1:2, :]
        up1 = jnp.where(row == tm - 1, n1, pltpu.roll(dc, tm - 1, 0))
        up2 = jnp.where(row == tm - 1, n2, jnp.where(row == tm - 2, n1, pltpu.roll(dc, tm - 2, 0)))
        dcb_ref[...] += _colsum(dc)
        dcw_ref[0:1, :] += _colsum(up2 * av)
        dcw_ref[1:2, :] += _colsum(up1 * av)
        dcw_ref[2:3, :] += _colsum(dc * av)
        da = (cw[2:3, :] * dc + cw[1:2, :] * up1 + cw[0:1, :] * up2).astype(bf16)
        carry_ref[...] = dc[0:8, :]
        da_ref[...] = da
        dh2 = lax.dot_general(da, wu_ref[...], (((1,), (1,)), ((), ())), preferred_element_type=f32)
        xh, r = _rms(x1_ref[...])
        dx1n, dg_rows = _rms_bwd(dh2, xh, r, g_ref[...])
        dg_ref[...] += _colsum(dg_rows)
        dx1_ref[...] = dx2 + dx1n

    rev = lambda i: nt - 1 - i
    row = lambda n: pl.BlockSpec((tm, n), lambda i: (rev(i), 0))
    return _pc(
        body, name="ffn_bwd", grid=(nt,),
        in_specs=[row(D), row(2 * DFF), row(DFF), row(DFF), _whole((3, 2 * DFF)), _whole(w_down.shape),
                  _whole(w_up.shape), row(D), _whole((1, D))],
        out_specs=[row(2 * DFF), row(D), _acc((3, 2 * DFF)), _acc((1, 2 * DFF)), _acc((1, D))],
        out_shape=[jax.ShapeDtypeStruct((T, 2 * DFF), bf16), jax.ShapeDtypeStruct((T, D), f32),
                   jax.ShapeDtypeStruct((3, 2 * DFF), f32), jax.ShapeDtypeStruct((1, 2 * DFF), f32),
                   jax.ShapeDtypeStruct((1, D), f32)],
        scratch_shapes=[pltpu.VMEM((8, 2 * DFF), f32)],
        compiler_params=_params(),
    )(dx2, a, g_a, g_b, conv_w, w_down, w_up, x1, g_ffn)


def _mix_bwd(dx1, hg_o, s5_o, gates, w_bhg, w_bs5, w_out):
    T = dx1.shape[0]
    tm = 256

    def body(dx1_ref, hg_ref, s5_ref, gt_ref, wh_ref, ws_ref, wo_ref, dgt_ref, dhg_ref, ds5_ref, dyh_ref, dys_ref):
        dm = _dot_nt(dx1_ref[...], wo_ref[...])
        yh = jnp.dot(hg_ref[...], wh_ref[...], preferred_element_type=f32)
        ys = jnp.dot(s5_ref[...], ws_ref[...], preferred_element_type=f32)
        sh = _sig(gt_ref[:, 0:D])
        ss = _sig(gt_ref[:, D:2 * D])
        dgt_ref[:, 0:D] = dm * yh * sh * (1.0 - sh)
        dgt_ref[:, D:2 * D] = dm * ys * ss * (1.0 - ss)
        dyh = (dm * sh).astype(bf16)
        dys = (dm * ss).astype(bf16)
        dyh_ref[...] = dyh
        dys_ref[...] = dys
        dhg_ref[...] = lax.dot_general(dyh, wh_ref[...], (((1,), (1,)), ((), ())), preferred_element_type=f32)
        ds5_ref[...] = lax.dot_general(dys, ws_ref[...], (((1,), (1,)), ((), ())), preferred_element_type=f32)

    row = lambda n: pl.BlockSpec((tm, n), lambda i: (i, 0))
    return _pc(
        body, name="mix_bwd", grid=(T // tm,),
        in_specs=[row(D), row(HW), row(SW), row(2 * D), _whole(w_bhg.shape), _whole(w_bs5.shape), _whole(w_out.shape)],
        out_specs=[row(2 * D), row(HW), row(SW), row(D), row(D)],
        out_shape=[jax.ShapeDtypeStruct((T, 2 * D), f32), jax.ShapeDtypeStruct((T, HW), f32), jax.ShapeDtypeStruct((T, SW), f32),
                   jax.ShapeDtypeStruct((T, D), bf16), jax.ShapeDtypeStruct((T, D), bf16)],
        compiler_params=_params(),
    )(dx1, hg_o, s5_o, gates, w_bhg, w_bs5, w_out)


def _in_bwd(d_hg, d_u, d_gt, x, dx1, w, g):
    T = x.shape[0]
    tm = 256

    def body(dhg_ref, du_ref, dgt_ref, x_ref, dx1_ref, w_ref, g_ref, dx_ref, dg_ref):
        @pl.when(pl.program_id(0) == 0)
        def _():
            dg_ref[...] = jnp.zeros_like(dg_ref)

        dh = (_dot_nt(dhg_ref[...], w_ref[:, 0:4 * HW]) + _dot_nt(du_ref[...], w_ref[:, 4 * HW:4 * HW + SW])
              + _dot_nt(dgt_ref[...], w_ref[:, 4 * HW + SW:]))
        xh, r = _rms(x_ref[...])
        dxn, dg_rows = _rms_bwd(dh, xh, r, g_ref[...])
        dg_ref[...] += _colsum(dg_rows)
        dx_ref[...] = dx1_ref[...] + dxn

    row = lambda n: pl.BlockSpec((tm, n), lambda i: (i, 0))
    return _pc(
        body, name="in_bwd", grid=(T // tm,),
        in_specs=[row(4 * HW), row(SW), row(2 * D), row(D), row(D), _whole(w.shape), _whole((1, D))],
        out_specs=[row(D), _acc((1, D))],
        out_shape=[jax.ShapeDtypeStruct((T, D), f32), jax.ShapeDtypeStruct((1, D), f32)],
        compiler_params=_params(),
    )(d_hg, d_u, d_gt, x, dx1, w, g)


def _wgrad(name, a, b, nj=None, a_blk=None, a_idx=None, b_blk=None, b_idx=None):
    T = a.shape[0]
    tm = 512
    dense = nj is None
    if dense:
        K, N = a.shape[1], b.shape[1]
        a_blk, a_idx = K, (lambda j: 0)
        b_blk = N
        while K * b_blk * 4 > 6 * 1024 * 1024 and b_blk % 256 == 0:
            b_blk //= 2
        nj, b_idx = N // b_blk, (lambda j: j)

    def body(a_ref, b_ref, o_ref):
        @pl.when(pl.program_id(1) == 0)
        def _():
            o_ref[...] = jnp.zeros_like(o_ref)

        o_ref[0] += _dot_tn(a_ref[...], b_ref[...])

    out = _pc(
        body, name=name, grid=(nj, T // tm),
        in_specs=[pl.BlockSpec((tm, a_blk), lambda j, i: (i, a_idx(j))), pl.BlockSpec((tm, b_blk), lambda j, i: (i, b_idx(j)))],
        out_specs=pl.BlockSpec((1, a_blk, b_blk), lambda j, i: (j, 0, 0)),
        out_shape=jax.ShapeDtypeStruct((nj, a_blk, b_blk), f32),
        compiler_params=_params(2),
    )(a, b)
    if dense:
        return out[0] if nj == 1 else jnp.transpose(out, (1, 0, 2)).reshape(a.shape[1], b.shape[1])
    return out


ANY = pl.BlockSpec(memory_space=pl.ANY)


def _all_gather(name, shard):
    R, C = shard.shape

    def body(x_ref, out_ref, send_sems, recv_sems, local_sem):
        x, y, c = lax.axis_index("x"), lax.axis_index("y"), lax.axis_index("c")
        me, sibling = (x, y, c), (x, y, 1 - c)
        chips = [(1 - x, y), (x, 1 - y), (1 - x, 1 - y)]

        def slot(px, py, pc):
            return out_ref.at[4 * px + 2 * py + pc]

        def copy(k, block, to, src=None):
            return pltpu.make_async_remote_copy(
                src_ref=slot(*block) if src is None else src, dst_ref=slot(*block),
                send_sem=send_sems.at[k], recv_sem=recv_sems.at[k], device_id=to, device_id_type=MESH)

        mine = pltpu.make_async_copy(x_ref, slot(*me), local_sem)
        mine.start()
        first = [copy(0, me, sibling, src=x_ref)]
        first += [copy(1 + j, me, (*chip, c), src=x_ref) for j, chip in enumerate(chips)]
        for cp in first:
            cp.start()
        passed = [copy(4 + j, (*chip, c), sibling) for j, chip in enumerate(chips)]
        for j, chip in enumerate(chips):
            copy(1 + j, (*chip, c), me).wait_recv()
            passed[j].start()
        copy(0, sibling, me).wait_recv()
        for j, chip in enumerate(chips):
            copy(4 + j, (*chip, 1 - c), me).wait_recv()
        for cp in first + passed:
            cp.wait_send()
        mine.wait()

    return _pc(
        body, name=name, in_specs=[ANY], out_specs=ANY,
        out_shape=jax.ShapeDtypeStruct((N_DEV, R, C), shard.dtype),
        scratch_shapes=[pltpu.SemaphoreType.DMA((7,)), pltpu.SemaphoreType.DMA((7,)), pltpu.SemaphoreType.DMA],
    )(shard)


def _swap_sibling(gs):
    n = len(gs)

    def body(*refs):
        g_refs, r_refs, (send_sems, recv_sems) = refs[:n], refs[n:2 * n], refs[2 * n:]
        x, y, c = lax.axis_index("x"), lax.axis_index("y"), lax.axis_index("c")
        cps = [pltpu.make_async_remote_copy(src_ref=g.at[k, 1 - c], dst_ref=r.at[k], send_sem=send_sems.at[4 * i + k],
                                            recv_sem=recv_sems.at[4 * i + k], device_id=(x, y, 1 - c), device_id_type=MESH)
               for i, (g, r) in enumerate(zip(g_refs, r_refs)) for k in range(4)]
        for cp in cps:
            cp.start()
        for cp in cps:
            cp.wait()

    return _pc(
        body, name="rs_sibling", in_specs=[ANY] * n, out_specs=[ANY] * n,
        out_shape=[jax.ShapeDtypeStruct((4, *g.shape[2:]), g.dtype) for g in gs],
        scratch_shapes=[pltpu.SemaphoreType.DMA((4 * n,)), pltpu.SemaphoreType.DMA((4 * n,))],
    )(*gs)


def _swap_chips(ps):
    n = len(ps)

    def body(*refs):
        p_refs, r_refs, (send_sems, recv_sems) = refs[:n], refs[n:2 * n], refs[2 * n:]
        x, y, c = lax.axis_index("x"), lax.axis_index("y"), lax.axis_index("c")
        chips = [(1 - x, y), (x, 1 - y), (1 - x, 1 - y)]
        cps = [pltpu.make_async_remote_copy(src_ref=p.at[2 * px + py], dst_ref=r.at[k], send_sem=send_sems.at[3 * i + k],
                                            recv_sem=recv_sems.at[3 * i + k], device_id=(px, py, c), device_id_type=MESH)
               for i, (p, r) in enumerate(zip(p_refs, r_refs)) for k, (px, py) in enumerate(chips)]
        for cp in cps:
            cp.start()
        for cp in cps:
            cp.wait()

    return _pc(
        body, name="rs_chips", in_specs=[ANY] * n, out_specs=[ANY] * n,
        out_shape=[jax.ShapeDtypeStruct((3, *p.shape[1:]), p.dtype) for p in ps],
        scratch_shapes=[pltpu.SemaphoreType.DMA((3 * n,)), pltpu.SemaphoreType.DMA((3 * n,))],
    )(*ps)


def _add_halves(name, g4, got, ids):
    _, _, K, c = g4.shape

    def body(ids_ref, a_ref, b_ref, p16_ref, own_ref):
        s = a_ref[0, 0] + b_ref[0]
        p16_ref[0] = s.astype(bf16)

        @pl.when(pl.program_id(0) == ids_ref[1])
        def _():
            own_ref[...] = s

    return _pc(
        body, name=name,
        grid_spec=pltpu.PrefetchScalarGridSpec(
            num_scalar_prefetch=1, grid=(4,),
            in_specs=[pl.BlockSpec((1, 1, K, c), lambda k, ids: (k, ids[0], 0, 0)),
                      pl.BlockSpec((1, K, c), lambda k, ids: (k, 0, 0))],
            out_specs=[pl.BlockSpec((1, K, c), lambda k, ids: (k, 0, 0)), pl.BlockSpec((K, c), lambda k, ids: (0, 0))]),
        out_shape=[jax.ShapeDtypeStruct((4, K, c), bf16), jax.ShapeDtypeStruct((K, c), f32)],
        compiler_params=_params(),
    )(ids, g4, got)


def _row_tile(K):
    for cand in (256, 176, 128, 64):
        if K % cand == 0:
            return cand
    return K


def _adam_shard(name, own, got3, w, m, v):
    K, c = own.shape
    tr = _row_tile(K)

    def body(own_ref, got_ref, w_ref, m_ref, v_ref, g_ref, d_ref, m2_ref, v2_ref):
        g = own_ref[...] + got_ref[0].astype(f32) + got_ref[1].astype(f32) + got_ref[2].astype(f32)
        g_ref[0] = g
        delta, m2, v2 = _adam_math(g, w_ref[0], m_ref[0], v_ref[0])
        d_ref[0] = delta
        m2_ref[0] = m2
        v2_ref[0] = v2

    blk = pl.BlockSpec((1, tr, c), lambda i: (0, i, 0))
    out = jax.ShapeDtypeStruct((1, K, c), f32)
    return _pc(
        body, name=name, grid=(K // tr,),
        in_specs=[pl.BlockSpec((tr, c), lambda i: (i, 0)), pl.BlockSpec((3, tr, c), lambda i: (0, i, 0)), blk, blk, blk],
        out_specs=[blk, blk, blk, blk], out_shape=[out, out, out, out], compiler_params=_params(),
    )(own, got3, w, m, v)


def _allreduce_small(grads):
    n = len(grads)
    shapes = [g.shape for g in grads]

    def body(*refs):
        g_refs, outs, recv = refs[0:n], refs[n:2 * n], refs[2 * n:5 * n]
        send_sems, recv_sems = refs[5 * n:]
        x, y, c = lax.axis_index("x"), lax.axis_index("y"), lax.axis_index("c")
        peers = [(x, y, 1 - c), (1 - x, y, c), (x, 1 - y, c)]
        for i in range(n):
            outs[i][...] = g_refs[i][...]
        for s, peer in enumerate(peers):
            cps = [pltpu.make_async_remote_copy(src_ref=outs[i], dst_ref=recv[s * n + i], send_sem=send_sems.at[s * n + i],
                                                recv_sem=recv_sems.at[s * n + i], device_id=peer, device_id_type=MESH)
                   for i in range(n)]
            for cp in cps:
                cp.start()
            for cp in cps:
                cp.wait()
            for i in range(n):
                outs[i][...] = outs[i][...] + recv[s * n + i][...]

    return _pc(
        body, name="allreduce_small", grid=(1,), in_specs=[_whole(s) for s in shapes], out_specs=[_acc(s) for s in shapes],
        out_shape=[jax.ShapeDtypeStruct(s, f32) for s in shapes],
        scratch_shapes=[pltpu.VMEM(s, f32) for s in shapes] * 3
        + [pltpu.SemaphoreType.DMA((3 * n,)), pltpu.SemaphoreType.DMA((3 * n,))],
        compiler_params=_params(),
    )(*grads)


def _adam_small(grads, ws, ms, vs):
    n = len(grads)
    shapes = [g.shape for g in grads]

    def body(*refs):
        g_refs, w_refs, m_refs, v_refs = refs[0:n], refs[n:2 * n], refs[2 * n:3 * n], refs[3 * n:4 * n]
        outs = refs[4 * n:8 * n]
        for i in range(n):
            g = g_refs[i][...]
            delta, m2, v2 = _adam_math(g, w_refs[i][...], m_refs[i][...], v_refs[i][...])
            outs[i][...] = g
            outs[n + i][...] = delta
            outs[2 * n + i][...] = m2
            outs[3 * n + i][...] = v2

    return _pc(
        body, name="adam_small", grid=(1,), in_specs=[_whole(s) for s in shapes] * 4, out_specs=[_acc(s) for s in shapes] * 4,
        out_shape=[jax.ShapeDtypeStruct(s, f32) for s in shapes] * 4, compiler_params=_params(),
    )(*grads, *ws, *ms, *vs)


def _adam_math(g, w, m, v):
    m2 = ADAM_B1 * m + (1.0 - ADAM_B1) * g
    v2 = ADAM_B2 * v + (1.0 - ADAM_B2) * (g * g)
    m_hat = m2 / (1.0 - ADAM_B1 ** ADAM_STEP)
    v_hat = v2 / (1.0 - ADAM_B2 ** ADAM_STEP)
    delta = -ADAM_LR * (m_hat / (jnp.sqrt(v_hat) + ADAM_EPS) + ADAM_WD * w)
    return delta, m2, v2


def _pack(arrs, dtype, row_mult):
    rows = []
    for a in arrs:
        flat = a.reshape(-1).astype(dtype)
        pad = (-flat.shape[0]) % LANES
        if pad:
            flat = jnp.concatenate([flat, jnp.zeros((pad,), dtype)])
        rows.append(flat.reshape(-1, LANES))
    out = jnp.concatenate(rows, axis=0)
    pad = (-out.shape[0]) % row_mult
    if pad:
        out = jnp.concatenate([out, jnp.zeros((pad, LANES), dtype)], axis=0)
    return out


def _unpack(buf, shapes):
    lead = buf.shape[:-2]
    outs, r = [], 0
    for shp in shapes:
        n = math.prod(shp)
        nr = -(-n // LANES)
        piece = buf[..., r:r + nr, :].reshape(*lead, nr * LANES)[..., :n]
        outs.append(piece.reshape(*lead, *shp))
        r += nr
    return outs


def _to_slabs(full, axis):
    shp = full.shape
    n = shp[axis] // N_DEV
    return jnp.moveaxis(full.reshape(*shp[:axis], N_DEV, n, *shp[axis + 1:]), axis, 0)


def _from_slabs(slabs, axis):
    t = jnp.moveaxis(slabs, 0, axis)
    shp = t.shape
    return t.reshape(*shp[:axis], shp[axis] * shp[axis + 1], *shp[axis + 2:])


def _s5_discretise(lam_re, lam_im, log_dt, b_re, b_im):
    dt = jnp.exp(log_dt)[:, None]
    mag = jnp.exp(lam_re * dt)
    a_re = mag * jnp.cos(lam_im * dt)
    a_im = mag * jnp.sin(lam_im * dt)
    den = lam_re * lam_re + lam_im * lam_im
    coef_re = ((a_re - 1.0) * lam_re + a_im * lam_im) / den
    coef_im = (a_im * lam_re - (a_re - 1.0) * lam_im) / den
    bbar_re = coef_re[..., None] * b_re - coef_im[..., None] * b_im
    bbar_im = coef_re[..., None] * b_im + coef_im[..., None] * b_re
    return a_re, a_im, bbar_re, bbar_im


def _s5_operands(bbar_re, bbar_im, c_re, c_im):
    eye = jnp.eye(SG // NST, dtype=f32)

    def b_op(bb):
        return jnp.einsum("sgnq,gh->sgqhn", bb.reshape(NST, SG // NST, SN, SP), eye).reshape(NST, 128, STW)

    def c_op(cc):
        return jnp.einsum("sgpn,gh->shngp", cc.reshape(NST, SG // NST, SP, SN), eye).reshape(NST, STW, 128)

    bdb = jnp.concatenate([b_op(bbar_re), b_op(bbar_im)], axis=0)
    bdc = jnp.concatenate([c_op(c_re), c_op(-c_im)], axis=0)
    return bdb, bdc


_BIG = ["w_in", "s5_glu_w", "w_branch_hg", "w_branch_s5", "w_out", "w_up", "w_down", "w_ple_gate", "w_ple_proj", "conv_w"]
_BIG_AXIS = {"w_in": 1, "s5_glu_w": 0, "w_branch_hg": 1, "w_branch_s5": 1, "w_out": 0, "w_up": 1, "w_down": 0,
             "w_ple_gate": 0, "w_ple_proj": 1, "conv_w": 1}
_SMALL = ["norm_mix_g", "hg_lb_logits", "hg_norm_g", "s5_lambda_re", "s5_lambda_im", "s5_log_dt", "s5_b_re", "s5_b_im",
          "s5_c_re", "s5_c_im", "s5_d", "s5_glu_b", "norm_ffn_g", "conv_b", "norm_ple_g", "norm_final_g"]
_ORDER = ["norm_mix_g", "w_in", "hg_lb_logits", "hg_norm_g", "s5_lambda_re", "s5_lambda_im", "s5_log_dt", "s5_b_re",
          "s5_b_im", "s5_c_re", "s5_c_im", "s5_d", "s5_glu_w", "s5_glu_b", "w_branch_hg", "w_branch_s5", "w_out",
          "norm_ffn_g", "w_up", "conv_w", "conv_b", "w_down", "norm_ple_g", "w_ple_gate", "w_ple_proj", "norm_final_g"]


def kernel(x, p, norm_mix_g, w_in, hg_lb_logits, hg_norm_g, s5_lambda_re, s5_lambda_im, s5_log_dt, s5_b_re, s5_b_im, s5_c_re, s5_c_im, s5_d, s5_glu_w, s5_glu_b, w_branch_hg, w_branch_s5, w_out, norm_ffn_g, w_up, conv_w, conv_b, w_down, norm_ple_g, w_ple_gate, w_ple_proj, norm_final_g, loss_target, m_norm_mix_g, m_w_in, m_hg_lb_logits, m_hg_norm_g, m_s5_lambda_re, m_s5_lambda_im, m_s5_log_dt, m_s5_b_re, m_s5_b_im, m_s5_c_re, m_s5_c_im, m_s5_d, m_s5_glu_w, m_s5_glu_b, m_w_branch_hg, m_w_branch_s5, m_w_out, m_norm_ffn_g, m_w_up, m_conv_w, m_conv_b, m_w_down, m_norm_ple_g, m_w_ple_gate, m_w_ple_proj, m_norm_final_g, v_norm_mix_g, v_w_in, v_hg_lb_logits, v_hg_norm_g, v_s5_lambda_re, v_s5_lambda_im, v_s5_log_dt, v_s5_b_re, v_s5_b_im, v_s5_c_re, v_s5_c_im, v_s5_d, v_s5_glu_w, v_s5_glu_b, v_w_branch_hg, v_w_branch_s5, v_w_out, v_norm_ffn_g, v_w_up, v_conv_w, v_conv_b, v_w_down, v_norm_ple_g, v_w_ple_gate, v_w_ple_proj, v_norm_final_g):
    W = dict(norm_mix_g=norm_mix_g, w_in=w_in, hg_lb_logits=hg_lb_logits, hg_norm_g=hg_norm_g, s5_lambda_re=s5_lambda_re, s5_lambda_im=s5_lambda_im, s5_log_dt=s5_log_dt, s5_b_re=s5_b_re, s5_b_im=s5_b_im, s5_c_re=s5_c_re, s5_c_im=s5_c_im, s5_d=s5_d, s5_glu_w=s5_glu_w, s5_glu_b=s5_glu_b, w_branch_hg=w_branch_hg, w_branch_s5=w_branch_s5, w_out=w_out, norm_ffn_g=norm_ffn_g, w_up=w_up, conv_w=conv_w, conv_b=conv_b, w_down=w_down, norm_ple_g=norm_ple_g, w_ple_gate=w_ple_gate, w_ple_proj=w_ple_proj, norm_final_g=norm_final_g)
    M = dict(norm_mix_g=m_norm_mix_g, w_in=m_w_in, hg_lb_logits=m_hg_lb_logits, hg_norm_g=m_hg_norm_g, s5_lambda_re=m_s5_lambda_re, s5_lambda_im=m_s5_lambda_im, s5_log_dt=m_s5_log_dt, s5_b_re=m_s5_b_re, s5_b_im=m_s5_b_im, s5_c_re=m_s5_c_re, s5_c_im=m_s5_c_im, s5_d=m_s5_d, s5_glu_w=m_s5_glu_w, s5_glu_b=m_s5_glu_b, w_branch_hg=m_w_branch_hg, w_branch_s5=m_w_branch_s5, w_out=m_w_out, norm_ffn_g=m_norm_ffn_g, w_up=m_w_up, conv_w=m_conv_w, conv_b=m_conv_b, w_down=m_w_down, norm_ple_g=m_norm_ple_g, w_ple_gate=m_w_ple_gate, w_ple_proj=m_w_ple_proj, norm_final_g=m_norm_final_g)
    V = dict(norm_mix_g=v_norm_mix_g, w_in=v_w_in, hg_lb_logits=v_hg_lb_logits, hg_norm_g=v_hg_norm_g, s5_lambda_re=v_s5_lambda_re, s5_lambda_im=v_s5_lambda_im, s5_log_dt=v_s5_log_dt, s5_b_re=v_s5_b_re, s5_b_im=v_s5_b_im, s5_c_re=v_s5_c_re, s5_c_im=v_s5_c_im, s5_d=v_s5_d, s5_glu_w=v_s5_glu_w, s5_glu_b=v_s5_glu_b, w_branch_hg=v_w_branch_hg, w_branch_s5=v_w_branch_s5, w_out=v_w_out, norm_ffn_g=v_norm_ffn_g, w_up=v_w_up, conv_w=v_conv_w, conv_b=v_conv_b, w_down=v_w_down, norm_ple_g=v_norm_ple_g, w_ple_gate=v_w_ple_gate, w_ple_proj=v_w_ple_proj, norm_final_g=v_norm_final_g)

    mm_names = _BIG[:-1]
    shard2 = {n: W[n][0] for n in _BIG}
    conv_bits = lax.bitcast_convert_type(shard2["conv_w"], bf16)
    wpack = _pack([shard2[n] for n in mm_names] + [conv_bits], bf16, 16)
    gathered = _all_gather("ag_weights", wpack)
    pieces = _unpack(gathered, [shard2[n].shape for n in mm_names] + [conv_bits.shape])
    full = {n: _from_slabs(pc, _BIG_AXIS[n]) for n, pc in zip(mm_names, pieces[:-1])}
    conv_w_full = _from_slabs(lax.bitcast_convert_type(pieces[-1], f32), 1)

    xt = x[0]
    pt = p[0, 0]
    tgt = loss_target[0]
    T = xt.shape[0]
    lam_re, lam_im, log_dt = s5_lambda_re[0], s5_lambda_im[0], s5_log_dt[0]
    b_re, b_im, c_re, c_im = s5_b_re[0], s5_b_im[0], s5_c_re[0], s5_c_im[0]

    def s5_prep(lam_re, lam_im, log_dt, b_re, b_im, c_re, c_im):
        a_re, a_im, bbar_re, bbar_im = _s5_discretise(lam_re, lam_im, log_dt, b_re, b_im)
        bdb, bdc = _s5_operands(bbar_re, bbar_im, c_re, c_im)
        return a_re, a_im, bdb, bdc

    (a_re, a_im, bdb, bdc), s5_prep_vjp = jax.vjp(s5_prep, lam_re, lam_im, log_dt, b_re, b_im, c_re, c_im)
    a_row = jnp.concatenate([a_re.reshape(1, SL), a_im.reshape(1, SL)], axis=1)
    bdb_b, bdc_b = bdb.astype(bf16), bdc.astype(bf16)

    h1, proj_hg, u_raw, gates = _in_proj(xt, norm_mix_g, full["w_in"])
    ng4 = jnp.tile(hg_norm_g, (1, NH))
    hg_o, sprev = _hgrn_fwd(proj_hg, hg_lb_logits, ng4)
    x_st, y_s5, g_s5, s5_o = _s5_fwd(u_raw, a_row, bdb_b, bdc_b, s5_d, full["s5_glu_w"], s5_glu_b)
    x1, merged, h2, a_up = _mix_up(xt, hg_o, s5_o, gates, full["w_branch_hg"], full["w_branch_s5"], full["w_out"],
                                   norm_ffn_g, full["w_up"])
    (dx2, gated, g_a, g_b, h3, dz_ple, dpp, loss_part, d_norm_final, d_norm_ple) = _ffn_tail(
        a_up, conv_w_full, conv_b, full["w_down"], x1, pt, norm_ple_g, full["w_ple_gate"], full["w_ple_proj"],
        norm_final_g.reshape(1, D), tgt)

    da_up, dx1, d_conv_w, d_conv_b, d_norm_ffn = _ffn_bwd(dx2, a_up, g_a, g_b, conv_w_full, full["w_down"],
                                                           full["w_up"], x1, norm_ffn_g)
    d_gates, d_hg_o, d_s5_o, dyh, dys = _mix_bwd(dx1, hg_o, s5_o, gates, full["w_branch_hg"], full["w_branch_s5"],
                                                   full["w_out"])
    d_proj_hg, d_lb, d_hg_norm = _hgrn_bwd(proj_hg, hg_lb_logits, ng4, sprev, d_hg_o)
    d_u, dz_glu, d_a_re, d_a_im, d_s5_d, d_glu_b, d_bdb, d_bdc = _s5_bwd(
        d_s5_o, y_s5, u_raw, x_st, a_row, bdb_b, bdc_b, s5_d, full["s5_glu_w"], s5_glu_b)
    grad_x, d_norm_mix = _in_bwd(d_proj_hg, d_u, d_gates, xt, dx1, full["w_in"], norm_mix_g)

    gw = {}
    gw["w_in"] = jnp.concatenate([_wgrad("wg_in_hg", h1, d_proj_hg), _wgrad("wg_in_u", h1, d_u),
                                  _wgrad("wg_in_gates", h1, d_gates)], axis=1)
    gw["s5_glu_w"] = _wgrad("wg_glu", g_s5, dz_glu)
    gw["w_branch_hg"] = _wgrad("wg_bhg", hg_o, dyh)
    gw["w_branch_s5"] = _wgrad("wg_bs5", s5_o, dys)
    gw["w_out"] = _wgrad("wg_out", merged, dx1)
    gw["w_up"] = _wgrad("wg_up", h2, da_up)
    gw["w_down"] = _wgrad("wg_down", gated, dx2)
    gw["w_ple_gate"] = _wgrad("wg_pg", h3, dz_ple)
    gw["w_ple_proj"] = _wgrad("wg_pp", pt, dpp)
    gw["conv_w"] = d_conv_w
    (d_lam_re, d_lam_im, d_log_dt, d_b_re, d_b_im, d_c_re, d_c_im) = s5_prep_vjp(
        (d_a_re.reshape(SG, SN), d_a_im.reshape(SG, SN), d_bdb, d_bdc))
    sm = jax.nn.softmax(hg_lb_logits, axis=0)
    d_l0 = d_lb[0] * sm[0] * sm[1]
    d_logits = jnp.stack([d_l0, -d_l0], axis=0)

    gs = {"norm_mix_g": d_norm_mix, "hg_lb_logits": d_logits, "hg_norm_g": d_hg_norm, "s5_lambda_re": d_lam_re,
          "s5_lambda_im": d_lam_im, "s5_log_dt": d_log_dt, "s5_b_re": d_b_re, "s5_b_im": d_b_im, "s5_c_re": d_c_re,
          "s5_c_im": d_c_im, "s5_d": d_s5_d, "s5_glu_b": d_glu_b, "norm_ffn_g": d_norm_ffn, "conv_b": d_conv_b,
          "norm_ple_g": d_norm_ple, "norm_final_g": d_norm_final}

    ids = jnp.stack([lax.axis_index("c"), 2 * lax.axis_index("x") + lax.axis_index("y")]).astype(jnp.int32)
    g4 = [_to_slabs(gw[n], _BIG_AXIS[n]).reshape(4, 2, *shard2[n].shape) for n in _BIG]
    got = _swap_sibling(g4)
    sums = [_add_halves("rs_add_" + n, g, r, ids) for n, g, r in zip(_BIG, g4, got)]
    got3 = _swap_chips([p16 for p16, _ in sums])
    big_out = [_adam_shard("adam_" + n, own, r3, W[n], M[n], V[n]) for n, (_, own), r3 in zip(_BIG, sums, got3)]

    two_d = lambda a: a.reshape(1, -1) if a.ndim == 1 else a
    g_sum = _allreduce_small([two_d(gs[n].reshape(W[n].shape)) for n in _SMALL])
    small_out = _adam_small(g_sum, [two_d(W[n]) for n in _SMALL], [two_d(M[n]) for n in _SMALL],
                            [two_d(V[n]) for n in _SMALL])

    res = {}
    for k in range(4):
        d = {n: big_out[i][k] for i, n in enumerate(_BIG)}
        d.update({n: small_out[k * len(_SMALL) + i].reshape(W[n].shape) for i, n in enumerate(_SMALL)})
        res[k] = d
    loss = lax.psum(loss_part[0, 0], ("x", "y", "c"))
    return (loss, grad_x[None], *[res[0][n] for n in _ORDER], *[res[1][n] for n in _ORDER],
            *[res[2][n] for n in _ORDER], *[res[3][n] for n in _ORDER])
```

```python
import functools
import math

import jax
import jax.numpy as jnp
from jax import lax
from jax.experimental import pallas as pl
from jax.experimental.pallas import tpu as pltpu

f32 = jnp.float32
bf16 = jnp.bfloat16
MESH = pl.DeviceIdType.MESH

N_DEV = 8
D = 1024
HW = 512
HD = 128
NH = 4
CH = 64
SW = 512
SG = 32
SP = 16
SN = 64
SL = SG * SN
NST = 4
STW = SL // NST
DFF = 2816
PLE = 256
EPS = 1e-6
LANES = 1024
VMEM_LIMIT = 56 * 1024 * 1024

ADAM_LR, ADAM_B1, ADAM_B2, ADAM_EPS, ADAM_WD, ADAM_STEP = 0.001, 0.9, 0.999, 1e-08, 0.01, 10


def _pc(body, **kw):
    return pl.pallas_call(body, **kw)


def _params(n_axes=1, **kw):
    return pltpu.CompilerParams(dimension_semantics=("arbitrary",) * n_axes, vmem_limit_bytes=VMEM_LIMIT, **kw)


def _whole(shape):
    nd = len(shape)
    return pl.BlockSpec(shape, lambda *_: (0,) * nd, pipeline_mode=pl.Buffered(1))


def _acc(shape):
    nd = len(shape)
    return pl.BlockSpec(shape, lambda *_: (0,) * nd)


def _dot(a, b):
    return jnp.dot(a.astype(bf16), b.astype(bf16), preferred_element_type=f32)


def _dot_nt(a, b):
    return lax.dot_general(a.astype(bf16), b.astype(bf16), (((1,), (1,)), ((), ())), preferred_element_type=f32)


def _dot_tn(a, b):
    return lax.dot_general(a.astype(bf16), b.astype(bf16), (((0,), (0,)), ((), ())), preferred_element_type=f32)


def _sig(x):
    return jax.nn.sigmoid(x)


def _dsilu(z, s):
    return s * (1.0 + z * (1.0 - s))


_GC = math.sqrt(2.0 / math.pi)


def _gelu_and_grad(y):
    t = jnp.tanh(_GC * (y + 0.044715 * y * y * y))
    g = 0.5 * y * (1.0 + t)
    dg = 0.5 * (1.0 + t) + 0.5 * y * (1.0 - t * t) * _GC * (1.0 + 3.0 * 0.044715 * y * y)
    return g, dg


def _rms(x):
    r = lax.rsqrt(jnp.mean(x * x, axis=-1, keepdims=True) + EPS)
    return x * r, r


def _rms_bwd(dy, xh, r, g):
    dxh = dy * g
    dx = r * (dxh - xh * jnp.mean(dxh * xh, axis=-1, keepdims=True))
    return dx, dy * xh


def _colsum(x):
    return jnp.sum(x, axis=0, keepdims=True)


def _in_proj(x, g, w):
    T = x.shape[0]
    tm = 256

    def body(x_ref, g_ref, w_ref, h_ref, hg_ref, u_ref, gt_ref):
        xh, _ = _rms(x_ref[...])
        h = (xh * g_ref[...]).astype(bf16)
        h_ref[...] = h
        hg_ref[...] = jnp.dot(h, w_ref[:, 0:4 * HW], preferred_element_type=f32)
        u_ref[...] = jnp.dot(h, w_ref[:, 4 * HW:4 * HW + SW], preferred_element_type=f32)
        gt_ref[...] = jnp.dot(h, w_ref[:, 4 * HW + SW:], preferred_element_type=f32)

    row = lambda n: pl.BlockSpec((tm, n), lambda i: (i, 0))
    return _pc(
        body, name="in_proj", grid=(T // tm,),
        in_specs=[row(D), _whole((1, D)), _whole(w.shape)],
        out_specs=[row(D), row(4 * HW), row(SW), row(2 * D)],
        out_shape=[jax.ShapeDtypeStruct((T, D), bf16), jax.ShapeDtypeStruct((T, 4 * HW), f32),
                   jax.ShapeDtypeStruct((T, SW), f32), jax.ShapeDtypeStruct((T, 2 * D), f32)],
        compiler_params=_params(),
    )(x, g, w)


HG_NC = 2


def _tri_matmul(tri, x):
    hi = x.astype(bf16)
    r1 = x - hi.astype(f32)
    mid = r1.astype(bf16)
    lo = (r1 - mid.astype(f32)).astype(bf16)
    n = x.shape[1]
    out = jnp.dot(tri.astype(bf16), jnp.concatenate([hi, mid, lo], axis=1), preferred_element_type=f32)
    return out[:, 0:n] + out[:, n:2 * n] + out[:, 2 * n:3 * n]


def _hgrn_gates(lg, qr, fr):
    mx = jnp.max(lg, axis=0, keepdims=True)
    e = jnp.exp(lg - mx)
    lb = e[0:1, :] / (e[0:1, :] + e[1:2, :])
    sig = _sig(fr)
    f = lb + (1.0 - lb) * sig
    k = 1.0 - f
    r_i = lax.broadcasted_iota(jnp.int32, (CH, CH), 0)
    c_i = lax.broadcasted_iota(jnp.int32, (CH, CH), 1)
    tril = (r_i >= c_i)
    b = _tri_matmul(tril, jnp.log(f))
    bref = b[CH // 2:CH // 2 + 1, :]
    blast = b[CH - 1:CH, :]
    sq = _sig(qr)
    q = qr * sq
    e1 = jnp.exp(b - bref)
    e2 = jnp.exp(bref - b)
    e3 = jnp.exp(blast - b)
    e4 = jnp.exp(b)
    return dict(lb=lb, qr=qr, sq=sq, sig=sig, f=f, k=k, tril=tril, triu=(c_i >= r_i), e1=e1, e2=e2, e3=e3, e4=e4,
                qs=q * e1, ks=k * e2, kl=k * e3, qb=q * e4, dec=jnp.exp(blast))


def _hgrn_fwd(proj_hg, logits, ng4):
    T = proj_hg.shape[0]
    nch = T // CH
    tm = HG_NC * CH

    def body(q_ref, f_ref, i_ref, og_ref, lg_ref, ng_ref, out_ref, sprev_ref, st_ref):
        @pl.when(pl.program_id(0) == 0)
        def _():
            st_ref[...] = jnp.zeros_like(st_ref)

        for ci in range(HG_NC):
            rows = slice(ci * CH, (ci + 1) * CH)
            c = _hgrn_gates(lg_ref[...], q_ref[rows, :], f_ref[rows, :])
            v = i_ref[rows, :]
            og = og_ref[rows, :]
            ohs = []
            for h in range(NH):
                sl = slice(h * HD, (h + 1) * HD)
                p = jnp.where(c["tril"], _dot_nt(c["qs"][:, sl], c["ks"][:, sl]), 0.0)
                st = st_ref[h]
                sprev_ref[ci, h] = st
                o = _dot(p, v[:, sl]) + _dot_nt(c["qb"][:, sl], st)
                st_ref[h] = c["dec"][:, sl] * st + _dot_tn(v[:, sl], c["kl"][:, sl])
                ohs.append(_rms(o)[0])
            out_ref[rows, :] = (jnp.concatenate(ohs, axis=1) * ng_ref[...] * (og * _sig(og))).astype(bf16)

    col = lambda j: pl.BlockSpec((tm, HW), lambda n, j=j: (n, j))
    return _pc(
        body, name="hgrn_fwd", grid=(nch // HG_NC,),
        in_specs=[col(0), col(1), col(2), col(3), _whole((2, HW)), _whole((1, HW))],
        out_specs=[pl.BlockSpec((tm, HW), lambda n: (n, 0)),
                   pl.BlockSpec((HG_NC, NH, HD, HD), lambda n: (n, 0, 0, 0))],
        out_shape=[jax.ShapeDtypeStruct((T, HW), bf16), jax.ShapeDtypeStruct((nch, NH, HD, HD), f32)],
        scratch_shapes=[pltpu.VMEM((NH, HD, HD), f32)],
        compiler_params=_params(),
    )(proj_hg, proj_hg, proj_hg, proj_hg, logits, ng4)


def _hgrn_bwd(proj_hg, logits, ng4, sprev, d_out):
    T = proj_hg.shape[0]
    nch = T // CH
    tm = HG_NC * CH
    nst = nch // HG_NC

    def body(q_ref, f_ref, i_ref, og_ref, lg_ref, ng_ref, sp_ref, do_ref, dp_ref, dlb_ref, dng_ref, gt_ref):
        @pl.when(pl.program_id(0) == 0)
        def _():
            gt_ref[...] = jnp.zeros_like(gt_ref)
            dlb_ref[...] = jnp.zeros_like(dlb_ref)
            dng_ref[...] = jnp.zeros_like(dng_ref)

        row = lax.broadcasted_iota(jnp.int32, (CH, HW), 0)
        for ci in reversed(range(HG_NC)):
            rows = slice(ci * CH, (ci + 1) * CH)
            c = _hgrn_gates(lg_ref[...], q_ref[rows, :], f_ref[rows, :])
            tril = c["tril"]
            v = i_ref[rows, :]
            og = og_ref[rows, :]
            sog = _sig(og)
            d_gated = do_ref[rows, :]
            d_on_all = d_gated * (og * sog)
            parts = {n: [] for n in ("dqs", "dks", "dkl", "dqb", "dv", "ohg", "ddec")}
            for h in range(NH):
                sl = slice(h * HD, (h + 1) * HD)
                ng_h = ng_ref[:, sl]
                qs, ks, kl, qb, vh = c["qs"][:, sl], c["ks"][:, sl], c["kl"][:, sl], c["qb"][:, sl], v[:, sl]
                st = sp_ref[ci, h]
                gt = gt_ref[h]
                p = jnp.where(tril, _dot_nt(qs, ks), 0.0)
                o = _dot(p, vh) + _dot_nt(qb, st)
                oh, r = _rms(o)
                d_o, dng_rows = _rms_bwd(d_on_all[:, sl], oh, r, ng_h)
                dng_ref[...] += _colsum(dng_rows)
                dp = jnp.where(tril, _dot_nt(d_o, vh), 0.0)
                parts["dqb"].append(_dot(d_o, st))
                parts["dv"].append(_dot_tn(p, d_o) + _dot_nt(kl, gt))
                parts["dqs"].append(_dot(dp, ks))
                parts["dks"].append(_dot_tn(dp, qs))
                parts["dkl"].append(_dot(vh, gt))
                parts["ddec"].append(_colsum(gt * st))
                parts["ohg"].append(oh * ng_h)
                gt_ref[h] = _dot_tn(d_o, qb) + c["dec"][:, sl] * gt
            cat = {n: jnp.concatenate(vs, axis=1) for n, vs in parts.items()}
            dqs, dks, dkl, dqb = cat["dqs"], cat["dks"], cat["dkl"], cat["dqb"]
            dq = dqs * c["e1"] + dqb * c["e4"]
            dk = dks * c["e2"] + dkl * c["e3"]
            t_qs = dqs * c["qs"]
            t_ks = dks * c["ks"]
            t_kl = dkl * c["kl"]
            db = t_qs - t_ks - t_kl + dqb * c["qb"]
            dbref = _colsum(t_ks - t_qs)
            dblast = _colsum(t_kl) + cat["ddec"] * c["dec"]
            db = db + jnp.where(row == CH // 2, dbref, 0.0) + jnp.where(row == CH - 1, dblast, 0.0)
            df = _tri_matmul(c["triu"], db) / c["f"] - dk
            sig = c["sig"]
            dlb_ref[...] += _colsum(df * (1.0 - sig))
            dp_ref[rows, 0:HW] = dq * _dsilu(c["qr"], c["sq"])
            dp_ref[rows, HW:2 * HW] = df * (1.0 - c["lb"]) * sig * (1.0 - sig)
            dp_ref[rows, 2 * HW:3 * HW] = cat["dv"]
            dp_ref[rows, 3 * HW:4 * HW] = d_gated * cat["ohg"] * _dsilu(og, sog)

    rev = lambda n: nst - 1 - n
    col = lambda j: pl.BlockSpec((tm, HW), lambda n, j=j: (rev(n), j))
    return _pc(
        body, name="hgrn_bwd", grid=(nst,),
        in_specs=[col(0), col(1), col(2), col(3), _whole((2, HW)), _whole((1, HW)),
                  pl.BlockSpec((HG_NC, NH, HD, HD), lambda n: (rev(n), 0, 0, 0)),
                  pl.BlockSpec((tm, HW), lambda n: (rev(n), 0))],
        out_specs=[pl.BlockSpec((tm, 4 * HW), lambda n: (rev(n), 0)), _acc((1, HW)), _acc((1, HD))],
        out_shape=[jax.ShapeDtypeStruct((T, 4 * HW), f32), jax.ShapeDtypeStruct((1, HW), f32),
                   jax.ShapeDtypeStruct((1, HD), f32)],
        scratch_shapes=[pltpu.VMEM((NH, HD, HD), f32)],
        compiler_params=_params(),
    )(proj_hg, proj_hg, proj_hg, proj_hg, logits, ng4, sprev, d_out)


S5_TM = 256
S5_SEG = 8
S5_STEPS = S5_TM // S5_SEG
NLT = SL // 128


def _s5_tables(a_ref, pw_ref, pseg_ref, descending):
    re, im = slice(0, SL), slice(SL, 2 * SL)

    def cmul(ar, ai, br, bi):
        return ar * br - ai * bi, ar * bi + ai * br

    pw_ref[0:1, :] = a_ref[...]
    m = 1
    while m < S5_STEPS:
        pr, pi = cmul(pw_ref[0:m, re], pw_ref[0:m, im], pw_ref[m - 1:m, re], pw_ref[m - 1:m, im])
        pw_ref[m:2 * m, re] = pr
        pw_ref[m:2 * m, im] = pi
        m *= 2
    base = S5_STEPS - 1
    if descending:
        pseg_ref[7:8, :] = pw_ref[base:base + 1, :]
        m = 1
        while m < 8:
            pr, pi = cmul(pseg_ref[8 - m:8, re], pseg_ref[8 - m:8, im], pseg_ref[8 - m:9 - m, re], pseg_ref[8 - m:9 - m, im])
            pseg_ref[8 - 2 * m:8 - m, re] = pr
            pseg_ref[8 - 2 * m:8 - m, im] = pi
            m *= 2
    else:
        pseg_ref[0:1, :] = pw_ref[base:base + 1, :]
        m = 1
        while m < 8:
            pr, pi = cmul(pseg_ref[0:m, re], pseg_ref[0:m, im], pseg_ref[m - 1:m, re], pseg_ref[m - 1:m, im])
            pseg_ref[m:2 * m, re] = pr
            pseg_ref[m:2 * m, im] = pi
            m *= 2


def _seg_rows(j):
    return pl.ds(j * S5_SEG, S5_SEG)


def _seg_perm(transpose=False):
    r_i = lax.broadcasted_iota(jnp.int32, (S5_TM, S5_TM), 0)
    c_i = lax.broadcasted_iota(jnp.int32, (S5_TM, S5_TM), 1)
    if transpose:
        r_i, c_i = c_i, r_i
    return c_i == S5_STEPS * (r_i % S5_SEG) + r_i // S5_SEG


def _scan_fwd(x3_ref, pw_ref, pseg_ref, carry_ref):
    row8 = lax.broadcasted_iota(jnp.int32, (S5_SEG, 128), 0)
    for lt in range(NLT):
        kr, ki = lt, NLT + lt
        lr, li = slice(lt * 128, (lt + 1) * 128), slice(SL + lt * 128, SL + (lt + 1) * 128)
        ar, ai = pw_ref[0:1, lr], pw_ref[0:1, li]
        sr = jnp.zeros((S5_SEG, 128), f32)
        si = jnp.zeros((S5_SEG, 128), f32)
        for j in range(S5_STEPS):
            sr, si = ar * sr - ai * si + x3_ref[kr, _seg_rows(j), :], ar * si + ai * sr + x3_ref[ki, _seg_rows(j), :]
            x3_ref[kr, _seg_rows(j), :] = sr
            x3_ref[ki, _seg_rows(j), :] = si
        for d in (1, 2, 4):
            pr, pi = pseg_ref[d - 1:d, lr], pseg_ref[d - 1:d, li]
            tr, ti = pltpu.roll(sr, d, 0), pltpu.roll(si, d, 0)
            m = row8 >= d
            sr, si = sr + jnp.where(m, pr * tr - pi * ti, 0.0), si + jnp.where(m, pr * ti + pi * tr, 0.0)
        c0r, c0i = carry_ref[7:8, lr], carry_ref[7:8, li]
        qr, qi = pseg_ref[:, lr], pseg_ref[:, li]
        sr, si = sr + qr * c0r - qi * c0i, si + qr * c0i + qi * c0r
        carry_ref[:, lr] = sr
        carry_ref[:, li] = si
        cr = jnp.where(row8 == 0, c0r, pltpu.roll(sr, 1, 0))
        ci = jnp.where(row8 == 0, c0i, pltpu.roll(si, 1, 0))
        for j in range(S5_STEPS):
            pr, pi = pw_ref[j:j + 1, lr], pw_ref[j:j + 1, li]
            x3_ref[kr, _seg_rows(j), :] = x3_ref[kr, _seg_rows(j), :] + pr * cr - pi * ci
            x3_ref[ki, _seg_rows(j), :] = x3_ref[ki, _seg_rows(j), :] + pr * ci + pi * cr


def _scan_bwd(g3_ref, x3_ref, xh_ref, first, pw_ref, pseg_ref, carry_ref, dar_ref, dai_ref):
    row8 = lax.broadcasted_iota(jnp.int32, (S5_SEG, 128), 0)
    for lt in range(NLT):
        kr, ki = lt, NLT + lt
        lr, li = slice(lt * 128, (lt + 1) * 128), slice(SL + lt * 128, SL + (lt + 1) * 128)
        ar, ai = pw_ref[0:1, lr], pw_ref[0:1, li]
        sr = jnp.zeros((S5_SEG, 128), f32)
        si = jnp.zeros((S5_SEG, 128), f32)
        for j in reversed(range(S5_STEPS)):
            sr, si = ar * sr + ai * si + g3_ref[kr, _seg_rows(j), :], ar * si - ai * sr + g3_ref[ki, _seg_rows(j), :]
            g3_ref[kr, _seg_rows(j), :] = sr
            g3_ref[ki, _seg_rows(j), :] = si
        for d in (1, 2, 4):
            pr, pi = pseg_ref[8 - d:9 - d, lr], pseg_ref[8 - d:9 - d, li]
            tr, ti = pltpu.roll(sr, 8 - d, 0), pltpu.roll(si, 8 - d, 0)
            m = row8 < 8 - d
            sr, si = sr + jnp.where(m, pr * tr + pi * ti, 0.0), si + jnp.where(m, pr * ti - pi * tr, 0.0)
        c0r, c0i = carry_ref[0:1, lr], carry_ref[0:1, li]
        qr, qi = pseg_ref[:, lr], pseg_ref[:, li]
        sr, si = sr + qr * c0r + qi * c0i, si + qr * c0i - qi * c0r
        carry_ref[:, lr] = sr
        carry_ref[:, li] = si
        cr = jnp.where(row8 == 7, c0r, pltpu.roll(sr, 7, 0))
        ci = jnp.where(row8 == 7, c0i, pltpu.roll(si, 7, 0))
        hr = jnp.where(first, 0.0, xh_ref[kr, 7:8, :])
        hi = jnp.where(first, 0.0, xh_ref[ki, 7:8, :])
        acc_r = jnp.zeros((S5_SEG, 128), f32)
        acc_i = jnp.zeros((S5_SEG, 128), f32)
        for j in range(S5_STEPS):
            pr, pi = pw_ref[S5_STEPS - 1 - j:S5_STEPS - j, lr], pw_ref[S5_STEPS - 1 - j:S5_STEPS - j, li]
            lam_r = g3_ref[kr, _seg_rows(j), :] + pr * cr + pi * ci
            lam_i = g3_ref[ki, _seg_rows(j), :] + pr * ci - pi * cr
            g3_ref[kr, _seg_rows(j), :] = lam_r
            g3_ref[ki, _seg_rows(j), :] = lam_i
            if j == 0:
                xpr = jnp.where(row8 == 0, hr, pltpu.roll(x3_ref[kr, _seg_rows(S5_STEPS - 1), :], 1, 0))
                xpi = jnp.where(row8 == 0, hi, pltpu.roll(x3_ref[ki, _seg_rows(S5_STEPS - 1), :], 1, 0))
            else:
                xpr = x3_ref[kr, _seg_rows(j - 1), :]
                xpi = x3_ref[ki, _seg_rows(j - 1), :]
            acc_r = acc_r + lam_r * xpr + lam_i * xpi
            acc_i = acc_i + lam_i * xpr - lam_r * xpi
        dar_ref[:, lr] += _colsum(acc_r)
        dai_ref[:, lr] += _colsum(acc_i)


def _strip(x3_ref, part, s):
    k0 = part * NLT + s * (STW // 128)
    return jnp.concatenate([x3_ref[k0 + q] for q in range(STW // 128)], axis=1)


def _s5_fwd(u, a_row, bdb, bdc, dskip, glu_w, glu_b):
    T = u.shape[0]
    tm = S5_TM

    def body(u_ref, a_ref, bdb_ref, bdc_ref, ds_ref, gw_ref, gb_ref, x_ref, y_ref, g_ref, o_ref, pw_ref, pseg_ref, carry_ref):
        @pl.when(pl.program_id(0) == 0)
        def _():
            carry_ref[...] = jnp.zeros_like(carry_ref)
            _s5_tables(a_ref, pw_ref, pseg_ref, descending=False)

        uv = u_ref[...]
        ub = jnp.dot(_seg_perm().astype(bf16), uv.astype(bf16), preferred_element_type=f32).astype(bf16)
        for part in range(2):
            for s in range(NST):
                bu = jnp.dot(ub[:, s * 128:(s + 1) * 128], bdb_ref[part * NST + s], preferred_element_type=f32)
                for q in range(STW // 128):
                    x_ref[part * NLT + s * (STW // 128) + q] = bu[:, q * 128:(q + 1) * 128]
        _scan_fwd(x_ref, pw_ref, pseg_ref, carry_ref)
        ys = []
        for s in range(NST):
            acc = None
            for part in range(2):
                t = jnp.dot(_strip(x_ref, part, s).astype(bf16), bdc_ref[part * NST + s], preferred_element_type=f32)
                acc = t if acc is None else acc + t
            ys.append(acc)
        y = _tri_matmul(_seg_perm(transpose=True), jnp.concatenate(ys, axis=1)) + ds_ref[...] * uv
        y_ref[...] = y
        g, _ = _gelu_and_grad(y)
        gb = g.astype(bf16)
        g_ref[...] = gb
        z = jnp.dot(gb, gw_ref[...], preferred_element_type=f32) + gb_ref[...]
        o_ref[...] = (g * _sig(z)).astype(bf16)

    row = lambda n: pl.BlockSpec((tm, n), lambda i: (i, 0))
    return _pc(
        body, name="s5_fwd", grid=(T // tm,),
        in_specs=[row(SW), _whole((1, 2 * SL)), _whole(bdb.shape), _whole(bdc.shape), _whole((1, SW)),
                  _whole((SW, SW)), _whole((1, SW))],
        out_specs=[pl.BlockSpec((2 * NLT, tm, 128), lambda i: (0, i, 0)), row(SW), row(SW), row(SW)],
        out_shape=[jax.ShapeDtypeStruct((2 * NLT, T, 128), f32), jax.ShapeDtypeStruct((T, SW), f32),
                   jax.ShapeDtypeStruct((T, SW), bf16), jax.ShapeDtypeStruct((T, SW), bf16)],
        scratch_shapes=[pltpu.VMEM((S5_STEPS, 2 * SL), f32), pltpu.VMEM((8, 2 * SL), f32), pltpu.VMEM((8, 2 * SL), f32)],
        compiler_params=_params(),
    )(u, a_row, bdb, bdc, dskip, glu_w, glu_b)


def _s5_bwd(d_out, y, u, x, a_row, bdb, bdc, dskip, glu_w, glu_b):
    T = u.shape[0]
    tm = S5_TM
    nt = T // tm

    def body(do_ref, y_ref, u_ref, x_ref, xh_ref, a_ref, bdb_ref, bdc_ref, ds_ref, gw_ref, gb_ref,
             du_ref, dz_ref, dar_ref, dai_ref, dd_ref, dgb_ref, dbdb_ref, dbdc_ref, gs_ref, pw_ref, pseg_ref, carry_ref):
        i = pl.program_id(0)

        @pl.when(i == 0)
        def _():
            carry_ref[...] = jnp.zeros_like(carry_ref)
            _s5_tables(a_ref, pw_ref, pseg_ref, descending=True)
            dar_ref[...] = jnp.zeros_like(dar_ref)
            dai_ref[...] = jnp.zeros_like(dai_ref)
            dd_ref[...] = jnp.zeros_like(dd_ref)
            dgb_ref[...] = jnp.zeros_like(dgb_ref)
            dbdb_ref[...] = jnp.zeros_like(dbdb_ref)
            dbdc_ref[...] = jnp.zeros_like(dbdc_ref)

        yv = y_ref[...]
        uv = u_ref[...]
        g, gp = _gelu_and_grad(yv)
        z = jnp.dot(g.astype(bf16), gw_ref[...], preferred_element_type=f32) + gb_ref[...]
        sg = _sig(z)
        do = do_ref[...].astype(f32)
        dz = do * g * sg * (1.0 - sg)
        dz_ref[...] = dz.astype(bf16)
        dgb_ref[...] += _colsum(dz)
        dy = (do * sg + _dot_nt(dz, gw_ref[...])) * gp
        perm = _seg_perm().astype(bf16)
        dyb = jnp.dot(perm, dy.astype(bf16), preferred_element_type=f32).astype(bf16)
        dd_ref[...] += _colsum(dy * uv)
        for part in range(2):
            for s in range(NST):
                gx = lax.dot_general(dyb[:, s * 128:(s + 1) * 128], bdc_ref[part * NST + s], (((1,), (1,)), ((), ())),
                                     preferred_element_type=f32)
                for q in range(STW // 128):
                    gs_ref[part * NLT + s * (STW // 128) + q] = gx[:, q * 128:(q + 1) * 128]
        _scan_bwd(gs_ref, x_ref, xh_ref, i == nt - 1, pw_ref, pseg_ref, carry_ref, dar_ref, dai_ref)
        ub = jnp.dot(perm, uv.astype(bf16), preferred_element_type=f32).astype(bf16)
        dus = []
        for s in range(NST):
            acc = None
            for part in range(2):
                lv = _strip(gs_ref, part, s).astype(bf16)
                t = lax.dot_general(lv, bdb_ref[part * NST + s], (((1,), (1,)), ((), ())), preferred_element_type=f32)
                acc = t if acc is None else acc + t
                dbdb_ref[part * NST + s] += _dot_tn(ub[:, s * 128:(s + 1) * 128], lv)
                dbdc_ref[part * NST + s] += _dot_tn(_strip(x_ref, part, s), dyb[:, s * 128:(s + 1) * 128])
            dus.append(acc)
        du_ref[...] = _tri_matmul(_seg_perm(transpose=True), jnp.concatenate(dus, axis=1)) + dy * ds_ref[...]

    rev = lambda i: nt - 1 - i
    row = lambda n: pl.BlockSpec((tm, n), lambda i: (rev(i), 0))
    xblk = pl.BlockSpec((2 * NLT, tm, 128), lambda i: (0, rev(i), 0))
    halo = pl.BlockSpec((2 * NLT, 8, 128), lambda i: (0, jnp.maximum(rev(i) * (tm // 8) - 1, 0), 0))
    return _pc(
        body, name="s5_bwd", grid=(nt,),
        in_specs=[row(SW), row(SW), row(SW), xblk, halo, _whole((1, 2 * SL)), _whole(bdb.shape), _whole(bdc.shape),
                  _whole((1, SW)), _whole((SW, SW)), _whole((1, SW))],
        out_specs=[row(SW), row(SW), _acc((1, SL)), _acc((1, SL)), _acc((1, SW)), _acc((1, SW)),
                   _acc(bdb.shape), _acc(bdc.shape)],
        out_shape=[jax.ShapeDtypeStruct((T, SW), f32), jax.ShapeDtypeStruct((T, SW), bf16),
                   jax.ShapeDtypeStruct((1, SL), f32), jax.ShapeDtypeStruct((1, SL), f32),
                   jax.ShapeDtypeStruct((1, SW), f32), jax.ShapeDtypeStruct((1, SW), f32),
                   jax.ShapeDtypeStruct(bdb.shape, f32), jax.ShapeDtypeStruct(bdc.shape, f32)],
        scratch_shapes=[pltpu.VMEM((2 * NLT, tm, 128), f32), pltpu.VMEM((S5_STEPS, 2 * SL), f32),
                        pltpu.VMEM((8, 2 * SL), f32), pltpu.VMEM((8, 2 * SL), f32)],
        compiler_params=_params(),
    )(d_out, y, u, x, x, a_row, bdb, bdc, dskip, glu_w, glu_b)


def _mix_up(x, hg_o, s5_o, gates, w_bhg, w_bs5, w_out, g_ffn, w_up):
    T = x.shape[0]
    tm = 256

    def body(x_ref, hg_ref, s5_ref, gt_ref, wh_ref, ws_ref, wo_ref, g_ref, wu_ref, x1_ref, mg_ref, h2_ref, a_ref):
        yh = jnp.dot(hg_ref[...], wh_ref[...], preferred_element_type=f32)
        ys = jnp.dot(s5_ref[...], ws_ref[...], preferred_element_type=f32)
        merged = (_sig(gt_ref[:, 0:D]) * yh + _sig(gt_ref[:, D:2 * D]) * ys).astype(bf16)
        mg_ref[...] = merged
        x1 = x_ref[...] + jnp.dot(merged, wo_ref[...], preferred_element_type=f32)
        x1_ref[...] = x1
        xh, _ = _rms(x1)
        h2 = (xh * g_ref[...]).astype(bf16)
        h2_ref[...] = h2
        a_ref[...] = jnp.dot(h2, wu_ref[...], preferred_element_type=f32)

    row = lambda n: pl.BlockSpec((tm, n), lambda i: (i, 0))
    return _pc(
        body, name="mix_up", grid=(T // tm,),
        in_specs=[row(D), row(HW), row(SW), row(2 * D), _whole(w_bhg.shape), _whole(w_bs5.shape), _whole(w_out.shape),
                  _whole((1, D)), _whole(w_up.shape)],
        out_specs=[row(D), row(D), row(D), row(2 * DFF)],
        out_shape=[jax.ShapeDtypeStruct((T, D), f32), jax.ShapeDtypeStruct((T, D), bf16),
                   jax.ShapeDtypeStruct((T, D), bf16), jax.ShapeDtypeStruct((T, 2 * DFF), f32)],
        compiler_params=_params(),
    )(x, hg_o, s5_o, gates, w_bhg, w_bs5, w_out, g_ffn, w_up)


FFN_TM = 128


def _conv_rows(a, halo, first, conv_w, conv_b):
    tm = a.shape[0]
    row = lax.broadcasted_iota(jnp.int32, (tm, 1), 0)
    hm1 = jnp.where(first, 0.0, halo[7:8, :])
    hm2 = jnp.where(first, 0.0, halo[6:7, :])
    a1 = jnp.where(row == 0, hm1, pltpu.roll(a, 1, 0))
    a2 = jnp.where(row == 0, hm2, jnp.where(row == 1, hm1, pltpu.roll(a, 2, 0)))
    c = conv_b + conv_w[0:1, :] * a2 + conv_w[1:2, :] * a1 + conv_w[2:3, :] * a
    return c, a1, a2


def _ffn_tail(a, conv_w, conv_b, w_down, x1, p, g_ple, w_pg, w_pp, g_fin, tgt):
    T = a.shape[0]
    tm = FFN_TM

    def body(a_ref, ah_ref, cw_ref, cb_ref, wd_ref, x1_ref, p_ref, gp_ref, wpg_ref, wpp_ref, gf_ref, t_ref,
             dx2_ref, gd_ref, ga_ref, gb_ref, h3_ref, dz_ref, dpp_ref, loss_ref, dgf_ref, dgp_ref):
        i = pl.program_id(0)

        @pl.when(i == 0)
        def _():
            loss_ref[...] = jnp.zeros_like(loss_ref)
            dgf_ref[...] = jnp.zeros_like(dgf_ref)
            dgp_ref[...] = jnp.zeros_like(dgp_ref)

        c, _, _ = _conv_rows(a_ref[...], ah_ref[...], i == 0, cw_ref[...], cb_ref[...])
        gl, gp = _gelu_and_grad(c[:, 0:DFF])
        ga_ref[...] = c[:, DFF:] * gp
        gb_ref[...] = gl
        gated = (gl * c[:, DFF:]).astype(bf16)
        gd_ref[...] = gated
        x2 = x1_ref[...] + jnp.dot(gated, wd_ref[...], preferred_element_type=f32)
        xh2, r2 = _rms(x2)
        h3 = (xh2 * gp_ref[...]).astype(bf16)
        h3_ref[...] = h3
        pg = _sig(jnp.dot(h3, wpg_ref[...], preferred_element_type=f32))
        pp = _dot(p_ref[...], wpp_ref[...])
        x3 = x2 + pg * pp
        xh3, r3 = _rms(x3)
        diff = xh3 * gf_ref[...] - t_ref[...]
        loss_ref[...] += 0.5 * jnp.sum(jnp.mean(diff * diff, axis=-1, keepdims=True), axis=0, keepdims=True)
        dy = diff * (1.0 / D)
        dx3, dgf_rows = _rms_bwd(dy, xh3, r3, gf_ref[...])
        dgf_ref[...] += _colsum(dgf_rows)
        dpp = dx3 * pg
        dpp_ref[...] = dpp.astype(bf16)
        dz = dx3 * pp * pg * (1.0 - pg)
        dz_ref[...] = dz.astype(bf16)
        dh3 = _dot_nt(dz, wpg_ref[...])
        dx2n, dgp_rows = _rms_bwd(dh3, xh2, r2, gp_ref[...])
        dgp_ref[...] += _colsum(dgp_rows)
        dx2_ref[...] = dx3 + dx2n

    row = lambda n: pl.BlockSpec((tm, n), lambda i: (i, 0))
    halo = pl.BlockSpec((8, 2 * DFF), lambda i: (jnp.maximum(i * (tm // 8) - 1, 0), 0))
    return _pc(
        body, name="ffn_tail", grid=(T // tm,),
        in_specs=[row(2 * DFF), halo, _whole((3, 2 * DFF)), _whole((1, 2 * DFF)), _whole(w_down.shape), row(D), row(PLE),
                  _whole((1, D)), _whole(w_pg.shape), _whole(w_pp.shape), _whole((1, D)), row(D)],
        out_specs=[row(D), row(DFF), row(DFF), row(DFF), row(D), row(D), row(D), _acc((1, 128)), _acc((1, D)), _acc((1, D))],
        out_shape=[jax.ShapeDtypeStruct((T, D), f32), jax.ShapeDtypeStruct((T, DFF), bf16),
                   jax.ShapeDtypeStruct((T, DFF), f32), jax.ShapeDtypeStruct((T, DFF), f32), jax.ShapeDtypeStruct((T, D), bf16),
                   jax.ShapeDtypeStruct((T, D), bf16), jax.ShapeDtypeStruct((T, D), bf16),
                   jax.ShapeDtypeStruct((1, 128), f32), jax.ShapeDtypeStruct((1, D), f32), jax.ShapeDtypeStruct((1, D), f32)],
        compiler_params=_params(),
    )(a, a, conv_w, conv_b, w_down, x1, p, g_ple, w_pg, w_pp, g_fin, tgt)


def _ffn_bwd(dx2, a, g_a, g_b, conv_w, w_down, w_up, x1, g_ffn):
    T = a.shape[0]
    tm = FFN_TM
    nt = T // tm

    def body(dx2_ref, a_ref, ga_ref, gb_ref, cw_ref, wd_ref, wu_ref, x1_ref, g_ref,
             da_ref, dx1_ref, dcw_ref, dcb_ref, dg_ref, carry_ref):
        i = pl.program_id(0)

        @pl.when(i == 0)
        def _():
            carry_ref[...] = jnp.zeros_like(carry_ref)
            dcw_ref[...] = jnp.zeros_like(dcw_ref)
            dcb_ref[...] = jnp.zeros_like(dcb_ref)
            dg_ref[...] = jnp.zeros_like(dg_ref)

        av = a_ref[...]
        cw = cw_ref[...]
        dx2 = dx2_ref[...]
        dgd = _dot_nt(dx2, wd_ref[...])
        dc = jnp.concatenate([dgd * ga_ref[...], dgd * gb_ref[...]], axis=1)
        row = lax.broadcasted_iota(jnp.int32, (tm, 1), 0)
        n1 = carry_ref[0:1, :]
        n2 = carry_ref[1:2, :]
        up1 = jnp.where(row == tm - 1, n1, pltpu.roll(dc, tm - 1, 0))
        up2 = jnp.where(row == tm - 1, n2, jnp.where(row == tm - 2, n1, pltpu.roll(dc, tm - 2, 0)))
        dcb_ref[...] += _colsum(dc)
        dcw_ref[0:1, :] += _colsum(up2 * av)
        dcw_ref[1:2, :] += _colsum(up1 * av)
        dcw_ref[2:3, :] += _colsum(dc * av)
        da = (cw[2:3, :] * dc + cw[1:2, :] * up1 + cw[0:1, :] * up2).astype(bf16)
        carry_ref[...] = dc[0:8, :]
        da_ref[...] = da
        dh2 = lax.dot_general(da, wu_ref[...], (((1,), (1,)), ((), ())), preferred_element_type=f32)
        xh, r = _rms(x1_ref[...])
        dx1n, dg_rows = _rms_bwd(dh2, xh, r, g_ref[...])
        dg_ref[...] += _colsum(dg_rows)
        dx1_ref[...] = dx2 + dx1n

    rev = lambda i: nt - 1 - i
    row = lambda n: pl.BlockSpec((tm, n), lambda i: (rev(i), 0))
    return _pc(
        body, name="ffn_bwd", grid=(nt,),
        in_specs=[row(D), row(2 * DFF), row(DFF), row(DFF), _whole((3, 2 * DFF)), _whole(w_down.shape),
                  _whole(w_up.shape), row(D), _whole((1, D))],
        out_specs=[row(2 * DFF), row(D), _acc((3, 2 * DFF)), _acc((1, 2 * DFF)), _acc((1, D))],
        out_shape=[jax.ShapeDtypeStruct((T, 2 * DFF), bf16), jax.ShapeDtypeStruct((T, D), f32),
                   jax.ShapeDtypeStruct((3, 2 * DFF), f32), jax.ShapeDtypeStruct((1, 2 * DFF), f32),
                   jax.ShapeDtypeStruct((1, D), f32)],
        scratch_shapes=[pltpu.VMEM((8, 2 * DFF), f32)],
        compiler_params=_params(),
    )(dx2, a, g_a, g_b, conv_w, w_down, w_up, x1, g_ffn)


def _mix_bwd(dx1, hg_o, s5_o, gates, w_bhg, w_bs5, w_out):
    T = dx1.shape[0]
    tm = 256

    def body(dx1_ref, hg_ref, s5_ref, gt_ref, wh_ref, ws_ref, wo_ref, dgt_ref, dhg_ref, ds5_ref, dyh_ref, dys_ref):
        dm = _dot_nt(dx1_ref[...], wo_ref[...])
        yh = jnp.dot(hg_ref[...], wh_ref[...], preferred_element_type=f32)
        ys = jnp.dot(s5_ref[...], ws_ref[...], preferred_element_type=f32)
        sh = _sig(gt_ref[:, 0:D])
        ss = _sig(gt_ref[:, D:2 * D])
        dgt_ref[:, 0:D] = dm * yh * sh * (1.0 - sh)
        dgt_ref[:, D:2 * D] = dm * ys * ss * (1.0 - ss)
        dyh = (dm * sh).astype(bf16)
        dys = (dm * ss).astype(bf16)
        dyh_ref[...] = dyh
        dys_ref[...] = dys
        dhg_ref[...] = lax.dot_general(dyh, wh_ref[...], (((1,), (1,)), ((), ())), preferred_element_type=f32)
        ds5_ref[...] = lax.dot_general(dys, ws_ref[...], (((1,), (1,)), ((), ())), preferred_element_type=f32)

    row = lambda n: pl.BlockSpec((tm, n), lambda i: (i, 0))
    return _pc(
        body, name="mix_bwd", grid=(T // tm,),
        in_specs=[row(D), row(HW), row(SW), row(2 * D), _whole(w_bhg.shape), _whole(w_bs5.shape), _whole(w_out.shape)],
        out_specs=[row(2 * D), row(HW), row(SW), row(D), row(D)],
        out_shape=[jax.ShapeDtypeStruct((T, 2 * D), f32), jax.ShapeDtypeStruct((T, HW), f32), jax.ShapeDtypeStruct((T, SW), f32),
                   jax.ShapeDtypeStruct((T, D), bf16), jax.ShapeDtypeStruct((T, D), bf16)],
        compiler_params=_params(),
    )(dx1, hg_o, s5_o, gates, w_bhg, w_bs5, w_out)


def _in_bwd(d_hg, d_u, d_gt, x, dx1, w, g):
    T = x.shape[0]
    tm = 256

    def body(dhg_ref, du_ref, dgt_ref, x_ref, dx1_ref, w_ref, g_ref, dx_ref, dg_ref):
        @pl.when(pl.program_id(0) == 0)
        def _():
            dg_ref[...] = jnp.zeros_like(dg_ref)

        dh = (_dot_nt(dhg_ref[...], w_ref[:, 0:4 * HW]) + _dot_nt(du_ref[...], w_ref[:, 4 * HW:4 * HW + SW])
              + _dot_nt(dgt_ref[...], w_ref[:, 4 * HW + SW:]))
        xh, r = _rms(x_ref[...])
        dxn, dg_rows = _rms_bwd(dh, xh, r, g_ref[...])
        dg_ref[...] += _colsum(dg_rows)
        dx_ref[...] = dx1_ref[...] + dxn

    row = lambda n: pl.BlockSpec((tm, n), lambda i: (i, 0))
    return _pc(
        body, name="in_bwd", grid=(T // tm,),
        in_specs=[row(4 * HW), row(SW), row(2 * D), row(D), row(D), _whole(w.shape), _whole((1, D))],
        out_specs=[row(D), _acc((1, D))],
        out_shape=[jax.ShapeDtypeStruct((T, D), f32), jax.ShapeDtypeStruct((1, D), f32)],
        compiler_params=_params(),
    )(d_hg, d_u, d_gt, x, dx1, w, g)


def _wgrad(name, a, b, nj=None, a_blk=None, a_idx=None, b_blk=None, b_idx=None):
    T = a.shape[0]
    tm = 512
    dense = nj is None
    if dense:
        K, N = a.shape[1], b.shape[1]
        a_blk, a_idx = K, (lambda j: 0)
        b_blk = N
        while K * b_blk * 4 > 6 * 1024 * 1024 and b_blk % 256 == 0:
            b_blk //= 2
        nj, b_idx = N // b_blk, (lambda j: j)

    def body(a_ref, b_ref, o_ref):
        @pl.when(pl.program_id(1) == 0)
        def _():
            o_ref[...] = jnp.zeros_like(o_ref)

        o_ref[0] += _dot_tn(a_ref[...], b_ref[...])

    out = _pc(
        body, name=name, grid=(nj, T // tm),
        in_specs=[pl.BlockSpec((tm, a_blk), lambda j, i: (i, a_idx(j))), pl.BlockSpec((tm, b_blk), lambda j, i: (i, b_idx(j)))],
        out_specs=pl.BlockSpec((1, a_blk, b_blk), lambda j, i: (j, 0, 0)),
        out_shape=jax.ShapeDtypeStruct((nj, a_blk, b_blk), f32),
        compiler_params=_params(2),
    )(a, b)
    if dense:
        return out[0] if nj == 1 else jnp.transpose(out, (1, 0, 2)).reshape(a.shape[1], b.shape[1])
    return out


ANY = pl.BlockSpec(memory_space=pl.ANY)


def _all_gather(name, shard):
    R, C = shard.shape

    def body(x_ref, out_ref, send_sems, recv_sems, local_sem):
        x, y, c = lax.axis_index("x"), lax.axis_index("y"), lax.axis_index("c")
        me, sibling = (x, y, c), (x, y, 1 - c)
        chips = [(1 - x, y), (x, 1 - y), (1 - x, 1 - y)]

        def slot(px, py, pc):
            return out_ref.at[4 * px + 2 * py + pc]

        def copy(k, block, to, src=None):
            return pltpu.make_async_remote_copy(
                src_ref=slot(*block) if src is None else src, dst_ref=slot(*block),
                send_sem=send_sems.at[k], recv_sem=recv_sems.at[k], device_id=to, device_id_type=MESH)

        mine = pltpu.make_async_copy(x_ref, slot(*me), local_sem)
        mine.start()
        first = [copy(0, me, sibling, src=x_ref)]
        first += [copy(1 + j, me, (*chip, c), src=x_ref) for j, chip in enumerate(chips)]
        for cp in first:
            cp.start()
        passed = [copy(4 + j, (*chip, c), sibling) for j, chip in enumerate(chips)]
        for j, chip in enumerate(chips):
            copy(1 + j, (*chip, c), me).wait_recv()
            passed[j].start()
        copy(0, sibling, me).wait_recv()
        for j, chip in enumerate(chips):
            copy(4 + j, (*chip, 1 - c), me).wait_recv()
        for cp in first + passed:
            cp.wait_send()
        mine.wait()

    return _pc(
        body, name=name, in_specs=[ANY], out_specs=ANY,
        out_shape=jax.ShapeDtypeStruct((N_DEV, R, C), shard.dtype),
        scratch_shapes=[pltpu.SemaphoreType.DMA((7,)), pltpu.SemaphoreType.DMA((7,)), pltpu.SemaphoreType.DMA],
    )(shard)


def _swap_sibling(gs):
    n = len(gs)

    def body(*refs):
        g_refs, r_refs, (send_sems, recv_sems) = refs[:n], refs[n:2 * n], refs[2 * n:]
        x, y, c = lax.axis_index("x"), lax.axis_index("y"), lax.axis_index("c")
        cps = [pltpu.make_async_remote_copy(src_ref=g.at[k, 1 - c], dst_ref=r.at[k], send_sem=send_sems.at[4 * i + k],
                                            recv_sem=recv_sems.at[4 * i + k], device_id=(x, y, 1 - c), device_id_type=MESH)
               for i, (g, r) in enumerate(zip(g_refs, r_refs)) for k in range(4)]
        for cp in cps:
            cp.start()
        for cp in cps:
            cp.wait()

    return _pc(
        body, name="rs_sibling", in_specs=[ANY] * n, out_specs=[ANY] * n,
        out_shape=[jax.ShapeDtypeStruct((4, *g.shape[2:]), g.dtype) for g in gs],
        scratch_shapes=[pltpu.SemaphoreType.DMA((4 * n,)), pltpu.SemaphoreType.DMA((4 * n,))],
    )(*gs)


def _swap_chips(ps):
    n = len(ps)

    def body(*refs):
        p_refs, r_refs, (send_sems, recv_sems) = refs[:n], refs[n:2 * n], refs[2 * n:]
        x, y, c = lax.axis_index("x"), lax.axis_index("y"), lax.axis_index("c")
        chips = [(1 - x, y), (x, 1 - y), (1 - x, 1 - y)]
        cps = [pltpu.make_async_remote_copy(src_ref=p.at[2 * px + py], dst_ref=r.at[k], send_sem=send_sems.at[3 * i + k],
                                            recv_sem=recv_sems.at[3 * i + k], device_id=(px, py, c), device_id_type=MESH)
               for i, (p, r) in enumerate(zip(p_refs, r_refs)) for k, (px, py) in enumerate(chips)]
        for cp in cps:
            cp.start()
        for cp in cps:
            cp.wait()

    return _pc(
        body, name="rs_chips", in_specs=[ANY] * n, out_specs=[ANY] * n,
        out_shape=[jax.ShapeDtypeStruct((3, *p.shape[1:]), p.dtype) for p in ps],
        scratch_shapes=[pltpu.SemaphoreType.DMA((3 * n,)), pltpu.SemaphoreType.DMA((3 * n,))],
    )(*ps)


def _add_halves(name, g4, got, ids):
    _, _, K, c = g4.shape

    def body(ids_ref, a_ref, b_ref, p16_ref, own_ref):
        s = a_ref[0, 0] + b_ref[0]
        p16_ref[0] = s.astype(bf16)

        @pl.when(pl.program_id(0) == ids_ref[1])
        def _():
            own_ref[...] = s

    return _pc(
        body, name=name,
        grid_spec=pltpu.PrefetchScalarGridSpec(
            num_scalar_prefetch=1, grid=(4,),
            in_specs=[pl.BlockSpec((1, 1, K, c), lambda k, ids: (k, ids[0], 0, 0)),
                      pl.BlockSpec((1, K, c), lambda k, ids: (k, 0, 0))],
            out_specs=[pl.BlockSpec((1, K, c), lambda k, ids: (k, 0, 0)), pl.BlockSpec((K, c), lambda k, ids: (0, 0))]),
        out_shape=[jax.ShapeDtypeStruct((4, K, c), bf16), jax.ShapeDtypeStruct((K, c), f32)],
        compiler_params=_params(),
    )(ids, g4, got)


def _row_tile(K):
    for cand in (256, 176, 128, 64):
        if K % cand == 0:
            return cand
    return K


def _adam_shard(name, own, got3, w, m, v):
    K, c = own.shape
    tr = _row_tile(K)

    def body(own_ref, got_ref, w_ref, m_ref, v_ref, g_ref, d_ref, m2_ref, v2_ref):
        g = own_ref[...] + got_ref[0].astype(f32) + got_ref[1].astype(f32) + got_ref[2].astype(f32)
        g_ref[0] = g
        delta, m2, v2 = _adam_math(g, w_ref[0], m_ref[0], v_ref[0])
        d_ref[0] = delta
        m2_ref[0] = m2
        v2_ref[0] = v2

    blk = pl.BlockSpec((1, tr, c), lambda i: (0, i, 0))
    out = jax.ShapeDtypeStruct((1, K, c), f32)
    return _pc(
        body, name=name, grid=(K // tr,),
        in_specs=[pl.BlockSpec((tr, c), lambda i: (i, 0)), pl.BlockSpec((3, tr, c), lambda i: (0, i, 0)), blk, blk, blk],
        out_specs=[blk, blk, blk, blk], out_shape=[out, out, out, out], compiler_params=_params(),
    )(own, got3, w, m, v)


def _allreduce_small(grads):
    n = len(grads)
    shapes = [g.shape for g in grads]

    def body(*refs):
        g_refs, outs, recv = refs[0:n], refs[n:2 * n], refs[2 * n:5 * n]
        send_sems, recv_sems = refs[5 * n:]
        x, y, c = lax.axis_index("x"), lax.axis_index("y"), lax.axis_index("c")
        peers = [(x, y, 1 - c), (1 - x, y, c), (x, 1 - y, c)]
        for i in range(n):
            outs[i][...] = g_refs[i][...]
        for s, peer in enumerate(peers):
            cps = [pltpu.make_async_remote_copy(src_ref=outs[i], dst_ref=recv[s * n + i], send_sem=send_sems.at[s * n + i],
                                                recv_sem=recv_sems.at[s * n + i], device_id=peer, device_id_type=MESH)
                   for i in range(n)]
            for cp in cps:
                cp.start()
            for cp in cps:
                cp.wait()
            for i in range(n):
                outs[i][...] = outs[i][...] + recv[s * n + i][...]

    return _pc(
        body, name="allreduce_small", grid=(1,), in_specs=[_whole(s) for s in shapes], out_specs=[_acc(s) for s in shapes],
        out_shape=[jax.ShapeDtypeStruct(s, f32) for s in shapes],
        scratch_shapes=[pltpu.VMEM(s, f32) for s in shapes] * 3
        + [pltpu.SemaphoreType.DMA((3 * n,)), pltpu.SemaphoreType.DMA((3 * n,))],
        compiler_params=_params(),
    )(*grads)


def _adam_small(grads, ws, ms, vs):
    n = len(grads)
    shapes = [g.shape for g in grads]

    def body(*refs):
        g_refs, w_refs, m_refs, v_refs = refs[0:n], refs[n:2 * n], refs[2 * n:3 * n], refs[3 * n:4 * n]
        outs = refs[4 * n:8 * n]
        for i in range(n):
            g = g_refs[i][...]
            delta, m2, v2 = _adam_math(g, w_refs[i][...], m_refs[i][...], v_refs[i][...])
            outs[i][...] = g
            outs[n + i][...] = delta
            outs[2 * n + i][...] = m2
            outs[3 * n + i][...] = v2

    return _pc(
        body, name="adam_small", grid=(1,), in_specs=[_whole(s) for s in shapes] * 4, out_specs=[_acc(s) for s in shapes] * 4,
        out_shape=[jax.ShapeDtypeStruct(s, f32) for s in shapes] * 4, compiler_params=_params(),
    )(*grads, *ws, *ms, *vs)


def _adam_math(g, w, m, v):
    m2 = ADAM_B1 * m + (1.0 - ADAM_B1) * g
    v2 = ADAM_B2 * v + (1.0 - ADAM_B2) * (g * g)
    m_hat = m2 / (1.0 - ADAM_B1 ** ADAM_STEP)
    v_hat = v2 / (1.0 - ADAM_B2 ** ADAM_STEP)
    delta = -ADAM_LR * (m_hat / (jnp.sqrt(v_hat) + ADAM_EPS) + ADAM_WD * w)
    return delta, m2, v2


def _pack(arrs, dtype, row_mult):
    rows = []
    for a in arrs:
        flat = a.reshape(-1).astype(dtype)
        pad = (-flat.shape[0]) % LANES
        if pad:
            flat = jnp.concatenate([flat, jnp.zeros((pad,), dtype)])
        rows.append(flat.reshape(-1, LANES))
    out = jnp.concatenate(rows, axis=0)
    pad = (-out.shape[0]) % row_mult
    if pad:
        out = jnp.concatenate([out, jnp.zeros((pad, LANES), dtype)], axis=0)
    return out


def _unpack(buf, shapes):
    lead = buf.shape[:-2]
    outs, r = [], 0
    for shp in shapes:
        n = math.prod(shp)
        nr = -(-n // LANES)
        piece = buf[..., r:r + nr, :].reshape(*lead, nr * LANES)[..., :n]
        outs.append(piece.reshape(*lead, *shp))
        r += nr
    return outs


def _to_slabs(full, axis):
    shp = full.shape
    n = shp[axis] // N_DEV
    return jnp.moveaxis(full.reshape(*shp[:axis], N_DEV, n, *shp[axis + 1:]), axis, 0)


def _from_slabs(slabs, axis):
    t = jnp.moveaxis(slabs, 0, axis)
    shp = t.shape
    return t.reshape(*shp[:axis], shp[axis] * shp[axis + 1], *shp[axis + 2:])


def _s5_discretise(lam_re, lam_im, log_dt, b_re, b_im):
    dt = jnp.exp(log_dt)[:, None]
    mag = jnp.exp(lam_re * dt)
    a_re = mag * jnp.cos(lam_im * dt)
    a_im = mag * jnp.sin(lam_im * dt)
    den = lam_re * lam_re + lam_im * lam_im
    coef_re = ((a_re - 1.0) * lam_re + a_im * lam_im) / den
    coef_im = (a_im * lam_re - (a_re - 1.0) * lam_im) / den
    bbar_re = coef_re[..., None] * b_re - coef_im[..., None] * b_im
    bbar_im = coef_re[..., None] * b_im + coef_im[..., None] * b_re
    return a_re, a_im, bbar_re, bbar_im


def _s5_operands(bbar_re, bbar_im, c_re, c_im):
    eye = jnp.eye(SG // NST, dtype=f32)

    def b_op(bb):
        return jnp.einsum("sgnq,gh->sgqhn", bb.reshape(NST, SG // NST, SN, SP), eye).reshape(NST, 128, STW)

    def c_op(cc):
        return jnp.einsum("sgpn,gh->shngp", cc.reshape(NST, SG // NST, SP, SN), eye).reshape(NST, STW, 128)

    bdb = jnp.concatenate([b_op(bbar_re), b_op(bbar_im)], axis=0)
    bdc = jnp.concatenate([c_op(c_re), c_op(-c_im)], axis=0)
    return bdb, bdc


_BIG = ["w_in", "s5_glu_w", "w_branch_hg", "w_branch_s5", "w_out", "w_up", "w_down", "w_ple_gate", "w_ple_proj", "conv_w"]
_BIG_AXIS = {"w_in": 1, "s5_glu_w": 0, "w_branch_hg": 1, "w_branch_s5": 1, "w_out": 0, "w_up": 1, "w_down": 0,
             "w_ple_gate": 0, "w_ple_proj": 1, "conv_w": 1}
_SMALL = ["norm_mix_g", "hg_lb_logits", "hg_norm_g", "s5_lambda_re", "s5_lambda_im", "s5_log_dt", "s5_b_re", "s5_b_im",
          "s5_c_re", "s5_c_im", "s5_d", "s5_glu_b", "norm_ffn_g", "conv_b", "norm_ple_g", "norm_final_g"]
_ORDER = ["norm_mix_g", "w_in", "hg_lb_logits", "hg_norm_g", "s5_lambda_re", "s5_lambda_im", "s5_log_dt", "s5_b_re",
          "s5_b_im", "s5_c_re", "s5_c_im", "s5_d", "s5_glu_w", "s5_glu_b", "w_branch_hg", "w_branch_s5", "w_out",
          "norm_ffn_g", "w_up", "conv_w", "conv_b", "w_down", "norm_ple_g", "w_ple_gate", "w_ple_proj", "norm_final_g"]


def kernel(x, p, norm_mix_g, w_in, hg_lb_logits, hg_norm_g, s5_lambda_re, s5_lambda_im, s5_log_dt, s5_b_re, s5_b_im, s5_c_re, s5_c_im, s5_d, s5_glu_w, s5_glu_b, w_branch_hg, w_branch_s5, w_out, norm_ffn_g, w_up, conv_w, conv_b, w_down, norm_ple_g, w_ple_gate, w_ple_proj, norm_final_g, loss_target, m_norm_mix_g, m_w_in, m_hg_lb_logits, m_hg_norm_g, m_s5_lambda_re, m_s5_lambda_im, m_s5_log_dt, m_s5_b_re, m_s5_b_im, m_s5_c_re, m_s5_c_im, m_s5_d, m_s5_glu_w, m_s5_glu_b, m_w_branch_hg, m_w_branch_s5, m_w_out, m_norm_ffn_g, m_w_up, m_conv_w, m_conv_b, m_w_down, m_norm_ple_g, m_w_ple_gate, m_w_ple_proj, m_norm_final_g, v_norm_mix_g, v_w_in, v_hg_lb_logits, v_hg_norm_g, v_s5_lambda_re, v_s5_lambda_im, v_s5_log_dt, v_s5_b_re, v_s5_b_im, v_s5_c_re, v_s5_c_im, v_s5_d, v_s5_glu_w, v_s5_glu_b, v_w_branch_hg, v_w_branch_s5, v_w_out, v_norm_ffn_g, v_w_up, v_conv_w, v_conv_b, v_w_down, v_norm_ple_g, v_w_ple_gate, v_w_ple_proj, v_norm_final_g):
    W = dict(norm_mix_g=norm_mix_g, w_in=w_in, hg_lb_logits=hg_lb_logits, hg_norm_g=hg_norm_g, s5_lambda_re=s5_lambda_re, s5_lambda_im=s5_lambda_im, s5_log_dt=s5_log_dt, s5_b_re=s5_b_re, s5_b_im=s5_b_im, s5_c_re=s5_c_re, s5_c_im=s5_c_im, s5_d=s5_d, s5_glu_w=s5_glu_w, s5_glu_b=s5_glu_b, w_branch_hg=w_branch_hg, w_branch_s5=w_branch_s5, w_out=w_out, norm_ffn_g=norm_ffn_g, w_up=w_up, conv_w=conv_w, conv_b=conv_b, w_down=w_down, norm_ple_g=norm_ple_g, w_ple_gate=w_ple_gate, w_ple_proj=w_ple_proj, norm_final_g=norm_final_g)
    M = dict(norm_mix_g=m_norm_mix_g, w_in=m_w_in, hg_lb_logits=m_hg_lb_logits, hg_norm_g=m_hg_norm_g, s5_lambda_re=m_s5_lambda_re, s5_lambda_im=m_s5_lambda_im, s5_log_dt=m_s5_log_dt, s5_b_re=m_s5_b_re, s5_b_im=m_s5_b_im, s5_c_re=m_s5_c_re, s5_c_im=m_s5_c_im, s5_d=m_s5_d, s5_glu_w=m_s5_glu_w, s5_glu_b=m_s5_glu_b, w_branch_hg=m_w_branch_hg, w_branch_s5=m_w_branch_s5, w_out=m_w_out, norm_ffn_g=m_norm_ffn_g, w_up=m_w_up, conv_w=m_conv_w, conv_b=m_conv_b, w_down=m_w_down, norm_ple_g=m_norm_ple_g, w_ple_gate=m_w_ple_gate, w_ple_proj=m_w_ple_proj, norm_final_g=m_norm_final_g)
    V = dict(norm_mix_g=v_norm_mix_g, w_in=v_w_in, hg_lb_logits=v_hg_lb_logits, hg_norm_g=v_hg_norm_g, s5_lambda_re=v_s5_lambda_re, s5_lambda_im=v_s5_lambda_im, s5_log_dt=v_s5_log_dt, s5_b_re=v_s5_b_re, s5_b_im=v_s5_b_im, s5_c_re=v_s5_c_re, s5_c_im=v_s5_c_im, s5_d=v_s5_d, s5_glu_w=v_s5_glu_w, s5_glu_b=v_s5_glu_b, w_branch_hg=v_w_branch_hg, w_branch_s5=v_w_branch_s5, w_out=v_w_out, norm_ffn_g=v_norm_ffn_g, w_up=v_w_up, conv_w=v_conv_w, conv_b=v_conv_b, w_down=v_w_down, norm_ple_g=v_norm_ple_g, w_ple_gate=v_w_ple_gate, w_ple_proj=v_w_ple_proj, norm_final_g=v_norm_final_g)

    mm_names = _BIG[:-1]
    shard2 = {n: W[n][0] for n in _BIG}
    conv_bits = lax.bitcast_convert_type(shard2["conv_w"], bf16)
    wpack = _pack([shard2[n] for n in mm_names] + [conv_bits], bf16, 16)
    gathered = _all_gather("ag_weights", wpack)
    pieces = _unpack(gathered, [shard2[n].shape for n in mm_names] + [conv_bits.shape])
    full = {n: _from_slabs(pc, _BIG_AXIS[n]) for n, pc in zip(mm_names, pieces[:-1])}
    conv_w_full = _from_slabs(lax.bitcast_convert_type(pieces[-1], f32), 1)

    xt = x[0]
    pt = p[0, 0]
    tgt = loss_target[0]
    T = xt.shape[0]
    lam_re, lam_im, log_dt = s5_lambda_re[0], s5_lambda_im[0], s5_log_dt[0]
    b_re, b_im, c_re, c_im = s5_b_re[0], s5_b_im[0], s5_c_re[0], s5_c_im[0]

    def s5_prep(lam_re, lam_im, log_dt, b_re, b_im, c_re, c_im):
        a_re, a_im, bbar_re, bbar_im = _s5_discretise(lam_re, lam_im, log_dt, b_re, b_im)
        bdb, bdc = _s5_operands(bbar_re, bbar_im, c_re, c_im)
        return a_re, a_im, bdb, bdc

    (a_re, a_im, bdb, bdc), s5_prep_vjp = jax.vjp(s5_prep, lam_re, lam_im, log_dt, b_re, b_im, c_re, c_im)
    a_row = jnp.concatenate([a_re.reshape(1, SL), a_im.reshape(1, SL)], axis=1)
    bdb_b, bdc_b = bdb.astype(bf16), bdc.astype(bf16)

    h1, proj_hg, u_raw, gates = _in_proj(xt, norm_mix_g, full["w_in"])
    ng4 = jnp.tile(hg_norm_g, (1, NH))
    hg_o, sprev = _hgrn_fwd(proj_hg, hg_lb_logits, ng4)
    x_st, y_s5, g_s5, s5_o = _s5_fwd(u_raw, a_row, bdb_b, bdc_b, s5_d, full["s5_glu_w"], s5_glu_b)
    x1, merged, h2, a_up = _mix_up(xt, hg_o, s5_o, gates, full["w_branch_hg"], full["w_branch_s5"], full["w_out"],
                                   norm_ffn_g, full["w_up"])
    (dx2, gated, g_a, g_b, h3, dz_ple, dpp, loss_part, d_norm_final, d_norm_ple) = _ffn_tail(
        a_up, conv_w_full, conv_b, full["w_down"], x1, pt, norm_ple_g, full["w_ple_gate"], full["w_ple_proj"],
        norm_final_g.reshape(1, D), tgt)

    da_up, dx1, d_conv_w, d_conv_b, d_norm_ffn = _ffn_bwd(dx2, a_up, g_a, g_b, conv_w_full, full["w_down"],
                                                           full["w_up"], x1, norm_ffn_g)
    d_gates, d_hg_o, d_s5_o, dyh, dys = _mix_bwd(dx1, hg_o, s5_o, gates, full["w_branch_hg"], full["w_branch_s5"],
                                                   full["w_out"])
    d_proj_hg, d_lb, d_hg_norm = _hgrn_bwd(proj_hg, hg_lb_logits, ng4, sprev, d_hg_o)
    d_u, dz_glu, d_a_re, d_a_im, d_s5_d, d_glu_b, d_bdb, d_bdc = _s5_bwd(
        d_s5_o, y_s5, u_raw, x_st, a_row, bdb_b, bdc_b, s5_d, full["s5_glu_w"], s5_glu_b)
    grad_x, d_norm_mix = _in_bwd(d_proj_hg, d_u, d_gates, xt, dx1, full["w_in"], norm_mix_g)

    gw = {}
    gw["w_in"] = jnp.concatenate([_wgrad("wg_in_hg", h1, d_proj_hg), _wgrad("wg_in_u", h1, d_u),
                                  _wgrad("wg_in_gates", h1, d_gates)], axis=1)
    gw["s5_glu_w"] = _wgrad("wg_glu", g_s5, dz_glu)
    gw["w_branch_hg"] = _wgrad("wg_bhg", hg_o, dyh)
    gw["w_branch_s5"] = _wgrad("wg_bs5", s5_o, dys)
    gw["w_out"] = _wgrad("wg_out", merged, dx1)
    gw["w_up"] = _wgrad("wg_up", h2, da_up)
    gw["w_down"] = _wgrad("wg_down", gated, dx2)
    gw["w_ple_gate"] = _wgrad("wg_pg", h3, dz_ple)
    gw["w_ple_proj"] = _wgrad("wg_pp", pt, dpp)
    gw["conv_w"] = d_conv_w
    (d_lam_re, d_lam_im, d_log_dt, d_b_re, d_b_im, d_c_re, d_c_im) = s5_prep_vjp(
        (d_a_re.reshape(SG, SN), d_a_im.reshape(SG, SN), d_bdb, d_bdc))
    sm = jax.nn.softmax(hg_lb_logits, axis=0)
    d_l0 = d_lb[0] * sm[0] * sm[1]
    d_logits = jnp.stack([d_l0, -d_l0], axis=0)

    gs = {"norm_mix_g": d_norm_mix, "hg_lb_logits": d_logits, "hg_norm_g": d_hg_norm, "s5_lambda_re": d_lam_re,
          "s5_lambda_im": d_lam_im, "s5_log_dt": d_log_dt, "s5_b_re": d_b_re, "s5_b_im": d_b_im, "s5_c_re": d_c_re,
          "s5_c_im": d_c_im, "s5_d": d_s5_d, "s5_glu_b": d_glu_b, "norm_ffn_g": d_norm_ffn, "conv_b": d_conv_b,
          "norm_ple_g": d_norm_ple, "norm_final_g": d_norm_final}

    ids = jnp.stack([lax.axis_index("c"), 2 * lax.axis_index("x") + lax.axis_index("y")]).astype(jnp.int32)
    g4 = [_to_slabs(gw[n], _BIG_AXIS[n]).reshape(4, 2, *shard2[n].shape) for n in _BIG]
    got = _swap_sibling(g4)
    sums = [_add_halves("rs_add_" + n, g, r, ids) for n, g, r in zip(_BIG, g4, got)]
    got3 = _swap_chips([p16 for p16, _ in sums])
    big_out = [_adam_shard("adam_" + n, own, r3, W[n], M[n], V[n]) for n, (_, own), r3 in zip(_BIG, sums, got3)]

    two_d = lambda a: a.reshape(1, -1) if a.ndim == 1 else a
    g_sum = _allreduce_small([two_d(gs[n].reshape(W[n].shape)) for n in _SMALL])
    small_out = _adam_small(g_sum, [two_d(W[n]) for n in _SMALL], [two_d(M[n]) for n in _SMALL],
                            [two_d(V[n]) for n in _SMALL])

    res = {}
    for k in range(4):
        d = {n: big_out[i][k] for i, n in enumerate(_BIG)}
        d.update({n: small_out[k * len(_SMALL) + i].reshape(W[n].shape) for i, n in enumerate(_SMALL)})
        res[k] = d
    loss = lax.psum(loss_part[0, 0], ("x", "y", "c"))
    return (loss, grad_x[None], *[res[0][n] for n in _ORDER], *[res[1][n] for n in _ORDER],
            *[res[2][n] for n in _ORDER], *[res[3][n] for n in _ORDER])
```

```python
import functools
import math

import jax
import jax.numpy as jnp
from jax import lax
from jax.experimental import pallas as pl
from jax.experimental.pallas import tpu as pltpu

f32 = jnp.float32
bf16 = jnp.bfloat16
MESH = pl.DeviceIdType.MESH

N_DEV = 8
D = 1024
HW = 512
HD = 128
NH = 4
CH = 64
SW = 512
SG = 32
SP = 16
SN = 64
SL = SG * SN
NST = 4
STW = SL // NST
DFF = 2816
PLE = 256
EPS = 1e-6
LANES = 1024
VMEM_LIMIT = 56 * 1024 * 1024

ADAM_LR, ADAM_B1, ADAM_B2, ADAM_EPS, ADAM_WD, ADAM_STEP = 0.001, 0.9, 0.999, 1e-08, 0.01, 10


def _pc(body, **kw):
    return pl.pallas_call(body, **kw)


def _params(n_axes=1, **kw):
    return pltpu.CompilerParams(dimension_semantics=("arbitrary",) * n_axes, vmem_limit_bytes=VMEM_LIMIT, **kw)


def _whole(shape):
    nd = len(shape)
    return pl.BlockSpec(shape, lambda *_: (0,) * nd, pipeline_mode=pl.Buffered(1))


def _acc(shape):
    nd = len(shape)
    return pl.BlockSpec(shape, lambda *_: (0,) * nd)


def _dot(a, b):
    return jnp.dot(a.astype(bf16), b.astype(bf16), preferred_element_type=f32)


def _dot_nt(a, b):
    return lax.dot_general(a.astype(bf16), b.astype(bf16), (((1,), (1,)), ((), ())), preferred_element_type=f32)


def _dot_tn(a, b):
    return lax.dot_general(a.astype(bf16), b.astype(bf16), (((0,), (0,)), ((), ())), preferred_element_type=f32)


def _sig(x):
    return jax.nn.sigmoid(x)


def _dsilu(z, s):
    return s * (1.0 + z * (1.0 - s))


_GC = math.sqrt(2.0 / math.pi)


def _gelu_and_grad(y):
    t = jnp.tanh(_GC * (y + 0.044715 * y * y * y))
    g = 0.5 * y * (1.0 + t)
    dg = 0.5 * (1.0 + t) + 0.5 * y * (1.0 - t * t) * _GC * (1.0 + 3.0 * 0.044715 * y * y)
    return g, dg


def _rms(x):
    r = lax.rsqrt(jnp.mean(x * x, axis=-1, keepdims=True) + EPS)
    return x * r, r


def _rms_bwd(dy, xh, r, g):
    dxh = dy * g
    dx = r * (dxh - xh * jnp.mean(dxh * xh, axis=-1, keepdims=True))
    return dx, dy * xh


def _colsum(x):
    return jnp.sum(x, axis=0, keepdims=True)


def _in_proj(x, g, w, ag_shard):
    T = x.shape[0]
    tm = 256
    nt = T // tm

    def body(x_ref, g_ref, w_ref, ag_ref, h_ref, hg_ref, u_ref, gt_ref, ago_ref, send_sems, recv_sems, local_sem):
        i = pl.program_id(0)
        start, forward, finish = _ag_steps(ag_ref, ago_ref, send_sems, recv_sems, local_sem)
        pl.when(i == 0)(start)
        xh, _ = _rms(x_ref[...])
        h = (xh * g_ref[...]).astype(bf16)
        h_ref[...] = h
        hg_ref[...] = jnp.dot(h, w_ref[:, 0:4 * HW], preferred_element_type=f32)
        u_ref[...] = jnp.dot(h, w_ref[:, 4 * HW:4 * HW + SW], preferred_element_type=f32)
        gt_ref[...] = jnp.dot(h, w_ref[:, 4 * HW + SW:], preferred_element_type=f32)
        pl.when(i == nt // 2)(forward)
        pl.when(i == nt - 1)(finish)

    row = lambda n: pl.BlockSpec((tm, n), lambda i: (i, 0))
    return _pc(
        body, name="in_proj", grid=(nt,),
        in_specs=[row(D), _whole((1, D)), _whole(w.shape), ANY],
        out_specs=[row(D), row(4 * HW), row(SW), row(2 * D), ANY],
        out_shape=[jax.ShapeDtypeStruct((T, D), bf16), jax.ShapeDtypeStruct((T, 4 * HW), f32),
                   jax.ShapeDtypeStruct((T, SW), f32), jax.ShapeDtypeStruct((T, 2 * D), f32),
                   jax.ShapeDtypeStruct((N_DEV, *ag_shard.shape), ag_shard.dtype)],
        scratch_shapes=list(AG_SEMS),
        compiler_params=_params(),
    )(x, g, w, ag_shard)


HG_NC = 2


def _tri_matmul(tri, x):
    hi = x.astype(bf16)
    r1 = x - hi.astype(f32)
    mid = r1.astype(bf16)
    lo = (r1 - mid.astype(f32)).astype(bf16)
    n = x.shape[1]
    out = jnp.dot(tri.astype(bf16), jnp.concatenate([hi, mid, lo], axis=1), preferred_element_type=f32)
    return out[:, 0:n] + out[:, n:2 * n] + out[:, 2 * n:3 * n]


def _hgrn_gates(lg, qr, fr):
    mx = jnp.max(lg, axis=0, keepdims=True)
    e = jnp.exp(lg - mx)
    lb = e[0:1, :] / (e[0:1, :] + e[1:2, :])
    sig = _sig(fr)
    f = lb + (1.0 - lb) * sig
    k = 1.0 - f
    r_i = lax.broadcasted_iota(jnp.int32, (CH, CH), 0)
    c_i = lax.broadcasted_iota(jnp.int32, (CH, CH), 1)
    tril = (r_i >= c_i)
    b = _tri_matmul(tril, jnp.log(f))
    bref = b[CH // 2:CH // 2 + 1, :]
    blast = b[CH - 1:CH, :]
    sq = _sig(qr)
    q = qr * sq
    e1 = jnp.exp(b - bref)
    e2 = jnp.exp(bref - b)
    e3 = jnp.exp(blast - b)
    e4 = jnp.exp(b)
    return dict(lb=lb, qr=qr, sq=sq, sig=sig, f=f, k=k, tril=tril, triu=(c_i >= r_i), e1=e1, e2=e2, e3=e3, e4=e4,
                qs=q * e1, ks=k * e2, kl=k * e3, qb=q * e4, dec=jnp.exp(blast))


def _hgrn_fwd(proj_hg, logits, ng4):
    T = proj_hg.shape[0]
    nch = T // CH
    tm = HG_NC * CH

    def body(q_ref, f_ref, i_ref, og_ref, lg_ref, ng_ref, out_ref, sprev_ref, st_ref):
        @pl.when(pl.program_id(0) == 0)
        def _():
            st_ref[...] = jnp.zeros_like(st_ref)

        for ci in range(HG_NC):
            rows = slice(ci * CH, (ci + 1) * CH)
            c = _hgrn_gates(lg_ref[...], q_ref[rows, :], f_ref[rows, :])
            v = i_ref[rows, :]
            og = og_ref[rows, :]
            ohs = []
            for h in range(NH):
                sl = slice(h * HD, (h + 1) * HD)
                p = jnp.where(c["tril"], _dot_nt(c["qs"][:, sl], c["ks"][:, sl]), 0.0)
                st = st_ref[h]
                sprev_ref[ci, h] = st
                o = _dot(p, v[:, sl]) + _dot_nt(c["qb"][:, sl], st)
                st_ref[h] = c["dec"][:, sl] * st + _dot_tn(v[:, sl], c["kl"][:, sl])
                ohs.append(_rms(o)[0])
            out_ref[rows, :] = (jnp.concatenate(ohs, axis=1) * ng_ref[...] * (og * _sig(og))).astype(bf16)

    col = lambda j: pl.BlockSpec((tm, HW), lambda n, j=j: (n, j))
    return _pc(
        body, name="hgrn_fwd", grid=(nch // HG_NC,),
        in_specs=[col(0), col(1), col(2), col(3), _whole((2, HW)), _whole((1, HW))],
        out_specs=[pl.BlockSpec((tm, HW), lambda n: (n, 0)),
                   pl.BlockSpec((HG_NC, NH, HD, HD), lambda n: (n, 0, 0, 0))],
        out_shape=[jax.ShapeDtypeStruct((T, HW), bf16), jax.ShapeDtypeStruct((nch, NH, HD, HD), f32)],
        scratch_shapes=[pltpu.VMEM((NH, HD, HD), f32)],
        compiler_params=_params(),
    )(proj_hg, proj_hg, proj_hg, proj_hg, logits, ng4)


def _hgrn_bwd(proj_hg, logits, ng4, sprev, d_out):
    T = proj_hg.shape[0]
    nch = T // CH
    tm = HG_NC * CH
    nst = nch // HG_NC

    def body(q_ref, f_ref, i_ref, og_ref, lg_ref, ng_ref, sp_ref, do_ref, dp_ref, dlb_ref, dng_ref, gt_ref):
        @pl.when(pl.program_id(0) == 0)
        def _():
            gt_ref[...] = jnp.zeros_like(gt_ref)
            dlb_ref[...] = jnp.zeros_like(dlb_ref)
            dng_ref[...] = jnp.zeros_like(dng_ref)

        row = lax.broadcasted_iota(jnp.int32, (CH, HW), 0)
        for ci in reversed(range(HG_NC)):
            rows = slice(ci * CH, (ci + 1) * CH)
            c = _hgrn_gates(lg_ref[...], q_ref[rows, :], f_ref[rows, :])
            tril = c["tril"]
            v = i_ref[rows, :]
            og = og_ref[rows, :]
            sog = _sig(og)
            d_gated = do_ref[rows, :]
            d_on_all = d_gated * (og * sog)
            parts = {n: [] for n in ("dqs", "dks", "dkl", "dqb", "dv", "ohg", "ddec")}
            for h in range(NH):
                sl = slice(h * HD, (h + 1) * HD)
                ng_h = ng_ref[:, sl]
                qs, ks, kl, qb, vh = c["qs"][:, sl], c["ks"][:, sl], c["kl"][:, sl], c["qb"][:, sl], v[:, sl]
                st = sp_ref[ci, h]
                gt = gt_ref[h]
                p = jnp.where(tril, _dot_nt(qs, ks), 0.0)
                o = _dot(p, vh) + _dot_nt(qb, st)
                oh, r = _rms(o)
                d_o, dng_rows = _rms_bwd(d_on_all[:, sl], oh, r, ng_h)
                dng_ref[...] += _colsum(dng_rows)
                dp = jnp.where(tril, _dot_nt(d_o, vh), 0.0)
                parts["dqb"].append(_dot(d_o, st))
                parts["dv"].append(_dot_tn(p, d_o) + _dot_nt(kl, gt))
                parts["dqs"].append(_dot(dp, ks))
                parts["dks"].append(_dot_tn(dp, qs))
                parts["dkl"].append(_dot(vh, gt))
                parts["ddec"].append(_colsum(gt * st))
                parts["ohg"].append(oh * ng_h)
                gt_ref[h] = _dot_tn(d_o, qb) + c["dec"][:, sl] * gt
            cat = {n: jnp.concatenate(vs, axis=1) for n, vs in parts.items()}
            dqs, dks, dkl, dqb = cat["dqs"], cat["dks"], cat["dkl"], cat["dqb"]
            dq = dqs * c["e1"] + dqb * c["e4"]
            dk = dks * c["e2"] + dkl * c["e3"]
            t_qs = dqs * c["qs"]
            t_ks = dks * c["ks"]
            t_kl = dkl * c["kl"]
            db = t_qs - t_ks - t_kl + dqb * c["qb"]
            dbref = _colsum(t_ks - t_qs)
            dblast = _colsum(t_kl) + cat["ddec"] * c["dec"]
            db = db + jnp.where(row == CH // 2, dbref, 0.0) + jnp.where(row == CH - 1, dblast, 0.0)
            df = _tri_matmul(c["triu"], db) / c["f"] - dk
            sig = c["sig"]
            dlb_ref[...] += _colsum(df * (1.0 - sig))
            dp_ref[rows, 0:HW] = dq * _dsilu(c["qr"], c["sq"])
            dp_ref[rows, HW:2 * HW] = df * (1.0 - c["lb"]) * sig * (1.0 - sig)
            dp_ref[rows, 2 * HW:3 * HW] = cat["dv"]
            dp_ref[rows, 3 * HW:4 * HW] = d_gated * cat["ohg"] * _dsilu(og, sog)

    rev = lambda n: nst - 1 - n
    col = lambda j: pl.BlockSpec((tm, HW), lambda n, j=j: (rev(n), j))
    return _pc(
        body, name="hgrn_bwd", grid=(nst,),
        in_specs=[col(0), col(1), col(2), col(3), _whole((2, HW)), _whole((1, HW)),
                  pl.BlockSpec((HG_NC, NH, HD, HD), lambda n: (rev(n), 0, 0, 0)),
                  pl.BlockSpec((tm, HW), lambda n: (rev(n), 0))],
        out_specs=[pl.BlockSpec((tm, 4 * HW), lambda n: (rev(n), 0)), _acc((1, HW)), _acc((1, HD))],
        out_shape=[jax.ShapeDtypeStruct((T, 4 * HW), f32), jax.ShapeDtypeStruct((1, HW), f32),
                   jax.ShapeDtypeStruct((1, HD), f32)],
        scratch_shapes=[pltpu.VMEM((NH, HD, HD), f32)],
        compiler_params=_params(),
    )(proj_hg, proj_hg, proj_hg, proj_hg, logits, ng4, sprev, d_out)


S5_TM = 256
S5_SEG = 8
S5_STEPS = S5_TM // S5_SEG
NLT = SL // 128


def _s5_tables(a_ref, pw_ref, pseg_ref, descending):
    re, im = slice(0, SL), slice(SL, 2 * SL)

    def cmul(ar, ai, br, bi):
        return ar * br - ai * bi, ar * bi + ai * br

    pw_ref[0:1, :] = a_ref[...]
    m = 1
    while m < S5_STEPS:
        pr, pi = cmul(pw_ref[0:m, re], pw_ref[0:m, im], pw_ref[m - 1:m, re], pw_ref[m - 1:m, im])
        pw_ref[m:2 * m, re] = pr
        pw_ref[m:2 * m, im] = pi
        m *= 2
    base = S5_STEPS - 1
    if descending:
        pseg_ref[7:8, :] = pw_ref[base:base + 1, :]
        m = 1
        while m < 8:
            pr, pi = cmul(pseg_ref[8 - m:8, re], pseg_ref[8 - m:8, im], pseg_ref[8 - m:9 - m, re], pseg_ref[8 - m:9 - m, im])
            pseg_ref[8 - 2 * m:8 - m, re] = pr
            pseg_ref[8 - 2 * m:8 - m, im] = pi
            m *= 2
    else:
        pseg_ref[0:1, :] = pw_ref[base:base + 1, :]
        m = 1
        while m < 8:
            pr, pi = cmul(pseg_ref[0:m, re], pseg_ref[0:m, im], pseg_ref[m - 1:m, re], pseg_ref[m - 1:m, im])
            pseg_ref[m:2 * m, re] = pr
            pseg_ref[m:2 * m, im] = pi
            m *= 2


def _seg_rows(j):
    return pl.ds(j * S5_SEG, S5_SEG)


def _seg_perm(transpose=False):
    r_i = lax.broadcasted_iota(jnp.int32, (S5_TM, S5_TM), 0)
    c_i = lax.broadcasted_iota(jnp.int32, (S5_TM, S5_TM), 1)
    if transpose:
        r_i, c_i = c_i, r_i
    return c_i == S5_STEPS * (r_i % S5_SEG) + r_i // S5_SEG


def _scan_fwd(x3_ref, pw_ref, pseg_ref, carry_ref):
    row8 = lax.broadcasted_iota(jnp.int32, (S5_SEG, 128), 0)
    for lt in range(NLT):
        kr, ki = lt, NLT + lt
        lr, li = slice(lt * 128, (lt + 1) * 128), slice(SL + lt * 128, SL + (lt + 1) * 128)
        ar, ai = pw_ref[0:1, lr], pw_ref[0:1, li]
        sr = jnp.zeros((S5_SEG, 128), f32)
        si = jnp.zeros((S5_SEG, 128), f32)
        for j in range(S5_STEPS):
            sr, si = ar * sr - ai * si + x3_ref[kr, _seg_rows(j), :], ar * si + ai * sr + x3_ref[ki, _seg_rows(j), :]
            x3_ref[kr, _seg_rows(j), :] = sr
            x3_ref[ki, _seg_rows(j), :] = si
        for d in (1, 2, 4):
            pr, pi = pseg_ref[d - 1:d, lr], pseg_ref[d - 1:d, li]
            tr, ti = pltpu.roll(sr, d, 0), pltpu.roll(si, d, 0)
            m = row8 >= d
            sr, si = sr + jnp.where(m, pr * tr - pi * ti, 0.0), si + jnp.where(m, pr * ti + pi * tr, 0.0)
        c0r, c0i = carry_ref[7:8, lr], carry_ref[7:8, li]
        qr, qi = pseg_ref[:, lr], pseg_ref[:, li]
        sr, si = sr + qr * c0r - qi * c0i, si + qr * c0i + qi * c0r
        carry_ref[:, lr] = sr
        carry_ref[:, li] = si
        cr = jnp.where(row8 == 0, c0r, pltpu.roll(sr, 1, 0))
        ci = jnp.where(row8 == 0, c0i, pltpu.roll(si, 1, 0))
        for j in range(S5_STEPS):
            pr, pi = pw_ref[j:j + 1, lr], pw_ref[j:j + 1, li]
            x3_ref[kr, _seg_rows(j), :] = x3_ref[kr, _seg_rows(j), :] + pr * cr - pi * ci
            x3_ref[ki, _seg_rows(j), :] = x3_ref[ki, _seg_rows(j), :] + pr * ci + pi * cr


def _scan_bwd(g3_ref, x3_ref, xh_ref, first, pw_ref, pseg_ref, carry_ref, dar_ref, dai_ref):
    row8 = lax.broadcasted_iota(jnp.int32, (S5_SEG, 128), 0)
    for lt in range(NLT):
        kr, ki = lt, NLT + lt
        lr, li = slice(lt * 128, (lt + 1) * 128), slice(SL + lt * 128, SL + (lt + 1) * 128)
        ar, ai = pw_ref[0:1, lr], pw_ref[0:1, li]
        sr = jnp.zeros((S5_SEG, 128), f32)
        si = jnp.zeros((S5_SEG, 128), f32)
        for j in reversed(range(S5_STEPS)):
            sr, si = ar * sr + ai * si + g3_ref[kr, _seg_rows(j), :], ar * si - ai * sr + g3_ref[ki, _seg_rows(j), :]
            g3_ref[kr, _seg_rows(j), :] = sr
            g3_ref[ki, _seg_rows(j), :] = si
        for d in (1, 2, 4):
            pr, pi = pseg_ref[8 - d:9 - d, lr], pseg_ref[8 - d:9 - d, li]
            tr, ti = pltpu.roll(sr, 8 - d, 0), pltpu.roll(si, 8 - d, 0)
            m = row8 < 8 - d
            sr, si = sr + jnp.where(m, pr * tr + pi * ti, 0.0), si + jnp.where(m, pr * ti - pi * tr, 0.0)
        c0r, c0i = carry_ref[0:1, lr], carry_ref[0:1, li]
        qr, qi = pseg_ref[:, lr], pseg_ref[:, li]
        sr, si = sr + qr * c0r + qi * c0i, si + qr * c0i - qi * c0r
        carry_ref[:, lr] = sr
        carry_ref[:, li] = si
        cr = jnp.where(row8 == 7, c0r, pltpu.roll(sr, 7, 0))
        ci = jnp.where(row8 == 7, c0i, pltpu.roll(si, 7, 0))
        hr = jnp.where(first, 0.0, xh_ref[kr, 7:8, :])
        hi = jnp.where(first, 0.0, xh_ref[ki, 7:8, :])
        acc_r = jnp.zeros((S5_SEG, 128), f32)
        acc_i = jnp.zeros((S5_SEG, 128), f32)
        for j in range(S5_STEPS):
            pr, pi = pw_ref[S5_STEPS - 1 - j:S5_STEPS - j, lr], pw_ref[S5_STEPS - 1 - j:S5_STEPS - j, li]
            lam_r = g3_ref[kr, _seg_rows(j), :] + pr * cr + pi * ci
            lam_i = g3_ref[ki, _seg_rows(j), :] + pr * ci - pi * cr
            g3_ref[kr, _seg_rows(j), :] = lam_r
            g3_ref[ki, _seg_rows(j), :] = lam_i
            if j == 0:
                xpr = jnp.where(row8 == 0, hr, pltpu.roll(x3_ref[kr, _seg_rows(S5_STEPS - 1), :], 1, 0))
                xpi = jnp.where(row8 == 0, hi, pltpu.roll(x3_ref[ki, _seg_rows(S5_STEPS - 1), :], 1, 0))
            else:
                xpr = x3_ref[kr, _seg_rows(j - 1), :]
                xpi = x3_ref[ki, _seg_rows(j - 1), :]
            acc_r = acc_r + lam_r * xpr + lam_i * xpi
            acc_i = acc_i + lam_i * xpr - lam_r * xpi
        dar_ref[:, lr] += _colsum(acc_r)
        dai_ref[:, lr] += _colsum(acc_i)


def _strip(x3_ref, part, s):
    k0 = part * NLT + s * (STW // 128)
    return jnp.concatenate([x3_ref[k0 + q] for q in range(STW // 128)], axis=1)


def _s5_fwd(u, a_row, bdb, bdc, dskip, glu_w, glu_b, ag_shard):
    T = u.shape[0]
    tm = S5_TM
    nt = T // tm

    def body(u_ref, a_ref, bdb_ref, bdc_ref, ds_ref, gw_ref, gb_ref, ag_ref, x_ref, y_ref, g_ref, o_ref, ago_ref,
             pw_ref, pseg_ref, carry_ref, send_sems, recv_sems, local_sem):
        i = pl.program_id(0)
        start, forward, finish = _ag_steps(ag_ref, ago_ref, send_sems, recv_sems, local_sem)
        pl.when(i == 0)(start)

        @pl.when(i == 0)
        def _():
            carry_ref[...] = jnp.zeros_like(carry_ref)
            _s5_tables(a_ref, pw_ref, pseg_ref, descending=False)

        uv = u_ref[...]
        ub = jnp.dot(_seg_perm().astype(bf16), uv.astype(bf16), preferred_element_type=f32).astype(bf16)
        for part in range(2):
            for s in range(NST):
                bu = jnp.dot(ub[:, s * 128:(s + 1) * 128], bdb_ref[part * NST + s], preferred_element_type=f32)
                for q in range(STW // 128):
                    x_ref[part * NLT + s * (STW // 128) + q] = bu[:, q * 128:(q + 1) * 128]
        _scan_fwd(x_ref, pw_ref, pseg_ref, carry_ref)
        ys = []
        for s in range(NST):
            acc = None
            for part in range(2):
                t = jnp.dot(_strip(x_ref, part, s).astype(bf16), bdc_ref[part * NST + s], preferred_element_type=f32)
                acc = t if acc is None else acc + t
            ys.append(acc)
        y = _tri_matmul(_seg_perm(transpose=True), jnp.concatenate(ys, axis=1)) + ds_ref[...] * uv
        y_ref[...] = y
        g, _ = _gelu_and_grad(y)
        gb = g.astype(bf16)
        g_ref[...] = gb
        z = jnp.dot(gb, gw_ref[...], preferred_element_type=f32) + gb_ref[...]
        o_ref[...] = (g * _sig(z)).astype(bf16)
        pl.when(i == nt // 2)(forward)
        pl.when(i == nt - 1)(finish)

    row = lambda n: pl.BlockSpec((tm, n), lambda i: (i, 0))
    return _pc(
        body, name="s5_fwd", grid=(nt,),
        in_specs=[row(SW), _whole((1, 2 * SL)), _whole(bdb.shape), _whole(bdc.shape), _whole((1, SW)),
                  _whole((SW, SW)), _whole((1, SW)), ANY],
        out_specs=[pl.BlockSpec((2 * NLT, tm, 128), lambda i: (0, i, 0)), row(SW), row(SW), row(SW), ANY],
        out_shape=[jax.ShapeDtypeStruct((2 * NLT, T, 128), f32), jax.ShapeDtypeStruct((T, SW), f32),
                   jax.ShapeDtypeStruct((T, SW), bf16), jax.ShapeDtypeStruct((T, SW), bf16),
                   jax.ShapeDtypeStruct((N_DEV, *ag_shard.shape), ag_shard.dtype)],
        scratch_shapes=[pltpu.VMEM((S5_STEPS, 2 * SL), f32), pltpu.VMEM((8, 2 * SL), f32), pltpu.VMEM((8, 2 * SL), f32)]
        + list(AG_SEMS),
        compiler_params=_params(),
    )(u, a_row, bdb, bdc, dskip, glu_w, glu_b, ag_shard)


def _s5_bwd(d_out, y, u, x, a_row, bdb, bdc, dskip, glu_w, glu_b):
    T = u.shape[0]
    tm = S5_TM
    nt = T // tm

    def body(do_ref, y_ref, u_ref, x_ref, xh_ref, a_ref, bdb_ref, bdc_ref, ds_ref, gw_ref, gb_ref,
             du_ref, dz_ref, dar_ref, dai_ref, dd_ref, dgb_ref, dbdb_ref, dbdc_ref, gs_ref, pw_ref, pseg_ref, carry_ref):
        i = pl.program_id(0)

        @pl.when(i == 0)
        def _():
            carry_ref[...] = jnp.zeros_like(carry_ref)
            _s5_tables(a_ref, pw_ref, pseg_ref, descending=True)
            dar_ref[...] = jnp.zeros_like(dar_ref)
            dai_ref[...] = jnp.zeros_like(dai_ref)
            dd_ref[...] = jnp.zeros_like(dd_ref)
            dgb_ref[...] = jnp.zeros_like(dgb_ref)
            dbdb_ref[...] = jnp.zeros_like(dbdb_ref)
            dbdc_ref[...] = jnp.zeros_like(dbdc_ref)

        yv = y_ref[...]
        uv = u_ref[...]
        g, gp = _gelu_and_grad(yv)
        z = jnp.dot(g.astype(bf16), gw_ref[...], preferred_element_type=f32) + gb_ref[...]
        sg = _sig(z)
        do = do_ref[...].astype(f32)
        dz = do * g * sg * (1.0 - sg)
        dz_ref[...] = dz.astype(bf16)
        dgb_ref[...] += _colsum(dz)
        dy = (do * sg + _dot_nt(dz, gw_ref[...])) * gp
        perm = _seg_perm().astype(bf16)
        dyb = jnp.dot(perm, dy.astype(bf16), preferred_element_type=f32).astype(bf16)
        dd_ref[...] += _colsum(dy * uv)
        for part in range(2):
            for s in range(NST):
                gx = lax.dot_general(dyb[:, s * 128:(s + 1) * 128], bdc_ref[part * NST + s], (((1,), (1,)), ((), ())),
                                     preferred_element_type=f32)
                for q in range(STW // 128):
                    gs_ref[part * NLT + s * (STW // 128) + q] = gx[:, q * 128:(q + 1) * 128]
        _scan_bwd(gs_ref, x_ref, xh_ref, i == nt - 1, pw_ref, pseg_ref, carry_ref, dar_ref, dai_ref)
        ub = jnp.dot(perm, uv.astype(bf16), preferred_element_type=f32).astype(bf16)
        dus = []
        for s in range(NST):
            acc = None
            for part in range(2):
                lv = _strip(gs_ref, part, s).astype(bf16)
                t = lax.dot_general(lv, bdb_ref[part * NST + s], (((1,), (1,)), ((), ())), preferred_element_type=f32)
                acc = t if acc is None else acc + t
                dbdb_ref[part * NST + s] += _dot_tn(ub[:, s * 128:(s + 1) * 128], lv)
                dbdc_ref[part * NST + s] += _dot_tn(_strip(x_ref, part, s), dyb[:, s * 128:(s + 1) * 128])
            dus.append(acc)
        du_ref[...] = _tri_matmul(_seg_perm(transpose=True), jnp.concatenate(dus, axis=1)) + dy * ds_ref[...]

    rev = lambda i: nt - 1 - i
    row = lambda n: pl.BlockSpec((tm, n), lambda i: (rev(i), 0))
    xblk = pl.BlockSpec((2 * NLT, tm, 128), lambda i: (0, rev(i), 0))
    halo = pl.BlockSpec((2 * NLT, 8, 128), lambda i: (0, jnp.maximum(rev(i) * (tm // 8) - 1, 0), 0))
    return _pc(
        body, name="s5_bwd", grid=(nt,),
        in_specs=[row(SW), row(SW), row(SW), xblk, halo, _whole((1, 2 * SL)), _whole(bdb.shape), _whole(bdc.shape),
                  _whole((1, SW)), _whole((SW, SW)), _whole((1, SW))],
        out_specs=[row(SW), row(SW), _acc((1, SL)), _acc((1, SL)), _acc((1, SW)), _acc((1, SW)),
                   _acc(bdb.shape), _acc(bdc.shape)],
        out_shape=[jax.ShapeDtypeStruct((T, SW), f32), jax.ShapeDtypeStruct((T, SW), bf16),
                   jax.ShapeDtypeStruct((1, SL), f32), jax.ShapeDtypeStruct((1, SL), f32),
                   jax.ShapeDtypeStruct((1, SW), f32), jax.ShapeDtypeStruct((1, SW), f32),
                   jax.ShapeDtypeStruct(bdb.shape, f32), jax.ShapeDtypeStruct(bdc.shape, f32)],
        scratch_shapes=[pltpu.VMEM((2 * NLT, tm, 128), f32), pltpu.VMEM((S5_STEPS, 2 * SL), f32),
                        pltpu.VMEM((8, 2 * SL), f32), pltpu.VMEM((8, 2 * SL), f32)],
        compiler_params=_params(),
    )(d_out, y, u, x, x, a_row, bdb, bdc, dskip, glu_w, glu_b)


def _mix_up(x, hg_o, s5_o, gates, w_bhg, w_bs5, w_out, g_ffn, w_up):
    T = x.shape[0]
    tm = 256

    def body(x_ref, hg_ref, s5_ref, gt_ref, wh_ref, ws_ref, wo_ref, g_ref, wu_ref, x1_ref, mg_ref, h2_ref, a_ref):
        yh = jnp.dot(hg_ref[...], wh_ref[...], preferred_element_type=f32)
        ys = jnp.dot(s5_ref[...], ws_ref[...], preferred_element_type=f32)
        merged = (_sig(gt_ref[:, 0:D]) * yh + _sig(gt_ref[:, D:2 * D]) * ys).astype(bf16)
        mg_ref[...] = merged
        x1 = x_ref[...] + jnp.dot(merged, wo_ref[...], preferred_element_type=f32)
        x1_ref[...] = x1
        xh, _ = _rms(x1)
        h2 = (xh * g_ref[...]).astype(bf16)
        h2_ref[...] = h2
        a_ref[...] = jnp.dot(h2, wu_ref[...], preferred_element_type=f32)

    row = lambda n: pl.BlockSpec((tm, n), lambda i: (i, 0))
    return _pc(
        body, name="mix_up", grid=(T // tm,),
        in_specs=[row(D), row(HW), row(SW), row(2 * D), _whole(w_bhg.shape), _whole(w_bs5.shape), _whole(w_out.shape),
                  _whole((1, D)), _whole(w_up.shape)],
        out_specs=[row(D), row(D), row(D), row(2 * DFF)],
        out_shape=[jax.ShapeDtypeStruct((T, D), f32), jax.ShapeDtypeStruct((T, D), bf16),
                   jax.ShapeDtypeStruct((T, D), bf16), jax.ShapeDtypeStruct((T, 2 * DFF), f32)],
        compiler_params=_params(),
    )(x, hg_o, s5_o, gates, w_bhg, w_bs5, w_out, g_ffn, w_up)


FFN_TM = 128


def _conv_rows(a, halo, first, conv_w, conv_b):
    tm = a.shape[0]
    row = lax.broadcasted_iota(jnp.int32, (tm, 1), 0)
    hm1 = jnp.where(first, 0.0, halo[7:8, :])
    hm2 = jnp.where(first, 0.0, halo[6:7, :])
    a1 = jnp.where(row == 0, hm1, pltpu.roll(a, 1, 0))
    a2 = jnp.where(row == 0, hm2, jnp.where(row == 1, hm1, pltpu.roll(a, 2, 0)))
    c = conv_b + conv_w[0:1, :] * a2 + conv_w[1:2, :] * a1 + conv_w[2:3, :] * a
    return c, a1, a2


def _ffn_tail(a, conv_w, conv_b, w_down, x1, p, g_ple, w_pg, w_pp, g_fin, tgt):
    T = a.shape[0]
    tm = FFN_TM

    def body(a_ref, ah_ref, cw_ref, cb_ref, wd_ref, x1_ref, p_ref, gp_ref, wpg_ref, wpp_ref, gf_ref, t_ref,
             dx2_ref, gd_ref, ga_ref, gb_ref, h3_ref, dz_ref, dpp_ref, loss_ref, dgf_ref, dgp_ref):
        i = pl.program_id(0)

        @pl.when(i == 0)
        def _():
            loss_ref[...] = jnp.zeros_like(loss_ref)
            dgf_ref[...] = jnp.zeros_like(dgf_ref)
            dgp_ref[...] = jnp.zeros_like(dgp_ref)

        c, _, _ = _conv_rows(a_ref[...], ah_ref[...], i == 0, cw_ref[...], cb_ref[...])
        gl, gp = _gelu_and_grad(c[:, 0:DFF])
        ga_ref[...] = c[:, DFF:] * gp
        gb_ref[...] = gl
        gated = (gl * c[:, DFF:]).astype(bf16)
        gd_ref[...] = gated
        x2 = x1_ref[...] + jnp.dot(gated, wd_ref[...], preferred_element_type=f32)
        xh2, r2 = _rms(x2)
        h3 = (xh2 * gp_ref[...]).astype(bf16)
        h3_ref[...] = h3
        pg = _sig(jnp.dot(h3, wpg_ref[...], preferred_element_type=f32))
        pp = _dot(p_ref[...], wpp_ref[...])
        x3 = x2 + pg * pp
        xh3, r3 = _rms(x3)
        diff = xh3 * gf_ref[...] - t_ref[...]
        loss_ref[...] += 0.5 * jnp.sum(jnp.mean(diff * diff, axis=-1, keepdims=True), axis=0, keepdims=True)
        dy = diff * (1.0 / D)
        dx3, dgf_rows = _rms_bwd(dy, xh3, r3, gf_ref[...])
        dgf_ref[...] += _colsum(dgf_rows)
        dpp = dx3 * pg
        dpp_ref[...] = dpp.astype(bf16)
        dz = dx3 * pp * pg * (1.0 - pg)
        dz_ref[...] = dz.astype(bf16)
        dh3 = _dot_nt(dz, wpg_ref[...])
        dx2n, dgp_rows = _rms_bwd(dh3, xh2, r2, gp_ref[...])
        dgp_ref[...] += _colsum(dgp_rows)
        dx2_ref[...] = dx3 + dx2n

    row = lambda n: pl.BlockSpec((tm, n), lambda i: (i, 0))
    halo = pl.BlockSpec((8, 2 * DFF), lambda i: (jnp.maximum(i * (tm // 8) - 1, 0), 0))
    return _pc(
        body, name="ffn_tail", grid=(T // tm,),
        in_specs=[row(2 * DFF), halo, _whole((3, 2 * DFF)), _whole((1, 2 * DFF)), _whole(w_down.shape), row(D), row(PLE),
                  _whole((1, D)), _whole(w_pg.shape), _whole(w_pp.shape), _whole((1, D)), row(D)],
        out_specs=[row(D), row(DFF), row(DFF), row(DFF), row(D), row(D), row(D), _acc((1, 128)), _acc((1, D)), _acc((1, D))],
        out_shape=[jax.ShapeDtypeStruct((T, D), f32), jax.ShapeDtypeStruct((T, DFF), bf16),
                   jax.ShapeDtypeStruct((T, DFF), f32), jax.ShapeDtypeStruct((T, DFF), f32), jax.ShapeDtypeStruct((T, D), bf16),
                   jax.ShapeDtypeStruct((T, D), bf16), jax.ShapeDtypeStruct((T, D), bf16),
                   jax.ShapeDtypeStruct((1, 128), f32), jax.ShapeDtypeStruct((1, D), f32), jax.ShapeDtypeStruct((1, D), f32)],
        compiler_params=_params(),
    )(a, a, conv_w, conv_b, w_down, x1, p, g_ple, w_pg, w_pp, g_fin, tgt)


def _ffn_bwd(dx2, a, g_a, g_b, conv_w, w_down, w_up, x1, g_ffn):
    T = a.shape[0]
    tm = FFN_TM
    nt = T // tm

    def body(dx2_ref, a_ref, ga_ref, gb_ref, cw_ref, wd_ref, wu_ref, x1_ref, g_ref,
             da_ref, dx1_ref, dcw_ref, dcb_ref, dg_ref, carry_ref):
        i = pl.program_id(0)

        @pl.when(i == 0)
        def _():
            carry_ref[...] = jnp.zeros_like(carry_ref)
            dcw_ref[...] = jnp.zeros_like(dcw_ref)
            dcb_ref[...] = jnp.zeros_like(dcb_ref)
            dg_ref[...] = jnp.zeros_like(dg_ref)

        av = a_ref[...]
        cw = cw_ref[...]
        dx2 = dx2_ref[...]
        dgd = _dot_nt(dx2, wd_ref[...])
        dc = jnp.concatenate([dgd * ga_ref[...], dgd * gb_ref[...]], axis=1)
        row = lax.broadcasted_iota(jnp.int32, (tm, 1), 0)
        n1 = carry_ref[0:1, :]
        n2 = carry_ref[1:2, :]
        up1 = jnp.where(row == tm - 1, n1, pltpu.roll(dc, tm - 1, 0))
        up2 = jnp.where(row == tm - 1, n2, jnp.where(row == tm - 2, n1, pltpu.roll(dc, tm - 2, 0)))
        dcb_ref[...] += _colsum(dc)
        dcw_ref[0:1, :] += _colsum(up2 * av)
        dcw_ref[1:2, :] += _colsum(up1 * av)
        dcw_ref[2:3, :] += _colsum(dc * av)
        da = (cw[2:3, :] * dc + cw[1:2, :] * up1 + cw[0:1, :] * up2).astype(bf16)
        carry_ref[...] = dc[0:8, :]
        da_ref[...] = da
        dh2 = lax.dot_general(da, wu_ref[...], (((1,), (1,)), ((), ())), preferred_element_type=f32)
        xh, r = _rms(x1_ref[...])
        dx1n, dg_rows = _rms_bwd(dh2, xh, r, g_ref[...])
        dg_ref[...] += _colsum(dg_rows)
        dx1_ref[...] = dx2 + dx1n

    rev = lambda i: nt - 1 - i
    row = lambda n: pl.BlockSpec((tm, n), lambda i: (rev(i), 0))
    return _pc(
        body, name="ffn_bwd", grid=(nt,),
        in_specs=[row(D), row(2 * DFF), row(DFF), row(DFF), _whole((3, 2 * DFF)), _whole(w_down.shape),
                  _whole(w_up.shape), row(D), _whole((1, D))],
        out_specs=[row(2 * DFF), row(D), _acc((3, 2 * DFF)), _acc((1, 2 * DFF)), _acc((1, D))],
        out_shape=[jax.ShapeDtypeStruct((T, 2 * DFF), bf16), jax.ShapeDtypeStruct((T, D), f32),
                   jax.ShapeDtypeStruct((3, 2 * DFF), f32), jax.ShapeDtypeStruct((1, 2 * DFF), f32),
                   jax.ShapeDtypeStruct((1, D), f32)],
        scratch_shapes=[pltpu.VMEM((8, 2 * DFF), f32)],
        compiler_params=_params(),
    )(dx2, a, g_a, g_b, conv_w, w_down, w_up, x1, g_ffn)


def _mix_bwd(dx1, hg_o, s5_o, gates, w_bhg, w_bs5, w_out):
    T = dx1.shape[0]
    tm = 256

    def body(dx1_ref, hg_ref, s5_ref, gt_ref, wh_ref, ws_ref, wo_ref, dgt_ref, dhg_ref, ds5_ref, dyh_ref, dys_ref):
        dm = _dot_nt(dx1_ref[...], wo_ref[...])
        yh = jnp.dot(hg_ref[...], wh_ref[...], preferred_element_type=f32)
        ys = jnp.dot(s5_ref[...], ws_ref[...], preferred_element_type=f32)
        sh = _sig(gt_ref[:, 0:D])
        ss = _sig(gt_ref[:, D:2 * D])
        dgt_ref[:, 0:D] = dm * yh * sh * (1.0 - sh)
        dgt_ref[:, D:2 * D] = dm * ys * ss * (1.0 - ss)
        dyh = (dm * sh).astype(bf16)
        dys = (dm * ss).astype(bf16)
        dyh_ref[...] = dyh
        dys_ref[...] = dys
        dhg_ref[...] = lax.dot_general(dyh, wh_ref[...], (((1,), (1,)), ((), ())), preferred_element_type=f32)
        ds5_ref[...] = lax.dot_general(dys, ws_ref[...], (((1,), (1,)), ((), ())), preferred_element_type=f32)

    row = lambda n: pl.BlockSpec((tm, n), lambda i: (i, 0))
    return _pc(
        body, name="mix_bwd", grid=(T // tm,),
        in_specs=[row(D), row(HW), row(SW), row(2 * D), _whole(w_bhg.shape), _whole(w_bs5.shape), _whole(w_out.shape)],
        out_specs=[row(2 * D), row(HW), row(SW), row(D), row(D)],
        out_shape=[jax.ShapeDtypeStruct((T, 2 * D), f32), jax.ShapeDtypeStruct((T, HW), f32), jax.ShapeDtypeStruct((T, SW), f32),
                   jax.ShapeDtypeStruct((T, D), bf16), jax.ShapeDtypeStruct((T, D), bf16)],
        compiler_params=_params(),
    )(dx1, hg_o, s5_o, gates, w_bhg, w_bs5, w_out)


def _in_bwd(d_hg, d_u, d_gt, x, dx1, w, g):
    T = x.shape[0]
    tm = 256

    def body(dhg_ref, du_ref, dgt_ref, x_ref, dx1_ref, w_ref, g_ref, dx_ref, dg_ref):
        @pl.when(pl.program_id(0) == 0)
        def _():
            dg_ref[...] = jnp.zeros_like(dg_ref)

        dh = (_dot_nt(dhg_ref[...], w_ref[:, 0:4 * HW]) + _dot_nt(du_ref[...], w_ref[:, 4 * HW:4 * HW + SW])
              + _dot_nt(dgt_ref[...], w_ref[:, 4 * HW + SW:]))
        xh, r = _rms(x_ref[...])
        dxn, dg_rows = _rms_bwd(dh, xh, r, g_ref[...])
        dg_ref[...] += _colsum(dg_rows)
        dx_ref[...] = dx1_ref[...] + dxn

    row = lambda n: pl.BlockSpec((tm, n), lambda i: (i, 0))
    return _pc(
        body, name="in_bwd", grid=(T // tm,),
        in_specs=[row(4 * HW), row(SW), row(2 * D), row(D), row(D), _whole(w.shape), _whole((1, D))],
        out_specs=[row(D), _acc((1, D))],
        out_shape=[jax.ShapeDtypeStruct((T, D), f32), jax.ShapeDtypeStruct((1, D), f32)],
        compiler_params=_params(),
    )(d_hg, d_u, d_gt, x, dx1, w, g)


def _wgrad(name, a, b, nj=None, a_blk=None, a_idx=None, b_blk=None, b_idx=None):
    T = a.shape[0]
    tm = 512
    dense = nj is None
    if dense:
        K, N = a.shape[1], b.shape[1]
        a_blk, a_idx = K, (lambda j: 0)
        b_blk = N
        while K * b_blk * 4 > 6 * 1024 * 1024 and b_blk % 256 == 0:
            b_blk //= 2
        nj, b_idx = N // b_blk, (lambda j: j)

    def body(a_ref, b_ref, o_ref):
        @pl.when(pl.program_id(1) == 0)
        def _():
            o_ref[...] = jnp.zeros_like(o_ref)

        o_ref[0] += _dot_tn(a_ref[...], b_ref[...])

    out = _pc(
        body, name=name, grid=(nj, T // tm),
        in_specs=[pl.BlockSpec((tm, a_blk), lambda j, i: (i, a_idx(j))), pl.BlockSpec((tm, b_blk), lambda j, i: (i, b_idx(j)))],
        out_specs=pl.BlockSpec((1, a_blk, b_blk), lambda j, i: (j, 0, 0)),
        out_shape=jax.ShapeDtypeStruct((nj, a_blk, b_blk), f32),
        compiler_params=_params(2),
    )(a, b)
    if dense:
        return out[0] if nj == 1 else jnp.transpose(out, (1, 0, 2)).reshape(a.shape[1], b.shape[1])
    return out


ANY = pl.BlockSpec(memory_space=pl.ANY)


AG_SEMS = [pltpu.SemaphoreType.DMA((7,)), pltpu.SemaphoreType.DMA((7,)), pltpu.SemaphoreType.DMA]


def _ag_steps(x_ref, out_ref, send_sems, recv_sems, local_sem):
    x, y, c = lax.axis_index("x"), lax.axis_index("y"), lax.axis_index("c")
    me, sibling = (x, y, c), (x, y, 1 - c)
    chips = [(1 - x, y), (x, 1 - y), (1 - x, 1 - y)]

    def slot(px, py, pc):
        return out_ref.at[4 * px + 2 * py + pc]

    def copy(k, block, to, src=None):
        return pltpu.make_async_remote_copy(
            src_ref=slot(*block) if src is None else src, dst_ref=slot(*block),
            send_sem=send_sems.at[k], recv_sem=recv_sems.at[k], device_id=to, device_id_type=MESH)

    def mine():
        return pltpu.make_async_copy(x_ref, slot(*me), local_sem)

    def first():
        return [copy(0, me, sibling, src=x_ref)] + [copy(1 + j, me, (*chip, c), src=x_ref) for j, chip in enumerate(chips)]

    def passed():
        return [copy(4 + j, (*chip, c), sibling) for j, chip in enumerate(chips)]

    def start():
        mine().start()
        for cp in first():
            cp.start()

    def forward():
        for j, (chip, cp) in enumerate(zip(chips, passed())):
            copy(1 + j, (*chip, c), me).wait_recv()
            cp.start()

    def finish():
        copy(0, sibling, me).wait_recv()
        for j, chip in enumerate(chips):
            copy(4 + j, (*chip, 1 - c), me).wait_recv()
        for cp in first() + passed():
            cp.wait_send()
        mine().wait()

    return start, forward, finish


def _all_gather(name, shard):
    R, C = shard.shape

    def body(x_ref, out_ref, send_sems, recv_sems, local_sem):
        for phase in _ag_steps(x_ref, out_ref, send_sems, recv_sems, local_sem):
            phase()

    return _pc(
        body, name=name, in_specs=[ANY], out_specs=ANY,
        out_shape=jax.ShapeDtypeStruct((N_DEV, R, C), shard.dtype), scratch_shapes=list(AG_SEMS),
    )(shard)


def _swap_sibling(gs):
    n = len(gs)

    def body(*refs):
        g_refs, r_refs, (send_sems, recv_sems) = refs[:n], refs[n:2 * n], refs[2 * n:]
        x, y, c = lax.axis_index("x"), lax.axis_index("y"), lax.axis_index("c")
        cps = [pltpu.make_async_remote_copy(src_ref=g.at[k, 1 - c], dst_ref=r.at[k], send_sem=send_sems.at[4 * i + k],
                                            recv_sem=recv_sems.at[4 * i + k], device_id=(x, y, 1 - c), device_id_type=MESH)
               for i, (g, r) in enumerate(zip(g_refs, r_refs)) for k in range(4)]
        for cp in cps:
            cp.start()
        for cp in cps:
            cp.wait()

    return _pc(
        body, name="rs_sibling", in_specs=[ANY] * n, out_specs=[ANY] * n,
        out_shape=[jax.ShapeDtypeStruct((4, *g.shape[2:]), g.dtype) for g in gs],
        scratch_shapes=[pltpu.SemaphoreType.DMA((4 * n,)), pltpu.SemaphoreType.DMA((4 * n,))],
    )(*gs)


def _swap_chips(ps):
    n = len(ps)

    def body(*refs):
        p_refs, r_refs, (send_sems, recv_sems) = refs[:n], refs[n:2 * n], refs[2 * n:]
        x, y, c = lax.axis_index("x"), lax.axis_index("y"), lax.axis_index("c")
        chips = [(1 - x, y), (x, 1 - y), (1 - x, 1 - y)]
        cps = [pltpu.make_async_remote_copy(src_ref=p.at[2 * px + py], dst_ref=r.at[k], send_sem=send_sems.at[3 * i + k],
                                            recv_sem=recv_sems.at[3 * i + k], device_id=(px, py, c), device_id_type=MESH)
               for i, (p, r) in enumerate(zip(p_refs, r_refs)) for k, (px, py) in enumerate(chips)]
        for cp in cps:
            cp.start()
        for cp in cps:
            cp.wait()

    return _pc(
        body, name="rs_chips", in_specs=[ANY] * n, out_specs=[ANY] * n,
        out_shape=[jax.ShapeDtypeStruct((3, *p.shape[1:]), p.dtype) for p in ps],
        scratch_shapes=[pltpu.SemaphoreType.DMA((3 * n,)), pltpu.SemaphoreType.DMA((3 * n,))],
    )(*ps)


def _add_halves(name, g4, got, ids):
    _, _, K, c = g4.shape

    def body(ids_ref, a_ref, b_ref, p16_ref, own_ref):
        s = a_ref[0, 0] + b_ref[0]
        p16_ref[0] = s.astype(bf16)

        @pl.when(pl.program_id(0) == ids_ref[1])
        def _():
            own_ref[...] = s

    return _pc(
        body, name=name,
        grid_spec=pltpu.PrefetchScalarGridSpec(
            num_scalar_prefetch=1, grid=(4,),
            in_specs=[pl.BlockSpec((1, 1, K, c), lambda k, ids: (k, ids[0], 0, 0)),
                      pl.BlockSpec((1, K, c), lambda k, ids: (k, 0, 0))],
            out_specs=[pl.BlockSpec((1, K, c), lambda k, ids: (k, 0, 0)), pl.BlockSpec((K, c), lambda k, ids: (0, 0))]),
        out_shape=[jax.ShapeDtypeStruct((4, K, c), bf16), jax.ShapeDtypeStruct((K, c), f32)],
        compiler_params=_params(),
    )(ids, g4, got)


def _row_tile(K):
    for cand in (256, 176, 128, 64):
        if K % cand == 0:
            return cand
    return K


def _adam_shard(name, own, got3, w, m, v):
    K, c = own.shape
    tr = _row_tile(K)

    def body(own_ref, got_ref, w_ref, m_ref, v_ref, g_ref, d_ref, m2_ref, v2_ref):
        g = own_ref[...] + got_ref[0].astype(f32) + got_ref[1].astype(f32) + got_ref[2].astype(f32)
        g_ref[0] = g
        delta, m2, v2 = _adam_math(g, w_ref[0], m_ref[0], v_ref[0])
        d_ref[0] = delta
        m2_ref[0] = m2
        v2_ref[0] = v2

    blk = pl.BlockSpec((1, tr, c), lambda i: (0, i, 0))
    out = jax.ShapeDtypeStruct((1, K, c), f32)
    return _pc(
        body, name=name, grid=(K // tr,),
        in_specs=[pl.BlockSpec((tr, c), lambda i: (i, 0)), pl.BlockSpec((3, tr, c), lambda i: (0, i, 0)), blk, blk, blk],
        out_specs=[blk, blk, blk, blk], out_shape=[out, out, out, out], compiler_params=_params(),
    )(own, got3, w, m, v)


def _allreduce_small(grads):
    n = len(grads)
    shapes = [g.shape for g in grads]

    def body(*refs):
        g_refs, outs, recv = refs[0:n], refs[n:2 * n], refs[2 * n:5 * n]
        send_sems, recv_sems = refs[5 * n:]
        x, y, c = lax.axis_index("x"), lax.axis_index("y"), lax.axis_index("c")
        peers = [(x, y, 1 - c), (1 - x, y, c), (x, 1 - y, c)]
        for i in range(n):
            outs[i][...] = g_refs[i][...]
        for s, peer in enumerate(peers):
            cps = [pltpu.make_async_remote_copy(src_ref=outs[i], dst_ref=recv[s * n + i], send_sem=send_sems.at[s * n + i],
                                                recv_sem=recv_sems.at[s * n + i], device_id=peer, device_id_type=MESH)
                   for i in range(n)]
            for cp in cps:
                cp.start()
            for cp in cps:
                cp.wait()
            for i in range(n):
                outs[i][...] = outs[i][...] + recv[s * n + i][...]

    return _pc(
        body, name="allreduce_small", grid=(1,), in_specs=[_whole(s) for s in shapes], out_specs=[_acc(s) for s in shapes],
        out_shape=[jax.ShapeDtypeStruct(s, f32) for s in shapes],
        scratch_shapes=[pltpu.VMEM(s, f32) for s in shapes] * 3
        + [pltpu.SemaphoreType.DMA((3 * n,)), pltpu.SemaphoreType.DMA((3 * n,))],
        compiler_params=_params(),
    )(*grads)


def _adam_small(grads, ws, ms, vs):
    n = len(grads)
    shapes = [g.shape for g in grads]

    def body(*refs):
        g_refs, w_refs, m_refs, v_refs = refs[0:n], refs[n:2 * n], refs[2 * n:3 * n], refs[3 * n:4 * n]
        outs = refs[4 * n:8 * n]
        for i in range(n):
            g = g_refs[i][...]
            delta, m2, v2 = _adam_math(g, w_refs[i][...], m_refs[i][...], v_refs[i][...])
            outs[i][...] = g
            outs[n + i][...] = delta
            outs[2 * n + i][...] = m2
            outs[3 * n + i][...] = v2

    return _pc(
        body, name="adam_small", grid=(1,), in_specs=[_whole(s) for s in shapes] * 4, out_specs=[_acc(s) for s in shapes] * 4,
        out_shape=[jax.ShapeDtypeStruct(s, f32) for s in shapes] * 4, compiler_params=_params(),
    )(*grads, *ws, *ms, *vs)


def _adam_math(g, w, m, v):
    m2 = ADAM_B1 * m + (1.0 - ADAM_B1) * g
    v2 = ADAM_B2 * v + (1.0 - ADAM_B2) * (g * g)
    m_hat = m2 / (1.0 - ADAM_B1 ** ADAM_STEP)
    v_hat = v2 / (1.0 - ADAM_B2 ** ADAM_STEP)
    delta = -ADAM_LR * (m_hat / (jnp.sqrt(v_hat) + ADAM_EPS) + ADAM_WD * w)
    return delta, m2, v2


def _pack(arrs, dtype, row_mult):
    rows = []
    for a in arrs:
        flat = a.reshape(-1).astype(dtype)
        pad = (-flat.shape[0]) % LANES
        if pad:
            flat = jnp.concatenate([flat, jnp.zeros((pad,), dtype)])
        rows.append(flat.reshape(-1, LANES))
    out = jnp.concatenate(rows, axis=0)
    pad = (-out.shape[0]) % row_mult
    if pad:
        out = jnp.concatenate([out, jnp.zeros((pad, LANES), dtype)], axis=0)
    return out


def _unpack(buf, shapes):
    lead = buf.shape[:-2]
    outs, r = [], 0
    for shp in shapes:
        n = math.prod(shp)
        nr = -(-n // LANES)
        piece = buf[..., r:r + nr, :].reshape(*lead, nr * LANES)[..., :n]
        outs.append(piece.reshape(*lead, *shp))
        r += nr
    return outs


def _to_slabs(full, axis):
    shp = full.shape
    n = shp[axis] // N_DEV
    return jnp.moveaxis(full.reshape(*shp[:axis], N_DEV, n, *shp[axis + 1:]), axis, 0)


def _from_slabs(slabs, axis):
    t = jnp.moveaxis(slabs, 0, axis)
    shp = t.shape
    return t.reshape(*shp[:axis], shp[axis] * shp[axis + 1], *shp[axis + 2:])


def _s5_discretise(lam_re, lam_im, log_dt, b_re, b_im):
    dt = jnp.exp(log_dt)[:, None]
    mag = jnp.exp(lam_re * dt)
    a_re = mag * jnp.cos(lam_im * dt)
    a_im = mag * jnp.sin(lam_im * dt)
    den = lam_re * lam_re + lam_im * lam_im
    coef_re = ((a_re - 1.0) * lam_re + a_im * lam_im) / den
    coef_im = (a_im * lam_re - (a_re - 1.0) * lam_im) / den
    bbar_re = coef_re[..., None] * b_re - coef_im[..., None] * b_im
    bbar_im = coef_re[..., None] * b_im + coef_im[..., None] * b_re
    return a_re, a_im, bbar_re, bbar_im


def _s5_operands(bbar_re, bbar_im, c_re, c_im):
    eye = jnp.eye(SG // NST, dtype=f32)

    def b_op(bb):
        return jnp.einsum("sgnq,gh->sgqhn", bb.reshape(NST, SG // NST, SN, SP), eye).reshape(NST, 128, STW)

    def c_op(cc):
        return jnp.einsum("sgpn,gh->shngp", cc.reshape(NST, SG // NST, SP, SN), eye).reshape(NST, STW, 128)

    bdb = jnp.concatenate([b_op(bbar_re), b_op(bbar_im)], axis=0)
    bdc = jnp.concatenate([c_op(c_re), c_op(-c_im)], axis=0)
    return bdb, bdc


_BIG = ["w_in", "s5_glu_w", "w_branch_hg", "w_branch_s5", "w_out", "w_up", "w_down", "w_ple_gate", "w_ple_proj", "conv_w"]
_BIG_AXIS = {"w_in": 1, "s5_glu_w": 0, "w_branch_hg": 1, "w_branch_s5": 1, "w_out": 0, "w_up": 1, "w_down": 0,
             "w_ple_gate": 0, "w_ple_proj": 1, "conv_w": 1}
_SMALL = ["norm_mix_g", "hg_lb_logits", "hg_norm_g", "s5_lambda_re", "s5_lambda_im", "s5_log_dt", "s5_b_re", "s5_b_im",
          "s5_c_re", "s5_c_im", "s5_d", "s5_glu_b", "norm_ffn_g", "conv_b", "norm_ple_g", "norm_final_g"]
_ORDER = ["norm_mix_g", "w_in", "hg_lb_logits", "hg_norm_g", "s5_lambda_re", "s5_lambda_im", "s5_log_dt", "s5_b_re",
          "s5_b_im", "s5_c_re", "s5_c_im", "s5_d", "s5_glu_w", "s5_glu_b", "w_branch_hg", "w_branch_s5", "w_out",
          "norm_ffn_g", "w_up", "conv_w", "conv_b", "w_down", "norm_ple_g", "w_ple_gate", "w_ple_proj", "norm_final_g"]


def kernel(x, p, norm_mix_g, w_in, hg_lb_logits, hg_norm_g, s5_lambda_re, s5_lambda_im, s5_log_dt, s5_b_re, s5_b_im, s5_c_re, s5_c_im, s5_d, s5_glu_w, s5_glu_b, w_branch_hg, w_branch_s5, w_out, norm_ffn_g, w_up, conv_w, conv_b, w_down, norm_ple_g, w_ple_gate, w_ple_proj, norm_final_g, loss_target, m_norm_mix_g, m_w_in, m_hg_lb_logits, m_hg_norm_g, m_s5_lambda_re, m_s5_lambda_im, m_s5_log_dt, m_s5_b_re, m_s5_b_im, m_s5_c_re, m_s5_c_im, m_s5_d, m_s5_glu_w, m_s5_glu_b, m_w_branch_hg, m_w_branch_s5, m_w_out, m_norm_ffn_g, m_w_up, m_conv_w, m_conv_b, m_w_down, m_norm_ple_g, m_w_ple_gate, m_w_ple_proj, m_norm_final_g, v_norm_mix_g, v_w_in, v_hg_lb_logits, v_hg_norm_g, v_s5_lambda_re, v_s5_lambda_im, v_s5_log_dt, v_s5_b_re, v_s5_b_im, v_s5_c_re, v_s5_c_im, v_s5_d, v_s5_glu_w, v_s5_glu_b, v_w_branch_hg, v_w_branch_s5, v_w_out, v_norm_ffn_g, v_w_up, v_conv_w, v_conv_b, v_w_down, v_norm_ple_g, v_w_ple_gate, v_w_ple_proj, v_norm_final_g):
    W = dict(norm_mix_g=norm_mix_g, w_in=w_in, hg_lb_logits=hg_lb_logits, hg_norm_g=hg_norm_g, s5_lambda_re=s5_lambda_re, s5_lambda_im=s5_lambda_im, s5_log_dt=s5_log_dt, s5_b_re=s5_b_re, s5_b_im=s5_b_im, s5_c_re=s5_c_re, s5_c_im=s5_c_im, s5_d=s5_d, s5_glu_w=s5_glu_w, s5_glu_b=s5_glu_b, w_branch_hg=w_branch_hg, w_branch_s5=w_branch_s5, w_out=w_out, norm_ffn_g=norm_ffn_g, w_up=w_up, conv_w=conv_w, conv_b=conv_b, w_down=w_down, norm_ple_g=norm_ple_g, w_ple_gate=w_ple_gate, w_ple_proj=w_ple_proj, norm_final_g=norm_final_g)
    M = dict(norm_mix_g=m_norm_mix_g, w_in=m_w_in, hg_lb_logits=m_hg_lb_logits, hg_norm_g=m_hg_norm_g, s5_lambda_re=m_s5_lambda_re, s5_lambda_im=m_s5_lambda_im, s5_log_dt=m_s5_log_dt, s5_b_re=m_s5_b_re, s5_b_im=m_s5_b_im, s5_c_re=m_s5_c_re, s5_c_im=m_s5_c_im, s5_d=m_s5_d, s5_glu_w=m_s5_glu_w, s5_glu_b=m_s5_glu_b, w_branch_hg=m_w_branch_hg, w_branch_s5=m_w_branch_s5, w_out=m_w_out, norm_ffn_g=m_norm_ffn_g, w_up=m_w_up, conv_w=m_conv_w, conv_b=m_conv_b, w_down=m_w_down, norm_ple_g=m_norm_ple_g, w_ple_gate=m_w_ple_gate, w_ple_proj=m_w_ple_proj, norm_final_g=m_norm_final_g)
    V = dict(norm_mix_g=v_norm_mix_g, w_in=v_w_in, hg_lb_logits=v_hg_lb_logits, hg_norm_g=v_hg_norm_g, s5_lambda_re=v_s5_lambda_re, s5_lambda_im=v_s5_lambda_im, s5_log_dt=v_s5_log_dt, s5_b_re=v_s5_b_re, s5_b_im=v_s5_b_im, s5_c_re=v_s5_c_re, s5_c_im=v_s5_c_im, s5_d=v_s5_d, s5_glu_w=v_s5_glu_w, s5_glu_b=v_s5_glu_b, w_branch_hg=v_w_branch_hg, w_branch_s5=v_w_branch_s5, w_out=v_w_out, norm_ffn_g=v_norm_ffn_g, w_up=v_w_up, conv_w=v_conv_w, conv_b=v_conv_b, w_down=v_w_down, norm_ple_g=v_norm_ple_g, w_ple_gate=v_w_ple_gate, w_ple_proj=v_w_ple_proj, norm_final_g=v_norm_final_g)

    shard2 = {n: W[n][0] for n in _BIG}
    conv_bits = lax.bitcast_convert_type(shard2["conv_w"], bf16)
    groups = [["w_in", "s5_glu_w"], ["w_branch_hg", "w_branch_s5", "w_out", "w_ple_gate", "w_ple_proj", "w_down"], ["w_up"]]
    packs = [_pack([shard2[n] for n in grp] + ([conv_bits] if k == 0 else []), bf16, 16) for k, grp in enumerate(groups)]
    full = {}

    def take(k, gathered):
        pieces = _unpack(gathered, [shard2[n].shape for n in groups[k]] + ([conv_bits.shape] if k == 0 else []))
        full.update({n: _from_slabs(pc, _BIG_AXIS[n]) for n, pc in zip(groups[k], pieces)})
        return pieces

    conv_w_full = _from_slabs(lax.bitcast_convert_type(take(0, _all_gather("ag_weights", packs[0]))[-1], f32), 1)

    xt = x[0]
    pt = p[0, 0]
    tgt = loss_target[0]
    T = xt.shape[0]
    lam_re, lam_im, log_dt = s5_lambda_re[0], s5_lambda_im[0], s5_log_dt[0]
    b_re, b_im, c_re, c_im = s5_b_re[0], s5_b_im[0], s5_c_re[0], s5_c_im[0]

    def s5_prep(lam_re, lam_im, log_dt, b_re, b_im, c_re, c_im):
        a_re, a_im, bbar_re, bbar_im = _s5_discretise(lam_re, lam_im, log_dt, b_re, b_im)
        bdb, bdc = _s5_operands(bbar_re, bbar_im, c_re, c_im)
        return a_re, a_im, bdb, bdc

    (a_re, a_im, bdb, bdc), s5_prep_vjp = jax.vjp(s5_prep, lam_re, lam_im, log_dt, b_re, b_im, c_re, c_im)
    a_row = jnp.concatenate([a_re.reshape(1, SL), a_im.reshape(1, SL)], axis=1)
    bdb_b, bdc_b = bdb.astype(bf16), bdc.astype(bf16)

    h1, proj_hg, u_raw, gates, gathered1 = _in_proj(xt, norm_mix_g, full["w_in"], packs[1])
    take(1, gathered1)
    ng4 = jnp.tile(hg_norm_g, (1, NH))
    hg_o, sprev = _hgrn_fwd(proj_hg, hg_lb_logits, ng4)
    x_st, y_s5, g_s5, s5_o, gathered2 = _s5_fwd(u_raw, a_row, bdb_b, bdc_b, s5_d, full["s5_glu_w"], s5_glu_b, packs[2])
    take(2, gathered2)
    x1, merged, h2, a_up = _mix_up(xt, hg_o, s5_o, gates, full["w_branch_hg"], full["w_branch_s5"], full["w_out"],
                                   norm_ffn_g, full["w_up"])
    (dx2, gated, g_a, g_b, h3, dz_ple, dpp, loss_part, d_norm_final, d_norm_ple) = _ffn_tail(
        a_up, conv_w_full, conv_b, full["w_down"], x1, pt, norm_ple_g, full["w_ple_gate"], full["w_ple_proj"],
        norm_final_g.reshape(1, D), tgt)

    da_up, dx1, d_conv_w, d_conv_b, d_norm_ffn = _ffn_bwd(dx2, a_up, g_a, g_b, conv_w_full, full["w_down"],
                                                           full["w_up"], x1, norm_ffn_g)
    d_gates, d_hg_o, d_s5_o, dyh, dys = _mix_bwd(dx1, hg_o, s5_o, gates, full["w_branch_hg"], full["w_branch_s5"],
                                                   full["w_out"])
    d_proj_hg, d_lb, d_hg_norm = _hgrn_bwd(proj_hg, hg_lb_logits, ng4, sprev, d_hg_o)
    d_u, dz_glu, d_a_re, d_a_im, d_s5_d, d_glu_b, d_bdb, d_bdc = _s5_bwd(
        d_s5_o, y_s5, u_raw, x_st, a_row, bdb_b, bdc_b, s5_d, full["s5_glu_w"], s5_glu_b)
    grad_x, d_norm_mix = _in_bwd(d_proj_hg, d_u, d_gates, xt, dx1, full["w_in"], norm_mix_g)

    gw = {}
    gw["w_in"] = jnp.concatenate([_wgrad("wg_in_hg", h1, d_proj_hg), _wgrad("wg_in_u", h1, d_u),
                                  _wgrad("wg_in_gates", h1, d_gates)], axis=1)
    gw["s5_glu_w"] = _wgrad("wg_glu", g_s5, dz_glu)
    gw["w_branch_hg"] = _wgrad("wg_bhg", hg_o, dyh)
    gw["w_branch_s5"] = _wgrad("wg_bs5", s5_o, dys)
    gw["w_out"] = _wgrad("wg_out", merged, dx1)
    gw["w_up"] = _wgrad("wg_up", h2, da_up)
    gw["w_down"] = _wgrad("wg_down", gated, dx2)
    gw["w_ple_gate"] = _wgrad("wg_pg", h3, dz_ple)
    gw["w_ple_proj"] = _wgrad("wg_pp", pt, dpp)
    gw["conv_w"] = d_conv_w
    (d_lam_re, d_lam_im, d_log_dt, d_b_re, d_b_im, d_c_re, d_c_im) = s5_prep_vjp(
        (d_a_re.reshape(SG, SN), d_a_im.reshape(SG, SN), d_bdb, d_bdc))
    sm = jax.nn.softmax(hg_lb_logits, axis=0)
    d_l0 = d_lb[0] * sm[0] * sm[1]
    d_logits = jnp.stack([d_l0, -d_l0], axis=0)

    gs = {"norm_mix_g": d_norm_mix, "hg_lb_logits": d_logits, "hg_norm_g": d_hg_norm, "s5_lambda_re": d_lam_re,
          "s5_lambda_im": d_lam_im, "s5_log_dt": d_log_dt, "s5_b_re": d_b_re, "s5_b_im": d_b_im, "s5_c_re": d_c_re,
          "s5_c_im": d_c_im, "s5_d": d_s5_d, "s5_glu_b": d_glu_b, "norm_ffn_g": d_norm_ffn, "conv_b": d_conv_b,
          "norm_ple_g": d_norm_ple, "norm_final_g": d_norm_final}

    ids = jnp.stack([lax.axis_index("c"), 2 * lax.axis_index("x") + lax.axis_index("y")]).astype(jnp.int32)
    g4 = [_to_slabs(gw[n], _BIG_AXIS[n]).reshape(4, 2, *shard2[n].shape) for n in _BIG]
    got = _swap_sibling(g4)
    sums = [_add_halves("rs_add_" + n, g, r, ids) for n, g, r in zip(_BIG, g4, got)]
    got3 = _swap_chips([p16 for p16, _ in sums])
    big_out = [_adam_shard("adam_" + n, own, r3, W[n], M[n], V[n]) for n, (_, own), r3 in zip(_BIG, sums, got3)]

    two_d = lambda a: a.reshape(1, -1) if a.ndim == 1 else a
    dense = lambda a: a.reshape(SG, -1) if a.ndim == 4 else two_d(a)
    g_sum = _allreduce_small([dense(gs[n].reshape(W[n].shape)) for n in _SMALL])
    small_out = _adam_small([g.reshape(two_d(W[n]).shape) for g, n in zip(g_sum, _SMALL)], [two_d(W[n]) for n in _SMALL],
                            [two_d(M[n]) for n in _SMALL], [two_d(V[n]) for n in _SMALL])

    res = {}
    for k in range(4):
        d = {n: big_out[i][k] for i, n in enumerate(_BIG)}
        d.update({n: small_out[k * len(_SMALL) + i].reshape(W[n].shape) for i, n in enumerate(_SMALL)})
        res[k] = d
    loss = lax.psum(loss_part[0, 0], ("x", "y", "c"))
    return (loss, grad_x[None], *[res[0][n] for n in _ORDER], *[res[1][n] for n in _ORDER],
            *[res[2][n] for n in _ORDER], *[res[3][n] for n in _ORDER])
```

```python
import functools
import math

import jax
import jax.numpy as jnp
from jax import lax
from jax.experimental import pallas as pl
from jax.experimental.pallas import tpu as pltpu

f32 = jnp.float32
bf16 = jnp.bfloat16
MESH = pl.DeviceIdType.MESH

N_DEV = 8
D = 1024
HW = 512
HD = 128
NH = 4
CH = 64
SW = 512
SG = 32
SP = 16
SN = 64
SL = SG * SN
NST = 4
STW = SL // NST
DFF = 2816
PLE = 256
EPS = 1e-6
LANES = 1024
VMEM_LIMIT = 56 * 1024 * 1024

ADAM_LR, ADAM_B1, ADAM_B2, ADAM_EPS, ADAM_WD, ADAM_STEP = 0.001, 0.9, 0.999, 1e-08, 0.01, 10


def _pc(body, **kw):
    return pl.pallas_call(body, **kw)


def _params(n_axes=1, **kw):
    return pltpu.CompilerParams(dimension_semantics=("arbitrary",) * n_axes, vmem_limit_bytes=VMEM_LIMIT, **kw)


def _whole(shape):
    nd = len(shape)
    return pl.BlockSpec(shape, lambda *_: (0,) * nd, pipeline_mode=pl.Buffered(1))


def _acc(shape):
    nd = len(shape)
    return pl.BlockSpec(shape, lambda *_: (0,) * nd)


def _dot(a, b):
    return jnp.dot(a.astype(bf16), b.astype(bf16), preferred_element_type=f32)


def _dot_nt(a, b):
    return lax.dot_general(a.astype(bf16), b.astype(bf16), (((1,), (1,)), ((), ())), preferred_element_type=f32)


def _dot_tn(a, b):
    return lax.dot_general(a.astype(bf16), b.astype(bf16), (((0,), (0,)), ((), ())), preferred_element_type=f32)


def _sig(x):
    return jax.nn.sigmoid(x)


def _dsilu(z, s):
    return s * (1.0 + z * (1.0 - s))


_GC = math.sqrt(2.0 / math.pi)


def _gelu_and_grad(y):
    t = jnp.tanh(_GC * (y + 0.044715 * y * y * y))
    g = 0.5 * y * (1.0 + t)
    dg = 0.5 * (1.0 + t) + 0.5 * y * (1.0 - t * t) * _GC * (1.0 + 3.0 * 0.044715 * y * y)
    return g, dg


def _rms(x):
    r = lax.rsqrt(jnp.mean(x * x, axis=-1, keepdims=True) + EPS)
    return x * r, r


def _rms_bwd(dy, xh, r, g):
    dxh = dy * g
    dx = r * (dxh - xh * jnp.mean(dxh * xh, axis=-1, keepdims=True))
    return dx, dy * xh


def _colsum(x):
    return jnp.sum(x, axis=0, keepdims=True)


def _in_proj(x, g, w, ag_shard):
    T = x.shape[0]
    tm = 256
    nt = T // tm

    def body(x_ref, g_ref, w_ref, ag_ref, h_ref, hg_ref, u_ref, gt_ref, ago_ref, send_sems, recv_sems, local_sem):
        i = pl.program_id(0)
        start, forward, finish = _ag_steps(ag_ref, ago_ref, send_sems, recv_sems, local_sem)
        pl.when(i == 0)(start)
        xh, _ = _rms(x_ref[...])
        h = (xh * g_ref[...]).astype(bf16)
        h_ref[...] = h
        hg_ref[...] = jnp.dot(h, w_ref[:, 0:4 * HW], preferred_element_type=f32)
        u_ref[...] = jnp.dot(h, w_ref[:, 4 * HW:4 * HW + SW], preferred_element_type=f32)
        gt_ref[...] = jnp.dot(h, w_ref[:, 4 * HW + SW:], preferred_element_type=f32)
        pl.when(i == nt // 2)(forward)
        pl.when(i == nt - 1)(finish)

    row = lambda n: pl.BlockSpec((tm, n), lambda i: (i, 0))
    return _pc(
        body, name="in_proj", grid=(nt,),
        in_specs=[row(D), _whole((1, D)), _whole(w.shape), ANY],
        out_specs=[row(D), row(4 * HW), row(SW), row(2 * D), ANY],
        out_shape=[jax.ShapeDtypeStruct((T, D), bf16), jax.ShapeDtypeStruct((T, 4 * HW), f32),
                   jax.ShapeDtypeStruct((T, SW), f32), jax.ShapeDtypeStruct((T, 2 * D), f32),
                   jax.ShapeDtypeStruct((N_DEV, *ag_shard.shape), ag_shard.dtype)],
        scratch_shapes=list(AG_SEMS),
        compiler_params=_params(),
    )(x, g, w, ag_shard)


HG_NC = 2


def _tri_matmul(tri, x):
    hi = x.astype(bf16)
    r1 = x - hi.astype(f32)
    mid = r1.astype(bf16)
    lo = (r1 - mid.astype(f32)).astype(bf16)
    n = x.shape[1]
    out = jnp.dot(tri.astype(bf16), jnp.concatenate([hi, mid, lo], axis=1), preferred_element_type=f32)
    return out[:, 0:n] + out[:, n:2 * n] + out[:, 2 * n:3 * n]


def _hgrn_gates(lg, qr, fr):
    mx = jnp.max(lg, axis=0, keepdims=True)
    e = jnp.exp(lg - mx)
    lb = e[0:1, :] / (e[0:1, :] + e[1:2, :])
    sig = _sig(fr)
    f = lb + (1.0 - lb) * sig
    k = 1.0 - f
    r_i = lax.broadcasted_iota(jnp.int32, (CH, CH), 0)
    c_i = lax.broadcasted_iota(jnp.int32, (CH, CH), 1)
    tril = (r_i >= c_i)
    b = _tri_matmul(tril, jnp.log(f))
    bref = b[CH // 2:CH // 2 + 1, :]
    blast = b[CH - 1:CH, :]
    sq = _sig(qr)
    q = qr * sq
    e1 = jnp.exp(b - bref)
    e2 = jnp.exp(bref - b)
    e3 = jnp.exp(blast - b)
    e4 = jnp.exp(b)
    return dict(lb=lb, qr=qr, sq=sq, sig=sig, f=f, k=k, tril=tril, triu=(c_i >= r_i), e1=e1, e2=e2, e3=e3, e4=e4,
                qs=q * e1, ks=k * e2, kl=k * e3, qb=q * e4, dec=jnp.exp(blast))


def _hgrn_fwd(proj_hg, logits, ng4):
    T = proj_hg.shape[0]
    nch = T // CH
    tm = HG_NC * CH

    def body(q_ref, f_ref, i_ref, og_ref, lg_ref, ng_ref, out_ref, sprev_ref, st_ref):
        @pl.when(pl.program_id(0) == 0)
        def _():
            st_ref[...] = jnp.zeros_like(st_ref)

        for ci in range(HG_NC):
            rows = slice(ci * CH, (ci + 1) * CH)
            c = _hgrn_gates(lg_ref[...], q_ref[rows, :], f_ref[rows, :])
            v = i_ref[rows, :]
            og = og_ref[rows, :]
            ohs = []
            for h in range(NH):
                sl = slice(h * HD, (h + 1) * HD)
                p = jnp.where(c["tril"], _dot_nt(c["qs"][:, sl], c["ks"][:, sl]), 0.0)
                st = st_ref[h]
                sprev_ref[ci, h] = st
                o = _dot(p, v[:, sl]) + _dot_nt(c["qb"][:, sl], st)
                st_ref[h] = c["dec"][:, sl] * st + _dot_tn(v[:, sl], c["kl"][:, sl])
                ohs.append(_rms(o)[0])
            out_ref[rows, :] = (jnp.concatenate(ohs, axis=1) * ng_ref[...] * (og * _sig(og))).astype(bf16)

    col = lambda j: pl.BlockSpec((tm, HW), lambda n, j=j: (n, j))
    return _pc(
        body, name="hgrn_fwd", grid=(nch // HG_NC,),
        in_specs=[col(0), col(1), col(2), col(3), _whole((2, HW)), _whole((1, HW))],
        out_specs=[pl.BlockSpec((tm, HW), lambda n: (n, 0)),
                   pl.BlockSpec((HG_NC, NH, HD, HD), lambda n: (n, 0, 0, 0))],
        out_shape=[jax.ShapeDtypeStruct((T, HW), bf16), jax.ShapeDtypeStruct((nch, NH, HD, HD), f32)],
        scratch_shapes=[pltpu.VMEM((NH, HD, HD), f32)],
        compiler_params=_params(),
    )(proj_hg, proj_hg, proj_hg, proj_hg, logits, ng4)


def _hgrn_bwd(proj_hg, logits, ng4, sprev, d_out, exch=None):
    T = proj_hg.shape[0]
    nch = T // CH
    tm = HG_NC * CH
    nst = nch // HG_NC

    def body(q_ref, f_ref, i_ref, og_ref, lg_ref, ng_ref, sp_ref, do_ref, dp_ref, dlb_ref, dng_ref, gt_ref):
        @pl.when(pl.program_id(0) == 0)
        def _():
            gt_ref[...] = jnp.zeros_like(gt_ref)
            dlb_ref[...] = jnp.zeros_like(dlb_ref)
            dng_ref[...] = jnp.zeros_like(dng_ref)

        row = lax.broadcasted_iota(jnp.int32, (CH, HW), 0)
        for ci in reversed(range(HG_NC)):
            rows = slice(ci * CH, (ci + 1) * CH)
            c = _hgrn_gates(lg_ref[...], q_ref[rows, :], f_ref[rows, :])
            tril = c["tril"]
            v = i_ref[rows, :]
            og = og_ref[rows, :]
            sog = _sig(og)
            d_gated = do_ref[rows, :]
            d_on_all = d_gated * (og * sog)
            parts = {n: [] for n in ("dqs", "dks", "dkl", "dqb", "dv", "ohg", "ddec")}
            for h in range(NH):
                sl = slice(h * HD, (h + 1) * HD)
                ng_h = ng_ref[:, sl]
                qs, ks, kl, qb, vh = c["qs"][:, sl], c["ks"][:, sl], c["kl"][:, sl], c["qb"][:, sl], v[:, sl]
                st = sp_ref[ci, h]
                gt = gt_ref[h]
                p = jnp.where(tril, _dot_nt(qs, ks), 0.0)
                o = _dot(p, vh) + _dot_nt(qb, st)
                oh, r = _rms(o)
                d_o, dng_rows = _rms_bwd(d_on_all[:, sl], oh, r, ng_h)
                dng_ref[...] += _colsum(dng_rows)
                dp = jnp.where(tril, _dot_nt(d_o, vh), 0.0)
                parts["dqb"].append(_dot(d_o, st))
                parts["dv"].append(_dot_tn(p, d_o) + _dot_nt(kl, gt))
                parts["dqs"].append(_dot(dp, ks))
                parts["dks"].append(_dot_tn(dp, qs))
                parts["dkl"].append(_dot(vh, gt))
                parts["ddec"].append(_colsum(gt * st))
                parts["ohg"].append(oh * ng_h)
                gt_ref[h] = _dot_tn(d_o, qb) + c["dec"][:, sl] * gt
            cat = {n: jnp.concatenate(vs, axis=1) for n, vs in parts.items()}
            dqs, dks, dkl, dqb = cat["dqs"], cat["dks"], cat["dkl"], cat["dqb"]
            dq = dqs * c["e1"] + dqb * c["e4"]
            dk = dks * c["e2"] + dkl * c["e3"]
            t_qs = dqs * c["qs"]
            t_ks = dks * c["ks"]
            t_kl = dkl * c["kl"]
            db = t_qs - t_ks - t_kl + dqb * c["qb"]
            dbref = _colsum(t_ks - t_qs)
            dblast = _colsum(t_kl) + cat["ddec"] * c["dec"]
            db = db + jnp.where(row == CH // 2, dbref, 0.0) + jnp.where(row == CH - 1, dblast, 0.0)
            df = _tri_matmul(c["triu"], db) / c["f"] - dk
            sig = c["sig"]
            dlb_ref[...] += _colsum(df * (1.0 - sig))
            dp_ref[rows, 0:HW] = dq * _dsilu(c["qr"], c["sq"])
            dp_ref[rows, HW:2 * HW] = df * (1.0 - c["lb"]) * sig * (1.0 - sig)
            dp_ref[rows, 2 * HW:3 * HW] = cat["dv"]
            dp_ref[rows, 3 * HW:4 * HW] = d_gated * cat["ohg"] * _dsilu(og, sog)

    rev = lambda n: nst - 1 - n
    col = lambda j: pl.BlockSpec((tm, HW), lambda n, j=j: (rev(n), j))
    return _pc_behind(
        body, exch, nst, name="hgrn_bwd", grid=(nst,),
        in_specs=[col(0), col(1), col(2), col(3), _whole((2, HW)), _whole((1, HW)),
                  pl.BlockSpec((HG_NC, NH, HD, HD), lambda n: (rev(n), 0, 0, 0)),
                  pl.BlockSpec((tm, HW), lambda n: (rev(n), 0))],
        out_specs=[pl.BlockSpec((tm, 4 * HW), lambda n: (rev(n), 0)), _acc((1, HW)), _acc((1, HD))],
        out_shape=[jax.ShapeDtypeStruct((T, 4 * HW), f32), jax.ShapeDtypeStruct((1, HW), f32),
                   jax.ShapeDtypeStruct((1, HD), f32)],
        scratch_shapes=[pltpu.VMEM((NH, HD, HD), f32)],
        compiler_params=_params(), args=(proj_hg, proj_hg, proj_hg, proj_hg, logits, ng4, sprev, d_out))


S5_TM = 256
S5_SEG = 8
S5_STEPS = S5_TM // S5_SEG
NLT = SL // 128


def _s5_tables(a_ref, pw_ref, pseg_ref, descending):
    re, im = slice(0, SL), slice(SL, 2 * SL)

    def cmul(ar, ai, br, bi):
        return ar * br - ai * bi, ar * bi + ai * br

    pw_ref[0:1, :] = a_ref[...]
    m = 1
    while m < S5_STEPS:
        pr, pi = cmul(pw_ref[0:m, re], pw_ref[0:m, im], pw_ref[m - 1:m, re], pw_ref[m - 1:m, im])
        pw_ref[m:2 * m, re] = pr
        pw_ref[m:2 * m, im] = pi
        m *= 2
    base = S5_STEPS - 1
    if descending:
        pseg_ref[7:8, :] = pw_ref[base:base + 1, :]
        m = 1
        while m < 8:
            pr, pi = cmul(pseg_ref[8 - m:8, re], pseg_ref[8 - m:8, im], pseg_ref[8 - m:9 - m, re], pseg_ref[8 - m:9 - m, im])
            pseg_ref[8 - 2 * m:8 - m, re] = pr
            pseg_ref[8 - 2 * m:8 - m, im] = pi
            m *= 2
    else:
        pseg_ref[0:1, :] = pw_ref[base:base + 1, :]
        m = 1
        while m < 8:
            pr, pi = cmul(pseg_ref[0:m, re], pseg_ref[0:m, im], pseg_ref[m - 1:m, re], pseg_ref[m - 1:m, im])
            pseg_ref[m:2 * m, re] = pr
            pseg_ref[m:2 * m, im] = pi
            m *= 2


def _seg_rows(j):
    return pl.ds(j * S5_SEG, S5_SEG)


def _seg_perm(transpose=False):
    r_i = lax.broadcasted_iota(jnp.int32, (S5_TM, S5_TM), 0)
    c_i = lax.broadcasted_iota(jnp.int32, (S5_TM, S5_TM), 1)
    if transpose:
        r_i, c_i = c_i, r_i
    return c_i == S5_STEPS * (r_i % S5_SEG) + r_i // S5_SEG


def _scan_fwd(x3_ref, pw_ref, pseg_ref, carry_ref):
    row8 = lax.broadcasted_iota(jnp.int32, (S5_SEG, 128), 0)
    for lt in range(NLT):
        kr, ki = lt, NLT + lt
        lr, li = slice(lt * 128, (lt + 1) * 128), slice(SL + lt * 128, SL + (lt + 1) * 128)
        ar, ai = pw_ref[0:1, lr], pw_ref[0:1, li]
        sr = jnp.zeros((S5_SEG, 128), f32)
        si = jnp.zeros((S5_SEG, 128), f32)
        for j in range(S5_STEPS):
            sr, si = ar * sr - ai * si + x3_ref[kr, _seg_rows(j), :], ar * si + ai * sr + x3_ref[ki, _seg_rows(j), :]
            x3_ref[kr, _seg_rows(j), :] = sr
            x3_ref[ki, _seg_rows(j), :] = si
        for d in (1, 2, 4):
            pr, pi = pseg_ref[d - 1:d, lr], pseg_ref[d - 1:d, li]
            tr, ti = pltpu.roll(sr, d, 0), pltpu.roll(si, d, 0)
            m = row8 >= d
            sr, si = sr + jnp.where(m, pr * tr - pi * ti, 0.0), si + jnp.where(m, pr * ti + pi * tr, 0.0)
        c0r, c0i = carry_ref[7:8, lr], carry_ref[7:8, li]
        qr, qi = pseg_ref[:, lr], pseg_ref[:, li]
        sr, si = sr + qr * c0r - qi * c0i, si + qr * c0i + qi * c0r
        carry_ref[:, lr] = sr
        carry_ref[:, li] = si
        cr = jnp.where(row8 == 0, c0r, pltpu.roll(sr, 1, 0))
        ci = jnp.where(row8 == 0, c0i, pltpu.roll(si, 1, 0))
        for j in range(S5_STEPS):
            pr, pi = pw_ref[j:j + 1, lr], pw_ref[j:j + 1, li]
            x3_ref[kr, _seg_rows(j), :] = x3_ref[kr, _seg_rows(j), :] + pr * cr - pi * ci
            x3_ref[ki, _seg_rows(j), :] = x3_ref[ki, _seg_rows(j), :] + pr * ci + pi * cr


def _scan_bwd(g3_ref, x3_ref, xh_ref, first, pw_ref, pseg_ref, carry_ref, dar_ref, dai_ref):
    row8 = lax.broadcasted_iota(jnp.int32, (S5_SEG, 128), 0)
    for lt in range(NLT):
        kr, ki = lt, NLT + lt
        lr, li = slice(lt * 128, (lt + 1) * 128), slice(SL + lt * 128, SL + (lt + 1) * 128)
        ar, ai = pw_ref[0:1, lr], pw_ref[0:1, li]
        sr = jnp.zeros((S5_SEG, 128), f32)
        si = jnp.zeros((S5_SEG, 128), f32)
        for j in reversed(range(S5_STEPS)):
            sr, si = ar * sr + ai * si + g3_ref[kr, _seg_rows(j), :], ar * si - ai * sr + g3_ref[ki, _seg_rows(j), :]
            g3_ref[kr, _seg_rows(j), :] = sr
            g3_ref[ki, _seg_rows(j), :] = si
        for d in (1, 2, 4):
            pr, pi = pseg_ref[8 - d:9 - d, lr], pseg_ref[8 - d:9 - d, li]
            tr, ti = pltpu.roll(sr, 8 - d, 0), pltpu.roll(si, 8 - d, 0)
            m = row8 < 8 - d
            sr, si = sr + jnp.where(m, pr * tr + pi * ti, 0.0), si + jnp.where(m, pr * ti - pi * tr, 0.0)
        c0r, c0i = carry_ref[0:1, lr], carry_ref[0:1, li]
        qr, qi = pseg_ref[:, lr], pseg_ref[:, li]
        sr, si = sr + qr * c0r + qi * c0i, si + qr * c0i - qi * c0r
        carry_ref[:, lr] = sr
        carry_ref[:, li] = si
        cr = jnp.where(row8 == 7, c0r, pltpu.roll(sr, 7, 0))
        ci = jnp.where(row8 == 7, c0i, pltpu.roll(si, 7, 0))
        hr = jnp.where(first, 0.0, xh_ref[kr, 7:8, :])
        hi = jnp.where(first, 0.0, xh_ref[ki, 7:8, :])
        acc_r = jnp.zeros((S5_SEG, 128), f32)
        acc_i = jnp.zeros((S5_SEG, 128), f32)
        for j in range(S5_STEPS):
            pr, pi = pw_ref[S5_STEPS - 1 - j:S5_STEPS - j, lr], pw_ref[S5_STEPS - 1 - j:S5_STEPS - j, li]
            lam_r = g3_ref[kr, _seg_rows(j), :] + pr * cr + pi * ci
            lam_i = g3_ref[ki, _seg_rows(j), :] + pr * ci - pi * cr
            g3_ref[kr, _seg_rows(j), :] = lam_r
            g3_ref[ki, _seg_rows(j), :] = lam_i
            if j == 0:
                xpr = jnp.where(row8 == 0, hr, pltpu.roll(x3_ref[kr, _seg_rows(S5_STEPS - 1), :], 1, 0))
                xpi = jnp.where(row8 == 0, hi, pltpu.roll(x3_ref[ki, _seg_rows(S5_STEPS - 1), :], 1, 0))
            else:
                xpr = x3_ref[kr, _seg_rows(j - 1), :]
                xpi = x3_ref[ki, _seg_rows(j - 1), :]
            acc_r = acc_r + lam_r * xpr + lam_i * xpi
            acc_i = acc_i + lam_i * xpr - lam_r * xpi
        dar_ref[:, lr] += _colsum(acc_r)
        dai_ref[:, lr] += _colsum(acc_i)


def _strip(x3_ref, part, s):
    k0 = part * NLT + s * (STW // 128)
    return jnp.concatenate([x3_ref[k0 + q] for q in range(STW // 128)], axis=1)


def _s5_fwd(u, a_row, bdb, bdc, dskip, glu_w, glu_b, ag_shard):
    T = u.shape[0]
    tm = S5_TM
    nt = T // tm

    def body(u_ref, a_ref, bdb_ref, bdc_ref, ds_ref, gw_ref, gb_ref, ag_ref, x_ref, y_ref, g_ref, o_ref, ago_ref,
             pw_ref, pseg_ref, carry_ref, send_sems, recv_sems, local_sem):
        i = pl.program_id(0)
        start, forward, finish = _ag_steps(ag_ref, ago_ref, send_sems, recv_sems, local_sem)
        pl.when(i == 0)(start)

        @pl.when(i == 0)
        def _():
            carry_ref[...] = jnp.zeros_like(carry_ref)
            _s5_tables(a_ref, pw_ref, pseg_ref, descending=False)

        uv = u_ref[...]
        ub = jnp.dot(_seg_perm().astype(bf16), uv.astype(bf16), preferred_element_type=f32).astype(bf16)
        for part in range(2):
            for s in range(NST):
                bu = jnp.dot(ub[:, s * 128:(s + 1) * 128], bdb_ref[part * NST + s], preferred_element_type=f32)
                for q in range(STW // 128):
                    x_ref[part * NLT + s * (STW // 128) + q] = bu[:, q * 128:(q + 1) * 128]
        _scan_fwd(x_ref, pw_ref, pseg_ref, carry_ref)
        ys = []
        for s in range(NST):
            acc = None
            for part in range(2):
                t = jnp.dot(_strip(x_ref, part, s).astype(bf16), bdc_ref[part * NST + s], preferred_element_type=f32)
                acc = t if acc is None else acc + t
            ys.append(acc)
        y = _tri_matmul(_seg_perm(transpose=True), jnp.concatenate(ys, axis=1)) + ds_ref[...] * uv
        y_ref[...] = y
        g, _ = _gelu_and_grad(y)
        gb = g.astype(bf16)
        g_ref[...] = gb
        z = jnp.dot(gb, gw_ref[...], preferred_element_type=f32) + gb_ref[...]
        o_ref[...] = (g * _sig(z)).astype(bf16)
        pl.when(i == nt // 2)(forward)
        pl.when(i == nt - 1)(finish)

    row = lambda n: pl.BlockSpec((tm, n), lambda i: (i, 0))
    return _pc(
        body, name="s5_fwd", grid=(nt,),
        in_specs=[row(SW), _whole((1, 2 * SL)), _whole(bdb.shape), _whole(bdc.shape), _whole((1, SW)),
                  _whole((SW, SW)), _whole((1, SW)), ANY],
        out_specs=[pl.BlockSpec((2 * NLT, tm, 128), lambda i: (0, i, 0)), row(SW), row(SW), row(SW), ANY],
        out_shape=[jax.ShapeDtypeStruct((2 * NLT, T, 128), f32), jax.ShapeDtypeStruct((T, SW), f32),
                   jax.ShapeDtypeStruct((T, SW), bf16), jax.ShapeDtypeStruct((T, SW), bf16),
                   jax.ShapeDtypeStruct((N_DEV, *ag_shard.shape), ag_shard.dtype)],
        scratch_shapes=[pltpu.VMEM((S5_STEPS, 2 * SL), f32), pltpu.VMEM((8, 2 * SL), f32), pltpu.VMEM((8, 2 * SL), f32)]
        + list(AG_SEMS),
        compiler_params=_params(),
    )(u, a_row, bdb, bdc, dskip, glu_w, glu_b, ag_shard)


def _s5_bwd(d_out, y, u, x, a_row, bdb, bdc, dskip, glu_w, glu_b, exch=None):
    T = u.shape[0]
    tm = S5_TM
    nt = T // tm

    def body(do_ref, y_ref, u_ref, x_ref, xh_ref, a_ref, bdb_ref, bdc_ref, ds_ref, gw_ref, gb_ref,
             du_ref, dz_ref, dar_ref, dai_ref, dd_ref, dgb_ref, dbdb_ref, dbdc_ref, gs_ref, pw_ref, pseg_ref, carry_ref):
        i = pl.program_id(0)

        @pl.when(i == 0)
        def _():
            carry_ref[...] = jnp.zeros_like(carry_ref)
            _s5_tables(a_ref, pw_ref, pseg_ref, descending=True)
            dar_ref[...] = jnp.zeros_like(dar_ref)
            dai_ref[...] = jnp.zeros_like(dai_ref)
            dd_ref[...] = jnp.zeros_like(dd_ref)
            dgb_ref[...] = jnp.zeros_like(dgb_ref)
            dbdb_ref[...] = jnp.zeros_like(dbdb_ref)
            dbdc_ref[...] = jnp.zeros_like(dbdc_ref)

        yv = y_ref[...]
        uv = u_ref[...]
        g, gp = _gelu_and_grad(yv)
        z = jnp.dot(g.astype(bf16), gw_ref[...], preferred_element_type=f32) + gb_ref[...]
        sg = _sig(z)
        do = do_ref[...].astype(f32)
        dz = do * g * sg * (1.0 - sg)
        dz_ref[...] = dz.astype(bf16)
        dgb_ref[...] += _colsum(dz)
        dy = (do * sg + _dot_nt(dz, gw_ref[...])) * gp
        perm = _seg_perm().astype(bf16)
        dyb = jnp.dot(perm, dy.astype(bf16), preferred_element_type=f32).astype(bf16)
        dd_ref[...] += _colsum(dy * uv)
        for part in range(2):
            for s in range(NST):
                gx = lax.dot_general(dyb[:, s * 128:(s + 1) * 128], bdc_ref[part * NST + s], (((1,), (1,)), ((), ())),
                                     preferred_element_type=f32)
                for q in range(STW // 128):
                    gs_ref[part * NLT + s * (STW // 128) + q] = gx[:, q * 128:(q + 1) * 128]
        _scan_bwd(gs_ref, x_ref, xh_ref, i == nt - 1, pw_ref, pseg_ref, carry_ref, dar_ref, dai_ref)
        ub = jnp.dot(perm, uv.astype(bf16), preferred_element_type=f32).astype(bf16)
        dus = []
        for s in range(NST):
            acc = None
            for part in range(2):
                lv = _strip(gs_ref, part, s).astype(bf16)
                t = lax.dot_general(lv, bdb_ref[part * NST + s], (((1,), (1,)), ((), ())), preferred_element_type=f32)
                acc = t if acc is None else acc + t
                dbdb_ref[part * NST + s] += _dot_tn(ub[:, s * 128:(s + 1) * 128], lv)
                dbdc_ref[part * NST + s] += _dot_tn(_strip(x_ref, part, s), dyb[:, s * 128:(s + 1) * 128])
            dus.append(acc)
        du_ref[...] = _tri_matmul(_seg_perm(transpose=True), jnp.concatenate(dus, axis=1)) + dy * ds_ref[...]

    rev = lambda i: nt - 1 - i
    row = lambda n: pl.BlockSpec((tm, n), lambda i: (rev(i), 0))
    xblk = pl.BlockSpec((2 * NLT, tm, 128), lambda i: (0, rev(i), 0))
    halo = pl.BlockSpec((2 * NLT, 8, 128), lambda i: (0, jnp.maximum(rev(i) * (tm // 8) - 1, 0), 0))
    return _pc_behind(
        body, exch, nt, name="s5_bwd", grid=(nt,),
        in_specs=[row(SW), row(SW), row(SW), xblk, halo, _whole((1, 2 * SL)), _whole(bdb.shape), _whole(bdc.shape),
                  _whole((1, SW)), _whole((SW, SW)), _whole((1, SW))],
        out_specs=[row(SW), row(SW), _acc((1, SL)), _acc((1, SL)), _acc((1, SW)), _acc((1, SW)),
                   _acc(bdb.shape), _acc(bdc.shape)],
        out_shape=[jax.ShapeDtypeStruct((T, SW), f32), jax.ShapeDtypeStruct((T, SW), bf16),
                   jax.ShapeDtypeStruct((1, SL), f32), jax.ShapeDtypeStruct((1, SL), f32),
                   jax.ShapeDtypeStruct((1, SW), f32), jax.ShapeDtypeStruct((1, SW), f32),
                   jax.ShapeDtypeStruct(bdb.shape, f32), jax.ShapeDtypeStruct(bdc.shape, f32)],
        scratch_shapes=[pltpu.VMEM((2 * NLT, tm, 128), f32), pltpu.VMEM((S5_STEPS, 2 * SL), f32),
                        pltpu.VMEM((8, 2 * SL), f32), pltpu.VMEM((8, 2 * SL), f32)],
        compiler_params=_params(), args=(d_out, y, u, x, x, a_row, bdb, bdc, dskip, glu_w, glu_b))


def _mix_up(x, hg_o, s5_o, gates, w_bhg, w_bs5, w_out, g_ffn, w_up):
    T = x.shape[0]
    tm = 256

    def body(x_ref, hg_ref, s5_ref, gt_ref, wh_ref, ws_ref, wo_ref, g_ref, wu_ref, x1_ref, mg_ref, h2_ref, a_ref):
        yh = jnp.dot(hg_ref[...], wh_ref[...], preferred_element_type=f32)
        ys = jnp.dot(s5_ref[...], ws_ref[...], preferred_element_type=f32)
        merged = (_sig(gt_ref[:, 0:D]) * yh + _sig(gt_ref[:, D:2 * D]) * ys).astype(bf16)
        mg_ref[...] = merged
        x1 = x_ref[...] + jnp.dot(merged, wo_ref[...], preferred_element_type=f32)
        x1_ref[...] = x1
        xh, _ = _rms(x1)
        h2 = (xh * g_ref[...]).astype(bf16)
        h2_ref[...] = h2
        a_ref[...] = jnp.dot(h2, wu_ref[...], preferred_element_type=f32)

    row = lambda n: pl.BlockSpec((tm, n), lambda i: (i, 0))
    return _pc(
        body, name="mix_up", grid=(T // tm,),
        in_specs=[row(D), row(HW), row(SW), row(2 * D), _whole(w_bhg.shape), _whole(w_bs5.shape), _whole(w_out.shape),
                  _whole((1, D)), _whole(w_up.shape)],
        out_specs=[row(D), row(D), row(D), row(2 * DFF)],
        out_shape=[jax.ShapeDtypeStruct((T, D), f32), jax.ShapeDtypeStruct((T, D), bf16),
                   jax.ShapeDtypeStruct((T, D), bf16), jax.ShapeDtypeStruct((T, 2 * DFF), f32)],
        compiler_params=_params(),
    )(x, hg_o, s5_o, gates, w_bhg, w_bs5, w_out, g_ffn, w_up)


FFN_TM = 128


def _conv_rows(a, halo, first, conv_w, conv_b):
    tm = a.shape[0]
    row = lax.broadcasted_iota(jnp.int32, (tm, 1), 0)
    hm1 = jnp.where(first, 0.0, halo[7:8, :])
    hm2 = jnp.where(first, 0.0, halo[6:7, :])
    a1 = jnp.where(row == 0, hm1, pltpu.roll(a, 1, 0))
    a2 = jnp.where(row == 0, hm2, jnp.where(row == 1, hm1, pltpu.roll(a, 2, 0)))
    c = conv_b + conv_w[0:1, :] * a2 + conv_w[1:2, :] * a1 + conv_w[2:3, :] * a
    return c, a1, a2


def _ffn_tail(a, conv_w, conv_b, w_down, x1, p, g_ple, w_pg, w_pp, g_fin, tgt):
    T = a.shape[0]
    tm = FFN_TM

    def body(a_ref, ah_ref, cw_ref, cb_ref, wd_ref, x1_ref, p_ref, gp_ref, wpg_ref, wpp_ref, gf_ref, t_ref,
             dx2_ref, gd_ref, ga_ref, gb_ref, h3_ref, dz_ref, dpp_ref, loss_ref, dgf_ref, dgp_ref):
        i = pl.program_id(0)

        @pl.when(i == 0)
        def _():
            loss_ref[...] = jnp.zeros_like(loss_ref)
            dgf_ref[...] = jnp.zeros_like(dgf_ref)
            dgp_ref[...] = jnp.zeros_like(dgp_ref)

        c, _, _ = _conv_rows(a_ref[...], ah_ref[...], i == 0, cw_ref[...], cb_ref[...])
        gl, gp = _gelu_and_grad(c[:, 0:DFF])
        ga_ref[...] = c[:, DFF:] * gp
        gb_ref[...] = gl
        gated = (gl * c[:, DFF:]).astype(bf16)
        gd_ref[...] = gated
        x2 = x1_ref[...] + jnp.dot(gated, wd_ref[...], preferred_element_type=f32)
        xh2, r2 = _rms(x2)
        h3 = (xh2 * gp_ref[...]).astype(bf16)
        h3_ref[...] = h3
        pg = _sig(jnp.dot(h3, wpg_ref[...], preferred_element_type=f32))
        pp = _dot(p_ref[...], wpp_ref[...])
        x3 = x2 + pg * pp
        xh3, r3 = _rms(x3)
        diff = xh3 * gf_ref[...] - t_ref[...]
        loss_ref[...] += 0.5 * jnp.sum(jnp.mean(diff * diff, axis=-1, keepdims=True), axis=0, keepdims=True)
        dy = diff * (1.0 / D)
        dx3, dgf_rows = _rms_bwd(dy, xh3, r3, gf_ref[...])
        dgf_ref[...] += _colsum(dgf_rows)
        dpp = dx3 * pg
        dpp_ref[...] = dpp.astype(bf16)
        dz = dx3 * pp * pg * (1.0 - pg)
        dz_ref[...] = dz.astype(bf16)
        dh3 = _dot_nt(dz, wpg_ref[...])
        dx2n, dgp_rows = _rms_bwd(dh3, xh2, r2, gp_ref[...])
        dgp_ref[...] += _colsum(dgp_rows)
        dx2_ref[...] = dx3 + dx2n

    row = lambda n: pl.BlockSpec((tm, n), lambda i: (i, 0))
    halo = pl.BlockSpec((8, 2 * DFF), lambda i: (jnp.maximum(i * (tm // 8) - 1, 0), 0))
    return _pc(
        body, name="ffn_tail", grid=(T // tm,),
        in_specs=[row(2 * DFF), halo, _whole((3, 2 * DFF)), _whole((1, 2 * DFF)), _whole(w_down.shape), row(D), row(PLE),
                  _whole((1, D)), _whole(w_pg.shape), _whole(w_pp.shape), _whole((1, D)), row(D)],
        out_specs=[row(D), row(DFF), row(DFF), row(DFF), row(D), row(D), row(D), _acc((1, 128)), _acc((1, D)), _acc((1, D))],
        out_shape=[jax.ShapeDtypeStruct((T, D), f32), jax.ShapeDtypeStruct((T, DFF), bf16),
                   jax.ShapeDtypeStruct((T, DFF), f32), jax.ShapeDtypeStruct((T, DFF), f32), jax.ShapeDtypeStruct((T, D), bf16),
                   jax.ShapeDtypeStruct((T, D), bf16), jax.ShapeDtypeStruct((T, D), bf16),
                   jax.ShapeDtypeStruct((1, 128), f32), jax.ShapeDtypeStruct((1, D), f32), jax.ShapeDtypeStruct((1, D), f32)],
        compiler_params=_params(),
    )(a, a, conv_w, conv_b, w_down, x1, p, g_ple, w_pg, w_pp, g_fin, tgt)


def _ffn_bwd(dx2, a, g_a, g_b, conv_w, w_down, w_up, x1, g_ffn, exch=None):
    T = a.shape[0]
    tm = FFN_TM
    nt = T // tm

    def body(dx2_ref, a_ref, ga_ref, gb_ref, cw_ref, wd_ref, wu_ref, x1_ref, g_ref,
             da_ref, dx1_ref, dcw_ref, dcb_ref, dg_ref, carry_ref):
        i = pl.program_id(0)

        @pl.when(i == 0)
        def _():
            carry_ref[...] = jnp.zeros_like(carry_ref)
            dcw_ref[...] = jnp.zeros_like(dcw_ref)
            dcb_ref[...] = jnp.zeros_like(dcb_ref)
            dg_ref[...] = jnp.zeros_like(dg_ref)

        av = a_ref[...]
        cw = cw_ref[...]
        dx2 = dx2_ref[...]
        dgd = _dot_nt(dx2, wd_ref[...])
        dc = jnp.concatenate([dgd * ga_ref[...], dgd * gb_ref[...]], axis=1)
        row = lax.broadcasted_iota(jnp.int32, (tm, 1), 0)
        n1 = carry_ref[0:1, :]
        n2 = carry_ref[1:2, :]
        up1 = jnp.where(row == tm - 1, n1, pltpu.roll(dc, tm - 1, 0))
        up2 = jnp.where(row == tm - 1, n2, jnp.where(row == tm - 2, n1, pltpu.roll(dc, tm - 2, 0)))
        dcb_ref[...] += _colsum(dc)
        dcw_ref[0:1, :] += _colsum(up2 * av)
        dcw_ref[1:2, :] += _colsum(up1 * av)
        dcw_ref[2:3, :] += _colsum(dc * av)
        da = (cw[2:3, :] * dc + cw[1:2, :] * up1 + cw[0:1, :] * up2).astype(bf16)
        carry_ref[...] = dc[0:8, :]
        da_ref[...] = da
        dh2 = lax.dot_general(da, wu_ref[...], (((1,), (1,)), ((), ())), preferred_element_type=f32)
        xh, r = _rms(x1_ref[...])
        dx1n, dg_rows = _rms_bwd(dh2, xh, r, g_ref[...])
        dg_ref[...] += _colsum(dg_rows)
        dx1_ref[...] = dx2 + dx1n

    rev = lambda i: nt - 1 - i
    row = lambda n: pl.BlockSpec((tm, n), lambda i: (rev(i), 0))
    return _pc_behind(
        body, exch, nt, name="ffn_bwd", grid=(nt,),
        in_specs=[row(D), row(2 * DFF), row(DFF), row(DFF), _whole((3, 2 * DFF)), _whole(w_down.shape),
                  _whole(w_up.shape), row(D), _whole((1, D))],
        out_specs=[row(2 * DFF), row(D), _acc((3, 2 * DFF)), _acc((1, 2 * DFF)), _acc((1, D))],
        out_shape=[jax.ShapeDtypeStruct((T, 2 * DFF), bf16), jax.ShapeDtypeStruct((T, D), f32),
                   jax.ShapeDtypeStruct((3, 2 * DFF), f32), jax.ShapeDtypeStruct((1, 2 * DFF), f32),
                   jax.ShapeDtypeStruct((1, D), f32)],
        scratch_shapes=[pltpu.VMEM((8, 2 * DFF), f32)],
        compiler_params=_params(), args=(dx2, a, g_a, g_b, conv_w, w_down, w_up, x1, g_ffn))


def _mix_bwd(dx1, hg_o, s5_o, gates, w_bhg, w_bs5, w_out, exch=None):
    T = dx1.shape[0]
    tm = 256

    def body(dx1_ref, hg_ref, s5_ref, gt_ref, wh_ref, ws_ref, wo_ref, dgt_ref, dhg_ref, ds5_ref, dyh_ref, dys_ref):
        dm = _dot_nt(dx1_ref[...], wo_ref[...])
        yh = jnp.dot(hg_ref[...], wh_ref[...], preferred_element_type=f32)
        ys = jnp.dot(s5_ref[...], ws_ref[...], preferred_element_type=f32)
        sh = _sig(gt_ref[:, 0:D])
        ss = _sig(gt_ref[:, D:2 * D])
        dgt_ref[:, 0:D] = dm * yh * sh * (1.0 - sh)
        dgt_ref[:, D:2 * D] = dm * ys * ss * (1.0 - ss)
        dyh = (dm * sh).astype(bf16)
        dys = (dm * ss).astype(bf16)
        dyh_ref[...] = dyh
        dys_ref[...] = dys
        dhg_ref[...] = lax.dot_general(dyh, wh_ref[...], (((1,), (1,)), ((), ())), preferred_element_type=f32)
        ds5_ref[...] = lax.dot_general(dys, ws_ref[...], (((1,), (1,)), ((), ())), preferred_element_type=f32)

    row = lambda n: pl.BlockSpec((tm, n), lambda i: (i, 0))
    return _pc_behind(
        body, exch, T // tm, name="mix_bwd", grid=(T // tm,),
        in_specs=[row(D), row(HW), row(SW), row(2 * D), _whole(w_bhg.shape), _whole(w_bs5.shape), _whole(w_out.shape)],
        out_specs=[row(2 * D), row(HW), row(SW), row(D), row(D)],
        out_shape=[jax.ShapeDtypeStruct((T, 2 * D), f32), jax.ShapeDtypeStruct((T, HW), f32), jax.ShapeDtypeStruct((T, SW), f32),
                   jax.ShapeDtypeStruct((T, D), bf16), jax.ShapeDtypeStruct((T, D), bf16)],
        compiler_params=_params(), args=(dx1, hg_o, s5_o, gates, w_bhg, w_bs5, w_out))


def _in_bwd(d_hg, d_u, d_gt, x, dx1, w, g):
    T = x.shape[0]
    tm = 256

    def body(dhg_ref, du_ref, dgt_ref, x_ref, dx1_ref, w_ref, g_ref, dx_ref, dg_ref):
        @pl.when(pl.program_id(0) == 0)
        def _():
            dg_ref[...] = jnp.zeros_like(dg_ref)

        dh = (_dot_nt(dhg_ref[...], w_ref[:, 0:4 * HW]) + _dot_nt(du_ref[...], w_ref[:, 4 * HW:4 * HW + SW])
              + _dot_nt(dgt_ref[...], w_ref[:, 4 * HW + SW:]))
        xh, r = _rms(x_ref[...])
        dxn, dg_rows = _rms_bwd(dh, xh, r, g_ref[...])
        dg_ref[...] += _colsum(dg_rows)
        dx_ref[...] = dx1_ref[...] + dxn

    row = lambda n: pl.BlockSpec((tm, n), lambda i: (i, 0))
    return _pc(
        body, name="in_bwd", grid=(T // tm,),
        in_specs=[row(4 * HW), row(SW), row(2 * D), row(D), row(D), _whole(w.shape), _whole((1, D))],
        out_specs=[row(D), _acc((1, D))],
        out_shape=[jax.ShapeDtypeStruct((T, D), f32), jax.ShapeDtypeStruct((1, D), f32)],
        compiler_params=_params(),
    )(d_hg, d_u, d_gt, x, dx1, w, g)


def _wgrad(name, a, b, nj=None, a_blk=None, a_idx=None, b_blk=None, b_idx=None):
    T = a.shape[0]
    tm = 512
    dense = nj is None
    if dense:
        K, N = a.shape[1], b.shape[1]
        a_blk, a_idx = K, (lambda j: 0)
        b_blk = N
        while K * b_blk * 4 > 6 * 1024 * 1024 and b_blk % 256 == 0:
            b_blk //= 2
        nj, b_idx = N // b_blk, (lambda j: j)

    def body(a_ref, b_ref, o_ref):
        @pl.when(pl.program_id(1) == 0)
        def _():
            o_ref[...] = jnp.zeros_like(o_ref)

        o_ref[0] += _dot_tn(a_ref[...], b_ref[...])

    out = _pc(
        body, name=name, grid=(nj, T // tm),
        in_specs=[pl.BlockSpec((tm, a_blk), lambda j, i: (i, a_idx(j))), pl.BlockSpec((tm, b_blk), lambda j, i: (i, b_idx(j)))],
        out_specs=pl.BlockSpec((1, a_blk, b_blk), lambda j, i: (j, 0, 0)),
        out_shape=jax.ShapeDtypeStruct((nj, a_blk, b_blk), f32),
        compiler_params=_params(2),
    )(a, b)
    if dense:
        return out[0] if nj == 1 else jnp.transpose(out, (1, 0, 2)).reshape(a.shape[1], b.shape[1])
    return out


ANY = pl.BlockSpec(memory_space=pl.ANY)


AG_SEMS = [pltpu.SemaphoreType.DMA((7,)), pltpu.SemaphoreType.DMA((7,)), pltpu.SemaphoreType.DMA]


def _ag_steps(x_ref, out_ref, send_sems, recv_sems, local_sem):
    x, y, c = lax.axis_index("x"), lax.axis_index("y"), lax.axis_index("c")
    me, sibling = (x, y, c), (x, y, 1 - c)
    chips = [(1 - x, y), (x, 1 - y), (1 - x, 1 - y)]

    def slot(px, py, pc):
        return out_ref.at[4 * px + 2 * py + pc]

    def copy(k, block, to, src=None):
        return pltpu.make_async_remote_copy(
            src_ref=slot(*block) if src is None else src, dst_ref=slot(*block),
            send_sem=send_sems.at[k], recv_sem=recv_sems.at[k], device_id=to, device_id_type=MESH)

    def mine():
        return pltpu.make_async_copy(x_ref, slot(*me), local_sem)

    def first():
        return [copy(0, me, sibling, src=x_ref)] + [copy(1 + j, me, (*chip, c), src=x_ref) for j, chip in enumerate(chips)]

    def passed():
        return [copy(4 + j, (*chip, c), sibling) for j, chip in enumerate(chips)]

    def start():
        mine().start()
        for cp in first():
            cp.start()

    def forward():
        for j, (chip, cp) in enumerate(zip(chips, passed())):
            copy(1 + j, (*chip, c), me).wait_recv()
            cp.start()

    def finish():
        copy(0, sibling, me).wait_recv()
        for j, chip in enumerate(chips):
            copy(4 + j, (*chip, 1 - c), me).wait_recv()
        for cp in first() + passed():
            cp.wait_send()
        mine().wait()

    return start, forward, finish


def _all_gather(name, shard):
    R, C = shard.shape

    def body(x_ref, out_ref, send_sems, recv_sems, local_sem):
        for phase in _ag_steps(x_ref, out_ref, send_sems, recv_sems, local_sem):
            phase()

    return _pc(
        body, name=name, in_specs=[ANY], out_specs=ANY,
        out_shape=jax.ShapeDtypeStruct((N_DEV, R, C), shard.dtype), scratch_shapes=list(AG_SEMS),
    )(shard)


class _Exchange:
    def __init__(self, kind, arrays):
        self.kind, self.arrays, self.n = kind, list(arrays), len(arrays)
        self.per = 4 if kind == "sibling" else 3
        tail = (lambda a: a.shape[2:]) if kind == "sibling" else (lambda a: a.shape[1:])
        self.out_shape = [jax.ShapeDtypeStruct((self.per, *tail(a)), a.dtype) for a in self.arrays]
        self.scratch = [pltpu.SemaphoreType.DMA((self.per * self.n,)), pltpu.SemaphoreType.DMA((self.per * self.n,))]

    def steps(self, in_refs, out_refs, send_sems, recv_sems):
        x, y, c = lax.axis_index("x"), lax.axis_index("y"), lax.axis_index("c")
        chips = [(1 - x, y), (x, 1 - y), (1 - x, 1 - y)]

        def copies():
            cps = []
            for i, (src, dst) in enumerate(zip(in_refs, out_refs)):
                for k in range(self.per):
                    if self.kind == "sibling":
                        s, to = src.at[k, 1 - c], (x, y, 1 - c)
                    else:
                        s, to = src.at[2 * chips[k][0] + chips[k][1]], (*chips[k], c)
                    cps.append(pltpu.make_async_remote_copy(
                        src_ref=s, dst_ref=dst.at[k], send_sem=send_sems.at[self.per * i + k],
                        recv_sem=recv_sems.at[self.per * i + k], device_id=to, device_id_type=MESH))
            return cps

        def start():
            for cp in copies():
                cp.start()

        def finish():
            for cp in copies():
                cp.wait()

        return start, finish


def _exchange_call(name, exch):
    n = exch.n

    def body(*refs):
        start, finish = exch.steps(refs[:n], refs[n:2 * n], *refs[2 * n:])
        start()
        finish()

    return _pc(body, name=name, in_specs=[ANY] * n, out_specs=[ANY] * n, out_shape=exch.out_shape,
               scratch_shapes=exch.scratch)(*exch.arrays)


def _pc_behind(body, exch, nsteps, *, in_specs, out_specs, out_shape, args, scratch_shapes=(), **kw):
    if exch is None:
        return _pc(body, in_specs=in_specs, out_specs=out_specs, out_shape=out_shape, scratch_shapes=list(scratch_shapes),
                   **kw)(*args), None
    n_in, n_out, n_scr, ne = len(in_specs), len(out_specs), len(scratch_shapes), exch.n

    def wrapped(*refs):
        ins, e_in = refs[:n_in], refs[n_in:n_in + ne]
        o0 = n_in + ne
        outs, e_out = refs[o0:o0 + n_out], refs[o0 + n_out:o0 + n_out + ne]
        s0 = o0 + n_out + ne
        scr, sems = refs[s0:s0 + n_scr], refs[s0 + n_scr:]
        start, finish = exch.steps(e_in, e_out, *sems)
        i = pl.program_id(0)
        pl.when(i == 0)(start)
        body(*ins, *outs, *scr)
        pl.when(i == nsteps - 1)(finish)

    res = _pc(wrapped, in_specs=list(in_specs) + [ANY] * ne, out_specs=list(out_specs) + [ANY] * ne,
              out_shape=list(out_shape) + exch.out_shape, scratch_shapes=list(scratch_shapes) + exch.scratch,
              **kw)(*args, *exch.arrays)
    return res[:n_out], res[n_out:]


def _add_halves(name, g4, got, ids):
    _, _, K, c = g4.shape

    def body(ids_ref, a_ref, b_ref, p16_ref, own_ref):
        s = a_ref[0, 0] + b_ref[0]
        p16_ref[0] = s.astype(bf16)

        @pl.when(pl.program_id(0) == ids_ref[1])
        def _():
            own_ref[...] = s

    return _pc(
        body, name=name,
        grid_spec=pltpu.PrefetchScalarGridSpec(
            num_scalar_prefetch=1, grid=(4,),
            in_specs=[pl.BlockSpec((1, 1, K, c), lambda k, ids: (k, ids[0], 0, 0)),
                      pl.BlockSpec((1, K, c), lambda k, ids: (k, 0, 0))],
            out_specs=[pl.BlockSpec((1, K, c), lambda k, ids: (k, 0, 0)), pl.BlockSpec((K, c), lambda k, ids: (0, 0))]),
        out_shape=[jax.ShapeDtypeStruct((4, K, c), bf16), jax.ShapeDtypeStruct((K, c), f32)],
        compiler_params=_params(),
    )(ids, g4, got)


def _row_tile(K):
    for cand in (256, 176, 128, 64):
        if K % cand == 0:
            return cand
    return K


def _adam_shard(name, own, got3, w, m, v):
    K, c = own.shape
    tr = _row_tile(K)

    def body(own_ref, got_ref, w_ref, m_ref, v_ref, g_ref, d_ref, m2_ref, v2_ref):
        g = own_ref[...] + got_ref[0].astype(f32) + got_ref[1].astype(f32) + got_ref[2].astype(f32)
        g_ref[0] = g
        delta, m2, v2 = _adam_math(g, w_ref[0], m_ref[0], v_ref[0])
        d_ref[0] = delta
        m2_ref[0] = m2
        v2_ref[0] = v2

    blk = pl.BlockSpec((1, tr, c), lambda i: (0, i, 0))
    out = jax.ShapeDtypeStruct((1, K, c), f32)
    return _pc(
        body, name=name, grid=(K // tr,),
        in_specs=[pl.BlockSpec((tr, c), lambda i: (i, 0)), pl.BlockSpec((3, tr, c), lambda i: (0, i, 0)), blk, blk, blk],
        out_specs=[blk, blk, blk, blk], out_shape=[out, out, out, out], compiler_params=_params(),
    )(own, got3, w, m, v)


def _allreduce_small(grads):
    n = len(grads)
    shapes = [g.shape for g in grads]

    def body(*refs):
        g_refs, outs, recv = refs[0:n], refs[n:2 * n], refs[2 * n:5 * n]
        send_sems, recv_sems = refs[5 * n:]
        x, y, c = lax.axis_index("x"), lax.axis_index("y"), lax.axis_index("c")
        peers = [(x, y, 1 - c), (1 - x, y, c), (x, 1 - y, c)]
        for i in range(n):
            outs[i][...] = g_refs[i][...]
        for s, peer in enumerate(peers):
            cps = [pltpu.make_async_remote_copy(src_ref=outs[i], dst_ref=recv[s * n + i], send_sem=send_sems.at[s * n + i],
                                                recv_sem=recv_sems.at[s * n + i], device_id=peer, device_id_type=MESH)
                   for i in range(n)]
            for cp in cps:
                cp.start()
            for cp in cps:
                cp.wait()
            for i in range(n):
                outs[i][...] = outs[i][...] + recv[s * n + i][...]

    return _pc(
        body, name="allreduce_small", grid=(1,), in_specs=[_whole(s) for s in shapes], out_specs=[_acc(s) for s in shapes],
        out_shape=[jax.ShapeDtypeStruct(s, f32) for s in shapes],
        scratch_shapes=[pltpu.VMEM(s, f32) for s in shapes] * 3
        + [pltpu.SemaphoreType.DMA((3 * n,)), pltpu.SemaphoreType.DMA((3 * n,))],
        compiler_params=_params(),
    )(*grads)


def _adam_small(grads, ws, ms, vs):
    n = len(grads)
    shapes = [g.shape for g in grads]

    def body(*refs):
        g_refs, w_refs, m_refs, v_refs = refs[0:n], refs[n:2 * n], refs[2 * n:3 * n], refs[3 * n:4 * n]
        outs = refs[4 * n:8 * n]
        for i in range(n):
            g = g_refs[i][...]
            delta, m2, v2 = _adam_math(g, w_refs[i][...], m_refs[i][...], v_refs[i][...])
            outs[i][...] = g
            outs[n + i][...] = delta
            outs[2 * n + i][...] = m2
            outs[3 * n + i][...] = v2

    return _pc(
        body, name="adam_small", grid=(1,), in_specs=[_whole(s) for s in shapes] * 4, out_specs=[_acc(s) for s in shapes] * 4,
        out_shape=[jax.ShapeDtypeStruct(s, f32) for s in shapes] * 4, compiler_params=_params(),
    )(*grads, *ws, *ms, *vs)


def _adam_math(g, w, m, v):
    m2 = ADAM_B1 * m + (1.0 - ADAM_B1) * g
    v2 = ADAM_B2 * v + (1.0 - ADAM_B2) * (g * g)
    m_hat = m2 / (1.0 - ADAM_B1 ** ADAM_STEP)
    v_hat = v2 / (1.0 - ADAM_B2 ** ADAM_STEP)
    delta = -ADAM_LR * (m_hat / (jnp.sqrt(v_hat) + ADAM_EPS) + ADAM_WD * w)
    return delta, m2, v2


def _pack(arrs, dtype, row_mult):
    rows = []
    for a in arrs:
        flat = a.reshape(-1).astype(dtype)
        pad = (-flat.shape[0]) % LANES
        if pad:
            flat = jnp.concatenate([flat, jnp.zeros((pad,), dtype)])
        rows.append(flat.reshape(-1, LANES))
    out = jnp.concatenate(rows, axis=0)
    pad = (-out.shape[0]) % row_mult
    if pad:
        out = jnp.concatenate([out, jnp.zeros((pad, LANES), dtype)], axis=0)
    return out


def _unpack(buf, shapes):
    lead = buf.shape[:-2]
    outs, r = [], 0
    for shp in shapes:
        n = math.prod(shp)
        nr = -(-n // LANES)
        piece = buf[..., r:r + nr, :].reshape(*lead, nr * LANES)[..., :n]
        outs.append(piece.reshape(*lead, *shp))
        r += nr
    return outs


def _to_slabs(full, axis):
    shp = full.shape
    n = shp[axis] // N_DEV
    return jnp.moveaxis(full.reshape(*shp[:axis], N_DEV, n, *shp[axis + 1:]), axis, 0)


def _from_slabs(slabs, axis):
    t = jnp.moveaxis(slabs, 0, axis)
    shp = t.shape
    return t.reshape(*shp[:axis], shp[axis] * shp[axis + 1], *shp[axis + 2:])


def _s5_discretise(lam_re, lam_im, log_dt, b_re, b_im):
    dt = jnp.exp(log_dt)[:, None]
    mag = jnp.exp(lam_re * dt)
    a_re = mag * jnp.cos(lam_im * dt)
    a_im = mag * jnp.sin(lam_im * dt)
    den = lam_re * lam_re + lam_im * lam_im
    coef_re = ((a_re - 1.0) * lam_re + a_im * lam_im) / den
    coef_im = (a_im * lam_re - (a_re - 1.0) * lam_im) / den
    bbar_re = coef_re[..., None] * b_re - coef_im[..., None] * b_im
    bbar_im = coef_re[..., None] * b_im + coef_im[..., None] * b_re
    return a_re, a_im, bbar_re, bbar_im


def _s5_operands(bbar_re, bbar_im, c_re, c_im):
    eye = jnp.eye(SG // NST, dtype=f32)

    def b_op(bb):
        return jnp.einsum("sgnq,gh->sgqhn", bb.reshape(NST, SG // NST, SN, SP), eye).reshape(NST, 128, STW)

    def c_op(cc):
        return jnp.einsum("sgpn,gh->shngp", cc.reshape(NST, SG // NST, SP, SN), eye).reshape(NST, STW, 128)

    bdb = jnp.concatenate([b_op(bbar_re), b_op(bbar_im)], axis=0)
    bdc = jnp.concatenate([c_op(c_re), c_op(-c_im)], axis=0)
    return bdb, bdc


_BIG = ["w_in", "s5_glu_w", "w_branch_hg", "w_branch_s5", "w_out", "w_up", "w_down", "w_ple_gate", "w_ple_proj", "conv_w"]
_BIG_AXIS = {"w_in": 1, "s5_glu_w": 0, "w_branch_hg": 1, "w_branch_s5": 1, "w_out": 0, "w_up": 1, "w_down": 0,
             "w_ple_gate": 0, "w_ple_proj": 1, "conv_w": 1}
_SMALL = ["norm_mix_g", "hg_lb_logits", "hg_norm_g", "s5_lambda_re", "s5_lambda_im", "s5_log_dt", "s5_b_re", "s5_b_im",
          "s5_c_re", "s5_c_im", "s5_d", "s5_glu_b", "norm_ffn_g", "conv_b", "norm_ple_g", "norm_final_g"]
_ORDER = ["norm_mix_g", "w_in", "hg_lb_logits", "hg_norm_g", "s5_lambda_re", "s5_lambda_im", "s5_log_dt", "s5_b_re",
          "s5_b_im", "s5_c_re", "s5_c_im", "s5_d", "s5_glu_w", "s5_glu_b", "w_branch_hg", "w_branch_s5", "w_out",
          "norm_ffn_g", "w_up", "conv_w", "conv_b", "w_down", "norm_ple_g", "w_ple_gate", "w_ple_proj", "norm_final_g"]


def kernel(x, p, norm_mix_g, w_in, hg_lb_logits, hg_norm_g, s5_lambda_re, s5_lambda_im, s5_log_dt, s5_b_re, s5_b_im, s5_c_re, s5_c_im, s5_d, s5_glu_w, s5_glu_b, w_branch_hg, w_branch_s5, w_out, norm_ffn_g, w_up, conv_w, conv_b, w_down, norm_ple_g, w_ple_gate, w_ple_proj, norm_final_g, loss_target, m_norm_mix_g, m_w_in, m_hg_lb_logits, m_hg_norm_g, m_s5_lambda_re, m_s5_lambda_im, m_s5_log_dt, m_s5_b_re, m_s5_b_im, m_s5_c_re, m_s5_c_im, m_s5_d, m_s5_glu_w, m_s5_glu_b, m_w_branch_hg, m_w_branch_s5, m_w_out, m_norm_ffn_g, m_w_up, m_conv_w, m_conv_b, m_w_down, m_norm_ple_g, m_w_ple_gate, m_w_ple_proj, m_norm_final_g, v_norm_mix_g, v_w_in, v_hg_lb_logits, v_hg_norm_g, v_s5_lambda_re, v_s5_lambda_im, v_s5_log_dt, v_s5_b_re, v_s5_b_im, v_s5_c_re, v_s5_c_im, v_s5_d, v_s5_glu_w, v_s5_glu_b, v_w_branch_hg, v_w_branch_s5, v_w_out, v_norm_ffn_g, v_w_up, v_conv_w, v_conv_b, v_w_down, v_norm_ple_g, v_w_ple_gate, v_w_ple_proj, v_norm_final_g):
    W = dict(norm_mix_g=norm_mix_g, w_in=w_in, hg_lb_logits=hg_lb_logits, hg_norm_g=hg_norm_g, s5_lambda_re=s5_lambda_re, s5_lambda_im=s5_lambda_im, s5_log_dt=s5_log_dt, s5_b_re=s5_b_re, s5_b_im=s5_b_im, s5_c_re=s5_c_re, s5_c_im=s5_c_im, s5_d=s5_d, s5_glu_w=s5_glu_w, s5_glu_b=s5_glu_b, w_branch_hg=w_branch_hg, w_branch_s5=w_branch_s5, w_out=w_out, norm_ffn_g=norm_ffn_g, w_up=w_up, conv_w=conv_w, conv_b=conv_b, w_down=w_down, norm_ple_g=norm_ple_g, w_ple_gate=w_ple_gate, w_ple_proj=w_ple_proj, norm_final_g=norm_final_g)
    M = dict(norm_mix_g=m_norm_mix_g, w_in=m_w_in, hg_lb_logits=m_hg_lb_logits, hg_norm_g=m_hg_norm_g, s5_lambda_re=m_s5_lambda_re, s5_lambda_im=m_s5_lambda_im, s5_log_dt=m_s5_log_dt, s5_b_re=m_s5_b_re, s5_b_im=m_s5_b_im, s5_c_re=m_s5_c_re, s5_c_im=m_s5_c_im, s5_d=m_s5_d, s5_glu_w=m_s5_glu_w, s5_glu_b=m_s5_glu_b, w_branch_hg=m_w_branch_hg, w_branch_s5=m_w_branch_s5, w_out=m_w_out, norm_ffn_g=m_norm_ffn_g, w_up=m_w_up, conv_w=m_conv_w, conv_b=m_conv_b, w_down=m_w_down, norm_ple_g=m_norm_ple_g, w_ple_gate=m_w_ple_gate, w_ple_proj=m_w_ple_proj, norm_final_g=m_norm_final_g)
    V = dict(norm_mix_g=v_norm_mix_g, w_in=v_w_in, hg_lb_logits=v_hg_lb_logits, hg_norm_g=v_hg_norm_g, s5_lambda_re=v_s5_lambda_re, s5_lambda_im=v_s5_lambda_im, s5_log_dt=v_s5_log_dt, s5_b_re=v_s5_b_re, s5_b_im=v_s5_b_im, s5_c_re=v_s5_c_re, s5_c_im=v_s5_c_im, s5_d=v_s5_d, s5_glu_w=v_s5_glu_w, s5_glu_b=v_s5_glu_b, w_branch_hg=v_w_branch_hg, w_branch_s5=v_w_branch_s5, w_out=v_w_out, norm_ffn_g=v_norm_ffn_g, w_up=v_w_up, conv_w=v_conv_w, conv_b=v_conv_b, w_down=v_w_down, norm_ple_g=v_norm_ple_g, w_ple_gate=v_w_ple_gate, w_ple_proj=v_w_ple_proj, norm_final_g=v_norm_final_g)

    shard2 = {n: W[n][0] for n in _BIG}
    conv_bits = lax.bitcast_convert_type(shard2["conv_w"], bf16)
    groups = [["w_in", "s5_glu_w"], ["w_branch_hg", "w_branch_s5", "w_out", "w_ple_gate", "w_ple_proj", "w_down"], ["w_up"]]
    packs = [_pack([shard2[n] for n in grp] + ([conv_bits] if k == 0 else []), bf16, 16) for k, grp in enumerate(groups)]
    full = {}

    def take(k, gathered):
        pieces = _unpack(gathered, [shard2[n].shape for n in groups[k]] + ([conv_bits.shape] if k == 0 else []))
        full.update({n: _from_slabs(pc, _BIG_AXIS[n]) for n, pc in zip(groups[k], pieces)})
        return pieces

    conv_w_full = _from_slabs(lax.bitcast_convert_type(take(0, _all_gather("ag_weights", packs[0]))[-1], f32), 1)

    xt = x[0]
    pt = p[0, 0]
    tgt = loss_target[0]
    T = xt.shape[0]
    lam_re, lam_im, log_dt = s5_lambda_re[0], s5_lambda_im[0], s5_log_dt[0]
    b_re, b_im, c_re, c_im = s5_b_re[0], s5_b_im[0], s5_c_re[0], s5_c_im[0]

    def s5_prep(lam_re, lam_im, log_dt, b_re, b_im, c_re, c_im):
        a_re, a_im, bbar_re, bbar_im = _s5_discretise(lam_re, lam_im, log_dt, b_re, b_im)
        bdb, bdc = _s5_operands(bbar_re, bbar_im, c_re, c_im)
        return a_re, a_im, bdb, bdc

    (a_re, a_im, bdb, bdc), s5_prep_vjp = jax.vjp(s5_prep, lam_re, lam_im, log_dt, b_re, b_im, c_re, c_im)
    a_row = jnp.concatenate([a_re.reshape(1, SL), a_im.reshape(1, SL)], axis=1)
    bdb_b, bdc_b = bdb.astype(bf16), bdc.astype(bf16)

    h1, proj_hg, u_raw, gates, gathered1 = _in_proj(xt, norm_mix_g, full["w_in"], packs[1])
    take(1, gathered1)
    ng4 = jnp.tile(hg_norm_g, (1, NH))
    hg_o, sprev = _hgrn_fwd(proj_hg, hg_lb_logits, ng4)
    x_st, y_s5, g_s5, s5_o, gathered2 = _s5_fwd(u_raw, a_row, bdb_b, bdc_b, s5_d, full["s5_glu_w"], s5_glu_b, packs[2])
    take(2, gathered2)
    x1, merged, h2, a_up = _mix_up(xt, hg_o, s5_o, gates, full["w_branch_hg"], full["w_branch_s5"], full["w_out"],
                                   norm_ffn_g, full["w_up"])
    (dx2, gated, g_a, g_b, h3, dz_ple, dpp, loss_part, d_norm_final, d_norm_ple) = _ffn_tail(
        a_up, conv_w_full, conv_b, full["w_down"], x1, pt, norm_ple_g, full["w_ple_gate"], full["w_ple_proj"],
        norm_final_g.reshape(1, D), tgt)

    ids = jnp.stack([lax.axis_index("c"), 2 * lax.axis_index("x") + lax.axis_index("y")]).astype(jnp.int32)
    gw, own_sum, got3 = {}, {}, {}

    def slabs(names):
        return [_to_slabs(gw[n], _BIG_AXIS[n]).reshape(4, 2, *shard2[n].shape) for n in names]

    def add_pairs(names, g4, got):
        sums = [_add_halves("rs_add_" + n, g, r, ids) for n, g, r in zip(names, g4, got)]
        own_sum.update({n: own for n, (_, own) in zip(names, sums)})
        return [p16 for p16, _ in sums]

    grp_a = ["w_down", "w_ple_gate", "w_ple_proj"]
    gw["w_down"] = _wgrad("wg_down", gated, dx2)
    gw["w_ple_gate"] = _wgrad("wg_pg", h3, dz_ple)
    gw["w_ple_proj"] = _wgrad("wg_pp", pt, dpp)
    g4_a = slabs(grp_a)
    (da_up, dx1, d_conv_w, d_conv_b, d_norm_ffn), got_a = _ffn_bwd(
        dx2, a_up, g_a, g_b, conv_w_full, full["w_down"], full["w_up"], x1, norm_ffn_g, exch=_Exchange("sibling", g4_a))
    p16_a = add_pairs(grp_a, g4_a, got_a)
    (d_gates, d_hg_o, d_s5_o, dyh, dys), got3_a = _mix_bwd(
        dx1, hg_o, s5_o, gates, full["w_branch_hg"], full["w_branch_s5"], full["w_out"], exch=_Exchange("chips", p16_a))
    got3.update(zip(grp_a, got3_a))

    grp_b = ["w_up", "w_out", "w_branch_hg", "w_branch_s5", "conv_w"]
    gw["w_up"] = _wgrad("wg_up", h2, da_up)
    gw["w_out"] = _wgrad("wg_out", merged, dx1)
    gw["w_branch_hg"] = _wgrad("wg_bhg", hg_o, dyh)
    gw["w_branch_s5"] = _wgrad("wg_bs5", s5_o, dys)
    gw["conv_w"] = d_conv_w
    g4_b = slabs(grp_b)
    (d_proj_hg, d_lb, d_hg_norm), got_b = _hgrn_bwd(proj_hg, hg_lb_logits, ng4, sprev, d_hg_o,
                                                     exch=_Exchange("sibling", g4_b))
    p16_b = add_pairs(grp_b, g4_b, got_b)
    (d_u, dz_glu, d_a_re, d_a_im, d_s5_d, d_glu_b, d_bdb, d_bdc), got3_b = _s5_bwd(
        d_s5_o, y_s5, u_raw, x_st, a_row, bdb_b, bdc_b, s5_d, full["s5_glu_w"], s5_glu_b, exch=_Exchange("chips", p16_b))
    got3.update(zip(grp_b, got3_b))
    grad_x, d_norm_mix = _in_bwd(d_proj_hg, d_u, d_gates, xt, dx1, full["w_in"], norm_mix_g)

    grp_c = ["w_in", "s5_glu_w"]
    gw["w_in"] = jnp.concatenate([_wgrad("wg_in_hg", h1, d_proj_hg), _wgrad("wg_in_u", h1, d_u),
                                  _wgrad("wg_in_gates", h1, d_gates)], axis=1)
    gw["s5_glu_w"] = _wgrad("wg_glu", g_s5, dz_glu)
    g4_c = slabs(grp_c)
    p16_c = add_pairs(grp_c, g4_c, _exchange_call("rs_sibling", _Exchange("sibling", g4_c)))
    got3.update(zip(grp_c, _exchange_call("rs_chips", _Exchange("chips", p16_c))))

    (d_lam_re, d_lam_im, d_log_dt, d_b_re, d_b_im, d_c_re, d_c_im) = s5_prep_vjp(
        (d_a_re.reshape(SG, SN), d_a_im.reshape(SG, SN), d_bdb, d_bdc))
    sm = jax.nn.softmax(hg_lb_logits, axis=0)
    d_l0 = d_lb[0] * sm[0] * sm[1]
    d_logits = jnp.stack([d_l0, -d_l0], axis=0)

    gs = {"norm_mix_g": d_norm_mix, "hg_lb_logits": d_logits, "hg_norm_g": d_hg_norm, "s5_lambda_re": d_lam_re,
          "s5_lambda_im": d_lam_im, "s5_log_dt": d_log_dt, "s5_b_re": d_b_re, "s5_b_im": d_b_im, "s5_c_re": d_c_re,
          "s5_c_im": d_c_im, "s5_d": d_s5_d, "s5_glu_b": d_glu_b, "norm_ffn_g": d_norm_ffn, "conv_b": d_conv_b,
          "norm_ple_g": d_norm_ple, "norm_final_g": d_norm_final}

    big_out = [_adam_shard("adam_" + n, own_sum[n], got3[n], W[n], M[n], V[n]) for n in _BIG]

    two_d = lambda a: a.reshape(1, -1) if a.ndim == 1 else a
    dense = lambda a: a.reshape(SG, -1) if a.ndim == 4 else two_d(a)
    g_sum = _allreduce_small([dense(gs[n].reshape(W[n].shape)) for n in _SMALL])
    small_out = _adam_small([g.reshape(two_d(W[n]).shape) for g, n in zip(g_sum, _SMALL)], [two_d(W[n]) for n in _SMALL],
                            [two_d(M[n]) for n in _SMALL], [two_d(V[n]) for n in _SMALL])

    res = {}
    for k in range(4):
        d = {n: big_out[i][k] for i, n in enumerate(_BIG)}
        d.update({n: small_out[k * len(_SMALL) + i].reshape(W[n].shape) for i, n in enumerate(_SMALL)})
        res[k] = d
    loss = lax.psum(loss_part[0, 0], ("x", "y", "c"))
    return (loss, grad_x[None], *[res[0][n] for n in _ORDER], *[res[1][n] for n in _ORDER],
            *[res[2][n] for n in _ORDER], *[res[3][n] for n in _ORDER])
```

```python
import functools
import math

import jax
import jax.numpy as jnp
from jax import lax
from jax.experimental import pallas as pl
from jax.experimental.pallas import tpu as pltpu

f32 = jnp.float32
bf16 = jnp.bfloat16
MESH = pl.DeviceIdType.MESH

N_DEV = 8
D = 1024
HW = 512
HD = 128
NH = 4
CH = 64
SW = 512
SG = 32
SP = 16
SN = 64
SL = SG * SN
NST = 4
STW = SL // NST
DFF = 2816
PLE = 256
EPS = 1e-6
LANES = 1024
VMEM_LIMIT = 56 * 1024 * 1024

ADAM_LR, ADAM_B1, ADAM_B2, ADAM_EPS, ADAM_WD, ADAM_STEP = 0.001, 0.9, 0.999, 1e-08, 0.01, 10


def _pc(body, **kw):
    return pl.pallas_call(body, **kw)


def _params(n_axes=1, **kw):
    return pltpu.CompilerParams(dimension_semantics=("arbitrary",) * n_axes, vmem_limit_bytes=VMEM_LIMIT, **kw)


def _whole(shape):
    nd = len(shape)
    return pl.BlockSpec(shape, lambda *_: (0,) * nd, pipeline_mode=pl.Buffered(1))


def _acc(shape):
    nd = len(shape)
    return pl.BlockSpec(shape, lambda *_: (0,) * nd)


def _dot(a, b):
    return jnp.dot(a.astype(bf16), b.astype(bf16), preferred_element_type=f32)


def _dot_nt(a, b):
    return lax.dot_general(a.astype(bf16), b.astype(bf16), (((1,), (1,)), ((), ())), preferred_element_type=f32)


def _dot_tn(a, b):
    return lax.dot_general(a.astype(bf16), b.astype(bf16), (((0,), (0,)), ((), ())), preferred_element_type=f32)


def _sig(x):
    return jax.nn.sigmoid(x)


def _dsilu(z, s):
    return s * (1.0 + z * (1.0 - s))


_GC = math.sqrt(2.0 / math.pi)


def _gelu_and_grad(y):
    t = jnp.tanh(_GC * (y + 0.044715 * y * y * y))
    g = 0.5 * y * (1.0 + t)
    dg = 0.5 * (1.0 + t) + 0.5 * y * (1.0 - t * t) * _GC * (1.0 + 3.0 * 0.044715 * y * y)
    return g, dg


def _rms(x):
    r = lax.rsqrt(jnp.mean(x * x, axis=-1, keepdims=True) + EPS)
    return x * r, r


def _rms_bwd(dy, xh, r, g):
    dxh = dy * g
    dx = r * (dxh - xh * jnp.mean(dxh * xh, axis=-1, keepdims=True))
    return dx, dy * xh


def _colsum(x):
    return jnp.sum(x, axis=0, keepdims=True)


def _in_proj(x, g, w, ag_shard):
    T = x.shape[0]
    tm = 256
    nt = T // tm

    def body(x_ref, g_ref, w_ref, ag_ref, h_ref, hg_ref, u_ref, gt_ref, ago_ref, send_sems, recv_sems, local_sem):
        i = pl.program_id(0)
        start, forward, finish = _ag_steps(ag_ref, ago_ref, send_sems, recv_sems, local_sem)
        pl.when(i == 0)(start)
        xh, _ = _rms(x_ref[...])
        h = (xh * g_ref[...]).astype(bf16)
        h_ref[...] = h
        hg_ref[...] = jnp.dot(h, w_ref[:, 0:4 * HW], preferred_element_type=f32)
        u_ref[...] = jnp.dot(h, w_ref[:, 4 * HW:4 * HW + SW], preferred_element_type=f32)
        gt_ref[...] = jnp.dot(h, w_ref[:, 4 * HW + SW:], preferred_element_type=f32)
        pl.when(i == nt // 2)(forward)
        pl.when(i == nt - 1)(finish)

    row = lambda n: pl.BlockSpec((tm, n), lambda i: (i, 0))
    return _pc(
        body, name="in_proj", grid=(nt,),
        in_specs=[row(D), _whole((1, D)), _whole(w.shape), ANY],
        out_specs=[row(D), row(4 * HW), row(SW), row(2 * D), ANY],
        out_shape=[jax.ShapeDtypeStruct((T, D), bf16), jax.ShapeDtypeStruct((T, 4 * HW), f32),
                   jax.ShapeDtypeStruct((T, SW), f32), jax.ShapeDtypeStruct((T, 2 * D), f32),
                   jax.ShapeDtypeStruct((N_DEV, *ag_shard.shape), ag_shard.dtype)],
        scratch_shapes=list(AG_SEMS),
        compiler_params=_params(),
    )(x, g, w, ag_shard)


HG_NC = 2


def _tri_matmul(tri, x):
    hi = x.astype(bf16)
    r1 = x - hi.astype(f32)
    mid = r1.astype(bf16)
    lo = (r1 - mid.astype(f32)).astype(bf16)
    n = x.shape[1]
    out = jnp.dot(tri.astype(bf16), jnp.concatenate([hi, mid, lo], axis=1), preferred_element_type=f32)
    return out[:, 0:n] + out[:, n:2 * n] + out[:, 2 * n:3 * n]


HG_TM = HG_NC * CH


def _chunk_tri(upper):
    r_i = lax.broadcasted_iota(jnp.int32, (HG_TM, HG_TM), 0)
    c_i = lax.broadcasted_iota(jnp.int32, (HG_TM, HG_TM), 1)
    same = (r_i // CH) == (c_i // CH)
    return same & ((c_i >= r_i) if upper else (r_i >= c_i))


def _heads(x3):
    n = x3.shape[2] // NH
    return jnp.concatenate([x3[:, :, h * n:(h + 1) * n] for h in range(NH)], axis=0)


def _unheads(xb):
    return jnp.concatenate([xb[h * HG_NC:(h + 1) * HG_NC] for h in range(NH)], axis=2)


def _bdot(a, b, ca, cb):
    return lax.dot_general(a.astype(bf16), b.astype(bf16), (((ca,), (cb,)), ((0,), (0,))), preferred_element_type=f32)


def _hgrn_gates(lg, qr, fr):
    mx = jnp.max(lg, axis=0, keepdims=True)
    e = jnp.exp(lg - mx)
    lb = e[0:1, :] / (e[0:1, :] + e[1:2, :])
    sig = _sig(fr)
    f = lb + (1.0 - lb) * sig
    k = 1.0 - f
    b = _tri_matmul(_chunk_tri(False), jnp.log(f).reshape(HG_TM, HW)).reshape(HG_NC, CH, HW)
    bref = b[:, CH // 2:CH // 2 + 1, :]
    blast = b[:, CH - 1:CH, :]
    sq = _sig(qr)
    q = qr * sq
    e1 = jnp.exp(b - bref)
    e2 = jnp.exp(bref - b)
    e3 = jnp.exp(blast - b)
    e4 = jnp.exp(b)
    r_i = lax.broadcasted_iota(jnp.int32, (CH, CH), 0)
    c_i = lax.broadcasted_iota(jnp.int32, (CH, CH), 1)
    return dict(lb=lb, qr=qr, sq=sq, sig=sig, f=f, k=k, tril=(r_i >= c_i), e1=e1, e2=e2, e3=e3, e4=e4,
                qs=q * e1, ks=k * e2, kl=k * e3, qb=q * e4, dec=jnp.exp(blast))


def _hgrn_fwd(proj_hg, logits, ng4):
    T = proj_hg.shape[0]
    nch = T // CH
    tm = HG_NC * CH

    def body(q_ref, f_ref, i_ref, og_ref, lg_ref, ng_ref, out_ref, sprev_ref, st_ref):
        @pl.when(pl.program_id(0) == 0)
        def _():
            st_ref[...] = jnp.zeros_like(st_ref)

        three = lambda ref: ref[...].reshape(HG_NC, CH, HW)
        c = _hgrn_gates(lg_ref[...], three(q_ref), three(f_ref))
        qs, ks, kl, qb, dec = (_heads(c[n]) for n in ("qs", "ks", "kl", "qb", "dec"))
        vb = _heads(three(i_ref))
        p = jnp.where(c["tril"], _bdot(qs, ks, 2, 2), 0.0)
        ut = _bdot(vb, kl, 1, 1)
        sts = []
        for h in range(NH):
            st = st_ref[h]
            for ci in range(HG_NC):
                sts.append(st)
                sprev_ref[ci, h] = st
                st = dec[h * HG_NC + ci] * st + ut[h * HG_NC + ci]
            st_ref[h] = st
        o = _bdot(p, vb, 2, 1) + _bdot(qb, jnp.stack(sts), 2, 2)
        og = og_ref[...]
        out_ref[...] = (_unheads(_rms(o)[0]).reshape(HG_TM, HW) * ng_ref[...] * (og * _sig(og))).astype(bf16)

    col = lambda j: pl.BlockSpec((tm, HW), lambda n, j=j: (n, j))
    return _pc(
        body, name="hgrn_fwd", grid=(nch // HG_NC,),
        in_specs=[col(0), col(1), col(2), col(3), _whole((2, HW)), _whole((1, HW))],
        out_specs=[pl.BlockSpec((tm, HW), lambda n: (n, 0)),
                   pl.BlockSpec((HG_NC, NH, HD, HD), lambda n: (n, 0, 0, 0))],
        out_shape=[jax.ShapeDtypeStruct((T, HW), bf16), jax.ShapeDtypeStruct((nch, NH, HD, HD), f32)],
        scratch_shapes=[pltpu.VMEM((NH, HD, HD), f32)],
        compiler_params=_params(),
    )(proj_hg, proj_hg, proj_hg, proj_hg, logits, ng4)


def _hgrn_bwd(proj_hg, logits, ng4, sprev, d_out, exch=None):
    T = proj_hg.shape[0]
    nch = T // CH
    tm = HG_NC * CH
    nst = nch // HG_NC

    def body(q_ref, f_ref, i_ref, og_ref, lg_ref, ng_ref, sp_ref, do_ref, dp_ref, dlb_ref, dng_ref, gt_ref):
        @pl.when(pl.program_id(0) == 0)
        def _():
            gt_ref[...] = jnp.zeros_like(gt_ref)
            dlb_ref[...] = jnp.zeros_like(dlb_ref)
            dng_ref[...] = jnp.zeros_like(dng_ref)

        three = lambda x: x.reshape(HG_NC, CH, HW)
        flat = lambda x: x.reshape(HG_TM, HW)
        c = _hgrn_gates(lg_ref[...], three(q_ref[...]), three(f_ref[...]))
        tril = c["tril"]
        ng = ng_ref[:, 0:HD]
        og = og_ref[...]
        sog = _sig(og)
        d_gated = do_ref[...]
        qs, ks, kl, qb, dec = (_heads(c[n]) for n in ("qs", "ks", "kl", "qb", "dec"))
        vb = _heads(three(i_ref[...]))
        spb = jnp.stack([sp_ref[ci, h] for h in range(NH) for ci in range(HG_NC)])
        p = jnp.where(tril, _bdot(qs, ks, 2, 2), 0.0)
        o = _bdot(p, vb, 2, 1) + _bdot(qb, spb, 2, 2)
        oh, r = _rms(o)
        d_o, dng_rows = _rms_bwd(_heads(three(d_gated * (og * sog))), oh, r, ng)
        dng_ref[...] += _colsum(jnp.sum(dng_rows, axis=0))
        dp = jnp.where(tril, _bdot(d_o, vb, 2, 2), 0.0)
        dst = _bdot(d_o, qb, 1, 1)
        gts = [None] * (NH * HG_NC)
        for h in range(NH):
            gt = gt_ref[h]
            for ci in reversed(range(HG_NC)):
                gts[h * HG_NC + ci] = gt
                gt = dst[h * HG_NC + ci] + dec[h * HG_NC + ci] * gt
            gt_ref[h] = gt
        gtb = jnp.stack(gts)
        dqs = _unheads(_bdot(dp, ks, 2, 1))
        dks = _unheads(_bdot(dp, qs, 1, 1))
        dkl = _unheads(_bdot(vb, gtb, 2, 1))
        dqb = _unheads(_bdot(d_o, spb, 2, 1))
        dv = _unheads(_bdot(p, d_o, 1, 1) + _bdot(kl, gtb, 2, 2))
        ddec = _unheads(jnp.sum(gtb * spb, axis=1, keepdims=True))
        dq = dqs * c["e1"] + dqb * c["e4"]
        dk = dks * c["e2"] + dkl * c["e3"]
        t_qs = dqs * c["qs"]
        t_ks = dks * c["ks"]
        t_kl = dkl * c["kl"]
        db = t_qs - t_ks - t_kl + dqb * c["qb"]
        dbref = jnp.sum(t_ks - t_qs, axis=1, keepdims=True)
        dblast = jnp.sum(t_kl, axis=1, keepdims=True) + ddec * c["dec"]
        row = lax.broadcasted_iota(jnp.int32, (HG_NC, CH, HW), 1)
        db = db + jnp.where(row == CH // 2, dbref, 0.0) + jnp.where(row == CH - 1, dblast, 0.0)
        df = three(_tri_matmul(_chunk_tri(True), flat(db))) / c["f"] - dk
        sig = c["sig"]
        dlb_ref[...] += _colsum(jnp.sum(df * (1.0 - sig), axis=0))
        dp_ref[:, 0:HW] = flat(dq * _dsilu(c["qr"], c["sq"]))
        dp_ref[:, HW:2 * HW] = flat(df * (1.0 - c["lb"]) * sig * (1.0 - sig))
        dp_ref[:, 2 * HW:3 * HW] = flat(dv)
        dp_ref[:, 3 * HW:4 * HW] = d_gated * flat(_unheads(oh * ng)) * _dsilu(og, sog)

    rev = lambda n: nst - 1 - n
    col = lambda j: pl.BlockSpec((tm, HW), lambda n, j=j: (rev(n), j))
    return _pc_behind(
        body, exch, nst, name="hgrn_bwd", grid=(nst,),
        in_specs=[col(0), col(1), col(2), col(3), _whole((2, HW)), _whole((1, HW)),
                  pl.BlockSpec((HG_NC, NH, HD, HD), lambda n: (rev(n), 0, 0, 0)),
                  pl.BlockSpec((tm, HW), lambda n: (rev(n), 0))],
        out_specs=[pl.BlockSpec((tm, 4 * HW), lambda n: (rev(n), 0)), _acc((1, HW)), _acc((1, HD))],
        out_shape=[jax.ShapeDtypeStruct((T, 4 * HW), f32), jax.ShapeDtypeStruct((1, HW), f32),
                   jax.ShapeDtypeStruct((1, HD), f32)],
        scratch_shapes=[pltpu.VMEM((NH, HD, HD), f32)],
        compiler_params=_params(), args=(proj_hg, proj_hg, proj_hg, proj_hg, logits, ng4, sprev, d_out))


S5_TM = 256
S5_SEG = 8
S5_STEPS = S5_TM // S5_SEG
NLT = SL // 128


def _s5_tables(a_ref, pw_ref, pseg_ref, descending):
    re, im = slice(0, SL), slice(SL, 2 * SL)

    def cmul(ar, ai, br, bi):
        return ar * br - ai * bi, ar * bi + ai * br

    pw_ref[0:1, :] = a_ref[...]
    m = 1
    while m < S5_STEPS:
        pr, pi = cmul(pw_ref[0:m, re], pw_ref[0:m, im], pw_ref[m - 1:m, re], pw_ref[m - 1:m, im])
        pw_ref[m:2 * m, re] = pr
        pw_ref[m:2 * m, im] = pi
        m *= 2
    base = S5_STEPS - 1
    if descending:
        pseg_ref[7:8, :] = pw_ref[base:base + 1, :]
        m = 1
        while m < 8:
            pr, pi = cmul(pseg_ref[8 - m:8, re], pseg_ref[8 - m:8, im], pseg_ref[8 - m:9 - m, re], pseg_ref[8 - m:9 - m, im])
            pseg_ref[8 - 2 * m:8 - m, re] = pr
            pseg_ref[8 - 2 * m:8 - m, im] = pi
            m *= 2
    else:
        pseg_ref[0:1, :] = pw_ref[base:base + 1, :]
        m = 1
        while m < 8:
            pr, pi = cmul(pseg_ref[0:m, re], pseg_ref[0:m, im], pseg_ref[m - 1:m, re], pseg_ref[m - 1:m, im])
            pseg_ref[m:2 * m, re] = pr
            pseg_ref[m:2 * m, im] = pi
            m *= 2


def _seg_rows(j):
    return pl.ds(j * S5_SEG, S5_SEG)


def _seg_perm(transpose=False):
    r_i = lax.broadcasted_iota(jnp.int32, (S5_TM, S5_TM), 0)
    c_i = lax.broadcasted_iota(jnp.int32, (S5_TM, S5_TM), 1)
    if transpose:
        r_i, c_i = c_i, r_i
    return c_i == S5_STEPS * (r_i % S5_SEG) + r_i // S5_SEG


def _scan_fwd(x3_ref, pw_ref, pseg_ref, carry_ref):
    row8 = lax.broadcasted_iota(jnp.int32, (S5_SEG, 128), 0)
    for lt in range(NLT):
        kr, ki = lt, NLT + lt
        lr, li = slice(lt * 128, (lt + 1) * 128), slice(SL + lt * 128, SL + (lt + 1) * 128)
        ar, ai = pw_ref[0:1, lr], pw_ref[0:1, li]
        sr = jnp.zeros((S5_SEG, 128), f32)
        si = jnp.zeros((S5_SEG, 128), f32)
        for j in range(S5_STEPS):
            sr, si = ar * sr - ai * si + x3_ref[kr, _seg_rows(j), :], ar * si + ai * sr + x3_ref[ki, _seg_rows(j), :]
            x3_ref[kr, _seg_rows(j), :] = sr
            x3_ref[ki, _seg_rows(j), :] = si
        for d in (1, 2, 4):
            pr, pi = pseg_ref[d - 1:d, lr], pseg_ref[d - 1:d, li]
            tr, ti = pltpu.roll(sr, d, 0), pltpu.roll(si, d, 0)
            m = row8 >= d
            sr, si = sr + jnp.where(m, pr * tr - pi * ti, 0.0), si + jnp.where(m, pr * ti + pi * tr, 0.0)
        c0r, c0i = carry_ref[7:8, lr], carry_ref[7:8, li]
        qr, qi = pseg_ref[:, lr], pseg_ref[:, li]
        sr, si = sr + qr * c0r - qi * c0i, si + qr * c0i + qi * c0r
        carry_ref[:, lr] = sr
        carry_ref[:, li] = si
        cr = jnp.where(row8 == 0, c0r, pltpu.roll(sr, 1, 0))
        ci = jnp.where(row8 == 0, c0i, pltpu.roll(si, 1, 0))
        for j in range(S5_STEPS):
            pr, pi = pw_ref[j:j + 1, lr], pw_ref[j:j + 1, li]
            x3_ref[kr, _seg_rows(j), :] = x3_ref[kr, _seg_rows(j), :] + pr * cr - pi * ci
            x3_ref[ki, _seg_rows(j), :] = x3_ref[ki, _seg_rows(j), :] + pr * ci + pi * cr


def _scan_bwd(g3_ref, x3_ref, xh_ref, first, pw_ref, pseg_ref, carry_ref, dar_ref, dai_ref):
    row8 = lax.broadcasted_iota(jnp.int32, (S5_SEG, 128), 0)
    for lt in range(NLT):
        kr, ki = lt, NLT + lt
        lr, li = slice(lt * 128, (lt + 1) * 128), slice(SL + lt * 128, SL + (lt + 1) * 128)
        ar, ai = pw_ref[0:1, lr], pw_ref[0:1, li]
        sr = jnp.zeros((S5_SEG, 128), f32)
        si = jnp.zeros((S5_SEG, 128), f32)
        for j in reversed(range(S5_STEPS)):
            sr, si = ar * sr + ai * si + g3_ref[kr, _seg_rows(j), :], ar * si - ai * sr + g3_ref[ki, _seg_rows(j), :]
            g3_ref[kr, _seg_rows(j), :] = sr
            g3_ref[ki, _seg_rows(j), :] = si
        for d in (1, 2, 4):
            pr, pi = pseg_ref[8 - d:9 - d, lr], pseg_ref[8 - d:9 - d, li]
            tr, ti = pltpu.roll(sr, 8 - d, 0), pltpu.roll(si, 8 - d, 0)
            m = row8 < 8 - d
            sr, si = sr + jnp.where(m, pr * tr + pi * ti, 0.0), si + jnp.where(m, pr * ti - pi * tr, 0.0)
        c0r, c0i = carry_ref[0:1, lr], carry_ref[0:1, li]
        qr, qi = pseg_ref[:, lr], pseg_ref[:, li]
        sr, si = sr + qr * c0r + qi * c0i, si + qr * c0i - qi * c0r
        carry_ref[:, lr] = sr
        carry_ref[:, li] = si
        cr = jnp.where(row8 == 7, c0r, pltpu.roll(sr, 7, 0))
        ci = jnp.where(row8 == 7, c0i, pltpu.roll(si, 7, 0))
        hr = jnp.where(first, 0.0, xh_ref[kr, 7:8, :])
        hi = jnp.where(first, 0.0, xh_ref[ki, 7:8, :])
        acc_r = jnp.zeros((S5_SEG, 128), f32)
        acc_i = jnp.zeros((S5_SEG, 128), f32)
        for j in range(S5_STEPS):
            pr, pi = pw_ref[S5_STEPS - 1 - j:S5_STEPS - j, lr], pw_ref[S5_STEPS - 1 - j:S5_STEPS - j, li]
            lam_r = g3_ref[kr, _seg_rows(j), :] + pr * cr + pi * ci
            lam_i = g3_ref[ki, _seg_rows(j), :] + pr * ci - pi * cr
            g3_ref[kr, _seg_rows(j), :] = lam_r
            g3_ref[ki, _seg_rows(j), :] = lam_i
            if j == 0:
                xpr = jnp.where(row8 == 0, hr, pltpu.roll(x3_ref[kr, _seg_rows(S5_STEPS - 1), :], 1, 0))
                xpi = jnp.where(row8 == 0, hi, pltpu.roll(x3_ref[ki, _seg_rows(S5_STEPS - 1), :], 1, 0))
            else:
                xpr = x3_ref[kr, _seg_rows(j - 1), :]
                xpi = x3_ref[ki, _seg_rows(j - 1), :]
            acc_r = acc_r + lam_r * xpr + lam_i * xpi
            acc_i = acc_i + lam_i * xpr - lam_r * xpi
        dar_ref[:, lr] += _colsum(acc_r)
        dai_ref[:, lr] += _colsum(acc_i)


def _strip(x3_ref, part, s):
    k0 = part * NLT + s * (STW // 128)
    return jnp.concatenate([x3_ref[k0 + q] for q in range(STW // 128)], axis=1)


def _s5_fwd(u, a_row, bdb, bdc, dskip, glu_w, glu_b, ag_shard):
    T = u.shape[0]
    tm = S5_TM
    nt = T // tm

    def body(u_ref, a_ref, bdb_ref, bdc_ref, ds_ref, gw_ref, gb_ref, ag_ref, x_ref, y_ref, g_ref, o_ref, ago_ref,
             pw_ref, pseg_ref, carry_ref, send_sems, recv_sems, local_sem):
        i = pl.program_id(0)
        start, forward, finish = _ag_steps(ag_ref, ago_ref, send_sems, recv_sems, local_sem)
        pl.when(i == 0)(start)

        @pl.when(i == 0)
        def _():
            carry_ref[...] = jnp.zeros_like(carry_ref)
            _s5_tables(a_ref, pw_ref, pseg_ref, descending=False)

        uv = u_ref[...]
        ub = jnp.dot(_seg_perm().astype(bf16), uv.astype(bf16), preferred_element_type=f32).astype(bf16)
        for part in range(2):
            for s in range(NST):
                bu = jnp.dot(ub[:, s * 128:(s + 1) * 128], bdb_ref[part * NST + s], preferred_element_type=f32)
                for q in range(STW // 128):
                    x_ref[part * NLT + s * (STW // 128) + q] = bu[:, q * 128:(q + 1) * 128]
        _scan_fwd(x_ref, pw_ref, pseg_ref, carry_ref)
        ys = []
        for s in range(NST):
            acc = None
            for part in range(2):
                t = jnp.dot(_strip(x_ref, part, s).astype(bf16), bdc_ref[part * NST + s], preferred_element_type=f32)
                acc = t if acc is None else acc + t
            ys.append(acc)
        y = _tri_matmul(_seg_perm(transpose=True), jnp.concatenate(ys, axis=1)) + ds_ref[...] * uv
        y_ref[...] = y
        g, _ = _gelu_and_grad(y)
        gb = g.astype(bf16)
        g_ref[...] = gb
        z = jnp.dot(gb, gw_ref[...], preferred_element_type=f32) + gb_ref[...]
        o_ref[...] = (g * _sig(z)).astype(bf16)
        pl.when(i == nt // 2)(forward)
        pl.when(i == nt - 1)(finish)

    row = lambda n: pl.BlockSpec((tm, n), lambda i: (i, 0))
    return _pc(
        body, name="s5_fwd", grid=(nt,),
        in_specs=[row(SW), _whole((1, 2 * SL)), _whole(bdb.shape), _whole(bdc.shape), _whole((1, SW)),
                  _whole((SW, SW)), _whole((1, SW)), ANY],
        out_specs=[pl.BlockSpec((2 * NLT, tm, 128), lambda i: (0, i, 0)), row(SW), row(SW), row(SW), ANY],
        out_shape=[jax.ShapeDtypeStruct((2 * NLT, T, 128), f32), jax.ShapeDtypeStruct((T, SW), f32),
                   jax.ShapeDtypeStruct((T, SW), bf16), jax.ShapeDtypeStruct((T, SW), bf16),
                   jax.ShapeDtypeStruct((N_DEV, *ag_shard.shape), ag_shard.dtype)],
        scratch_shapes=[pltpu.VMEM((S5_STEPS, 2 * SL), f32), pltpu.VMEM((8, 2 * SL), f32), pltpu.VMEM((8, 2 * SL), f32)]
        + list(AG_SEMS),
        compiler_params=_params(),
    )(u, a_row, bdb, bdc, dskip, glu_w, glu_b, ag_shard)


def _s5_bwd(d_out, y, u, x, a_row, bdb, bdc, dskip, glu_w, glu_b, exch=None):
    T = u.shape[0]
    tm = S5_TM
    nt = T // tm

    def body(do_ref, y_ref, u_ref, x_ref, xh_ref, a_ref, bdb_ref, bdc_ref, ds_ref, gw_ref, gb_ref,
             du_ref, dz_ref, dar_ref, dai_ref, dd_ref, dgb_ref, dbdb_ref, dbdc_ref, gs_ref, pw_ref, pseg_ref, carry_ref):
        i = pl.program_id(0)

        @pl.when(i == 0)
        def _():
            carry_ref[...] = jnp.zeros_like(carry_ref)
            _s5_tables(a_ref, pw_ref, pseg_ref, descending=True)
            dar_ref[...] = jnp.zeros_like(dar_ref)
            dai_ref[...] = jnp.zeros_like(dai_ref)
            dd_ref[...] = jnp.zeros_like(dd_ref)
            dgb_ref[...] = jnp.zeros_like(dgb_ref)
            dbdb_ref[...] = jnp.zeros_like(dbdb_ref)
            dbdc_ref[...] = jnp.zeros_like(dbdc_ref)

        yv = y_ref[...]
        uv = u_ref[...]
        g, gp = _gelu_and_grad(yv)
        z = jnp.dot(g.astype(bf16), gw_ref[...], preferred_element_type=f32) + gb_ref[...]
        sg = _sig(z)
        do = do_ref[...].astype(f32)
        dz = do * g * sg * (1.0 - sg)
        dz_ref[...] = dz.astype(bf16)
        dgb_ref[...] += _colsum(dz)
        dy = (do * sg + _dot_nt(dz, gw_ref[...])) * gp
        perm = _seg_perm().astype(bf16)
        dyb = jnp.dot(perm, dy.astype(bf16), preferred_element_type=f32).astype(bf16)
        dd_ref[...] += _colsum(dy * uv)
        for part in range(2):
            for s in range(NST):
                gx = lax.dot_general(dyb[:, s * 128:(s + 1) * 128], bdc_ref[part * NST + s], (((1,), (1,)), ((), ())),
                                     preferred_element_type=f32)
                for q in range(STW // 128):
                    gs_ref[part * NLT + s * (STW // 128) + q] = gx[:, q * 128:(q + 1) * 128]
        _scan_bwd(gs_ref, x_ref, xh_ref, i == nt - 1, pw_ref, pseg_ref, carry_ref, dar_ref, dai_ref)
        ub = jnp.dot(perm, uv.astype(bf16), preferred_element_type=f32).astype(bf16)
        dus = []
        for s in range(NST):
            acc = None
            for part in range(2):
                lv = _strip(gs_ref, part, s).astype(bf16)
                t = lax.dot_general(lv, bdb_ref[part * NST + s], (((1,), (1,)), ((), ())), preferred_element_type=f32)
                acc = t if acc is None else acc + t
                dbdb_ref[part * NST + s] += _dot_tn(ub[:, s * 128:(s + 1) * 128], lv)
                dbdc_ref[part * NST + s] += _dot_tn(_strip(x_ref, part, s), dyb[:, s * 128:(s + 1) * 128])
            dus.append(acc)
        du_ref[...] = _tri_matmul(_seg_perm(transpose=True), jnp.concatenate(dus, axis=1)) + dy * ds_ref[...]

    rev = lambda i: nt - 1 - i
    row = lambda n: pl.BlockSpec((tm, n), lambda i: (rev(i), 0))
    xblk = pl.BlockSpec((2 * NLT, tm, 128), lambda i: (0, rev(i), 0))
    halo = pl.BlockSpec((2 * NLT, 8, 128), lambda i: (0, jnp.maximum(rev(i) * (tm // 8) - 1, 0), 0))
    return _pc_behind(
        body, exch, nt, name="s5_bwd", grid=(nt,),
        in_specs=[row(SW), row(SW), row(SW), xblk, halo, _whole((1, 2 * SL)), _whole(bdb.shape), _whole(bdc.shape),
                  _whole((1, SW)), _whole((SW, SW)), _whole((1, SW))],
        out_specs=[row(SW), row(SW), _acc((1, SL)), _acc((1, SL)), _acc((1, SW)), _acc((1, SW)),
                   _acc(bdb.shape), _acc(bdc.shape)],
        out_shape=[jax.ShapeDtypeStruct((T, SW), f32), jax.ShapeDtypeStruct((T, SW), bf16),
                   jax.ShapeDtypeStruct((1, SL), f32), jax.ShapeDtypeStruct((1, SL), f32),
                   jax.ShapeDtypeStruct((1, SW), f32), jax.ShapeDtypeStruct((1, SW), f32),
                   jax.ShapeDtypeStruct(bdb.shape, f32), jax.ShapeDtypeStruct(bdc.shape, f32)],
        scratch_shapes=[pltpu.VMEM((2 * NLT, tm, 128), f32), pltpu.VMEM((S5_STEPS, 2 * SL), f32),
                        pltpu.VMEM((8, 2 * SL), f32), pltpu.VMEM((8, 2 * SL), f32)],
        compiler_params=_params(), args=(d_out, y, u, x, x, a_row, bdb, bdc, dskip, glu_w, glu_b))


def _mix_up(x, hg_o, s5_o, gates, w_bhg, w_bs5, w_out, g_ffn, w_up):
    T = x.shape[0]
    tm = 256

    def body(x_ref, hg_ref, s5_ref, gt_ref, wh_ref, ws_ref, wo_ref, g_ref, wu_ref, x1_ref, mg_ref, h2_ref, a_ref):
        yh = jnp.dot(hg_ref[...], wh_ref[...], preferred_element_type=f32)
        ys = jnp.dot(s5_ref[...], ws_ref[...], preferred_element_type=f32)
        merged = (_sig(gt_ref[:, 0:D]) * yh + _sig(gt_ref[:, D:2 * D]) * ys).astype(bf16)
        mg_ref[...] = merged
        x1 = x_ref[...] + jnp.dot(merged, wo_ref[...], preferred_element_type=f32)
        x1_ref[...] = x1
        xh, _ = _rms(x1)
        h2 = (xh * g_ref[...]).astype(bf16)
        h2_ref[...] = h2
        a_ref[...] = jnp.dot(h2, wu_ref[...], preferred_element_type=f32)

    row = lambda n: pl.BlockSpec((tm, n), lambda i: (i, 0))
    return _pc(
        body, name="mix_up", grid=(T // tm,),
        in_specs=[row(D), row(HW), row(SW), row(2 * D), _whole(w_bhg.shape), _whole(w_bs5.shape), _whole(w_out.shape),
                  _whole((1, D)), _whole(w_up.shape)],
        out_specs=[row(D), row(D), row(D), row(2 * DFF)],
        out_shape=[jax.ShapeDtypeStruct((T, D), f32), jax.ShapeDtypeStruct((T, D), bf16),
                   jax.ShapeDtypeStruct((T, D), bf16), jax.ShapeDtypeStruct((T, 2 * DFF), f32)],
        compiler_params=_params(),
    )(x, hg_o, s5_o, gates, w_bhg, w_bs5, w_out, g_ffn, w_up)


FFN_TM = 128


def _conv_rows(a, halo, first, conv_w, conv_b):
    tm = a.shape[0]
    row = lax.broadcasted_iota(jnp.int32, (tm, 1), 0)
    hm1 = jnp.where(first, 0.0, halo[7:8, :])
    hm2 = jnp.where(first, 0.0, halo[6:7, :])
    a1 = jnp.where(row == 0, hm1, pltpu.roll(a, 1, 0))
    a2 = jnp.where(row == 0, hm2, jnp.where(row == 1, hm1, pltpu.roll(a, 2, 0)))
    c = conv_b + conv_w[0:1, :] * a2 + conv_w[1:2, :] * a1 + conv_w[2:3, :] * a
    return c, a1, a2


def _ffn_tail(a, conv_w, conv_b, w_down, x1, p, g_ple, w_pg, w_pp, g_fin, tgt):
    T = a.shape[0]
    tm = FFN_TM

    def body(a_ref, ah_ref, cw_ref, cb_ref, wd_ref, x1_ref, p_ref, gp_ref, wpg_ref, wpp_ref, gf_ref, t_ref,
             dx2_ref, gd_ref, ga_ref, gb_ref, h3_ref, dz_ref, dpp_ref, loss_ref, dgf_ref, dgp_ref):
        i = pl.program_id(0)

        @pl.when(i == 0)
        def _():
            loss_ref[...] = jnp.zeros_like(loss_ref)
            dgf_ref[...] = jnp.zeros_like(dgf_ref)
            dgp_ref[...] = jnp.zeros_like(dgp_ref)

        c, _, _ = _conv_rows(a_ref[...], ah_ref[...], i == 0, cw_ref[...], cb_ref[...])
        gl, gp = _gelu_and_grad(c[:, 0:DFF])
        ga_ref[...] = c[:, DFF:] * gp
        gb_ref[...] = gl
        gated = (gl * c[:, DFF:]).astype(bf16)
        gd_ref[...] = gated
        x2 = x1_ref[...] + jnp.dot(gated, wd_ref[...], preferred_element_type=f32)
        xh2, r2 = _rms(x2)
        h3 = (xh2 * gp_ref[...]).astype(bf16)
        h3_ref[...] = h3
        pg = _sig(jnp.dot(h3, wpg_ref[...], preferred_element_type=f32))
        pp = _dot(p_ref[...], wpp_ref[...])
        x3 = x2 + pg * pp
        xh3, r3 = _rms(x3)
        diff = xh3 * gf_ref[...] - t_ref[...]
        loss_ref[...] += 0.5 * jnp.sum(jnp.mean(diff * diff, axis=-1, keepdims=True), axis=0, keepdims=True)
        dy = diff * (1.0 / D)
        dx3, dgf_rows = _rms_bwd(dy, xh3, r3, gf_ref[...])
        dgf_ref[...] += _colsum(dgf_rows)
        dpp = dx3 * pg
        dpp_ref[...] = dpp.astype(bf16)
        dz = dx3 * pp * pg * (1.0 - pg)
        dz_ref[...] = dz.astype(bf16)
        dh3 = _dot_nt(dz, wpg_ref[...])
        dx2n, dgp_rows = _rms_bwd(dh3, xh2, r2, gp_ref[...])
        dgp_ref[...] += _colsum(dgp_rows)
        dx2_ref[...] = dx3 + dx2n

    row = lambda n: pl.BlockSpec((tm, n), lambda i: (i, 0))
    halo = pl.BlockSpec((8, 2 * DFF), lambda i: (jnp.maximum(i * (tm // 8) - 1, 0), 0))
    return _pc(
        body, name="ffn_tail", grid=(T // tm,),
        in_specs=[row(2 * DFF), halo, _whole((3, 2 * DFF)), _whole((1, 2 * DFF)), _whole(w_down.shape), row(D), row(PLE),
                  _whole((1, D)), _whole(w_pg.shape), _whole(w_pp.shape), _whole((1, D)), row(D)],
        out_specs=[row(D), row(DFF), row(DFF), row(DFF), row(D), row(D), row(D), _acc((1, 128)), _acc((1, D)), _acc((1, D))],
        out_shape=[jax.ShapeDtypeStruct((T, D), f32), jax.ShapeDtypeStruct((T, DFF), bf16),
                   jax.ShapeDtypeStruct((T, DFF), f32), jax.ShapeDtypeStruct((T, DFF), f32), jax.ShapeDtypeStruct((T, D), bf16),
                   jax.ShapeDtypeStruct((T, D), bf16), jax.ShapeDtypeStruct((T, D), bf16),
                   jax.ShapeDtypeStruct((1, 128), f32), jax.ShapeDtypeStruct((1, D), f32), jax.ShapeDtypeStruct((1, D), f32)],
        compiler_params=_params(),
    )(a, a, conv_w, conv_b, w_down, x1, p, g_ple, w_pg, w_pp, g_fin, tgt)


def _ffn_bwd(dx2, a, g_a, g_b, conv_w, w_down, w_up, x1, g_ffn, exch=None):
    T = a.shape[0]
    tm = FFN_TM
    nt = T // tm

    def body(dx2_ref, a_ref, ga_ref, gb_ref, cw_ref, wd_ref, wu_ref, x1_ref, g_ref,
             da_ref, dx1_ref, dcw_ref, dcb_ref, dg_ref, carry_ref):
        i = pl.program_id(0)

        @pl.when(i == 0)
        def _():
            carry_ref[...] = jnp.zeros_like(carry_ref)
            dcw_ref[...] = jnp.zeros_like(dcw_ref)
            dcb_ref[...] = jnp.zeros_like(dcb_ref)
            dg_ref[...] = jnp.zeros_like(dg_ref)

        av = a_ref[...]
        cw = cw_ref[...]
        dx2 = dx2_ref[...]
        dgd = _dot_nt(dx2, wd_ref[...])
        dc = jnp.concatenate([dgd * ga_ref[...], dgd * gb_ref[...]], axis=1)
        row = lax.broadcasted_iota(jnp.int32, (tm, 1), 0)
        n1 = carry_ref[0:1, :]
        n2 = carry_ref[1:2, :]
        up1 = jnp.where(row == tm - 1, n1, pltpu.roll(dc, tm - 1, 0))
        up2 = jnp.where(row == tm - 1, n2, jnp.where(row == tm - 2, n1, pltpu.roll(dc, tm - 2, 0)))
        dcb_ref[...] += _colsum(dc)
        dcw_ref[0:1, :] += _colsum(up2 * av)
        dcw_ref[1:2, :] += _colsum(up1 * av)
        dcw_ref[2:3, :] += _colsum(dc * av)
        da = (cw[2:3, :] * dc + cw[1:2, :] * up1 + cw[0:1, :] * up2).astype(bf16)
        carry_ref[...] = dc[0:8, :]
        da_ref[...] = da
        dh2 = lax.dot_general(da, wu_ref[...], (((1,), (1,)), ((), ())), preferred_element_type=f32)
        xh, r = _rms(x1_ref[...])
        dx1n, dg_rows = _rms_bwd(dh2, xh, r, g_ref[...])
        dg_ref[...] += _colsum(dg_rows)
        dx1_ref[...] = dx2 + dx1n

    rev = lambda i: nt - 1 - i
    row = lambda n: pl.BlockSpec((tm, n), lambda i: (rev(i), 0))
    return _pc_behind(
        body, exch, nt, name="ffn_bwd", grid=(nt,),
        in_specs=[row(D), row(2 * DFF), row(DFF), row(DFF), _whole((3, 2 * DFF)), _whole(w_down.shape),
                  _whole(w_up.shape), row(D), _whole((1, D))],
        out_specs=[row(2 * DFF), row(D), _acc((3, 2 * DFF)), _acc((1, 2 * DFF)), _acc((1, D))],
        out_shape=[jax.ShapeDtypeStruct((T, 2 * DFF), bf16), jax.ShapeDtypeStruct((T, D), f32),
                   jax.ShapeDtypeStruct((3, 2 * DFF), f32), jax.ShapeDtypeStruct((1, 2 * DFF), f32),
                   jax.ShapeDtypeStruct((1, D), f32)],
        scratch_shapes=[pltpu.VMEM((8, 2 * DFF), f32)],
        compiler_params=_params(), args=(dx2, a, g_a, g_b, conv_w, w_down, w_up, x1, g_ffn))


def _mix_bwd(dx1, hg_o, s5_o, gates, w_bhg, w_bs5, w_out, exch=None):
    T = dx1.shape[0]
    tm = 256

    def body(dx1_ref, hg_ref, s5_ref, gt_ref, wh_ref, ws_ref, wo_ref, dgt_ref, dhg_ref, ds5_ref, dyh_ref, dys_ref):
        dm = _dot_nt(dx1_ref[...], wo_ref[...])
        yh = jnp.dot(hg_ref[...], wh_ref[...], preferred_element_type=f32)
        ys = jnp.dot(s5_ref[...], ws_ref[...], preferred_element_type=f32)
        sh = _sig(gt_ref[:, 0:D])
        ss = _sig(gt_ref[:, D:2 * D])
        dgt_ref[:, 0:D] = dm * yh * sh * (1.0 - sh)
        dgt_ref[:, D:2 * D] = dm * ys * ss * (1.0 - ss)
        dyh = (dm * sh).astype(bf16)
        dys = (dm * ss).astype(bf16)
        dyh_ref[...] = dyh
        dys_ref[...] = dys
        dhg_ref[...] = lax.dot_general(dyh, wh_ref[...], (((1,), (1,)), ((), ())), preferred_element_type=f32)
        ds5_ref[...] = lax.dot_general(dys, ws_ref[...], (((1,), (1,)), ((), ())), preferred_element_type=f32)

    row = lambda n: pl.BlockSpec((tm, n), lambda i: (i, 0))
    return _pc_behind(
        body, exch, T // tm, name="mix_bwd", grid=(T // tm,),
        in_specs=[row(D), row(HW), row(SW), row(2 * D), _whole(w_bhg.shape), _whole(w_bs5.shape), _whole(w_out.shape)],
        out_specs=[row(2 * D), row(HW), row(SW), row(D), row(D)],
        out_shape=[jax.ShapeDtypeStruct((T, 2 * D), f32), jax.ShapeDtypeStruct((T, HW), f32), jax.ShapeDtypeStruct((T, SW), f32),
                   jax.ShapeDtypeStruct((T, D), bf16), jax.ShapeDtypeStruct((T, D), bf16)],
        compiler_params=_params(), args=(dx1, hg_o, s5_o, gates, w_bhg, w_bs5, w_out))


def _in_bwd(d_hg, d_u, d_gt, x, dx1, w, g):
    T = x.shape[0]
    tm = 256

    def body(dhg_ref, du_ref, dgt_ref, x_ref, dx1_ref, w_ref, g_ref, dx_ref, dg_ref):
        @pl.when(pl.program_id(0) == 0)
        def _():
            dg_ref[...] = jnp.zeros_like(dg_ref)

        dh = (_dot_nt(dhg_ref[...], w_ref[:, 0:4 * HW]) + _dot_nt(du_ref[...], w_ref[:, 4 * HW:4 * HW + SW])
              + _dot_nt(dgt_ref[...], w_ref[:, 4 * HW + SW:]))
        xh, r = _rms(x_ref[...])
        dxn, dg_rows = _rms_bwd(dh, xh, r, g_ref[...])
        dg_ref[...] += _colsum(dg_rows)
        dx_ref[...] = dx1_ref[...] + dxn

    row = lambda n: pl.BlockSpec((tm, n), lambda i: (i, 0))
    return _pc(
        body, name="in_bwd", grid=(T // tm,),
        in_specs=[row(4 * HW), row(SW), row(2 * D), row(D), row(D), _whole(w.shape), _whole((1, D))],
        out_specs=[row(D), _acc((1, D))],
        out_shape=[jax.ShapeDtypeStruct((T, D), f32), jax.ShapeDtypeStruct((1, D), f32)],
        compiler_params=_params(),
    )(d_hg, d_u, d_gt, x, dx1, w, g)


def _wgrad(name, a, b, nj=None, a_blk=None, a_idx=None, b_blk=None, b_idx=None):
    T = a.shape[0]
    tm = 512
    dense = nj is None
    if dense:
        K, N = a.shape[1], b.shape[1]
        a_blk, a_idx = K, (lambda j: 0)
        b_blk = N
        while K * b_blk * 4 > 6 * 1024 * 1024 and b_blk % 256 == 0:
            b_blk //= 2
        nj, b_idx = N // b_blk, (lambda j: j)

    def body(a_ref, b_ref, o_ref):
        @pl.when(pl.program_id(1) == 0)
        def _():
            o_ref[...] = jnp.zeros_like(o_ref)

        o_ref[0] += _dot_tn(a_ref[...], b_ref[...])

    out = _pc(
        body, name=name, grid=(nj, T // tm),
        in_specs=[pl.BlockSpec((tm, a_blk), lambda j, i: (i, a_idx(j))), pl.BlockSpec((tm, b_blk), lambda j, i: (i, b_idx(j)))],
        out_specs=pl.BlockSpec((1, a_blk, b_blk), lambda j, i: (j, 0, 0)),
        out_shape=jax.ShapeDtypeStruct((nj, a_blk, b_blk), f32),
        compiler_params=_params(2),
    )(a, b)
    if dense:
        return out[0] if nj == 1 else jnp.transpose(out, (1, 0, 2)).reshape(a.shape[1], b.shape[1])
    return out


ANY = pl.BlockSpec(memory_space=pl.ANY)


AG_SEMS = [pltpu.SemaphoreType.DMA((7,)), pltpu.SemaphoreType.DMA((7,)), pltpu.SemaphoreType.DMA]


def _ag_steps(x_ref, out_ref, send_sems, recv_sems, local_sem):
    x, y, c = lax.axis_index("x"), lax.axis_index("y"), lax.axis_index("c")
    me, sibling = (x, y, c), (x, y, 1 - c)
    chips = [(1 - x, y), (x, 1 - y), (1 - x, 1 - y)]

    def slot(px, py, pc):
        return out_ref.at[4 * px + 2 * py + pc]

    def copy(k, block, to, src=None):
        return pltpu.make_async_remote_copy(
            src_ref=slot(*block) if src is None else src, dst_ref=slot(*block),
            send_sem=send_sems.at[k], recv_sem=recv_sems.at[k], device_id=to, device_id_type=MESH)

    def mine():
        return pltpu.make_async_copy(x_ref, slot(*me), local_sem)

    def first():
        return [copy(0, me, sibling, src=x_ref)] + [copy(1 + j, me, (*chip, c), src=x_ref) for j, chip in enumerate(chips)]

    def passed():
        return [copy(4 + j, (*chip, c), sibling) for j, chip in enumerate(chips)]

    def start():
        mine().start()
        for cp in first():
            cp.start()

    def forward():
        for j, (chip, cp) in enumerate(zip(chips, passed())):
            copy(1 + j, (*chip, c), me).wait_recv()
            cp.start()

    def finish():
        copy(0, sibling, me).wait_recv()
        for j, chip in enumerate(chips):
            copy(4 + j, (*chip, 1 - c), me).wait_recv()
        for cp in first() + passed():
            cp.wait_send()
        mine().wait()

    return start, forward, finish


def _all_gather(name, shard):
    R, C = shard.shape

    def body(x_ref, out_ref, send_sems, recv_sems, local_sem):
        for phase in _ag_steps(x_ref, out_ref, send_sems, recv_sems, local_sem):
            phase()

    return _pc(
        body, name=name, in_specs=[ANY], out_specs=ANY,
        out_shape=jax.ShapeDtypeStruct((N_DEV, R, C), shard.dtype), scratch_shapes=list(AG_SEMS),
    )(shard)


class _Exchange:
    def __init__(self, kind, arrays):
        self.kind, self.arrays, self.n = kind, list(arrays), len(arrays)
        self.per = 4 if kind == "sibling" else 3
        tail = (lambda a: a.shape[2:]) if kind == "sibling" else (lambda a: a.shape[1:])
        self.out_shape = [jax.ShapeDtypeStruct((self.per, *tail(a)), a.dtype) for a in self.arrays]
        self.scratch = [pltpu.SemaphoreType.DMA((self.per * self.n,)), pltpu.SemaphoreType.DMA((self.per * self.n,))]

    def steps(self, in_refs, out_refs, send_sems, recv_sems):
        x, y, c = lax.axis_index("x"), lax.axis_index("y"), lax.axis_index("c")
        chips = [(1 - x, y), (x, 1 - y), (1 - x, 1 - y)]

        def copies():
            cps = []
            for i, (src, dst) in enumerate(zip(in_refs, out_refs)):
                for k in range(self.per):
                    if self.kind == "sibling":
                        s, to = src.at[k, 1 - c], (x, y, 1 - c)
                    else:
                        s, to = src.at[2 * chips[k][0] + chips[k][1]], (*chips[k], c)
                    cps.append(pltpu.make_async_remote_copy(
                        src_ref=s, dst_ref=dst.at[k], send_sem=send_sems.at[self.per * i + k],
                        recv_sem=recv_sems.at[self.per * i + k], device_id=to, device_id_type=MESH))
            return cps

        def start():
            for cp in copies():
                cp.start()

        def finish():
            for cp in copies():
                cp.wait()

        return start, finish


def _exchange_call(name, exch):
    n = exch.n

    def body(*refs):
        start, finish = exch.steps(refs[:n], refs[n:2 * n], *refs[2 * n:])
        start()
        finish()

    return _pc(body, name=name, in_specs=[ANY] * n, out_specs=[ANY] * n, out_shape=exch.out_shape,
               scratch_shapes=exch.scratch)(*exch.arrays)


def _pc_behind(body, exch, nsteps, *, in_specs, out_specs, out_shape, args, scratch_shapes=(), **kw):
    if exch is None:
        return _pc(body, in_specs=in_specs, out_specs=out_specs, out_shape=out_shape, scratch_shapes=list(scratch_shapes),
                   **kw)(*args), None
    n_in, n_out, n_scr, ne = len(in_specs), len(out_specs), len(scratch_shapes), exch.n

    def wrapped(*refs):
        ins, e_in = refs[:n_in], refs[n_in:n_in + ne]
        o0 = n_in + ne
        outs, e_out = refs[o0:o0 + n_out], refs[o0 + n_out:o0 + n_out + ne]
        s0 = o0 + n_out + ne
        scr, sems = refs[s0:s0 + n_scr], refs[s0 + n_scr:]
        start, finish = exch.steps(e_in, e_out, *sems)
        i = pl.program_id(0)
        pl.when(i == 0)(start)
        body(*ins, *outs, *scr)
        pl.when(i == nsteps - 1)(finish)

    res = _pc(wrapped, in_specs=list(in_specs) + [ANY] * ne, out_specs=list(out_specs) + [ANY] * ne,
              out_shape=list(out_shape) + exch.out_shape, scratch_shapes=list(scratch_shapes) + exch.scratch,
              **kw)(*args, *exch.arrays)
    return res[:n_out], res[n_out:]


def _add_halves(name, g4, got, ids):
    _, _, K, c = g4.shape

    def body(ids_ref, a_ref, b_ref, p16_ref, own_ref):
        s = a_ref[0, 0] + b_ref[0]
        p16_ref[0] = s.astype(bf16)

        @pl.when(pl.program_id(0) == ids_ref[1])
        def _():
            own_ref[...] = s

    return _pc(
        body, name=name,
        grid_spec=pltpu.PrefetchScalarGridSpec(
            num_scalar_prefetch=1, grid=(4,),
            in_specs=[pl.BlockSpec((1, 1, K, c), lambda k, ids: (k, ids[0], 0, 0)),
                      pl.BlockSpec((1, K, c), lambda k, ids: (k, 0, 0))],
            out_specs=[pl.BlockSpec((1, K, c), lambda k, ids: (k, 0, 0)), pl.BlockSpec((K, c), lambda k, ids: (0, 0))]),
        out_shape=[jax.ShapeDtypeStruct((4, K, c), bf16), jax.ShapeDtypeStruct((K, c), f32)],
        compiler_params=_params(),
    )(ids, g4, got)


def _row_tile(K):
    for cand in (256, 176, 128, 64):
        if K % cand == 0:
            return cand
    return K


def _adam_shard(name, own, got3, w, m, v):
    K, c = own.shape
    tr = _row_tile(K)

    def body(own_ref, got_ref, w_ref, m_ref, v_ref, g_ref, d_ref, m2_ref, v2_ref):
        g = own_ref[...] + got_ref[0].astype(f32) + got_ref[1].astype(f32) + got_ref[2].astype(f32)
        g_ref[0] = g
        delta, m2, v2 = _adam_math(g, w_ref[0], m_ref[0], v_ref[0])
        d_ref[0] = delta
        m2_ref[0] = m2
        v2_ref[0] = v2

    blk = pl.BlockSpec((1, tr, c), lambda i: (0, i, 0))
    out = jax.ShapeDtypeStruct((1, K, c), f32)
    return _pc(
        body, name=name, grid=(K // tr,),
        in_specs=[pl.BlockSpec((tr, c), lambda i: (i, 0)), pl.BlockSpec((3, tr, c), lambda i: (0, i, 0)), blk, blk, blk],
        out_specs=[blk, blk, blk, blk], out_shape=[out, out, out, out], compiler_params=_params(),
    )(own, got3, w, m, v)


def _allreduce_small(grads):
    n = len(grads)
    shapes = [g.shape for g in grads]

    def body(*refs):
        g_refs, outs, recv = refs[0:n], refs[n:2 * n], refs[2 * n:5 * n]
        send_sems, recv_sems = refs[5 * n:]
        x, y, c = lax.axis_index("x"), lax.axis_index("y"), lax.axis_index("c")
        peers = [(x, y, 1 - c), (1 - x, y, c), (x, 1 - y, c)]
        for i in range(n):
            outs[i][...] = g_refs[i][...]
        for s, peer in enumerate(peers):
            cps = [pltpu.make_async_remote_copy(src_ref=outs[i], dst_ref=recv[s * n + i], send_sem=send_sems.at[s * n + i],
                                                recv_sem=recv_sems.at[s * n + i], device_id=peer, device_id_type=MESH)
                   for i in range(n)]
            for cp in cps:
                cp.start()
            for cp in cps:
                cp.wait()
            for i in range(n):
                outs[i][...] = outs[i][...] + recv[s * n + i][...]

    return _pc(
        body, name="allreduce_small", grid=(1,), in_specs=[_whole(s) for s in shapes], out_specs=[_acc(s) for s in shapes],
        out_shape=[jax.ShapeDtypeStruct(s, f32) for s in shapes],
        scratch_shapes=[pltpu.VMEM(s, f32) for s in shapes] * 3
        + [pltpu.SemaphoreType.DMA((3 * n,)), pltpu.SemaphoreType.DMA((3 * n,))],
        compiler_params=_params(),
    )(*grads)


def _adam_small(grads, ws, ms, vs):
    n = len(grads)
    shapes = [g.shape for g in grads]

    def body(*refs):
        g_refs, w_refs, m_refs, v_refs = refs[0:n], refs[n:2 * n], refs[2 * n:3 * n], refs[3 * n:4 * n]
        outs = refs[4 * n:8 * n]
        for i in range(n):
            g = g_refs[i][...]
            delta, m2, v2 = _adam_math(g, w_refs[i][...], m_refs[i][...], v_refs[i][...])
            outs[i][...] = g
            outs[n + i][...] = delta
            outs[2 * n + i][...] = m2
            outs[3 * n + i][...] = v2

    return _pc(
        body, name="adam_small", grid=(1,), in_specs=[_whole(s) for s in shapes] * 4, out_specs=[_acc(s) for s in shapes] * 4,
        out_shape=[jax.ShapeDtypeStruct(s, f32) for s in shapes] * 4, compiler_params=_params(),
    )(*grads, *ws, *ms, *vs)


def _adam_math(g, w, m, v):
    m2 = ADAM_B1 * m + (1.0 - ADAM_B1) * g
    v2 = ADAM_B2 * v + (1.0 - ADAM_B2) * (g * g)
    m_hat = m2 / (1.0 - ADAM_B1 ** ADAM_STEP)
    v_hat = v2 / (1.0 - ADAM_B2 ** ADAM_STEP)
    delta = -ADAM_LR * (m_hat / (jnp.sqrt(v_hat) + ADAM_EPS) + ADAM_WD * w)
    return delta, m2, v2


def _pack(arrs, dtype, row_mult):
    rows = []
    for a in arrs:
        flat = a.reshape(-1).astype(dtype)
        pad = (-flat.shape[0]) % LANES
        if pad:
            flat = jnp.concatenate([flat, jnp.zeros((pad,), dtype)])
        rows.append(flat.reshape(-1, LANES))
    out = jnp.concatenate(rows, axis=0)
    pad = (-out.shape[0]) % row_mult
    if pad:
        out = jnp.concatenate([out, jnp.zeros((pad, LANES), dtype)], axis=0)
    return out


def _unpack(buf, shapes):
    lead = buf.shape[:-2]
    outs, r = [], 0
    for shp in shapes:
        n = math.prod(shp)
        nr = -(-n // LANES)
        piece = buf[..., r:r + nr, :].reshape(*lead, nr * LANES)[..., :n]
        outs.append(piece.reshape(*lead, *shp))
        r += nr
    return outs


def _to_slabs(full, axis):
    shp = full.shape
    n = shp[axis] // N_DEV
    return jnp.moveaxis(full.reshape(*shp[:axis], N_DEV, n, *shp[axis + 1:]), axis, 0)


def _from_slabs(slabs, axis):
    t = jnp.moveaxis(slabs, 0, axis)
    shp = t.shape
    return t.reshape(*shp[:axis], shp[axis] * shp[axis + 1], *shp[axis + 2:])


def _s5_discretise(lam_re, lam_im, log_dt, b_re, b_im):
    dt = jnp.exp(log_dt)[:, None]
    mag = jnp.exp(lam_re * dt)
    a_re = mag * jnp.cos(lam_im * dt)
    a_im = mag * jnp.sin(lam_im * dt)
    den = lam_re * lam_re + lam_im * lam_im
    coef_re = ((a_re - 1.0) * lam_re + a_im * lam_im) / den
    coef_im = (a_im * lam_re - (a_re - 1.0) * lam_im) / den
    bbar_re = coef_re[..., None] * b_re - coef_im[..., None] * b_im
    bbar_im = coef_re[..., None] * b_im + coef_im[..., None] * b_re
    return a_re, a_im, bbar_re, bbar_im


def _s5_operands(bbar_re, bbar_im, c_re, c_im):
    eye = jnp.eye(SG // NST, dtype=f32)

    def b_op(bb):
        return jnp.einsum("sgnq,gh->sgqhn", bb.reshape(NST, SG // NST, SN, SP), eye).reshape(NST, 128, STW)

    def c_op(cc):
        return jnp.einsum("sgpn,gh->shngp", cc.reshape(NST, SG // NST, SP, SN), eye).reshape(NST, STW, 128)

    bdb = jnp.concatenate([b_op(bbar_re), b_op(bbar_im)], axis=0)
    bdc = jnp.concatenate([c_op(c_re), c_op(-c_im)], axis=0)
    return bdb, bdc


_BIG = ["w_in", "s5_glu_w", "w_branch_hg", "w_branch_s5", "w_out", "w_up", "w_down", "w_ple_gate", "w_ple_proj", "conv_w"]
_BIG_AXIS = {"w_in": 1, "s5_glu_w": 0, "w_branch_hg": 1, "w_branch_s5": 1, "w_out": 0, "w_up": 1, "w_down": 0,
             "w_ple_gate": 0, "w_ple_proj": 1, "conv_w": 1}
_SMALL = ["norm_mix_g", "hg_lb_logits", "hg_norm_g", "s5_lambda_re", "s5_lambda_im", "s5_log_dt", "s5_b_re", "s5_b_im",
          "s5_c_re", "s5_c_im", "s5_d", "s5_glu_b", "norm_ffn_g", "conv_b", "norm_ple_g", "norm_final_g"]
_ORDER = ["norm_mix_g", "w_in", "hg_lb_logits", "hg_norm_g", "s5_lambda_re", "s5_lambda_im", "s5_log_dt", "s5_b_re",
          "s5_b_im", "s5_c_re", "s5_c_im", "s5_d", "s5_glu_w", "s5_glu_b", "w_branch_hg", "w_branch_s5", "w_out",
          "norm_ffn_g", "w_up", "conv_w", "conv_b", "w_down", "norm_ple_g", "w_ple_gate", "w_ple_proj", "norm_final_g"]


def kernel(x, p, norm_mix_g, w_in, hg_lb_logits, hg_norm_g, s5_lambda_re, s5_lambda_im, s5_log_dt, s5_b_re, s5_b_im, s5_c_re, s5_c_im, s5_d, s5_glu_w, s5_glu_b, w_branch_hg, w_branch_s5, w_out, norm_ffn_g, w_up, conv_w, conv_b, w_down, norm_ple_g, w_ple_gate, w_ple_proj, norm_final_g, loss_target, m_norm_mix_g, m_w_in, m_hg_lb_logits, m_hg_norm_g, m_s5_lambda_re, m_s5_lambda_im, m_s5_log_dt, m_s5_b_re, m_s5_b_im, m_s5_c_re, m_s5_c_im, m_s5_d, m_s5_glu_w, m_s5_glu_b, m_w_branch_hg, m_w_branch_s5, m_w_out, m_norm_ffn_g, m_w_up, m_conv_w, m_conv_b, m_w_down, m_norm_ple_g, m_w_ple_gate, m_w_ple_proj, m_norm_final_g, v_norm_mix_g, v_w_in, v_hg_lb_logits, v_hg_norm_g, v_s5_lambda_re, v_s5_lambda_im, v_s5_log_dt, v_s5_b_re, v_s5_b_im, v_s5_c_re, v_s5_c_im, v_s5_d, v_s5_glu_w, v_s5_glu_b, v_w_branch_hg, v_w_branch_s5, v_w_out, v_norm_ffn_g, v_w_up, v_conv_w, v_conv_b, v_w_down, v_norm_ple_g, v_w_ple_gate, v_w_ple_proj, v_norm_final_g):
    W = dict(norm_mix_g=norm_mix_g, w_in=w_in, hg_lb_logits=hg_lb_logits, hg_norm_g=hg_norm_g, s5_lambda_re=s5_lambda_re, s5_lambda_im=s5_lambda_im, s5_log_dt=s5_log_dt, s5_b_re=s5_b_re, s5_b_im=s5_b_im, s5_c_re=s5_c_re, s5_c_im=s5_c_im, s5_d=s5_d, s5_glu_w=s5_glu_w, s5_glu_b=s5_glu_b, w_branch_hg=w_branch_hg, w_branch_s5=w_branch_s5, w_out=w_out, norm_ffn_g=norm_ffn_g, w_up=w_up, conv_w=conv_w, conv_b=conv_b, w_down=w_down, norm_ple_g=norm_ple_g, w_ple_gate=w_ple_gate, w_ple_proj=w_ple_proj, norm_final_g=norm_final_g)
    M = dict(norm_mix_g=m_norm_mix_g, w_in=m_w_in, hg_lb_logits=m_hg_lb_logits, hg_norm_g=m_hg_norm_g, s5_lambda_re=m_s5_lambda_re, s5_lambda_im=m_s5_lambda_im, s5_log_dt=m_s5_log_dt, s5_b_re=m_s5_b_re, s5_b_im=m_s5_b_im, s5_c_re=m_s5_c_re, s5_c_im=m_s5_c_im, s5_d=m_s5_d, s5_glu_w=m_s5_glu_w, s5_glu_b=m_s5_glu_b, w_branch_hg=m_w_branch_hg, w_branch_s5=m_w_branch_s5, w_out=m_w_out, norm_ffn_g=m_norm_ffn_g, w_up=m_w_up, conv_w=m_conv_w, conv_b=m_conv_b, w_down=m_w_down, norm_ple_g=m_norm_ple_g, w_ple_gate=m_w_ple_gate, w_ple_proj=m_w_ple_proj, norm_final_g=m_norm_final_g)
    V = dict(norm_mix_g=v_norm_mix_g, w_in=v_w_in, hg_lb_logits=v_hg_lb_logits, hg_norm_g=v_hg_norm_g, s5_lambda_re=v_s5_lambda_re, s5_lambda_im=v_s5_lambda_im, s5_log_dt=v_s5_log_dt, s5_b_re=v_s5_b_re, s5_b_im=v_s5_b_im, s5_c_re=v_s5_c_re, s5_c_im=v_s5_c_im, s5_d=v_s5_d, s5_glu_w=v_s5_glu_w, s5_glu_b=v_s5_glu_b, w_branch_hg=v_w_branch_hg, w_branch_s5=v_w_branch_s5, w_out=v_w_out, norm_ffn_g=v_norm_ffn_g, w_up=v_w_up, conv_w=v_conv_w, conv_b=v_conv_b, w_down=v_w_down, norm_ple_g=v_norm_ple_g, w_ple_gate=v_w_ple_gate, w_ple_proj=v_w_ple_proj, norm_final_g=v_norm_final_g)

    shard2 = {n: W[n][0] for n in _BIG}
    conv_bits = lax.bitcast_convert_type(shard2["conv_w"], bf16)
    groups = [["w_in", "s5_glu_w"], ["w_branch_hg", "w_branch_s5", "w_out", "w_ple_gate", "w_ple_proj", "w_down"], ["w_up"]]
    packs = [_pack([shard2[n] for n in grp] + ([conv_bits] if k == 0 else []), bf16, 16) for k, grp in enumerate(groups)]
    full = {}

    def take(k, gathered):
        pieces = _unpack(gathered, [shard2[n].shape for n in groups[k]] + ([conv_bits.shape] if k == 0 else []))
        full.update({n: _from_slabs(pc, _BIG_AXIS[n]) for n, pc in zip(groups[k], pieces)})
        return pieces

    conv_w_full = _from_slabs(lax.bitcast_convert_type(take(0, _all_gather("ag_weights", packs[0]))[-1], f32), 1)

    xt = x[0]
    pt = p[0, 0]
    tgt = loss_target[0]
    T = xt.shape[0]
    lam_re, lam_im, log_dt = s5_lambda_re[0], s5_lambda_im[0], s5_log_dt[0]
    b_re, b_im, c_re, c_im = s5_b_re[0], s5_b_im[0], s5_c_re[0], s5_c_im[0]

    def s5_prep(lam_re, lam_im, log_dt, b_re, b_im, c_re, c_im):
        a_re, a_im, bbar_re, bbar_im = _s5_discretise(lam_re, lam_im, log_dt, b_re, b_im)
        bdb, bdc = _s5_operands(bbar_re, bbar_im, c_re, c_im)
        return a_re, a_im, bdb, bdc

    (a_re, a_im, bdb, bdc), s5_prep_vjp = jax.vjp(s5_prep, lam_re, lam_im, log_dt, b_re, b_im, c_re, c_im)
    a_row = jnp.concatenate([a_re.reshape(1, SL), a_im.reshape(1, SL)], axis=1)
    bdb_b, bdc_b = bdb.astype(bf16), bdc.astype(bf16)

    h1, proj_hg, u_raw, gates, gathered1 = _in_proj(xt, norm_mix_g, full["w_in"], packs[1])
    take(1, gathered1)
    ng4 = jnp.tile(hg_norm_g, (1, NH))
    hg_o, sprev = _hgrn_fwd(proj_hg, hg_lb_logits, ng4)
    x_st, y_s5, g_s5, s5_o, gathered2 = _s5_fwd(u_raw, a_row, bdb_b, bdc_b, s5_d, full["s5_glu_w"], s5_glu_b, packs[2])
    take(2, gathered2)
    x1, merged, h2, a_up = _mix_up(xt, hg_o, s5_o, gates, full["w_branch_hg"], full["w_branch_s5"], full["w_out"],
                                   norm_ffn_g, full["w_up"])
    (dx2, gated, g_a, g_b, h3, dz_ple, dpp, loss_part, d_norm_final, d_norm_ple) = _ffn_tail(
        a_up, conv_w_full, conv_b, full["w_down"], x1, pt, norm_ple_g, full["w_ple_gate"], full["w_ple_proj"],
        norm_final_g.reshape(1, D), tgt)

    ids = jnp.stack([lax.axis_index("c"), 2 * lax.axis_index("x") + lax.axis_index("y")]).astype(jnp.int32)
    gw, own_sum, got3 = {}, {}, {}

    def slabs(names):
        return [_to_slabs(gw[n], _BIG_AXIS[n]).reshape(4, 2, *shard2[n].shape) for n in names]

    def add_pairs(names, g4, got):
        sums = [_add_halves("rs_add_" + n, g, r, ids) for n, g, r in zip(names, g4, got)]
        own_sum.update({n: own for n, (_, own) in zip(names, sums)})
        return [p16 for p16, _ in sums]

    grp_a = ["w_down", "w_ple_gate", "w_ple_proj"]
    gw["w_down"] = _wgrad("wg_down", gated, dx2)
    gw["w_ple_gate"] = _wgrad("wg_pg", h3, dz_ple)
    gw["w_ple_proj"] = _wgrad("wg_pp", pt, dpp)
    g4_a = slabs(grp_a)
    (da_up, dx1, d_conv_w, d_conv_b, d_norm_ffn), got_a = _ffn_bwd(
        dx2, a_up, g_a, g_b, conv_w_full, full["w_down"], full["w_up"], x1, norm_ffn_g, exch=_Exchange("sibling", g4_a))
    p16_a = add_pairs(grp_a, g4_a, got_a)
    (d_gates, d_hg_o, d_s5_o, dyh, dys), got3_a = _mix_bwd(
        dx1, hg_o, s5_o, gates, full["w_branch_hg"], full["w_branch_s5"], full["w_out"], exch=_Exchange("chips", p16_a))
    got3.update(zip(grp_a, got3_a))

    grp_b = ["w_up", "w_out", "w_branch_hg", "w_branch_s5", "conv_w"]
    gw["w_up"] = _wgrad("wg_up", h2, da_up)
    gw["w_out"] = _wgrad("wg_out", merged, dx1)
    gw["w_branch_hg"] = _wgrad("wg_bhg", hg_o, dyh)
    gw["w_branch_s5"] = _wgrad("wg_bs5", s5_o, dys)
    gw["conv_w"] = d_conv_w
    g4_b = slabs(grp_b)
    (d_proj_hg, d_lb, d_hg_norm), got_b = _hgrn_bwd(proj_hg, hg_lb_logits, ng4, sprev, d_hg_o,
                                                     exch=_Exchange("sibling", g4_b))
    p16_b = add_pairs(grp_b, g4_b, got_b)
    (d_u, dz_glu, d_a_re, d_a_im, d_s5_d, d_glu_b, d_bdb, d_bdc), got3_b = _s5_bwd(
        d_s5_o, y_s5, u_raw, x_st, a_row, bdb_b, bdc_b, s5_d, full["s5_glu_w"], s5_glu_b, exch=_Exchange("chips", p16_b))
    got3.update(zip(grp_b, got3_b))
    grad_x, d_norm_mix = _in_bwd(d_proj_hg, d_u, d_gates, xt, dx1, full["w_in"], norm_mix_g)

    grp_c = ["w_in", "s5_glu_w"]
    gw["w_in"] = jnp.concatenate([_wgrad("wg_in_hg", h1, d_proj_hg), _wgrad("wg_in_u", h1, d_u),
                                  _wgrad("wg_in_gates", h1, d_gates)], axis=1)
    gw["s5_glu_w"] = _wgrad("wg_glu", g_s5, dz_glu)
    g4_c = slabs(grp_c)
    p16_c = add_pairs(grp_c, g4_c, _exchange_call("rs_sibling", _Exchange("sibling", g4_c)))
    got3.update(zip(grp_c, _exchange_call("rs_chips", _Exchange("chips", p16_c))))

    (d_lam_re, d_lam_im, d_log_dt, d_b_re, d_b_im, d_c_re, d_c_im) = s5_prep_vjp(
        (d_a_re.reshape(SG, SN), d_a_im.reshape(SG, SN), d_bdb, d_bdc))
    sm = jax.nn.softmax(hg_lb_logits, axis=0)
    d_l0 = d_lb[0] * sm[0] * sm[1]
    d_logits = jnp.stack([d_l0, -d_l0], axis=0)

    gs = {"norm_mix_g": d_norm_mix, "hg_lb_logits": d_logits, "hg_norm_g": d_hg_norm, "s5_lambda_re": d_lam_re,
          "s5_lambda_im": d_lam_im, "s5_log_dt": d_log_dt, "s5_b_re": d_b_re, "s5_b_im": d_b_im, "s5_c_re": d_c_re,
          "s5_c_im": d_c_im, "s5_d": d_s5_d, "s5_glu_b": d_glu_b, "norm_ffn_g": d_norm_ffn, "conv_b": d_conv_b,
          "norm_ple_g": d_norm_ple, "norm_final_g": d_norm_final}

    big_out = [_adam_shard("adam_" + n, own_sum[n], got3[n], W[n], M[n], V[n]) for n in _BIG]

    two_d = lambda a: a.reshape(1, -1) if a.ndim == 1 else a
    dense = lambda a: a.reshape(SG, -1) if a.ndim == 4 else two_d(a)
    g_sum = _allreduce_small([dense(gs[n].reshape(W[n].shape)) for n in _SMALL])
    small_out = _adam_small([g.reshape(two_d(W[n]).shape) for g, n in zip(g_sum, _SMALL)], [two_d(W[n]) for n in _SMALL],
                            [two_d(M[n]) for n in _SMALL], [two_d(V[n]) for n in _SMALL])

    res = {}
    for k in range(4):
        d = {n: big_out[i][k] for i, n in enumerate(_BIG)}
        d.update({n: small_out[k * len(_SMALL) + i].reshape(W[n].shape) for i, n in enumerate(_SMALL)})
        res[k] = d
    loss = lax.psum(loss_part[0, 0], ("x", "y", "c"))
    return (loss, grad_x[None], *[res[0][n] for n in _ORDER], *[res[1][n] for n in _ORDER],
            *[res[2][n] for n in _ORDER], *[res[3][n] for n in _ORDER])
```

```python
import functools
import math

import jax
import jax.numpy as jnp
from jax import lax
from jax.experimental import pallas as pl
from jax.experimental.pallas import tpu as pltpu

f32 = jnp.float32
bf16 = jnp.bfloat16
MESH = pl.DeviceIdType.MESH

N_DEV = 8
D = 1024
HW = 512
HD = 128
NH = 4
CH = 64
SW = 512
SG = 32
SP = 16
SN = 64
SL = SG * SN
NST = 4
STW = SL // NST
DFF = 2816
PLE = 256
EPS = 1e-6
LANES = 1024
VMEM_LIMIT = 56 * 1024 * 1024

ADAM_LR, ADAM_B1, ADAM_B2, ADAM_EPS, ADAM_WD, ADAM_STEP = 0.001, 0.9, 0.999, 1e-08, 0.01, 10


def _pc(body, **kw):
    return pl.pallas_call(body, **kw)


def _params(n_axes=1, **kw):
    return pltpu.CompilerParams(dimension_semantics=("arbitrary",) * n_axes, vmem_limit_bytes=VMEM_LIMIT, **kw)


def _whole(shape):
    nd = len(shape)
    return pl.BlockSpec(shape, lambda *_: (0,) * nd, pipeline_mode=pl.Buffered(1))


def _acc(shape):
    nd = len(shape)
    return pl.BlockSpec(shape, lambda *_: (0,) * nd)


def _dot(a, b):
    return jnp.dot(a.astype(bf16), b.astype(bf16), preferred_element_type=f32)


def _dot_nt(a, b):
    return lax.dot_general(a.astype(bf16), b.astype(bf16), (((1,), (1,)), ((), ())), preferred_element_type=f32)


def _dot_tn(a, b):
    return lax.dot_general(a.astype(bf16), b.astype(bf16), (((0,), (0,)), ((), ())), preferred_element_type=f32)


def _sig(x):
    return jax.nn.sigmoid(x)


def _dsilu(z, s):
    return s * (1.0 + z * (1.0 - s))


_GC = math.sqrt(2.0 / math.pi)


def _gelu_and_grad(y):
    t = jnp.tanh(_GC * (y + 0.044715 * y * y * y))
    g = 0.5 * y * (1.0 + t)
    dg = 0.5 * (1.0 + t) + 0.5 * y * (1.0 - t * t) * _GC * (1.0 + 3.0 * 0.044715 * y * y)
    return g, dg


def _rms(x):
    r = lax.rsqrt(jnp.mean(x * x, axis=-1, keepdims=True) + EPS)
    return x * r, r


def _rms_bwd(dy, xh, r, g):
    dxh = dy * g
    dx = r * (dxh - xh * jnp.mean(dxh * xh, axis=-1, keepdims=True))
    return dx, dy * xh


def _colsum(x):
    return jnp.sum(x, axis=0, keepdims=True)


def _in_proj(x, g, w, ag_shard):
    T = x.shape[0]
    tm = 256
    nt = T // tm

    def body(x_ref, g_ref, w_ref, ag_ref, h_ref, hg_ref, u_ref, gt_ref, ago_ref, send_sems, recv_sems, local_sem):
        i = pl.program_id(0)
        start, forward, finish = _ag_steps(ag_ref, ago_ref, send_sems, recv_sems, local_sem)
        pl.when(i == 0)(start)
        xh, _ = _rms(x_ref[...])
        h = (xh * g_ref[...]).astype(bf16)
        h_ref[...] = h
        hg_ref[...] = jnp.dot(h, w_ref[:, 0:4 * HW], preferred_element_type=f32)
        u_ref[...] = jnp.dot(h, w_ref[:, 4 * HW:4 * HW + SW], preferred_element_type=f32)
        gt_ref[...] = jnp.dot(h, w_ref[:, 4 * HW + SW:], preferred_element_type=f32)
        pl.when(i == nt // 2)(forward)
        pl.when(i == nt - 1)(finish)

    row = lambda n: pl.BlockSpec((tm, n), lambda i: (i, 0))
    return _pc(
        body, name="in_proj", grid=(nt,),
        in_specs=[row(D), _whole((1, D)), _whole(w.shape), ANY],
        out_specs=[row(D), row(4 * HW), row(SW), row(2 * D), ANY],
        out_shape=[jax.ShapeDtypeStruct((T, D), bf16), jax.ShapeDtypeStruct((T, 4 * HW), f32),
                   jax.ShapeDtypeStruct((T, SW), f32), jax.ShapeDtypeStruct((T, 2 * D), f32),
                   jax.ShapeDtypeStruct((N_DEV, *ag_shard.shape), ag_shard.dtype)],
        scratch_shapes=list(AG_SEMS),
        compiler_params=_params(),
    )(x, g, w, ag_shard)


HG_NC = 4


def _tri_matmul(tri, x):
    hi = x.astype(bf16)
    r1 = x - hi.astype(f32)
    mid = r1.astype(bf16)
    lo = (r1 - mid.astype(f32)).astype(bf16)
    n = x.shape[1]
    out = jnp.dot(tri.astype(bf16), jnp.concatenate([hi, mid, lo], axis=1), preferred_element_type=f32)
    return out[:, 0:n] + out[:, n:2 * n] + out[:, 2 * n:3 * n]


HG_TM = HG_NC * CH


def _chunk_tri(upper):
    r_i = lax.broadcasted_iota(jnp.int32, (HG_TM, HG_TM), 0)
    c_i = lax.broadcasted_iota(jnp.int32, (HG_TM, HG_TM), 1)
    same = (r_i // CH) == (c_i // CH)
    return same & ((c_i >= r_i) if upper else (r_i >= c_i))


def _heads(x3):
    n = x3.shape[2] // NH
    return jnp.concatenate([x3[:, :, h * n:(h + 1) * n] for h in range(NH)], axis=0)


def _unheads(xb):
    return jnp.concatenate([xb[h * HG_NC:(h + 1) * HG_NC] for h in range(NH)], axis=2)


def _bdot(a, b, ca, cb):
    return lax.dot_general(a.astype(bf16), b.astype(bf16), (((ca,), (cb,)), ((0,), (0,))), preferred_element_type=f32)


def _hgrn_gates(lg, qr, fr):
    mx = jnp.max(lg, axis=0, keepdims=True)
    e = jnp.exp(lg - mx)
    lb = e[0:1, :] / (e[0:1, :] + e[1:2, :])
    sig = _sig(fr)
    f = lb + (1.0 - lb) * sig
    k = 1.0 - f
    b = _tri_matmul(_chunk_tri(False), jnp.log(f).reshape(HG_TM, HW)).reshape(HG_NC, CH, HW)
    bref = b[:, CH // 2:CH // 2 + 1, :]
    blast = b[:, CH - 1:CH, :]
    sq = _sig(qr)
    q = qr * sq
    e1 = jnp.exp(b - bref)
    e2 = jnp.exp(bref - b)
    e3 = jnp.exp(blast - b)
    e4 = jnp.exp(b)
    r_i = lax.broadcasted_iota(jnp.int32, (CH, CH), 0)
    c_i = lax.broadcasted_iota(jnp.int32, (CH, CH), 1)
    return dict(lb=lb, qr=qr, sq=sq, sig=sig, f=f, k=k, tril=(r_i >= c_i), e1=e1, e2=e2, e3=e3, e4=e4,
                qs=q * e1, ks=k * e2, kl=k * e3, qb=q * e4, dec=jnp.exp(blast))


def _hgrn_fwd(proj_hg, logits, ng4):
    T = proj_hg.shape[0]
    nch = T // CH
    tm = HG_NC * CH

    def body(q_ref, f_ref, i_ref, og_ref, lg_ref, ng_ref, out_ref, sprev_ref, st_ref):
        @pl.when(pl.program_id(0) == 0)
        def _():
            st_ref[...] = jnp.zeros_like(st_ref)

        three = lambda ref: ref[...].reshape(HG_NC, CH, HW)
        c = _hgrn_gates(lg_ref[...], three(q_ref), three(f_ref))
        qs, ks, kl, qb, dec = (_heads(c[n]) for n in ("qs", "ks", "kl", "qb", "dec"))
        vb = _heads(three(i_ref))
        p = jnp.where(c["tril"], _bdot(qs, ks, 2, 2), 0.0)
        ut = _bdot(vb, kl, 1, 1)
        sts = []
        for h in range(NH):
            st = st_ref[h]
            for ci in range(HG_NC):
                sts.append(st)
                sprev_ref[ci, h] = st
                st = dec[h * HG_NC + ci] * st + ut[h * HG_NC + ci]
            st_ref[h] = st
        o = _bdot(p, vb, 2, 1) + _bdot(qb, jnp.stack(sts), 2, 2)
        og = og_ref[...]
        out_ref[...] = (_unheads(_rms(o)[0]).reshape(HG_TM, HW) * ng_ref[...] * (og * _sig(og))).astype(bf16)

    col = lambda j: pl.BlockSpec((tm, HW), lambda n, j=j: (n, j))
    return _pc(
        body, name="hgrn_fwd", grid=(nch // HG_NC,),
        in_specs=[col(0), col(1), col(2), col(3), _whole((2, HW)), _whole((1, HW))],
        out_specs=[pl.BlockSpec((tm, HW), lambda n: (n, 0)),
                   pl.BlockSpec((HG_NC, NH, HD, HD), lambda n: (n, 0, 0, 0))],
        out_shape=[jax.ShapeDtypeStruct((T, HW), bf16), jax.ShapeDtypeStruct((nch, NH, HD, HD), f32)],
        scratch_shapes=[pltpu.VMEM((NH, HD, HD), f32)],
        compiler_params=_params(),
    )(proj_hg, proj_hg, proj_hg, proj_hg, logits, ng4)


def _hgrn_bwd(proj_hg, logits, ng4, sprev, d_out, exch=None):
    T = proj_hg.shape[0]
    nch = T // CH
    tm = HG_NC * CH
    nst = nch // HG_NC

    def body(q_ref, f_ref, i_ref, og_ref, lg_ref, ng_ref, sp_ref, do_ref, dp_ref, dlb_ref, dng_ref, gt_ref):
        @pl.when(pl.program_id(0) == 0)
        def _():
            gt_ref[...] = jnp.zeros_like(gt_ref)
            dlb_ref[...] = jnp.zeros_like(dlb_ref)
            dng_ref[...] = jnp.zeros_like(dng_ref)

        three = lambda x: x.reshape(HG_NC, CH, HW)
        flat = lambda x: x.reshape(HG_TM, HW)
        c = _hgrn_gates(lg_ref[...], three(q_ref[...]), three(f_ref[...]))
        tril = c["tril"]
        ng = ng_ref[:, 0:HD]
        og = og_ref[...]
        sog = _sig(og)
        d_gated = do_ref[...]
        qs, ks, kl, qb, dec = (_heads(c[n]) for n in ("qs", "ks", "kl", "qb", "dec"))
        vb = _heads(three(i_ref[...]))
        spb = jnp.stack([sp_ref[ci, h] for h in range(NH) for ci in range(HG_NC)])
        p = jnp.where(tril, _bdot(qs, ks, 2, 2), 0.0)
        o = _bdot(p, vb, 2, 1) + _bdot(qb, spb, 2, 2)
        oh, r = _rms(o)
        d_o, dng_rows = _rms_bwd(_heads(three(d_gated * (og * sog))), oh, r, ng)
        dng_ref[...] += _colsum(jnp.sum(dng_rows, axis=0))
        dp = jnp.where(tril, _bdot(d_o, vb, 2, 2), 0.0)
        dst = _bdot(d_o, qb, 1, 1)
        gts = [None] * (NH * HG_NC)
        for h in range(NH):
            gt = gt_ref[h]
            for ci in reversed(range(HG_NC)):
                gts[h * HG_NC + ci] = gt
                gt = dst[h * HG_NC + ci] + dec[h * HG_NC + ci] * gt
            gt_ref[h] = gt
        gtb = jnp.stack(gts)
        dqs = _unheads(_bdot(dp, ks, 2, 1))
        dks = _unheads(_bdot(dp, qs, 1, 1))
        dkl = _unheads(_bdot(vb, gtb, 2, 1))
        dqb = _unheads(_bdot(d_o, spb, 2, 1))
        dv = _unheads(_bdot(p, d_o, 1, 1) + _bdot(kl, gtb, 2, 2))
        ddec = _unheads(jnp.sum(gtb * spb, axis=1, keepdims=True))
        dq = dqs * c["e1"] + dqb * c["e4"]
        dk = dks * c["e2"] + dkl * c["e3"]
        t_qs = dqs * c["qs"]
        t_ks = dks * c["ks"]
        t_kl = dkl * c["kl"]
        db = t_qs - t_ks - t_kl + dqb * c["qb"]
        dbref = jnp.sum(t_ks - t_qs, axis=1, keepdims=True)
        dblast = jnp.sum(t_kl, axis=1, keepdims=True) + ddec * c["dec"]
        row = lax.broadcasted_iota(jnp.int32, (HG_NC, CH, HW), 1)
        db = db + jnp.where(row == CH // 2, dbref, 0.0) + jnp.where(row == CH - 1, dblast, 0.0)
        df = three(_tri_matmul(_chunk_tri(True), flat(db))) / c["f"] - dk
        sig = c["sig"]
        dlb_ref[...] += _colsum(jnp.sum(df * (1.0 - sig), axis=0))
        dp_ref[:, 0:HW] = flat(dq * _dsilu(c["qr"], c["sq"]))
        dp_ref[:, HW:2 * HW] = flat(df * (1.0 - c["lb"]) * sig * (1.0 - sig))
        dp_ref[:, 2 * HW:3 * HW] = flat(dv)
        dp_ref[:, 3 * HW:4 * HW] = d_gated * flat(_unheads(oh * ng)) * _dsilu(og, sog)

    rev = lambda n: nst - 1 - n
    col = lambda j: pl.BlockSpec((tm, HW), lambda n, j=j: (rev(n), j))
    return _pc_behind(
        body, exch, nst, name="hgrn_bwd", grid=(nst,),
        in_specs=[col(0), col(1), col(2), col(3), _whole((2, HW)), _whole((1, HW)),
                  pl.BlockSpec((HG_NC, NH, HD, HD), lambda n: (rev(n), 0, 0, 0)),
                  pl.BlockSpec((tm, HW), lambda n: (rev(n), 0))],
        out_specs=[pl.BlockSpec((tm, 4 * HW), lambda n: (rev(n), 0)), _acc((1, HW)), _acc((1, HD))],
        out_shape=[jax.ShapeDtypeStruct((T, 4 * HW), f32), jax.ShapeDtypeStruct((1, HW), f32),
                   jax.ShapeDtypeStruct((1, HD), f32)],
        scratch_shapes=[pltpu.VMEM((NH, HD, HD), f32)],
        compiler_params=_params(), args=(proj_hg, proj_hg, proj_hg, proj_hg, logits, ng4, sprev, d_out))


S5_TM = 256
S5_SEG = 8
S5_STEPS = S5_TM // S5_SEG
NLT = SL // 128


def _s5_tables(a_ref, pw_ref, pseg_ref, descending):
    re, im = slice(0, SL), slice(SL, 2 * SL)

    def cmul(ar, ai, br, bi):
        return ar * br - ai * bi, ar * bi + ai * br

    pw_ref[0:1, :] = a_ref[...]
    m = 1
    while m < S5_STEPS:
        pr, pi = cmul(pw_ref[0:m, re], pw_ref[0:m, im], pw_ref[m - 1:m, re], pw_ref[m - 1:m, im])
        pw_ref[m:2 * m, re] = pr
        pw_ref[m:2 * m, im] = pi
        m *= 2
    base = S5_STEPS - 1
    if descending:
        pseg_ref[7:8, :] = pw_ref[base:base + 1, :]
        m = 1
        while m < 8:
            pr, pi = cmul(pseg_ref[8 - m:8, re], pseg_ref[8 - m:8, im], pseg_ref[8 - m:9 - m, re], pseg_ref[8 - m:9 - m, im])
            pseg_ref[8 - 2 * m:8 - m, re] = pr
            pseg_ref[8 - 2 * m:8 - m, im] = pi
            m *= 2
    else:
        pseg_ref[0:1, :] = pw_ref[base:base + 1, :]
        m = 1
        while m < 8:
            pr, pi = cmul(pseg_ref[0:m, re], pseg_ref[0:m, im], pseg_ref[m - 1:m, re], pseg_ref[m - 1:m, im])
            pseg_ref[m:2 * m, re] = pr
            pseg_ref[m:2 * m, im] = pi
            m *= 2


def _seg_rows(j):
    return pl.ds(j * S5_SEG, S5_SEG)


def _seg_perm(transpose=False):
    r_i = lax.broadcasted_iota(jnp.int32, (S5_TM, S5_TM), 0)
    c_i = lax.broadcasted_iota(jnp.int32, (S5_TM, S5_TM), 1)
    if transpose:
        r_i, c_i = c_i, r_i
    return c_i == S5_STEPS * (r_i % S5_SEG) + r_i // S5_SEG


def _scan_fwd(x3_ref, pw_ref, pseg_ref, carry_ref):
    row8 = lax.broadcasted_iota(jnp.int32, (S5_SEG, 128), 0)
    for lt in range(NLT):
        kr, ki = lt, NLT + lt
        lr, li = slice(lt * 128, (lt + 1) * 128), slice(SL + lt * 128, SL + (lt + 1) * 128)
        ar, ai = pw_ref[0:1, lr], pw_ref[0:1, li]
        sr = jnp.zeros((S5_SEG, 128), f32)
        si = jnp.zeros((S5_SEG, 128), f32)
        for j in range(S5_STEPS):
            sr, si = ar * sr - ai * si + x3_ref[kr, _seg_rows(j), :], ar * si + ai * sr + x3_ref[ki, _seg_rows(j), :]
            x3_ref[kr, _seg_rows(j), :] = sr
            x3_ref[ki, _seg_rows(j), :] = si
        for d in (1, 2, 4):
            pr, pi = pseg_ref[d - 1:d, lr], pseg_ref[d - 1:d, li]
            tr, ti = pltpu.roll(sr, d, 0), pltpu.roll(si, d, 0)
            m = row8 >= d
            sr, si = sr + jnp.where(m, pr * tr - pi * ti, 0.0), si + jnp.where(m, pr * ti + pi * tr, 0.0)
        c0r, c0i = carry_ref[7:8, lr], carry_ref[7:8, li]
        qr, qi = pseg_ref[:, lr], pseg_ref[:, li]
        sr, si = sr + qr * c0r - qi * c0i, si + qr * c0i + qi * c0r
        carry_ref[:, lr] = sr
        carry_ref[:, li] = si
        cr = jnp.where(row8 == 0, c0r, pltpu.roll(sr, 1, 0))
        ci = jnp.where(row8 == 0, c0i, pltpu.roll(si, 1, 0))
        for j in range(S5_STEPS):
            pr, pi = pw_ref[j:j + 1, lr], pw_ref[j:j + 1, li]
            x3_ref[kr, _seg_rows(j), :] = x3_ref[kr, _seg_rows(j), :] + pr * cr - pi * ci
            x3_ref[ki, _seg_rows(j), :] = x3_ref[ki, _seg_rows(j), :] + pr * ci + pi * cr


def _scan_bwd(g3_ref, x3_ref, xh_ref, first, pw_ref, pseg_ref, carry_ref, dar_ref, dai_ref):
    row8 = lax.broadcasted_iota(jnp.int32, (S5_SEG, 128), 0)
    for lt in range(NLT):
        kr, ki = lt, NLT + lt
        lr, li = slice(lt * 128, (lt + 1) * 128), slice(SL + lt * 128, SL + (lt + 1) * 128)
        ar, ai = pw_ref[0:1, lr], pw_ref[0:1, li]
        sr = jnp.zeros((S5_SEG, 128), f32)
        si = jnp.zeros((S5_SEG, 128), f32)
        for j in reversed(range(S5_STEPS)):
            sr, si = ar * sr + ai * si + g3_ref[kr, _seg_rows(j), :], ar * si - ai * sr + g3_ref[ki, _seg_rows(j), :]
            g3_ref[kr, _seg_rows(j), :] = sr
            g3_ref[ki, _seg_rows(j), :] = si
        for d in (1, 2, 4):
            pr, pi = pseg_ref[8 - d:9 - d, lr], pseg_ref[8 - d:9 - d, li]
            tr, ti = pltpu.roll(sr, 8 - d, 0), pltpu.roll(si, 8 - d, 0)
            m = row8 < 8 - d
            sr, si = sr + jnp.where(m, pr * tr + pi * ti, 0.0), si + jnp.where(m, pr * ti - pi * tr, 0.0)
        c0r, c0i = carry_ref[0:1, lr], carry_ref[0:1, li]
        qr, qi = pseg_ref[:, lr], pseg_ref[:, li]
        sr, si = sr + qr * c0r + qi * c0i, si + qr * c0i - qi * c0r
        carry_ref[:, lr] = sr
        carry_ref[:, li] = si
        cr = jnp.where(row8 == 7, c0r, pltpu.roll(sr, 7, 0))
        ci = jnp.where(row8 == 7, c0i, pltpu.roll(si, 7, 0))
        hr = jnp.where(first, 0.0, xh_ref[kr, 7:8, :])
        hi = jnp.where(first, 0.0, xh_ref[ki, 7:8, :])
        acc_r = jnp.zeros((S5_SEG, 128), f32)
        acc_i = jnp.zeros((S5_SEG, 128), f32)
        for j in range(S5_STEPS):
            pr, pi = pw_ref[S5_STEPS - 1 - j:S5_STEPS - j, lr], pw_ref[S5_STEPS - 1 - j:S5_STEPS - j, li]
            lam_r = g3_ref[kr, _seg_rows(j), :] + pr * cr + pi * ci
            lam_i = g3_ref[ki, _seg_rows(j), :] + pr * ci - pi * cr
            g3_ref[kr, _seg_rows(j), :] = lam_r
            g3_ref[ki, _seg_rows(j), :] = lam_i
            if j == 0:
                xpr = jnp.where(row8 == 0, hr, pltpu.roll(x3_ref[kr, _seg_rows(S5_STEPS - 1), :], 1, 0))
                xpi = jnp.where(row8 == 0, hi, pltpu.roll(x3_ref[ki, _seg_rows(S5_STEPS - 1), :], 1, 0))
            else:
                xpr = x3_ref[kr, _seg_rows(j - 1), :]
                xpi = x3_ref[ki, _seg_rows(j - 1), :]
            acc_r = acc_r + lam_r * xpr + lam_i * xpi
            acc_i = acc_i + lam_i * xpr - lam_r * xpi
        dar_ref[:, lr] += _colsum(acc_r)
        dai_ref[:, lr] += _colsum(acc_i)


def _strip(x3_ref, part, s):
    k0 = part * NLT + s * (STW // 128)
    return jnp.concatenate([x3_ref[k0 + q] for q in range(STW // 128)], axis=1)


def _s5_fwd(u, a_row, bdb, bdc, dskip, glu_w, glu_b, ag_shard):
    T = u.shape[0]
    tm = S5_TM
    nt = T // tm

    def body(u_ref, a_ref, bdb_ref, bdc_ref, ds_ref, gw_ref, gb_ref, ag_ref, x_ref, y_ref, g_ref, o_ref, ago_ref,
             pw_ref, pseg_ref, carry_ref, send_sems, recv_sems, local_sem):
        i = pl.program_id(0)
        start, forward, finish = _ag_steps(ag_ref, ago_ref, send_sems, recv_sems, local_sem)
        pl.when(i == 0)(start)

        @pl.when(i == 0)
        def _():
            carry_ref[...] = jnp.zeros_like(carry_ref)
            _s5_tables(a_ref, pw_ref, pseg_ref, descending=False)

        uv = u_ref[...]
        ub = jnp.dot(_seg_perm().astype(bf16), uv.astype(bf16), preferred_element_type=f32).astype(bf16)
        for part in range(2):
            for s in range(NST):
                bu = jnp.dot(ub[:, s * 128:(s + 1) * 128], bdb_ref[part * NST + s], preferred_element_type=f32)
                for q in range(STW // 128):
                    x_ref[part * NLT + s * (STW // 128) + q] = bu[:, q * 128:(q + 1) * 128]
        _scan_fwd(x_ref, pw_ref, pseg_ref, carry_ref)
        ys = []
        for s in range(NST):
            acc = None
            for part in range(2):
                t = jnp.dot(_strip(x_ref, part, s).astype(bf16), bdc_ref[part * NST + s], preferred_element_type=f32)
                acc = t if acc is None else acc + t
            ys.append(acc)
        y = _tri_matmul(_seg_perm(transpose=True), jnp.concatenate(ys, axis=1)) + ds_ref[...] * uv
        y_ref[...] = y
        g, _ = _gelu_and_grad(y)
        gb = g.astype(bf16)
        g_ref[...] = gb
        z = jnp.dot(gb, gw_ref[...], preferred_element_type=f32) + gb_ref[...]
        o_ref[...] = (g * _sig(z)).astype(bf16)
        pl.when(i == nt // 2)(forward)
        pl.when(i == nt - 1)(finish)

    row = lambda n: pl.BlockSpec((tm, n), lambda i: (i, 0))
    return _pc(
        body, name="s5_fwd", grid=(nt,),
        in_specs=[row(SW), _whole((1, 2 * SL)), _whole(bdb.shape), _whole(bdc.shape), _whole((1, SW)),
                  _whole((SW, SW)), _whole((1, SW)), ANY],
        out_specs=[pl.BlockSpec((2 * NLT, tm, 128), lambda i: (0, i, 0)), row(SW), row(SW), row(SW), ANY],
        out_shape=[jax.ShapeDtypeStruct((2 * NLT, T, 128), f32), jax.ShapeDtypeStruct((T, SW), f32),
                   jax.ShapeDtypeStruct((T, SW), bf16), jax.ShapeDtypeStruct((T, SW), bf16),
                   jax.ShapeDtypeStruct((N_DEV, *ag_shard.shape), ag_shard.dtype)],
        scratch_shapes=[pltpu.VMEM((S5_STEPS, 2 * SL), f32), pltpu.VMEM((8, 2 * SL), f32), pltpu.VMEM((8, 2 * SL), f32)]
        + list(AG_SEMS),
        compiler_params=_params(),
    )(u, a_row, bdb, bdc, dskip, glu_w, glu_b, ag_shard)


def _s5_bwd(d_out, y, u, x, a_row, bdb, bdc, dskip, glu_w, glu_b, exch=None):
    T = u.shape[0]
    tm = S5_TM
    nt = T // tm

    def body(do_ref, y_ref, u_ref, x_ref, xh_ref, a_ref, bdb_ref, bdc_ref, ds_ref, gw_ref, gb_ref,
             du_ref, dz_ref, dar_ref, dai_ref, dd_ref, dgb_ref, dbdb_ref, dbdc_ref, gs_ref, pw_ref, pseg_ref, carry_ref):
        i = pl.program_id(0)

        @pl.when(i == 0)
        def _():
            carry_ref[...] = jnp.zeros_like(carry_ref)
            _s5_tables(a_ref, pw_ref, pseg_ref, descending=True)
            dar_ref[...] = jnp.zeros_like(dar_ref)
            dai_ref[...] = jnp.zeros_like(dai_ref)
            dd_ref[...] = jnp.zeros_like(dd_ref)
            dgb_ref[...] = jnp.zeros_like(dgb_ref)
            dbdb_ref[...] = jnp.zeros_like(dbdb_ref)
            dbdc_ref[...] = jnp.zeros_like(dbdc_ref)

        yv = y_ref[...]
        uv = u_ref[...]
        g, gp = _gelu_and_grad(yv)
        z = jnp.dot(g.astype(bf16), gw_ref[...], preferred_element_type=f32) + gb_ref[...]
        sg = _sig(z)
        do = do_ref[...].astype(f32)
        dz = do * g * sg * (1.0 - sg)
        dz_ref[...] = dz.astype(bf16)
        dgb_ref[...] += _colsum(dz)
        dy = (do * sg + _dot_nt(dz, gw_ref[...])) * gp
        perm = _seg_perm().astype(bf16)
        dyb = jnp.dot(perm, dy.astype(bf16), preferred_element_type=f32).astype(bf16)
        dd_ref[...] += _colsum(dy * uv)
        for part in range(2):
            for s in range(NST):
                gx = lax.dot_general(dyb[:, s * 128:(s + 1) * 128], bdc_ref[part * NST + s], (((1,), (1,)), ((), ())),
                                     preferred_element_type=f32)
                for q in range(STW // 128):
                    gs_ref[part * NLT + s * (STW // 128) + q] = gx[:, q * 128:(q + 1) * 128]
        _scan_bwd(gs_ref, x_ref, xh_ref, i == nt - 1, pw_ref, pseg_ref, carry_ref, dar_ref, dai_ref)
        ub = jnp.dot(perm, uv.astype(bf16), preferred_element_type=f32).astype(bf16)
        dus = []
        for s in range(NST):
            acc = None
            for part in range(2):
                lv = _strip(gs_ref, part, s).astype(bf16)
                t = lax.dot_general(lv, bdb_ref[part * NST + s], (((1,), (1,)), ((), ())), preferred_element_type=f32)
                acc = t if acc is None else acc + t
                dbdb_ref[part * NST + s] += _dot_tn(ub[:, s * 128:(s + 1) * 128], lv)
                dbdc_ref[part * NST + s] += _dot_tn(_strip(x_ref, part, s), dyb[:, s * 128:(s + 1) * 128])
            dus.append(acc)
        du_ref[...] = _tri_matmul(_seg_perm(transpose=True), jnp.concatenate(dus, axis=1)) + dy * ds_ref[...]

    rev = lambda i: nt - 1 - i
    row = lambda n: pl.BlockSpec((tm, n), lambda i: (rev(i), 0))
    xblk = pl.BlockSpec((2 * NLT, tm, 128), lambda i: (0, rev(i), 0))
    halo = pl.BlockSpec((2 * NLT, 8, 128), lambda i: (0, jnp.maximum(rev(i) * (tm // 8) - 1, 0), 0))
    return _pc_behind(
        body, exch, nt, name="s5_bwd", grid=(nt,),
        in_specs=[row(SW), row(SW), row(SW), xblk, halo, _whole((1, 2 * SL)), _whole(bdb.shape), _whole(bdc.shape),
                  _whole((1, SW)), _whole((SW, SW)), _whole((1, SW))],
        out_specs=[row(SW), row(SW), _acc((1, SL)), _acc((1, SL)), _acc((1, SW)), _acc((1, SW)),
                   _acc(bdb.shape), _acc(bdc.shape)],
        out_shape=[jax.ShapeDtypeStruct((T, SW), f32), jax.ShapeDtypeStruct((T, SW), bf16),
                   jax.ShapeDtypeStruct((1, SL), f32), jax.ShapeDtypeStruct((1, SL), f32),
                   jax.ShapeDtypeStruct((1, SW), f32), jax.ShapeDtypeStruct((1, SW), f32),
                   jax.ShapeDtypeStruct(bdb.shape, f32), jax.ShapeDtypeStruct(bdc.shape, f32)],
        scratch_shapes=[pltpu.VMEM((2 * NLT, tm, 128), f32), pltpu.VMEM((S5_STEPS, 2 * SL), f32),
                        pltpu.VMEM((8, 2 * SL), f32), pltpu.VMEM((8, 2 * SL), f32)],
        compiler_params=_params(), args=(d_out, y, u, x, x, a_row, bdb, bdc, dskip, glu_w, glu_b))


def _mix_up(x, hg_o, s5_o, gates, w_bhg, w_bs5, w_out, g_ffn, w_up):
    T = x.shape[0]
    tm = 256

    def body(x_ref, hg_ref, s5_ref, gt_ref, wh_ref, ws_ref, wo_ref, g_ref, wu_ref, x1_ref, mg_ref, h2_ref, a_ref):
        yh = jnp.dot(hg_ref[...], wh_ref[...], preferred_element_type=f32)
        ys = jnp.dot(s5_ref[...], ws_ref[...], preferred_element_type=f32)
        merged = (_sig(gt_ref[:, 0:D]) * yh + _sig(gt_ref[:, D:2 * D]) * ys).astype(bf16)
        mg_ref[...] = merged
        x1 = x_ref[...] + jnp.dot(merged, wo_ref[...], preferred_element_type=f32)
        x1_ref[...] = x1
        xh, _ = _rms(x1)
        h2 = (xh * g_ref[...]).astype(bf16)
        h2_ref[...] = h2
        a_ref[...] = jnp.dot(h2, wu_ref[...], preferred_element_type=f32)

    row = lambda n: pl.BlockSpec((tm, n), lambda i: (i, 0))
    return _pc(
        body, name="mix_up", grid=(T // tm,),
        in_specs=[row(D), row(HW), row(SW), row(2 * D), _whole(w_bhg.shape), _whole(w_bs5.shape), _whole(w_out.shape),
                  _whole((1, D)), _whole(w_up.shape)],
        out_specs=[row(D), row(D), row(D), row(2 * DFF)],
        out_shape=[jax.ShapeDtypeStruct((T, D), f32), jax.ShapeDtypeStruct((T, D), bf16),
                   jax.ShapeDtypeStruct((T, D), bf16), jax.ShapeDtypeStruct((T, 2 * DFF), f32)],
        compiler_params=_params(),
    )(x, hg_o, s5_o, gates, w_bhg, w_bs5, w_out, g_ffn, w_up)


FFN_TM = 128
FFN_FS = 256


def _conv_gelu_blocks(a_ref, ah_ref, first, cw_ref, cb_ref, ga_ref, gb_ref, gated_ref):
    row8 = lax.broadcasted_iota(jnp.int32, (8, FFN_FS), 0)
    for s in range(DFF // FFN_FS):
        halves = (slice(s * FFN_FS, (s + 1) * FFN_FS), slice(DFF + s * FFN_FS, DFF + (s + 1) * FFN_FS))
        w = [[cw_ref[k:k + 1, ln] for k in range(3)] for ln in halves]
        bias = [cb_ref[:, ln] for ln in halves]
        prev = [jnp.where(first, 0.0, ah_ref[:, ln]) for ln in halves]
        p1 = [pltpu.roll(p, 1, 0) for p in prev]
        p2 = [pltpu.roll(p, 2, 0) for p in prev]
        for j in range(FFN_TM // 8):
            rows = slice(8 * j, 8 * j + 8)
            c = []
            for hf, ln in enumerate(halves):
                av = a_ref[rows, ln]
                r1, r2 = pltpu.roll(av, 1, 0), pltpu.roll(av, 2, 0)
                a1 = jnp.where(row8 >= 1, r1, p1[hf])
                a2 = jnp.where(row8 >= 2, r2, p2[hf])
                c.append(bias[hf] + w[hf][0] * a2 + w[hf][1] * a1 + w[hf][2] * av)
                p1[hf], p2[hf] = r1, r2
            gl, gp = _gelu_and_grad(c[0])
            ga_ref[rows, halves[0]] = c[1] * gp
            gb_ref[rows, halves[0]] = gl
            gated_ref[rows, halves[0]] = gl * c[1]


def _ffn_tail(a, conv_w, conv_b, w_down, x1, p, g_ple, w_pg, w_pp, g_fin, tgt):
    T = a.shape[0]
    tm = FFN_TM

    def body(a_ref, ah_ref, cw_ref, cb_ref, wd_ref, x1_ref, p_ref, gp_ref, wpg_ref, wpp_ref, gf_ref, t_ref,
             dx2_ref, gd_ref, ga_ref, gb_ref, h3_ref, dz_ref, dpp_ref, loss_ref, dgf_ref, dgp_ref, gsc_ref):
        i = pl.program_id(0)

        @pl.when(i == 0)
        def _():
            loss_ref[...] = jnp.zeros_like(loss_ref)
            dgf_ref[...] = jnp.zeros_like(dgf_ref)
            dgp_ref[...] = jnp.zeros_like(dgp_ref)

        _conv_gelu_blocks(a_ref, ah_ref, i == 0, cw_ref, cb_ref, ga_ref, gb_ref, gsc_ref)
        gated = gsc_ref[...].astype(bf16)
        gd_ref[...] = gated
        x2 = x1_ref[...] + jnp.dot(gated, wd_ref[...], preferred_element_type=f32)
        xh2, r2 = _rms(x2)
        h3 = (xh2 * gp_ref[...]).astype(bf16)
        h3_ref[...] = h3
        pg = _sig(jnp.dot(h3, wpg_ref[...], preferred_element_type=f32))
        pp = _dot(p_ref[...], wpp_ref[...])
        x3 = x2 + pg * pp
        xh3, r3 = _rms(x3)
        diff = xh3 * gf_ref[...] - t_ref[...]
        loss_ref[...] += 0.5 * jnp.sum(jnp.mean(diff * diff, axis=-1, keepdims=True), axis=0, keepdims=True)
        dy = diff * (1.0 / D)
        dx3, dgf_rows = _rms_bwd(dy, xh3, r3, gf_ref[...])
        dgf_ref[...] += _colsum(dgf_rows)
        dpp = dx3 * pg
        dpp_ref[...] = dpp.astype(bf16)
        dz = dx3 * pp * pg * (1.0 - pg)
        dz_ref[...] = dz.astype(bf16)
        dh3 = _dot_nt(dz, wpg_ref[...])
        dx2n, dgp_rows = _rms_bwd(dh3, xh2, r2, gp_ref[...])
        dgp_ref[...] += _colsum(dgp_rows)
        dx2_ref[...] = dx3 + dx2n

    row = lambda n: pl.BlockSpec((tm, n), lambda i: (i, 0))
    halo = pl.BlockSpec((8, 2 * DFF), lambda i: (jnp.maximum(i * (tm // 8) - 1, 0), 0))
    return _pc(
        body, name="ffn_tail", grid=(T // tm,),
        in_specs=[row(2 * DFF), halo, _whole((3, 2 * DFF)), _whole((1, 2 * DFF)), _whole(w_down.shape), row(D), row(PLE),
                  _whole((1, D)), _whole(w_pg.shape), _whole(w_pp.shape), _whole((1, D)), row(D)],
        out_specs=[row(D), row(DFF), row(DFF), row(DFF), row(D), row(D), row(D), _acc((1, 128)), _acc((1, D)), _acc((1, D))],
        out_shape=[jax.ShapeDtypeStruct((T, D), f32), jax.ShapeDtypeStruct((T, DFF), bf16),
                   jax.ShapeDtypeStruct((T, DFF), f32), jax.ShapeDtypeStruct((T, DFF), f32), jax.ShapeDtypeStruct((T, D), bf16),
                   jax.ShapeDtypeStruct((T, D), bf16), jax.ShapeDtypeStruct((T, D), bf16),
                   jax.ShapeDtypeStruct((1, 128), f32), jax.ShapeDtypeStruct((1, D), f32), jax.ShapeDtypeStruct((1, D), f32)],
        scratch_shapes=[pltpu.VMEM((tm, DFF), f32)],
        compiler_params=_params(),
    )(a, a, conv_w, conv_b, w_down, x1, p, g_ple, w_pg, w_pp, g_fin, tgt)


def _ffn_bwd(dx2, a, g_a, g_b, conv_w, w_down, w_up, x1, g_ffn, exch=None):
    T = a.shape[0]
    tm = FFN_TM
    nt = T // tm

    def body(dx2_ref, a_ref, ga_ref, gb_ref, cw_ref, wd_ref, wu_ref, x1_ref, g_ref,
             da_ref, dx1_ref, dcw_ref, dcb_ref, dg_ref, carry_ref, dgd_ref, dasc_ref):
        i = pl.program_id(0)

        @pl.when(i == 0)
        def _():
            carry_ref[...] = jnp.zeros_like(carry_ref)
            dcw_ref[...] = jnp.zeros_like(dcw_ref)
            dcb_ref[...] = jnp.zeros_like(dcb_ref)
            dg_ref[...] = jnp.zeros_like(dg_ref)

        dx2 = dx2_ref[...]
        dgd_ref[...] = _dot_nt(dx2, wd_ref[...])
        row8 = lax.broadcasted_iota(jnp.int32, (8, FFN_FS), 0)
        for s in range(DFF // FFN_FS):
            src = slice(s * FFN_FS, (s + 1) * FFN_FS)
            halves = (src, slice(DFF + s * FFN_FS, DFF + (s + 1) * FFN_FS))
            w = [[cw_ref[k:k + 1, ln] for k in range(3)] for ln in halves]
            nxt = [carry_ref[:, ln] for ln in halves]
            n7 = [pltpu.roll(v, 7, 0) for v in nxt]
            n6 = [pltpu.roll(v, 6, 0) for v in nxt]
            acc = [[jnp.zeros((8, FFN_FS), f32) for _ in range(4)] for _ in halves]
            for j in reversed(range(tm // 8)):
                rows = slice(8 * j, 8 * j + 8)
                dg = dgd_ref[rows, src]
                for hf, (ln, saved) in enumerate(zip(halves, (ga_ref, gb_ref))):
                    dc = dg * saved[rows, src]
                    r7, r6 = pltpu.roll(dc, 7, 0), pltpu.roll(dc, 6, 0)
                    up1 = jnp.where(row8 < 7, r7, n7[hf])
                    up2 = jnp.where(row8 < 6, r6, n6[hf])
                    av = a_ref[rows, ln]
                    acc[hf][0] = acc[hf][0] + up2 * av
                    acc[hf][1] = acc[hf][1] + up1 * av
                    acc[hf][2] = acc[hf][2] + dc * av
                    acc[hf][3] = acc[hf][3] + dc
                    dasc_ref[rows, ln] = w[hf][2] * dc + w[hf][1] * up1 + w[hf][0] * up2
                    n7[hf], n6[hf] = r7, r6
                    if j == 0:
                        carry_ref[:, ln] = dc
            for hf, ln in enumerate(halves):
                for k in range(3):
                    dcw_ref[k:k + 1, ln] += _colsum(acc[hf][k])
                dcb_ref[:, ln] += _colsum(acc[hf][3])
        da = dasc_ref[...].astype(bf16)
        da_ref[...] = da
        dh2 = lax.dot_general(da, wu_ref[...], (((1,), (1,)), ((), ())), preferred_element_type=f32)
        xh, r = _rms(x1_ref[...])
        dx1n, dg_rows = _rms_bwd(dh2, xh, r, g_ref[...])
        dg_ref[...] += _colsum(dg_rows)
        dx1_ref[...] = dx2 + dx1n

    rev = lambda i: nt - 1 - i
    row = lambda n: pl.BlockSpec((tm, n), lambda i: (rev(i), 0))
    return _pc_behind(
        body, exch, nt, name="ffn_bwd", grid=(nt,),
        in_specs=[row(D), row(2 * DFF), row(DFF), row(DFF), _whole((3, 2 * DFF)), _whole(w_down.shape),
                  _whole(w_up.shape), row(D), _whole((1, D))],
        out_specs=[row(2 * DFF), row(D), _acc((3, 2 * DFF)), _acc((1, 2 * DFF)), _acc((1, D))],
        out_shape=[jax.ShapeDtypeStruct((T, 2 * DFF), bf16), jax.ShapeDtypeStruct((T, D), f32),
                   jax.ShapeDtypeStruct((3, 2 * DFF), f32), jax.ShapeDtypeStruct((1, 2 * DFF), f32),
                   jax.ShapeDtypeStruct((1, D), f32)],
        scratch_shapes=[pltpu.VMEM((8, 2 * DFF), f32), pltpu.VMEM((tm, DFF), f32), pltpu.VMEM((tm, 2 * DFF), f32)],
        compiler_params=_params(), args=(dx2, a, g_a, g_b, conv_w, w_down, w_up, x1, g_ffn))


def _mix_bwd(dx1, hg_o, s5_o, gates, w_bhg, w_bs5, w_out, exch=None):
    T = dx1.shape[0]
    tm = 256

    def body(dx1_ref, hg_ref, s5_ref, gt_ref, wh_ref, ws_ref, wo_ref, dgt_ref, dhg_ref, ds5_ref, dyh_ref, dys_ref):
        dm = _dot_nt(dx1_ref[...], wo_ref[...])
        yh = jnp.dot(hg_ref[...], wh_ref[...], preferred_element_type=f32)
        ys = jnp.dot(s5_ref[...], ws_ref[...], preferred_element_type=f32)
        sh = _sig(gt_ref[:, 0:D])
        ss = _sig(gt_ref[:, D:2 * D])
        dgt_ref[:, 0:D] = dm * yh * sh * (1.0 - sh)
        dgt_ref[:, D:2 * D] = dm * ys * ss * (1.0 - ss)
        dyh = (dm * sh).astype(bf16)
        dys = (dm * ss).astype(bf16)
        dyh_ref[...] = dyh
        dys_ref[...] = dys
        dhg_ref[...] = lax.dot_general(dyh, wh_ref[...], (((1,), (1,)), ((), ())), preferred_element_type=f32)
        ds5_ref[...] = lax.dot_general(dys, ws_ref[...], (((1,), (1,)), ((), ())), preferred_element_type=f32)

    row = lambda n: pl.BlockSpec((tm, n), lambda i: (i, 0))
    return _pc_behind(
        body, exch, T // tm, name="mix_bwd", grid=(T // tm,),
        in_specs=[row(D), row(HW), row(SW), row(2 * D), _whole(w_bhg.shape), _whole(w_bs5.shape), _whole(w_out.shape)],
        out_specs=[row(2 * D), row(HW), row(SW), row(D), row(D)],
        out_shape=[jax.ShapeDtypeStruct((T, 2 * D), f32), jax.ShapeDtypeStruct((T, HW), f32), jax.ShapeDtypeStruct((T, SW), f32),
                   jax.ShapeDtypeStruct((T, D), bf16), jax.ShapeDtypeStruct((T, D), bf16)],
        compiler_params=_params(), args=(dx1, hg_o, s5_o, gates, w_bhg, w_bs5, w_out))


def _in_bwd(d_hg, d_u, d_gt, x, dx1, w, g):
    T = x.shape[0]
    tm = 256

    def body(dhg_ref, du_ref, dgt_ref, x_ref, dx1_ref, w_ref, g_ref, dx_ref, dg_ref):
        @pl.when(pl.program_id(0) == 0)
        def _():
            dg_ref[...] = jnp.zeros_like(dg_ref)

        dh = (_dot_nt(dhg_ref[...], w_ref[:, 0:4 * HW]) + _dot_nt(du_ref[...], w_ref[:, 4 * HW:4 * HW + SW])
              + _dot_nt(dgt_ref[...], w_ref[:, 4 * HW + SW:]))
        xh, r = _rms(x_ref[...])
        dxn, dg_rows = _rms_bwd(dh, xh, r, g_ref[...])
        dg_ref[...] += _colsum(dg_rows)
        dx_ref[...] = dx1_ref[...] + dxn

    row = lambda n: pl.BlockSpec((tm, n), lambda i: (i, 0))
    return _pc(
        body, name="in_bwd", grid=(T // tm,),
        in_specs=[row(4 * HW), row(SW), row(2 * D), row(D), row(D), _whole(w.shape), _whole((1, D))],
        out_specs=[row(D), _acc((1, D))],
        out_shape=[jax.ShapeDtypeStruct((T, D), f32), jax.ShapeDtypeStruct((1, D), f32)],
        compiler_params=_params(),
    )(d_hg, d_u, d_gt, x, dx1, w, g)


def _wgrad(name, a, b, nj=None, a_blk=None, a_idx=None, b_blk=None, b_idx=None):
    T = a.shape[0]
    tm = 512
    dense = nj is None
    if dense:
        K, N = a.shape[1], b.shape[1]
        a_blk, a_idx = K, (lambda j: 0)
        b_blk = N
        while K * b_blk * 4 > 6 * 1024 * 1024 and b_blk % 256 == 0:
            b_blk //= 2
        nj, b_idx = N // b_blk, (lambda j: j)

    def body(a_ref, b_ref, o_ref):
        @pl.when(pl.program_id(1) == 0)
        def _():
            o_ref[...] = jnp.zeros_like(o_ref)

        o_ref[0] += _dot_tn(a_ref[...], b_ref[...])

    out = _pc(
        body, name=name, grid=(nj, T // tm),
        in_specs=[pl.BlockSpec((tm, a_blk), lambda j, i: (i, a_idx(j))), pl.BlockSpec((tm, b_blk), lambda j, i: (i, b_idx(j)))],
        out_specs=pl.BlockSpec((1, a_blk, b_blk), lambda j, i: (j, 0, 0)),
        out_shape=jax.ShapeDtypeStruct((nj, a_blk, b_blk), f32),
        compiler_params=_params(2),
    )(a, b)
    if dense:
        return out[0] if nj == 1 else jnp.transpose(out, (1, 0, 2)).reshape(a.shape[1], b.shape[1])
    return out


ANY = pl.BlockSpec(memory_space=pl.ANY)


AG_SEMS = [pltpu.SemaphoreType.DMA((7,)), pltpu.SemaphoreType.DMA((7,)), pltpu.SemaphoreType.DMA]


def _ag_steps(x_ref, out_ref, send_sems, recv_sems, local_sem):
    x, y, c = lax.axis_index("x"), lax.axis_index("y"), lax.axis_index("c")
    me, sibling = (x, y, c), (x, y, 1 - c)
    chips = [(1 - x, y), (x, 1 - y), (1 - x, 1 - y)]

    def slot(px, py, pc):
        return out_ref.at[4 * px + 2 * py + pc]

    def copy(k, block, to, src=None):
        return pltpu.make_async_remote_copy(
            src_ref=slot(*block) if src is None else src, dst_ref=slot(*block),
            send_sem=send_sems.at[k], recv_sem=recv_sems.at[k], device_id=to, device_id_type=MESH)

    def mine():
        return pltpu.make_async_copy(x_ref, slot(*me), local_sem)

    def first():
        return [copy(0, me, sibling, src=x_ref)] + [copy(1 + j, me, (*chip, c), src=x_ref) for j, chip in enumerate(chips)]

    def passed():
        return [copy(4 + j, (*chip, c), sibling) for j, chip in enumerate(chips)]

    def start():
        mine().start()
        for cp in first():
            cp.start()

    def forward():
        for j, (chip, cp) in enumerate(zip(chips, passed())):
            copy(1 + j, (*chip, c), me).wait_recv()
            cp.start()

    def finish():
        copy(0, sibling, me).wait_recv()
        for j, chip in enumerate(chips):
            copy(4 + j, (*chip, 1 - c), me).wait_recv()
        for cp in first() + passed():
            cp.wait_send()
        mine().wait()

    return start, forward, finish


def _all_gather(name, shard):
    R, C = shard.shape

    def body(x_ref, out_ref, send_sems, recv_sems, local_sem):
        for phase in _ag_steps(x_ref, out_ref, send_sems, recv_sems, local_sem):
            phase()

    return _pc(
        body, name=name, in_specs=[ANY], out_specs=ANY,
        out_shape=jax.ShapeDtypeStruct((N_DEV, R, C), shard.dtype), scratch_shapes=list(AG_SEMS),
    )(shard)


class _Exchange:
    def __init__(self, kind, arrays):
        self.kind, self.arrays, self.n = kind, list(arrays), len(arrays)
        self.per = 4 if kind == "sibling" else 3
        tail = (lambda a: a.shape[2:]) if kind == "sibling" else (lambda a: a.shape[1:])
        self.out_shape = [jax.ShapeDtypeStruct((self.per, *tail(a)), a.dtype) for a in self.arrays]
        self.scratch = [pltpu.SemaphoreType.DMA((self.per * self.n,)), pltpu.SemaphoreType.DMA((self.per * self.n,))]

    def steps(self, in_refs, out_refs, send_sems, recv_sems):
        x, y, c = lax.axis_index("x"), lax.axis_index("y"), lax.axis_index("c")
        chips = [(1 - x, y), (x, 1 - y), (1 - x, 1 - y)]

        def copies():
            cps = []
            for i, (src, dst) in enumerate(zip(in_refs, out_refs)):
                for k in range(self.per):
                    if self.kind == "sibling":
                        s, to = src.at[k, 1 - c], (x, y, 1 - c)
                    else:
                        s, to = src.at[2 * chips[k][0] + chips[k][1]], (*chips[k], c)
                    cps.append(pltpu.make_async_remote_copy(
                        src_ref=s, dst_ref=dst.at[k], send_sem=send_sems.at[self.per * i + k],
                        recv_sem=recv_sems.at[self.per * i + k], device_id=to, device_id_type=MESH))
            return cps

        def start():
            for cp in copies():
                cp.start()

        def finish():
            for cp in copies():
                cp.wait()

        return start, finish


def _exchange_call(name, exch):
    n = exch.n

    def body(*refs):
        start, finish = exch.steps(refs[:n], refs[n:2 * n], *refs[2 * n:])
        start()
        finish()

    return _pc(body, name=name, in_specs=[ANY] * n, out_specs=[ANY] * n, out_shape=exch.out_shape,
               scratch_shapes=exch.scratch)(*exch.arrays)


def _pc_behind(body, exch, nsteps, *, in_specs, out_specs, out_shape, args, scratch_shapes=(), **kw):
    if exch is None:
        return _pc(body, in_specs=in_specs, out_specs=out_specs, out_shape=out_shape, scratch_shapes=list(scratch_shapes),
                   **kw)(*args), None
    n_in, n_out, n_scr, ne = len(in_specs), len(out_specs), len(scratch_shapes), exch.n

    def wrapped(*refs):
        ins, e_in = refs[:n_in], refs[n_in:n_in + ne]
        o0 = n_in + ne
        outs, e_out = refs[o0:o0 + n_out], refs[o0 + n_out:o0 + n_out + ne]
        s0 = o0 + n_out + ne
        scr, sems = refs[s0:s0 + n_scr], refs[s0 + n_scr:]
        start, finish = exch.steps(e_in, e_out, *sems)
        i = pl.program_id(0)
        pl.when(i == 0)(start)
        body(*ins, *outs, *scr)
        pl.when(i == nsteps - 1)(finish)

    res = _pc(wrapped, in_specs=list(in_specs) + [ANY] * ne, out_specs=list(out_specs) + [ANY] * ne,
              out_shape=list(out_shape) + exch.out_shape, scratch_shapes=list(scratch_shapes) + exch.scratch,
              **kw)(*args, *exch.arrays)
    return res[:n_out], res[n_out:]


def _add_halves(name, g4, got, ids):
    _, _, K, c = g4.shape

    def body(ids_ref, a_ref, b_ref, p16_ref, own_ref):
        s = a_ref[0, 0] + b_ref[0]
        p16_ref[0] = s.astype(bf16)

        @pl.when(pl.program_id(0) == ids_ref[1])
        def _():
            own_ref[...] = s

    return _pc(
        body, name=name,
        grid_spec=pltpu.PrefetchScalarGridSpec(
            num_scalar_prefetch=1, grid=(4,),
            in_specs=[pl.BlockSpec((1, 1, K, c), lambda k, ids: (k, ids[0], 0, 0)),
                      pl.BlockSpec((1, K, c), lambda k, ids: (k, 0, 0))],
            out_specs=[pl.BlockSpec((1, K, c), lambda k, ids: (k, 0, 0)), pl.BlockSpec((K, c), lambda k, ids: (0, 0))]),
        out_shape=[jax.ShapeDtypeStruct((4, K, c), bf16), jax.ShapeDtypeStruct((K, c), f32)],
        compiler_params=_params(),
    )(ids, g4, got)


def _row_tile(K):
    for cand in (256, 176, 128, 64):
        if K % cand == 0:
            return cand
    return K


def _adam_shard(name, own, got3, w, m, v):
    K, c = own.shape
    tr = _row_tile(K)

    def body(own_ref, got_ref, w_ref, m_ref, v_ref, g_ref, d_ref, m2_ref, v2_ref):
        g = own_ref[...] + got_ref[0].astype(f32) + got_ref[1].astype(f32) + got_ref[2].astype(f32)
        g_ref[0] = g
        delta, m2, v2 = _adam_math(g, w_ref[0], m_ref[0], v_ref[0])
        d_ref[0] = delta
        m2_ref[0] = m2
        v2_ref[0] = v2

    blk = pl.BlockSpec((1, tr, c), lambda i: (0, i, 0))
    out = jax.ShapeDtypeStruct((1, K, c), f32)
    return _pc(
        body, name=name, grid=(K // tr,),
        in_specs=[pl.BlockSpec((tr, c), lambda i: (i, 0)), pl.BlockSpec((3, tr, c), lambda i: (0, i, 0)), blk, blk, blk],
        out_specs=[blk, blk, blk, blk], out_shape=[out, out, out, out], compiler_params=_params(),
    )(own, got3, w, m, v)


def _allreduce_small(grads):
    n = len(grads)
    shapes = [g.shape for g in grads]

    def body(*refs):
        g_refs, outs, recv = refs[0:n], refs[n:2 * n], refs[2 * n:5 * n]
        send_sems, recv_sems = refs[5 * n:]
        x, y, c = lax.axis_index("x"), lax.axis_index("y"), lax.axis_index("c")
        peers = [(x, y, 1 - c), (1 - x, y, c), (x, 1 - y, c)]
        for i in range(n):
            outs[i][...] = g_refs[i][...]
        for s, peer in enumerate(peers):
            cps = [pltpu.make_async_remote_copy(src_ref=outs[i], dst_ref=recv[s * n + i], send_sem=send_sems.at[s * n + i],
                                                recv_sem=recv_sems.at[s * n + i], device_id=peer, device_id_type=MESH)
                   for i in range(n)]
            for cp in cps:
                cp.start()
            for cp in cps:
                cp.wait()
            for i in range(n):
                outs[i][...] = outs[i][...] + recv[s * n + i][...]

    return _pc(
        body, name="allreduce_small", grid=(1,), in_specs=[_whole(s) for s in shapes], out_specs=[_acc(s) for s in shapes],
        out_shape=[jax.ShapeDtypeStruct(s, f32) for s in shapes],
        scratch_shapes=[pltpu.VMEM(s, f32) for s in shapes] * 3
        + [pltpu.SemaphoreType.DMA((3 * n,)), pltpu.SemaphoreType.DMA((3 * n,))],
        compiler_params=_params(),
    )(*grads)


def _adam_small(grads, ws, ms, vs):
    n = len(grads)
    shapes = [g.shape for g in grads]

    def body(*refs):
        g_refs, w_refs, m_refs, v_refs = refs[0:n], refs[n:2 * n], refs[2 * n:3 * n], refs[3 * n:4 * n]
        outs = refs[4 * n:8 * n]
        for i in range(n):
            g = g_refs[i][...]
            delta, m2, v2 = _adam_math(g, w_refs[i][...], m_refs[i][...], v_refs[i][...])
            outs[i][...] = g
            outs[n + i][...] = delta
            outs[2 * n + i][...] = m2
            outs[3 * n + i][...] = v2

    return _pc(
        body, name="adam_small", grid=(1,), in_specs=[_whole(s) for s in shapes] * 4, out_specs=[_acc(s) for s in shapes] * 4,
        out_shape=[jax.ShapeDtypeStruct(s, f32) for s in shapes] * 4, compiler_params=_params(),
    )(*grads, *ws, *ms, *vs)


def _adam_math(g, w, m, v):
    m2 = ADAM_B1 * m + (1.0 - ADAM_B1) * g
    v2 = ADAM_B2 * v + (1.0 - ADAM_B2) * (g * g)
    m_hat = m2 / (1.0 - ADAM_B1 ** ADAM_STEP)
    v_hat = v2 / (1.0 - ADAM_B2 ** ADAM_STEP)
    delta = -ADAM_LR * (m_hat / (jnp.sqrt(v_hat) + ADAM_EPS) + ADAM_WD * w)
    return delta, m2, v2


def _pack(arrs, dtype, row_mult):
    rows = []
    for a in arrs:
        flat = a.reshape(-1).astype(dtype)
        pad = (-flat.shape[0]) % LANES
        if pad:
            flat = jnp.concatenate([flat, jnp.zeros((pad,), dtype)])
        rows.append(flat.reshape(-1, LANES))
    out = jnp.concatenate(rows, axis=0)
    pad = (-out.shape[0]) % row_mult
    if pad:
        out = jnp.concatenate([out, jnp.zeros((pad, LANES), dtype)], axis=0)
    return out


def _unpack(buf, shapes):
    lead = buf.shape[:-2]
    outs, r = [], 0
    for shp in shapes:
        n = math.prod(shp)
        nr = -(-n // LANES)
        piece = buf[..., r:r + nr, :].reshape(*lead, nr * LANES)[..., :n]
        outs.append(piece.reshape(*lead, *shp))
        r += nr
    return outs


def _to_slabs(full, axis):
    shp = full.shape
    n = shp[axis] // N_DEV
    return jnp.moveaxis(full.reshape(*shp[:axis], N_DEV, n, *shp[axis + 1:]), axis, 0)


def _from_slabs(slabs, axis):
    t = jnp.moveaxis(slabs, 0, axis)
    shp = t.shape
    return t.reshape(*shp[:axis], shp[axis] * shp[axis + 1], *shp[axis + 2:])


def _s5_discretise(lam_re, lam_im, log_dt, b_re, b_im):
    dt = jnp.exp(log_dt)[:, None]
    mag = jnp.exp(lam_re * dt)
    a_re = mag * jnp.cos(lam_im * dt)
    a_im = mag * jnp.sin(lam_im * dt)
    den = lam_re * lam_re + lam_im * lam_im
    coef_re = ((a_re - 1.0) * lam_re + a_im * lam_im) / den
    coef_im = (a_im * lam_re - (a_re - 1.0) * lam_im) / den
    bbar_re = coef_re[..., None] * b_re - coef_im[..., None] * b_im
    bbar_im = coef_re[..., None] * b_im + coef_im[..., None] * b_re
    return a_re, a_im, bbar_re, bbar_im


def _s5_operands(bbar_re, bbar_im, c_re, c_im):
    eye = jnp.eye(SG // NST, dtype=f32)

    def b_op(bb):
        return jnp.einsum("sgnq,gh->sgqhn", bb.reshape(NST, SG // NST, SN, SP), eye).reshape(NST, 128, STW)

    def c_op(cc):
        return jnp.einsum("sgpn,gh->shngp", cc.reshape(NST, SG // NST, SP, SN), eye).reshape(NST, STW, 128)

    bdb = jnp.concatenate([b_op(bbar_re), b_op(bbar_im)], axis=0)
    bdc = jnp.concatenate([c_op(c_re), c_op(-c_im)], axis=0)
    return bdb, bdc


_BIG = ["w_in", "s5_glu_w", "w_branch_hg", "w_branch_s5", "w_out", "w_up", "w_down", "w_ple_gate", "w_ple_proj", "conv_w"]
_BIG_AXIS = {"w_in": 1, "s5_glu_w": 0, "w_branch_hg": 1, "w_branch_s5": 1, "w_out": 0, "w_up": 1, "w_down": 0,
             "w_ple_gate": 0, "w_ple_proj": 1, "conv_w": 1}
_SMALL = ["norm_mix_g", "hg_lb_logits", "hg_norm_g", "s5_lambda_re", "s5_lambda_im", "s5_log_dt", "s5_b_re", "s5_b_im",
          "s5_c_re", "s5_c_im", "s5_d", "s5_glu_b", "norm_ffn_g", "conv_b", "norm_ple_g", "norm_final_g"]
_ORDER = ["norm_mix_g", "w_in", "hg_lb_logits", "hg_norm_g", "s5_lambda_re", "s5_lambda_im", "s5_log_dt", "s5_b_re",
          "s5_b_im", "s5_c_re", "s5_c_im", "s5_d", "s5_glu_w", "s5_glu_b", "w_branch_hg", "w_branch_s5", "w_out",
          "norm_ffn_g", "w_up", "conv_w", "conv_b", "w_down", "norm_ple_g", "w_ple_gate", "w_ple_proj", "norm_final_g"]


def kernel(x, p, norm_mix_g, w_in, hg_lb_logits, hg_norm_g, s5_lambda_re, s5_lambda_im, s5_log_dt, s5_b_re, s5_b_im, s5_c_re, s5_c_im, s5_d, s5_glu_w, s5_glu_b, w_branch_hg, w_branch_s5, w_out, norm_ffn_g, w_up, conv_w, conv_b, w_down, norm_ple_g, w_ple_gate, w_ple_proj, norm_final_g, loss_target, m_norm_mix_g, m_w_in, m_hg_lb_logits, m_hg_norm_g, m_s5_lambda_re, m_s5_lambda_im, m_s5_log_dt, m_s5_b_re, m_s5_b_im, m_s5_c_re, m_s5_c_im, m_s5_d, m_s5_glu_w, m_s5_glu_b, m_w_branch_hg, m_w_branch_s5, m_w_out, m_norm_ffn_g, m_w_up, m_conv_w, m_conv_b, m_w_down, m_norm_ple_g, m_w_ple_gate, m_w_ple_proj, m_norm_final_g, v_norm_mix_g, v_w_in, v_hg_lb_logits, v_hg_norm_g, v_s5_lambda_re, v_s5_lambda_im, v_s5_log_dt, v_s5_b_re, v_s5_b_im, v_s5_c_re, v_s5_c_im, v_s5_d, v_s5_glu_w, v_s5_glu_b, v_w_branch_hg, v_w_branch_s5, v_w_out, v_norm_ffn_g, v_w_up, v_conv_w, v_conv_b, v_w_down, v_norm_ple_g, v_w_ple_gate, v_w_ple_proj, v_norm_final_g):
    W = dict(norm_mix_g=norm_mix_g, w_in=w_in, hg_lb_logits=hg_lb_logits, hg_norm_g=hg_norm_g, s5_lambda_re=s5_lambda_re, s5_lambda_im=s5_lambda_im, s5_log_dt=s5_log_dt, s5_b_re=s5_b_re, s5_b_im=s5_b_im, s5_c_re=s5_c_re, s5_c_im=s5_c_im, s5_d=s5_d, s5_glu_w=s5_glu_w, s5_glu_b=s5_glu_b, w_branch_hg=w_branch_hg, w_branch_s5=w_branch_s5, w_out=w_out, norm_ffn_g=norm_ffn_g, w_up=w_up, conv_w=conv_w, conv_b=conv_b, w_down=w_down, norm_ple_g=norm_ple_g, w_ple_gate=w_ple_gate, w_ple_proj=w_ple_proj, norm_final_g=norm_final_g)
    M = dict(norm_mix_g=m_norm_mix_g, w_in=m_w_in, hg_lb_logits=m_hg_lb_logits, hg_norm_g=m_hg_norm_g, s5_lambda_re=m_s5_lambda_re, s5_lambda_im=m_s5_lambda_im, s5_log_dt=m_s5_log_dt, s5_b_re=m_s5_b_re, s5_b_im=m_s5_b_im, s5_c_re=m_s5_c_re, s5_c_im=m_s5_c_im, s5_d=m_s5_d, s5_glu_w=m_s5_glu_w, s5_glu_b=m_s5_glu_b, w_branch_hg=m_w_branch_hg, w_branch_s5=m_w_branch_s5, w_out=m_w_out, norm_ffn_g=m_norm_ffn_g, w_up=m_w_up, conv_w=m_conv_w, conv_b=m_conv_b, w_down=m_w_down, norm_ple_g=m_norm_ple_g, w_ple_gate=m_w_ple_gate, w_ple_proj=m_w_ple_proj, norm_final_g=m_norm_final_g)
    V = dict(norm_mix_g=v_norm_mix_g, w_in=v_w_in, hg_lb_logits=v_hg_lb_logits, hg_norm_g=v_hg_norm_g, s5_lambda_re=v_s5_lambda_re, s5_lambda_im=v_s5_lambda_im, s5_log_dt=v_s5_log_dt, s5_b_re=v_s5_b_re, s5_b_im=v_s5_b_im, s5_c_re=v_s5_c_re, s5_c_im=v_s5_c_im, s5_d=v_s5_d, s5_glu_w=v_s5_glu_w, s5_glu_b=v_s5_glu_b, w_branch_hg=v_w_branch_hg, w_branch_s5=v_w_branch_s5, w_out=v_w_out, norm_ffn_g=v_norm_ffn_g, w_up=v_w_up, conv_w=v_conv_w, conv_b=v_conv_b, w_down=v_w_down, norm_ple_g=v_norm_ple_g, w_ple_gate=v_w_ple_gate, w_ple_proj=v_w_ple_proj, norm_final_g=v_norm_final_g)

    shard2 = {n: W[n][0] for n in _BIG}
    conv_bits = lax.bitcast_convert_type(shard2["conv_w"], bf16)
    groups = [["w_in", "s5_glu_w"], ["w_branch_hg", "w_branch_s5", "w_out", "w_ple_gate", "w_ple_proj", "w_down"], ["w_up"]]
    packs = [_pack([shard2[n] for n in grp] + ([conv_bits] if k == 0 else []), bf16, 16) for k, grp in enumerate(groups)]
    full = {}

    def take(k, gathered):
        pieces = _unpack(gathered, [shard2[n].shape for n in groups[k]] + ([conv_bits.shape] if k == 0 else []))
        full.update({n: _from_slabs(pc, _BIG_AXIS[n]) for n, pc in zip(groups[k], pieces)})
        return pieces

    conv_w_full = _from_slabs(lax.bitcast_convert_type(take(0, _all_gather("ag_weights", packs[0]))[-1], f32), 1)

    xt = x[0]
    pt = p[0, 0]
    tgt = loss_target[0]
    T = xt.shape[0]
    lam_re, lam_im, log_dt = s5_lambda_re[0], s5_lambda_im[0], s5_log_dt[0]
    b_re, b_im, c_re, c_im = s5_b_re[0], s5_b_im[0], s5_c_re[0], s5_c_im[0]

    def s5_prep(lam_re, lam_im, log_dt, b_re, b_im, c_re, c_im):
        a_re, a_im, bbar_re, bbar_im = _s5_discretise(lam_re, lam_im, log_dt, b_re, b_im)
        bdb, bdc = _s5_operands(bbar_re, bbar_im, c_re, c_im)
        return a_re, a_im, bdb, bdc

    (a_re, a_im, bdb, bdc), s5_prep_vjp = jax.vjp(s5_prep, lam_re, lam_im, log_dt, b_re, b_im, c_re, c_im)
    a_row = jnp.concatenate([a_re.reshape(1, SL), a_im.reshape(1, SL)], axis=1)
    bdb_b, bdc_b = bdb.astype(bf16), bdc.astype(bf16)

    h1, proj_hg, u_raw, gates, gathered1 = _in_proj(xt, norm_mix_g, full["w_in"], packs[1])
    take(1, gathered1)
    ng4 = jnp.tile(hg_norm_g, (1, NH))
    hg_o, sprev = _hgrn_fwd(proj_hg, hg_lb_logits, ng4)
    x_st, y_s5, g_s5, s5_o, gathered2 = _s5_fwd(u_raw, a_row, bdb_b, bdc_b, s5_d, full["s5_glu_w"], s5_glu_b, packs[2])
    take(2, gathered2)
    x1, merged, h2, a_up = _mix_up(xt, hg_o, s5_o, gates, full["w_branch_hg"], full["w_branch_s5"], full["w_out"],
                                   norm_ffn_g, full["w_up"])
    (dx2, gated, g_a, g_b, h3, dz_ple, dpp, loss_part, d_norm_final, d_norm_ple) = _ffn_tail(
        a_up, conv_w_full, conv_b, full["w_down"], x1, pt, norm_ple_g, full["w_ple_gate"], full["w_ple_proj"],
        norm_final_g.reshape(1, D), tgt)

    ids = jnp.stack([lax.axis_index("c"), 2 * lax.axis_index("x") + lax.axis_index("y")]).astype(jnp.int32)
    gw, own_sum, got3 = {}, {}, {}

    def slabs(names):
        return [_to_slabs(gw[n], _BIG_AXIS[n]).reshape(4, 2, *shard2[n].shape) for n in names]

    def add_pairs(names, g4, got):
        sums = [_add_halves("rs_add_" + n, g, r, ids) for n, g, r in zip(names, g4, got)]
        own_sum.update({n: own for n, (_, own) in zip(names, sums)})
        return [p16 for p16, _ in sums]

    grp_a = ["w_down", "w_ple_gate", "w_ple_proj"]
    gw["w_down"] = _wgrad("wg_down", gated, dx2)
    gw["w_ple_gate"] = _wgrad("wg_pg", h3, dz_ple)
    gw["w_ple_proj"] = _wgrad("wg_pp", pt, dpp)
    g4_a = slabs(grp_a)
    (da_up, dx1, d_conv_w, d_conv_b, d_norm_ffn), got_a = _ffn_bwd(
        dx2, a_up, g_a, g_b, conv_w_full, full["w_down"], full["w_up"], x1, norm_ffn_g, exch=_Exchange("sibling", g4_a))
    p16_a = add_pairs(grp_a, g4_a, got_a)
    (d_gates, d_hg_o, d_s5_o, dyh, dys), got3_a = _mix_bwd(
        dx1, hg_o, s5_o, gates, full["w_branch_hg"], full["w_branch_s5"], full["w_out"], exch=_Exchange("chips", p16_a))
    got3.update(zip(grp_a, got3_a))

    grp_b = ["w_up", "w_out", "w_branch_hg", "w_branch_s5", "conv_w"]
    gw["w_up"] = _wgrad("wg_up", h2, da_up)
    gw["w_out"] = _wgrad("wg_out", merged, dx1)
    gw["w_branch_hg"] = _wgrad("wg_bhg", hg_o, dyh)
    gw["w_branch_s5"] = _wgrad("wg_bs5", s5_o, dys)
    gw["conv_w"] = d_conv_w
    g4_b = slabs(grp_b)
    (d_proj_hg, d_lb, d_hg_norm), got_b = _hgrn_bwd(proj_hg, hg_lb_logits, ng4, sprev, d_hg_o,
                                                     exch=_Exchange("sibling", g4_b))
    p16_b = add_pairs(grp_b, g4_b, got_b)
    (d_u, dz_glu, d_a_re, d_a_im, d_s5_d, d_glu_b, d_bdb, d_bdc), got3_b = _s5_bwd(
        d_s5_o, y_s5, u_raw, x_st, a_row, bdb_b, bdc_b, s5_d, full["s5_glu_w"], s5_glu_b, exch=_Exchange("chips", p16_b))
    got3.update(zip(grp_b, got3_b))
    grad_x, d_norm_mix = _in_bwd(d_proj_hg, d_u, d_gates, xt, dx1, full["w_in"], norm_mix_g)

    grp_c = ["w_in", "s5_glu_w"]
    gw["w_in"] = jnp.concatenate([_wgrad("wg_in_hg", h1, d_proj_hg), _wgrad("wg_in_u", h1, d_u),
                                  _wgrad("wg_in_gates", h1, d_gates)], axis=1)
    gw["s5_glu_w"] = _wgrad("wg_glu", g_s5, dz_glu)
    g4_c = slabs(grp_c)
    p16_c = add_pairs(grp_c, g4_c, _exchange_call("rs_sibling", _Exchange("sibling", g4_c)))
    got3.update(zip(grp_c, _exchange_call("rs_chips", _Exchange("chips", p16_c))))

    (d_lam_re, d_lam_im, d_log_dt, d_b_re, d_b_im, d_c_re, d_c_im) = s5_prep_vjp(
        (d_a_re.reshape(SG, SN), d_a_im.reshape(SG, SN), d_bdb, d_bdc))
    sm = jax.nn.softmax(hg_lb_logits, axis=0)
    d_l0 = d_lb[0] * sm[0] * sm[1]
    d_logits = jnp.stack([d_l0, -d_l0], axis=0)

    gs = {"norm_mix_g": d_norm_mix, "hg_lb_logits": d_logits, "hg_norm_g": d_hg_norm, "s5_lambda_re": d_lam_re,
          "s5_lambda_im": d_lam_im, "s5_log_dt": d_log_dt, "s5_b_re": d_b_re, "s5_b_im": d_b_im, "s5_c_re": d_c_re,
          "s5_c_im": d_c_im, "s5_d": d_s5_d, "s5_glu_b": d_glu_b, "norm_ffn_g": d_norm_ffn, "conv_b": d_conv_b,
          "norm_ple_g": d_norm_ple, "norm_final_g": d_norm_final}

    big_out = [_adam_shard("adam_" + n, own_sum[n], got3[n], W[n], M[n], V[n]) for n in _BIG]

    two_d = lambda a: a.reshape(1, -1) if a.ndim == 1 else a
    dense = lambda a: a.reshape(SG, -1) if a.ndim == 4 else two_d(a)
    g_sum = _allreduce_small([dense(gs[n].reshape(W[n].shape)) for n in _SMALL])
    small_out = _adam_small([g.reshape(two_d(W[n]).shape) for g, n in zip(g_sum, _SMALL)], [two_d(W[n]) for n in _SMALL],
                            [two_d(M[n]) for n in _SMALL], [two_d(V[n]) for n in _SMALL])

    res = {}
    for k in range(4):
        d = {n: big_out[i][k] for i, n in enumerate(_BIG)}
        d.update({n: small_out[k * len(_SMALL) + i].reshape(W[n].shape) for i, n in enumerate(_SMALL)})
        res[k] = d
    loss = lax.psum(loss_part[0, 0], ("x", "y", "c"))
    return (loss, grad_x[None], *[res[0][n] for n in _ORDER], *[res[1][n] for n in _ORDER],
            *[res[2][n] for n in _ORDER], *[res[3][n] for n in _ORDER])
```

```python
import functools
import math

import jax
import jax.numpy as jnp
from jax import lax
from jax.experimental import pallas as pl
from jax.experimental.pallas import tpu as pltpu

f32 = jnp.float32
bf16 = jnp.bfloat16
MESH = pl.DeviceIdType.MESH

N_DEV = 8
D = 1024
HW = 512
HD = 128
NH = 4
CH = 64
SW = 512
SG = 32
SP = 16
SN = 64
SL = SG * SN
NST = 4
STW = SL // NST
DFF = 2816
PLE = 256
EPS = 1e-6
LANES = 1024
VMEM_LIMIT = 56 * 1024 * 1024

ADAM_LR, ADAM_B1, ADAM_B2, ADAM_EPS, ADAM_WD, ADAM_STEP = 0.001, 0.9, 0.999, 1e-08, 0.01, 10


def _pc(body, **kw):
    return pl.pallas_call(body, **kw)


def _params(n_axes=1, **kw):
    return pltpu.CompilerParams(dimension_semantics=("arbitrary",) * n_axes, vmem_limit_bytes=VMEM_LIMIT, **kw)


def _whole(shape):
    nd = len(shape)
    return pl.BlockSpec(shape, lambda *_: (0,) * nd, pipeline_mode=pl.Buffered(1))


def _acc(shape):
    nd = len(shape)
    return pl.BlockSpec(shape, lambda *_: (0,) * nd)


def _dot(a, b):
    return jnp.dot(a.astype(bf16), b.astype(bf16), preferred_element_type=f32)


def _dot_nt(a, b):
    return lax.dot_general(a.astype(bf16), b.astype(bf16), (((1,), (1,)), ((), ())), preferred_element_type=f32)


def _dot_tn(a, b):
    return lax.dot_general(a.astype(bf16), b.astype(bf16), (((0,), (0,)), ((), ())), preferred_element_type=f32)


def _sig(x):
    return jax.nn.sigmoid(x)


def _dsilu(z, s):
    return s * (1.0 + z * (1.0 - s))


_GC = math.sqrt(2.0 / math.pi)


def _gelu_and_grad(y):
    t = jnp.tanh(_GC * (y + 0.044715 * y * y * y))
    g = 0.5 * y * (1.0 + t)
    dg = 0.5 * (1.0 + t) + 0.5 * y * (1.0 - t * t) * _GC * (1.0 + 3.0 * 0.044715 * y * y)
    return g, dg


def _rms(x):
    r = lax.rsqrt(jnp.mean(x * x, axis=-1, keepdims=True) + EPS)
    return x * r, r


def _rms_bwd(dy, xh, r, g):
    dxh = dy * g
    dx = r * (dxh - xh * jnp.mean(dxh * xh, axis=-1, keepdims=True))
    return dx, dy * xh


def _colsum(x):
    return jnp.sum(x, axis=0, keepdims=True)


def _in_proj(x, g, w, ag_shard):
    T = x.shape[0]
    tm = 256
    nt = T // tm

    def body(x_ref, g_ref, w_ref, ag_ref, h_ref, hg_ref, u_ref, gt_ref, ago_ref, send_sems, recv_sems, local_sem):
        i = pl.program_id(0)
        start, forward, finish = _ag_steps(ag_ref, ago_ref, send_sems, recv_sems, local_sem)
        pl.when(i == 0)(start)
        xh, _ = _rms(x_ref[...])
        h = (xh * g_ref[...]).astype(bf16)
        h_ref[...] = h
        hg_ref[...] = jnp.dot(h, w_ref[:, 0:4 * HW], preferred_element_type=f32)
        u_ref[...] = jnp.dot(h, w_ref[:, 4 * HW:4 * HW + SW], preferred_element_type=f32)
        gt_ref[...] = jnp.dot(h, w_ref[:, 4 * HW + SW:], preferred_element_type=f32)
        pl.when(i == nt // 2)(forward)
        pl.when(i == nt - 1)(finish)

    row = lambda n: pl.BlockSpec((tm, n), lambda i: (i, 0))
    return _pc(
        body, name="in_proj", grid=(nt,),
        in_specs=[row(D), _whole((1, D)), _whole(w.shape), ANY],
        out_specs=[row(D), row(4 * HW), row(SW), row(2 * D), ANY],
        out_shape=[jax.ShapeDtypeStruct((T, D), bf16), jax.ShapeDtypeStruct((T, 4 * HW), f32),
                   jax.ShapeDtypeStruct((T, SW), f32), jax.ShapeDtypeStruct((T, 2 * D), f32),
                   jax.ShapeDtypeStruct((N_DEV, *ag_shard.shape), ag_shard.dtype)],
        scratch_shapes=list(AG_SEMS),
        compiler_params=_params(),
    )(x, g, w, ag_shard)


HG_NC = 4


def _tri_matmul(tri, x):
    hi = x.astype(bf16)
    r1 = x - hi.astype(f32)
    mid = r1.astype(bf16)
    lo = (r1 - mid.astype(f32)).astype(bf16)
    n = x.shape[1]
    out = jnp.dot(tri.astype(bf16), jnp.concatenate([hi, mid, lo], axis=1), preferred_element_type=f32)
    return out[:, 0:n] + out[:, n:2 * n] + out[:, 2 * n:3 * n]


HG_TM = HG_NC * CH


def _chunk_tri(upper):
    r_i = lax.broadcasted_iota(jnp.int32, (HG_TM, HG_TM), 0)
    c_i = lax.broadcasted_iota(jnp.int32, (HG_TM, HG_TM), 1)
    same = (r_i // CH) == (c_i // CH)
    return same & ((c_i >= r_i) if upper else (r_i >= c_i))


def _heads(x3):
    n = x3.shape[2] // NH
    return jnp.concatenate([x3[:, :, h * n:(h + 1) * n] for h in range(NH)], axis=0)


def _unheads(xb):
    return jnp.concatenate([xb[h * HG_NC:(h + 1) * HG_NC] for h in range(NH)], axis=2)


def _bdot(a, b, ca, cb):
    return lax.dot_general(a.astype(bf16), b.astype(bf16), (((ca,), (cb,)), ((0,), (0,))), preferred_element_type=f32)


def _hgrn_gates(lg, qr, fr):
    mx = jnp.max(lg, axis=0, keepdims=True)
    e = jnp.exp(lg - mx)
    lb = e[0:1, :] / (e[0:1, :] + e[1:2, :])
    sig = _sig(fr)
    f = lb + (1.0 - lb) * sig
    k = 1.0 - f
    b = _tri_matmul(_chunk_tri(False), jnp.log(f).reshape(HG_TM, HW)).reshape(HG_NC, CH, HW)
    bref = b[:, CH // 2:CH // 2 + 1, :]
    blast = b[:, CH - 1:CH, :]
    sq = _sig(qr)
    q = qr * sq
    e1 = jnp.exp(b - bref)
    e2 = jnp.exp(bref - b)
    e3 = jnp.exp(blast - b)
    e4 = jnp.exp(b)
    r_i = lax.broadcasted_iota(jnp.int32, (CH, CH), 0)
    c_i = lax.broadcasted_iota(jnp.int32, (CH, CH), 1)
    return dict(lb=lb, qr=qr, sq=sq, sig=sig, f=f, k=k, tril=(r_i >= c_i), e1=e1, e2=e2, e3=e3, e4=e4,
                qs=q * e1, ks=k * e2, kl=k * e3, qb=q * e4, dec=jnp.exp(blast))


def _hgrn_fwd(proj_hg, logits, ng4):
    T = proj_hg.shape[0]
    nch = T // CH
    tm = HG_NC * CH

    def body(q_ref, f_ref, i_ref, og_ref, lg_ref, ng_ref, out_ref, sprev_ref, st_ref):
        @pl.when(pl.program_id(0) == 0)
        def _():
            st_ref[...] = jnp.zeros_like(st_ref)

        three = lambda ref: ref[...].reshape(HG_NC, CH, HW)
        c = _hgrn_gates(lg_ref[...], three(q_ref), three(f_ref))
        qs, ks, kl, qb, dec = (_heads(c[n]) for n in ("qs", "ks", "kl", "qb", "dec"))
        vb = _heads(three(i_ref))
        p = jnp.where(c["tril"], _bdot(qs, ks, 2, 2), 0.0)
        ut = _bdot(vb, kl, 1, 1)
        sts = []
        for h in range(NH):
            st = st_ref[h]
            for ci in range(HG_NC):
                sts.append(st)
                sprev_ref[ci, h] = st
                st = dec[h * HG_NC + ci] * st + ut[h * HG_NC + ci]
            st_ref[h] = st
        o = _bdot(p, vb, 2, 1) + _bdot(qb, jnp.stack(sts), 2, 2)
        og = og_ref[...]
        out_ref[...] = (_unheads(_rms(o)[0]).reshape(HG_TM, HW) * ng_ref[...] * (og * _sig(og))).astype(bf16)

    col = lambda j: pl.BlockSpec((tm, HW), lambda n, j=j: (n, j))
    return _pc(
        body, name="hgrn_fwd", grid=(nch // HG_NC,),
        in_specs=[col(0), col(1), col(2), col(3), _whole((2, HW)), _whole((1, HW))],
        out_specs=[pl.BlockSpec((tm, HW), lambda n: (n, 0)),
                   pl.BlockSpec((HG_NC, NH, HD, HD), lambda n: (n, 0, 0, 0))],
        out_shape=[jax.ShapeDtypeStruct((T, HW), bf16), jax.ShapeDtypeStruct((nch, NH, HD, HD), f32)],
        scratch_shapes=[pltpu.VMEM((NH, HD, HD), f32)],
        compiler_params=_params(),
    )(proj_hg, proj_hg, proj_hg, proj_hg, logits, ng4)


def _hgrn_bwd(proj_hg, logits, ng4, sprev, d_out, exch=None):
    T = proj_hg.shape[0]
    nch = T // CH
    tm = HG_NC * CH
    nst = nch // HG_NC

    def body(q_ref, f_ref, i_ref, og_ref, lg_ref, ng_ref, sp_ref, do_ref, dp_ref, dlb_ref, dng_ref, gt_ref):
        @pl.when(pl.program_id(0) == 0)
        def _():
            gt_ref[...] = jnp.zeros_like(gt_ref)
            dlb_ref[...] = jnp.zeros_like(dlb_ref)
            dng_ref[...] = jnp.zeros_like(dng_ref)

        three = lambda x: x.reshape(HG_NC, CH, HW)
        flat = lambda x: x.reshape(HG_TM, HW)
        c = _hgrn_gates(lg_ref[...], three(q_ref[...]), three(f_ref[...]))
        tril = c["tril"]
        ng = ng_ref[:, 0:HD]
        og = og_ref[...]
        sog = _sig(og)
        d_gated = do_ref[...]
        qs, ks, kl, qb, dec = (_heads(c[n]) for n in ("qs", "ks", "kl", "qb", "dec"))
        vb = _heads(three(i_ref[...]))
        spb = jnp.stack([sp_ref[ci, h] for h in range(NH) for ci in range(HG_NC)])
        p = jnp.where(tril, _bdot(qs, ks, 2, 2), 0.0)
        o = _bdot(p, vb, 2, 1) + _bdot(qb, spb, 2, 2)
        oh, r = _rms(o)
        d_o, dng_rows = _rms_bwd(_heads(three(d_gated * (og * sog))), oh, r, ng)
        dng_ref[...] += _colsum(jnp.sum(dng_rows, axis=0))
        dp = jnp.where(tril, _bdot(d_o, vb, 2, 2), 0.0)
        dst = _bdot(d_o, qb, 1, 1)
        gts = [None] * (NH * HG_NC)
        for h in range(NH):
            gt = gt_ref[h]
            for ci in reversed(range(HG_NC)):
                gts[h * HG_NC + ci] = gt
                gt = dst[h * HG_NC + ci] + dec[h * HG_NC + ci] * gt
            gt_ref[h] = gt
        gtb = jnp.stack(gts)
        dqs = _unheads(_bdot(dp, ks, 2, 1))
        dks = _unheads(_bdot(dp, qs, 1, 1))
        dkl = _unheads(_bdot(vb, gtb, 2, 1))
        dqb = _unheads(_bdot(d_o, spb, 2, 1))
        dv = _unheads(_bdot(p, d_o, 1, 1) + _bdot(kl, gtb, 2, 2))
        ddec = _unheads(jnp.sum(gtb * spb, axis=1, keepdims=True))
        dq = dqs * c["e1"] + dqb * c["e4"]
        dk = dks * c["e2"] + dkl * c["e3"]
        t_qs = dqs * c["qs"]
        t_ks = dks * c["ks"]
        t_kl = dkl * c["kl"]
        db = t_qs - t_ks - t_kl + dqb * c["qb"]
        dbref = jnp.sum(t_ks - t_qs, axis=1, keepdims=True)
        dblast = jnp.sum(t_kl, axis=1, keepdims=True) + ddec * c["dec"]
        row = lax.broadcasted_iota(jnp.int32, (HG_NC, CH, HW), 1)
        db = db + jnp.where(row == CH // 2, dbref, 0.0) + jnp.where(row == CH - 1, dblast, 0.0)
        df = three(_tri_matmul(_chunk_tri(True), flat(db))) / c["f"] - dk
        sig = c["sig"]
        dlb_ref[...] += _colsum(jnp.sum(df * (1.0 - sig), axis=0))
        dp_ref[:, 0:HW] = flat(dq * _dsilu(c["qr"], c["sq"]))
        dp_ref[:, HW:2 * HW] = flat(df * (1.0 - c["lb"]) * sig * (1.0 - sig))
        dp_ref[:, 2 * HW:3 * HW] = flat(dv)
        dp_ref[:, 3 * HW:4 * HW] = d_gated * flat(_unheads(oh * ng)) * _dsilu(og, sog)

    rev = lambda n: nst - 1 - n
    col = lambda j: pl.BlockSpec((tm, HW), lambda n, j=j: (rev(n), j))
    return _pc_behind(
        body, exch, nst, name="hgrn_bwd", grid=(nst,),
        in_specs=[col(0), col(1), col(2), col(3), _whole((2, HW)), _whole((1, HW)),
                  pl.BlockSpec((HG_NC, NH, HD, HD), lambda n: (rev(n), 0, 0, 0)),
                  pl.BlockSpec((tm, HW), lambda n: (rev(n), 0))],
        out_specs=[pl.BlockSpec((tm, 4 * HW), lambda n: (rev(n), 0)), _acc((1, HW)), _acc((1, HD))],
        out_shape=[jax.ShapeDtypeStruct((T, 4 * HW), f32), jax.ShapeDtypeStruct((1, HW), f32),
                   jax.ShapeDtypeStruct((1, HD), f32)],
        scratch_shapes=[pltpu.VMEM((NH, HD, HD), f32)],
        compiler_params=_params(), args=(proj_hg, proj_hg, proj_hg, proj_hg, logits, ng4, sprev, d_out))


S5_TM = 256
S5_SEG = 8
S5_STEPS = S5_TM // S5_SEG
NLT = SL // 128


def _s5_tables(a_ref, pw_ref, pseg_ref, descending):
    re, im = slice(0, SL), slice(SL, 2 * SL)

    def cmul(ar, ai, br, bi):
        return ar * br - ai * bi, ar * bi + ai * br

    pw_ref[0:1, :] = a_ref[...]
    m = 1
    while m < S5_STEPS:
        pr, pi = cmul(pw_ref[0:m, re], pw_ref[0:m, im], pw_ref[m - 1:m, re], pw_ref[m - 1:m, im])
        pw_ref[m:2 * m, re] = pr
        pw_ref[m:2 * m, im] = pi
        m *= 2
    base = S5_STEPS - 1
    if descending:
        pseg_ref[7:8, :] = pw_ref[base:base + 1, :]
        m = 1
        while m < 8:
            pr, pi = cmul(pseg_ref[8 - m:8, re], pseg_ref[8 - m:8, im], pseg_ref[8 - m:9 - m, re], pseg_ref[8 - m:9 - m, im])
            pseg_ref[8 - 2 * m:8 - m, re] = pr
            pseg_ref[8 - 2 * m:8 - m, im] = pi
            m *= 2
    else:
        pseg_ref[0:1, :] = pw_ref[base:base + 1, :]
        m = 1
        while m < 8:
            pr, pi = cmul(pseg_ref[0:m, re], pseg_ref[0:m, im], pseg_ref[m - 1:m, re], pseg_ref[m - 1:m, im])
            pseg_ref[m:2 * m, re] = pr
            pseg_ref[m:2 * m, im] = pi
            m *= 2


def _seg_rows(j):
    return pl.ds(j * S5_SEG, S5_SEG)


def _seg_perm(transpose=False):
    r_i = lax.broadcasted_iota(jnp.int32, (S5_TM, S5_TM), 0)
    c_i = lax.broadcasted_iota(jnp.int32, (S5_TM, S5_TM), 1)
    if transpose:
        r_i, c_i = c_i, r_i
    return c_i == S5_STEPS * (r_i % S5_SEG) + r_i // S5_SEG


def _scan_fwd(x3_ref, pw_ref, pseg_ref, carry_ref):
    row8 = lax.broadcasted_iota(jnp.int32, (S5_SEG, 128), 0)
    for lt in range(NLT):
        kr, ki = lt, NLT + lt
        lr, li = slice(lt * 128, (lt + 1) * 128), slice(SL + lt * 128, SL + (lt + 1) * 128)
        ar, ai = pw_ref[0:1, lr], pw_ref[0:1, li]
        sr = jnp.zeros((S5_SEG, 128), f32)
        si = jnp.zeros((S5_SEG, 128), f32)
        for j in range(S5_STEPS):
            sr, si = ar * sr - ai * si + x3_ref[kr, _seg_rows(j), :], ar * si + ai * sr + x3_ref[ki, _seg_rows(j), :]
            x3_ref[kr, _seg_rows(j), :] = sr
            x3_ref[ki, _seg_rows(j), :] = si
        for d in (1, 2, 4):
            pr, pi = pseg_ref[d - 1:d, lr], pseg_ref[d - 1:d, li]
            tr, ti = pltpu.roll(sr, d, 0), pltpu.roll(si, d, 0)
            m = row8 >= d
            sr, si = sr + jnp.where(m, pr * tr - pi * ti, 0.0), si + jnp.where(m, pr * ti + pi * tr, 0.0)
        c0r, c0i = carry_ref[7:8, lr], carry_ref[7:8, li]
        qr, qi = pseg_ref[:, lr], pseg_ref[:, li]
        sr, si = sr + qr * c0r - qi * c0i, si + qr * c0i + qi * c0r
        carry_ref[:, lr] = sr
        carry_ref[:, li] = si
        cr = jnp.where(row8 == 0, c0r, pltpu.roll(sr, 1, 0))
        ci = jnp.where(row8 == 0, c0i, pltpu.roll(si, 1, 0))
        for j in range(S5_STEPS):
            pr, pi = pw_ref[j:j + 1, lr], pw_ref[j:j + 1, li]
            x3_ref[kr, _seg_rows(j), :] = x3_ref[kr, _seg_rows(j), :] + pr * cr - pi * ci
            x3_ref[ki, _seg_rows(j), :] = x3_ref[ki, _seg_rows(j), :] + pr * ci + pi * cr


def _scan_bwd(g3_ref, x3_ref, xh_ref, first, pw_ref, pseg_ref, carry_ref, dar_ref, dai_ref):
    row8 = lax.broadcasted_iota(jnp.int32, (S5_SEG, 128), 0)
    for lt in range(NLT):
        kr, ki = lt, NLT + lt
        lr, li = slice(lt * 128, (lt + 1) * 128), slice(SL + lt * 128, SL + (lt + 1) * 128)
        ar, ai = pw_ref[0:1, lr], pw_ref[0:1, li]
        sr = jnp.zeros((S5_SEG, 128), f32)
        si = jnp.zeros((S5_SEG, 128), f32)
        for j in reversed(range(S5_STEPS)):
            sr, si = ar * sr + ai * si + g3_ref[kr, _seg_rows(j), :], ar * si - ai * sr + g3_ref[ki, _seg_rows(j), :]
            g3_ref[kr, _seg_rows(j), :] = sr
            g3_ref[ki, _seg_rows(j), :] = si
        for d in (1, 2, 4):
            pr, pi = pseg_ref[8 - d:9 - d, lr], pseg_ref[8 - d:9 - d, li]
            tr, ti = pltpu.roll(sr, 8 - d, 0), pltpu.roll(si, 8 - d, 0)
            m = row8 < 8 - d
            sr, si = sr + jnp.where(m, pr * tr + pi * ti, 0.0), si + jnp.where(m, pr * ti - pi * tr, 0.0)
        c0r, c0i = carry_ref[0:1, lr], carry_ref[0:1, li]
        qr, qi = pseg_ref[:, lr], pseg_ref[:, li]
        sr, si = sr + qr * c0r + qi * c0i, si + qr * c0i - qi * c0r
        carry_ref[:, lr] = sr
        carry_ref[:, li] = si
        cr = jnp.where(row8 == 7, c0r, pltpu.roll(sr, 7, 0))
        ci = jnp.where(row8 == 7, c0i, pltpu.roll(si, 7, 0))
        hr = jnp.where(first, 0.0, xh_ref[kr, 7:8, :])
        hi = jnp.where(first, 0.0, xh_ref[ki, 7:8, :])
        acc_r = jnp.zeros((S5_SEG, 128), f32)
        acc_i = jnp.zeros((S5_SEG, 128), f32)
        for j in range(S5_STEPS):
            pr, pi = pw_ref[S5_STEPS - 1 - j:S5_STEPS - j, lr], pw_ref[S5_STEPS - 1 - j:S5_STEPS - j, li]
            lam_r = g3_ref[kr, _seg_rows(j), :] + pr * cr + pi * ci
            lam_i = g3_ref[ki, _seg_rows(j), :] + pr * ci - pi * cr
            g3_ref[kr, _seg_rows(j), :] = lam_r
            g3_ref[ki, _seg_rows(j), :] = lam_i
            if j == 0:
                xpr = jnp.where(row8 == 0, hr, pltpu.roll(x3_ref[kr, _seg_rows(S5_STEPS - 1), :], 1, 0))
                xpi = jnp.where(row8 == 0, hi, pltpu.roll(x3_ref[ki, _seg_rows(S5_STEPS - 1), :], 1, 0))
            else:
                xpr = x3_ref[kr, _seg_rows(j - 1), :]
                xpi = x3_ref[ki, _seg_rows(j - 1), :]
            acc_r = acc_r + lam_r * xpr + lam_i * xpi
            acc_i = acc_i + lam_i * xpr - lam_r * xpi
        dar_ref[:, lr] += _colsum(acc_r)
        dai_ref[:, lr] += _colsum(acc_i)


def _strip(x3_ref, part, s):
    k0 = part * NLT + s * (STW // 128)
    return jnp.concatenate([x3_ref[k0 + q] for q in range(STW // 128)], axis=1)


def _s5_fwd(u, a_row, bdb, bdc, dskip, glu_w, glu_b, ag_shard):
    T = u.shape[0]
    tm = S5_TM
    nt = T // tm

    def body(u_ref, a_ref, bdb_ref, bdc_ref, ds_ref, gw_ref, gb_ref, ag_ref, x_ref, y_ref, o_ref, ago_ref,
             pw_ref, pseg_ref, carry_ref, send_sems, recv_sems, local_sem):
        i = pl.program_id(0)
        start, forward, finish = _ag_steps(ag_ref, ago_ref, send_sems, recv_sems, local_sem)
        pl.when(i == 0)(start)

        @pl.when(i == 0)
        def _():
            carry_ref[...] = jnp.zeros_like(carry_ref)
            _s5_tables(a_ref, pw_ref, pseg_ref, descending=False)

        uv = u_ref[...]
        ub = jnp.dot(_seg_perm().astype(bf16), uv.astype(bf16), preferred_element_type=f32).astype(bf16)
        for part in range(2):
            for s in range(NST):
                bu = jnp.dot(ub[:, s * 128:(s + 1) * 128], bdb_ref[part * NST + s], preferred_element_type=f32)
                for q in range(STW // 128):
                    x_ref[part * NLT + s * (STW // 128) + q] = bu[:, q * 128:(q + 1) * 128]
        _scan_fwd(x_ref, pw_ref, pseg_ref, carry_ref)
        ys = []
        for s in range(NST):
            acc = None
            for part in range(2):
                t = jnp.dot(_strip(x_ref, part, s).astype(bf16), bdc_ref[part * NST + s], preferred_element_type=f32)
                acc = t if acc is None else acc + t
            ys.append(acc)
        y = _tri_matmul(_seg_perm(transpose=True), jnp.concatenate(ys, axis=1)) + ds_ref[...] * uv
        y_ref[...] = y
        g, _ = _gelu_and_grad(y)
        z = jnp.dot(g.astype(bf16), gw_ref[...], preferred_element_type=f32) + gb_ref[...]
        o_ref[...] = (g * _sig(z)).astype(bf16)
        pl.when(i == nt // 2)(forward)
        pl.when(i == nt - 1)(finish)

    row = lambda n: pl.BlockSpec((tm, n), lambda i: (i, 0))
    return _pc(
        body, name="s5_fwd", grid=(nt,),
        in_specs=[row(SW), _whole((1, 2 * SL)), _whole(bdb.shape), _whole(bdc.shape), _whole((1, SW)),
                  _whole((SW, SW)), _whole((1, SW)), ANY],
        out_specs=[pl.BlockSpec((2 * NLT, tm, 128), lambda i: (0, i, 0)), row(SW), row(SW), ANY],
        out_shape=[jax.ShapeDtypeStruct((2 * NLT, T, 128), f32), jax.ShapeDtypeStruct((T, SW), f32),
                   jax.ShapeDtypeStruct((T, SW), bf16),
                   jax.ShapeDtypeStruct((N_DEV, *ag_shard.shape), ag_shard.dtype)],
        scratch_shapes=[pltpu.VMEM((S5_STEPS, 2 * SL), f32), pltpu.VMEM((8, 2 * SL), f32), pltpu.VMEM((8, 2 * SL), f32)]
        + list(AG_SEMS),
        compiler_params=_params(),
    )(u, a_row, bdb, bdc, dskip, glu_w, glu_b, ag_shard)


def _s5_bwd(d_out, y, u, x, a_row, bdb, bdc, dskip, glu_w, glu_b, exch=None):
    T = u.shape[0]
    tm = S5_TM
    nt = T // tm

    def body(do_ref, y_ref, u_ref, x_ref, xh_ref, a_ref, bdb_ref, bdc_ref, ds_ref, gw_ref, gb_ref,
             du_ref, dglu_ref, dar_ref, dai_ref, dd_ref, dgb_ref, dbdb_ref, dbdc_ref, gs_ref, pw_ref, pseg_ref, carry_ref):
        i = pl.program_id(0)

        @pl.when(i == 0)
        def _():
            carry_ref[...] = jnp.zeros_like(carry_ref)
            _s5_tables(a_ref, pw_ref, pseg_ref, descending=True)
            dar_ref[...] = jnp.zeros_like(dar_ref)
            dai_ref[...] = jnp.zeros_like(dai_ref)
            dd_ref[...] = jnp.zeros_like(dd_ref)
            dgb_ref[...] = jnp.zeros_like(dgb_ref)
            dglu_ref[...] = jnp.zeros_like(dglu_ref)
            dbdb_ref[...] = jnp.zeros_like(dbdb_ref)
            dbdc_ref[...] = jnp.zeros_like(dbdc_ref)

        yv = y_ref[...]
        uv = u_ref[...]
        g, gp = _gelu_and_grad(yv)
        z = jnp.dot(g.astype(bf16), gw_ref[...], preferred_element_type=f32) + gb_ref[...]
        sg = _sig(z)
        do = do_ref[...].astype(f32)
        dz = do * g * sg * (1.0 - sg)
        dglu_ref[...] += _dot_tn(g, dz)
        dgb_ref[...] += _colsum(dz)
        dy = (do * sg + _dot_nt(dz, gw_ref[...])) * gp
        perm = _seg_perm().astype(bf16)
        dyb = jnp.dot(perm, dy.astype(bf16), preferred_element_type=f32).astype(bf16)
        dd_ref[...] += _colsum(dy * uv)
        for part in range(2):
            for s in range(NST):
                gx = lax.dot_general(dyb[:, s * 128:(s + 1) * 128], bdc_ref[part * NST + s], (((1,), (1,)), ((), ())),
                                     preferred_element_type=f32)
                for q in range(STW // 128):
                    gs_ref[part * NLT + s * (STW // 128) + q] = gx[:, q * 128:(q + 1) * 128]
        _scan_bwd(gs_ref, x_ref, xh_ref, i == nt - 1, pw_ref, pseg_ref, carry_ref, dar_ref, dai_ref)
        ub = jnp.dot(perm, uv.astype(bf16), preferred_element_type=f32).astype(bf16)
        dus = []
        for s in range(NST):
            acc = None
            for part in range(2):
                lv = _strip(gs_ref, part, s).astype(bf16)
                t = lax.dot_general(lv, bdb_ref[part * NST + s], (((1,), (1,)), ((), ())), preferred_element_type=f32)
                acc = t if acc is None else acc + t
                dbdb_ref[part * NST + s] += _dot_tn(ub[:, s * 128:(s + 1) * 128], lv)
                dbdc_ref[part * NST + s] += _dot_tn(_strip(x_ref, part, s), dyb[:, s * 128:(s + 1) * 128])
            dus.append(acc)
        du_ref[...] = _tri_matmul(_seg_perm(transpose=True), jnp.concatenate(dus, axis=1)) + dy * ds_ref[...]

    rev = lambda i: nt - 1 - i
    row = lambda n: pl.BlockSpec((tm, n), lambda i: (rev(i), 0))
    xblk = pl.BlockSpec((2 * NLT, tm, 128), lambda i: (0, rev(i), 0))
    halo = pl.BlockSpec((2 * NLT, 8, 128), lambda i: (0, jnp.maximum(rev(i) * (tm // 8) - 1, 0), 0))
    return _pc_behind(
        body, exch, nt, name="s5_bwd", grid=(nt,),
        in_specs=[row(SW), row(SW), row(SW), xblk, halo, _whole((1, 2 * SL)), _whole(bdb.shape), _whole(bdc.shape),
                  _whole((1, SW)), _whole((SW, SW)), _whole((1, SW))],
        out_specs=[row(SW), _acc((SW, SW)), _acc((1, SL)), _acc((1, SL)), _acc((1, SW)), _acc((1, SW)),
                   _acc(bdb.shape), _acc(bdc.shape)],
        out_shape=[jax.ShapeDtypeStruct((T, SW), f32), jax.ShapeDtypeStruct((SW, SW), f32),
                   jax.ShapeDtypeStruct((1, SL), f32), jax.ShapeDtypeStruct((1, SL), f32),
                   jax.ShapeDtypeStruct((1, SW), f32), jax.ShapeDtypeStruct((1, SW), f32),
                   jax.ShapeDtypeStruct(bdb.shape, f32), jax.ShapeDtypeStruct(bdc.shape, f32)],
        scratch_shapes=[pltpu.VMEM((2 * NLT, tm, 128), f32), pltpu.VMEM((S5_STEPS, 2 * SL), f32),
                        pltpu.VMEM((8, 2 * SL), f32), pltpu.VMEM((8, 2 * SL), f32)],
        compiler_params=_params(), args=(d_out, y, u, x, x, a_row, bdb, bdc, dskip, glu_w, glu_b))


def _mix_up(x, hg_o, s5_o, gates, w_bhg, w_bs5, w_out, g_ffn, w_up):
    T = x.shape[0]
    tm = 256

    def body(x_ref, hg_ref, s5_ref, gt_ref, wh_ref, ws_ref, wo_ref, g_ref, wu_ref, x1_ref, mg_ref, h2_ref, a_ref):
        yh = jnp.dot(hg_ref[...], wh_ref[...], preferred_element_type=f32)
        ys = jnp.dot(s5_ref[...], ws_ref[...], preferred_element_type=f32)
        merged = (_sig(gt_ref[:, 0:D]) * yh + _sig(gt_ref[:, D:2 * D]) * ys).astype(bf16)
        mg_ref[...] = merged
        x1 = x_ref[...] + jnp.dot(merged, wo_ref[...], preferred_element_type=f32)
        x1_ref[...] = x1
        xh, _ = _rms(x1)
        h2 = (xh * g_ref[...]).astype(bf16)
        h2_ref[...] = h2
        a_ref[...] = jnp.dot(h2, wu_ref[...], preferred_element_type=f32)

    row = lambda n: pl.BlockSpec((tm, n), lambda i: (i, 0))
    return _pc(
        body, name="mix_up", grid=(T // tm,),
        in_specs=[row(D), row(HW), row(SW), row(2 * D), _whole(w_bhg.shape), _whole(w_bs5.shape), _whole(w_out.shape),
                  _whole((1, D)), _whole(w_up.shape)],
        out_specs=[row(D), row(D), row(D), row(2 * DFF)],
        out_shape=[jax.ShapeDtypeStruct((T, D), f32), jax.ShapeDtypeStruct((T, D), bf16),
                   jax.ShapeDtypeStruct((T, D), bf16), jax.ShapeDtypeStruct((T, 2 * DFF), f32)],
        compiler_params=_params(),
    )(x, hg_o, s5_o, gates, w_bhg, w_bs5, w_out, g_ffn, w_up)


FFN_TM = 128
FFN_FS = 256


def _conv_gelu_blocks(a_ref, ah_ref, first, cw_ref, cb_ref, ga_ref, gb_ref, gated_ref):
    row8 = lax.broadcasted_iota(jnp.int32, (8, FFN_FS), 0)
    for s in range(DFF // FFN_FS):
        halves = (slice(s * FFN_FS, (s + 1) * FFN_FS), slice(DFF + s * FFN_FS, DFF + (s + 1) * FFN_FS))
        w = [[cw_ref[k:k + 1, ln] for k in range(3)] for ln in halves]
        bias = [cb_ref[:, ln] for ln in halves]
        prev = [jnp.where(first, 0.0, ah_ref[:, ln]) for ln in halves]
        p1 = [pltpu.roll(p, 1, 0) for p in prev]
        p2 = [pltpu.roll(p, 2, 0) for p in prev]
        for j in range(FFN_TM // 8):
            rows = slice(8 * j, 8 * j + 8)
            c = []
            for hf, ln in enumerate(halves):
                av = a_ref[rows, ln]
                r1, r2 = pltpu.roll(av, 1, 0), pltpu.roll(av, 2, 0)
                a1 = jnp.where(row8 >= 1, r1, p1[hf])
                a2 = jnp.where(row8 >= 2, r2, p2[hf])
                c.append(bias[hf] + w[hf][0] * a2 + w[hf][1] * a1 + w[hf][2] * av)
                p1[hf], p2[hf] = r1, r2
            gl, gp = _gelu_and_grad(c[0])
            ga_ref[rows, halves[0]] = c[1] * gp
            gb_ref[rows, halves[0]] = gl
            gated_ref[rows, halves[0]] = gl * c[1]


def _ffn_tail(a, conv_w, conv_b, w_down, x1, p, g_ple, w_pg, w_pp, g_fin, tgt):
    T = a.shape[0]
    tm = FFN_TM

    def body(a_ref, ah_ref, cw_ref, cb_ref, wd_ref, x1_ref, p_ref, gp_ref, wpg_ref, wpp_ref, gf_ref, t_ref,
             dx2_ref, gd_ref, ga_ref, gb_ref, dwpg_ref, dwpp_ref, loss_ref, dgf_ref, dgp_ref, gsc_ref):
        i = pl.program_id(0)

        @pl.when(i == 0)
        def _():
            loss_ref[...] = jnp.zeros_like(loss_ref)
            dgf_ref[...] = jnp.zeros_like(dgf_ref)
            dgp_ref[...] = jnp.zeros_like(dgp_ref)
            dwpg_ref[...] = jnp.zeros_like(dwpg_ref)
            dwpp_ref[...] = jnp.zeros_like(dwpp_ref)

        _conv_gelu_blocks(a_ref, ah_ref, i == 0, cw_ref, cb_ref, ga_ref, gb_ref, gsc_ref)
        gated = gsc_ref[...].astype(bf16)
        gd_ref[...] = gated
        x2 = x1_ref[...] + jnp.dot(gated, wd_ref[...], preferred_element_type=f32)
        xh2, r2 = _rms(x2)
        h3 = (xh2 * gp_ref[...]).astype(bf16)
        pg = _sig(jnp.dot(h3, wpg_ref[...], preferred_element_type=f32))
        pp = _dot(p_ref[...], wpp_ref[...])
        x3 = x2 + pg * pp
        xh3, r3 = _rms(x3)
        diff = xh3 * gf_ref[...] - t_ref[...]
        loss_ref[...] += 0.5 * jnp.sum(jnp.mean(diff * diff, axis=-1, keepdims=True), axis=0, keepdims=True)
        dy = diff * (1.0 / D)
        dx3, dgf_rows = _rms_bwd(dy, xh3, r3, gf_ref[...])
        dgf_ref[...] += _colsum(dgf_rows)
        dwpp_ref[...] += _dot_tn(p_ref[...], dx3 * pg)
        dz = (dx3 * pp * pg * (1.0 - pg)).astype(bf16)
        dwpg_ref[...] += _dot_tn(h3, dz)
        dh3 = _dot_nt(dz, wpg_ref[...])
        dx2n, dgp_rows = _rms_bwd(dh3, xh2, r2, gp_ref[...])
        dgp_ref[...] += _colsum(dgp_rows)
        dx2_ref[...] = dx3 + dx2n

    row = lambda n: pl.BlockSpec((tm, n), lambda i: (i, 0))
    halo = pl.BlockSpec((8, 2 * DFF), lambda i: (jnp.maximum(i * (tm // 8) - 1, 0), 0))
    return _pc(
        body, name="ffn_tail", grid=(T // tm,),
        in_specs=[row(2 * DFF), halo, _whole((3, 2 * DFF)), _whole((1, 2 * DFF)), _whole(w_down.shape), row(D), row(PLE),
                  _whole((1, D)), _whole(w_pg.shape), _whole(w_pp.shape), _whole((1, D)), row(D)],
        out_specs=[row(D), row(DFF), row(DFF), row(DFF), _acc(w_pg.shape), _acc(w_pp.shape),
                   _acc((1, 128)), _acc((1, D)), _acc((1, D))],
        out_shape=[jax.ShapeDtypeStruct((T, D), f32), jax.ShapeDtypeStruct((T, DFF), bf16),
                   jax.ShapeDtypeStruct((T, DFF), f32), jax.ShapeDtypeStruct((T, DFF), f32),
                   jax.ShapeDtypeStruct(w_pg.shape, f32), jax.ShapeDtypeStruct(w_pp.shape, f32),
                   jax.ShapeDtypeStruct((1, 128), f32), jax.ShapeDtypeStruct((1, D), f32), jax.ShapeDtypeStruct((1, D), f32)],
        scratch_shapes=[pltpu.VMEM((tm, DFF), f32)],
        compiler_params=_params(),
    )(a, a, conv_w, conv_b, w_down, x1, p, g_ple, w_pg, w_pp, g_fin, tgt)


def _ffn_bwd(dx2, a, g_a, g_b, conv_w, w_down, w_up, x1, g_ffn, exch=None):
    T = a.shape[0]
    tm = FFN_TM
    nt = T // tm

    def body(dx2_ref, a_ref, ga_ref, gb_ref, cw_ref, wd_ref, wu_ref, x1_ref, g_ref,
             da_ref, dx1_ref, dcw_ref, dcb_ref, dg_ref, carry_ref, dgd_ref, dasc_ref):
        i = pl.program_id(0)

        @pl.when(i == 0)
        def _():
            carry_ref[...] = jnp.zeros_like(carry_ref)
            dcw_ref[...] = jnp.zeros_like(dcw_ref)
            dcb_ref[...] = jnp.zeros_like(dcb_ref)
            dg_ref[...] = jnp.zeros_like(dg_ref)

        dx2 = dx2_ref[...]
        dgd_ref[...] = _dot_nt(dx2, wd_ref[...])
        row8 = lax.broadcasted_iota(jnp.int32, (8, FFN_FS), 0)
        for s in range(DFF // FFN_FS):
            src = slice(s * FFN_FS, (s + 1) * FFN_FS)
            halves = (src, slice(DFF + s * FFN_FS, DFF + (s + 1) * FFN_FS))
            w = [[cw_ref[k:k + 1, ln] for k in range(3)] for ln in halves]
            nxt = [carry_ref[:, ln] for ln in halves]
            n7 = [pltpu.roll(v, 7, 0) for v in nxt]
            n6 = [pltpu.roll(v, 6, 0) for v in nxt]
            acc = [[jnp.zeros((8, FFN_FS), f32) for _ in range(4)] for _ in halves]
            for j in reversed(range(tm // 8)):
                rows = slice(8 * j, 8 * j + 8)
                dg = dgd_ref[rows, src]
                for hf, (ln, saved) in enumerate(zip(halves, (ga_ref, gb_ref))):
                    dc = dg * saved[rows, src]
                    r7, r6 = pltpu.roll(dc, 7, 0), pltpu.roll(dc, 6, 0)
                    up1 = jnp.where(row8 < 7, r7, n7[hf])
                    up2 = jnp.where(row8 < 6, r6, n6[hf])
                    av = a_ref[rows, ln]
                    acc[hf][0] = acc[hf][0] + up2 * av
                    acc[hf][1] = acc[hf][1] + up1 * av
                    acc[hf][2] = acc[hf][2] + dc * av
                    acc[hf][3] = acc[hf][3] + dc
                    dasc_ref[rows, ln] = w[hf][2] * dc + w[hf][1] * up1 + w[hf][0] * up2
                    n7[hf], n6[hf] = r7, r6
                    if j == 0:
                        carry_ref[:, ln] = dc
            for hf, ln in enumerate(halves):
                for k in range(3):
                    dcw_ref[k:k + 1, ln] += _colsum(acc[hf][k])
                dcb_ref[:, ln] += _colsum(acc[hf][3])
        da = dasc_ref[...].astype(bf16)
        da_ref[...] = da
        dh2 = lax.dot_general(da, wu_ref[...], (((1,), (1,)), ((), ())), preferred_element_type=f32)
        xh, r = _rms(x1_ref[...])
        dx1n, dg_rows = _rms_bwd(dh2, xh, r, g_ref[...])
        dg_ref[...] += _colsum(dg_rows)
        dx1_ref[...] = dx2 + dx1n

    rev = lambda i: nt - 1 - i
    row = lambda n: pl.BlockSpec((tm, n), lambda i: (rev(i), 0))
    return _pc_behind(
        body, exch, nt, name="ffn_bwd", grid=(nt,),
        in_specs=[row(D), row(2 * DFF), row(DFF), row(DFF), _whole((3, 2 * DFF)), _whole(w_down.shape),
                  _whole(w_up.shape), row(D), _whole((1, D))],
        out_specs=[row(2 * DFF), row(D), _acc((3, 2 * DFF)), _acc((1, 2 * DFF)), _acc((1, D))],
        out_shape=[jax.ShapeDtypeStruct((T, 2 * DFF), bf16), jax.ShapeDtypeStruct((T, D), f32),
                   jax.ShapeDtypeStruct((3, 2 * DFF), f32), jax.ShapeDtypeStruct((1, 2 * DFF), f32),
                   jax.ShapeDtypeStruct((1, D), f32)],
        scratch_shapes=[pltpu.VMEM((8, 2 * DFF), f32), pltpu.VMEM((tm, DFF), f32), pltpu.VMEM((tm, 2 * DFF), f32)],
        compiler_params=_params(), args=(dx2, a, g_a, g_b, conv_w, w_down, w_up, x1, g_ffn))


def _mix_bwd(dx1, hg_o, s5_o, gates, w_bhg, w_bs5, w_out, exch=None):
    T = dx1.shape[0]
    tm = 256

    def body(dx1_ref, hg_ref, s5_ref, gt_ref, wh_ref, ws_ref, wo_ref, dgt_ref, dhg_ref, ds5_ref, dwh_ref, dws_ref):
        @pl.when(pl.program_id(0) == 0)
        def _():
            dwh_ref[...] = jnp.zeros_like(dwh_ref)
            dws_ref[...] = jnp.zeros_like(dws_ref)

        dm = _dot_nt(dx1_ref[...], wo_ref[...])
        yh = jnp.dot(hg_ref[...], wh_ref[...], preferred_element_type=f32)
        ys = jnp.dot(s5_ref[...], ws_ref[...], preferred_element_type=f32)
        sh = _sig(gt_ref[:, 0:D])
        ss = _sig(gt_ref[:, D:2 * D])
        dgt_ref[:, 0:D] = dm * yh * sh * (1.0 - sh)
        dgt_ref[:, D:2 * D] = dm * ys * ss * (1.0 - ss)
        dyh = (dm * sh).astype(bf16)
        dys = (dm * ss).astype(bf16)
        dwh_ref[...] += _dot_tn(hg_ref[...], dyh)
        dws_ref[...] += _dot_tn(s5_ref[...], dys)
        dhg_ref[...] = lax.dot_general(dyh, wh_ref[...], (((1,), (1,)), ((), ())), preferred_element_type=f32)
        ds5_ref[...] = lax.dot_general(dys, ws_ref[...], (((1,), (1,)), ((), ())), preferred_element_type=f32)

    row = lambda n: pl.BlockSpec((tm, n), lambda i: (i, 0))
    return _pc_behind(
        body, exch, T // tm, name="mix_bwd", grid=(T // tm,),
        in_specs=[row(D), row(HW), row(SW), row(2 * D), _whole(w_bhg.shape), _whole(w_bs5.shape), _whole(w_out.shape)],
        out_specs=[row(2 * D), row(HW), row(SW), _acc(w_bhg.shape), _acc(w_bs5.shape)],
        out_shape=[jax.ShapeDtypeStruct((T, 2 * D), f32), jax.ShapeDtypeStruct((T, HW), f32), jax.ShapeDtypeStruct((T, SW), f32),
                   jax.ShapeDtypeStruct(w_bhg.shape, f32), jax.ShapeDtypeStruct(w_bs5.shape, f32)],
        compiler_params=_params(), args=(dx1, hg_o, s5_o, gates, w_bhg, w_bs5, w_out))


def _in_bwd(d_hg, d_u, d_gt, x, dx1, w, g):
    T = x.shape[0]
    tm = 256

    def body(dhg_ref, du_ref, dgt_ref, x_ref, dx1_ref, w_ref, g_ref, dx_ref, dg_ref):
        @pl.when(pl.program_id(0) == 0)
        def _():
            dg_ref[...] = jnp.zeros_like(dg_ref)

        dh = (_dot_nt(dhg_ref[...], w_ref[:, 0:4 * HW]) + _dot_nt(du_ref[...], w_ref[:, 4 * HW:4 * HW + SW])
              + _dot_nt(dgt_ref[...], w_ref[:, 4 * HW + SW:]))
        xh, r = _rms(x_ref[...])
        dxn, dg_rows = _rms_bwd(dh, xh, r, g_ref[...])
        dg_ref[...] += _colsum(dg_rows)
        dx_ref[...] = dx1_ref[...] + dxn

    row = lambda n: pl.BlockSpec((tm, n), lambda i: (i, 0))
    return _pc(
        body, name="in_bwd", grid=(T // tm,),
        in_specs=[row(4 * HW), row(SW), row(2 * D), row(D), row(D), _whole(w.shape), _whole((1, D))],
        out_specs=[row(D), _acc((1, D))],
        out_shape=[jax.ShapeDtypeStruct((T, D), f32), jax.ShapeDtypeStruct((1, D), f32)],
        compiler_params=_params(),
    )(d_hg, d_u, d_gt, x, dx1, w, g)


def _wgrad(name, a, b, nj=None, a_blk=None, a_idx=None, b_blk=None, b_idx=None):
    T = a.shape[0]
    tm = 512
    dense = nj is None
    if dense:
        K, N = a.shape[1], b.shape[1]
        a_blk, a_idx = K, (lambda j: 0)
        b_blk = N
        while K * b_blk * 4 > 6 * 1024 * 1024 and b_blk % 256 == 0:
            b_blk //= 2
        nj, b_idx = N // b_blk, (lambda j: j)

    def body(a_ref, b_ref, o_ref):
        @pl.when(pl.program_id(1) == 0)
        def _():
            o_ref[...] = jnp.zeros_like(o_ref)

        o_ref[0] += _dot_tn(a_ref[...], b_ref[...])

    out = _pc(
        body, name=name, grid=(nj, T // tm),
        in_specs=[pl.BlockSpec((tm, a_blk), lambda j, i: (i, a_idx(j))), pl.BlockSpec((tm, b_blk), lambda j, i: (i, b_idx(j)))],
        out_specs=pl.BlockSpec((1, a_blk, b_blk), lambda j, i: (j, 0, 0)),
        out_shape=jax.ShapeDtypeStruct((nj, a_blk, b_blk), f32),
        compiler_params=_params(2),
    )(a, b)
    if dense:
        return out[0] if nj == 1 else jnp.transpose(out, (1, 0, 2)).reshape(a.shape[1], b.shape[1])
    return out


ANY = pl.BlockSpec(memory_space=pl.ANY)


AG_SEMS = [pltpu.SemaphoreType.DMA((7,)), pltpu.SemaphoreType.DMA((7,)), pltpu.SemaphoreType.DMA]


def _ag_steps(x_ref, out_ref, send_sems, recv_sems, local_sem):
    x, y, c = lax.axis_index("x"), lax.axis_index("y"), lax.axis_index("c")
    me, sibling = (x, y, c), (x, y, 1 - c)
    chips = [(1 - x, y), (x, 1 - y), (1 - x, 1 - y)]

    def slot(px, py, pc):
        return out_ref.at[4 * px + 2 * py + pc]

    def copy(k, block, to, src=None):
        return pltpu.make_async_remote_copy(
            src_ref=slot(*block) if src is None else src, dst_ref=slot(*block),
            send_sem=send_sems.at[k], recv_sem=recv_sems.at[k], device_id=to, device_id_type=MESH)

    def mine():
        return pltpu.make_async_copy(x_ref, slot(*me), local_sem)

    def first():
        return [copy(0, me, sibling, src=x_ref)] + [copy(1 + j, me, (*chip, c), src=x_ref) for j, chip in enumerate(chips)]

    def passed():
        return [copy(4 + j, (*chip, c), sibling) for j, chip in enumerate(chips)]

    def start():
        mine().start()
        for cp in first():
            cp.start()

    def forward():
        for j, (chip, cp) in enumerate(zip(chips, passed())):
            copy(1 + j, (*chip, c), me).wait_recv()
            cp.start()

    def finish():
        copy(0, sibling, me).wait_recv()
        for j, chip in enumerate(chips):
            copy(4 + j, (*chip, 1 - c), me).wait_recv()
        for cp in first() + passed():
            cp.wait_send()
        mine().wait()

    return start, forward, finish


def _all_gather(name, shard):
    R, C = shard.shape

    def body(x_ref, out_ref, send_sems, recv_sems, local_sem):
        for phase in _ag_steps(x_ref, out_ref, send_sems, recv_sems, local_sem):
            phase()

    return _pc(
        body, name=name, in_specs=[ANY], out_specs=ANY,
        out_shape=jax.ShapeDtypeStruct((N_DEV, R, C), shard.dtype), scratch_shapes=list(AG_SEMS),
    )(shard)


class _Exchange:
    def __init__(self, kind, arrays):
        self.kind, self.arrays, self.n = kind, list(arrays), len(arrays)
        self.per = 4 if kind == "sibling" else 3
        tail = (lambda a: a.shape[2:]) if kind == "sibling" else (lambda a: a.shape[1:])
        self.out_shape = [jax.ShapeDtypeStruct((self.per, *tail(a)), a.dtype) for a in self.arrays]
        self.scratch = [pltpu.SemaphoreType.DMA((self.per * self.n,)), pltpu.SemaphoreType.DMA((self.per * self.n,))]

    def steps(self, in_refs, out_refs, send_sems, recv_sems):
        x, y, c = lax.axis_index("x"), lax.axis_index("y"), lax.axis_index("c")
        chips = [(1 - x, y), (x, 1 - y), (1 - x, 1 - y)]

        def copies():
            cps = []
            for i, (src, dst) in enumerate(zip(in_refs, out_refs)):
                for k in range(self.per):
                    if self.kind == "sibling":
                        s, to = src.at[k, 1 - c], (x, y, 1 - c)
                    else:
                        s, to = src.at[2 * chips[k][0] + chips[k][1]], (*chips[k], c)
                    cps.append(pltpu.make_async_remote_copy(
                        src_ref=s, dst_ref=dst.at[k], send_sem=send_sems.at[self.per * i + k],
                        recv_sem=recv_sems.at[self.per * i + k], device_id=to, device_id_type=MESH))
            return cps

        def start():
            for cp in copies():
                cp.start()

        def finish():
            for cp in copies():
                cp.wait()

        return start, finish


def _exchange_call(name, exch):
    n = exch.n

    def body(*refs):
        start, finish = exch.steps(refs[:n], refs[n:2 * n], *refs[2 * n:])
        start()
        finish()

    return _pc(body, name=name, in_specs=[ANY] * n, out_specs=[ANY] * n, out_shape=exch.out_shape,
               scratch_shapes=exch.scratch)(*exch.arrays)


def _pc_behind(body, exch, nsteps, *, in_specs, out_specs, out_shape, args, scratch_shapes=(), **kw):
    if exch is None:
        return _pc(body, in_specs=in_specs, out_specs=out_specs, out_shape=out_shape, scratch_shapes=list(scratch_shapes),
                   **kw)(*args), None
    n_in, n_out, n_scr, ne = len(in_specs), len(out_specs), len(scratch_shapes), exch.n

    def wrapped(*refs):
        ins, e_in = refs[:n_in], refs[n_in:n_in + ne]
        o0 = n_in + ne
        outs, e_out = refs[o0:o0 + n_out], refs[o0 + n_out:o0 + n_out + ne]
        s0 = o0 + n_out + ne
        scr, sems = refs[s0:s0 + n_scr], refs[s0 + n_scr:]
        start, finish = exch.steps(e_in, e_out, *sems)
        i = pl.program_id(0)
        pl.when(i == 0)(start)
        body(*ins, *outs, *scr)
        pl.when(i == nsteps - 1)(finish)

    res = _pc(wrapped, in_specs=list(in_specs) + [ANY] * ne, out_specs=list(out_specs) + [ANY] * ne,
              out_shape=list(out_shape) + exch.out_shape, scratch_shapes=list(scratch_shapes) + exch.scratch,
              **kw)(*args, *exch.arrays)
    return res[:n_out], res[n_out:]


def _add_halves(name, g4, got, ids):
    _, _, K, c = g4.shape

    def body(ids_ref, a_ref, b_ref, p16_ref, own_ref):
        s = a_ref[0, 0] + b_ref[0]
        p16_ref[0] = s.astype(bf16)

        @pl.when(pl.program_id(0) == ids_ref[1])
        def _():
            own_ref[...] = s

    return _pc(
        body, name=name,
        grid_spec=pltpu.PrefetchScalarGridSpec(
            num_scalar_prefetch=1, grid=(4,),
            in_specs=[pl.BlockSpec((1, 1, K, c), lambda k, ids: (k, ids[0], 0, 0)),
                      pl.BlockSpec((1, K, c), lambda k, ids: (k, 0, 0))],
            out_specs=[pl.BlockSpec((1, K, c), lambda k, ids: (k, 0, 0)), pl.BlockSpec((K, c), lambda k, ids: (0, 0))]),
        out_shape=[jax.ShapeDtypeStruct((4, K, c), bf16), jax.ShapeDtypeStruct((K, c), f32)],
        compiler_params=_params(),
    )(ids, g4, got)


def _row_tile(K):
    for cand in (256, 176, 128, 64):
        if K % cand == 0:
            return cand
    return K


def _adam_shard(name, own, got3, w, m, v):
    K, c = own.shape
    tr = _row_tile(K)

    def body(own_ref, got_ref, w_ref, m_ref, v_ref, g_ref, d_ref, m2_ref, v2_ref):
        g = own_ref[...] + got_ref[0].astype(f32) + got_ref[1].astype(f32) + got_ref[2].astype(f32)
        g_ref[0] = g
        delta, m2, v2 = _adam_math(g, w_ref[0], m_ref[0], v_ref[0])
        d_ref[0] = delta
        m2_ref[0] = m2
        v2_ref[0] = v2

    blk = pl.BlockSpec((1, tr, c), lambda i: (0, i, 0))
    out = jax.ShapeDtypeStruct((1, K, c), f32)
    return _pc(
        body, name=name, grid=(K // tr,),
        in_specs=[pl.BlockSpec((tr, c), lambda i: (i, 0)), pl.BlockSpec((3, tr, c), lambda i: (0, i, 0)), blk, blk, blk],
        out_specs=[blk, blk, blk, blk], out_shape=[out, out, out, out], compiler_params=_params(),
    )(own, got3, w, m, v)


def _allreduce_small(grads):
    n = len(grads)
    shapes = [g.shape for g in grads]

    def body(*refs):
        g_refs, outs, recv = refs[0:n], refs[n:2 * n], refs[2 * n:5 * n]
        send_sems, recv_sems = refs[5 * n:]
        x, y, c = lax.axis_index("x"), lax.axis_index("y"), lax.axis_index("c")
        peers = [(x, y, 1 - c), (1 - x, y, c), (x, 1 - y, c)]
        for i in range(n):
            outs[i][...] = g_refs[i][...]
        for s, peer in enumerate(peers):
            cps = [pltpu.make_async_remote_copy(src_ref=outs[i], dst_ref=recv[s * n + i], send_sem=send_sems.at[s * n + i],
                                                recv_sem=recv_sems.at[s * n + i], device_id=peer, device_id_type=MESH)
                   for i in range(n)]
            for cp in cps:
                cp.start()
            for cp in cps:
                cp.wait()
            for i in range(n):
                outs[i][...] = outs[i][...] + recv[s * n + i][...]

    return _pc(
        body, name="allreduce_small", grid=(1,), in_specs=[_whole(s) for s in shapes], out_specs=[_acc(s) for s in shapes],
        out_shape=[jax.ShapeDtypeStruct(s, f32) for s in shapes],
        scratch_shapes=[pltpu.VMEM(s, f32) for s in shapes] * 3
        + [pltpu.SemaphoreType.DMA((3 * n,)), pltpu.SemaphoreType.DMA((3 * n,))],
        compiler_params=_params(),
    )(*grads)


def _adam_small(grads, ws, ms, vs):
    n = len(grads)
    shapes = [g.shape for g in grads]

    def body(*refs):
        g_refs, w_refs, m_refs, v_refs = refs[0:n], refs[n:2 * n], refs[2 * n:3 * n], refs[3 * n:4 * n]
        outs = refs[4 * n:8 * n]
        for i in range(n):
            g = g_refs[i][...]
            delta, m2, v2 = _adam_math(g, w_refs[i][...], m_refs[i][...], v_refs[i][...])
            outs[i][...] = g
            outs[n + i][...] = delta
            outs[2 * n + i][...] = m2
            outs[3 * n + i][...] = v2

    return _pc(
        body, name="adam_small", grid=(1,), in_specs=[_whole(s) for s in shapes] * 4, out_specs=[_acc(s) for s in shapes] * 4,
        out_shape=[jax.ShapeDtypeStruct(s, f32) for s in shapes] * 4, compiler_params=_params(),
    )(*grads, *ws, *ms, *vs)


def _adam_math(g, w, m, v):
    m2 = ADAM_B1 * m + (1.0 - ADAM_B1) * g
    v2 = ADAM_B2 * v + (1.0 - ADAM_B2) * (g * g)
    m_hat = m2 / (1.0 - ADAM_B1 ** ADAM_STEP)
    v_hat = v2 / (1.0 - ADAM_B2 ** ADAM_STEP)
    delta = -ADAM_LR * (m_hat / (jnp.sqrt(v_hat) + ADAM_EPS) + ADAM_WD * w)
    return delta, m2, v2


def _pack(arrs, dtype, row_mult):
    rows = []
    for a in arrs:
        flat = a.reshape(-1).astype(dtype)
        pad = (-flat.shape[0]) % LANES
        if pad:
            flat = jnp.concatenate([flat, jnp.zeros((pad,), dtype)])
        rows.append(flat.reshape(-1, LANES))
    out = jnp.concatenate(rows, axis=0)
    pad = (-out.shape[0]) % row_mult
    if pad:
        out = jnp.concatenate([out, jnp.zeros((pad, LANES), dtype)], axis=0)
    return out


def _unpack(buf, shapes):
    lead = buf.shape[:-2]
    outs, r = [], 0
    for shp in shapes:
        n = math.prod(shp)
        nr = -(-n // LANES)
        piece = buf[..., r:r + nr, :].reshape(*lead, nr * LANES)[..., :n]
        outs.append(piece.reshape(*lead, *shp))
        r += nr
    return outs


def _to_slabs(full, axis):
    shp = full.shape
    n = shp[axis] // N_DEV
    return jnp.moveaxis(full.reshape(*shp[:axis], N_DEV, n, *shp[axis + 1:]), axis, 0)


def _from_slabs(slabs, axis):
    t = jnp.moveaxis(slabs, 0, axis)
    shp = t.shape
    return t.reshape(*shp[:axis], shp[axis] * shp[axis + 1], *shp[axis + 2:])


def _s5_discretise(lam_re, lam_im, log_dt, b_re, b_im):
    dt = jnp.exp(log_dt)[:, None]
    mag = jnp.exp(lam_re * dt)
    a_re = mag * jnp.cos(lam_im * dt)
    a_im = mag * jnp.sin(lam_im * dt)
    den = lam_re * lam_re + lam_im * lam_im
    coef_re = ((a_re - 1.0) * lam_re + a_im * lam_im) / den
    coef_im = (a_im * lam_re - (a_re - 1.0) * lam_im) / den
    bbar_re = coef_re[..., None] * b_re - coef_im[..., None] * b_im
    bbar_im = coef_re[..., None] * b_im + coef_im[..., None] * b_re
    return a_re, a_im, bbar_re, bbar_im


def _s5_operands(bbar_re, bbar_im, c_re, c_im):
    eye = jnp.eye(SG // NST, dtype=f32)

    def b_op(bb):
        return jnp.einsum("sgnq,gh->sgqhn", bb.reshape(NST, SG // NST, SN, SP), eye).reshape(NST, 128, STW)

    def c_op(cc):
        return jnp.einsum("sgpn,gh->shngp", cc.reshape(NST, SG // NST, SP, SN), eye).reshape(NST, STW, 128)

    bdb = jnp.concatenate([b_op(bbar_re), b_op(bbar_im)], axis=0)
    bdc = jnp.concatenate([c_op(c_re), c_op(-c_im)], axis=0)
    return bdb, bdc


_BIG = ["w_in", "s5_glu_w", "w_branch_hg", "w_branch_s5", "w_out", "w_up", "w_down", "w_ple_gate", "w_ple_proj", "conv_w"]
_BIG_AXIS = {"w_in": 1, "s5_glu_w": 0, "w_branch_hg": 1, "w_branch_s5": 1, "w_out": 0, "w_up": 1, "w_down": 0,
             "w_ple_gate": 0, "w_ple_proj": 1, "conv_w": 1}
_SMALL = ["norm_mix_g", "hg_lb_logits", "hg_norm_g", "s5_lambda_re", "s5_lambda_im", "s5_log_dt", "s5_b_re", "s5_b_im",
          "s5_c_re", "s5_c_im", "s5_d", "s5_glu_b", "norm_ffn_g", "conv_b", "norm_ple_g", "norm_final_g"]
_ORDER = ["norm_mix_g", "w_in", "hg_lb_logits", "hg_norm_g", "s5_lambda_re", "s5_lambda_im", "s5_log_dt", "s5_b_re",
          "s5_b_im", "s5_c_re", "s5_c_im", "s5_d", "s5_glu_w", "s5_glu_b", "w_branch_hg", "w_branch_s5", "w_out",
          "norm_ffn_g", "w_up", "conv_w", "conv_b", "w_down", "norm_ple_g", "w_ple_gate", "w_ple_proj", "norm_final_g"]


def kernel(x, p, norm_mix_g, w_in, hg_lb_logits, hg_norm_g, s5_lambda_re, s5_lambda_im, s5_log_dt, s5_b_re, s5_b_im, s5_c_re, s5_c_im, s5_d, s5_glu_w, s5_glu_b, w_branch_hg, w_branch_s5, w_out, norm_ffn_g, w_up, conv_w, conv_b, w_down, norm_ple_g, w_ple_gate, w_ple_proj, norm_final_g, loss_target, m_norm_mix_g, m_w_in, m_hg_lb_logits, m_hg_norm_g, m_s5_lambda_re, m_s5_lambda_im, m_s5_log_dt, m_s5_b_re, m_s5_b_im, m_s5_c_re, m_s5_c_im, m_s5_d, m_s5_glu_w, m_s5_glu_b, m_w_branch_hg, m_w_branch_s5, m_w_out, m_norm_ffn_g, m_w_up, m_conv_w, m_conv_b, m_w_down, m_norm_ple_g, m_w_ple_gate, m_w_ple_proj, m_norm_final_g, v_norm_mix_g, v_w_in, v_hg_lb_logits, v_hg_norm_g, v_s5_lambda_re, v_s5_lambda_im, v_s5_log_dt, v_s5_b_re, v_s5_b_im, v_s5_c_re, v_s5_c_im, v_s5_d, v_s5_glu_w, v_s5_glu_b, v_w_branch_hg, v_w_branch_s5, v_w_out, v_norm_ffn_g, v_w_up, v_conv_w, v_conv_b, v_w_down, v_norm_ple_g, v_w_ple_gate, v_w_ple_proj, v_norm_final_g):
    W = dict(norm_mix_g=norm_mix_g, w_in=w_in, hg_lb_logits=hg_lb_logits, hg_norm_g=hg_norm_g, s5_lambda_re=s5_lambda_re, s5_lambda_im=s5_lambda_im, s5_log_dt=s5_log_dt, s5_b_re=s5_b_re, s5_b_im=s5_b_im, s5_c_re=s5_c_re, s5_c_im=s5_c_im, s5_d=s5_d, s5_glu_w=s5_glu_w, s5_glu_b=s5_glu_b, w_branch_hg=w_branch_hg, w_branch_s5=w_branch_s5, w_out=w_out, norm_ffn_g=norm_ffn_g, w_up=w_up, conv_w=conv_w, conv_b=conv_b, w_down=w_down, norm_ple_g=norm_ple_g, w_ple_gate=w_ple_gate, w_ple_proj=w_ple_proj, norm_final_g=norm_final_g)
    M = dict(norm_mix_g=m_norm_mix_g, w_in=m_w_in, hg_lb_logits=m_hg_lb_logits, hg_norm_g=m_hg_norm_g, s5_lambda_re=m_s5_lambda_re, s5_lambda_im=m_s5_lambda_im, s5_log_dt=m_s5_log_dt, s5_b_re=m_s5_b_re, s5_b_im=m_s5_b_im, s5_c_re=m_s5_c_re, s5_c_im=m_s5_c_im, s5_d=m_s5_d, s5_glu_w=m_s5_glu_w, s5_glu_b=m_s5_glu_b, w_branch_hg=m_w_branch_hg, w_branch_s5=m_w_branch_s5, w_out=m_w_out, norm_ffn_g=m_norm_ffn_g, w_up=m_w_up, conv_w=m_conv_w, conv_b=m_conv_b, w_down=m_w_down, norm_ple_g=m_norm_ple_g, w_ple_gate=m_w_ple_gate, w_ple_proj=m_w_ple_proj, norm_final_g=m_norm_final_g)
    V = dict(norm_mix_g=v_norm_mix_g, w_in=v_w_in, hg_lb_logits=v_hg_lb_logits, hg_norm_g=v_hg_norm_g, s5_lambda_re=v_s5_lambda_re, s5_lambda_im=v_s5_lambda_im, s5_log_dt=v_s5_log_dt, s5_b_re=v_s5_b_re, s5_b_im=v_s5_b_im, s5_c_re=v_s5_c_re, s5_c_im=v_s5_c_im, s5_d=v_s5_d, s5_glu_w=v_s5_glu_w, s5_glu_b=v_s5_glu_b, w_branch_hg=v_w_branch_hg, w_branch_s5=v_w_branch_s5, w_out=v_w_out, norm_ffn_g=v_norm_ffn_g, w_up=v_w_up, conv_w=v_conv_w, conv_b=v_conv_b, w_down=v_w_down, norm_ple_g=v_norm_ple_g, w_ple_gate=v_w_ple_gate, w_ple_proj=v_w_ple_proj, norm_final_g=v_norm_final_g)

    shard2 = {n: W[n][0] for n in _BIG}
    conv_bits = lax.bitcast_convert_type(shard2["conv_w"], bf16)
    groups = [["w_in", "s5_glu_w"], ["w_branch_hg", "w_branch_s5", "w_out", "w_ple_gate", "w_ple_proj", "w_down"], ["w_up"]]
    packs = [_pack([shard2[n] for n in grp] + ([conv_bits] if k == 0 else []), bf16, 16) for k, grp in enumerate(groups)]
    full = {}

    def take(k, gathered):
        pieces = _unpack(gathered, [shard2[n].shape for n in groups[k]] + ([conv_bits.shape] if k == 0 else []))
        full.update({n: _from_slabs(pc, _BIG_AXIS[n]) for n, pc in zip(groups[k], pieces)})
        return pieces

    conv_w_full = _from_slabs(lax.bitcast_convert_type(take(0, _all_gather("ag_weights", packs[0]))[-1], f32), 1)

    xt = x[0]
    pt = p[0, 0]
    tgt = loss_target[0]
    T = xt.shape[0]
    lam_re, lam_im, log_dt = s5_lambda_re[0], s5_lambda_im[0], s5_log_dt[0]
    b_re, b_im, c_re, c_im = s5_b_re[0], s5_b_im[0], s5_c_re[0], s5_c_im[0]

    def s5_prep(lam_re, lam_im, log_dt, b_re, b_im, c_re, c_im):
        a_re, a_im, bbar_re, bbar_im = _s5_discretise(lam_re, lam_im, log_dt, b_re, b_im)
        bdb, bdc = _s5_operands(bbar_re, bbar_im, c_re, c_im)
        return a_re, a_im, bdb, bdc

    (a_re, a_im, bdb, bdc), s5_prep_vjp = jax.vjp(s5_prep, lam_re, lam_im, log_dt, b_re, b_im, c_re, c_im)
    a_row = jnp.concatenate([a_re.reshape(1, SL), a_im.reshape(1, SL)], axis=1)
    bdb_b, bdc_b = bdb.astype(bf16), bdc.astype(bf16)

    h1, proj_hg, u_raw, gates, gathered1 = _in_proj(xt, norm_mix_g, full["w_in"], packs[1])
    take(1, gathered1)
    ng4 = jnp.tile(hg_norm_g, (1, NH))
    hg_o, sprev = _hgrn_fwd(proj_hg, hg_lb_logits, ng4)
    x_st, y_s5, s5_o, gathered2 = _s5_fwd(u_raw, a_row, bdb_b, bdc_b, s5_d, full["s5_glu_w"], s5_glu_b, packs[2])
    take(2, gathered2)
    x1, merged, h2, a_up = _mix_up(xt, hg_o, s5_o, gates, full["w_branch_hg"], full["w_branch_s5"], full["w_out"],
                                   norm_ffn_g, full["w_up"])
    (dx2, gated, g_a, g_b, d_w_pg, d_w_pp, loss_part, d_norm_final, d_norm_ple) = _ffn_tail(
        a_up, conv_w_full, conv_b, full["w_down"], x1, pt, norm_ple_g, full["w_ple_gate"], full["w_ple_proj"],
        norm_final_g.reshape(1, D), tgt)

    ids = jnp.stack([lax.axis_index("c"), 2 * lax.axis_index("x") + lax.axis_index("y")]).astype(jnp.int32)
    gw, own_sum, got3 = {}, {}, {}

    def slabs(names):
        return [_to_slabs(gw[n], _BIG_AXIS[n]).reshape(4, 2, *shard2[n].shape) for n in names]

    def add_pairs(names, g4, got):
        sums = [_add_halves("rs_add_" + n, g, r, ids) for n, g, r in zip(names, g4, got)]
        own_sum.update({n: own for n, (_, own) in zip(names, sums)})
        return [p16 for p16, _ in sums]

    grp_a = ["w_down", "w_ple_gate", "w_ple_proj"]
    gw["w_down"] = _wgrad("wg_down", gated, dx2)
    gw["w_ple_gate"] = d_w_pg
    gw["w_ple_proj"] = d_w_pp
    g4_a = slabs(grp_a)
    (da_up, dx1, d_conv_w, d_conv_b, d_norm_ffn), got_a = _ffn_bwd(
        dx2, a_up, g_a, g_b, conv_w_full, full["w_down"], full["w_up"], x1, norm_ffn_g, exch=_Exchange("sibling", g4_a))
    p16_a = add_pairs(grp_a, g4_a, got_a)
    (d_gates, d_hg_o, d_s5_o, d_w_bhg, d_w_bs5), got3_a = _mix_bwd(
        dx1, hg_o, s5_o, gates, full["w_branch_hg"], full["w_branch_s5"], full["w_out"], exch=_Exchange("chips", p16_a))
    got3.update(zip(grp_a, got3_a))

    grp_b = ["w_up", "w_out", "w_branch_hg", "w_branch_s5", "conv_w"]
    gw["w_up"] = _wgrad("wg_up", h2, da_up)
    gw["w_out"] = _wgrad("wg_out", merged, dx1)
    gw["w_branch_hg"] = d_w_bhg
    gw["w_branch_s5"] = d_w_bs5
    gw["conv_w"] = d_conv_w
    g4_b = slabs(grp_b)
    (d_proj_hg, d_lb, d_hg_norm), got_b = _hgrn_bwd(proj_hg, hg_lb_logits, ng4, sprev, d_hg_o,
                                                     exch=_Exchange("sibling", g4_b))
    p16_b = add_pairs(grp_b, g4_b, got_b)
    (d_u, d_w_glu, d_a_re, d_a_im, d_s5_d, d_glu_b, d_bdb, d_bdc), got3_b = _s5_bwd(
        d_s5_o, y_s5, u_raw, x_st, a_row, bdb_b, bdc_b, s5_d, full["s5_glu_w"], s5_glu_b, exch=_Exchange("chips", p16_b))
    got3.update(zip(grp_b, got3_b))
    grad_x, d_norm_mix = _in_bwd(d_proj_hg, d_u, d_gates, xt, dx1, full["w_in"], norm_mix_g)

    grp_c = ["w_in", "s5_glu_w"]
    gw["w_in"] = jnp.concatenate([_wgrad("wg_in_hg", h1, d_proj_hg), _wgrad("wg_in_u", h1, d_u),
                                  _wgrad("wg_in_gates", h1, d_gates)], axis=1)
    gw["s5_glu_w"] = d_w_glu
    g4_c = slabs(grp_c)
    p16_c = add_pairs(grp_c, g4_c, _exchange_call("rs_sibling", _Exchange("sibling", g4_c)))
    got3.update(zip(grp_c, _exchange_call("rs_chips", _Exchange("chips", p16_c))))

    (d_lam_re, d_lam_im, d_log_dt, d_b_re, d_b_im, d_c_re, d_c_im) = s5_prep_vjp(
        (d_a_re.reshape(SG, SN), d_a_im.reshape(SG, SN), d_bdb, d_bdc))
    sm = jax.nn.softmax(hg_lb_logits, axis=0)
    d_l0 = d_lb[0] * sm[0] * sm[1]
    d_logits = jnp.stack([d_l0, -d_l0], axis=0)

    gs = {"norm_mix_g": d_norm_mix, "hg_lb_logits": d_logits, "hg_norm_g": d_hg_norm, "s5_lambda_re": d_lam_re,
          "s5_lambda_im": d_lam_im, "s5_log_dt": d_log_dt, "s5_b_re": d_b_re, "s5_b_im": d_b_im, "s5_c_re": d_c_re,
          "s5_c_im": d_c_im, "s5_d": d_s5_d, "s5_glu_b": d_glu_b, "norm_ffn_g": d_norm_ffn, "conv_b": d_conv_b,
          "norm_ple_g": d_norm_ple, "norm_final_g": d_norm_final}

    big_out = [_adam_shard("adam_" + n, own_sum[n], got3[n], W[n], M[n], V[n]) for n in _BIG]

    two_d = lambda a: a.reshape(1, -1) if a.ndim == 1 else a
    dense = lambda a: a.reshape(SG, -1) if a.ndim == 4 else two_d(a)
    g_sum = _allreduce_small([dense(gs[n].reshape(W[n].shape)) for n in _SMALL])
    small_out = _adam_small([g.reshape(two_d(W[n]).shape) for g, n in zip(g_sum, _SMALL)], [two_d(W[n]) for n in _SMALL],
                            [two_d(M[n]) for n in _SMALL], [two_d(V[n]) for n in _SMALL])

    res = {}
    for k in range(4):
        d = {n: big_out[i][k] for i, n in enumerate(_BIG)}
        d.update({n: small_out[k * len(_SMALL) + i].reshape(W[n].shape) for i, n in enumerate(_SMALL)})
        res[k] = d
    loss = lax.psum(loss_part[0, 0], ("x", "y", "c"))
    return (loss, grad_x[None], *[res[0][n] for n in _ORDER], *[res[1][n] for n in _ORDER],
            *[res[2][n] for n in _ORDER], *[res[3][n] for n in _ORDER])
```

```python
import functools
import math

import jax
import jax.numpy as jnp
from jax import lax
from jax.experimental import pallas as pl
from jax.experimental.pallas import tpu as pltpu

f32 = jnp.float32
bf16 = jnp.bfloat16
MESH = pl.DeviceIdType.MESH

N_DEV = 8
D = 1024
HW = 512
HD = 128
NH = 4
CH = 64
SW = 512
SG = 32
SP = 16
SN = 64
SL = SG * SN
NST = 4
STW = SL // NST
DFF = 2816
PLE = 256
EPS = 1e-6
LANES = 1024
VMEM_LIMIT = 56 * 1024 * 1024

ADAM_LR, ADAM_B1, ADAM_B2, ADAM_EPS, ADAM_WD, ADAM_STEP = 0.001, 0.9, 0.999, 1e-08, 0.01, 10


def _pc(body, **kw):
    return pl.pallas_call(body, **kw)


def _params(n_axes=1, **kw):
    return pltpu.CompilerParams(dimension_semantics=("arbitrary",) * n_axes, vmem_limit_bytes=VMEM_LIMIT, **kw)


def _whole(shape):
    nd = len(shape)
    return pl.BlockSpec(shape, lambda *_: (0,) * nd, pipeline_mode=pl.Buffered(1))


def _acc(shape):
    nd = len(shape)
    return pl.BlockSpec(shape, lambda *_: (0,) * nd)


def _dot(a, b):
    return jnp.dot(a.astype(bf16), b.astype(bf16), preferred_element_type=f32)


def _dot_nt(a, b):
    return lax.dot_general(a.astype(bf16), b.astype(bf16), (((1,), (1,)), ((), ())), preferred_element_type=f32)


def _dot_tn(a, b):
    return lax.dot_general(a.astype(bf16), b.astype(bf16), (((0,), (0,)), ((), ())), preferred_element_type=f32)


def _sig(x):
    return jax.nn.sigmoid(x)


def _dsilu(z, s):
    return s * (1.0 + z * (1.0 - s))


_GC = math.sqrt(2.0 / math.pi)


def _gelu_and_grad(y):
    t = jnp.tanh(_GC * (y + 0.044715 * y * y * y))
    g = 0.5 * y * (1.0 + t)
    dg = 0.5 * (1.0 + t) + 0.5 * y * (1.0 - t * t) * _GC * (1.0 + 3.0 * 0.044715 * y * y)
    return g, dg


def _rms(x):
    r = lax.rsqrt(jnp.mean(x * x, axis=-1, keepdims=True) + EPS)
    return x * r, r


def _rms_bwd(dy, xh, r, g):
    dxh = dy * g
    dx = r * (dxh - xh * jnp.mean(dxh * xh, axis=-1, keepdims=True))
    return dx, dy * xh


def _colsum(x):
    return jnp.sum(x, axis=0, keepdims=True)


def _in_proj(x, g, w, ag_shard):
    T = x.shape[0]
    tm = 256
    nt = T // tm

    def body(x_ref, g_ref, w_ref, ag_ref, h_ref, hg_ref, u_ref, gt_ref, ago_ref, send_sems, recv_sems, local_sem):
        i = pl.program_id(0)
        start, forward, finish = _ag_steps(ag_ref, ago_ref, send_sems, recv_sems, local_sem)
        pl.when(i == 0)(start)
        xh, _ = _rms(x_ref[...])
        h = (xh * g_ref[...]).astype(bf16)
        h_ref[...] = h
        hg_ref[...] = jnp.dot(h, w_ref[:, 0:4 * HW], preferred_element_type=f32)
        u_ref[...] = jnp.dot(h, w_ref[:, 4 * HW:4 * HW + SW], preferred_element_type=f32)
        gt_ref[...] = jnp.dot(h, w_ref[:, 4 * HW + SW:], preferred_element_type=f32)
        pl.when(i == nt // 2)(forward)
        pl.when(i == nt - 1)(finish)

    row = lambda n: pl.BlockSpec((tm, n), lambda i: (i, 0))
    return _pc(
        body, name="in_proj", grid=(nt,),
        in_specs=[row(D), _whole((1, D)), _whole(w.shape), ANY],
        out_specs=[row(D), row(4 * HW), row(SW), row(2 * D), ANY],
        out_shape=[jax.ShapeDtypeStruct((T, D), bf16), jax.ShapeDtypeStruct((T, 4 * HW), f32),
                   jax.ShapeDtypeStruct((T, SW), f32), jax.ShapeDtypeStruct((T, 2 * D), f32),
                   jax.ShapeDtypeStruct((N_DEV, *ag_shard.shape), ag_shard.dtype)],
        scratch_shapes=list(AG_SEMS),
        compiler_params=_params(),
    )(x, g, w, ag_shard)


HG_NC = 4


def _tri_matmul(tri, x):
    hi = x.astype(bf16)
    r1 = x - hi.astype(f32)
    mid = r1.astype(bf16)
    lo = (r1 - mid.astype(f32)).astype(bf16)
    n = x.shape[1]
    out = jnp.dot(tri.astype(bf16), jnp.concatenate([hi, mid, lo], axis=1), preferred_element_type=f32)
    return out[:, 0:n] + out[:, n:2 * n] + out[:, 2 * n:3 * n]


HG_TM = HG_NC * CH


def _chunk_tri(upper):
    r_i = lax.broadcasted_iota(jnp.int32, (HG_TM, HG_TM), 0)
    c_i = lax.broadcasted_iota(jnp.int32, (HG_TM, HG_TM), 1)
    same = (r_i // CH) == (c_i // CH)
    return same & ((c_i >= r_i) if upper else (r_i >= c_i))


def _heads(x3):
    n = x3.shape[2] // NH
    return jnp.concatenate([x3[:, :, h * n:(h + 1) * n] for h in range(NH)], axis=0)


def _unheads(xb):
    return jnp.concatenate([xb[h * HG_NC:(h + 1) * HG_NC] for h in range(NH)], axis=2)


def _bdot(a, b, ca, cb):
    return lax.dot_general(a.astype(bf16), b.astype(bf16), (((ca,), (cb,)), ((0,), (0,))), preferred_element_type=f32)


def _hgrn_gates(lg, qr, fr):
    mx = jnp.max(lg, axis=0, keepdims=True)
    e = jnp.exp(lg - mx)
    lb = e[0:1, :] / (e[0:1, :] + e[1:2, :])
    sig = _sig(fr)
    f = lb + (1.0 - lb) * sig
    k = 1.0 - f
    b = _tri_matmul(_chunk_tri(False), jnp.log(f).reshape(HG_TM, HW)).reshape(HG_NC, CH, HW)
    bref = b[:, CH // 2:CH // 2 + 1, :]
    blast = b[:, CH - 1:CH, :]
    sq = _sig(qr)
    q = qr * sq
    e1 = jnp.exp(b - bref)
    e2 = jnp.exp(bref - b)
    e3 = jnp.exp(blast - b)
    e4 = jnp.exp(b)
    r_i = lax.broadcasted_iota(jnp.int32, (CH, CH), 0)
    c_i = lax.broadcasted_iota(jnp.int32, (CH, CH), 1)
    return dict(lb=lb, qr=qr, sq=sq, sig=sig, f=f, k=k, tril=(r_i >= c_i), e1=e1, e2=e2, e3=e3, e4=e4,
                qs=q * e1, ks=k * e2, kl=k * e3, qb=q * e4, dec=jnp.exp(blast))


def _hgrn_fwd(proj_hg, logits, ng4):
    T = proj_hg.shape[0]
    nch = T // CH
    tm = HG_NC * CH

    def body(q_ref, f_ref, i_ref, og_ref, lg_ref, ng_ref, out_ref, sprev_ref, st_ref):
        @pl.when(pl.program_id(0) == 0)
        def _():
            st_ref[...] = jnp.zeros_like(st_ref)

        three = lambda ref: ref[...].reshape(HG_NC, CH, HW)
        c = _hgrn_gates(lg_ref[...], three(q_ref), three(f_ref))
        qs, ks, kl, qb, dec = (_heads(c[n]) for n in ("qs", "ks", "kl", "qb", "dec"))
        vb = _heads(three(i_ref))
        p = jnp.where(c["tril"], _bdot(qs, ks, 2, 2), 0.0)
        ut = _bdot(vb, kl, 1, 1)
        sts = []
        for h in range(NH):
            st = st_ref[h]
            for ci in range(HG_NC):
                sts.append(st)
                sprev_ref[ci, h] = st
                st = dec[h * HG_NC + ci] * st + ut[h * HG_NC + ci]
            st_ref[h] = st
        o = _bdot(p, vb, 2, 1) + _bdot(qb, jnp.stack(sts), 2, 2)
        og = og_ref[...]
        out_ref[...] = (_unheads(_rms(o)[0]).reshape(HG_TM, HW) * ng_ref[...] * (og * _sig(og))).astype(bf16)

    col = lambda j: pl.BlockSpec((tm, HW), lambda n, j=j: (n, j))
    return _pc(
        body, name="hgrn_fwd", grid=(nch // HG_NC,),
        in_specs=[col(0), col(1), col(2), col(3), _whole((2, HW)), _whole((1, HW))],
        out_specs=[pl.BlockSpec((tm, HW), lambda n: (n, 0)),
                   pl.BlockSpec((HG_NC, NH, HD, HD), lambda n: (n, 0, 0, 0))],
        out_shape=[jax.ShapeDtypeStruct((T, HW), bf16), jax.ShapeDtypeStruct((nch, NH, HD, HD), f32)],
        scratch_shapes=[pltpu.VMEM((NH, HD, HD), f32)],
        compiler_params=_params(),
    )(proj_hg, proj_hg, proj_hg, proj_hg, logits, ng4)


def _hgrn_bwd(proj_hg, logits, ng4, sprev, d_out, exch=None):
    T = proj_hg.shape[0]
    nch = T // CH
    tm = HG_NC * CH
    nst = nch // HG_NC

    def body(q_ref, f_ref, i_ref, og_ref, lg_ref, ng_ref, sp_ref, do_ref, dp_ref, dlb_ref, dng_ref, gt_ref):
        @pl.when(pl.program_id(0) == 0)
        def _():
            gt_ref[...] = jnp.zeros_like(gt_ref)
            dlb_ref[...] = jnp.zeros_like(dlb_ref)
            dng_ref[...] = jnp.zeros_like(dng_ref)

        three = lambda x: x.reshape(HG_NC, CH, HW)
        flat = lambda x: x.reshape(HG_TM, HW)
        c = _hgrn_gates(lg_ref[...], three(q_ref[...]), three(f_ref[...]))
        tril = c["tril"]
        ng = ng_ref[:, 0:HD]
        og = og_ref[...]
        sog = _sig(og)
        d_gated = do_ref[...]
        qs, ks, kl, qb, dec = (_heads(c[n]) for n in ("qs", "ks", "kl", "qb", "dec"))
        vb = _heads(three(i_ref[...]))
        spb = jnp.stack([sp_ref[ci, h] for h in range(NH) for ci in range(HG_NC)])
        p = jnp.where(tril, _bdot(qs, ks, 2, 2), 0.0)
        o = _bdot(p, vb, 2, 1) + _bdot(qb, spb, 2, 2)
        oh, r = _rms(o)
        d_o, dng_rows = _rms_bwd(_heads(three(d_gated * (og * sog))), oh, r, ng)
        dng_ref[...] += _colsum(jnp.sum(dng_rows, axis=0))
        dp = jnp.where(tril, _bdot(d_o, vb, 2, 2), 0.0)
        dst = _bdot(d_o, qb, 1, 1)
        gts = [None] * (NH * HG_NC)
        for h in range(NH):
            gt = gt_ref[h]
            for ci in reversed(range(HG_NC)):
                gts[h * HG_NC + ci] = gt
                gt = dst[h * HG_NC + ci] + dec[h * HG_NC + ci] * gt
            gt_ref[h] = gt
        gtb = jnp.stack(gts)
        dqs = _unheads(_bdot(dp, ks, 2, 1))
        dks = _unheads(_bdot(dp, qs, 1, 1))
        dkl = _unheads(_bdot(vb, gtb, 2, 1))
        dqb = _unheads(_bdot(d_o, spb, 2, 1))
        dv = _unheads(_bdot(p, d_o, 1, 1) + _bdot(kl, gtb, 2, 2))
        ddec = _unheads(jnp.sum(gtb * spb, axis=1, keepdims=True))
        dq = dqs * c["e1"] + dqb * c["e4"]
        dk = dks * c["e2"] + dkl * c["e3"]
        t_qs = dqs * c["qs"]
        t_ks = dks * c["ks"]
        t_kl = dkl * c["kl"]
        db = t_qs - t_ks - t_kl + dqb * c["qb"]
        dbref = jnp.sum(t_ks - t_qs, axis=1, keepdims=True)
        dblast = jnp.sum(t_kl, axis=1, keepdims=True) + ddec * c["dec"]
        row = lax.broadcasted_iota(jnp.int32, (HG_NC, CH, HW), 1)
        db = db + jnp.where(row == CH // 2, dbref, 0.0) + jnp.where(row == CH - 1, dblast, 0.0)
        df = three(_tri_matmul(_chunk_tri(True), flat(db))) / c["f"] - dk
        sig = c["sig"]
        dlb_ref[...] += _colsum(jnp.sum(df * (1.0 - sig), axis=0))
        dp_ref[:, 0:HW] = flat(dq * _dsilu(c["qr"], c["sq"]))
        dp_ref[:, HW:2 * HW] = flat(df * (1.0 - c["lb"]) * sig * (1.0 - sig))
        dp_ref[:, 2 * HW:3 * HW] = flat(dv)
        dp_ref[:, 3 * HW:4 * HW] = d_gated * flat(_unheads(oh * ng)) * _dsilu(og, sog)

    rev = lambda n: nst - 1 - n
    col = lambda j: pl.BlockSpec((tm, HW), lambda n, j=j: (rev(n), j))
    return _pc_behind(
        body, exch, nst, name="hgrn_bwd", grid=(nst,),
        in_specs=[col(0), col(1), col(2), col(3), _whole((2, HW)), _whole((1, HW)),
                  pl.BlockSpec((HG_NC, NH, HD, HD), lambda n: (rev(n), 0, 0, 0)),
                  pl.BlockSpec((tm, HW), lambda n: (rev(n), 0))],
        out_specs=[pl.BlockSpec((tm, 4 * HW), lambda n: (rev(n), 0)), _acc((1, HW)), _acc((1, HD))],
        out_shape=[jax.ShapeDtypeStruct((T, 4 * HW), f32), jax.ShapeDtypeStruct((1, HW), f32),
                   jax.ShapeDtypeStruct((1, HD), f32)],
        scratch_shapes=[pltpu.VMEM((NH, HD, HD), f32)],
        compiler_params=_params(), args=(proj_hg, proj_hg, proj_hg, proj_hg, logits, ng4, sprev, d_out))


S5_TM = 256
S5_SEG = 8
S5_STEPS = S5_TM // S5_SEG
NLT = SL // 128


def _s5_tables(a_ref, pw_ref, pseg_ref, descending):
    re, im = slice(0, SL), slice(SL, 2 * SL)

    def cmul(ar, ai, br, bi):
        return ar * br - ai * bi, ar * bi + ai * br

    pw_ref[0:1, :] = a_ref[...]
    m = 1
    while m < S5_STEPS:
        pr, pi = cmul(pw_ref[0:m, re], pw_ref[0:m, im], pw_ref[m - 1:m, re], pw_ref[m - 1:m, im])
        pw_ref[m:2 * m, re] = pr
        pw_ref[m:2 * m, im] = pi
        m *= 2
    base = S5_STEPS - 1
    if descending:
        pseg_ref[7:8, :] = pw_ref[base:base + 1, :]
        m = 1
        while m < 8:
            pr, pi = cmul(pseg_ref[8 - m:8, re], pseg_ref[8 - m:8, im], pseg_ref[8 - m:9 - m, re], pseg_ref[8 - m:9 - m, im])
            pseg_ref[8 - 2 * m:8 - m, re] = pr
            pseg_ref[8 - 2 * m:8 - m, im] = pi
            m *= 2
    else:
        pseg_ref[0:1, :] = pw_ref[base:base + 1, :]
        m = 1
        while m < 8:
            pr, pi = cmul(pseg_ref[0:m, re], pseg_ref[0:m, im], pseg_ref[m - 1:m, re], pseg_ref[m - 1:m, im])
            pseg_ref[m:2 * m, re] = pr
            pseg_ref[m:2 * m, im] = pi
            m *= 2


def _seg_rows(j):
    return pl.ds(j * S5_SEG, S5_SEG)


def _seg_perm(transpose=False):
    r_i = lax.broadcasted_iota(jnp.int32, (S5_TM, S5_TM), 0)
    c_i = lax.broadcasted_iota(jnp.int32, (S5_TM, S5_TM), 1)
    if transpose:
        r_i, c_i = c_i, r_i
    return c_i == S5_STEPS * (r_i % S5_SEG) + r_i // S5_SEG


def _scan_fwd(x3_ref, pw_ref, pseg_ref, carry_ref):
    row8 = lax.broadcasted_iota(jnp.int32, (S5_SEG, 128), 0)
    for lt in range(NLT):
        kr, ki = lt, NLT + lt
        lr, li = slice(lt * 128, (lt + 1) * 128), slice(SL + lt * 128, SL + (lt + 1) * 128)
        ar, ai = pw_ref[0:1, lr], pw_ref[0:1, li]
        sr = jnp.zeros((S5_SEG, 128), f32)
        si = jnp.zeros((S5_SEG, 128), f32)
        for j in range(S5_STEPS):
            sr, si = ar * sr - ai * si + x3_ref[kr, _seg_rows(j), :], ar * si + ai * sr + x3_ref[ki, _seg_rows(j), :]
            x3_ref[kr, _seg_rows(j), :] = sr
            x3_ref[ki, _seg_rows(j), :] = si
        for d in (1, 2, 4):
            pr, pi = pseg_ref[d - 1:d, lr], pseg_ref[d - 1:d, li]
            tr, ti = pltpu.roll(sr, d, 0), pltpu.roll(si, d, 0)
            m = row8 >= d
            sr, si = sr + jnp.where(m, pr * tr - pi * ti, 0.0), si + jnp.where(m, pr * ti + pi * tr, 0.0)
        c0r, c0i = carry_ref[7:8, lr], carry_ref[7:8, li]
        qr, qi = pseg_ref[:, lr], pseg_ref[:, li]
        sr, si = sr + qr * c0r - qi * c0i, si + qr * c0i + qi * c0r
        carry_ref[:, lr] = sr
        carry_ref[:, li] = si
        cr = jnp.where(row8 == 0, c0r, pltpu.roll(sr, 1, 0))
        ci = jnp.where(row8 == 0, c0i, pltpu.roll(si, 1, 0))
        for j in range(S5_STEPS):
            pr, pi = pw_ref[j:j + 1, lr], pw_ref[j:j + 1, li]
            x3_ref[kr, _seg_rows(j), :] = x3_ref[kr, _seg_rows(j), :] + pr * cr - pi * ci
            x3_ref[ki, _seg_rows(j), :] = x3_ref[ki, _seg_rows(j), :] + pr * ci + pi * cr


def _scan_bwd(g3_ref, x3_ref, xh_ref, first, pw_ref, pseg_ref, carry_ref, dar_ref, dai_ref):
    row8 = lax.broadcasted_iota(jnp.int32, (S5_SEG, 128), 0)
    for lt in range(NLT):
        kr, ki = lt, NLT + lt
        lr, li = slice(lt * 128, (lt + 1) * 128), slice(SL + lt * 128, SL + (lt + 1) * 128)
        ar, ai = pw_ref[0:1, lr], pw_ref[0:1, li]
        sr = jnp.zeros((S5_SEG, 128), f32)
        si = jnp.zeros((S5_SEG, 128), f32)
        for j in reversed(range(S5_STEPS)):
            sr, si = ar * sr + ai * si + g3_ref[kr, _seg_rows(j), :], ar * si - ai * sr + g3_ref[ki, _seg_rows(j), :]
            g3_ref[kr, _seg_rows(j), :] = sr
            g3_ref[ki, _seg_rows(j), :] = si
        for d in (1, 2, 4):
            pr, pi = pseg_ref[8 - d:9 - d, lr], pseg_ref[8 - d:9 - d, li]
            tr, ti = pltpu.roll(sr, 8 - d, 0), pltpu.roll(si, 8 - d, 0)
            m = row8 < 8 - d
            sr, si = sr + jnp.where(m, pr * tr + pi * ti, 0.0), si + jnp.where(m, pr * ti - pi * tr, 0.0)
        c0r, c0i = carry_ref[0:1, lr], carry_ref[0:1, li]
        qr, qi = pseg_ref[:, lr], pseg_ref[:, li]
        sr, si = sr + qr * c0r + qi * c0i, si + qr * c0i - qi * c0r
        carry_ref[:, lr] = sr
        carry_ref[:, li] = si
        cr = jnp.where(row8 == 7, c0r, pltpu.roll(sr, 7, 0))
        ci = jnp.where(row8 == 7, c0i, pltpu.roll(si, 7, 0))
        hr = jnp.where(first, 0.0, xh_ref[kr, 7:8, :])
        hi = jnp.where(first, 0.0, xh_ref[ki, 7:8, :])
        acc_r = jnp.zeros((S5_SEG, 128), f32)
        acc_i = jnp.zeros((S5_SEG, 128), f32)
        for j in range(S5_STEPS):
            pr, pi = pw_ref[S5_STEPS - 1 - j:S5_STEPS - j, lr], pw_ref[S5_STEPS - 1 - j:S5_STEPS - j, li]
            lam_r = g3_ref[kr, _seg_rows(j), :] + pr * cr + pi * ci
            lam_i = g3_ref[ki, _seg_rows(j), :] + pr * ci - pi * cr
            g3_ref[kr, _seg_rows(j), :] = lam_r
            g3_ref[ki, _seg_rows(j), :] = lam_i
            if j == 0:
                xpr = jnp.where(row8 == 0, hr, pltpu.roll(x3_ref[kr, _seg_rows(S5_STEPS - 1), :], 1, 0))
                xpi = jnp.where(row8 == 0, hi, pltpu.roll(x3_ref[ki, _seg_rows(S5_STEPS - 1), :], 1, 0))
            else:
                xpr = x3_ref[kr, _seg_rows(j - 1), :]
                xpi = x3_ref[ki, _seg_rows(j - 1), :]
            acc_r = acc_r + lam_r * xpr + lam_i * xpi
            acc_i = acc_i + lam_i * xpr - lam_r * xpi
        dar_ref[:, lr] += _colsum(acc_r)
        dai_ref[:, lr] += _colsum(acc_i)


def _strip(x3_ref, part, s):
    k0 = part * NLT + s * (STW // 128)
    return jnp.concatenate([x3_ref[k0 + q] for q in range(STW // 128)], axis=1)


def _s5_fwd(u, a_row, bdb, bdc, dskip, glu_w, glu_b, ag_shard):
    T = u.shape[0]
    tm = S5_TM
    nt = T // tm

    def body(u_ref, a_ref, bdb_ref, bdc_ref, ds_ref, gw_ref, gb_ref, ag_ref, x_ref, y_ref, o_ref, ago_ref,
             pw_ref, pseg_ref, carry_ref, send_sems, recv_sems, local_sem):
        i = pl.program_id(0)
        start, forward, finish = _ag_steps(ag_ref, ago_ref, send_sems, recv_sems, local_sem)
        pl.when(i == 0)(start)

        @pl.when(i == 0)
        def _():
            carry_ref[...] = jnp.zeros_like(carry_ref)
            _s5_tables(a_ref, pw_ref, pseg_ref, descending=False)

        uv = u_ref[...]
        ub = jnp.dot(_seg_perm().astype(bf16), uv.astype(bf16), preferred_element_type=f32).astype(bf16)
        for part in range(2):
            for s in range(NST):
                bu = jnp.dot(ub[:, s * 128:(s + 1) * 128], bdb_ref[part * NST + s], preferred_element_type=f32)
                for q in range(STW // 128):
                    x_ref[part * NLT + s * (STW // 128) + q] = bu[:, q * 128:(q + 1) * 128]
        _scan_fwd(x_ref, pw_ref, pseg_ref, carry_ref)
        ys = []
        for s in range(NST):
            acc = None
            for part in range(2):
                t = jnp.dot(_strip(x_ref, part, s).astype(bf16), bdc_ref[part * NST + s], preferred_element_type=f32)
                acc = t if acc is None else acc + t
            ys.append(acc)
        y = _tri_matmul(_seg_perm(transpose=True), jnp.concatenate(ys, axis=1)) + ds_ref[...] * uv
        y_ref[...] = y
        g, _ = _gelu_and_grad(y)
        z = jnp.dot(g.astype(bf16), gw_ref[...], preferred_element_type=f32) + gb_ref[...]
        o_ref[...] = (g * _sig(z)).astype(bf16)
        pl.when(i == nt // 2)(forward)
        pl.when(i == nt - 1)(finish)

    row = lambda n: pl.BlockSpec((tm, n), lambda i: (i, 0))
    return _pc(
        body, name="s5_fwd", grid=(nt,),
        in_specs=[row(SW), _whole((1, 2 * SL)), _whole(bdb.shape), _whole(bdc.shape), _whole((1, SW)),
                  _whole((SW, SW)), _whole((1, SW)), ANY],
        out_specs=[pl.BlockSpec((2 * NLT, tm, 128), lambda i: (0, i, 0)), row(SW), row(SW), ANY],
        out_shape=[jax.ShapeDtypeStruct((2 * NLT, T, 128), f32), jax.ShapeDtypeStruct((T, SW), f32),
                   jax.ShapeDtypeStruct((T, SW), bf16),
                   jax.ShapeDtypeStruct((N_DEV, *ag_shard.shape), ag_shard.dtype)],
        scratch_shapes=[pltpu.VMEM((S5_STEPS, 2 * SL), f32), pltpu.VMEM((8, 2 * SL), f32), pltpu.VMEM((8, 2 * SL), f32)]
        + list(AG_SEMS),
        compiler_params=_params(),
    )(u, a_row, bdb, bdc, dskip, glu_w, glu_b, ag_shard)


def _s5_bwd(d_out, y, u, x, a_row, bdb, bdc, dskip, glu_w, glu_b, exch=None):
    T = u.shape[0]
    tm = S5_TM
    nt = T // tm

    def body(do_ref, y_ref, u_ref, x_ref, xh_ref, a_ref, bdb_ref, bdc_ref, ds_ref, gw_ref, gb_ref,
             du_ref, dglu_ref, dar_ref, dai_ref, dd_ref, dgb_ref, dbdb_ref, dbdc_ref, gs_ref, pw_ref, pseg_ref, carry_ref):
        i = pl.program_id(0)

        @pl.when(i == 0)
        def _():
            carry_ref[...] = jnp.zeros_like(carry_ref)
            _s5_tables(a_ref, pw_ref, pseg_ref, descending=True)
            dar_ref[...] = jnp.zeros_like(dar_ref)
            dai_ref[...] = jnp.zeros_like(dai_ref)
            dd_ref[...] = jnp.zeros_like(dd_ref)
            dgb_ref[...] = jnp.zeros_like(dgb_ref)
            dglu_ref[...] = jnp.zeros_like(dglu_ref)
            dbdb_ref[...] = jnp.zeros_like(dbdb_ref)
            dbdc_ref[...] = jnp.zeros_like(dbdc_ref)

        yv = y_ref[...]
        uv = u_ref[...]
        g, gp = _gelu_and_grad(yv)
        z = jnp.dot(g.astype(bf16), gw_ref[...], preferred_element_type=f32) + gb_ref[...]
        sg = _sig(z)
        do = do_ref[...].astype(f32)
        dz = do * g * sg * (1.0 - sg)
        dglu_ref[...] += _dot_tn(g, dz)
        dgb_ref[...] += _colsum(dz)
        dy = (do * sg + _dot_nt(dz, gw_ref[...])) * gp
        perm = _seg_perm().astype(bf16)
        dyb = jnp.dot(perm, dy.astype(bf16), preferred_element_type=f32).astype(bf16)
        dd_ref[...] += _colsum(dy * uv)
        for part in range(2):
            for s in range(NST):
                gx = lax.dot_general(dyb[:, s * 128:(s + 1) * 128], bdc_ref[part * NST + s], (((1,), (1,)), ((), ())),
                                     preferred_element_type=f32)
                for q in range(STW // 128):
                    gs_ref[part * NLT + s * (STW // 128) + q] = gx[:, q * 128:(q + 1) * 128]
        _scan_bwd(gs_ref, x_ref, xh_ref, i == nt - 1, pw_ref, pseg_ref, carry_ref, dar_ref, dai_ref)
        ub = jnp.dot(perm, uv.astype(bf16), preferred_element_type=f32).astype(bf16)
        dus = []
        for s in range(NST):
            acc = None
            for part in range(2):
                lv = _strip(gs_ref, part, s).astype(bf16)
                t = lax.dot_general(lv, bdb_ref[part * NST + s], (((1,), (1,)), ((), ())), preferred_element_type=f32)
                acc = t if acc is None else acc + t
                dbdb_ref[part * NST + s] += _dot_tn(ub[:, s * 128:(s + 1) * 128], lv)
                dbdc_ref[part * NST + s] += _dot_tn(_strip(x_ref, part, s), dyb[:, s * 128:(s + 1) * 128])
            dus.append(acc)
        du_ref[...] = _tri_matmul(_seg_perm(transpose=True), jnp.concatenate(dus, axis=1)) + dy * ds_ref[...]

    rev = lambda i: nt - 1 - i
    row = lambda n: pl.BlockSpec((tm, n), lambda i: (rev(i), 0))
    xblk = pl.BlockSpec((2 * NLT, tm, 128), lambda i: (0, rev(i), 0))
    halo = pl.BlockSpec((2 * NLT, 8, 128), lambda i: (0, jnp.maximum(rev(i) * (tm // 8) - 1, 0), 0))
    return _pc_behind(
        body, exch, nt, name="s5_bwd", grid=(nt,),
        in_specs=[row(SW), row(SW), row(SW), xblk, halo, _whole((1, 2 * SL)), _whole(bdb.shape), _whole(bdc.shape),
                  _whole((1, SW)), _whole((SW, SW)), _whole((1, SW))],
        out_specs=[row(SW), _acc((SW, SW)), _acc((1, SL)), _acc((1, SL)), _acc((1, SW)), _acc((1, SW)),
                   _acc(bdb.shape), _acc(bdc.shape)],
        out_shape=[jax.ShapeDtypeStruct((T, SW), f32), jax.ShapeDtypeStruct((SW, SW), f32),
                   jax.ShapeDtypeStruct((1, SL), f32), jax.ShapeDtypeStruct((1, SL), f32),
                   jax.ShapeDtypeStruct((1, SW), f32), jax.ShapeDtypeStruct((1, SW), f32),
                   jax.ShapeDtypeStruct(bdb.shape, f32), jax.ShapeDtypeStruct(bdc.shape, f32)],
        scratch_shapes=[pltpu.VMEM((2 * NLT, tm, 128), f32), pltpu.VMEM((S5_STEPS, 2 * SL), f32),
                        pltpu.VMEM((8, 2 * SL), f32), pltpu.VMEM((8, 2 * SL), f32)],
        compiler_params=_params(), args=(d_out, y, u, x, x, a_row, bdb, bdc, dskip, glu_w, glu_b))


def _mix_up(x, hg_o, s5_o, gates, w_bhg, w_bs5, w_out, g_ffn, w_up):
    T = x.shape[0]
    tm = 256

    def body(x_ref, hg_ref, s5_ref, gt_ref, wh_ref, ws_ref, wo_ref, g_ref, wu_ref, x1_ref, mg_ref, h2_ref, a_ref):
        yh = jnp.dot(hg_ref[...], wh_ref[...], preferred_element_type=f32)
        ys = jnp.dot(s5_ref[...], ws_ref[...], preferred_element_type=f32)
        merged = (_sig(gt_ref[:, 0:D]) * yh + _sig(gt_ref[:, D:2 * D]) * ys).astype(bf16)
        mg_ref[...] = merged
        x1 = x_ref[...] + jnp.dot(merged, wo_ref[...], preferred_element_type=f32)
        x1_ref[...] = x1
        xh, _ = _rms(x1)
        h2 = (xh * g_ref[...]).astype(bf16)
        h2_ref[...] = h2
        a_ref[...] = jnp.dot(h2, wu_ref[...], preferred_element_type=f32)

    row = lambda n: pl.BlockSpec((tm, n), lambda i: (i, 0))
    return _pc(
        body, name="mix_up", grid=(T // tm,),
        in_specs=[row(D), row(HW), row(SW), row(2 * D), _whole(w_bhg.shape), _whole(w_bs5.shape), _whole(w_out.shape),
                  _whole((1, D)), _whole(w_up.shape)],
        out_specs=[row(D), row(D), row(D), row(2 * DFF)],
        out_shape=[jax.ShapeDtypeStruct((T, D), f32), jax.ShapeDtypeStruct((T, D), bf16),
                   jax.ShapeDtypeStruct((T, D), bf16), jax.ShapeDtypeStruct((T, 2 * DFF), f32)],
        compiler_params=_params(),
    )(x, hg_o, s5_o, gates, w_bhg, w_bs5, w_out, g_ffn, w_up)


FFN_TM = 128
FFN_FS = 256


def _conv_gelu_blocks(a_ref, ah_ref, first, cw_ref, cb_ref, ga_ref, gb_ref, gated_ref):
    row8 = lax.broadcasted_iota(jnp.int32, (8, FFN_FS), 0)
    for s in range(DFF // FFN_FS):
        halves = (slice(s * FFN_FS, (s + 1) * FFN_FS), slice(DFF + s * FFN_FS, DFF + (s + 1) * FFN_FS))
        w = [[cw_ref[k:k + 1, ln] for k in range(3)] for ln in halves]
        bias = [cb_ref[:, ln] for ln in halves]
        prev = [jnp.where(first, 0.0, ah_ref[:, ln]) for ln in halves]
        p1 = [pltpu.roll(p, 1, 0) for p in prev]
        p2 = [pltpu.roll(p, 2, 0) for p in prev]
        for j in range(FFN_TM // 8):
            rows = slice(8 * j, 8 * j + 8)
            c = []
            for hf, ln in enumerate(halves):
                av = a_ref[rows, ln]
                r1, r2 = pltpu.roll(av, 1, 0), pltpu.roll(av, 2, 0)
                a1 = jnp.where(row8 >= 1, r1, p1[hf])
                a2 = jnp.where(row8 >= 2, r2, p2[hf])
                c.append(bias[hf] + w[hf][0] * a2 + w[hf][1] * a1 + w[hf][2] * av)
                p1[hf], p2[hf] = r1, r2
            gl, gp = _gelu_and_grad(c[0])
            ga_ref[rows, halves[0]] = c[1] * gp
            gb_ref[rows, halves[0]] = gl
            gated_ref[rows, halves[0]] = gl * c[1]


def _ffn_tail(a, conv_w, conv_b, w_down, x1, p, g_ple, w_pg, w_pp, g_fin, tgt):
    T = a.shape[0]
    tm = FFN_TM

    def body(a_ref, ah_ref, cw_ref, cb_ref, wd_ref, x1_ref, p_ref, gp_ref, wpg_ref, wpp_ref, gf_ref, t_ref,
             dx2_ref, gd_ref, ga_ref, gb_ref, dwpg_ref, dwpp_ref, loss_ref, dgf_ref, dgp_ref, gsc_ref):
        i = pl.program_id(0)

        @pl.when(i == 0)
        def _():
            loss_ref[...] = jnp.zeros_like(loss_ref)
            dgf_ref[...] = jnp.zeros_like(dgf_ref)
            dgp_ref[...] = jnp.zeros_like(dgp_ref)
            dwpg_ref[...] = jnp.zeros_like(dwpg_ref)
            dwpp_ref[...] = jnp.zeros_like(dwpp_ref)

        _conv_gelu_blocks(a_ref, ah_ref, i == 0, cw_ref, cb_ref, ga_ref, gb_ref, gsc_ref)
        gated = gsc_ref[...].astype(bf16)
        gd_ref[...] = gated
        x2 = x1_ref[...] + jnp.dot(gated, wd_ref[...], preferred_element_type=f32)
        xh2, r2 = _rms(x2)
        h3 = (xh2 * gp_ref[...]).astype(bf16)
        pg = _sig(jnp.dot(h3, wpg_ref[...], preferred_element_type=f32))
        pp = _dot(p_ref[...], wpp_ref[...])
        x3 = x2 + pg * pp
        xh3, r3 = _rms(x3)
        diff = xh3 * gf_ref[...] - t_ref[...]
        loss_ref[...] += 0.5 * jnp.sum(jnp.mean(diff * diff, axis=-1, keepdims=True), axis=0, keepdims=True)
        dy = diff * (1.0 / D)
        dx3, dgf_rows = _rms_bwd(dy, xh3, r3, gf_ref[...])
        dgf_ref[...] += _colsum(dgf_rows)
        dwpp_ref[...] += _dot_tn(p_ref[...], dx3 * pg)
        dz = (dx3 * pp * pg * (1.0 - pg)).astype(bf16)
        dwpg_ref[...] += _dot_tn(h3, dz)
        dh3 = _dot_nt(dz, wpg_ref[...])
        dx2n, dgp_rows = _rms_bwd(dh3, xh2, r2, gp_ref[...])
        dgp_ref[...] += _colsum(dgp_rows)
        dx2_ref[...] = dx3 + dx2n

    row = lambda n: pl.BlockSpec((tm, n), lambda i: (i, 0))
    halo = pl.BlockSpec((8, 2 * DFF), lambda i: (jnp.maximum(i * (tm // 8) - 1, 0), 0))
    return _pc(
        body, name="ffn_tail", grid=(T // tm,),
        in_specs=[row(2 * DFF), halo, _whole((3, 2 * DFF)), _whole((1, 2 * DFF)), _whole(w_down.shape), row(D), row(PLE),
                  _whole((1, D)), _whole(w_pg.shape), _whole(w_pp.shape), _whole((1, D)), row(D)],
        out_specs=[row(D), row(DFF), row(DFF), row(DFF), _acc(w_pg.shape), _acc(w_pp.shape),
                   _acc((1, 128)), _acc((1, D)), _acc((1, D))],
        out_shape=[jax.ShapeDtypeStruct((T, D), f32), jax.ShapeDtypeStruct((T, DFF), bf16),
                   jax.ShapeDtypeStruct((T, DFF), f32), jax.ShapeDtypeStruct((T, DFF), f32),
                   jax.ShapeDtypeStruct(w_pg.shape, f32), jax.ShapeDtypeStruct(w_pp.shape, f32),
                   jax.ShapeDtypeStruct((1, 128), f32), jax.ShapeDtypeStruct((1, D), f32), jax.ShapeDtypeStruct((1, D), f32)],
        scratch_shapes=[pltpu.VMEM((tm, DFF), f32)],
        compiler_params=_params(),
    )(a, a, conv_w, conv_b, w_down, x1, p, g_ple, w_pg, w_pp, g_fin, tgt)


def _ffn_bwd(dx2, a, g_a, g_b, conv_w, w_down, w_up, x1, g_ffn, exch=None):
    T = a.shape[0]
    tm = FFN_TM
    nt = T // tm

    def body(dx2_ref, a_ref, ga_ref, gb_ref, cw_ref, wd_ref, wu_ref, x1_ref, g_ref,
             da_ref, dx1_ref, dcw_ref, dcb_ref, dg_ref, carry_ref, dgd_ref, dasc_ref):
        i = pl.program_id(0)

        @pl.when(i == 0)
        def _():
            carry_ref[...] = jnp.zeros_like(carry_ref)
            dcw_ref[...] = jnp.zeros_like(dcw_ref)
            dcb_ref[...] = jnp.zeros_like(dcb_ref)
            dg_ref[...] = jnp.zeros_like(dg_ref)

        dx2 = dx2_ref[...]
        dgd_ref[...] = _dot_nt(dx2, wd_ref[...])
        row8 = lax.broadcasted_iota(jnp.int32, (8, FFN_FS), 0)
        for s in range(DFF // FFN_FS):
            src = slice(s * FFN_FS, (s + 1) * FFN_FS)
            halves = (src, slice(DFF + s * FFN_FS, DFF + (s + 1) * FFN_FS))
            w = [[cw_ref[k:k + 1, ln] for k in range(3)] for ln in halves]
            nxt = [carry_ref[:, ln] for ln in halves]
            n7 = [pltpu.roll(v, 7, 0) for v in nxt]
            n6 = [pltpu.roll(v, 6, 0) for v in nxt]
            acc = [[jnp.zeros((8, FFN_FS), f32) for _ in range(4)] for _ in halves]
            for j in reversed(range(tm // 8)):
                rows = slice(8 * j, 8 * j + 8)
                dg = dgd_ref[rows, src]
                for hf, (ln, saved) in enumerate(zip(halves, (ga_ref, gb_ref))):
                    dc = dg * saved[rows, src]
                    r7, r6 = pltpu.roll(dc, 7, 0), pltpu.roll(dc, 6, 0)
                    up1 = jnp.where(row8 < 7, r7, n7[hf])
                    up2 = jnp.where(row8 < 6, r6, n6[hf])
                    av = a_ref[rows, ln]
                    acc[hf][0] = acc[hf][0] + up2 * av
                    acc[hf][1] = acc[hf][1] + up1 * av
                    acc[hf][2] = acc[hf][2] + dc * av
                    acc[hf][3] = acc[hf][3] + dc
                    dasc_ref[rows, ln] = w[hf][2] * dc + w[hf][1] * up1 + w[hf][0] * up2
                    n7[hf], n6[hf] = r7, r6
                    if j == 0:
                        carry_ref[:, ln] = dc
            for hf, ln in enumerate(halves):
                for k in range(3):
                    dcw_ref[k:k + 1, ln] += _colsum(acc[hf][k])
                dcb_ref[:, ln] += _colsum(acc[hf][3])
        da = dasc_ref[...].astype(bf16)
        da_ref[...] = da
        dh2 = lax.dot_general(da, wu_ref[...], (((1,), (1,)), ((), ())), preferred_element_type=f32)
        xh, r = _rms(x1_ref[...])
        dx1n, dg_rows = _rms_bwd(dh2, xh, r, g_ref[...])
        dg_ref[...] += _colsum(dg_rows)
        dx1_ref[...] = dx2 + dx1n

    rev = lambda i: nt - 1 - i
    row = lambda n: pl.BlockSpec((tm, n), lambda i: (rev(i), 0))
    return _pc_behind(
        body, exch, nt, name="ffn_bwd", grid=(nt,),
        in_specs=[row(D), row(2 * DFF), row(DFF), row(DFF), _whole((3, 2 * DFF)), _whole(w_down.shape),
                  _whole(w_up.shape), row(D), _whole((1, D))],
        out_specs=[row(2 * DFF), row(D), _acc((3, 2 * DFF)), _acc((1, 2 * DFF)), _acc((1, D))],
        out_shape=[jax.ShapeDtypeStruct((T, 2 * DFF), bf16), jax.ShapeDtypeStruct((T, D), f32),
                   jax.ShapeDtypeStruct((3, 2 * DFF), f32), jax.ShapeDtypeStruct((1, 2 * DFF), f32),
                   jax.ShapeDtypeStruct((1, D), f32)],
        scratch_shapes=[pltpu.VMEM((8, 2 * DFF), f32), pltpu.VMEM((tm, DFF), f32), pltpu.VMEM((tm, 2 * DFF), f32)],
        compiler_params=_params(), args=(dx2, a, g_a, g_b, conv_w, w_down, w_up, x1, g_ffn))


def _mix_bwd(dx1, hg_o, s5_o, merged, gates, w_bhg, w_bs5, w_out, exch=None):
    T = dx1.shape[0]
    tm = 256

    def body(dx1_ref, hg_ref, s5_ref, mg_ref, gt_ref, wh_ref, ws_ref, wo_ref, dgt_ref, dhg_ref, ds5_ref, dwh_ref, dws_ref,
             dwo_ref):
        @pl.when(pl.program_id(0) == 0)
        def _():
            dwh_ref[...] = jnp.zeros_like(dwh_ref)
            dws_ref[...] = jnp.zeros_like(dws_ref)
            dwo_ref[...] = jnp.zeros_like(dwo_ref)

        dwo_ref[...] += _dot_tn(mg_ref[...], dx1_ref[...])
        dm = _dot_nt(dx1_ref[...], wo_ref[...])
        yh = jnp.dot(hg_ref[...], wh_ref[...], preferred_element_type=f32)
        ys = jnp.dot(s5_ref[...], ws_ref[...], preferred_element_type=f32)
        sh = _sig(gt_ref[:, 0:D])
        ss = _sig(gt_ref[:, D:2 * D])
        dgt_ref[:, 0:D] = dm * yh * sh * (1.0 - sh)
        dgt_ref[:, D:2 * D] = dm * ys * ss * (1.0 - ss)
        dyh = (dm * sh).astype(bf16)
        dys = (dm * ss).astype(bf16)
        dwh_ref[...] += _dot_tn(hg_ref[...], dyh)
        dws_ref[...] += _dot_tn(s5_ref[...], dys)
        dhg_ref[...] = lax.dot_general(dyh, wh_ref[...], (((1,), (1,)), ((), ())), preferred_element_type=f32)
        ds5_ref[...] = lax.dot_general(dys, ws_ref[...], (((1,), (1,)), ((), ())), preferred_element_type=f32)

    row = lambda n: pl.BlockSpec((tm, n), lambda i: (i, 0))
    return _pc_behind(
        body, exch, T // tm, name="mix_bwd", grid=(T // tm,),
        in_specs=[row(D), row(HW), row(SW), row(D), row(2 * D), _whole(w_bhg.shape), _whole(w_bs5.shape),
                  _whole(w_out.shape)],
        out_specs=[row(2 * D), row(HW), row(SW), _acc(w_bhg.shape), _acc(w_bs5.shape), _acc(w_out.shape)],
        out_shape=[jax.ShapeDtypeStruct((T, 2 * D), f32), jax.ShapeDtypeStruct((T, HW), f32), jax.ShapeDtypeStruct((T, SW), f32),
                   jax.ShapeDtypeStruct(w_bhg.shape, f32), jax.ShapeDtypeStruct(w_bs5.shape, f32),
                   jax.ShapeDtypeStruct(w_out.shape, f32)],
        compiler_params=_params(), args=(dx1, hg_o, s5_o, merged, gates, w_bhg, w_bs5, w_out))


def _in_bwd(d_hg, d_u, d_gt, x, dx1, w, g):
    T = x.shape[0]
    tm = 256

    def body(dhg_ref, du_ref, dgt_ref, x_ref, dx1_ref, w_ref, g_ref, dx_ref, dg_ref):
        @pl.when(pl.program_id(0) == 0)
        def _():
            dg_ref[...] = jnp.zeros_like(dg_ref)

        dh = (_dot_nt(dhg_ref[...], w_ref[:, 0:4 * HW]) + _dot_nt(du_ref[...], w_ref[:, 4 * HW:4 * HW + SW])
              + _dot_nt(dgt_ref[...], w_ref[:, 4 * HW + SW:]))
        xh, r = _rms(x_ref[...])
        dxn, dg_rows = _rms_bwd(dh, xh, r, g_ref[...])
        dg_ref[...] += _colsum(dg_rows)
        dx_ref[...] = dx1_ref[...] + dxn

    row = lambda n: pl.BlockSpec((tm, n), lambda i: (i, 0))
    return _pc(
        body, name="in_bwd", grid=(T // tm,),
        in_specs=[row(4 * HW), row(SW), row(2 * D), row(D), row(D), _whole(w.shape), _whole((1, D))],
        out_specs=[row(D), _acc((1, D))],
        out_shape=[jax.ShapeDtypeStruct((T, D), f32), jax.ShapeDtypeStruct((1, D), f32)],
        compiler_params=_params(),
    )(d_hg, d_u, d_gt, x, dx1, w, g)


def _wgrad(name, a, b, nj=None, a_blk=None, a_idx=None, b_blk=None, b_idx=None):
    T = a.shape[0]
    tm = 512
    dense = nj is None
    if dense:
        K, N = a.shape[1], b.shape[1]
        a_blk, a_idx = K, (lambda j: 0)
        b_blk = N
        while K * b_blk * 4 > 6 * 1024 * 1024 and b_blk % 256 == 0:
            b_blk //= 2
        nj, b_idx = N // b_blk, (lambda j: j)

    def body(a_ref, b_ref, o_ref):
        @pl.when(pl.program_id(1) == 0)
        def _():
            o_ref[...] = jnp.zeros_like(o_ref)

        o_ref[0] += _dot_tn(a_ref[...], b_ref[...])

    out = _pc(
        body, name=name, grid=(nj, T // tm),
        in_specs=[pl.BlockSpec((tm, a_blk), lambda j, i: (i, a_idx(j))), pl.BlockSpec((tm, b_blk), lambda j, i: (i, b_idx(j)))],
        out_specs=pl.BlockSpec((1, a_blk, b_blk), lambda j, i: (j, 0, 0)),
        out_shape=jax.ShapeDtypeStruct((nj, a_blk, b_blk), f32),
        compiler_params=_params(2),
    )(a, b)
    if dense:
        return out[0] if nj == 1 else jnp.transpose(out, (1, 0, 2)).reshape(a.shape[1], b.shape[1])
    return out


ANY = pl.BlockSpec(memory_space=pl.ANY)


AG_SEMS = [pltpu.SemaphoreType.DMA((7,)), pltpu.SemaphoreType.DMA((7,)), pltpu.SemaphoreType.DMA]


def _ag_steps(x_ref, out_ref, send_sems, recv_sems, local_sem):
    x, y, c = lax.axis_index("x"), lax.axis_index("y"), lax.axis_index("c")
    me, sibling = (x, y, c), (x, y, 1 - c)
    chips = [(1 - x, y), (x, 1 - y), (1 - x, 1 - y)]

    def slot(px, py, pc):
        return out_ref.at[4 * px + 2 * py + pc]

    def copy(k, block, to, src=None):
        return pltpu.make_async_remote_copy(
            src_ref=slot(*block) if src is None else src, dst_ref=slot(*block),
            send_sem=send_sems.at[k], recv_sem=recv_sems.at[k], device_id=to, device_id_type=MESH)

    def mine():
        return pltpu.make_async_copy(x_ref, slot(*me), local_sem)

    def first():
        return [copy(0, me, sibling, src=x_ref)] + [copy(1 + j, me, (*chip, c), src=x_ref) for j, chip in enumerate(chips)]

    def passed():
        return [copy(4 + j, (*chip, c), sibling) for j, chip in enumerate(chips)]

    def start():
        mine().start()
        for cp in first():
            cp.start()

    def forward():
        for j, (chip, cp) in enumerate(zip(chips, passed())):
            copy(1 + j, (*chip, c), me).wait_recv()
            cp.start()

    def finish():
        copy(0, sibling, me).wait_recv()
        for j, chip in enumerate(chips):
            copy(4 + j, (*chip, 1 - c), me).wait_recv()
        for cp in first() + passed():
            cp.wait_send()
        mine().wait()

    return start, forward, finish


def _all_gather(name, shard):
    R, C = shard.shape

    def body(x_ref, out_ref, send_sems, recv_sems, local_sem):
        for phase in _ag_steps(x_ref, out_ref, send_sems, recv_sems, local_sem):
            phase()

    return _pc(
        body, name=name, in_specs=[ANY], out_specs=ANY,
        out_shape=jax.ShapeDtypeStruct((N_DEV, R, C), shard.dtype), scratch_shapes=list(AG_SEMS),
    )(shard)


class _Exchange:
    def __init__(self, kind, arrays):
        self.kind, self.arrays, self.n = kind, list(arrays), len(arrays)
        self.per = 4 if kind == "sibling" else 3
        tail = (lambda a: a.shape[2:]) if kind == "sibling" else (lambda a: a.shape[1:])
        self.out_shape = [jax.ShapeDtypeStruct((self.per, *tail(a)), a.dtype) for a in self.arrays]
        self.scratch = [pltpu.SemaphoreType.DMA((self.per * self.n,)), pltpu.SemaphoreType.DMA((self.per * self.n,))]

    def steps(self, in_refs, out_refs, send_sems, recv_sems):
        x, y, c = lax.axis_index("x"), lax.axis_index("y"), lax.axis_index("c")
        chips = [(1 - x, y), (x, 1 - y), (1 - x, 1 - y)]

        def copies():
            cps = []
            for i, (src, dst) in enumerate(zip(in_refs, out_refs)):
                for k in range(self.per):
                    if self.kind == "sibling":
                        s, to = src.at[k, 1 - c], (x, y, 1 - c)
                    else:
                        s, to = src.at[2 * chips[k][0] + chips[k][1]], (*chips[k], c)
                    cps.append(pltpu.make_async_remote_copy(
                        src_ref=s, dst_ref=dst.at[k], send_sem=send_sems.at[self.per * i + k],
                        recv_sem=recv_sems.at[self.per * i + k], device_id=to, device_id_type=MESH))
            return cps

        def start():
            for cp in copies():
                cp.start()

        def finish():
            for cp in copies():
                cp.wait()

        return start, finish


def _exchange_call(name, exch):
    n = exch.n

    def body(*refs):
        start, finish = exch.steps(refs[:n], refs[n:2 * n], *refs[2 * n:])
        start()
        finish()

    return _pc(body, name=name, in_specs=[ANY] * n, out_specs=[ANY] * n, out_shape=exch.out_shape,
               scratch_shapes=exch.scratch)(*exch.arrays)


def _pc_behind(body, exch, nsteps, *, in_specs, out_specs, out_shape, args, scratch_shapes=(), **kw):
    if exch is None:
        return _pc(body, in_specs=in_specs, out_specs=out_specs, out_shape=out_shape, scratch_shapes=list(scratch_shapes),
                   **kw)(*args), None
    n_in, n_out, n_scr, ne = len(in_specs), len(out_specs), len(scratch_shapes), exch.n

    def wrapped(*refs):
        ins, e_in = refs[:n_in], refs[n_in:n_in + ne]
        o0 = n_in + ne
        outs, e_out = refs[o0:o0 + n_out], refs[o0 + n_out:o0 + n_out + ne]
        s0 = o0 + n_out + ne
        scr, sems = refs[s0:s0 + n_scr], refs[s0 + n_scr:]
        start, finish = exch.steps(e_in, e_out, *sems)
        i = pl.program_id(0)
        pl.when(i == 0)(start)
        body(*ins, *outs, *scr)
        pl.when(i == nsteps - 1)(finish)

    res = _pc(wrapped, in_specs=list(in_specs) + [ANY] * ne, out_specs=list(out_specs) + [ANY] * ne,
              out_shape=list(out_shape) + exch.out_shape, scratch_shapes=list(scratch_shapes) + exch.scratch,
              **kw)(*args, *exch.arrays)
    return res[:n_out], res[n_out:]


def _add_halves(name, g4, got, ids):
    _, _, K, c = g4.shape

    def body(ids_ref, a_ref, b_ref, p16_ref, own_ref):
        s = a_ref[0, 0] + b_ref[0]
        p16_ref[0] = s.astype(bf16)

        @pl.when(pl.program_id(0) == ids_ref[1])
        def _():
            own_ref[...] = s

    return _pc(
        body, name=name,
        grid_spec=pltpu.PrefetchScalarGridSpec(
            num_scalar_prefetch=1, grid=(4,),
            in_specs=[pl.BlockSpec((1, 1, K, c), lambda k, ids: (k, ids[0], 0, 0)),
                      pl.BlockSpec((1, K, c), lambda k, ids: (k, 0, 0))],
            out_specs=[pl.BlockSpec((1, K, c), lambda k, ids: (k, 0, 0)), pl.BlockSpec((K, c), lambda k, ids: (0, 0))]),
        out_shape=[jax.ShapeDtypeStruct((4, K, c), bf16), jax.ShapeDtypeStruct((K, c), f32)],
        compiler_params=_params(),
    )(ids, g4, got)


def _row_tile(K):
    for cand in (256, 176, 128, 64):
        if K % cand == 0:
            return cand
    return K


def _adam_shard(name, own, got3, w, m, v):
    K, c = own.shape
    tr = _row_tile(K)

    def body(own_ref, got_ref, w_ref, m_ref, v_ref, g_ref, d_ref, m2_ref, v2_ref):
        g = own_ref[...] + got_ref[0].astype(f32) + got_ref[1].astype(f32) + got_ref[2].astype(f32)
        g_ref[0] = g
        delta, m2, v2 = _adam_math(g, w_ref[0], m_ref[0], v_ref[0])
        d_ref[0] = delta
        m2_ref[0] = m2
        v2_ref[0] = v2

    blk = pl.BlockSpec((1, tr, c), lambda i: (0, i, 0))
    out = jax.ShapeDtypeStruct((1, K, c), f32)
    return _pc(
        body, name=name, grid=(K // tr,),
        in_specs=[pl.BlockSpec((tr, c), lambda i: (i, 0)), pl.BlockSpec((3, tr, c), lambda i: (0, i, 0)), blk, blk, blk],
        out_specs=[blk, blk, blk, blk], out_shape=[out, out, out, out], compiler_params=_params(),
    )(own, got3, w, m, v)


def _allreduce_small(grads):
    n = len(grads)
    shapes = [g.shape for g in grads]

    def body(*refs):
        g_refs, outs, recv = refs[0:n], refs[n:2 * n], refs[2 * n:5 * n]
        send_sems, recv_sems = refs[5 * n:]
        x, y, c = lax.axis_index("x"), lax.axis_index("y"), lax.axis_index("c")
        peers = [(x, y, 1 - c), (1 - x, y, c), (x, 1 - y, c)]
        for i in range(n):
            outs[i][...] = g_refs[i][...]
        for s, peer in enumerate(peers):
            cps = [pltpu.make_async_remote_copy(src_ref=outs[i], dst_ref=recv[s * n + i], send_sem=send_sems.at[s * n + i],
                                                recv_sem=recv_sems.at[s * n + i], device_id=peer, device_id_type=MESH)
                   for i in range(n)]
            for cp in cps:
                cp.start()
            for cp in cps:
                cp.wait()
            for i in range(n):
                outs[i][...] = outs[i][...] + recv[s * n + i][...]

    return _pc(
        body, name="allreduce_small", grid=(1,), in_specs=[_whole(s) for s in shapes], out_specs=[_acc(s) for s in shapes],
        out_shape=[jax.ShapeDtypeStruct(s, f32) for s in shapes],
        scratch_shapes=[pltpu.VMEM(s, f32) for s in shapes] * 3
        + [pltpu.SemaphoreType.DMA((3 * n,)), pltpu.SemaphoreType.DMA((3 * n,))],
        compiler_params=_params(),
    )(*grads)


def _adam_small(grads, ws, ms, vs):
    n = len(grads)
    shapes = [g.shape for g in grads]

    def body(*refs):
        g_refs, w_refs, m_refs, v_refs = refs[0:n], refs[n:2 * n], refs[2 * n:3 * n], refs[3 * n:4 * n]
        outs = refs[4 * n:8 * n]
        for i in range(n):
            g = g_refs[i][...]
            delta, m2, v2 = _adam_math(g, w_refs[i][...], m_refs[i][...], v_refs[i][...])
            outs[i][...] = g
            outs[n + i][...] = delta
            outs[2 * n + i][...] = m2
            outs[3 * n + i][...] = v2

    return _pc(
        body, name="adam_small", grid=(1,), in_specs=[_whole(s) for s in shapes] * 4, out_specs=[_acc(s) for s in shapes] * 4,
        out_shape=[jax.ShapeDtypeStruct(s, f32) for s in shapes] * 4, compiler_params=_params(),
    )(*grads, *ws, *ms, *vs)


def _adam_math(g, w, m, v):
    m2 = ADAM_B1 * m + (1.0 - ADAM_B1) * g
    v2 = ADAM_B2 * v + (1.0 - ADAM_B2) * (g * g)
    m_hat = m2 / (1.0 - ADAM_B1 ** ADAM_STEP)
    v_hat = v2 / (1.0 - ADAM_B2 ** ADAM_STEP)
    delta = -ADAM_LR * (m_hat / (jnp.sqrt(v_hat) + ADAM_EPS) + ADAM_WD * w)
    return delta, m2, v2


def _pack(arrs, dtype, row_mult):
    rows = []
    for a in arrs:
        flat = a.reshape(-1).astype(dtype)
        pad = (-flat.shape[0]) % LANES
        if pad:
            flat = jnp.concatenate([flat, jnp.zeros((pad,), dtype)])
        rows.append(flat.reshape(-1, LANES))
    out = jnp.concatenate(rows, axis=0)
    pad = (-out.shape[0]) % row_mult
    if pad:
        out = jnp.concatenate([out, jnp.zeros((pad, LANES), dtype)], axis=0)
    return out


def _unpack(buf, shapes):
    lead = buf.shape[:-2]
    outs, r = [], 0
    for shp in shapes:
        n = math.prod(shp)
        nr = -(-n // LANES)
        piece = buf[..., r:r + nr, :].reshape(*lead, nr * LANES)[..., :n]
        outs.append(piece.reshape(*lead, *shp))
        r += nr
    return outs


def _to_slabs(full, axis):
    shp = full.shape
    n = shp[axis] // N_DEV
    return jnp.moveaxis(full.reshape(*shp[:axis], N_DEV, n, *shp[axis + 1:]), axis, 0)


def _from_slabs(slabs, axis):
    t = jnp.moveaxis(slabs, 0, axis)
    shp = t.shape
    return t.reshape(*shp[:axis], shp[axis] * shp[axis + 1], *shp[axis + 2:])


def _s5_discretise(lam_re, lam_im, log_dt, b_re, b_im):
    dt = jnp.exp(log_dt)[:, None]
    mag = jnp.exp(lam_re * dt)
    a_re = mag * jnp.cos(lam_im * dt)
    a_im = mag * jnp.sin(lam_im * dt)
    den = lam_re * lam_re + lam_im * lam_im
    coef_re = ((a_re - 1.0) * lam_re + a_im * lam_im) / den
    coef_im = (a_im * lam_re - (a_re - 1.0) * lam_im) / den
    bbar_re = coef_re[..., None] * b_re - coef_im[..., None] * b_im
    bbar_im = coef_re[..., None] * b_im + coef_im[..., None] * b_re
    return a_re, a_im, bbar_re, bbar_im


def _s5_operands(bbar_re, bbar_im, c_re, c_im):
    eye = jnp.eye(SG // NST, dtype=f32)

    def b_op(bb):
        return jnp.einsum("sgnq,gh->sgqhn", bb.reshape(NST, SG // NST, SN, SP), eye).reshape(NST, 128, STW)

    def c_op(cc):
        return jnp.einsum("sgpn,gh->shngp", cc.reshape(NST, SG // NST, SP, SN), eye).reshape(NST, STW, 128)

    bdb = jnp.concatenate([b_op(bbar_re), b_op(bbar_im)], axis=0)
    bdc = jnp.concatenate([c_op(c_re), c_op(-c_im)], axis=0)
    return bdb, bdc


_BIG = ["w_in", "s5_glu_w", "w_branch_hg", "w_branch_s5", "w_out", "w_up", "w_down", "w_ple_gate", "w_ple_proj", "conv_w"]
_BIG_AXIS = {"w_in": 1, "s5_glu_w": 0, "w_branch_hg": 1, "w_branch_s5": 1, "w_out": 0, "w_up": 1, "w_down": 0,
             "w_ple_gate": 0, "w_ple_proj": 1, "conv_w": 1}
_SMALL = ["norm_mix_g", "hg_lb_logits", "hg_norm_g", "s5_lambda_re", "s5_lambda_im", "s5_log_dt", "s5_b_re", "s5_b_im",
          "s5_c_re", "s5_c_im", "s5_d", "s5_glu_b", "norm_ffn_g", "conv_b", "norm_ple_g", "norm_final_g"]
_ORDER = ["norm_mix_g", "w_in", "hg_lb_logits", "hg_norm_g", "s5_lambda_re", "s5_lambda_im", "s5_log_dt", "s5_b_re",
          "s5_b_im", "s5_c_re", "s5_c_im", "s5_d", "s5_glu_w", "s5_glu_b", "w_branch_hg", "w_branch_s5", "w_out",
          "norm_ffn_g", "w_up", "conv_w", "conv_b", "w_down", "norm_ple_g", "w_ple_gate", "w_ple_proj", "norm_final_g"]


def kernel(x, p, norm_mix_g, w_in, hg_lb_logits, hg_norm_g, s5_lambda_re, s5_lambda_im, s5_log_dt, s5_b_re, s5_b_im, s5_c_re, s5_c_im, s5_d, s5_glu_w, s5_glu_b, w_branch_hg, w_branch_s5, w_out, norm_ffn_g, w_up, conv_w, conv_b, w_down, norm_ple_g, w_ple_gate, w_ple_proj, norm_final_g, loss_target, m_norm_mix_g, m_w_in, m_hg_lb_logits, m_hg_norm_g, m_s5_lambda_re, m_s5_lambda_im, m_s5_log_dt, m_s5_b_re, m_s5_b_im, m_s5_c_re, m_s5_c_im, m_s5_d, m_s5_glu_w, m_s5_glu_b, m_w_branch_hg, m_w_branch_s5, m_w_out, m_norm_ffn_g, m_w_up, m_conv_w, m_conv_b, m_w_down, m_norm_ple_g, m_w_ple_gate, m_w_ple_proj, m_norm_final_g, v_norm_mix_g, v_w_in, v_hg_lb_logits, v_hg_norm_g, v_s5_lambda_re, v_s5_lambda_im, v_s5_log_dt, v_s5_b_re, v_s5_b_im, v_s5_c_re, v_s5_c_im, v_s5_d, v_s5_glu_w, v_s5_glu_b, v_w_branch_hg, v_w_branch_s5, v_w_out, v_norm_ffn_g, v_w_up, v_conv_w, v_conv_b, v_w_down, v_norm_ple_g, v_w_ple_gate, v_w_ple_proj, v_norm_final_g):
    W = dict(norm_mix_g=norm_mix_g, w_in=w_in, hg_lb_logits=hg_lb_logits, hg_norm_g=hg_norm_g, s5_lambda_re=s5_lambda_re, s5_lambda_im=s5_lambda_im, s5_log_dt=s5_log_dt, s5_b_re=s5_b_re, s5_b_im=s5_b_im, s5_c_re=s5_c_re, s5_c_im=s5_c_im, s5_d=s5_d, s5_glu_w=s5_glu_w, s5_glu_b=s5_glu_b, w_branch_hg=w_branch_hg, w_branch_s5=w_branch_s5, w_out=w_out, norm_ffn_g=norm_ffn_g, w_up=w_up, conv_w=conv_w, conv_b=conv_b, w_down=w_down, norm_ple_g=norm_ple_g, w_ple_gate=w_ple_gate, w_ple_proj=w_ple_proj, norm_final_g=norm_final_g)
    M = dict(norm_mix_g=m_norm_mix_g, w_in=m_w_in, hg_lb_logits=m_hg_lb_logits, hg_norm_g=m_hg_norm_g, s5_lambda_re=m_s5_lambda_re, s5_lambda_im=m_s5_lambda_im, s5_log_dt=m_s5_log_dt, s5_b_re=m_s5_b_re, s5_b_im=m_s5_b_im, s5_c_re=m_s5_c_re, s5_c_im=m_s5_c_im, s5_d=m_s5_d, s5_glu_w=m_s5_glu_w, s5_glu_b=m_s5_glu_b, w_branch_hg=m_w_branch_hg, w_branch_s5=m_w_branch_s5, w_out=m_w_out, norm_ffn_g=m_norm_ffn_g, w_up=m_w_up, conv_w=m_conv_w, conv_b=m_conv_b, w_down=m_w_down, norm_ple_g=m_norm_ple_g, w_ple_gate=m_w_ple_gate, w_ple_proj=m_w_ple_proj, norm_final_g=m_norm_final_g)
    V = dict(norm_mix_g=v_norm_mix_g, w_in=v_w_in, hg_lb_logits=v_hg_lb_logits, hg_norm_g=v_hg_norm_g, s5_lambda_re=v_s5_lambda_re, s5_lambda_im=v_s5_lambda_im, s5_log_dt=v_s5_log_dt, s5_b_re=v_s5_b_re, s5_b_im=v_s5_b_im, s5_c_re=v_s5_c_re, s5_c_im=v_s5_c_im, s5_d=v_s5_d, s5_glu_w=v_s5_glu_w, s5_glu_b=v_s5_glu_b, w_branch_hg=v_w_branch_hg, w_branch_s5=v_w_branch_s5, w_out=v_w_out, norm_ffn_g=v_norm_ffn_g, w_up=v_w_up, conv_w=v_conv_w, conv_b=v_conv_b, w_down=v_w_down, norm_ple_g=v_norm_ple_g, w_ple_gate=v_w_ple_gate, w_ple_proj=v_w_ple_proj, norm_final_g=v_norm_final_g)

    shard2 = {n: W[n][0] for n in _BIG}
    conv_bits = lax.bitcast_convert_type(shard2["conv_w"], bf16)
    groups = [["w_in", "s5_glu_w"], ["w_branch_hg", "w_branch_s5", "w_out", "w_ple_gate", "w_ple_proj", "w_down"], ["w_up"]]
    packs = [_pack([shard2[n] for n in grp] + ([conv_bits] if k == 0 else []), bf16, 16) for k, grp in enumerate(groups)]
    full = {}

    def take(k, gathered):
        pieces = _unpack(gathered, [shard2[n].shape for n in groups[k]] + ([conv_bits.shape] if k == 0 else []))
        full.update({n: _from_slabs(pc, _BIG_AXIS[n]) for n, pc in zip(groups[k], pieces)})
        return pieces

    conv_w_full = _from_slabs(lax.bitcast_convert_type(take(0, _all_gather("ag_weights", packs[0]))[-1], f32), 1)

    xt = x[0]
    pt = p[0, 0]
    tgt = loss_target[0]
    T = xt.shape[0]
    lam_re, lam_im, log_dt = s5_lambda_re[0], s5_lambda_im[0], s5_log_dt[0]
    b_re, b_im, c_re, c_im = s5_b_re[0], s5_b_im[0], s5_c_re[0], s5_c_im[0]

    def s5_prep(lam_re, lam_im, log_dt, b_re, b_im, c_re, c_im):
        a_re, a_im, bbar_re, bbar_im = _s5_discretise(lam_re, lam_im, log_dt, b_re, b_im)
        bdb, bdc = _s5_operands(bbar_re, bbar_im, c_re, c_im)
        return a_re, a_im, bdb, bdc

    (a_re, a_im, bdb, bdc), s5_prep_vjp = jax.vjp(s5_prep, lam_re, lam_im, log_dt, b_re, b_im, c_re, c_im)
    a_row = jnp.concatenate([a_re.reshape(1, SL), a_im.reshape(1, SL)], axis=1)
    bdb_b, bdc_b = bdb.astype(bf16), bdc.astype(bf16)

    h1, proj_hg, u_raw, gates, gathered1 = _in_proj(xt, norm_mix_g, full["w_in"], packs[1])
    take(1, gathered1)
    ng4 = jnp.tile(hg_norm_g, (1, NH))
    hg_o, sprev = _hgrn_fwd(proj_hg, hg_lb_logits, ng4)
    x_st, y_s5, s5_o, gathered2 = _s5_fwd(u_raw, a_row, bdb_b, bdc_b, s5_d, full["s5_glu_w"], s5_glu_b, packs[2])
    take(2, gathered2)
    x1, merged, h2, a_up = _mix_up(xt, hg_o, s5_o, gates, full["w_branch_hg"], full["w_branch_s5"], full["w_out"],
                                   norm_ffn_g, full["w_up"])
    (dx2, gated, g_a, g_b, d_w_pg, d_w_pp, loss_part, d_norm_final, d_norm_ple) = _ffn_tail(
        a_up, conv_w_full, conv_b, full["w_down"], x1, pt, norm_ple_g, full["w_ple_gate"], full["w_ple_proj"],
        norm_final_g.reshape(1, D), tgt)

    ids = jnp.stack([lax.axis_index("c"), 2 * lax.axis_index("x") + lax.axis_index("y")]).astype(jnp.int32)
    gw, own_sum, got3 = {}, {}, {}

    def slabs(names):
        return [_to_slabs(gw[n], _BIG_AXIS[n]).reshape(4, 2, *shard2[n].shape) for n in names]

    def add_pairs(names, g4, got):
        sums = [_add_halves("rs_add_" + n, g, r, ids) for n, g, r in zip(names, g4, got)]
        own_sum.update({n: own for n, (_, own) in zip(names, sums)})
        return [p16 for p16, _ in sums]

    grp_a = ["w_down", "w_ple_gate", "w_ple_proj"]
    gw["w_down"] = _wgrad("wg_down", gated, dx2)
    gw["w_ple_gate"] = d_w_pg
    gw["w_ple_proj"] = d_w_pp
    g4_a = slabs(grp_a)
    (da_up, dx1, d_conv_w, d_conv_b, d_norm_ffn), got_a = _ffn_bwd(
        dx2, a_up, g_a, g_b, conv_w_full, full["w_down"], full["w_up"], x1, norm_ffn_g, exch=_Exchange("sibling", g4_a))
    p16_a = add_pairs(grp_a, g4_a, got_a)
    (d_gates, d_hg_o, d_s5_o, d_w_bhg, d_w_bs5, d_w_out), got3_a = _mix_bwd(
        dx1, hg_o, s5_o, merged, gates, full["w_branch_hg"], full["w_branch_s5"], full["w_out"], exch=_Exchange("chips", p16_a))
    got3.update(zip(grp_a, got3_a))

    grp_b = ["w_up", "w_out", "w_branch_hg", "w_branch_s5", "conv_w"]
    gw["w_up"] = _wgrad("wg_up", h2, da_up)
    gw["w_out"] = d_w_out
    gw["w_branch_hg"] = d_w_bhg
    gw["w_branch_s5"] = d_w_bs5
    gw["conv_w"] = d_conv_w
    g4_b = slabs(grp_b)
    (d_proj_hg, d_lb, d_hg_norm), got_b = _hgrn_bwd(proj_hg, hg_lb_logits, ng4, sprev, d_hg_o,
                                                     exch=_Exchange("sibling", g4_b))
    p16_b = add_pairs(grp_b, g4_b, got_b)
    (d_u, d_w_glu, d_a_re, d_a_im, d_s5_d, d_glu_b, d_bdb, d_bdc), got3_b = _s5_bwd(
        d_s5_o, y_s5, u_raw, x_st, a_row, bdb_b, bdc_b, s5_d, full["s5_glu_w"], s5_glu_b, exch=_Exchange("chips", p16_b))
    got3.update(zip(grp_b, got3_b))
    grad_x, d_norm_mix = _in_bwd(d_proj_hg, d_u, d_gates, xt, dx1, full["w_in"], norm_mix_g)

    grp_c = ["w_in", "s5_glu_w"]
    gw["w_in"] = jnp.concatenate([_wgrad("wg_in_hg", h1, d_proj_hg), _wgrad("wg_in_u", h1, d_u),
                                  _wgrad("wg_in_gates", h1, d_gates)], axis=1)
    gw["s5_glu_w"] = d_w_glu
    g4_c = slabs(grp_c)
    p16_c = add_pairs(grp_c, g4_c, _exchange_call("rs_sibling", _Exchange("sibling", g4_c)))
    got3.update(zip(grp_c, _exchange_call("rs_chips", _Exchange("chips", p16_c))))

    (d_lam_re, d_lam_im, d_log_dt, d_b_re, d_b_im, d_c_re, d_c_im) = s5_prep_vjp(
        (d_a_re.reshape(SG, SN), d_a_im.reshape(SG, SN), d_bdb, d_bdc))
    sm = jax.nn.softmax(hg_lb_logits, axis=0)
    d_l0 = d_lb[0] * sm[0] * sm[1]
    d_logits = jnp.stack([d_l0, -d_l0], axis=0)

    gs = {"norm_mix_g": d_norm_mix, "hg_lb_logits": d_logits, "hg_norm_g": d_hg_norm, "s5_lambda_re": d_lam_re,
          "s5_lambda_im": d_lam_im, "s5_log_dt": d_log_dt, "s5_b_re": d_b_re, "s5_b_im": d_b_im, "s5_c_re": d_c_re,
          "s5_c_im": d_c_im, "s5_d": d_s5_d, "s5_glu_b": d_glu_b, "norm_ffn_g": d_norm_ffn, "conv_b": d_conv_b,
          "norm_ple_g": d_norm_ple, "norm_final_g": d_norm_final}

    big_out = [_adam_shard("adam_" + n, own_sum[n], got3[n], W[n], M[n], V[n]) for n in _BIG]

    two_d = lambda a: a.reshape(1, -1) if a.ndim == 1 else a
    dense = lambda a: a.reshape(SG, -1) if a.ndim == 4 else two_d(a)
    g_sum = _allreduce_small([dense(gs[n].reshape(W[n].shape)) for n in _SMALL])
    small_out = _adam_small([g.reshape(two_d(W[n]).shape) for g, n in zip(g_sum, _SMALL)], [two_d(W[n]) for n in _SMALL],
                            [two_d(M[n]) for n in _SMALL], [two_d(V[n]) for n in _SMALL])

    res = {}
    for k in range(4):
        d = {n: big_out[i][k] for i, n in enumerate(_BIG)}
        d.update({n: small_out[k * len(_SMALL) + i].reshape(W[n].shape) for i, n in enumerate(_SMALL)})
        res[k] = d
    loss = lax.psum(loss_part[0, 0], ("x", "y", "c"))
    return (loss, grad_x[None], *[res[0][n] for n in _ORDER], *[res[1][n] for n in _ORDER],
            *[res[2][n] for n in _ORDER], *[res[3][n] for n in _ORDER])
```

```python
import math

import jax
import jax.numpy as jnp
from jax import lax
from jax.experimental import pallas as pl
from jax.experimental.pallas import tpu as pltpu

f32 = jnp.float32
bf16 = jnp.bfloat16
MESH = pl.DeviceIdType.MESH

N_DEV = 8
D = 1024
HW = 512
HD = 128
NH = 4
CH = 64
SW = 512
SG = 32
SP = 16
SN = 64
SL = SG * SN
NST = 4
STW = SL // NST
DFF = 2816
PLE = 256
EPS = 1e-6
LANES = 1024
VMEM_LIMIT = 56 * 1024 * 1024

ADAM_LR, ADAM_B1, ADAM_B2, ADAM_EPS, ADAM_WD, ADAM_STEP = 0.001, 0.9, 0.999, 1e-08, 0.01, 10


def _pc(body, **kw):
    return pl.pallas_call(body, **kw)


def _params(n_axes=1, **kw):
    return pltpu.CompilerParams(dimension_semantics=("arbitrary",) * n_axes, vmem_limit_bytes=VMEM_LIMIT, **kw)


def _whole(shape):
    nd = len(shape)
    return pl.BlockSpec(shape, lambda *_: (0,) * nd, pipeline_mode=pl.Buffered(1))


def _acc(shape):
    nd = len(shape)
    return pl.BlockSpec(shape, lambda *_: (0,) * nd)


def _dot(a, b):
    return jnp.dot(a.astype(bf16), b.astype(bf16), preferred_element_type=f32)


def _dot_nt(a, b):
    return lax.dot_general(a.astype(bf16), b.astype(bf16), (((1,), (1,)), ((), ())), preferred_element_type=f32)


def _dot_tn(a, b):
    return lax.dot_general(a.astype(bf16), b.astype(bf16), (((0,), (0,)), ((), ())), preferred_element_type=f32)


def _sig(x):
    return jax.nn.sigmoid(x)


def _dsilu(z, s):
    return s * (1.0 + z * (1.0 - s))


_GC = math.sqrt(2.0 / math.pi)


def _gelu_and_grad(y):
    t = jnp.tanh(_GC * (y + 0.044715 * y * y * y))
    g = 0.5 * y * (1.0 + t)
    dg = 0.5 * (1.0 + t) + 0.5 * y * (1.0 - t * t) * _GC * (1.0 + 3.0 * 0.044715 * y * y)
    return g, dg


def _rms(x):
    r = lax.rsqrt(jnp.mean(x * x, axis=-1, keepdims=True) + EPS)
    return x * r, r


def _rms_bwd(dy, xh, r, g):
    dxh = dy * g
    dx = r * (dxh - xh * jnp.mean(dxh * xh, axis=-1, keepdims=True))
    return dx, dy * xh


def _colsum(x):
    return jnp.sum(x, axis=0, keepdims=True)


def _in_proj(x, g, w, ag_shard):
    T = x.shape[0]
    tm = 256
    nt = T // tm

    def body(x_ref, g_ref, w_ref, ag_ref, h_ref, hg_ref, u_ref, gt_ref, ago_ref, send_sems, recv_sems, local_sem):
        i = pl.program_id(0)
        start, forward, finish = _ag_steps(ag_ref, ago_ref, send_sems, recv_sems, local_sem)
        pl.when(i == 0)(start)
        xh, _ = _rms(x_ref[...])
        h = (xh * g_ref[...]).astype(bf16)
        h_ref[...] = h
        hg_ref[...] = jnp.dot(h, w_ref[:, 0:4 * HW], preferred_element_type=f32)
        u_ref[...] = jnp.dot(h, w_ref[:, 4 * HW:4 * HW + SW], preferred_element_type=f32)
        gt_ref[...] = jnp.dot(h, w_ref[:, 4 * HW + SW:], preferred_element_type=f32)
        pl.when(i == (3 * nt) // 4)(forward)
        pl.when(i == nt - 1)(finish)

    row = lambda n: pl.BlockSpec((tm, n), lambda i: (i, 0))
    return _pc(
        body, name="in_proj", grid=(nt,),
        in_specs=[row(D), _whole((1, D)), _whole(w.shape), ANY],
        out_specs=[row(D), row(4 * HW), row(SW), row(2 * D), ANY],
        out_shape=[jax.ShapeDtypeStruct((T, D), bf16), jax.ShapeDtypeStruct((T, 4 * HW), f32),
                   jax.ShapeDtypeStruct((T, SW), f32), jax.ShapeDtypeStruct((T, 2 * D), f32),
                   jax.ShapeDtypeStruct((N_DEV, *ag_shard.shape), ag_shard.dtype)],
        scratch_shapes=list(AG_SEMS),
        compiler_params=_params(),
    )(x, g, w, ag_shard)


HG_NC = 4


def _tri_matmul(tri, x):
    hi = x.astype(bf16)
    r1 = x - hi.astype(f32)
    mid = r1.astype(bf16)
    lo = (r1 - mid.astype(f32)).astype(bf16)
    n = x.shape[1]
    out = jnp.dot(tri.astype(bf16), jnp.concatenate([hi, mid, lo], axis=1), preferred_element_type=f32)
    return out[:, 0:n] + out[:, n:2 * n] + out[:, 2 * n:3 * n]


HG_TM = HG_NC * CH


def _chunk_tri(upper):
    r_i = lax.broadcasted_iota(jnp.int32, (HG_TM, HG_TM), 0)
    c_i = lax.broadcasted_iota(jnp.int32, (HG_TM, HG_TM), 1)
    same = (r_i // CH) == (c_i // CH)
    return same & ((c_i >= r_i) if upper else (r_i >= c_i))


def _heads(x3):
    n = x3.shape[2] // NH
    return jnp.concatenate([x3[:, :, h * n:(h + 1) * n] for h in range(NH)], axis=0)


def _unheads(xb):
    return jnp.concatenate([xb[h * HG_NC:(h + 1) * HG_NC] for h in range(NH)], axis=2)


def _bdot(a, b, ca, cb):
    return lax.dot_general(a.astype(bf16), b.astype(bf16), (((ca,), (cb,)), ((0,), (0,))), preferred_element_type=f32)


def _hgrn_gates(lg, qr, fr):
    mx = jnp.max(lg, axis=0, keepdims=True)
    e = jnp.exp(lg - mx)
    lb = e[0:1, :] / (e[0:1, :] + e[1:2, :])
    sig = _sig(fr)
    f = lb + (1.0 - lb) * sig
    k = 1.0 - f
    b = _tri_matmul(_chunk_tri(False), jnp.log(f).reshape(HG_TM, HW)).reshape(HG_NC, CH, HW)
    bref = b[:, CH // 2:CH // 2 + 1, :]
    blast = b[:, CH - 1:CH, :]
    sq = _sig(qr)
    q = qr * sq
    e1 = jnp.exp(b - bref)
    e2 = jnp.exp(bref - b)
    e3 = jnp.exp(blast - b)
    e4 = jnp.exp(b)
    r_i = lax.broadcasted_iota(jnp.int32, (CH, CH), 0)
    c_i = lax.broadcasted_iota(jnp.int32, (CH, CH), 1)
    return dict(lb=lb, qr=qr, sq=sq, sig=sig, f=f, k=k, tril=(r_i >= c_i), e1=e1, e2=e2, e3=e3, e4=e4,
                qs=q * e1, ks=k * e2, kl=k * e3, qb=q * e4, dec=jnp.exp(blast))


def _hgrn_fwd(proj_hg, logits, ng4):
    T = proj_hg.shape[0]
    nch = T // CH
    tm = HG_NC * CH

    def body(q_ref, f_ref, i_ref, og_ref, lg_ref, ng_ref, out_ref, sprev_ref, st_ref):
        @pl.when(pl.program_id(0) == 0)
        def _():
            st_ref[...] = jnp.zeros_like(st_ref)

        three = lambda ref: ref[...].reshape(HG_NC, CH, HW)
        c = _hgrn_gates(lg_ref[...], three(q_ref), three(f_ref))
        qs, ks, kl, qb, dec = (_heads(c[n]) for n in ("qs", "ks", "kl", "qb", "dec"))
        vb = _heads(three(i_ref))
        p = jnp.where(c["tril"], _bdot(qs, ks, 2, 2), 0.0)
        ut = _bdot(vb, kl, 1, 1)
        sts = []
        for h in range(NH):
            st = st_ref[h]
            for ci in range(HG_NC):
                sts.append(st)
                sprev_ref[ci, h] = st
                st = dec[h * HG_NC + ci] * st + ut[h * HG_NC + ci]
            st_ref[h] = st
        o = _bdot(p, vb, 2, 1) + _bdot(qb, jnp.stack(sts), 2, 2)
        og = og_ref[...]
        out_ref[...] = (_unheads(_rms(o)[0]).reshape(HG_TM, HW) * ng_ref[...] * (og * _sig(og))).astype(bf16)

    col = lambda j: pl.BlockSpec((tm, HW), lambda n, j=j: (n, j))
    return _pc(
        body, name="hgrn_fwd", grid=(nch // HG_NC,),
        in_specs=[col(0), col(1), col(2), col(3), _whole((2, HW)), _whole((1, HW))],
        out_specs=[pl.BlockSpec((tm, HW), lambda n: (n, 0)),
                   pl.BlockSpec((HG_NC, NH, HD, HD), lambda n: (n, 0, 0, 0))],
        out_shape=[jax.ShapeDtypeStruct((T, HW), bf16), jax.ShapeDtypeStruct((nch, NH, HD, HD), f32)],
        scratch_shapes=[pltpu.VMEM((NH, HD, HD), f32)],
        compiler_params=_params(),
    )(proj_hg, proj_hg, proj_hg, proj_hg, logits, ng4)


def _hgrn_bwd(proj_hg, logits, ng4, sprev, d_out, exch=None):
    T = proj_hg.shape[0]
    nch = T // CH
    tm = HG_NC * CH
    nst = nch // HG_NC

    def body(q_ref, f_ref, i_ref, og_ref, lg_ref, ng_ref, sp_ref, do_ref, dp_ref, dlb_ref, dng_ref, gt_ref):
        @pl.when(pl.program_id(0) == 0)
        def _():
            gt_ref[...] = jnp.zeros_like(gt_ref)
            dlb_ref[...] = jnp.zeros_like(dlb_ref)
            dng_ref[...] = jnp.zeros_like(dng_ref)

        three = lambda x: x.reshape(HG_NC, CH, HW)
        flat = lambda x: x.reshape(HG_TM, HW)
        c = _hgrn_gates(lg_ref[...], three(q_ref[...]), three(f_ref[...]))
        tril = c["tril"]
        ng = ng_ref[:, 0:HD]
        og = og_ref[...]
        sog = _sig(og)
        d_gated = do_ref[...]
        qs, ks, kl, qb, dec = (_heads(c[n]) for n in ("qs", "ks", "kl", "qb", "dec"))
        vb = _heads(three(i_ref[...]))
        spb = jnp.stack([sp_ref[ci, h] for h in range(NH) for ci in range(HG_NC)])
        p = jnp.where(tril, _bdot(qs, ks, 2, 2), 0.0)
        o = _bdot(p, vb, 2, 1) + _bdot(qb, spb, 2, 2)
        oh, r = _rms(o)
        d_o, dng_rows = _rms_bwd(_heads(three(d_gated * (og * sog))), oh, r, ng)
        dng_ref[...] += _colsum(jnp.sum(dng_rows, axis=0))
        dp = jnp.where(tril, _bdot(d_o, vb, 2, 2), 0.0)
        dst = _bdot(d_o, qb, 1, 1)
        gts = [None] * (NH * HG_NC)
        for h in range(NH):
            gt = gt_ref[h]
            for ci in reversed(range(HG_NC)):
                gts[h * HG_NC + ci] = gt
                gt = dst[h * HG_NC + ci] + dec[h * HG_NC + ci] * gt
            gt_ref[h] = gt
        gtb = jnp.stack(gts)
        dqs = _unheads(_bdot(dp, ks, 2, 1))
        dks = _unheads(_bdot(dp, qs, 1, 1))
        dkl = _unheads(_bdot(vb, gtb, 2, 1))
        dqb = _unheads(_bdot(d_o, spb, 2, 1))
        dv = _unheads(_bdot(p, d_o, 1, 1) + _bdot(kl, gtb, 2, 2))
        ddec = _unheads(jnp.sum(gtb * spb, axis=1, keepdims=True))
        dq = dqs * c["e1"] + dqb * c["e4"]
        dk = dks * c["e2"] + dkl * c["e3"]
        t_qs = dqs * c["qs"]
        t_ks = dks * c["ks"]
        t_kl = dkl * c["kl"]
        db = t_qs - t_ks - t_kl + dqb * c["qb"]
        dbref = jnp.sum(t_ks - t_qs, axis=1, keepdims=True)
        dblast = jnp.sum(t_kl, axis=1, keepdims=True) + ddec * c["dec"]
        row = lax.broadcasted_iota(jnp.int32, (HG_NC, CH, HW), 1)
        db = db + jnp.where(row == CH // 2, dbref, 0.0) + jnp.where(row == CH - 1, dblast, 0.0)
        df = three(_tri_matmul(_chunk_tri(True), flat(db))) / c["f"] - dk
        sig = c["sig"]
        dlb_ref[...] += _colsum(jnp.sum(df * (1.0 - sig), axis=0))
        dp_ref[:, 0:HW] = flat(dq * _dsilu(c["qr"], c["sq"]))
        dp_ref[:, HW:2 * HW] = flat(df * (1.0 - c["lb"]) * sig * (1.0 - sig))
        dp_ref[:, 2 * HW:3 * HW] = flat(dv)
        dp_ref[:, 3 * HW:4 * HW] = d_gated * flat(_unheads(oh * ng)) * _dsilu(og, sog)

    rev = lambda n: nst - 1 - n
    col = lambda j: pl.BlockSpec((tm, HW), lambda n, j=j: (rev(n), j))
    return _pc_behind(
        body, exch, nst, name="hgrn_bwd", grid=(nst,),
        in_specs=[col(0), col(1), col(2), col(3), _whole((2, HW)), _whole((1, HW)),
                  pl.BlockSpec((HG_NC, NH, HD, HD), lambda n: (rev(n), 0, 0, 0)),
                  pl.BlockSpec((tm, HW), lambda n: (rev(n), 0))],
        out_specs=[pl.BlockSpec((tm, 4 * HW), lambda n: (rev(n), 0)), _acc((1, HW)), _acc((1, HD))],
        out_shape=[jax.ShapeDtypeStruct((T, 4 * HW), f32), jax.ShapeDtypeStruct((1, HW), f32),
                   jax.ShapeDtypeStruct((1, HD), f32)],
        scratch_shapes=[pltpu.VMEM((NH, HD, HD), f32)],
        compiler_params=_params(), args=(proj_hg, proj_hg, proj_hg, proj_hg, logits, ng4, sprev, d_out))


S5_TM = 256
S5_SEG = 8
S5_STEPS = S5_TM // S5_SEG
NLT = SL // 128


def _s5_tables(a_ref, pw_ref, pseg_ref, descending):
    re, im = slice(0, SL), slice(SL, 2 * SL)

    def cmul(ar, ai, br, bi):
        return ar * br - ai * bi, ar * bi + ai * br

    pw_ref[0:1, :] = a_ref[...]
    m = 1
    while m < S5_STEPS:
        pr, pi = cmul(pw_ref[0:m, re], pw_ref[0:m, im], pw_ref[m - 1:m, re], pw_ref[m - 1:m, im])
        pw_ref[m:2 * m, re] = pr
        pw_ref[m:2 * m, im] = pi
        m *= 2
    base = S5_STEPS - 1
    if descending:
        pseg_ref[7:8, :] = pw_ref[base:base + 1, :]
        m = 1
        while m < 8:
            pr, pi = cmul(pseg_ref[8 - m:8, re], pseg_ref[8 - m:8, im], pseg_ref[8 - m:9 - m, re], pseg_ref[8 - m:9 - m, im])
            pseg_ref[8 - 2 * m:8 - m, re] = pr
            pseg_ref[8 - 2 * m:8 - m, im] = pi
            m *= 2
    else:
        pseg_ref[0:1, :] = pw_ref[base:base + 1, :]
        m = 1
        while m < 8:
            pr, pi = cmul(pseg_ref[0:m, re], pseg_ref[0:m, im], pseg_ref[m - 1:m, re], pseg_ref[m - 1:m, im])
            pseg_ref[m:2 * m, re] = pr
            pseg_ref[m:2 * m, im] = pi
            m *= 2


def _seg_rows(j):
    return pl.ds(j * S5_SEG, S5_SEG)


def _seg_perm(transpose=False):
    r_i = lax.broadcasted_iota(jnp.int32, (S5_TM, S5_TM), 0)
    c_i = lax.broadcasted_iota(jnp.int32, (S5_TM, S5_TM), 1)
    if transpose:
        r_i, c_i = c_i, r_i
    return c_i == S5_STEPS * (r_i % S5_SEG) + r_i // S5_SEG


def _scan_fwd(x3_ref, pw_ref, pseg_ref, carry_ref):
    row8 = lax.broadcasted_iota(jnp.int32, (S5_SEG, 128), 0)
    for lt in range(NLT):
        kr, ki = lt, NLT + lt
        lr, li = slice(lt * 128, (lt + 1) * 128), slice(SL + lt * 128, SL + (lt + 1) * 128)
        ar, ai = pw_ref[0:1, lr], pw_ref[0:1, li]
        sr = jnp.zeros((S5_SEG, 128), f32)
        si = jnp.zeros((S5_SEG, 128), f32)
        for j in range(S5_STEPS):
            sr, si = ar * sr - ai * si + x3_ref[kr, _seg_rows(j), :], ar * si + ai * sr + x3_ref[ki, _seg_rows(j), :]
            x3_ref[kr, _seg_rows(j), :] = sr
            x3_ref[ki, _seg_rows(j), :] = si
        for d in (1, 2, 4):
            pr, pi = pseg_ref[d - 1:d, lr], pseg_ref[d - 1:d, li]
            tr, ti = pltpu.roll(sr, d, 0), pltpu.roll(si, d, 0)
            m = row8 >= d
            sr, si = sr + jnp.where(m, pr * tr - pi * ti, 0.0), si + jnp.where(m, pr * ti + pi * tr, 0.0)
        c0r, c0i = carry_ref[7:8, lr], carry_ref[7:8, li]
        qr, qi = pseg_ref[:, lr], pseg_ref[:, li]
        sr, si = sr + qr * c0r - qi * c0i, si + qr * c0i + qi * c0r
        carry_ref[:, lr] = sr
        carry_ref[:, li] = si
        cr = jnp.where(row8 == 0, c0r, pltpu.roll(sr, 1, 0))
        ci = jnp.where(row8 == 0, c0i, pltpu.roll(si, 1, 0))
        for j in range(S5_STEPS):
            pr, pi = pw_ref[j:j + 1, lr], pw_ref[j:j + 1, li]
            x3_ref[kr, _seg_rows(j), :] = x3_ref[kr, _seg_rows(j), :] + pr * cr - pi * ci
            x3_ref[ki, _seg_rows(j), :] = x3_ref[ki, _seg_rows(j), :] + pr * ci + pi * cr


def _scan_bwd(g3_ref, x3_ref, xh_ref, first, pw_ref, pseg_ref, carry_ref, dar_ref, dai_ref):
    row8 = lax.broadcasted_iota(jnp.int32, (S5_SEG, 128), 0)
    for lt in range(NLT):
        kr, ki = lt, NLT + lt
        lr, li = slice(lt * 128, (lt + 1) * 128), slice(SL + lt * 128, SL + (lt + 1) * 128)
        ar, ai = pw_ref[0:1, lr], pw_ref[0:1, li]
        sr = jnp.zeros((S5_SEG, 128), f32)
        si = jnp.zeros((S5_SEG, 128), f32)
        for j in reversed(range(S5_STEPS)):
            sr, si = ar * sr + ai * si + g3_ref[kr, _seg_rows(j), :], ar * si - ai * sr + g3_ref[ki, _seg_rows(j), :]
            g3_ref[kr, _seg_rows(j), :] = sr
            g3_ref[ki, _seg_rows(j), :] = si
        for d in (1, 2, 4):
            pr, pi = pseg_ref[8 - d:9 - d, lr], pseg_ref[8 - d:9 - d, li]
            tr, ti = pltpu.roll(sr, 8 - d, 0), pltpu.roll(si, 8 - d, 0)
            m = row8 < 8 - d
            sr, si = sr + jnp.where(m, pr * tr + pi * ti, 0.0), si + jnp.where(m, pr * ti - pi * tr, 0.0)
        c0r, c0i = carry_ref[0:1, lr], carry_ref[0:1, li]
        qr, qi = pseg_ref[:, lr], pseg_ref[:, li]
        sr, si = sr + qr * c0r + qi * c0i, si + qr * c0i - qi * c0r
        carry_ref[:, lr] = sr
        carry_ref[:, li] = si
        cr = jnp.where(row8 == 7, c0r, pltpu.roll(sr, 7, 0))
        ci = jnp.where(row8 == 7, c0i, pltpu.roll(si, 7, 0))
        hr = jnp.where(first, 0.0, xh_ref[kr, 7:8, :])
        hi = jnp.where(first, 0.0, xh_ref[ki, 7:8, :])
        acc_r = jnp.zeros((S5_SEG, 128), f32)
        acc_i = jnp.zeros((S5_SEG, 128), f32)
        for j in range(S5_STEPS):
            pr, pi = pw_ref[S5_STEPS - 1 - j:S5_STEPS - j, lr], pw_ref[S5_STEPS - 1 - j:S5_STEPS - j, li]
            lam_r = g3_ref[kr, _seg_rows(j), :] + pr * cr + pi * ci
            lam_i = g3_ref[ki, _seg_rows(j), :] + pr * ci - pi * cr
            g3_ref[kr, _seg_rows(j), :] = lam_r
            g3_ref[ki, _seg_rows(j), :] = lam_i
            if j == 0:
                xpr = jnp.where(row8 == 0, hr, pltpu.roll(x3_ref[kr, _seg_rows(S5_STEPS - 1), :], 1, 0))
                xpi = jnp.where(row8 == 0, hi, pltpu.roll(x3_ref[ki, _seg_rows(S5_STEPS - 1), :], 1, 0))
            else:
                xpr = x3_ref[kr, _seg_rows(j - 1), :]
                xpi = x3_ref[ki, _seg_rows(j - 1), :]
            acc_r = acc_r + lam_r * xpr + lam_i * xpi
            acc_i = acc_i + lam_i * xpr - lam_r * xpi
        dar_ref[:, lr] += _colsum(acc_r)
        dai_ref[:, lr] += _colsum(acc_i)


def _strip(x3_ref, part, s):
    k0 = part * NLT + s * (STW // 128)
    return jnp.concatenate([x3_ref[k0 + q] for q in range(STW // 128)], axis=1)


def _s5_fwd(u, a_row, bdb, bdc, dskip, glu_w, glu_b, ag_shard):
    T = u.shape[0]
    tm = S5_TM
    nt = T // tm

    def body(u_ref, a_ref, bdb_ref, bdc_ref, ds_ref, gw_ref, gb_ref, ag_ref, x_ref, y_ref, o_ref, ago_ref,
             pw_ref, pseg_ref, carry_ref, send_sems, recv_sems, local_sem):
        i = pl.program_id(0)
        start, forward, finish = _ag_steps(ag_ref, ago_ref, send_sems, recv_sems, local_sem)
        pl.when(i == 0)(start)

        @pl.when(i == 0)
        def _():
            carry_ref[...] = jnp.zeros_like(carry_ref)
            _s5_tables(a_ref, pw_ref, pseg_ref, descending=False)

        uv = u_ref[...]
        ub = jnp.dot(_seg_perm().astype(bf16), uv.astype(bf16), preferred_element_type=f32).astype(bf16)
        for part in range(2):
            for s in range(NST):
                bu = jnp.dot(ub[:, s * 128:(s + 1) * 128], bdb_ref[part * NST + s], preferred_element_type=f32)
                for q in range(STW // 128):
                    x_ref[part * NLT + s * (STW // 128) + q] = bu[:, q * 128:(q + 1) * 128]
        _scan_fwd(x_ref, pw_ref, pseg_ref, carry_ref)
        ys = []
        for s in range(NST):
            acc = None
            for part in range(2):
                t = jnp.dot(_strip(x_ref, part, s).astype(bf16), bdc_ref[part * NST + s], preferred_element_type=f32)
                acc = t if acc is None else acc + t
            ys.append(acc)
        y = _tri_matmul(_seg_perm(transpose=True), jnp.concatenate(ys, axis=1)) + ds_ref[...] * uv
        y_ref[...] = y
        g, _ = _gelu_and_grad(y)
        z = jnp.dot(g.astype(bf16), gw_ref[...], preferred_element_type=f32) + gb_ref[...]
        o_ref[...] = (g * _sig(z)).astype(bf16)
        pl.when(i == (3 * nt) // 4)(forward)
        pl.when(i == nt - 1)(finish)

    row = lambda n: pl.BlockSpec((tm, n), lambda i: (i, 0))
    return _pc(
        body, name="s5_fwd", grid=(nt,),
        in_specs=[row(SW), _whole((1, 2 * SL)), _whole(bdb.shape), _whole(bdc.shape), _whole((1, SW)),
                  _whole((SW, SW)), _whole((1, SW)), ANY],
        out_specs=[pl.BlockSpec((2 * NLT, tm, 128), lambda i: (0, i, 0)), row(SW), row(SW), ANY],
        out_shape=[jax.ShapeDtypeStruct((2 * NLT, T, 128), f32), jax.ShapeDtypeStruct((T, SW), f32),
                   jax.ShapeDtypeStruct((T, SW), bf16),
                   jax.ShapeDtypeStruct((N_DEV, *ag_shard.shape), ag_shard.dtype)],
        scratch_shapes=[pltpu.VMEM((S5_STEPS, 2 * SL), f32), pltpu.VMEM((8, 2 * SL), f32), pltpu.VMEM((8, 2 * SL), f32)]
        + list(AG_SEMS),
        compiler_params=_params(),
    )(u, a_row, bdb, bdc, dskip, glu_w, glu_b, ag_shard)


def _s5_bwd(d_out, y, u, x, a_row, bdb, bdc, dskip, glu_w, glu_b, exch=None):
    T = u.shape[0]
    tm = S5_TM
    nt = T // tm

    def body(do_ref, y_ref, u_ref, x_ref, xh_ref, a_ref, bdb_ref, bdc_ref, ds_ref, gw_ref, gb_ref,
             du_ref, dglu_ref, dar_ref, dai_ref, dd_ref, dgb_ref, dbdb_ref, dbdc_ref, gs_ref, pw_ref, pseg_ref, carry_ref):
        i = pl.program_id(0)

        @pl.when(i == 0)
        def _():
            carry_ref[...] = jnp.zeros_like(carry_ref)
            _s5_tables(a_ref, pw_ref, pseg_ref, descending=True)
            dar_ref[...] = jnp.zeros_like(dar_ref)
            dai_ref[...] = jnp.zeros_like(dai_ref)
            dd_ref[...] = jnp.zeros_like(dd_ref)
            dgb_ref[...] = jnp.zeros_like(dgb_ref)
            dglu_ref[...] = jnp.zeros_like(dglu_ref)
            dbdb_ref[...] = jnp.zeros_like(dbdb_ref)
            dbdc_ref[...] = jnp.zeros_like(dbdc_ref)

        yv = y_ref[...]
        uv = u_ref[...]
        g, gp = _gelu_and_grad(yv)
        z = jnp.dot(g.astype(bf16), gw_ref[...], preferred_element_type=f32) + gb_ref[...]
        sg = _sig(z)
        do = do_ref[...].astype(f32)
        dz = do * g * sg * (1.0 - sg)
        dglu_ref[...] += _dot_tn(g, dz)
        dgb_ref[...] += _colsum(dz)
        dy = (do * sg + _dot_nt(dz, gw_ref[...])) * gp
        perm = _seg_perm().astype(bf16)
        dyb = jnp.dot(perm, dy.astype(bf16), preferred_element_type=f32).astype(bf16)
        dd_ref[...] += _colsum(dy * uv)
        for part in range(2):
            for s in range(NST):
                gx = lax.dot_general(dyb[:, s * 128:(s + 1) * 128], bdc_ref[part * NST + s], (((1,), (1,)), ((), ())),
                                     preferred_element_type=f32)
                for q in range(STW // 128):
                    gs_ref[part * NLT + s * (STW // 128) + q] = gx[:, q * 128:(q + 1) * 128]
        _scan_bwd(gs_ref, x_ref, xh_ref, i == nt - 1, pw_ref, pseg_ref, carry_ref, dar_ref, dai_ref)
        ub = jnp.dot(perm, uv.astype(bf16), preferred_element_type=f32).astype(bf16)
        dus = []
        for s in range(NST):
            acc = None
            for part in range(2):
                lv = _strip(gs_ref, part, s).astype(bf16)
                t = lax.dot_general(lv, bdb_ref[part * NST + s], (((1,), (1,)), ((), ())), preferred_element_type=f32)
                acc = t if acc is None else acc + t
                dbdb_ref[part * NST + s] += _dot_tn(ub[:, s * 128:(s + 1) * 128], lv)
                dbdc_ref[part * NST + s] += _dot_tn(_strip(x_ref, part, s), dyb[:, s * 128:(s + 1) * 128])
            dus.append(acc)
        du_ref[...] = _tri_matmul(_seg_perm(transpose=True), jnp.concatenate(dus, axis=1)) + dy * ds_ref[...]

    rev = lambda i: nt - 1 - i
    row = lambda n: pl.BlockSpec((tm, n), lambda i: (rev(i), 0))
    xblk = pl.BlockSpec((2 * NLT, tm, 128), lambda i: (0, rev(i), 0))
    halo = pl.BlockSpec((2 * NLT, 8, 128), lambda i: (0, jnp.maximum(rev(i) * (tm // 8) - 1, 0), 0))
    return _pc_behind(
        body, exch, nt, name="s5_bwd", grid=(nt,),
        in_specs=[row(SW), row(SW), row(SW), xblk, halo, _whole((1, 2 * SL)), _whole(bdb.shape), _whole(bdc.shape),
                  _whole((1, SW)), _whole((SW, SW)), _whole((1, SW))],
        out_specs=[row(SW), _acc((SW, SW)), _acc((1, SL)), _acc((1, SL)), _acc((1, SW)), _acc((1, SW)),
                   _acc(bdb.shape), _acc(bdc.shape)],
        out_shape=[jax.ShapeDtypeStruct((T, SW), f32), jax.ShapeDtypeStruct((SW, SW), f32),
                   jax.ShapeDtypeStruct((1, SL), f32), jax.ShapeDtypeStruct((1, SL), f32),
                   jax.ShapeDtypeStruct((1, SW), f32), jax.ShapeDtypeStruct((1, SW), f32),
                   jax.ShapeDtypeStruct(bdb.shape, f32), jax.ShapeDtypeStruct(bdc.shape, f32)],
        scratch_shapes=[pltpu.VMEM((2 * NLT, tm, 128), f32), pltpu.VMEM((S5_STEPS, 2 * SL), f32),
                        pltpu.VMEM((8, 2 * SL), f32), pltpu.VMEM((8, 2 * SL), f32)],
        compiler_params=_params(), args=(d_out, y, u, x, x, a_row, bdb, bdc, dskip, glu_w, glu_b))


def _mix_up(x, hg_o, s5_o, gates, w_bhg, w_bs5, w_out, g_ffn, w_up):
    T = x.shape[0]
    tm = 256

    def body(x_ref, hg_ref, s5_ref, gt_ref, wh_ref, ws_ref, wo_ref, g_ref, wu_ref, x1_ref, mg_ref, h2_ref, a_ref):
        yh = jnp.dot(hg_ref[...], wh_ref[...], preferred_element_type=f32)
        ys = jnp.dot(s5_ref[...], ws_ref[...], preferred_element_type=f32)
        merged = (_sig(gt_ref[:, 0:D]) * yh + _sig(gt_ref[:, D:2 * D]) * ys).astype(bf16)
        mg_ref[...] = merged
        x1 = x_ref[...] + jnp.dot(merged, wo_ref[...], preferred_element_type=f32)
        x1_ref[...] = x1
        xh, _ = _rms(x1)
        h2 = (xh * g_ref[...]).astype(bf16)
        h2_ref[...] = h2
        a_ref[...] = jnp.dot(h2, wu_ref[...], preferred_element_type=f32)

    row = lambda n: pl.BlockSpec((tm, n), lambda i: (i, 0))
    return _pc(
        body, name="mix_up", grid=(T // tm,),
        in_specs=[row(D), row(HW), row(SW), row(2 * D), _whole(w_bhg.shape), _whole(w_bs5.shape), _whole(w_out.shape),
                  _whole((1, D)), _whole(w_up.shape)],
        out_specs=[row(D), row(D), row(D), row(2 * DFF)],
        out_shape=[jax.ShapeDtypeStruct((T, D), f32), jax.ShapeDtypeStruct((T, D), bf16),
                   jax.ShapeDtypeStruct((T, D), bf16), jax.ShapeDtypeStruct((T, 2 * DFF), f32)],
        compiler_params=_params(),
    )(x, hg_o, s5_o, gates, w_bhg, w_bs5, w_out, g_ffn, w_up)


FFN_TM = 128
FFN_FS = 256


def _conv_gelu_blocks(a_ref, ah_ref, first, cw_ref, cb_ref, ga_ref, gb_ref, gated_ref):
    row8 = lax.broadcasted_iota(jnp.int32, (8, FFN_FS), 0)
    for s in range(DFF // FFN_FS):
        halves = (slice(s * FFN_FS, (s + 1) * FFN_FS), slice(DFF + s * FFN_FS, DFF + (s + 1) * FFN_FS))
        w = [[cw_ref[k:k + 1, ln] for k in range(3)] for ln in halves]
        bias = [cb_ref[:, ln] for ln in halves]
        prev = [jnp.where(first, 0.0, ah_ref[:, ln]) for ln in halves]
        p1 = [pltpu.roll(p, 1, 0) for p in prev]
        p2 = [pltpu.roll(p, 2, 0) for p in prev]
        for j in range(FFN_TM // 8):
            rows = slice(8 * j, 8 * j + 8)
            c = []
            for hf, ln in enumerate(halves):
                av = a_ref[rows, ln]
                r1, r2 = pltpu.roll(av, 1, 0), pltpu.roll(av, 2, 0)
                a1 = jnp.where(row8 >= 1, r1, p1[hf])
                a2 = jnp.where(row8 >= 2, r2, p2[hf])
                c.append(bias[hf] + w[hf][0] * a2 + w[hf][1] * a1 + w[hf][2] * av)
                p1[hf], p2[hf] = r1, r2
            gl, gp = _gelu_and_grad(c[0])
            ga_ref[rows, halves[0]] = c[1] * gp
            gb_ref[rows, halves[0]] = gl
            gated_ref[rows, halves[0]] = gl * c[1]


def _ffn_tail(a, conv_w, conv_b, w_down, x1, p, g_ple, w_pg, w_pp, g_fin, tgt):
    T = a.shape[0]
    tm = FFN_TM

    def body(a_ref, ah_ref, cw_ref, cb_ref, wd_ref, x1_ref, p_ref, gp_ref, wpg_ref, wpp_ref, gf_ref, t_ref,
             dx2_ref, gd_ref, ga_ref, gb_ref, dwpg_ref, dwpp_ref, loss_ref, dgf_ref, dgp_ref, gsc_ref):
        i = pl.program_id(0)

        @pl.when(i == 0)
        def _():
            loss_ref[...] = jnp.zeros_like(loss_ref)
            dgf_ref[...] = jnp.zeros_like(dgf_ref)
            dgp_ref[...] = jnp.zeros_like(dgp_ref)
            dwpg_ref[...] = jnp.zeros_like(dwpg_ref)
            dwpp_ref[...] = jnp.zeros_like(dwpp_ref)

        _conv_gelu_blocks(a_ref, ah_ref, i == 0, cw_ref, cb_ref, ga_ref, gb_ref, gsc_ref)
        gated = gsc_ref[...].astype(bf16)
        gd_ref[...] = gated
        x2 = x1_ref[...] + jnp.dot(gated, wd_ref[...], preferred_element_type=f32)
        xh2, r2 = _rms(x2)
        h3 = (xh2 * gp_ref[...]).astype(bf16)
        pg = _sig(jnp.dot(h3, wpg_ref[...], preferred_element_type=f32))
        pp = _dot(p_ref[...], wpp_ref[...])
        x3 = x2 + pg * pp
        xh3, r3 = _rms(x3)
        diff = xh3 * gf_ref[...] - t_ref[...]
        loss_ref[...] += 0.5 * jnp.sum(jnp.mean(diff * diff, axis=-1, keepdims=True), axis=0, keepdims=True)
        dy = diff * (1.0 / D)
        dx3, dgf_rows = _rms_bwd(dy, xh3, r3, gf_ref[...])
        dgf_ref[...] += _colsum(dgf_rows)
        dwpp_ref[...] += _dot_tn(p_ref[...], dx3 * pg)
        dz = (dx3 * pp * pg * (1.0 - pg)).astype(bf16)
        dwpg_ref[...] += _dot_tn(h3, dz)
        dh3 = _dot_nt(dz, wpg_ref[...])
        dx2n, dgp_rows = _rms_bwd(dh3, xh2, r2, gp_ref[...])
        dgp_ref[...] += _colsum(dgp_rows)
        dx2_ref[...] = dx3 + dx2n

    row = lambda n: pl.BlockSpec((tm, n), lambda i: (i, 0))
    halo = pl.BlockSpec((8, 2 * DFF), lambda i: (jnp.maximum(i * (tm // 8) - 1, 0), 0))
    return _pc(
        body, name="ffn_tail", grid=(T // tm,),
        in_specs=[row(2 * DFF), halo, _whole((3, 2 * DFF)), _whole((1, 2 * DFF)), _whole(w_down.shape), row(D), row(PLE),
                  _whole((1, D)), _whole(w_pg.shape), _whole(w_pp.shape), _whole((1, D)), row(D)],
        out_specs=[row(D), row(DFF), row(DFF), row(DFF), _acc(w_pg.shape), _acc(w_pp.shape),
                   _acc((1, 128)), _acc((1, D)), _acc((1, D))],
        out_shape=[jax.ShapeDtypeStruct((T, D), f32), jax.ShapeDtypeStruct((T, DFF), bf16),
                   jax.ShapeDtypeStruct((T, DFF), f32), jax.ShapeDtypeStruct((T, DFF), f32),
                   jax.ShapeDtypeStruct(w_pg.shape, f32), jax.ShapeDtypeStruct(w_pp.shape, f32),
                   jax.ShapeDtypeStruct((1, 128), f32), jax.ShapeDtypeStruct((1, D), f32), jax.ShapeDtypeStruct((1, D), f32)],
        scratch_shapes=[pltpu.VMEM((tm, DFF), f32)],
        compiler_params=_params(),
    )(a, a, conv_w, conv_b, w_down, x1, p, g_ple, w_pg, w_pp, g_fin, tgt)


def _ffn_bwd(dx2, a, g_a, g_b, conv_w, w_down, w_up, x1, g_ffn, exch=None):
    T = a.shape[0]
    tm = FFN_TM
    nt = T // tm

    def body(dx2_ref, a_ref, ga_ref, gb_ref, cw_ref, wd_ref, wu_ref, x1_ref, g_ref,
             da_ref, dx1_ref, dcw_ref, dcb_ref, dg_ref, carry_ref, dgd_ref, dasc_ref):
        i = pl.program_id(0)

        @pl.when(i == 0)
        def _():
            carry_ref[...] = jnp.zeros_like(carry_ref)
            dcw_ref[...] = jnp.zeros_like(dcw_ref)
            dcb_ref[...] = jnp.zeros_like(dcb_ref)
            dg_ref[...] = jnp.zeros_like(dg_ref)

        dx2 = dx2_ref[...]
        dgd_ref[...] = _dot_nt(dx2, wd_ref[...])
        row8 = lax.broadcasted_iota(jnp.int32, (8, FFN_FS), 0)
        for s in range(DFF // FFN_FS):
            src = slice(s * FFN_FS, (s + 1) * FFN_FS)
            halves = (src, slice(DFF + s * FFN_FS, DFF + (s + 1) * FFN_FS))
            w = [[cw_ref[k:k + 1, ln] for k in range(3)] for ln in halves]
            nxt = [carry_ref[:, ln] for ln in halves]
            n7 = [pltpu.roll(v, 7, 0) for v in nxt]
            n6 = [pltpu.roll(v, 6, 0) for v in nxt]
            acc = [[jnp.zeros((8, FFN_FS), f32) for _ in range(4)] for _ in halves]
            for j in reversed(range(tm // 8)):
                rows = slice(8 * j, 8 * j + 8)
                dg = dgd_ref[rows, src]
                for hf, (ln, saved) in enumerate(zip(halves, (ga_ref, gb_ref))):
                    dc = dg * saved[rows, src]
                    r7, r6 = pltpu.roll(dc, 7, 0), pltpu.roll(dc, 6, 0)
                    up1 = jnp.where(row8 < 7, r7, n7[hf])
                    up2 = jnp.where(row8 < 6, r6, n6[hf])
                    av = a_ref[rows, ln]
                    acc[hf][0] = acc[hf][0] + up2 * av
                    acc[hf][1] = acc[hf][1] + up1 * av
                    acc[hf][2] = acc[hf][2] + dc * av
                    acc[hf][3] = acc[hf][3] + dc
                    dasc_ref[rows, ln] = w[hf][2] * dc + w[hf][1] * up1 + w[hf][0] * up2
                    n7[hf], n6[hf] = r7, r6
                    if j == 0:
                        carry_ref[:, ln] = dc
            for hf, ln in enumerate(halves):
                for k in range(3):
                    dcw_ref[k:k + 1, ln] += _colsum(acc[hf][k])
                dcb_ref[:, ln] += _colsum(acc[hf][3])
        da = dasc_ref[...].astype(bf16)
        da_ref[...] = da
        dh2 = lax.dot_general(da, wu_ref[...], (((1,), (1,)), ((), ())), preferred_element_type=f32)
        xh, r = _rms(x1_ref[...])
        dx1n, dg_rows = _rms_bwd(dh2, xh, r, g_ref[...])
        dg_ref[...] += _colsum(dg_rows)
        dx1_ref[...] = dx2 + dx1n

    rev = lambda i: nt - 1 - i
    row = lambda n: pl.BlockSpec((tm, n), lambda i: (rev(i), 0))
    return _pc_behind(
        body, exch, nt, name="ffn_bwd", grid=(nt,),
        in_specs=[row(D), row(2 * DFF), row(DFF), row(DFF), _whole((3, 2 * DFF)), _whole(w_down.shape),
                  _whole(w_up.shape), row(D), _whole((1, D))],
        out_specs=[row(2 * DFF), row(D), _acc((3, 2 * DFF)), _acc((1, 2 * DFF)), _acc((1, D))],
        out_shape=[jax.ShapeDtypeStruct((T, 2 * DFF), bf16), jax.ShapeDtypeStruct((T, D), f32),
                   jax.ShapeDtypeStruct((3, 2 * DFF), f32), jax.ShapeDtypeStruct((1, 2 * DFF), f32),
                   jax.ShapeDtypeStruct((1, D), f32)],
        scratch_shapes=[pltpu.VMEM((8, 2 * DFF), f32), pltpu.VMEM((tm, DFF), f32), pltpu.VMEM((tm, 2 * DFF), f32)],
        compiler_params=_params(), args=(dx2, a, g_a, g_b, conv_w, w_down, w_up, x1, g_ffn))


def _mix_bwd(dx1, hg_o, s5_o, merged, gates, w_bhg, w_bs5, w_out, exch=None):
    T = dx1.shape[0]
    tm = 256

    def body(dx1_ref, hg_ref, s5_ref, mg_ref, gt_ref, wh_ref, ws_ref, wo_ref, dgt_ref, dhg_ref, ds5_ref, dwh_ref, dws_ref,
             dwo_ref):
        @pl.when(pl.program_id(0) == 0)
        def _():
            dwh_ref[...] = jnp.zeros_like(dwh_ref)
            dws_ref[...] = jnp.zeros_like(dws_ref)
            dwo_ref[...] = jnp.zeros_like(dwo_ref)

        dwo_ref[...] += _dot_tn(mg_ref[...], dx1_ref[...])
        dm = _dot_nt(dx1_ref[...], wo_ref[...])
        yh = jnp.dot(hg_ref[...], wh_ref[...], preferred_element_type=f32)
        ys = jnp.dot(s5_ref[...], ws_ref[...], preferred_element_type=f32)
        sh = _sig(gt_ref[:, 0:D])
        ss = _sig(gt_ref[:, D:2 * D])
        dgt_ref[:, 0:D] = dm * yh * sh * (1.0 - sh)
        dgt_ref[:, D:2 * D] = dm * ys * ss * (1.0 - ss)
        dyh = (dm * sh).astype(bf16)
        dys = (dm * ss).astype(bf16)
        dwh_ref[...] += _dot_tn(hg_ref[...], dyh)
        dws_ref[...] += _dot_tn(s5_ref[...], dys)
        dhg_ref[...] = lax.dot_general(dyh, wh_ref[...], (((1,), (1,)), ((), ())), preferred_element_type=f32)
        ds5_ref[...] = lax.dot_general(dys, ws_ref[...], (((1,), (1,)), ((), ())), preferred_element_type=f32)

    row = lambda n: pl.BlockSpec((tm, n), lambda i: (i, 0))
    return _pc_behind(
        body, exch, T // tm, name="mix_bwd", grid=(T // tm,),
        in_specs=[row(D), row(HW), row(SW), row(D), row(2 * D), _whole(w_bhg.shape), _whole(w_bs5.shape),
                  _whole(w_out.shape)],
        out_specs=[row(2 * D), row(HW), row(SW), _acc(w_bhg.shape), _acc(w_bs5.shape), _acc(w_out.shape)],
        out_shape=[jax.ShapeDtypeStruct((T, 2 * D), f32), jax.ShapeDtypeStruct((T, HW), f32), jax.ShapeDtypeStruct((T, SW), f32),
                   jax.ShapeDtypeStruct(w_bhg.shape, f32), jax.ShapeDtypeStruct(w_bs5.shape, f32),
                   jax.ShapeDtypeStruct(w_out.shape, f32)],
        compiler_params=_params(), args=(dx1, hg_o, s5_o, merged, gates, w_bhg, w_bs5, w_out))


def _in_bwd(d_hg, d_u, d_gt, x, dx1, w, g):
    T = x.shape[0]
    tm = 256

    def body(dhg_ref, du_ref, dgt_ref, x_ref, dx1_ref, w_ref, g_ref, dx_ref, dg_ref):
        @pl.when(pl.program_id(0) == 0)
        def _():
            dg_ref[...] = jnp.zeros_like(dg_ref)

        dh = (_dot_nt(dhg_ref[...], w_ref[:, 0:4 * HW]) + _dot_nt(du_ref[...], w_ref[:, 4 * HW:4 * HW + SW])
              + _dot_nt(dgt_ref[...], w_ref[:, 4 * HW + SW:]))
        xh, r = _rms(x_ref[...])
        dxn, dg_rows = _rms_bwd(dh, xh, r, g_ref[...])
        dg_ref[...] += _colsum(dg_rows)
        dx_ref[...] = dx1_ref[...] + dxn

    row = lambda n: pl.BlockSpec((tm, n), lambda i: (i, 0))
    return _pc(
        body, name="in_bwd", grid=(T // tm,),
        in_specs=[row(4 * HW), row(SW), row(2 * D), row(D), row(D), _whole(w.shape), _whole((1, D))],
        out_specs=[row(D), _acc((1, D))],
        out_shape=[jax.ShapeDtypeStruct((T, D), f32), jax.ShapeDtypeStruct((1, D), f32)],
        compiler_params=_params(),
    )(d_hg, d_u, d_gt, x, dx1, w, g)


def _wgrad(name, a, b, nj=None, a_blk=None, a_idx=None, b_blk=None, b_idx=None):
    T = a.shape[0]
    tm = 512
    dense = nj is None
    if dense:
        K, N = a.shape[1], b.shape[1]
        a_blk, a_idx = K, (lambda j: 0)
        b_blk = N
        while K * b_blk * 4 > 6 * 1024 * 1024 and b_blk % 256 == 0:
            b_blk //= 2
        nj, b_idx = N // b_blk, (lambda j: j)

    def body(a_ref, b_ref, o_ref):
        @pl.when(pl.program_id(1) == 0)
        def _():
            o_ref[...] = jnp.zeros_like(o_ref)

        o_ref[0] += _dot_tn(a_ref[...], b_ref[...])

    out = _pc(
        body, name=name, grid=(nj, T // tm),
        in_specs=[pl.BlockSpec((tm, a_blk), lambda j, i: (i, a_idx(j))), pl.BlockSpec((tm, b_blk), lambda j, i: (i, b_idx(j)))],
        out_specs=pl.BlockSpec((1, a_blk, b_blk), lambda j, i: (j, 0, 0)),
        out_shape=jax.ShapeDtypeStruct((nj, a_blk, b_blk), f32),
        compiler_params=_params(2),
    )(a, b)
    if dense:
        return out[0] if nj == 1 else jnp.transpose(out, (1, 0, 2)).reshape(a.shape[1], b.shape[1])
    return out


ANY = pl.BlockSpec(memory_space=pl.ANY)


AG_SEMS = [pltpu.SemaphoreType.DMA((7,)), pltpu.SemaphoreType.DMA((7,)), pltpu.SemaphoreType.DMA]


def _ag_steps(x_ref, out_ref, send_sems, recv_sems, local_sem):
    x, y, c = lax.axis_index("x"), lax.axis_index("y"), lax.axis_index("c")
    me, sibling = (x, y, c), (x, y, 1 - c)
    chips = [(1 - x, y), (x, 1 - y), (1 - x, 1 - y)]

    def slot(px, py, pc):
        return out_ref.at[4 * px + 2 * py + pc]

    def copy(k, block, to, src=None):
        return pltpu.make_async_remote_copy(
            src_ref=slot(*block) if src is None else src, dst_ref=slot(*block),
            send_sem=send_sems.at[k], recv_sem=recv_sems.at[k], device_id=to, device_id_type=MESH)

    def mine():
        return pltpu.make_async_copy(x_ref, slot(*me), local_sem)

    def first():
        return [copy(0, me, sibling, src=x_ref)] + [copy(1 + j, me, (*chip, c), src=x_ref) for j, chip in enumerate(chips)]

    def passed():
        return [copy(4 + j, (*chip, c), sibling) for j, chip in enumerate(chips)]

    def start():
        mine().start()
        for cp in first():
            cp.start()

    def forward():
        for j, (chip, cp) in enumerate(zip(chips, passed())):
            copy(1 + j, (*chip, c), me).wait_recv()
            cp.start()

    def finish():
        copy(0, sibling, me).wait_recv()
        for j, chip in enumerate(chips):
            copy(4 + j, (*chip, 1 - c), me).wait_recv()
        for cp in first() + passed():
            cp.wait_send()
        mine().wait()

    return start, forward, finish


def _all_gather(name, shard):
    R, C = shard.shape

    def body(x_ref, out_ref, send_sems, recv_sems, local_sem):
        for phase in _ag_steps(x_ref, out_ref, send_sems, recv_sems, local_sem):
            phase()

    return _pc(
        body, name=name, in_specs=[ANY], out_specs=ANY,
        out_shape=jax.ShapeDtypeStruct((N_DEV, R, C), shard.dtype), scratch_shapes=list(AG_SEMS),
    )(shard)


class _Exchange:
    def __init__(self, kind, arrays):
        self.kind, self.arrays, self.n = kind, list(arrays), len(arrays)
        self.per = 4 if kind == "sibling" else 3
        tail = (lambda a: a.shape[2:]) if kind == "sibling" else (lambda a: a.shape[1:])
        self.out_shape = [jax.ShapeDtypeStruct((self.per, *tail(a)), a.dtype) for a in self.arrays]
        self.scratch = [pltpu.SemaphoreType.DMA((self.per * self.n,)), pltpu.SemaphoreType.DMA((self.per * self.n,))]

    def steps(self, in_refs, out_refs, send_sems, recv_sems):
        x, y, c = lax.axis_index("x"), lax.axis_index("y"), lax.axis_index("c")
        chips = [(1 - x, y), (x, 1 - y), (1 - x, 1 - y)]

        def copies():
            cps = []
            for i, (src, dst) in enumerate(zip(in_refs, out_refs)):
                for k in range(self.per):
                    if self.kind == "sibling":
                        s, to = src.at[k, 1 - c], (x, y, 1 - c)
                    else:
                        s, to = src.at[2 * chips[k][0] + chips[k][1]], (*chips[k], c)
                    cps.append(pltpu.make_async_remote_copy(
                        src_ref=s, dst_ref=dst.at[k], send_sem=send_sems.at[self.per * i + k],
                        recv_sem=recv_sems.at[self.per * i + k], device_id=to, device_id_type=MESH))
            return cps

        def start():
            for cp in copies():
                cp.start()

        def finish():
            for cp in copies():
                cp.wait()

        return start, finish


def _exchange_call(name, exch):
    n = exch.n

    def body(*refs):
        start, finish = exch.steps(refs[:n], refs[n:2 * n], *refs[2 * n:])
        start()
        finish()

    return _pc(body, name=name, in_specs=[ANY] * n, out_specs=[ANY] * n, out_shape=exch.out_shape,
               scratch_shapes=exch.scratch)(*exch.arrays)


def _pc_behind(body, exch, nsteps, *, in_specs, out_specs, out_shape, args, scratch_shapes=(), **kw):
    if exch is None:
        return _pc(body, in_specs=in_specs, out_specs=out_specs, out_shape=out_shape, scratch_shapes=list(scratch_shapes),
                   **kw)(*args), None
    n_in, n_out, n_scr, ne = len(in_specs), len(out_specs), len(scratch_shapes), exch.n

    def wrapped(*refs):
        ins, e_in = refs[:n_in], refs[n_in:n_in + ne]
        o0 = n_in + ne
        outs, e_out = refs[o0:o0 + n_out], refs[o0 + n_out:o0 + n_out + ne]
        s0 = o0 + n_out + ne
        scr, sems = refs[s0:s0 + n_scr], refs[s0 + n_scr:]
        start, finish = exch.steps(e_in, e_out, *sems)
        i = pl.program_id(0)
        pl.when(i == 0)(start)
        body(*ins, *outs, *scr)
        pl.when(i == nsteps - 1)(finish)

    res = _pc(wrapped, in_specs=list(in_specs) + [ANY] * ne, out_specs=list(out_specs) + [ANY] * ne,
              out_shape=list(out_shape) + exch.out_shape, scratch_shapes=list(scratch_shapes) + exch.scratch,
              **kw)(*args, *exch.arrays)
    return res[:n_out], res[n_out:]


def _add_halves(name, g4, got, ids):
    _, _, K, c = g4.shape

    def body(ids_ref, a_ref, b_ref, p16_ref, own_ref):
        s = a_ref[0, 0] + b_ref[0]
        p16_ref[0] = s.astype(bf16)

        @pl.when(pl.program_id(0) == ids_ref[1])
        def _():
            own_ref[...] = s

    return _pc(
        body, name=name,
        grid_spec=pltpu.PrefetchScalarGridSpec(
            num_scalar_prefetch=1, grid=(4,),
            in_specs=[pl.BlockSpec((1, 1, K, c), lambda k, ids: (k, ids[0], 0, 0)),
                      pl.BlockSpec((1, K, c), lambda k, ids: (k, 0, 0))],
            out_specs=[pl.BlockSpec((1, K, c), lambda k, ids: (k, 0, 0)), pl.BlockSpec((K, c), lambda k, ids: (0, 0))]),
        out_shape=[jax.ShapeDtypeStruct((4, K, c), bf16), jax.ShapeDtypeStruct((K, c), f32)],
        compiler_params=_params(),
    )(ids, g4, got)


def _row_tile(K):
    for cand in (256, 176, 128, 64):
        if K % cand == 0:
            return cand
    return K


def _adam_shard(name, own, got3, w, m, v):
    K, c = own.shape
    tr = _row_tile(K)

    def body(own_ref, got_ref, w_ref, m_ref, v_ref, g_ref, d_ref, m2_ref, v2_ref):
        g = own_ref[...] + got_ref[0].astype(f32) + got_ref[1].astype(f32) + got_ref[2].astype(f32)
        g_ref[0] = g
        delta, m2, v2 = _adam_math(g, w_ref[0], m_ref[0], v_ref[0])
        d_ref[0] = delta
        m2_ref[0] = m2
        v2_ref[0] = v2

    blk = pl.BlockSpec((1, tr, c), lambda i: (0, i, 0))
    out = jax.ShapeDtypeStruct((1, K, c), f32)
    return _pc(
        body, name=name, grid=(K // tr,),
        in_specs=[pl.BlockSpec((tr, c), lambda i: (i, 0)), pl.BlockSpec((3, tr, c), lambda i: (0, i, 0)), blk, blk, blk],
        out_specs=[blk, blk, blk, blk], out_shape=[out, out, out, out], compiler_params=_params(),
    )(own, got3, w, m, v)


def _allreduce_small(grads):
    n = len(grads)
    shapes = [g.shape for g in grads]

    def body(*refs):
        g_refs, outs, recv = refs[0:n], refs[n:2 * n], refs[2 * n:5 * n]
        send_sems, recv_sems = refs[5 * n:]
        x, y, c = lax.axis_index("x"), lax.axis_index("y"), lax.axis_index("c")
        peers = [(x, y, 1 - c), (1 - x, y, c), (x, 1 - y, c)]
        for i in range(n):
            outs[i][...] = g_refs[i][...]
        for s, peer in enumerate(peers):
            cps = [pltpu.make_async_remote_copy(src_ref=outs[i], dst_ref=recv[s * n + i], send_sem=send_sems.at[s * n + i],
                                                recv_sem=recv_sems.at[s * n + i], device_id=peer, device_id_type=MESH)
                   for i in range(n)]
            for cp in cps:
                cp.start()
            for cp in cps:
                cp.wait()
            for i in range(n):
                outs[i][...] = outs[i][...] + recv[s * n + i][...]

    return _pc(
        body, name="allreduce_small", grid=(1,), in_specs=[_whole(s) for s in shapes], out_specs=[_acc(s) for s in shapes],
        out_shape=[jax.ShapeDtypeStruct(s, f32) for s in shapes],
        scratch_shapes=[pltpu.VMEM(s, f32) for s in shapes] * 3
        + [pltpu.SemaphoreType.DMA((3 * n,)), pltpu.SemaphoreType.DMA((3 * n,))],
        compiler_params=_params(),
    )(*grads)


def _adam_small(grads, ws, ms, vs):
    n = len(grads)
    shapes = [g.shape for g in grads]

    def body(*refs):
        g_refs, w_refs, m_refs, v_refs = refs[0:n], refs[n:2 * n], refs[2 * n:3 * n], refs[3 * n:4 * n]
        outs = refs[4 * n:8 * n]
        for i in range(n):
            g = g_refs[i][...]
            delta, m2, v2 = _adam_math(g, w_refs[i][...], m_refs[i][...], v_refs[i][...])
            outs[i][...] = g
            outs[n + i][...] = delta
            outs[2 * n + i][...] = m2
            outs[3 * n + i][...] = v2

    return _pc(
        body, name="adam_small", grid=(1,), in_specs=[_whole(s) for s in shapes] * 4, out_specs=[_acc(s) for s in shapes] * 4,
        out_shape=[jax.ShapeDtypeStruct(s, f32) for s in shapes] * 4, compiler_params=_params(),
    )(*grads, *ws, *ms, *vs)


def _adam_math(g, w, m, v):
    m2 = ADAM_B1 * m + (1.0 - ADAM_B1) * g
    v2 = ADAM_B2 * v + (1.0 - ADAM_B2) * (g * g)
    m_hat = m2 / (1.0 - ADAM_B1 ** ADAM_STEP)
    v_hat = v2 / (1.0 - ADAM_B2 ** ADAM_STEP)
    delta = -ADAM_LR * (m_hat / (jnp.sqrt(v_hat) + ADAM_EPS) + ADAM_WD * w)
    return delta, m2, v2


def _pack(arrs, dtype, row_mult):
    rows = []
    for a in arrs:
        flat = a.reshape(-1).astype(dtype)
        pad = (-flat.shape[0]) % LANES
        if pad:
            flat = jnp.concatenate([flat, jnp.zeros((pad,), dtype)])
        rows.append(flat.reshape(-1, LANES))
    out = jnp.concatenate(rows, axis=0)
    pad = (-out.shape[0]) % row_mult
    if pad:
        out = jnp.concatenate([out, jnp.zeros((pad, LANES), dtype)], axis=0)
    return out


def _unpack(buf, shapes):
    lead = buf.shape[:-2]
    outs, r = [], 0
    for shp in shapes:
        n = math.prod(shp)
        nr = -(-n // LANES)
        piece = buf[..., r:r + nr, :].reshape(*lead, nr * LANES)[..., :n]
        outs.append(piece.reshape(*lead, *shp))
        r += nr
    return outs


def _to_slabs(full, axis):
    shp = full.shape
    n = shp[axis] // N_DEV
    return jnp.moveaxis(full.reshape(*shp[:axis], N_DEV, n, *shp[axis + 1:]), axis, 0)


def _from_slabs(slabs, axis):
    t = jnp.moveaxis(slabs, 0, axis)
    shp = t.shape
    return t.reshape(*shp[:axis], shp[axis] * shp[axis + 1], *shp[axis + 2:])


def _s5_discretise(lam_re, lam_im, log_dt, b_re, b_im):
    dt = jnp.exp(log_dt)[:, None]
    mag = jnp.exp(lam_re * dt)
    a_re = mag * jnp.cos(lam_im * dt)
    a_im = mag * jnp.sin(lam_im * dt)
    den = lam_re * lam_re + lam_im * lam_im
    coef_re = ((a_re - 1.0) * lam_re + a_im * lam_im) / den
    coef_im = (a_im * lam_re - (a_re - 1.0) * lam_im) / den
    bbar_re = coef_re[..., None] * b_re - coef_im[..., None] * b_im
    bbar_im = coef_re[..., None] * b_im + coef_im[..., None] * b_re
    return a_re, a_im, bbar_re, bbar_im


def _s5_operands(bbar_re, bbar_im, c_re, c_im):
    eye = jnp.eye(SG // NST, dtype=f32)

    def b_op(bb):
        return jnp.einsum("sgnq,gh->sgqhn", bb.reshape(NST, SG // NST, SN, SP), eye).reshape(NST, 128, STW)

    def c_op(cc):
        return jnp.einsum("sgpn,gh->shngp", cc.reshape(NST, SG // NST, SP, SN), eye).reshape(NST, STW, 128)

    bdb = jnp.concatenate([b_op(bbar_re), b_op(bbar_im)], axis=0)
    bdc = jnp.concatenate([c_op(c_re), c_op(-c_im)], axis=0)
    return bdb, bdc


_BIG = ["w_in", "s5_glu_w", "w_branch_hg", "w_branch_s5", "w_out", "w_up", "w_down", "w_ple_gate", "w_ple_proj", "conv_w"]
_BIG_AXIS = {"w_in": 1, "s5_glu_w": 0, "w_branch_hg": 1, "w_branch_s5": 1, "w_out": 0, "w_up": 1, "w_down": 0,
             "w_ple_gate": 0, "w_ple_proj": 1, "conv_w": 1}
_SMALL = ["norm_mix_g", "hg_lb_logits", "hg_norm_g", "s5_lambda_re", "s5_lambda_im", "s5_log_dt", "s5_b_re", "s5_b_im",
          "s5_c_re", "s5_c_im", "s5_d", "s5_glu_b", "norm_ffn_g", "conv_b", "norm_ple_g", "norm_final_g"]
_ORDER = ["norm_mix_g", "w_in", "hg_lb_logits", "hg_norm_g", "s5_lambda_re", "s5_lambda_im", "s5_log_dt", "s5_b_re",
          "s5_b_im", "s5_c_re", "s5_c_im", "s5_d", "s5_glu_w", "s5_glu_b", "w_branch_hg", "w_branch_s5", "w_out",
          "norm_ffn_g", "w_up", "conv_w", "conv_b", "w_down", "norm_ple_g", "w_ple_gate", "w_ple_proj", "norm_final_g"]


def kernel(x, p, norm_mix_g, w_in, hg_lb_logits, hg_norm_g, s5_lambda_re, s5_lambda_im, s5_log_dt, s5_b_re, s5_b_im, s5_c_re, s5_c_im, s5_d, s5_glu_w, s5_glu_b, w_branch_hg, w_branch_s5, w_out, norm_ffn_g, w_up, conv_w, conv_b, w_down, norm_ple_g, w_ple_gate, w_ple_proj, norm_final_g, loss_target, m_norm_mix_g, m_w_in, m_hg_lb_logits, m_hg_norm_g, m_s5_lambda_re, m_s5_lambda_im, m_s5_log_dt, m_s5_b_re, m_s5_b_im, m_s5_c_re, m_s5_c_im, m_s5_d, m_s5_glu_w, m_s5_glu_b, m_w_branch_hg, m_w_branch_s5, m_w_out, m_norm_ffn_g, m_w_up, m_conv_w, m_conv_b, m_w_down, m_norm_ple_g, m_w_ple_gate, m_w_ple_proj, m_norm_final_g, v_norm_mix_g, v_w_in, v_hg_lb_logits, v_hg_norm_g, v_s5_lambda_re, v_s5_lambda_im, v_s5_log_dt, v_s5_b_re, v_s5_b_im, v_s5_c_re, v_s5_c_im, v_s5_d, v_s5_glu_w, v_s5_glu_b, v_w_branch_hg, v_w_branch_s5, v_w_out, v_norm_ffn_g, v_w_up, v_conv_w, v_conv_b, v_w_down, v_norm_ple_g, v_w_ple_gate, v_w_ple_proj, v_norm_final_g):
    W = dict(norm_mix_g=norm_mix_g, w_in=w_in, hg_lb_logits=hg_lb_logits, hg_norm_g=hg_norm_g, s5_lambda_re=s5_lambda_re, s5_lambda_im=s5_lambda_im, s5_log_dt=s5_log_dt, s5_b_re=s5_b_re, s5_b_im=s5_b_im, s5_c_re=s5_c_re, s5_c_im=s5_c_im, s5_d=s5_d, s5_glu_w=s5_glu_w, s5_glu_b=s5_glu_b, w_branch_hg=w_branch_hg, w_branch_s5=w_branch_s5, w_out=w_out, norm_ffn_g=norm_ffn_g, w_up=w_up, conv_w=conv_w, conv_b=conv_b, w_down=w_down, norm_ple_g=norm_ple_g, w_ple_gate=w_ple_gate, w_ple_proj=w_ple_proj, norm_final_g=norm_final_g)
    M = dict(norm_mix_g=m_norm_mix_g, w_in=m_w_in, hg_lb_logits=m_hg_lb_logits, hg_norm_g=m_hg_norm_g, s5_lambda_re=m_s5_lambda_re, s5_lambda_im=m_s5_lambda_im, s5_log_dt=m_s5_log_dt, s5_b_re=m_s5_b_re, s5_b_im=m_s5_b_im, s5_c_re=m_s5_c_re, s5_c_im=m_s5_c_im, s5_d=m_s5_d, s5_glu_w=m_s5_glu_w, s5_glu_b=m_s5_glu_b, w_branch_hg=m_w_branch_hg, w_branch_s5=m_w_branch_s5, w_out=m_w_out, norm_ffn_g=m_norm_ffn_g, w_up=m_w_up, conv_w=m_conv_w, conv_b=m_conv_b, w_down=m_w_down, norm_ple_g=m_norm_ple_g, w_ple_gate=m_w_ple_gate, w_ple_proj=m_w_ple_proj, norm_final_g=m_norm_final_g)
    V = dict(norm_mix_g=v_norm_mix_g, w_in=v_w_in, hg_lb_logits=v_hg_lb_logits, hg_norm_g=v_hg_norm_g, s5_lambda_re=v_s5_lambda_re, s5_lambda_im=v_s5_lambda_im, s5_log_dt=v_s5_log_dt, s5_b_re=v_s5_b_re, s5_b_im=v_s5_b_im, s5_c_re=v_s5_c_re, s5_c_im=v_s5_c_im, s5_d=v_s5_d, s5_glu_w=v_s5_glu_w, s5_glu_b=v_s5_glu_b, w_branch_hg=v_w_branch_hg, w_branch_s5=v_w_branch_s5, w_out=v_w_out, norm_ffn_g=v_norm_ffn_g, w_up=v_w_up, conv_w=v_conv_w, conv_b=v_conv_b, w_down=v_w_down, norm_ple_g=v_norm_ple_g, w_ple_gate=v_w_ple_gate, w_ple_proj=v_w_ple_proj, norm_final_g=v_norm_final_g)

    shard2 = {n: W[n][0] for n in _BIG}
    conv_bits = lax.bitcast_convert_type(shard2["conv_w"], bf16)
    groups = [["w_in", "s5_glu_w"], ["w_branch_hg", "w_branch_s5", "w_out", "w_ple_gate", "w_ple_proj"], ["w_up", "w_down"]]
    packs = [_pack([shard2[n] for n in grp] + ([conv_bits] if k == 0 else []), bf16, 16) for k, grp in enumerate(groups)]
    full = {}

    def take(k, gathered):
        pieces = _unpack(gathered, [shard2[n].shape for n in groups[k]] + ([conv_bits.shape] if k == 0 else []))
        full.update({n: _from_slabs(pc, _BIG_AXIS[n]) for n, pc in zip(groups[k], pieces)})
        return pieces

    conv_w_full = _from_slabs(lax.bitcast_convert_type(take(0, _all_gather("ag_weights", packs[0]))[-1], f32), 1)

    xt = x[0]
    pt = p[0, 0]
    tgt = loss_target[0]
    T = xt.shape[0]
    lam_re, lam_im, log_dt = s5_lambda_re[0], s5_lambda_im[0], s5_log_dt[0]
    b_re, b_im, c_re, c_im = s5_b_re[0], s5_b_im[0], s5_c_re[0], s5_c_im[0]

    def s5_prep(lam_re, lam_im, log_dt, b_re, b_im, c_re, c_im):
        a_re, a_im, bbar_re, bbar_im = _s5_discretise(lam_re, lam_im, log_dt, b_re, b_im)
        bdb, bdc = _s5_operands(bbar_re, bbar_im, c_re, c_im)
        return a_re, a_im, bdb, bdc

    (a_re, a_im, bdb, bdc), s5_prep_vjp = jax.vjp(s5_prep, lam_re, lam_im, log_dt, b_re, b_im, c_re, c_im)
    a_row = jnp.concatenate([a_re.reshape(1, SL), a_im.reshape(1, SL)], axis=1)
    bdb_b, bdc_b = bdb.astype(bf16), bdc.astype(bf16)

    h1, proj_hg, u_raw, gates, gathered1 = _in_proj(xt, norm_mix_g, full["w_in"], packs[1])
    take(1, gathered1)
    ng4 = jnp.tile(hg_norm_g, (1, NH))
    hg_o, sprev = _hgrn_fwd(proj_hg, hg_lb_logits, ng4)
    x_st, y_s5, s5_o, gathered2 = _s5_fwd(u_raw, a_row, bdb_b, bdc_b, s5_d, full["s5_glu_w"], s5_glu_b, packs[2])
    take(2, gathered2)
    x1, merged, h2, a_up = _mix_up(xt, hg_o, s5_o, gates, full["w_branch_hg"], full["w_branch_s5"], full["w_out"],
                                   norm_ffn_g, full["w_up"])
    (dx2, gated, g_a, g_b, d_w_pg, d_w_pp, loss_part, d_norm_final, d_norm_ple) = _ffn_tail(
        a_up, conv_w_full, conv_b, full["w_down"], x1, pt, norm_ple_g, full["w_ple_gate"], full["w_ple_proj"],
        norm_final_g.reshape(1, D), tgt)

    ids = jnp.stack([lax.axis_index("c"), 2 * lax.axis_index("x") + lax.axis_index("y")]).astype(jnp.int32)
    gw, own_sum, got3 = {}, {}, {}

    def slabs(names):
        return [_to_slabs(gw[n], _BIG_AXIS[n]).reshape(4, 2, *shard2[n].shape) for n in names]

    def add_pairs(names, g4, got):
        sums = [_add_halves("rs_add_" + n, g, r, ids) for n, g, r in zip(names, g4, got)]
        own_sum.update({n: own for n, (_, own) in zip(names, sums)})
        return [p16 for p16, _ in sums]

    grp_a = ["w_down", "w_ple_gate", "w_ple_proj"]
    gw["w_down"] = _wgrad("wg_down", gated, dx2)
    gw["w_ple_gate"] = d_w_pg
    gw["w_ple_proj"] = d_w_pp
    g4_a = slabs(grp_a)
    (da_up, dx1, d_conv_w, d_conv_b, d_norm_ffn), got_a = _ffn_bwd(
        dx2, a_up, g_a, g_b, conv_w_full, full["w_down"], full["w_up"], x1, norm_ffn_g, exch=_Exchange("sibling", g4_a))
    p16_a = add_pairs(grp_a, g4_a, got_a)
    (d_gates, d_hg_o, d_s5_o, d_w_bhg, d_w_bs5, d_w_out), got3_a = _mix_bwd(
        dx1, hg_o, s5_o, merged, gates, full["w_branch_hg"], full["w_branch_s5"], full["w_out"], exch=_Exchange("chips", p16_a))
    got3.update(zip(grp_a, got3_a))

    grp_b = ["w_up", "w_out", "w_branch_hg", "w_branch_s5", "conv_w"]
    gw["w_up"] = _wgrad("wg_up", h2, da_up)
    gw["w_out"] = d_w_out
    gw["w_branch_hg"] = d_w_bhg
    gw["w_branch_s5"] = d_w_bs5
    gw["conv_w"] = d_conv_w
    g4_b = slabs(grp_b)
    (d_proj_hg, d_lb, d_hg_norm), got_b = _hgrn_bwd(proj_hg, hg_lb_logits, ng4, sprev, d_hg_o,
                                                     exch=_Exchange("sibling", g4_b))
    p16_b = add_pairs(grp_b, g4_b, got_b)
    (d_u, d_w_glu, d_a_re, d_a_im, d_s5_d, d_glu_b, d_bdb, d_bdc), got3_b = _s5_bwd(
        d_s5_o, y_s5, u_raw, x_st, a_row, bdb_b, bdc_b, s5_d, full["s5_glu_w"], s5_glu_b, exch=_Exchange("chips", p16_b))
    got3.update(zip(grp_b, got3_b))
    grad_x, d_norm_mix = _in_bwd(d_proj_hg, d_u, d_gates, xt, dx1, full["w_in"], norm_mix_g)

    grp_c = ["w_in", "s5_glu_w"]
    gw["w_in"] = jnp.concatenate([_wgrad("wg_in_hg", h1, d_proj_hg), _wgrad("wg_in_u", h1, d_u),
                                  _wgrad("wg_in_gates", h1, d_gates)], axis=1)
    gw["s5_glu_w"] = d_w_glu
    g4_c = slabs(grp_c)
    p16_c = add_pairs(grp_c, g4_c, _exchange_call("rs_sibling", _Exchange("sibling", g4_c)))
    got3.update(zip(grp_c, _exchange_call("rs_chips", _Exchange("chips", p16_c))))

    (d_lam_re, d_lam_im, d_log_dt, d_b_re, d_b_im, d_c_re, d_c_im) = s5_prep_vjp(
        (d_a_re.reshape(SG, SN), d_a_im.reshape(SG, SN), d_bdb, d_bdc))
    sm = jax.nn.softmax(hg_lb_logits, axis=0)
    d_l0 = d_lb[0] * sm[0] * sm[1]
    d_logits = jnp.stack([d_l0, -d_l0], axis=0)

    gs = {"norm_mix_g": d_norm_mix, "hg_lb_logits": d_logits, "hg_norm_g": d_hg_norm, "s5_lambda_re": d_lam_re,
          "s5_lambda_im": d_lam_im, "s5_log_dt": d_log_dt, "s5_b_re": d_b_re, "s5_b_im": d_b_im, "s5_c_re": d_c_re,
          "s5_c_im": d_c_im, "s5_d": d_s5_d, "s5_glu_b": d_glu_b, "norm_ffn_g": d_norm_ffn, "conv_b": d_conv_b,
          "norm_ple_g": d_norm_ple, "norm_final_g": d_norm_final}

    big_out = [_adam_shard("adam_" + n, own_sum[n], got3[n], W[n], M[n], V[n]) for n in _BIG]

    two_d = lambda a: a.reshape(1, -1) if a.ndim == 1 else a
    dense = lambda a: a.reshape(SG, -1) if a.ndim == 4 else two_d(a)
    g_sum = _allreduce_small([dense(gs[n].reshape(W[n].shape)) for n in _SMALL])
    small_out = _adam_small([g.reshape(two_d(W[n]).shape) for g, n in zip(g_sum, _SMALL)], [two_d(W[n]) for n in _SMALL],
                            [two_d(M[n]) for n in _SMALL], [two_d(V[n]) for n in _SMALL])

    res = {}
    for k in range(4):
        d = {n: big_out[i][k] for i, n in enumerate(_BIG)}
        d.update({n: small_out[k * len(_SMALL) + i].reshape(W[n].shape) for i, n in enumerate(_SMALL)})
        res[k] = d
    loss = lax.psum(loss_part[0, 0], ("x", "y", "c"))
    return (loss, grad_x[None], *[res[0][n] for n in _ORDER], *[res[1][n] for n in _ORDER],
            *[res[2][n] for n in _ORDER], *[res[3][n] for n in _ORDER])
```

```python
import math

import jax
import jax.numpy as jnp
from jax import lax
from jax.experimental import pallas as pl
from jax.experimental.pallas import tpu as pltpu

f32 = jnp.float32
bf16 = jnp.bfloat16
MESH = pl.DeviceIdType.MESH

N_DEV = 8
D = 1024
HW = 512
HD = 128
NH = 4
CH = 64
SW = 512
SG = 32
SP = 16
SN = 64
SL = SG * SN
NST = 4
STW = SL // NST
DFF = 2816
PLE = 256
EPS = 1e-6
LANES = 1024
VMEM_LIMIT = 56 * 1024 * 1024

ADAM_LR, ADAM_B1, ADAM_B2, ADAM_EPS, ADAM_WD, ADAM_STEP = 0.001, 0.9, 0.999, 1e-08, 0.01, 10


def _pc(body, **kw):
    return pl.pallas_call(body, **kw)


def _params(n_axes=1, **kw):
    return pltpu.CompilerParams(dimension_semantics=("arbitrary",) * n_axes, vmem_limit_bytes=VMEM_LIMIT, **kw)


def _whole(shape):
    nd = len(shape)
    return pl.BlockSpec(shape, lambda *_: (0,) * nd, pipeline_mode=pl.Buffered(1))


def _acc(shape):
    nd = len(shape)
    return pl.BlockSpec(shape, lambda *_: (0,) * nd)


def _dot(a, b):
    return jnp.dot(a.astype(bf16), b.astype(bf16), preferred_element_type=f32)


def _dot_nt(a, b):
    return lax.dot_general(a.astype(bf16), b.astype(bf16), (((1,), (1,)), ((), ())), preferred_element_type=f32)


def _dot_tn(a, b):
    return lax.dot_general(a.astype(bf16), b.astype(bf16), (((0,), (0,)), ((), ())), preferred_element_type=f32)


def _sig(x):
    return jax.nn.sigmoid(x)


def _dsilu(z, s):
    return s * (1.0 + z * (1.0 - s))


_GC = math.sqrt(2.0 / math.pi)


def _gelu_and_grad(y):
    t = jnp.tanh(_GC * (y + 0.044715 * y * y * y))
    g = 0.5 * y * (1.0 + t)
    dg = 0.5 * (1.0 + t) + 0.5 * y * (1.0 - t * t) * _GC * (1.0 + 3.0 * 0.044715 * y * y)
    return g, dg


def _rms(x):
    r = lax.rsqrt(jnp.mean(x * x, axis=-1, keepdims=True) + EPS)
    return x * r, r


def _rms_bwd(dy, xh, r, g):
    dxh = dy * g
    dx = r * (dxh - xh * jnp.mean(dxh * xh, axis=-1, keepdims=True))
    return dx, dy * xh


def _colsum(x):
    return jnp.sum(x, axis=0, keepdims=True)


def _in_proj(x, g, w, ag_shard):
    T = x.shape[0]
    tm = 256
    nt = T // tm

    def body(x_ref, g_ref, w_ref, ag_ref, h_ref, hg_ref, u_ref, gt_ref, ago_ref, send_sems, recv_sems, local_sem):
        i = pl.program_id(0)
        start, forward, finish = _ag_steps(ag_ref, ago_ref, send_sems, recv_sems, local_sem)
        pl.when(i == 0)(start)
        xh, _ = _rms(x_ref[...])
        h = (xh * g_ref[...]).astype(bf16)
        h_ref[...] = h
        hg_ref[...] = jnp.dot(h, w_ref[:, 0:4 * HW], preferred_element_type=f32)
        u_ref[...] = jnp.dot(h, w_ref[:, 4 * HW:4 * HW + SW], preferred_element_type=f32)
        gt_ref[...] = jnp.dot(h, w_ref[:, 4 * HW + SW:], preferred_element_type=f32)
        pl.when(i == (3 * nt) // 4)(forward)
        pl.when(i == nt - 1)(finish)

    row = lambda n: pl.BlockSpec((tm, n), lambda i: (i, 0))
    return _pc(
        body, name="in_proj", grid=(nt,),
        in_specs=[row(D), _whole((1, D)), _whole(w.shape), ANY],
        out_specs=[row(D), row(4 * HW), row(SW), row(2 * D), ANY],
        out_shape=[jax.ShapeDtypeStruct((T, D), bf16), jax.ShapeDtypeStruct((T, 4 * HW), f32),
                   jax.ShapeDtypeStruct((T, SW), f32), jax.ShapeDtypeStruct((T, 2 * D), f32),
                   jax.ShapeDtypeStruct((N_DEV, *ag_shard.shape), ag_shard.dtype)],
        scratch_shapes=list(AG_SEMS),
        compiler_params=_params(),
    )(x, g, w, ag_shard)


HG_NC = 4


def _tri_matmul(tri, x):
    hi = x.astype(bf16)
    r1 = x - hi.astype(f32)
    mid = r1.astype(bf16)
    lo = (r1 - mid.astype(f32)).astype(bf16)
    n = x.shape[1]
    out = jnp.dot(tri.astype(bf16), jnp.concatenate([hi, mid, lo], axis=1), preferred_element_type=f32)
    return out[:, 0:n] + out[:, n:2 * n] + out[:, 2 * n:3 * n]


HG_TM = HG_NC * CH


def _chunk_tri(upper):
    r_i = lax.broadcasted_iota(jnp.int32, (HG_TM, HG_TM), 0)
    c_i = lax.broadcasted_iota(jnp.int32, (HG_TM, HG_TM), 1)
    same = (r_i // CH) == (c_i // CH)
    return same & ((c_i >= r_i) if upper else (r_i >= c_i))


def _heads(x3):
    n = x3.shape[2] // NH
    return jnp.concatenate([x3[:, :, h * n:(h + 1) * n] for h in range(NH)], axis=0)


def _unheads(xb):
    return jnp.concatenate([xb[h * HG_NC:(h + 1) * HG_NC] for h in range(NH)], axis=2)


def _bdot(a, b, ca, cb):
    return lax.dot_general(a.astype(bf16), b.astype(bf16), (((ca,), (cb,)), ((0,), (0,))), preferred_element_type=f32)


def _hgrn_gates(lg, qr, fr):
    mx = jnp.max(lg, axis=0, keepdims=True)
    e = jnp.exp(lg - mx)
    lb = e[0:1, :] / (e[0:1, :] + e[1:2, :])
    sig = _sig(fr)
    f = lb + (1.0 - lb) * sig
    k = 1.0 - f
    b = _tri_matmul(_chunk_tri(False), jnp.log(f).reshape(HG_TM, HW)).reshape(HG_NC, CH, HW)
    bref = b[:, CH // 2:CH // 2 + 1, :]
    blast = b[:, CH - 1:CH, :]
    sq = _sig(qr)
    q = qr * sq
    e1 = jnp.exp(b - bref)
    e2 = jnp.exp(bref - b)
    e3 = jnp.exp(blast - b)
    e4 = jnp.exp(b)
    r_i = lax.broadcasted_iota(jnp.int32, (CH, CH), 0)
    c_i = lax.broadcasted_iota(jnp.int32, (CH, CH), 1)
    return dict(lb=lb, qr=qr, sq=sq, sig=sig, f=f, k=k, tril=(r_i >= c_i), e1=e1, e2=e2, e3=e3, e4=e4,
                qs=q * e1, ks=k * e2, kl=k * e3, qb=q * e4, dec=jnp.exp(blast))


def _hgrn_fwd(proj_hg, logits, ng4):
    T = proj_hg.shape[0]
    nch = T // CH
    tm = HG_NC * CH

    def body(q_ref, f_ref, i_ref, og_ref, lg_ref, ng_ref, out_ref, sprev_ref, st_ref):
        @pl.when(pl.program_id(0) == 0)
        def _():
            st_ref[...] = jnp.zeros_like(st_ref)

        three = lambda ref: ref[...].reshape(HG_NC, CH, HW)
        c = _hgrn_gates(lg_ref[...], three(q_ref), three(f_ref))
        qs, ks, kl, qb, dec = (_heads(c[n]) for n in ("qs", "ks", "kl", "qb", "dec"))
        vb = _heads(three(i_ref))
        p = jnp.where(c["tril"], _bdot(qs, ks, 2, 2), 0.0)
        ut = _bdot(vb, kl, 1, 1)
        sts = []
        for h in range(NH):
            st = st_ref[h]
            for ci in range(HG_NC):
                sts.append(st)
                sprev_ref[ci, h] = st
                st = dec[h * HG_NC + ci] * st + ut[h * HG_NC + ci]
            st_ref[h] = st
        o = _bdot(p, vb, 2, 1) + _bdot(qb, jnp.stack(sts), 2, 2)
        og = og_ref[...]
        out_ref[...] = (_unheads(_rms(o)[0]).reshape(HG_TM, HW) * ng_ref[...] * (og * _sig(og))).astype(bf16)

    col = lambda j: pl.BlockSpec((tm, HW), lambda n, j=j: (n, j))
    return _pc(
        body, name="hgrn_fwd", grid=(nch // HG_NC,),
        in_specs=[col(0), col(1), col(2), col(3), _whole((2, HW)), _whole((1, HW))],
        out_specs=[pl.BlockSpec((tm, HW), lambda n: (n, 0)),
                   pl.BlockSpec((HG_NC, NH, HD, HD), lambda n: (n, 0, 0, 0))],
        out_shape=[jax.ShapeDtypeStruct((T, HW), bf16), jax.ShapeDtypeStruct((nch, NH, HD, HD), f32)],
        scratch_shapes=[pltpu.VMEM((NH, HD, HD), f32)],
        compiler_params=_params(),
    )(proj_hg, proj_hg, proj_hg, proj_hg, logits, ng4)


def _hgrn_bwd(proj_hg, logits, ng4, sprev, d_out, exch=None):
    T = proj_hg.shape[0]
    nch = T // CH
    tm = HG_NC * CH
    nst = nch // HG_NC

    def body(q_ref, f_ref, i_ref, og_ref, lg_ref, ng_ref, sp_ref, do_ref, dp_ref, dlb_ref, dng_ref, gt_ref):
        @pl.when(pl.program_id(0) == 0)
        def _():
            gt_ref[...] = jnp.zeros_like(gt_ref)
            dlb_ref[...] = jnp.zeros_like(dlb_ref)
            dng_ref[...] = jnp.zeros_like(dng_ref)

        three = lambda x: x.reshape(HG_NC, CH, HW)
        flat = lambda x: x.reshape(HG_TM, HW)
        c = _hgrn_gates(lg_ref[...], three(q_ref[...]), three(f_ref[...]))
        tril = c["tril"]
        ng = ng_ref[:, 0:HD]
        og = og_ref[...]
        sog = _sig(og)
        d_gated = do_ref[...]
        qs, ks, kl, qb, dec = (_heads(c[n]) for n in ("qs", "ks", "kl", "qb", "dec"))
        vb = _heads(three(i_ref[...]))
        spb = jnp.stack([sp_ref[ci, h] for h in range(NH) for ci in range(HG_NC)])
        p = jnp.where(tril, _bdot(qs, ks, 2, 2), 0.0)
        o = _bdot(p, vb, 2, 1) + _bdot(qb, spb, 2, 2)
        oh, r = _rms(o)
        d_o, dng_rows = _rms_bwd(_heads(three(d_gated * (og * sog))), oh, r, ng)
        dng_ref[...] += _colsum(jnp.sum(dng_rows, axis=0))
        dp = jnp.where(tril, _bdot(d_o, vb, 2, 2), 0.0)
        dst = _bdot(d_o, qb, 1, 1)
        gts = [None] * (NH * HG_NC)
        for h in range(NH):
            gt = gt_ref[h]
            for ci in reversed(range(HG_NC)):
                gts[h * HG_NC + ci] = gt
                gt = dst[h * HG_NC + ci] + dec[h * HG_NC + ci] * gt
            gt_ref[h] = gt
        gtb = jnp.stack(gts)
        dqs = _unheads(_bdot(dp, ks, 2, 1))
        dks = _unheads(_bdot(dp, qs, 1, 1))
        dkl = _unheads(_bdot(vb, gtb, 2, 1))
        dqb = _unheads(_bdot(d_o, spb, 2, 1))
        dv = _unheads(_bdot(p, d_o, 1, 1) + _bdot(kl, gtb, 2, 2))
        ddec = _unheads(jnp.sum(gtb * spb, axis=1, keepdims=True))
        dq = dqs * c["e1"] + dqb * c["e4"]
        dk = dks * c["e2"] + dkl * c["e3"]
        t_qs = dqs * c["qs"]
        t_ks = dks * c["ks"]
        t_kl = dkl * c["kl"]
        db = t_qs - t_ks - t_kl + dqb * c["qb"]
        dbref = jnp.sum(t_ks - t_qs, axis=1, keepdims=True)
        dblast = jnp.sum(t_kl, axis=1, keepdims=True) + ddec * c["dec"]
        row = lax.broadcasted_iota(jnp.int32, (HG_NC, CH, HW), 1)
        db = db + jnp.where(row == CH // 2, dbref, 0.0) + jnp.where(row == CH - 1, dblast, 0.0)
        df = three(_tri_matmul(_chunk_tri(True), flat(db))) / c["f"] - dk
        sig = c["sig"]
        dlb_ref[...] += _colsum(jnp.sum(df * (1.0 - sig), axis=0))
        dp_ref[:, 0:HW] = flat(dq * _dsilu(c["qr"], c["sq"]))
        dp_ref[:, HW:2 * HW] = flat(df * (1.0 - c["lb"]) * sig * (1.0 - sig))
        dp_ref[:, 2 * HW:3 * HW] = flat(dv)
        dp_ref[:, 3 * HW:4 * HW] = d_gated * flat(_unheads(oh * ng)) * _dsilu(og, sog)

    rev = lambda n: nst - 1 - n
    col = lambda j: pl.BlockSpec((tm, HW), lambda n, j=j: (rev(n), j))
    return _pc_behind(
        body, exch, nst, name="hgrn_bwd", grid=(nst,),
        in_specs=[col(0), col(1), col(2), col(3), _whole((2, HW)), _whole((1, HW)),
                  pl.BlockSpec((HG_NC, NH, HD, HD), lambda n: (rev(n), 0, 0, 0)),
                  pl.BlockSpec((tm, HW), lambda n: (rev(n), 0))],
        out_specs=[pl.BlockSpec((tm, 4 * HW), lambda n: (rev(n), 0)), _acc((1, HW)), _acc((1, HD))],
        out_shape=[jax.ShapeDtypeStruct((T, 4 * HW), f32), jax.ShapeDtypeStruct((1, HW), f32),
                   jax.ShapeDtypeStruct((1, HD), f32)],
        scratch_shapes=[pltpu.VMEM((NH, HD, HD), f32)],
        compiler_params=_params(), args=(proj_hg, proj_hg, proj_hg, proj_hg, logits, ng4, sprev, d_out))


S5_TM = 256
S5_SEG = 8
S5_STEPS = S5_TM // S5_SEG
NLT = SL // 128


def _s5_tables(a_ref, pw_ref, pseg_ref, descending):
    re, im = slice(0, SL), slice(SL, 2 * SL)

    def cmul(ar, ai, br, bi):
        return ar * br - ai * bi, ar * bi + ai * br

    pw_ref[0:1, :] = a_ref[...]
    m = 1
    while m < S5_STEPS:
        pr, pi = cmul(pw_ref[0:m, re], pw_ref[0:m, im], pw_ref[m - 1:m, re], pw_ref[m - 1:m, im])
        pw_ref[m:2 * m, re] = pr
        pw_ref[m:2 * m, im] = pi
        m *= 2
    base = S5_STEPS - 1
    if descending:
        pseg_ref[7:8, :] = pw_ref[base:base + 1, :]
        m = 1
        while m < 8:
            pr, pi = cmul(pseg_ref[8 - m:8, re], pseg_ref[8 - m:8, im], pseg_ref[8 - m:9 - m, re], pseg_ref[8 - m:9 - m, im])
            pseg_ref[8 - 2 * m:8 - m, re] = pr
            pseg_ref[8 - 2 * m:8 - m, im] = pi
            m *= 2
    else:
        pseg_ref[0:1, :] = pw_ref[base:base + 1, :]
        m = 1
        while m < 8:
            pr, pi = cmul(pseg_ref[0:m, re], pseg_ref[0:m, im], pseg_ref[m - 1:m, re], pseg_ref[m - 1:m, im])
            pseg_ref[m:2 * m, re] = pr
            pseg_ref[m:2 * m, im] = pi
            m *= 2


def _seg_rows(j):
    return pl.ds(j * S5_SEG, S5_SEG)


def _seg_perm(transpose=False):
    r_i = lax.broadcasted_iota(jnp.int32, (S5_TM, S5_TM), 0)
    c_i = lax.broadcasted_iota(jnp.int32, (S5_TM, S5_TM), 1)
    if transpose:
        r_i, c_i = c_i, r_i
    return c_i == S5_STEPS * (r_i % S5_SEG) + r_i // S5_SEG


def _scan_fwd(x3_ref, pw_ref, pseg_ref, carry_ref):
    row8 = lax.broadcasted_iota(jnp.int32, (S5_SEG, 128), 0)
    for lt in range(NLT):
        kr, ki = lt, NLT + lt
        lr, li = slice(lt * 128, (lt + 1) * 128), slice(SL + lt * 128, SL + (lt + 1) * 128)
        ar, ai = pw_ref[0:1, lr], pw_ref[0:1, li]
        sr = jnp.zeros((S5_SEG, 128), f32)
        si = jnp.zeros((S5_SEG, 128), f32)
        for j in range(S5_STEPS):
            sr, si = ar * sr - ai * si + x3_ref[kr, _seg_rows(j), :], ar * si + ai * sr + x3_ref[ki, _seg_rows(j), :]
            x3_ref[kr, _seg_rows(j), :] = sr
            x3_ref[ki, _seg_rows(j), :] = si
        for d in (1, 2, 4):
            pr, pi = pseg_ref[d - 1:d, lr], pseg_ref[d - 1:d, li]
            tr, ti = pltpu.roll(sr, d, 0), pltpu.roll(si, d, 0)
            m = row8 >= d
            sr, si = sr + jnp.where(m, pr * tr - pi * ti, 0.0), si + jnp.where(m, pr * ti + pi * tr, 0.0)
        c0r, c0i = carry_ref[7:8, lr], carry_ref[7:8, li]
        qr, qi = pseg_ref[:, lr], pseg_ref[:, li]
        sr, si = sr + qr * c0r - qi * c0i, si + qr * c0i + qi * c0r
        carry_ref[:, lr] = sr
        carry_ref[:, li] = si
        cr = jnp.where(row8 == 0, c0r, pltpu.roll(sr, 1, 0))
        ci = jnp.where(row8 == 0, c0i, pltpu.roll(si, 1, 0))
        for j in range(S5_STEPS):
            pr, pi = pw_ref[j:j + 1, lr], pw_ref[j:j + 1, li]
            x3_ref[kr, _seg_rows(j), :] = x3_ref[kr, _seg_rows(j), :] + pr * cr - pi * ci
            x3_ref[ki, _seg_rows(j), :] = x3_ref[ki, _seg_rows(j), :] + pr * ci + pi * cr


def _scan_bwd(g3_ref, x3_ref, xh_ref, first, pw_ref, pseg_ref, carry_ref, dar_ref, dai_ref):
    row8 = lax.broadcasted_iota(jnp.int32, (S5_SEG, 128), 0)
    for lt in range(NLT):
        kr, ki = lt, NLT + lt
        lr, li = slice(lt * 128, (lt + 1) * 128), slice(SL + lt * 128, SL + (lt + 1) * 128)
        ar, ai = pw_ref[0:1, lr], pw_ref[0:1, li]
        sr = jnp.zeros((S5_SEG, 128), f32)
        si = jnp.zeros((S5_SEG, 128), f32)
        for j in reversed(range(S5_STEPS)):
            sr, si = ar * sr + ai * si + g3_ref[kr, _seg_rows(j), :], ar * si - ai * sr + g3_ref[ki, _seg_rows(j), :]
            g3_ref[kr, _seg_rows(j), :] = sr
            g3_ref[ki, _seg_rows(j), :] = si
        for d in (1, 2, 4):
            pr, pi = pseg_ref[8 - d:9 - d, lr], pseg_ref[8 - d:9 - d, li]
            tr, ti = pltpu.roll(sr, 8 - d, 0), pltpu.roll(si, 8 - d, 0)
            m = row8 < 8 - d
            sr, si = sr + jnp.where(m, pr * tr + pi * ti, 0.0), si + jnp.where(m, pr * ti - pi * tr, 0.0)
        c0r, c0i = carry_ref[0:1, lr], carry_ref[0:1, li]
        qr, qi = pseg_ref[:, lr], pseg_ref[:, li]
        sr, si = sr + qr * c0r + qi * c0i, si + qr * c0i - qi * c0r
        carry_ref[:, lr] = sr
        carry_ref[:, li] = si
        cr = jnp.where(row8 == 7, c0r, pltpu.roll(sr, 7, 0))
        ci = jnp.where(row8 == 7, c0i, pltpu.roll(si, 7, 0))
        hr = jnp.where(first, 0.0, xh_ref[kr, 7:8, :])
        hi = jnp.where(first, 0.0, xh_ref[ki, 7:8, :])
        acc_r = jnp.zeros((S5_SEG, 128), f32)
        acc_i = jnp.zeros((S5_SEG, 128), f32)
        for j in range(S5_STEPS):
            pr, pi = pw_ref[S5_STEPS - 1 - j:S5_STEPS - j, lr], pw_ref[S5_STEPS - 1 - j:S5_STEPS - j, li]
            lam_r = g3_ref[kr, _seg_rows(j), :] + pr * cr + pi * ci
            lam_i = g3_ref[ki, _seg_rows(j), :] + pr * ci - pi * cr
            g3_ref[kr, _seg_rows(j), :] = lam_r
            g3_ref[ki, _seg_rows(j), :] = lam_i
            if j == 0:
                xpr = jnp.where(row8 == 0, hr, pltpu.roll(x3_ref[kr, _seg_rows(S5_STEPS - 1), :], 1, 0))
                xpi = jnp.where(row8 == 0, hi, pltpu.roll(x3_ref[ki, _seg_rows(S5_STEPS - 1), :], 1, 0))
            else:
                xpr = x3_ref[kr, _seg_rows(j - 1), :]
                xpi = x3_ref[ki, _seg_rows(j - 1), :]
            acc_r = acc_r + lam_r * xpr + lam_i * xpi
            acc_i = acc_i + lam_i * xpr - lam_r * xpi
        dar_ref[:, lr] += _colsum(acc_r)
        dai_ref[:, lr] += _colsum(acc_i)


def _strip(x3_ref, part, s):
    k0 = part * NLT + s * (STW // 128)
    return jnp.concatenate([x3_ref[k0 + q] for q in range(STW // 128)], axis=1)


def _s5_fwd(u, a_row, bdb, bdc, dskip, glu_w, glu_b, ag_shard):
    T = u.shape[0]
    tm = S5_TM
    nt = T // tm

    def body(u_ref, a_ref, bdb_ref, bdc_ref, ds_ref, gw_ref, gb_ref, ag_ref, x_ref, y_ref, o_ref, ago_ref,
             pw_ref, pseg_ref, carry_ref, send_sems, recv_sems, local_sem):
        i = pl.program_id(0)
        start, forward, finish = _ag_steps(ag_ref, ago_ref, send_sems, recv_sems, local_sem)
        pl.when(i == 0)(start)

        @pl.when(i == 0)
        def _():
            carry_ref[...] = jnp.zeros_like(carry_ref)
            _s5_tables(a_ref, pw_ref, pseg_ref, descending=False)

        uv = u_ref[...]
        ub = jnp.dot(_seg_perm().astype(bf16), uv.astype(bf16), preferred_element_type=f32).astype(bf16)
        for part in range(2):
            for s in range(NST):
                bu = jnp.dot(ub[:, s * 128:(s + 1) * 128], bdb_ref[part * NST + s], preferred_element_type=f32)
                for q in range(STW // 128):
                    x_ref[part * NLT + s * (STW // 128) + q] = bu[:, q * 128:(q + 1) * 128]
        _scan_fwd(x_ref, pw_ref, pseg_ref, carry_ref)
        ys = []
        for s in range(NST):
            acc = None
            for part in range(2):
                t = jnp.dot(_strip(x_ref, part, s).astype(bf16), bdc_ref[part * NST + s], preferred_element_type=f32)
                acc = t if acc is None else acc + t
            ys.append(acc)
        y = _tri_matmul(_seg_perm(transpose=True), jnp.concatenate(ys, axis=1)) + ds_ref[...] * uv
        y_ref[...] = y
        g, _ = _gelu_and_grad(y)
        z = jnp.dot(g.astype(bf16), gw_ref[...], preferred_element_type=f32) + gb_ref[...]
        o_ref[...] = (g * _sig(z)).astype(bf16)
        pl.when(i == (3 * nt) // 4)(forward)
        pl.when(i == nt - 1)(finish)

    row = lambda n: pl.BlockSpec((tm, n), lambda i: (i, 0))
    return _pc(
        body, name="s5_fwd", grid=(nt,),
        in_specs=[row(SW), _whole((1, 2 * SL)), _whole(bdb.shape), _whole(bdc.shape), _whole((1, SW)),
                  _whole((SW, SW)), _whole((1, SW)), ANY],
        out_specs=[pl.BlockSpec((2 * NLT, tm, 128), lambda i: (0, i, 0)), row(SW), row(SW), ANY],
        out_shape=[jax.ShapeDtypeStruct((2 * NLT, T, 128), f32), jax.ShapeDtypeStruct((T, SW), f32),
                   jax.ShapeDtypeStruct((T, SW), bf16),
                   jax.ShapeDtypeStruct((N_DEV, *ag_shard.shape), ag_shard.dtype)],
        scratch_shapes=[pltpu.VMEM((S5_STEPS, 2 * SL), f32), pltpu.VMEM((8, 2 * SL), f32), pltpu.VMEM((8, 2 * SL), f32)]
        + list(AG_SEMS),
        compiler_params=_params(),
    )(u, a_row, bdb, bdc, dskip, glu_w, glu_b, ag_shard)


def _s5_bwd(d_out, y, u, x, a_row, bdb, bdc, dskip, glu_w, glu_b, exch=None):
    T = u.shape[0]
    tm = S5_TM
    nt = T // tm

    def body(do_ref, y_ref, u_ref, x_ref, xh_ref, a_ref, bdb_ref, bdc_ref, ds_ref, gw_ref, gb_ref,
             du_ref, dglu_ref, dar_ref, dai_ref, dd_ref, dgb_ref, dbdb_ref, dbdc_ref, gs_ref, pw_ref, pseg_ref, carry_ref):
        i = pl.program_id(0)

        @pl.when(i == 0)
        def _():
            carry_ref[...] = jnp.zeros_like(carry_ref)
            _s5_tables(a_ref, pw_ref, pseg_ref, descending=True)
            dar_ref[...] = jnp.zeros_like(dar_ref)
            dai_ref[...] = jnp.zeros_like(dai_ref)
            dd_ref[...] = jnp.zeros_like(dd_ref)
            dgb_ref[...] = jnp.zeros_like(dgb_ref)
            dglu_ref[...] = jnp.zeros_like(dglu_ref)
            dbdb_ref[...] = jnp.zeros_like(dbdb_ref)
            dbdc_ref[...] = jnp.zeros_like(dbdc_ref)

        yv = y_ref[...]
        uv = u_ref[...]
        g, gp = _gelu_and_grad(yv)
        z = jnp.dot(g.astype(bf16), gw_ref[...], preferred_element_type=f32) + gb_ref[...]
        sg = _sig(z)
        do = do_ref[...].astype(f32)
        dz = do * g * sg * (1.0 - sg)
        dglu_ref[...] += _dot_tn(g, dz)
        dgb_ref[...] += _colsum(dz)
        dy = (do * sg + _dot_nt(dz, gw_ref[...])) * gp
        perm = _seg_perm().astype(bf16)
        dyb = jnp.dot(perm, dy.astype(bf16), preferred_element_type=f32).astype(bf16)
        dd_ref[...] += _colsum(dy * uv)
        for part in range(2):
            for s in range(NST):
                gx = lax.dot_general(dyb[:, s * 128:(s + 1) * 128], bdc_ref[part * NST + s], (((1,), (1,)), ((), ())),
                                     preferred_element_type=f32)
                for q in range(STW // 128):
                    gs_ref[part * NLT + s * (STW // 128) + q] = gx[:, q * 128:(q + 1) * 128]
        _scan_bwd(gs_ref, x_ref, xh_ref, i == nt - 1, pw_ref, pseg_ref, carry_ref, dar_ref, dai_ref)
        ub = jnp.dot(perm, uv.astype(bf16), preferred_element_type=f32).astype(bf16)
        dus = []
        for s in range(NST):
            acc = None
            for part in range(2):
                lv = _strip(gs_ref, part, s).astype(bf16)
                t = lax.dot_general(lv, bdb_ref[part * NST + s], (((1,), (1,)), ((), ())), preferred_element_type=f32)
                acc = t if acc is None else acc + t
                dbdb_ref[part * NST + s] += _dot_tn(ub[:, s * 128:(s + 1) * 128], lv)
                dbdc_ref[part * NST + s] += _dot_tn(_strip(x_ref, part, s), dyb[:, s * 128:(s + 1) * 128])
            dus.append(acc)
        du_ref[...] = _tri_matmul(_seg_perm(transpose=True), jnp.concatenate(dus, axis=1)) + dy * ds_ref[...]

    rev = lambda i: nt - 1 - i
    row = lambda n: pl.BlockSpec((tm, n), lambda i: (rev(i), 0))
    xblk = pl.BlockSpec((2 * NLT, tm, 128), lambda i: (0, rev(i), 0))
    halo = pl.BlockSpec((2 * NLT, 8, 128), lambda i: (0, jnp.maximum(rev(i) * (tm // 8) - 1, 0), 0))
    return _pc_behind(
        body, exch, nt, name="s5_bwd", grid=(nt,),
        in_specs=[row(SW), row(SW), row(SW), xblk, halo, _whole((1, 2 * SL)), _whole(bdb.shape), _whole(bdc.shape),
                  _whole((1, SW)), _whole((SW, SW)), _whole((1, SW))],
        out_specs=[row(SW), _acc((SW, SW)), _acc((1, SL)), _acc((1, SL)), _acc((1, SW)), _acc((1, SW)),
                   _acc(bdb.shape), _acc(bdc.shape)],
        out_shape=[jax.ShapeDtypeStruct((T, SW), f32), jax.ShapeDtypeStruct((SW, SW), f32),
                   jax.ShapeDtypeStruct((1, SL), f32), jax.ShapeDtypeStruct((1, SL), f32),
                   jax.ShapeDtypeStruct((1, SW), f32), jax.ShapeDtypeStruct((1, SW), f32),
                   jax.ShapeDtypeStruct(bdb.shape, f32), jax.ShapeDtypeStruct(bdc.shape, f32)],
        scratch_shapes=[pltpu.VMEM((2 * NLT, tm, 128), f32), pltpu.VMEM((S5_STEPS, 2 * SL), f32),
                        pltpu.VMEM((8, 2 * SL), f32), pltpu.VMEM((8, 2 * SL), f32)],
        compiler_params=_params(), args=(d_out, y, u, x, x, a_row, bdb, bdc, dskip, glu_w, glu_b))


def _mix_up(x, hg_o, s5_o, gates, w_bhg, w_bs5, w_out, g_ffn, w_up):
    T = x.shape[0]
    tm = 256

    def body(x_ref, hg_ref, s5_ref, gt_ref, wh_ref, ws_ref, wo_ref, g_ref, wu_ref, x1_ref, mg_ref, h2_ref, a_ref):
        yh = jnp.dot(hg_ref[...], wh_ref[...], preferred_element_type=f32)
        ys = jnp.dot(s5_ref[...], ws_ref[...], preferred_element_type=f32)
        merged = (_sig(gt_ref[:, 0:D]) * yh + _sig(gt_ref[:, D:2 * D]) * ys).astype(bf16)
        mg_ref[...] = merged
        x1 = x_ref[...] + jnp.dot(merged, wo_ref[...], preferred_element_type=f32)
        x1_ref[...] = x1
        xh, _ = _rms(x1)
        h2 = (xh * g_ref[...]).astype(bf16)
        h2_ref[...] = h2
        a_ref[...] = jnp.dot(h2, wu_ref[...], preferred_element_type=f32)

    row = lambda n: pl.BlockSpec((tm, n), lambda i: (i, 0))
    return _pc(
        body, name="mix_up", grid=(T // tm,),
        in_specs=[row(D), row(HW), row(SW), row(2 * D), _whole(w_bhg.shape), _whole(w_bs5.shape), _whole(w_out.shape),
                  _whole((1, D)), _whole(w_up.shape)],
        out_specs=[row(D), row(D), row(D), row(2 * DFF)],
        out_shape=[jax.ShapeDtypeStruct((T, D), f32), jax.ShapeDtypeStruct((T, D), bf16),
                   jax.ShapeDtypeStruct((T, D), bf16), jax.ShapeDtypeStruct((T, 2 * DFF), f32)],
        compiler_params=_params(),
    )(x, hg_o, s5_o, gates, w_bhg, w_bs5, w_out, g_ffn, w_up)


FFN_TM = 128
FFN_FS = 256


def _conv_gelu_blocks(a_ref, ah_ref, first, cw_ref, cb_ref, ga_ref, gb_ref, gated_ref):
    row8 = lax.broadcasted_iota(jnp.int32, (8, FFN_FS), 0)
    for s in range(DFF // FFN_FS):
        halves = (slice(s * FFN_FS, (s + 1) * FFN_FS), slice(DFF + s * FFN_FS, DFF + (s + 1) * FFN_FS))
        w = [[cw_ref[k:k + 1, ln] for k in range(3)] for ln in halves]
        bias = [cb_ref[:, ln] for ln in halves]
        prev = [jnp.where(first, 0.0, ah_ref[:, ln]) for ln in halves]
        p1 = [pltpu.roll(p, 1, 0) for p in prev]
        p2 = [pltpu.roll(p, 2, 0) for p in prev]
        for j in range(FFN_TM // 8):
            rows = slice(8 * j, 8 * j + 8)
            c = []
            for hf, ln in enumerate(halves):
                av = a_ref[rows, ln]
                r1, r2 = pltpu.roll(av, 1, 0), pltpu.roll(av, 2, 0)
                a1 = jnp.where(row8 >= 1, r1, p1[hf])
                a2 = jnp.where(row8 >= 2, r2, p2[hf])
                c.append(bias[hf] + w[hf][0] * a2 + w[hf][1] * a1 + w[hf][2] * av)
                p1[hf], p2[hf] = r1, r2
            gl, gp = _gelu_and_grad(c[0])
            ga_ref[rows, halves[0]] = c[1] * gp
            gb_ref[rows, halves[0]] = gl
            gated_ref[rows, halves[0]] = gl * c[1]


def _ffn_tail(a, conv_w, conv_b, w_down, x1, p, g_ple, w_pg, w_pp, g_fin, tgt):
    T = a.shape[0]
    tm = FFN_TM

    def body(a_ref, ah_ref, cw_ref, cb_ref, wd_ref, x1_ref, p_ref, gp_ref, wpg_ref, wpp_ref, gf_ref, t_ref,
             dx2_ref, gd_ref, ga_ref, gb_ref, dwpg_ref, dwpp_ref, loss_ref, dgf_ref, dgp_ref, gsc_ref):
        i = pl.program_id(0)

        @pl.when(i == 0)
        def _():
            loss_ref[...] = jnp.zeros_like(loss_ref)
            dgf_ref[...] = jnp.zeros_like(dgf_ref)
            dgp_ref[...] = jnp.zeros_like(dgp_ref)
            dwpg_ref[...] = jnp.zeros_like(dwpg_ref)
            dwpp_ref[...] = jnp.zeros_like(dwpp_ref)

        _conv_gelu_blocks(a_ref, ah_ref, i == 0, cw_ref, cb_ref, ga_ref, gb_ref, gsc_ref)
        gated = gsc_ref[...].astype(bf16)
        gd_ref[...] = gated
        x2 = x1_ref[...] + jnp.dot(gated, wd_ref[...], preferred_element_type=f32)
        xh2, r2 = _rms(x2)
        h3 = (xh2 * gp_ref[...]).astype(bf16)
        pg = _sig(jnp.dot(h3, wpg_ref[...], preferred_element_type=f32))
        pp = _dot(p_ref[...], wpp_ref[...])
        x3 = x2 + pg * pp
        xh3, r3 = _rms(x3)
        diff = xh3 * gf_ref[...] - t_ref[...]
        loss_ref[...] += 0.5 * jnp.sum(jnp.mean(diff * diff, axis=-1, keepdims=True), axis=0, keepdims=True)
        dy = diff * (1.0 / D)
        dx3, dgf_rows = _rms_bwd(dy, xh3, r3, gf_ref[...])
        dgf_ref[...] += _colsum(dgf_rows)
        dwpp_ref[...] += _dot_tn(p_ref[...], dx3 * pg)
        dz = (dx3 * pp * pg * (1.0 - pg)).astype(bf16)
        dwpg_ref[...] += _dot_tn(h3, dz)
        dh3 = _dot_nt(dz, wpg_ref[...])
        dx2n, dgp_rows = _rms_bwd(dh3, xh2, r2, gp_ref[...])
        dgp_ref[...] += _colsum(dgp_rows)
        dx2_ref[...] = dx3 + dx2n

    row = lambda n: pl.BlockSpec((tm, n), lambda i: (i, 0))
    halo = pl.BlockSpec((8, 2 * DFF), lambda i: (jnp.maximum(i * (tm // 8) - 1, 0), 0))
    return _pc(
        body, name="ffn_tail", grid=(T // tm,),
        in_specs=[row(2 * DFF), halo, _whole((3, 2 * DFF)), _whole((1, 2 * DFF)), _whole(w_down.shape), row(D), row(PLE),
                  _whole((1, D)), _whole(w_pg.shape), _whole(w_pp.shape), _whole((1, D)), row(D)],
        out_specs=[row(D), row(DFF), row(DFF), row(DFF), _acc(w_pg.shape), _acc(w_pp.shape),
                   _acc((1, 128)), _acc((1, D)), _acc((1, D))],
        out_shape=[jax.ShapeDtypeStruct((T, D), f32), jax.ShapeDtypeStruct((T, DFF), bf16),
                   jax.ShapeDtypeStruct((T, DFF), f32), jax.ShapeDtypeStruct((T, DFF), f32),
                   jax.ShapeDtypeStruct(w_pg.shape, f32), jax.ShapeDtypeStruct(w_pp.shape, f32),
                   jax.ShapeDtypeStruct((1, 128), f32), jax.ShapeDtypeStruct((1, D), f32), jax.ShapeDtypeStruct((1, D), f32)],
        scratch_shapes=[pltpu.VMEM((tm, DFF), f32)],
        compiler_params=_params(),
    )(a, a, conv_w, conv_b, w_down, x1, p, g_ple, w_pg, w_pp, g_fin, tgt)


def _ffn_bwd(dx2, a, g_a, g_b, conv_w, w_down, w_up, x1, g_ffn, exch=None):
    T = a.shape[0]
    tm = FFN_TM
    nt = T // tm

    def body(dx2_ref, a_ref, ga_ref, gb_ref, cw_ref, wd_ref, wu_ref, x1_ref, g_ref,
             da_ref, dx1_ref, dcw_ref, dcb_ref, dg_ref, carry_ref, dgd_ref, dasc_ref):
        i = pl.program_id(0)

        @pl.when(i == 0)
        def _():
            carry_ref[...] = jnp.zeros_like(carry_ref)
            dcw_ref[...] = jnp.zeros_like(dcw_ref)
            dcb_ref[...] = jnp.zeros_like(dcb_ref)
            dg_ref[...] = jnp.zeros_like(dg_ref)

        dx2 = dx2_ref[...]
        dgd_ref[...] = _dot_nt(dx2, wd_ref[...])
        row8 = lax.broadcasted_iota(jnp.int32, (8, FFN_FS), 0)
        for s in range(DFF // FFN_FS):
            src = slice(s * FFN_FS, (s + 1) * FFN_FS)
            halves = (src, slice(DFF + s * FFN_FS, DFF + (s + 1) * FFN_FS))
            w = [[cw_ref[k:k + 1, ln] for k in range(3)] for ln in halves]
            nxt = [carry_ref[:, ln] for ln in halves]
            n7 = [pltpu.roll(v, 7, 0) for v in nxt]
            n6 = [pltpu.roll(v, 6, 0) for v in nxt]
            acc = [[jnp.zeros((8, FFN_FS), f32) for _ in range(4)] for _ in halves]
            for j in reversed(range(tm // 8)):
                rows = slice(8 * j, 8 * j + 8)
                dg = dgd_ref[rows, src]
                for hf, (ln, saved) in enumerate(zip(halves, (ga_ref, gb_ref))):
                    dc = dg * saved[rows, src]
                    r7, r6 = pltpu.roll(dc, 7, 0), pltpu.roll(dc, 6, 0)
                    up1 = jnp.where(row8 < 7, r7, n7[hf])
                    up2 = jnp.where(row8 < 6, r6, n6[hf])
                    av = a_ref[rows, ln]
                    acc[hf][0] = acc[hf][0] + up2 * av
                    acc[hf][1] = acc[hf][1] + up1 * av
                    acc[hf][2] = acc[hf][2] + dc * av
                    acc[hf][3] = acc[hf][3] + dc
                    dasc_ref[rows, ln] = w[hf][2] * dc + w[hf][1] * up1 + w[hf][0] * up2
                    n7[hf], n6[hf] = r7, r6
                    if j == 0:
                        carry_ref[:, ln] = dc
            for hf, ln in enumerate(halves):
                for k in range(3):
                    dcw_ref[k:k + 1, ln] += _colsum(acc[hf][k])
                dcb_ref[:, ln] += _colsum(acc[hf][3])
        da = dasc_ref[...].astype(bf16)
        da_ref[...] = da
        dh2 = lax.dot_general(da, wu_ref[...], (((1,), (1,)), ((), ())), preferred_element_type=f32)
        xh, r = _rms(x1_ref[...])
        dx1n, dg_rows = _rms_bwd(dh2, xh, r, g_ref[...])
        dg_ref[...] += _colsum(dg_rows)
        dx1_ref[...] = dx2 + dx1n

    rev = lambda i: nt - 1 - i
    row = lambda n: pl.BlockSpec((tm, n), lambda i: (rev(i), 0))
    return _pc_behind(
        body, exch, nt, name="ffn_bwd", grid=(nt,),
        in_specs=[row(D), row(2 * DFF), row(DFF), row(DFF), _whole((3, 2 * DFF)), _whole(w_down.shape),
                  _whole(w_up.shape), row(D), _whole((1, D))],
        out_specs=[row(2 * DFF), row(D), _acc((3, 2 * DFF)), _acc((1, 2 * DFF)), _acc((1, D))],
        out_shape=[jax.ShapeDtypeStruct((T, 2 * DFF), bf16), jax.ShapeDtypeStruct((T, D), f32),
                   jax.ShapeDtypeStruct((3, 2 * DFF), f32), jax.ShapeDtypeStruct((1, 2 * DFF), f32),
                   jax.ShapeDtypeStruct((1, D), f32)],
        scratch_shapes=[pltpu.VMEM((8, 2 * DFF), f32), pltpu.VMEM((tm, DFF), f32), pltpu.VMEM((tm, 2 * DFF), f32)],
        compiler_params=_params(), args=(dx2, a, g_a, g_b, conv_w, w_down, w_up, x1, g_ffn))


def _mix_bwd(dx1, hg_o, s5_o, merged, gates, w_bhg, w_bs5, w_out, exch=None):
    T = dx1.shape[0]
    tm = 256

    def body(dx1_ref, hg_ref, s5_ref, mg_ref, gt_ref, wh_ref, ws_ref, wo_ref, dgt_ref, dhg_ref, ds5_ref, dwh_ref, dws_ref,
             dwo_ref):
        @pl.when(pl.program_id(0) == 0)
        def _():
            dwh_ref[...] = jnp.zeros_like(dwh_ref)
            dws_ref[...] = jnp.zeros_like(dws_ref)
            dwo_ref[...] = jnp.zeros_like(dwo_ref)

        dwo_ref[...] += _dot_tn(mg_ref[...], dx1_ref[...])
        dm = _dot_nt(dx1_ref[...], wo_ref[...])
        yh = jnp.dot(hg_ref[...], wh_ref[...], preferred_element_type=f32)
        ys = jnp.dot(s5_ref[...], ws_ref[...], preferred_element_type=f32)
        sh = _sig(gt_ref[:, 0:D])
        ss = _sig(gt_ref[:, D:2 * D])
        dgt_ref[:, 0:D] = dm * yh * sh * (1.0 - sh)
        dgt_ref[:, D:2 * D] = dm * ys * ss * (1.0 - ss)
        dyh = (dm * sh).astype(bf16)
        dys = (dm * ss).astype(bf16)
        dwh_ref[...] += _dot_tn(hg_ref[...], dyh)
        dws_ref[...] += _dot_tn(s5_ref[...], dys)
        dhg_ref[...] = lax.dot_general(dyh, wh_ref[...], (((1,), (1,)), ((), ())), preferred_element_type=f32)
        ds5_ref[...] = lax.dot_general(dys, ws_ref[...], (((1,), (1,)), ((), ())), preferred_element_type=f32)

    row = lambda n: pl.BlockSpec((tm, n), lambda i: (i, 0))
    return _pc_behind(
        body, exch, T // tm, name="mix_bwd", grid=(T // tm,),
        in_specs=[row(D), row(HW), row(SW), row(D), row(2 * D), _whole(w_bhg.shape), _whole(w_bs5.shape),
                  _whole(w_out.shape)],
        out_specs=[row(2 * D), row(HW), row(SW), _acc(w_bhg.shape), _acc(w_bs5.shape), _acc(w_out.shape)],
        out_shape=[jax.ShapeDtypeStruct((T, 2 * D), f32), jax.ShapeDtypeStruct((T, HW), f32), jax.ShapeDtypeStruct((T, SW), f32),
                   jax.ShapeDtypeStruct(w_bhg.shape, f32), jax.ShapeDtypeStruct(w_bs5.shape, f32),
                   jax.ShapeDtypeStruct(w_out.shape, f32)],
        compiler_params=_params(), args=(dx1, hg_o, s5_o, merged, gates, w_bhg, w_bs5, w_out))


def _in_bwd(d_hg, d_u, d_gt, x, dx1, w, g):
    T = x.shape[0]
    tm = 256

    def body(dhg_ref, du_ref, dgt_ref, x_ref, dx1_ref, w_ref, g_ref, dx_ref, dg_ref):
        @pl.when(pl.program_id(0) == 0)
        def _():
            dg_ref[...] = jnp.zeros_like(dg_ref)

        dh = (_dot_nt(dhg_ref[...], w_ref[:, 0:4 * HW]) + _dot_nt(du_ref[...], w_ref[:, 4 * HW:4 * HW + SW])
              + _dot_nt(dgt_ref[...], w_ref[:, 4 * HW + SW:]))
        xh, r = _rms(x_ref[...])
        dxn, dg_rows = _rms_bwd(dh, xh, r, g_ref[...])
        dg_ref[...] += _colsum(dg_rows)
        dx_ref[...] = dx1_ref[...] + dxn

    row = lambda n: pl.BlockSpec((tm, n), lambda i: (i, 0))
    return _pc(
        body, name="in_bwd", grid=(T // tm,),
        in_specs=[row(4 * HW), row(SW), row(2 * D), row(D), row(D), _whole(w.shape), _whole((1, D))],
        out_specs=[row(D), _acc((1, D))],
        out_shape=[jax.ShapeDtypeStruct((T, D), f32), jax.ShapeDtypeStruct((1, D), f32)],
        compiler_params=_params(),
    )(d_hg, d_u, d_gt, x, dx1, w, g)


def _wgrad(name, a, b):
    T, K = a.shape
    N = b.shape[1]
    tm = 512
    b_blk = N
    while K * b_blk * 4 > 6 * 1024 * 1024 and b_blk % 256 == 0:
        b_blk //= 2

    def body(a_ref, b_ref, o_ref):
        @pl.when(pl.program_id(1) == 0)
        def _():
            o_ref[...] = jnp.zeros_like(o_ref)

        o_ref[...] += _dot_tn(a_ref[...], b_ref[...])

    return _pc(
        body, name=name, grid=(N // b_blk, T // tm),
        in_specs=[pl.BlockSpec((tm, K), lambda j, i: (i, 0)), pl.BlockSpec((tm, b_blk), lambda j, i: (i, j))],
        out_specs=pl.BlockSpec((K, b_blk), lambda j, i: (0, j)),
        out_shape=jax.ShapeDtypeStruct((K, N), f32),
        compiler_params=_params(2),
    )(a, b)


ANY = pl.BlockSpec(memory_space=pl.ANY)


AG_SEMS = [pltpu.SemaphoreType.DMA((7,)), pltpu.SemaphoreType.DMA((7,)), pltpu.SemaphoreType.DMA]


def _ag_steps(x_ref, out_ref, send_sems, recv_sems, local_sem):
    x, y, c = lax.axis_index("x"), lax.axis_index("y"), lax.axis_index("c")
    me, sibling = (x, y, c), (x, y, 1 - c)
    chips = [(1 - x, y), (x, 1 - y), (1 - x, 1 - y)]

    def slot(px, py, pc):
        return out_ref.at[4 * px + 2 * py + pc]

    def copy(k, block, to, src=None):
        return pltpu.make_async_remote_copy(
            src_ref=slot(*block) if src is None else src, dst_ref=slot(*block),
            send_sem=send_sems.at[k], recv_sem=recv_sems.at[k], device_id=to, device_id_type=MESH)

    def mine():
        return pltpu.make_async_copy(x_ref, slot(*me), local_sem)

    def first():
        return [copy(0, me, sibling, src=x_ref)] + [copy(1 + j, me, (*chip, c), src=x_ref) for j, chip in enumerate(chips)]

    def passed():
        return [copy(4 + j, (*chip, c), sibling) for j, chip in enumerate(chips)]

    def start():
        mine().start()
        for cp in first():
            cp.start()

    def forward():
        for j, (chip, cp) in enumerate(zip(chips, passed())):
            copy(1 + j, (*chip, c), me).wait_recv()
            cp.start()

    def finish():
        copy(0, sibling, me).wait_recv()
        for j, chip in enumerate(chips):
            copy(4 + j, (*chip, 1 - c), me).wait_recv()
        for cp in first() + passed():
            cp.wait_send()
        mine().wait()

    return start, forward, finish


def _all_gather(name, shard):
    R, C = shard.shape

    def body(x_ref, out_ref, send_sems, recv_sems, local_sem):
        for phase in _ag_steps(x_ref, out_ref, send_sems, recv_sems, local_sem):
            phase()

    return _pc(
        body, name=name, in_specs=[ANY], out_specs=ANY,
        out_shape=jax.ShapeDtypeStruct((N_DEV, R, C), shard.dtype), scratch_shapes=list(AG_SEMS),
    )(shard)


class _Exchange:
    def __init__(self, kind, arrays):
        self.kind, self.arrays, self.n = kind, list(arrays), len(arrays)
        self.per = 4 if kind == "sibling" else 3
        tail = (lambda a: a.shape[2:]) if kind == "sibling" else (lambda a: a.shape[1:])
        self.out_shape = [jax.ShapeDtypeStruct((self.per, *tail(a)), a.dtype) for a in self.arrays]
        self.scratch = [pltpu.SemaphoreType.DMA((self.per * self.n,)), pltpu.SemaphoreType.DMA((self.per * self.n,))]

    def steps(self, in_refs, out_refs, send_sems, recv_sems):
        x, y, c = lax.axis_index("x"), lax.axis_index("y"), lax.axis_index("c")
        chips = [(1 - x, y), (x, 1 - y), (1 - x, 1 - y)]

        def copies():
            cps = []
            for i, (src, dst) in enumerate(zip(in_refs, out_refs)):
                for k in range(self.per):
                    if self.kind == "sibling":
                        s, to = src.at[k, 1 - c], (x, y, 1 - c)
                    else:
                        s, to = src.at[2 * chips[k][0] + chips[k][1]], (*chips[k], c)
                    cps.append(pltpu.make_async_remote_copy(
                        src_ref=s, dst_ref=dst.at[k], send_sem=send_sems.at[self.per * i + k],
                        recv_sem=recv_sems.at[self.per * i + k], device_id=to, device_id_type=MESH))
            return cps

        def start():
            for cp in copies():
                cp.start()

        def finish():
            for cp in copies():
                cp.wait()

        return start, finish


def _exchange_call(name, exch):
    n = exch.n

    def body(*refs):
        start, finish = exch.steps(refs[:n], refs[n:2 * n], *refs[2 * n:])
        start()
        finish()

    return _pc(body, name=name, in_specs=[ANY] * n, out_specs=[ANY] * n, out_shape=exch.out_shape,
               scratch_shapes=exch.scratch)(*exch.arrays)


def _pc_behind(body, exch, nsteps, *, in_specs, out_specs, out_shape, args, scratch_shapes=(), **kw):
    if exch is None:
        return _pc(body, in_specs=in_specs, out_specs=out_specs, out_shape=out_shape, scratch_shapes=list(scratch_shapes),
                   **kw)(*args), None
    n_in, n_out, n_scr, ne = len(in_specs), len(out_specs), len(scratch_shapes), exch.n

    def wrapped(*refs):
        ins, e_in = refs[:n_in], refs[n_in:n_in + ne]
        o0 = n_in + ne
        outs, e_out = refs[o0:o0 + n_out], refs[o0 + n_out:o0 + n_out + ne]
        s0 = o0 + n_out + ne
        scr, sems = refs[s0:s0 + n_scr], refs[s0 + n_scr:]
        start, finish = exch.steps(e_in, e_out, *sems)
        i = pl.program_id(0)
        pl.when(i == 0)(start)
        body(*ins, *outs, *scr)
        pl.when(i == nsteps - 1)(finish)

    res = _pc(wrapped, in_specs=list(in_specs) + [ANY] * ne, out_specs=list(out_specs) + [ANY] * ne,
              out_shape=list(out_shape) + exch.out_shape, scratch_shapes=list(scratch_shapes) + exch.scratch,
              **kw)(*args, *exch.arrays)
    return res[:n_out], res[n_out:]


def _add_halves(name, g4, got, ids):
    _, _, K, c = g4.shape

    def body(ids_ref, a_ref, b_ref, p16_ref, own_ref):
        s = a_ref[0, 0] + b_ref[0]
        p16_ref[0] = s.astype(bf16)

        @pl.when(pl.program_id(0) == ids_ref[1])
        def _():
            own_ref[...] = s

    return _pc(
        body, name=name,
        grid_spec=pltpu.PrefetchScalarGridSpec(
            num_scalar_prefetch=1, grid=(4,),
            in_specs=[pl.BlockSpec((1, 1, K, c), lambda k, ids: (k, ids[0], 0, 0)),
                      pl.BlockSpec((1, K, c), lambda k, ids: (k, 0, 0))],
            out_specs=[pl.BlockSpec((1, K, c), lambda k, ids: (k, 0, 0)), pl.BlockSpec((K, c), lambda k, ids: (0, 0))]),
        out_shape=[jax.ShapeDtypeStruct((4, K, c), bf16), jax.ShapeDtypeStruct((K, c), f32)],
        compiler_params=_params(),
    )(ids, g4, got)


def _row_tile(K):
    for cand in (256, 176, 128, 64):
        if K % cand == 0:
            return cand
    return K


def _adam_shard(name, own, got3, w, m, v):
    K, c = own.shape
    tr = _row_tile(K)

    def body(own_ref, got_ref, w_ref, m_ref, v_ref, g_ref, d_ref, m2_ref, v2_ref):
        g = own_ref[...] + got_ref[0].astype(f32) + got_ref[1].astype(f32) + got_ref[2].astype(f32)
        g_ref[0] = g
        delta, m2, v2 = _adam_math(g, w_ref[0], m_ref[0], v_ref[0])
        d_ref[0] = delta
        m2_ref[0] = m2
        v2_ref[0] = v2

    blk = pl.BlockSpec((1, tr, c), lambda i: (0, i, 0))
    out = jax.ShapeDtypeStruct((1, K, c), f32)
    return _pc(
        body, name=name, grid=(K // tr,),
        in_specs=[pl.BlockSpec((tr, c), lambda i: (i, 0)), pl.BlockSpec((3, tr, c), lambda i: (0, i, 0)), blk, blk, blk],
        out_specs=[blk, blk, blk, blk], out_shape=[out, out, out, out], compiler_params=_params(),
    )(own, got3, w, m, v)


def _allreduce_small(grads):
    n = len(grads)
    shapes = [g.shape for g in grads]

    def body(*refs):
        g_refs, outs, recv = refs[0:n], refs[n:2 * n], refs[2 * n:5 * n]
        send_sems, recv_sems = refs[5 * n:]
        x, y, c = lax.axis_index("x"), lax.axis_index("y"), lax.axis_index("c")
        peers = [(x, y, 1 - c), (1 - x, y, c), (x, 1 - y, c)]
        for i in range(n):
            outs[i][...] = g_refs[i][...]
        for s, peer in enumerate(peers):
            cps = [pltpu.make_async_remote_copy(src_ref=outs[i], dst_ref=recv[s * n + i], send_sem=send_sems.at[s * n + i],
                                                recv_sem=recv_sems.at[s * n + i], device_id=peer, device_id_type=MESH)
                   for i in range(n)]
            for cp in cps:
                cp.start()
            for cp in cps:
                cp.wait()
            for i in range(n):
                outs[i][...] = outs[i][...] + recv[s * n + i][...]

    return _pc(
        body, name="allreduce_small", grid=(1,), in_specs=[_whole(s) for s in shapes], out_specs=[_acc(s) for s in shapes],
        out_shape=[jax.ShapeDtypeStruct(s, f32) for s in shapes],
        scratch_shapes=[pltpu.VMEM(s, f32) for s in shapes] * 3
        + [pltpu.SemaphoreType.DMA((3 * n,)), pltpu.SemaphoreType.DMA((3 * n,))],
        compiler_params=_params(),
    )(*grads)


def _adam_small(grads, ws, ms, vs):
    n = len(grads)
    shapes = [g.shape for g in grads]

    def body(*refs):
        g_refs, w_refs, m_refs, v_refs = refs[0:n], refs[n:2 * n], refs[2 * n:3 * n], refs[3 * n:4 * n]
        outs = refs[4 * n:8 * n]
        for i in range(n):
            g = g_refs[i][...]
            delta, m2, v2 = _adam_math(g, w_refs[i][...], m_refs[i][...], v_refs[i][...])
            outs[i][...] = g
            outs[n + i][...] = delta
            outs[2 * n + i][...] = m2
            outs[3 * n + i][...] = v2

    return _pc(
        body, name="adam_small", grid=(1,), in_specs=[_whole(s) for s in shapes] * 4, out_specs=[_acc(s) for s in shapes] * 4,
        out_shape=[jax.ShapeDtypeStruct(s, f32) for s in shapes] * 4, compiler_params=_params(),
    )(*grads, *ws, *ms, *vs)


def _adam_math(g, w, m, v):
    m2 = ADAM_B1 * m + (1.0 - ADAM_B1) * g
    v2 = ADAM_B2 * v + (1.0 - ADAM_B2) * (g * g)
    m_hat = m2 / (1.0 - ADAM_B1 ** ADAM_STEP)
    v_hat = v2 / (1.0 - ADAM_B2 ** ADAM_STEP)
    delta = -ADAM_LR * (m_hat / (jnp.sqrt(v_hat) + ADAM_EPS) + ADAM_WD * w)
    return delta, m2, v2


def _pack(arrs, dtype, row_mult):
    rows = []
    for a in arrs:
        flat = a.reshape(-1).astype(dtype)
        pad = (-flat.shape[0]) % LANES
        if pad:
            flat = jnp.concatenate([flat, jnp.zeros((pad,), dtype)])
        rows.append(flat.reshape(-1, LANES))
    out = jnp.concatenate(rows, axis=0)
    pad = (-out.shape[0]) % row_mult
    if pad:
        out = jnp.concatenate([out, jnp.zeros((pad, LANES), dtype)], axis=0)
    return out


def _unpack(buf, shapes):
    lead = buf.shape[:-2]
    outs, r = [], 0
    for shp in shapes:
        n = math.prod(shp)
        nr = -(-n // LANES)
        piece = buf[..., r:r + nr, :].reshape(*lead, nr * LANES)[..., :n]
        outs.append(piece.reshape(*lead, *shp))
        r += nr
    return outs


def _to_slabs(full, axis):
    shp = full.shape
    n = shp[axis] // N_DEV
    return jnp.moveaxis(full.reshape(*shp[:axis], N_DEV, n, *shp[axis + 1:]), axis, 0)


def _from_slabs(slabs, axis):
    t = jnp.moveaxis(slabs, 0, axis)
    shp = t.shape
    return t.reshape(*shp[:axis], shp[axis] * shp[axis + 1], *shp[axis + 2:])


def _s5_discretise(lam_re, lam_im, log_dt, b_re, b_im):
    dt = jnp.exp(log_dt)[:, None]
    mag = jnp.exp(lam_re * dt)
    a_re = mag * jnp.cos(lam_im * dt)
    a_im = mag * jnp.sin(lam_im * dt)
    den = lam_re * lam_re + lam_im * lam_im
    coef_re = ((a_re - 1.0) * lam_re + a_im * lam_im) / den
    coef_im = (a_im * lam_re - (a_re - 1.0) * lam_im) / den
    bbar_re = coef_re[..., None] * b_re - coef_im[..., None] * b_im
    bbar_im = coef_re[..., None] * b_im + coef_im[..., None] * b_re
    return a_re, a_im, bbar_re, bbar_im


def _s5_operands(bbar_re, bbar_im, c_re, c_im):
    eye = jnp.eye(SG // NST, dtype=f32)

    def b_op(bb):
        return jnp.einsum("sgnq,gh->sgqhn", bb.reshape(NST, SG // NST, SN, SP), eye).reshape(NST, 128, STW)

    def c_op(cc):
        return jnp.einsum("sgpn,gh->shngp", cc.reshape(NST, SG // NST, SP, SN), eye).reshape(NST, STW, 128)

    bdb = jnp.concatenate([b_op(bbar_re), b_op(bbar_im)], axis=0)
    bdc = jnp.concatenate([c_op(c_re), c_op(-c_im)], axis=0)
    return bdb, bdc


_BIG = ["w_in", "s5_glu_w", "w_branch_hg", "w_branch_s5", "w_out", "w_up", "w_down", "w_ple_gate", "w_ple_proj", "conv_w"]
_BIG_AXIS = {"w_in": 1, "s5_glu_w": 0, "w_branch_hg": 1, "w_branch_s5": 1, "w_out": 0, "w_up": 1, "w_down": 0,
             "w_ple_gate": 0, "w_ple_proj": 1, "conv_w": 1}
_SMALL = ["norm_mix_g", "hg_lb_logits", "hg_norm_g", "s5_lambda_re", "s5_lambda_im", "s5_log_dt", "s5_b_re", "s5_b_im",
          "s5_c_re", "s5_c_im", "s5_d", "s5_glu_b", "norm_ffn_g", "conv_b", "norm_ple_g", "norm_final_g"]
_ORDER = ["norm_mix_g", "w_in", "hg_lb_logits", "hg_norm_g", "s5_lambda_re", "s5_lambda_im", "s5_log_dt", "s5_b_re",
          "s5_b_im", "s5_c_re", "s5_c_im", "s5_d", "s5_glu_w", "s5_glu_b", "w_branch_hg", "w_branch_s5", "w_out",
          "norm_ffn_g", "w_up", "conv_w", "conv_b", "w_down", "norm_ple_g", "w_ple_gate", "w_ple_proj", "norm_final_g"]


def kernel(x, p, norm_mix_g, w_in, hg_lb_logits, hg_norm_g, s5_lambda_re, s5_lambda_im, s5_log_dt, s5_b_re, s5_b_im, s5_c_re, s5_c_im, s5_d, s5_glu_w, s5_glu_b, w_branch_hg, w_branch_s5, w_out, norm_ffn_g, w_up, conv_w, conv_b, w_down, norm_ple_g, w_ple_gate, w_ple_proj, norm_final_g, loss_target, m_norm_mix_g, m_w_in, m_hg_lb_logits, m_hg_norm_g, m_s5_lambda_re, m_s5_lambda_im, m_s5_log_dt, m_s5_b_re, m_s5_b_im, m_s5_c_re, m_s5_c_im, m_s5_d, m_s5_glu_w, m_s5_glu_b, m_w_branch_hg, m_w_branch_s5, m_w_out, m_norm_ffn_g, m_w_up, m_conv_w, m_conv_b, m_w_down, m_norm_ple_g, m_w_ple_gate, m_w_ple_proj, m_norm_final_g, v_norm_mix_g, v_w_in, v_hg_lb_logits, v_hg_norm_g, v_s5_lambda_re, v_s5_lambda_im, v_s5_log_dt, v_s5_b_re, v_s5_b_im, v_s5_c_re, v_s5_c_im, v_s5_d, v_s5_glu_w, v_s5_glu_b, v_w_branch_hg, v_w_branch_s5, v_w_out, v_norm_ffn_g, v_w_up, v_conv_w, v_conv_b, v_w_down, v_norm_ple_g, v_w_ple_gate, v_w_ple_proj, v_norm_final_g):
    W = dict(norm_mix_g=norm_mix_g, w_in=w_in, hg_lb_logits=hg_lb_logits, hg_norm_g=hg_norm_g, s5_lambda_re=s5_lambda_re, s5_lambda_im=s5_lambda_im, s5_log_dt=s5_log_dt, s5_b_re=s5_b_re, s5_b_im=s5_b_im, s5_c_re=s5_c_re, s5_c_im=s5_c_im, s5_d=s5_d, s5_glu_w=s5_glu_w, s5_glu_b=s5_glu_b, w_branch_hg=w_branch_hg, w_branch_s5=w_branch_s5, w_out=w_out, norm_ffn_g=norm_ffn_g, w_up=w_up, conv_w=conv_w, conv_b=conv_b, w_down=w_down, norm_ple_g=norm_ple_g, w_ple_gate=w_ple_gate, w_ple_proj=w_ple_proj, norm_final_g=norm_final_g)
    M = dict(norm_mix_g=m_norm_mix_g, w_in=m_w_in, hg_lb_logits=m_hg_lb_logits, hg_norm_g=m_hg_norm_g, s5_lambda_re=m_s5_lambda_re, s5_lambda_im=m_s5_lambda_im, s5_log_dt=m_s5_log_dt, s5_b_re=m_s5_b_re, s5_b_im=m_s5_b_im, s5_c_re=m_s5_c_re, s5_c_im=m_s5_c_im, s5_d=m_s5_d, s5_glu_w=m_s5_glu_w, s5_glu_b=m_s5_glu_b, w_branch_hg=m_w_branch_hg, w_branch_s5=m_w_branch_s5, w_out=m_w_out, norm_ffn_g=m_norm_ffn_g, w_up=m_w_up, conv_w=m_conv_w, conv_b=m_conv_b, w_down=m_w_down, norm_ple_g=m_norm_ple_g, w_ple_gate=m_w_ple_gate, w_ple_proj=m_w_ple_proj, norm_final_g=m_norm_final_g)
    V = dict(norm_mix_g=v_norm_mix_g, w_in=v_w_in, hg_lb_logits=v_hg_lb_logits, hg_norm_g=v_hg_norm_g, s5_lambda_re=v_s5_lambda_re, s5_lambda_im=v_s5_lambda_im, s5_log_dt=v_s5_log_dt, s5_b_re=v_s5_b_re, s5_b_im=v_s5_b_im, s5_c_re=v_s5_c_re, s5_c_im=v_s5_c_im, s5_d=v_s5_d, s5_glu_w=v_s5_glu_w, s5_glu_b=v_s5_glu_b, w_branch_hg=v_w_branch_hg, w_branch_s5=v_w_branch_s5, w_out=v_w_out, norm_ffn_g=v_norm_ffn_g, w_up=v_w_up, conv_w=v_conv_w, conv_b=v_conv_b, w_down=v_w_down, norm_ple_g=v_norm_ple_g, w_ple_gate=v_w_ple_gate, w_ple_proj=v_w_ple_proj, norm_final_g=v_norm_final_g)

    shard2 = {n: W[n][0] for n in _BIG}
    conv_bits = lax.bitcast_convert_type(shard2["conv_w"], bf16)
    groups = [["w_in", "s5_glu_w"], ["w_branch_hg", "w_branch_s5", "w_out", "w_ple_gate", "w_ple_proj"], ["w_up", "w_down"]]
    packs = [_pack([shard2[n] for n in grp] + ([conv_bits] if k == 0 else []), bf16, 16) for k, grp in enumerate(groups)]
    full = {}

    def take(k, gathered):
        pieces = _unpack(gathered, [shard2[n].shape for n in groups[k]] + ([conv_bits.shape] if k == 0 else []))
        full.update({n: _from_slabs(pc, _BIG_AXIS[n]) for n, pc in zip(groups[k], pieces)})
        return pieces

    conv_w_full = _from_slabs(lax.bitcast_convert_type(take(0, _all_gather("ag_weights", packs[0]))[-1], f32), 1)

    xt = x[0]
    pt = p[0, 0]
    tgt = loss_target[0]
    T = xt.shape[0]
    lam_re, lam_im, log_dt = s5_lambda_re[0], s5_lambda_im[0], s5_log_dt[0]
    b_re, b_im, c_re, c_im = s5_b_re[0], s5_b_im[0], s5_c_re[0], s5_c_im[0]

    def s5_prep(lam_re, lam_im, log_dt, b_re, b_im, c_re, c_im):
        a_re, a_im, bbar_re, bbar_im = _s5_discretise(lam_re, lam_im, log_dt, b_re, b_im)
        bdb, bdc = _s5_operands(bbar_re, bbar_im, c_re, c_im)
        return a_re, a_im, bdb, bdc

    (a_re, a_im, bdb, bdc), s5_prep_vjp = jax.vjp(s5_prep, lam_re, lam_im, log_dt, b_re, b_im, c_re, c_im)
    a_row = jnp.concatenate([a_re.reshape(1, SL), a_im.reshape(1, SL)], axis=1)
    bdb_b, bdc_b = bdb.astype(bf16), bdc.astype(bf16)

    h1, proj_hg, u_raw, gates, gathered1 = _in_proj(xt, norm_mix_g, full["w_in"], packs[1])
    take(1, gathered1)
    ng4 = jnp.tile(hg_norm_g, (1, NH))
    hg_o, sprev = _hgrn_fwd(proj_hg, hg_lb_logits, ng4)
    x_st, y_s5, s5_o, gathered2 = _s5_fwd(u_raw, a_row, bdb_b, bdc_b, s5_d, full["s5_glu_w"], s5_glu_b, packs[2])
    take(2, gathered2)
    x1, merged, h2, a_up = _mix_up(xt, hg_o, s5_o, gates, full["w_branch_hg"], full["w_branch_s5"], full["w_out"],
                                   norm_ffn_g, full["w_up"])
    (dx2, gated, g_a, g_b, d_w_pg, d_w_pp, loss_part, d_norm_final, d_norm_ple) = _ffn_tail(
        a_up, conv_w_full, conv_b, full["w_down"], x1, pt, norm_ple_g, full["w_ple_gate"], full["w_ple_proj"],
        norm_final_g.reshape(1, D), tgt)

    ids = jnp.stack([lax.axis_index("c"), 2 * lax.axis_index("x") + lax.axis_index("y")]).astype(jnp.int32)
    gw, own_sum, got3 = {}, {}, {}

    def slabs(names):
        return [_to_slabs(gw[n], _BIG_AXIS[n]).reshape(4, 2, *shard2[n].shape) for n in names]

    def add_pairs(names, g4, got):
        sums = [_add_halves("rs_add_" + n, g, r, ids) for n, g, r in zip(names, g4, got)]
        own_sum.update({n: own for n, (_, own) in zip(names, sums)})
        return [p16 for p16, _ in sums]

    grp_a = ["w_down", "w_ple_gate", "w_ple_proj"]
    gw["w_down"] = _wgrad("wg_down", gated, dx2)
    gw["w_ple_gate"] = d_w_pg
    gw["w_ple_proj"] = d_w_pp
    g4_a = slabs(grp_a)
    (da_up, dx1, d_conv_w, d_conv_b, d_norm_ffn), got_a = _ffn_bwd(
        dx2, a_up, g_a, g_b, conv_w_full, full["w_down"], full["w_up"], x1, norm_ffn_g, exch=_Exchange("sibling", g4_a))
    p16_a = add_pairs(grp_a, g4_a, got_a)
    (d_gates, d_hg_o, d_s5_o, d_w_bhg, d_w_bs5, d_w_out), got3_a = _mix_bwd(
        dx1, hg_o, s5_o, merged, gates, full["w_branch_hg"], full["w_branch_s5"], full["w_out"], exch=_Exchange("chips", p16_a))
    got3.update(zip(grp_a, got3_a))

    grp_b = ["w_up", "w_out", "w_branch_hg", "w_branch_s5", "conv_w"]
    gw["w_up"] = _wgrad("wg_up", h2, da_up)
    gw["w_out"] = d_w_out
    gw["w_branch_hg"] = d_w_bhg
    gw["w_branch_s5"] = d_w_bs5
    gw["conv_w"] = d_conv_w
    g4_b = slabs(grp_b)
    (d_proj_hg, d_lb, d_hg_norm), got_b = _hgrn_bwd(proj_hg, hg_lb_logits, ng4, sprev, d_hg_o,
                                                     exch=_Exchange("sibling", g4_b))
    p16_b = add_pairs(grp_b, g4_b, got_b)
    (d_u, d_w_glu, d_a_re, d_a_im, d_s5_d, d_glu_b, d_bdb, d_bdc), got3_b = _s5_bwd(
        d_s5_o, y_s5, u_raw, x_st, a_row, bdb_b, bdc_b, s5_d, full["s5_glu_w"], s5_glu_b, exch=_Exchange("chips", p16_b))
    got3.update(zip(grp_b, got3_b))
    grad_x, d_norm_mix = _in_bwd(d_proj_hg, d_u, d_gates, xt, dx1, full["w_in"], norm_mix_g)

    grp_c = ["w_in", "s5_glu_w"]
    gw["w_in"] = jnp.concatenate([_wgrad("wg_in_hg", h1, d_proj_hg), _wgrad("wg_in_u", h1, d_u),
                                  _wgrad("wg_in_gates", h1, d_gates)], axis=1)
    gw["s5_glu_w"] = d_w_glu
    g4_c = slabs(grp_c)
    p16_c = add_pairs(grp_c, g4_c, _exchange_call("rs_sibling", _Exchange("sibling", g4_c)))
    got3.update(zip(grp_c, _exchange_call("rs_chips", _Exchange("chips", p16_c))))

    (d_lam_re, d_lam_im, d_log_dt, d_b_re, d_b_im, d_c_re, d_c_im) = s5_prep_vjp(
        (d_a_re.reshape(SG, SN), d_a_im.reshape(SG, SN), d_bdb, d_bdc))
    sm = jax.nn.softmax(hg_lb_logits, axis=0)
    d_l0 = d_lb[0] * sm[0] * sm[1]
    d_logits = jnp.stack([d_l0, -d_l0], axis=0)

    gs = {"norm_mix_g": d_norm_mix, "hg_lb_logits": d_logits, "hg_norm_g": d_hg_norm, "s5_lambda_re": d_lam_re,
          "s5_lambda_im": d_lam_im, "s5_log_dt": d_log_dt, "s5_b_re": d_b_re, "s5_b_im": d_b_im, "s5_c_re": d_c_re,
          "s5_c_im": d_c_im, "s5_d": d_s5_d, "s5_glu_b": d_glu_b, "norm_ffn_g": d_norm_ffn, "conv_b": d_conv_b,
          "norm_ple_g": d_norm_ple, "norm_final_g": d_norm_final}

    big_out = [_adam_shard("adam_" + n, own_sum[n], got3[n], W[n], M[n], V[n]) for n in _BIG]

    two_d = lambda a: a.reshape(1, -1) if a.ndim == 1 else a
    dense = lambda a: a.reshape(SG, -1) if a.ndim == 4 else two_d(a)
    g_sum = _allreduce_small([dense(gs[n].reshape(W[n].shape)) for n in _SMALL])
    small_out = _adam_small([g.reshape(two_d(W[n]).shape) for g, n in zip(g_sum, _SMALL)], [two_d(W[n]) for n in _SMALL],
                            [two_d(M[n]) for n in _SMALL], [two_d(V[n]) for n in _SMALL])

    res = {}
    for k in range(4):
        d = {n: big_out[i][k] for i, n in enumerate(_BIG)}
        d.update({n: small_out[k * len(_SMALL) + i].reshape(W[n].shape) for i, n in enumerate(_SMALL)})
        res[k] = d
    loss = lax.psum(loss_part[0, 0], ("x", "y", "c"))
    return (loss, grad_x[None], *[res[0][n] for n in _ORDER], *[res[1][n] for n in _ORDER],
            *[res[2][n] for n in _ORDER], *[res[3][n] for n in _ORDER])
```
